```python
import jax, jax.numpy as jnp
from jax import lax
import numpy as np

D_MODEL = 1024
BATCH = 8
SEQ = 4096
DEPTH = 2

D_MIX = D_MODEL
LRU_WIDTH = D_MIX // 2
GMLP_WIDTH = D_MIX - LRU_WIDTH
LRU_HEADS = 8
LRU_HEAD_DIM = LRU_WIDTH // LRU_HEADS
GMLP_HEADS = 8
GMLP_HEAD_DIM = GMLP_WIDTH // GMLP_HEADS
CONV_WIDTH = 4
RG_LRU_C = 8.0
CHUNK = 128
D_FF = ((8 * D_MODEL // 3 + 127) // 128) * 128
N_MOD = 9
IN_COLS = 2 * LRU_WIDTH + 2 * GMLP_WIDTH
EPS = 1e-6

kernel_name = "hybrid_rglru_gmlp_macaron_adaln"


def rms_norm(x, g):
    x32 = x.astype(jnp.float32)
    y = x32 * lax.rsqrt(jnp.mean(x32 * x32, axis=-1, keepdims=True) + EPS)
    return (y * g.astype(jnp.float32)).astype(x.dtype)


def layer_norm(x, g):
    x32 = x.astype(jnp.float32)
    mu = jnp.mean(x32, axis=-1, keepdims=True)
    var = jnp.mean(jnp.square(x32 - mu), axis=-1, keepdims=True)
    return ((x32 - mu) * lax.rsqrt(var + EPS) * g.astype(jnp.float32)).astype(x.dtype)


def modulate(h, shift, scale):
    return h * (1.0 + scale) + shift


def swiglu(h, w_gu, w_down):
    g, u = jnp.split(h @ w_gu, 2, axis=-1)
    return (jax.nn.silu(g) * u) @ w_down


def causal_depthwise_conv(x, w, b):
    S = x.shape[1]
    xp = jnp.pad(x, ((0, 0), (CONV_WIDTH - 1, 0), (0, 0)))
    y = b
    for k in range(CONV_WIDTH):
        y = y + xp[:, k:k + S] * w[k]
    return y


def _lin_rec_combine(left, right):
    a1, b1 = left
    a2, b2 = right
    return a1 * a2, a2 * b1 + b2


def rg_lru(xb, wa, ba, wx, bx, lam):
    B, S, _ = xb.shape
    xh = xb.reshape(B, S, LRU_HEADS, LRU_HEAD_DIM)
    r = jax.nn.sigmoid(jnp.einsum('bshd,hde->bshe', xh, wa) + ba).reshape(B, S, LRU_WIDTH)
    i = jax.nn.sigmoid(jnp.einsum('bshd,hde->bshe', xh, wx) + bx).reshape(B, S, LRU_WIDTH)
    log_a = RG_LRU_C * r.astype(jnp.float32) * jax.nn.log_sigmoid(lam.astype(jnp.float32))
    a = jnp.exp(log_a)
    mult = jnp.sqrt(-jnp.expm1(2.0 * log_a))
    inp = mult * (i * xb).astype(jnp.float32)
    _, h = lax.associative_scan(_lin_rec_combine, (a, inp), axis=1)
    return h.astype(xb.dtype)


def chunked_gmlp(u, v, v_norm, spatial_w, spatial_b):
    B, S, _ = u.shape
    nc = S // CHUNK
    u = jax.nn.gelu(u)
    v = jax.nn.gelu(v)
    vh = v.reshape(B, nc, CHUNK, GMLP_HEADS, GMLP_HEAD_DIM)
    vh = layer_norm(vh, v_norm.reshape(GMLP_HEADS, GMLP_HEAD_DIM))
    mask = jnp.tril(jnp.ones((CHUNK, CHUNK), dtype=spatial_w.dtype))
    ws = spatial_w * mask
    z = jnp.einsum('hts,bnshd->bnthd', ws, vh) + spatial_b.T[:, :, None]
    return u * z.reshape(B, S, GMLP_WIDTH)


def _fwd_setup_inputs(seed: int = 0) -> dict:
    key = jax.random.key(seed)
    ks = jax.random.split(key, 32)
    f32 = jnp.float32
    L, D = DEPTH, D_MODEL

    def nrm(k, shape, scale):
        return jax.random.normal(k, shape, f32) * scale

    def gain(k, shape):
        return 1.0 + 0.05 * jax.random.normal(k, shape, f32)

    a0 = jax.random.uniform(ks[12], (L, LRU_WIDTH), f32, minval=0.9, maxval=0.999)
    lru_lambda = jnp.log(a0) - jnp.log1p(-a0)
    return {
        'x': jax.random.normal(ks[0], (BATCH, SEQ, D), f32),
        'c': jax.random.normal(ks[1], (BATCH, D), f32),
        'w_ada': nrm(ks[2], (L, D, N_MOD * D), 0.5 * D ** -0.5),
        'b_ada': nrm(ks[3], (L, N_MOD * D), 0.02),
        'ffn1_norm': gain(ks[4], (L, D)),
        'ffn1_w_gu': nrm(ks[5], (L, D, 2 * D_FF), D ** -0.5),
        'ffn1_w_down': nrm(ks[6], (L, D_FF, D), D_FF ** -0.5),
        'mix_norm': gain(ks[7], (L, D)),
        'w_in': nrm(ks[8], (L, D, IN_COLS), D ** -0.5),
        'conv_w': nrm(ks[9], (L, CONV_WIDTH, LRU_WIDTH), CONV_WIDTH ** -0.5),
        'conv_b': nrm(ks[10], (L, LRU_WIDTH), 0.02),
        'gate_a_w': nrm(ks[11], (L, LRU_HEADS, LRU_HEAD_DIM, LRU_HEAD_DIM), LRU_HEAD_DIM ** -0.5),
        'gate_a_b': nrm(ks[13], (L, LRU_HEADS, LRU_HEAD_DIM), 0.02),
        'gate_x_w': nrm(ks[14], (L, LRU_HEADS, LRU_HEAD_DIM, LRU_HEAD_DIM), LRU_HEAD_DIM ** -0.5),
        'gate_x_b': nrm(ks[15], (L, LRU_HEADS, LRU_HEAD_DIM), 0.02),
        'lru_lambda': lru_lambda,
        'v_norm': gain(ks[16], (L, GMLP_WIDTH)),
        'spatial_w': nrm(ks[17], (L, GMLP_HEADS, CHUNK, CHUNK), CHUNK ** -0.5),
        'spatial_b': nrm(ks[18], (L, GMLP_HEADS, CHUNK), 0.02),
        'lru_out_norm': gain(ks[19], (L, LRU_WIDTH)),
        'gmlp_out_norm': gain(ks[20], (L, GMLP_WIDTH)),
        'w_out': nrm(ks[21], (L, D_MIX, D), D_MIX ** -0.5),
        'ffn2_norm': gain(ks[22], (L, D)),
        'ffn2_w_gu': nrm(ks[23], (L, D, 2 * D_FF), D ** -0.5),
        'ffn2_w_down': nrm(ks[24], (L, D_FF, D), D_FF ** -0.5),
        'final_norm': gain(ks[25], (D,)),
    }


def _fwd_reference(x, c, w_ada, b_ada, ffn1_norm, ffn1_w_gu, ffn1_w_down, mix_norm, w_in,
              conv_w, conv_b, gate_a_w, gate_a_b, gate_x_w, gate_x_b, lru_lambda,
              v_norm, spatial_w, spatial_b, lru_out_norm, gmlp_out_norm, w_out,
              ffn2_norm, ffn2_w_gu, ffn2_w_down, final_norm):
    B = x.shape[0]
    sc = jax.nn.silu(c)
    for l in range(DEPTH):
        mod = (sc @ w_ada[l] + b_ada[l]).reshape(B, N_MOD, 1, D_MODEL)

        h = modulate(rms_norm(x, ffn1_norm[l]), mod[:, 0], mod[:, 1])
        x = x + 0.5 * mod[:, 2] * swiglu(h, ffn1_w_gu[l], ffn1_w_down[l])

        h = modulate(rms_norm(x, mix_norm[l]), mod[:, 3], mod[:, 4])
        proj = h @ w_in[l]
        x_lru, g_lru, u, v = jnp.split(
            proj, [LRU_WIDTH, 2 * LRU_WIDTH, 2 * LRU_WIDTH + GMLP_WIDTH], axis=-1)
        x_lru = causal_depthwise_conv(x_lru, conv_w[l], conv_b[l])
        y_lru = rg_lru(x_lru, gate_a_w[l], gate_a_b[l], gate_x_w[l], gate_x_b[l],
                       lru_lambda[l]) * jax.nn.gelu(g_lru)
        y_gmlp = chunked_gmlp(u, v, v_norm[l], spatial_w[l], spatial_b[l])
        y = jnp.concatenate([rms_norm(y_lru, lru_out_norm[l]),
                             rms_norm(y_gmlp, gmlp_out_norm[l])], axis=-1)
        x = x + mod[:, 5] * (y @ w_out[l])

        h = modulate(rms_norm(x, ffn2_norm[l]), mod[:, 6], mod[:, 7])
        x = x + 0.5 * mod[:, 8] * swiglu(h, ffn2_w_gu[l], ffn2_w_down[l])
    return rms_norm(x, final_norm)


import jax as _jax
import jax.numpy as _jnp

TWIN_FORMAT = 'train_step'
FWD_PARAMS = ['x', 'c', 'w_ada', 'b_ada', 'ffn1_norm', 'ffn1_w_gu', 'ffn1_w_down', 'mix_norm', 'w_in', 'conv_w', 'conv_b', 'gate_a_w', 'gate_a_b', 'gate_x_w', 'gate_x_b', 'lru_lambda', 'v_norm', 'spatial_w', 'spatial_b', 'lru_out_norm', 'gmlp_out_norm', 'w_out', 'ffn2_norm', 'ffn2_w_gu', 'ffn2_w_down', 'final_norm']
TWIN_WEIGHTS = ['w_ada', 'b_ada', 'ffn1_norm', 'ffn1_w_gu', 'ffn1_w_down', 'mix_norm', 'w_in', 'conv_w', 'conv_b', 'gate_a_w', 'gate_a_b', 'gate_x_w', 'gate_x_b', 'lru_lambda', 'v_norm', 'spatial_w', 'spatial_b', 'lru_out_norm', 'gmlp_out_norm', 'w_out', 'ffn2_norm', 'ffn2_w_gu', 'ffn2_w_down', 'final_norm']
TWIN_DIFF_INPUT = 'x'
TWIN_INPUTS = ['x', 'c', 'w_ada', 'b_ada', 'ffn1_norm', 'ffn1_w_gu', 'ffn1_w_down', 'mix_norm', 'w_in', 'conv_w', 'conv_b', 'gate_a_w', 'gate_a_b', 'gate_x_w', 'gate_x_b', 'lru_lambda', 'v_norm', 'spatial_w', 'spatial_b', 'lru_out_norm', 'gmlp_out_norm', 'w_out', 'ffn2_norm', 'ffn2_w_gu', 'ffn2_w_down', 'final_norm', 'loss_target', 'm_w_ada', 'm_b_ada', 'm_ffn1_norm', 'm_ffn1_w_gu', 'm_ffn1_w_down', 'm_mix_norm', 'm_w_in', 'm_conv_w', 'm_conv_b', 'm_gate_a_w', 'm_gate_a_b', 'm_gate_x_w', 'm_gate_x_b', 'm_lru_lambda', 'm_v_norm', 'm_spatial_w', 'm_spatial_b', 'm_lru_out_norm', 'm_gmlp_out_norm', 'm_w_out', 'm_ffn2_norm', 'm_ffn2_w_gu', 'm_ffn2_w_down', 'm_final_norm', 'v_w_ada', 'v_b_ada', 'v_ffn1_norm', 'v_ffn1_w_gu', 'v_ffn1_w_down', 'v_mix_norm', 'v_w_in', 'v_conv_w', 'v_conv_b', 'v_gate_a_w', 'v_gate_a_b', 'v_gate_x_w', 'v_gate_x_b', 'v_lru_lambda', 'v_v_norm', 'v_spatial_w', 'v_spatial_b', 'v_lru_out_norm', 'v_gmlp_out_norm', 'v_w_out', 'v_ffn2_norm', 'v_ffn2_w_gu', 'v_ffn2_w_down', 'v_final_norm']
TWIN_OUTPUTS = ['loss', 'grad_x', 'grad_w_ada', 'grad_b_ada', 'grad_ffn1_norm', 'grad_ffn1_w_gu', 'grad_ffn1_w_down', 'grad_mix_norm', 'grad_w_in', 'grad_conv_w', 'grad_conv_b', 'grad_gate_a_w', 'grad_gate_a_b', 'grad_gate_x_w', 'grad_gate_x_b', 'grad_lru_lambda', 'grad_v_norm', 'grad_spatial_w', 'grad_spatial_b', 'grad_lru_out_norm', 'grad_gmlp_out_norm', 'grad_w_out', 'grad_ffn2_norm', 'grad_ffn2_w_gu', 'grad_ffn2_w_down', 'grad_final_norm', 'delta_w_ada', 'delta_b_ada', 'delta_ffn1_norm', 'delta_ffn1_w_gu', 'delta_ffn1_w_down', 'delta_mix_norm', 'delta_w_in', 'delta_conv_w', 'delta_conv_b', 'delta_gate_a_w', 'delta_gate_a_b', 'delta_gate_x_w', 'delta_gate_x_b', 'delta_lru_lambda', 'delta_v_norm', 'delta_spatial_w', 'delta_spatial_b', 'delta_lru_out_norm', 'delta_gmlp_out_norm', 'delta_w_out', 'delta_ffn2_norm', 'delta_ffn2_w_gu', 'delta_ffn2_w_down', 'delta_final_norm', 'new_m_w_ada', 'new_m_b_ada', 'new_m_ffn1_norm', 'new_m_ffn1_w_gu', 'new_m_ffn1_w_down', 'new_m_mix_norm', 'new_m_w_in', 'new_m_conv_w', 'new_m_conv_b', 'new_m_gate_a_w', 'new_m_gate_a_b', 'new_m_gate_x_w', 'new_m_gate_x_b', 'new_m_lru_lambda', 'new_m_v_norm', 'new_m_spatial_w', 'new_m_spatial_b', 'new_m_lru_out_norm', 'new_m_gmlp_out_norm', 'new_m_w_out', 'new_m_ffn2_norm', 'new_m_ffn2_w_gu', 'new_m_ffn2_w_down', 'new_m_final_norm', 'new_v_w_ada', 'new_v_b_ada', 'new_v_ffn1_norm', 'new_v_ffn1_w_gu', 'new_v_ffn1_w_down', 'new_v_mix_norm', 'new_v_w_in', 'new_v_conv_w', 'new_v_conv_b', 'new_v_gate_a_w', 'new_v_gate_a_b', 'new_v_gate_x_w', 'new_v_gate_x_b', 'new_v_lru_lambda', 'new_v_v_norm', 'new_v_spatial_w', 'new_v_spatial_b', 'new_v_lru_out_norm', 'new_v_gmlp_out_norm', 'new_v_w_out', 'new_v_ffn2_norm', 'new_v_ffn2_w_gu', 'new_v_ffn2_w_down', 'new_v_final_norm']
TWIN_LEAF_KINDS = {'loss': 'loss', 'grad_x': 'grad_x', 'grad_w_ada': 'grad_w', 'grad_b_ada': 'grad_w', 'grad_ffn1_norm': 'grad_w', 'grad_ffn1_w_gu': 'grad_w', 'grad_ffn1_w_down': 'grad_w', 'grad_mix_norm': 'grad_w', 'grad_w_in': 'grad_w', 'grad_conv_w': 'grad_w', 'grad_conv_b': 'grad_w', 'grad_gate_a_w': 'grad_w', 'grad_gate_a_b': 'grad_w', 'grad_gate_x_w': 'grad_w', 'grad_gate_x_b': 'grad_w', 'grad_lru_lambda': 'grad_w', 'grad_v_norm': 'grad_w', 'grad_spatial_w': 'grad_w', 'grad_spatial_b': 'grad_w', 'grad_lru_out_norm': 'grad_w', 'grad_gmlp_out_norm': 'grad_w', 'grad_w_out': 'grad_w', 'grad_ffn2_norm': 'grad_w', 'grad_ffn2_w_gu': 'grad_w', 'grad_ffn2_w_down': 'grad_w', 'grad_final_norm': 'grad_w', 'delta_w_ada': 'delta_w', 'delta_b_ada': 'delta_w', 'delta_ffn1_norm': 'delta_w', 'delta_ffn1_w_gu': 'delta_w', 'delta_ffn1_w_down': 'delta_w', 'delta_mix_norm': 'delta_w', 'delta_w_in': 'delta_w', 'delta_conv_w': 'delta_w', 'delta_conv_b': 'delta_w', 'delta_gate_a_w': 'delta_w', 'delta_gate_a_b': 'delta_w', 'delta_gate_x_w': 'delta_w', 'delta_gate_x_b': 'delta_w', 'delta_lru_lambda': 'delta_w', 'delta_v_norm': 'delta_w', 'delta_spatial_w': 'delta_w', 'delta_spatial_b': 'delta_w', 'delta_lru_out_norm': 'delta_w', 'delta_gmlp_out_norm': 'delta_w', 'delta_w_out': 'delta_w', 'delta_ffn2_norm': 'delta_w', 'delta_ffn2_w_gu': 'delta_w', 'delta_ffn2_w_down': 'delta_w', 'delta_final_norm': 'delta_w', 'new_m_w_ada': 'new_m', 'new_m_b_ada': 'new_m', 'new_m_ffn1_norm': 'new_m', 'new_m_ffn1_w_gu': 'new_m', 'new_m_ffn1_w_down': 'new_m', 'new_m_mix_norm': 'new_m', 'new_m_w_in': 'new_m', 'new_m_conv_w': 'new_m', 'new_m_conv_b': 'new_m', 'new_m_gate_a_w': 'new_m', 'new_m_gate_a_b': 'new_m', 'new_m_gate_x_w': 'new_m', 'new_m_gate_x_b': 'new_m', 'new_m_lru_lambda': 'new_m', 'new_m_v_norm': 'new_m', 'new_m_spatial_w': 'new_m', 'new_m_spatial_b': 'new_m', 'new_m_lru_out_norm': 'new_m', 'new_m_gmlp_out_norm': 'new_m', 'new_m_w_out': 'new_m', 'new_m_ffn2_norm': 'new_m', 'new_m_ffn2_w_gu': 'new_m', 'new_m_ffn2_w_down': 'new_m', 'new_m_final_norm': 'new_m', 'new_v_w_ada': 'new_v', 'new_v_b_ada': 'new_v', 'new_v_ffn1_norm': 'new_v', 'new_v_ffn1_w_gu': 'new_v', 'new_v_ffn1_w_down': 'new_v', 'new_v_mix_norm': 'new_v', 'new_v_w_in': 'new_v', 'new_v_conv_w': 'new_v', 'new_v_conv_b': 'new_v', 'new_v_gate_a_w': 'new_v', 'new_v_gate_a_b': 'new_v', 'new_v_gate_x_w': 'new_v', 'new_v_gate_x_b': 'new_v', 'new_v_lru_lambda': 'new_v', 'new_v_v_norm': 'new_v', 'new_v_spatial_w': 'new_v', 'new_v_spatial_b': 'new_v', 'new_v_lru_out_norm': 'new_v', 'new_v_gmlp_out_norm': 'new_v', 'new_v_w_out': 'new_v', 'new_v_ffn2_norm': 'new_v', 'new_v_ffn2_w_gu': 'new_v', 'new_v_ffn2_w_down': 'new_v', 'new_v_final_norm': 'new_v'}


def _forward(args):
    return _fwd_reference(*[args[k] for k in FWD_PARAMS])


def _output_shape():
    out = _jax.eval_shape(lambda: _forward(_fwd_setup_inputs(0)))
    return out.shape, out.dtype

N_MICROBATCH = 1
ADAM_LR = 0.001
ADAM_B1 = 0.9
ADAM_B2 = 0.999
ADAM_EPS = 1e-08
ADAM_WD = 0.01
ADAM_STEP = 10
PER_EXAMPLE_BATCH_AXIS = {'x': 0, 'c': 0, 'loss_target': 0}
SHARED_INPUTS = []
_WEIGHT_DTYPES = {'w_ada': _jnp.float32, 'b_ada': _jnp.float32, 'ffn1_norm': _jnp.float32, 'ffn1_w_gu': _jnp.float32, 'ffn1_w_down': _jnp.float32, 'mix_norm': _jnp.float32, 'w_in': _jnp.float32, 'conv_w': _jnp.float32, 'conv_b': _jnp.float32, 'gate_a_w': _jnp.float32, 'gate_a_b': _jnp.float32, 'gate_x_w': _jnp.float32, 'gate_x_b': _jnp.float32, 'lru_lambda': _jnp.float32, 'v_norm': _jnp.float32, 'spatial_w': _jnp.float32, 'spatial_b': _jnp.float32, 'lru_out_norm': _jnp.float32, 'gmlp_out_norm': _jnp.float32, 'w_out': _jnp.float32, 'ffn2_norm': _jnp.float32, 'ffn2_w_gu': _jnp.float32, 'ffn2_w_down': _jnp.float32, 'final_norm': _jnp.float32}
MOMENT_SCALE = {'w_ada': 7.237176e-02, 'b_ada': 1.260740e-01, 'ffn1_norm': 2.940314e-02, 'ffn1_w_gu': 1.258630e-02, 'ffn1_w_down': 2.052429e-02, 'mix_norm': 7.444366e-02, 'w_in': 5.955195e-02, 'conv_w': 5.320920e-02, 'conv_b': 1.382578e-01, 'gate_a_w': 1.034238e-02, 'gate_a_b': 1.209593e-02, 'gate_x_w': 1.820188e-02, 'gate_x_b': 2.004225e-02, 'lru_lambda': 2.880003e-02, 'v_norm': 5.537371e-02, 'spatial_w': 3.772018e-02, 'spatial_b': 1.024767e-01, 'lru_out_norm': 5.423249e-02, 'gmlp_out_norm': 5.485144e-02, 'w_out': 5.432328e-02, 'ffn2_norm': 2.764900e-02, 'ffn2_w_gu': 1.161859e-02, 'ffn2_w_down': 1.907892e-02, 'final_norm': 3.210780e+01}


def _to_microbatches(a, axis):
    t = _jnp.moveaxis(a, axis, 0)
    t = t.reshape((N_MICROBATCH, t.shape[0] // N_MICROBATCH) + t.shape[1:])
    return _jnp.moveaxis(t, 1, axis + 1)


def setup_inputs(seed: int = 0) -> dict:
    inp = _fwd_setup_inputs(seed)
    key = _jax.random.fold_in(_jax.random.key(seed), 7919)
    shape, _ = _output_shape()
    out = dict(inp)
    out["loss_target"] = _jax.random.normal(_jax.random.fold_in(key, 0), shape, _jnp.float32)
    for i, name in enumerate(TWIN_WEIGHTS):
        w = inp[name].astype(_jnp.float32)
        if MOMENT_SCALE is None:
            s = _jnp.sqrt(_jnp.mean(_jnp.square(w)) + 1e-30)
        else:
            s = MOMENT_SCALE[name]
        km, kv = _jax.random.split(_jax.random.fold_in(key, i + 1))
        out[name] = w
        out["m_" + name] = s * _jax.random.normal(km, w.shape, _jnp.float32)
        out["v_" + name] = (s * s) * _jax.random.uniform(kv, w.shape, _jnp.float32, 0.5, 1.5)
    if N_MICROBATCH > 1:
        for name, axis in PER_EXAMPLE_BATCH_AXIS.items():
            out[name] = _to_microbatches(out[name], axis)
    return {'x': out['x'], 'c': out['c'], 'w_ada': out['w_ada'], 'b_ada': out['b_ada'], 'ffn1_norm': out['ffn1_norm'], 'ffn1_w_gu': out['ffn1_w_gu'], 'ffn1_w_down': out['ffn1_w_down'], 'mix_norm': out['mix_norm'], 'w_in': out['w_in'], 'conv_w': out['conv_w'], 'conv_b': out['conv_b'], 'gate_a_w': out['gate_a_w'], 'gate_a_b': out['gate_a_b'], 'gate_x_w': out['gate_x_w'], 'gate_x_b': out['gate_x_b'], 'lru_lambda': out['lru_lambda'], 'v_norm': out['v_norm'], 'spatial_w': out['spatial_w'], 'spatial_b': out['spatial_b'], 'lru_out_norm': out['lru_out_norm'], 'gmlp_out_norm': out['gmlp_out_norm'], 'w_out': out['w_out'], 'ffn2_norm': out['ffn2_norm'], 'ffn2_w_gu': out['ffn2_w_gu'], 'ffn2_w_down': out['ffn2_w_down'], 'final_norm': out['final_norm'], 'loss_target': out['loss_target'], 'm_w_ada': out['m_w_ada'], 'm_b_ada': out['m_b_ada'], 'm_ffn1_norm': out['m_ffn1_norm'], 'm_ffn1_w_gu': out['m_ffn1_w_gu'], 'm_ffn1_w_down': out['m_ffn1_w_down'], 'm_mix_norm': out['m_mix_norm'], 'm_w_in': out['m_w_in'], 'm_conv_w': out['m_conv_w'], 'm_conv_b': out['m_conv_b'], 'm_gate_a_w': out['m_gate_a_w'], 'm_gate_a_b': out['m_gate_a_b'], 'm_gate_x_w': out['m_gate_x_w'], 'm_gate_x_b': out['m_gate_x_b'], 'm_lru_lambda': out['m_lru_lambda'], 'm_v_norm': out['m_v_norm'], 'm_spatial_w': out['m_spatial_w'], 'm_spatial_b': out['m_spatial_b'], 'm_lru_out_norm': out['m_lru_out_norm'], 'm_gmlp_out_norm': out['m_gmlp_out_norm'], 'm_w_out': out['m_w_out'], 'm_ffn2_norm': out['m_ffn2_norm'], 'm_ffn2_w_gu': out['m_ffn2_w_gu'], 'm_ffn2_w_down': out['m_ffn2_w_down'], 'm_final_norm': out['m_final_norm'], 'v_w_ada': out['v_w_ada'], 'v_b_ada': out['v_b_ada'], 'v_ffn1_norm': out['v_ffn1_norm'], 'v_ffn1_w_gu': out['v_ffn1_w_gu'], 'v_ffn1_w_down': out['v_ffn1_w_down'], 'v_mix_norm': out['v_mix_norm'], 'v_w_in': out['v_w_in'], 'v_conv_w': out['v_conv_w'], 'v_conv_b': out['v_conv_b'], 'v_gate_a_w': out['v_gate_a_w'], 'v_gate_a_b': out['v_gate_a_b'], 'v_gate_x_w': out['v_gate_x_w'], 'v_gate_x_b': out['v_gate_x_b'], 'v_lru_lambda': out['v_lru_lambda'], 'v_v_norm': out['v_v_norm'], 'v_spatial_w': out['v_spatial_w'], 'v_spatial_b': out['v_spatial_b'], 'v_lru_out_norm': out['v_lru_out_norm'], 'v_gmlp_out_norm': out['v_gmlp_out_norm'], 'v_w_out': out['v_w_out'], 'v_ffn2_norm': out['v_ffn2_norm'], 'v_ffn2_w_gu': out['v_ffn2_w_gu'], 'v_ffn2_w_down': out['v_ffn2_w_down'], 'v_final_norm': out['v_final_norm']}


def _loss(weights, diff, rest, loss_target):
    with _jax.named_scope("forward"):
        args = {**rest, TWIN_DIFF_INPUT: diff, **{k: w.astype(_WEIGHT_DTYPES[k]) for k, w in weights.items()}}
        y = _forward(args)
    with _jax.named_scope("loss_head"):
        err = _jnp.square(y.astype(_jnp.float32) - loss_target)
        return 0.5 * _jnp.sum(_jnp.mean(err, axis=-1)) if err.ndim else 0.5 * err


def _adamw(w, g, m, v):
    m = ADAM_B1 * m + (1.0 - ADAM_B1) * g
    v = ADAM_B2 * v + (1.0 - ADAM_B2) * _jnp.square(g)
    m_hat = m / (1.0 - ADAM_B1 ** ADAM_STEP)
    v_hat = v / (1.0 - ADAM_B2 ** ADAM_STEP)
    delta = -ADAM_LR * (m_hat / (_jnp.sqrt(v_hat) + ADAM_EPS) + ADAM_WD * w)
    return delta, m, v


def reference(x, c, w_ada, b_ada, ffn1_norm, ffn1_w_gu, ffn1_w_down, mix_norm, w_in, conv_w, conv_b, gate_a_w, gate_a_b, gate_x_w, gate_x_b, lru_lambda, v_norm, spatial_w, spatial_b, lru_out_norm, gmlp_out_norm, w_out, ffn2_norm, ffn2_w_gu, ffn2_w_down, final_norm, loss_target, m_w_ada, m_b_ada, m_ffn1_norm, m_ffn1_w_gu, m_ffn1_w_down, m_mix_norm, m_w_in, m_conv_w, m_conv_b, m_gate_a_w, m_gate_a_b, m_gate_x_w, m_gate_x_b, m_lru_lambda, m_v_norm, m_spatial_w, m_spatial_b, m_lru_out_norm, m_gmlp_out_norm, m_w_out, m_ffn2_norm, m_ffn2_w_gu, m_ffn2_w_down, m_final_norm, v_w_ada, v_b_ada, v_ffn1_norm, v_ffn1_w_gu, v_ffn1_w_down, v_mix_norm, v_w_in, v_conv_w, v_conv_b, v_gate_a_w, v_gate_a_b, v_gate_x_w, v_gate_x_b, v_lru_lambda, v_v_norm, v_spatial_w, v_spatial_b, v_lru_out_norm, v_gmlp_out_norm, v_w_out, v_ffn2_norm, v_ffn2_w_gu, v_ffn2_w_down, v_final_norm):
    given = dict(x=x, c=c, w_ada=w_ada, b_ada=b_ada, ffn1_norm=ffn1_norm, ffn1_w_gu=ffn1_w_gu, ffn1_w_down=ffn1_w_down, mix_norm=mix_norm, w_in=w_in, conv_w=conv_w, conv_b=conv_b, gate_a_w=gate_a_w, gate_a_b=gate_a_b, gate_x_w=gate_x_w, gate_x_b=gate_x_b, lru_lambda=lru_lambda, v_norm=v_norm, spatial_w=spatial_w, spatial_b=spatial_b, lru_out_norm=lru_out_norm, gmlp_out_norm=gmlp_out_norm, w_out=w_out, ffn2_norm=ffn2_norm, ffn2_w_gu=ffn2_w_gu, ffn2_w_down=ffn2_w_down, final_norm=final_norm, loss_target=loss_target, m_w_ada=m_w_ada, m_b_ada=m_b_ada, m_ffn1_norm=m_ffn1_norm, m_ffn1_w_gu=m_ffn1_w_gu, m_ffn1_w_down=m_ffn1_w_down, m_mix_norm=m_mix_norm, m_w_in=m_w_in, m_conv_w=m_conv_w, m_conv_b=m_conv_b, m_gate_a_w=m_gate_a_w, m_gate_a_b=m_gate_a_b, m_gate_x_w=m_gate_x_w, m_gate_x_b=m_gate_x_b, m_lru_lambda=m_lru_lambda, m_v_norm=m_v_norm, m_spatial_w=m_spatial_w, m_spatial_b=m_spatial_b, m_lru_out_norm=m_lru_out_norm, m_gmlp_out_norm=m_gmlp_out_norm, m_w_out=m_w_out, m_ffn2_norm=m_ffn2_norm, m_ffn2_w_gu=m_ffn2_w_gu, m_ffn2_w_down=m_ffn2_w_down, m_final_norm=m_final_norm, v_w_ada=v_w_ada, v_b_ada=v_b_ada, v_ffn1_norm=v_ffn1_norm, v_ffn1_w_gu=v_ffn1_w_gu, v_ffn1_w_down=v_ffn1_w_down, v_mix_norm=v_mix_norm, v_w_in=v_w_in, v_conv_w=v_conv_w, v_conv_b=v_conv_b, v_gate_a_w=v_gate_a_w, v_gate_a_b=v_gate_a_b, v_gate_x_w=v_gate_x_w, v_gate_x_b=v_gate_x_b, v_lru_lambda=v_lru_lambda, v_v_norm=v_v_norm, v_spatial_w=v_spatial_w, v_spatial_b=v_spatial_b, v_lru_out_norm=v_lru_out_norm, v_gmlp_out_norm=v_gmlp_out_norm, v_w_out=v_w_out, v_ffn2_norm=v_ffn2_norm, v_ffn2_w_gu=v_ffn2_w_gu, v_ffn2_w_down=v_ffn2_w_down, v_final_norm=v_final_norm)
    weights = {n: given[n] for n in TWIN_WEIGHTS}
    shared = {n: given[n] for n in SHARED_INPUTS}
    per_example = {n: given[n] for n in ['x', 'c']}
    grad_fn = _jax.value_and_grad(_loss, argnums=(0, 1))

    def one_microbatch(ex, loss_target):
        ex = dict(ex)
        diff = ex.pop(TWIN_DIFF_INPUT)
        return grad_fn(weights, diff, {**shared, **ex}, loss_target)

    if N_MICROBATCH == 1:
        loss, (grad_w, grad_x) = one_microbatch(per_example, given["loss_target"])
    else:
        def body(carry, xs):
            loss_sum, grad_sum = carry
            l_k, (gw_k, gx_k) = one_microbatch(xs[0], xs[1])
            with _jax.named_scope("update"):
                return (loss_sum + l_k, _jax.tree.map(_jnp.add, grad_sum, gw_k)), gx_k

        init = (_jnp.zeros((), _jnp.float32), _jax.tree.map(_jnp.zeros_like, weights))
        (loss, grad_w), grad_x = _jax.lax.scan(body, init, (per_example, given["loss_target"]))
    with _jax.named_scope("update"):
        delta_w, new_m, new_v = {}, {}, {}
        for n in TWIN_WEIGHTS:
            delta_w[n], new_m[n], new_v[n] = _adamw(weights[n], grad_w[n], given["m_" + n], given["v_" + n])
    return (loss, grad_x, *[grad_w[n] for n in TWIN_WEIGHTS], *[delta_w[n] for n in TWIN_WEIGHTS],
            *[new_m[n] for n in TWIN_WEIGHTS], *[new_v[n] for n in TWIN_WEIGHTS])
```

```python
import math

import jax
import jax.numpy as jnp
from jax import lax
from jax.experimental import pallas as pl
from jax.experimental.pallas import tpu as pltpu

F32 = jnp.float32
MXU_DTYPE = jnp.bfloat16
ACT_DTYPE = jnp.bfloat16
EPS = 1e-6
RG_LRU_C = 8.0
N_MOD = 9
CONV_WIDTH = 4
HEADS = 8
CHUNK = 128
LANES = 128
SUBLANES = 8
N_CHIPS = 4
N_DEV = 8
ADAM_LR, ADAM_B1, ADAM_B2, ADAM_EPS, ADAM_WD, ADAM_STEP = 0.001, 0.9, 0.999, 1e-08, 0.01, 10
VMEM_LIMIT_BYTES = 60 * 1024 * 1024
ROW_TILE_BYTES = 1 << 20
GELU_C = math.sqrt(2.0 / math.pi)
GELU_A = 0.044715

ANY = pl.BlockSpec(memory_space=pl.ANY)
MESH = pl.DeviceIdType.MESH
SDS = jax.ShapeDtypeStruct


def _params(*sem):
    return pltpu.CompilerParams(dimension_semantics=sem, vmem_limit_bytes=VMEM_LIMIT_BYTES)


def _dot(a, b):
    return jnp.dot(a.astype(MXU_DTYPE), b.astype(MXU_DTYPE), preferred_element_type=F32)


def _dot_nt(a, b):
    return lax.dot_general(a.astype(MXU_DTYPE), b.astype(MXU_DTYPE), (((1,), (1,)), ((), ())), preferred_element_type=F32)


def _dot_tn(a, b):
    return lax.dot_general(a.astype(MXU_DTYPE), b.astype(MXU_DTYPE), (((0,), (0,)), ((), ())), preferred_element_type=F32)


def _gelu(x):
    return x * (0.5 * (1.0 + jnp.tanh(GELU_C * (x + GELU_A * (x * x * x)))))


def _gelu_grad(x):
    t = jnp.tanh(GELU_C * (x + GELU_A * (x * x * x)))
    return 0.5 * (1.0 + t) + 0.5 * x * (1.0 - t * t) * (GELU_C * (1.0 + 3.0 * GELU_A * x * x))


def _sigmoid(x):
    return jax.nn.sigmoid(x)


def _rsqrt_ms(x):
    return lax.rsqrt(jnp.mean(x * x, axis=-1, keepdims=True) + EPS)


def _rowsum(x):
    return jnp.sum(x, axis=0, keepdims=True)


def _tile(n, want):
    t = min(n, want)
    assert n % t == 0, (n, want)
    return t


def _row_tile(rows, row_bytes):
    cap = max(SUBLANES, ROW_TILE_BYTES // row_bytes)
    best = None
    for t in range(SUBLANES, min(rows, cap) + 1, SUBLANES):
        if rows % t == 0:
            best = t
    assert best is not None, (rows, row_bytes)
    return best


def _modnorm(x, gn, sh, sc):
    S, D = x.shape
    tm = _tile(S, 512)

    def body(x_ref, gn_ref, sh_ref, sc_ref, h_ref):
        xv = x_ref[...]
        h = (xv * _rsqrt_ms(xv) * gn_ref[...]) * (1.0 + sc_ref[...]) + sh_ref[...]
        h_ref[...] = h.astype(ACT_DTYPE)

    row = pl.BlockSpec((1, D), lambda i: (0, 0))
    return pl.pallas_call(
        body, name="modnorm", grid=(S // tm,),
        in_specs=[pl.BlockSpec((tm, D), lambda i: (i, 0)), row, row, row],
        out_specs=pl.BlockSpec((tm, D), lambda i: (i, 0)),
        out_shape=SDS((S, D), ACT_DTYPE), compiler_params=_params("parallel"),
    )(x, gn, sh, sc)


def _norm_bwd(x, dh, dxo, f, gn, sc, res_scale, next_gate, next_scale):
    S, D = x.shape
    tm = _tile(S, 256)

    def body(x_ref, dh_ref, dxo_ref, f_ref, gn_ref, sc_ref, ng_ref, dx_ref, dq_ref, acc_ref):
        @pl.when(pl.program_id(0) == 0)
        def _():
            acc_ref[...] = jnp.zeros_like(acc_ref)

        xv, dh, dxo = x_ref[...], dh_ref[...], dxo_ref[...]
        r = _rsqrt_ms(xv)
        xhat = xv * r
        gn = gn_ref[...]
        dn = dh * (1.0 + sc_ref[...])
        dxh = dn * gn
        dx = dxo + r * (dxh - xhat * jnp.mean(dxh * xhat, axis=-1, keepdims=True))
        dx_ref[...] = dx
        dq_ref[...] = ((next_scale * ng_ref[...]) * dx).astype(ACT_DTYPE)
        acc_ref[0:1, :] += _rowsum(dh)
        acc_ref[1:2, :] += _rowsum(dh * (xhat * gn))
        acc_ref[2:3, :] += _rowsum(dn * xhat)
        acc_ref[3:4, :] += _rowsum((res_scale * f_ref[...]) * dxo)

    tile = pl.BlockSpec((tm, D), lambda i: (i, 0))
    row = pl.BlockSpec((1, D), lambda i: (0, 0))
    return pl.pallas_call(
        body, name="norm_bwd", grid=(S // tm,),
        in_specs=[tile, tile, tile, tile, row, row, row],
        out_specs=[tile, tile, pl.BlockSpec((SUBLANES, D), lambda i: (0, 0))],
        out_shape=[SDS((S, D), F32), SDS((S, D), ACT_DTYPE), SDS((SUBLANES, D), F32)],
        compiler_params=_params("arbitrary"),
    )(x, dh, dxo, f, gn, sc, next_gate)


def _loss_head(x, target, gn, next_gate, next_scale):
    S, D = x.shape
    tm = _tile(S, 256)

    def body(x_ref, t_ref, gn_ref, ng_ref, dx_ref, dq_ref, acc_ref):
        @pl.when(pl.program_id(0) == 0)
        def _():
            acc_ref[...] = jnp.zeros_like(acc_ref)

        xv = x_ref[...]
        r = _rsqrt_ms(xv)
        xhat = xv * r
        gn = gn_ref[...]
        err = xhat * gn - t_ref[...]
        dy = err * (1.0 / D)
        dxh = dy * gn
        dx = r * (dxh - xhat * jnp.mean(dxh * xhat, axis=-1, keepdims=True))
        dx_ref[...] = dx
        dq_ref[...] = ((next_scale * ng_ref[...]) * dx).astype(ACT_DTYPE)
        acc_ref[0:1, :] += _rowsum(dy * xhat)
        acc_ref[1:2, :] += _rowsum(err * err) * (0.5 / D)

    tile = pl.BlockSpec((tm, D), lambda i: (i, 0))
    row = pl.BlockSpec((1, D), lambda i: (0, 0))
    return pl.pallas_call(
        body, name="loss_head", grid=(S // tm,),
        in_specs=[tile, tile, row, row],
        out_specs=[tile, tile, pl.BlockSpec((SUBLANES, D), lambda i: (0, 0))],
        out_shape=[SDS((S, D), F32), SDS((S, D), ACT_DTYPE), SDS((SUBLANES, D), F32)],
        compiler_params=_params("arbitrary"),
    )(x, target, gn, next_gate)


def _ffn_up(h, wgu):
    S, D = h.shape
    C = wgu.shape[2]
    tm = _tile(S, 256)

    def body(h_ref, wg_ref, wu_ref, a_ref, gu_ref):
        hv = h_ref[...]
        g = _dot(hv, wg_ref[...])
        u = _dot(hv, wu_ref[...])
        a_ref[...] = (g * _sigmoid(g) * u).astype(ACT_DTYPE)
        gu_ref[0] = g.astype(ACT_DTYPE)
        gu_ref[1] = u.astype(ACT_DTYPE)

    return pl.pallas_call(
        body, name="ffn_up", grid=(2, S // tm),
        in_specs=[
            pl.BlockSpec((tm, D), lambda j, i: (i, 0)),
            pl.BlockSpec((None, D, C), lambda j, i: (j, 0, 0)),
            pl.BlockSpec((None, D, C), lambda j, i: (2 + j, 0, 0)),
        ],
        out_specs=[
            pl.BlockSpec((tm, C), lambda j, i: (i, j)),
            pl.BlockSpec((2, None, tm, C), lambda j, i: (0, j, i, 0)),
        ],
        out_shape=[SDS((S, 2 * C), ACT_DTYPE), SDS((2, 2, S, C), ACT_DTYPE)],
        compiler_params=_params("parallel", "parallel"),
    )(h, wgu, wgu)


def _ffn_bwd_act(dq, wd, gu):
    S, D = dq.shape
    C = gu.shape[3]
    tm = _tile(S, 256)

    def body(dq_ref, wd_ref, gu_ref, dgu_ref):
        da = _dot_nt(dq_ref[...], wd_ref[...])
        g = gu_ref[0].astype(F32)
        u = gu_ref[1].astype(F32)
        s = _sigmoid(g)
        dgu_ref[0] = (da * u * (s * (1.0 + g * (1.0 - s)))).astype(ACT_DTYPE)
        dgu_ref[1] = (da * (g * s)).astype(ACT_DTYPE)

    gu_spec = pl.BlockSpec((2, None, tm, C), lambda j, i: (0, j, i, 0))
    return pl.pallas_call(
        body, name="ffn_bwd_act", grid=(2, S // tm),
        in_specs=[pl.BlockSpec((tm, D), lambda j, i: (i, 0)), pl.BlockSpec((C, D), lambda j, i: (j, 0)), gu_spec],
        out_specs=gu_spec,
        out_shape=SDS(gu.shape, ACT_DTYPE),
        compiler_params=_params("parallel", "parallel"),
    )(dq, wd, gu)


def _mm_res(a, w, x, gate, scale):
    S, K = a.shape
    D = w.shape[1]
    tm, tn = _tile(S, 512), _tile(D, 512)

    def body(a_ref, w_ref, x_ref, g_ref, f_ref, xo_ref):
        f = _dot(a_ref[...], w_ref[...])
        f_ref[...] = f
        xo_ref[...] = x_ref[...] + (scale * g_ref[...]) * f

    tile = pl.BlockSpec((tm, tn), lambda j, i: (i, j))
    return pl.pallas_call(
        body, name=f"mm_res_k{K}", grid=(D // tn, S // tm),
        in_specs=[pl.BlockSpec((tm, K), lambda j, i: (i, 0)), pl.BlockSpec((K, tn), lambda j, i: (0, j)), tile,
                  pl.BlockSpec((1, tn), lambda j, i: (0, j))],
        out_specs=[tile, tile],
        out_shape=[SDS((S, D), F32), SDS((S, D), F32)],
        compiler_params=_params("parallel", "parallel"),
    )(a, w, x, gate)


def _mm_chunks(h, wc):
    S, K = h.shape
    P, _, N = wc.shape
    tm = _tile(S, 512)

    def body(h_ref, w_ref, o_ref):
        o_ref[...] = _dot(h_ref[...], w_ref[...])

    return pl.pallas_call(
        body, name="mm_chunks", grid=(P, S // tm),
        in_specs=[pl.BlockSpec((tm, K), lambda j, i: (i, 0)), pl.BlockSpec((None, K, N), lambda j, i: (j, 0, 0))],
        out_specs=pl.BlockSpec((tm, N), lambda j, i: (i, j)),
        out_shape=SDS((S, P * N), F32),
        compiler_params=_params("parallel", "parallel"),
    )(h, wc)


def _mm_nt_chunks(ac, wc):
    P, S, K = ac.shape
    N = wc.shape[1]
    tm, tn = _tile(S, 256), _tile(N, 512)

    def body(a_ref, w_ref, o_ref):
        acc = _dot_nt(a_ref[0], w_ref[0])
        for p in range(1, P):
            acc += _dot_nt(a_ref[p], w_ref[p])
        o_ref[...] = acc

    return pl.pallas_call(
        body, name=f"mm_nt_p{P}k{K}", grid=(S // tm, N // tn),
        in_specs=[pl.BlockSpec((P, tm, K), lambda i, j: (0, i, 0)), pl.BlockSpec((P, tn, K), lambda i, j: (0, j, 0))],
        out_specs=pl.BlockSpec((tm, tn), lambda i, j: (i, j)),
        out_shape=SDS((S, N), F32),
        compiler_params=_params("parallel", "parallel"),
    )(ac, wc)


def _mm_tn_chunks(a, bc, tile_m, tile_n):
    S, M = a.shape
    P, _, N = bc.shape
    ts, tm, tn = _tile(S, 512), _tile(M, tile_m), _tile(N, tile_n)

    def body(a_ref, b_ref, o_ref):
        @pl.when(pl.program_id(3) == 0)
        def _():
            o_ref[...] = jnp.zeros_like(o_ref)

        o_ref[...] += _dot_tn(a_ref[...], b_ref[...])

    return pl.pallas_call(
        body, name=f"mm_tn_m{M}n{N}", grid=(P, M // tm, N // tn, S // ts),
        in_specs=[pl.BlockSpec((ts, tm), lambda p, m, n, k: (k, m)), pl.BlockSpec((None, ts, tn), lambda p, m, n, k: (p, k, n))],
        out_specs=pl.BlockSpec((None, tm, tn), lambda p, m, n, k: (p, m, n)),
        out_shape=SDS((P, M, N), F32),
        compiler_params=_params("parallel", "parallel", "parallel", "arbitrary"),
    )(a, bc)


def _shift_down(x, s, row, fill):
    return jnp.where(row >= s, pltpu.roll(x, s, 0), fill)


def _shift_up(x, s, row, fill):
    n = x.shape[0]
    return jnp.where(row < n - s, pltpu.roll(x, n - s, 0), fill)


def _scan_down(a, b, row):
    s = 1
    while s < a.shape[0]:
        b = a * _shift_down(b, s, row, 0.0) + b
        a = a * _shift_down(a, s, row, 1.0)
        s *= 2
    return b


def _scan_up(a, b, row):
    s = 1
    while s < a.shape[0]:
        b = a * _shift_up(b, s, row, 0.0) + b
        a = a * _shift_up(a, s, row, 1.0)
        s *= 2
    return b


def _conv(xl, cw_ref, cb_ref, row):
    y = cb_ref[...] + _shift_down(xl, 3, row, 0.0) * cw_ref[0:1, :]
    y = y + _shift_down(xl, 2, row, 0.0) * cw_ref[1:2, :]
    y = y + _shift_down(xl, 1, row, 0.0) * cw_ref[2:3, :]
    return y + xl * cw_ref[3:4, :]


def _lru_gates(xc, wa_ref, ba_ref, wx_ref, bx_ref, lam_ref):
    ra = _sigmoid(_dot(xc, wa_ref[...]) + ba_ref[...])
    ri = _sigmoid(_dot(xc, wx_ref[...]) + bx_ref[...])
    ls = jax.nn.log_sigmoid(lam_ref[...])
    a = jnp.exp((RG_LRU_C * ra) * ls)
    mult = jnp.sqrt(1.0 - a * a)
    return ra, ri, ls, a, mult


def _lru_specs(S):
    col = lambda off: pl.BlockSpec((S, LANES), lambda j: (0, off + j))
    vec = pl.BlockSpec((1, LANES), lambda j: (0, j))
    blk = pl.BlockSpec((None, LANES, LANES), lambda j: (j, 0, 0))
    cw = pl.BlockSpec((CONV_WIDTH, LANES), lambda j: (0, j))
    return col, vec, blk, cw


def _lru_fwd(proj, cw, cb, wa, ba, wx, bx, lam):
    S = proj.shape[0]
    W = cb.shape[1]
    nb = W // LANES

    def body(xl_ref, gl_ref, cw_ref, cb_ref, wa_ref, ba_ref, wx_ref, bx_ref, lam_ref, y_ref):
        row = lax.broadcasted_iota(jnp.int32, (S, LANES), 0)
        xc = _conv(xl_ref[...], cw_ref, cb_ref, row)
        _, ri, _, a, mult = _lru_gates(xc, wa_ref, ba_ref, wx_ref, bx_ref, lam_ref)
        h = _scan_down(a, mult * (ri * xc), row)
        y_ref[...] = h * _gelu(gl_ref[...])

    col, vec, blk, cws = _lru_specs(S)
    return pl.pallas_call(
        body, name="lru_fwd", grid=(nb,),
        in_specs=[col(0), col(nb), cws, vec, blk, vec, blk, vec, vec],
        out_specs=pl.BlockSpec((S, LANES), lambda j: (0, j)),
        out_shape=SDS((S, W), F32), compiler_params=_params("parallel"),
    )(proj, proj, cw, cb, wa, ba, wx, bx, lam)


def _lru_bwd(proj, dy, cw, cb, wa, ba, wx, bx, lam, wat, wxt):
    S = proj.shape[0]
    W = cb.shape[1]
    nb = W // LANES

    def body(xl_ref, gl_ref, dy_ref, cw_ref, cb_ref, wa_ref, ba_ref, wx_ref, bx_ref, lam_ref, wat_ref, wxt_ref,
             dp_ref, dwa_ref, dwx_ref, vec_ref):
        row = lax.broadcasted_iota(jnp.int32, (S, LANES), 0)
        xl = xl_ref[...]
        xc = _conv(xl, cw_ref, cb_ref, row)
        ra, ri, ls, a, mult = _lru_gates(xc, wa_ref, ba_ref, wx_ref, bx_ref, lam_ref)
        h = _scan_down(a, mult * (ri * xc), row)
        gl = gl_ref[...]
        dyv = dy_ref[...]
        dp_ref[1] = (dyv * h * _gelu_grad(gl)).astype(ACT_DTYPE)
        adj = _scan_up(_shift_up(a, 1, row, 0.0), dyv * _gelu(gl), row)
        da = adj * _shift_down(h, 1, row, 0.0)
        dmult = adj * (ri * xc)
        dlog_a = da * a - dmult * (a * a) / mult
        dra = dlog_a * (RG_LRU_C * ls)
        dpa = dra * ra * (1.0 - ra)
        dpi = (adj * mult * xc) * ri * (1.0 - ri)
        dxc = adj * mult * ri + _dot(dpa, wat_ref[...]) + _dot(dpi, wxt_ref[...])
        dwa_ref[...] = _dot_tn(xc, dpa)
        dwx_ref[...] = _dot_tn(xc, dpi)
        dxl = dxc * cw_ref[3:4, :]
        dxl = dxl + _shift_up(dxc, 1, row, 0.0) * cw_ref[2:3, :]
        dxl = dxl + _shift_up(dxc, 2, row, 0.0) * cw_ref[1:2, :]
        dxl = dxl + _shift_up(dxc, 3, row, 0.0) * cw_ref[0:1, :]
        dp_ref[0] = dxl.astype(ACT_DTYPE)
        vec_ref[...] = jnp.zeros_like(vec_ref)
        vec_ref[0:1, :] = _rowsum(dpa)
        vec_ref[1:2, :] = _rowsum(dpi)
        vec_ref[2:3, :] = _rowsum(dlog_a * (RG_LRU_C * ra)) * _sigmoid(-lam_ref[...])
        vec_ref[3:4, :] = _rowsum(dxc)
        vec_ref[4:5, :] = _rowsum(dxc * _shift_down(xl, 3, row, 0.0))
        vec_ref[5:6, :] = _rowsum(dxc * _shift_down(xl, 2, row, 0.0))
        vec_ref[6:7, :] = _rowsum(dxc * _shift_down(xl, 1, row, 0.0))
        vec_ref[7:8, :] = _rowsum(dxc * xl)

    col, vec, blk, cws = _lru_specs(S)
    return pl.pallas_call(
        body, name="lru_bwd", grid=(nb,),
        in_specs=[col(0), col(nb), col(0), cws, vec, blk, vec, blk, vec, vec, blk, blk],
        out_specs=[pl.BlockSpec((2, S, LANES), lambda j: (0, 0, j)), blk, blk, pl.BlockSpec((2 * SUBLANES, LANES), lambda j: (0, j))],
        out_shape=[SDS((2, S, W), ACT_DTYPE), SDS((nb, LANES, LANES), F32), SDS((nb, LANES, LANES), F32), SDS((2 * SUBLANES, W), F32)],
        compiler_params=_params("parallel"),
    )(proj, proj, dy, cw, cb, wa, ba, wx, bx, lam, wat, wxt)


def _seg_mean(x, seg_ref, width):
    hi = x.astype(jnp.bfloat16)
    lo = (x - hi.astype(F32)).astype(jnp.bfloat16)
    ones = seg_ref[...]
    s = jnp.dot(hi, ones, preferred_element_type=F32) + jnp.dot(lo, ones, preferred_element_type=F32)
    return s * (1.0 / width)


def _gmlp_core(u_ref, v_ref, gv_ref, seg_ref, ws_ref, bfull_ref, z_scr, hd):
    tm, W = u_ref.shape
    lane = lax.broadcasted_iota(jnp.int32, (CHUNK, LANES), 1)
    ug = _gelu(u_ref[...])
    vg = _gelu(v_ref[...])
    cen = vg - _seg_mean(vg, seg_ref, hd)
    rstd = lax.rsqrt(_seg_mean(cen * cen, seg_ref, hd) + EPS)
    vhat = cen * rstd
    vh = vhat * gv_ref[...]
    vcats = {}
    for ci in range(tm // CHUNK):
        for p in range(W // LANES):
            blk = vh[ci * CHUNK:(ci + 1) * CHUNK, p * LANES:(p + 1) * LANES]
            vcat = jnp.concatenate([jnp.where(lane < hd, blk, 0.0), jnp.where(lane >= hd, blk, 0.0)], axis=0).astype(MXU_DTYPE)
            vcats[ci, p] = vcat
            z_scr[ci * CHUNK:(ci + 1) * CHUNK, p * LANES:(p + 1) * LANES] = (
                jnp.dot(ws_ref[p], vcat, preferred_element_type=F32) + bfull_ref[:, p * LANES:(p + 1) * LANES])
    return ug, vhat, rstd, vcats


def _gmlp_specs(tm, W, nb):
    rows = lambda off: pl.BlockSpec((tm, W), lambda i: (i, off))
    vec = pl.BlockSpec((1, W), lambda i: (0, 0))
    seg = pl.BlockSpec((W, W), lambda i: (0, 0))
    wsp = pl.BlockSpec((nb, CHUNK, 2 * CHUNK), lambda i: (0, 0, 0))
    bfull = pl.BlockSpec((CHUNK, W), lambda i: (0, 0))
    return rows, vec, seg, wsp, bfull


def _gmlp_fwd(proj, ylru, gv, seg, wsp, bfull, g_lru, g_gm):
    S, W = ylru.shape
    nb = W // LANES
    hd = W // HEADS
    tm = _tile(S, 512)

    def body(u_ref, v_ref, yl_ref, gv_ref, seg_ref, ws_ref, bfull_ref, gl_ref, gg_ref, yn_ref, ygm_ref, z_scr):
        ug, _, _, _ = _gmlp_core(u_ref, v_ref, gv_ref, seg_ref, ws_ref, bfull_ref, z_scr, hd)
        ygm = ug * z_scr[...]
        ygm_ref[...] = ygm
        yl = yl_ref[...]
        yn_ref[:, 0:W] = (yl * _rsqrt_ms(yl) * gl_ref[...]).astype(ACT_DTYPE)
        yn_ref[:, W:2 * W] = (ygm * _rsqrt_ms(ygm) * gg_ref[...]).astype(ACT_DTYPE)

    rows, vec, segs, wsps, bfulls = _gmlp_specs(tm, W, nb)
    return pl.pallas_call(
        body, name="gmlp_fwd", grid=(S // tm,),
        in_specs=[rows(2), rows(3), rows(0), vec, segs, wsps, bfulls, vec, vec],
        out_specs=[pl.BlockSpec((tm, 2 * W), lambda i: (i, 0)), rows(0)],
        out_shape=[SDS((S, 2 * W), ACT_DTYPE), SDS((S, W), F32)],
        scratch_shapes=[pltpu.VMEM((tm, W), F32)],
        compiler_params=_params("parallel"),
    )(proj, proj, ylru, gv, seg, wsp, bfull, g_lru, g_gm)


def _rms_bwd(y, g, dyn):
    r = _rsqrt_ms(y)
    yhat = y * r
    dyh = dyn * g
    return r * (dyh - yhat * jnp.mean(dyh * yhat, axis=-1, keepdims=True)), _rowsum(dyn * yhat)


def _gmlp_bwd(proj, ylru, ygm, dyn, gv, seg, wsp, wspt, bfull, g_lru, g_gm):
    S, W = ylru.shape
    nb = W // LANES
    hd = W // HEADS
    tm = _tile(S, 256)

    def body(u_ref, v_ref, yl_ref, ygm_ref, dl_ref, dg_ref, gv_ref, seg_ref, ws_ref, wst_ref, bfull_ref, gl_ref, gg_ref,
             dyl_ref, duv_ref, dws_ref, dbf_ref, acc_ref, z_scr, dvh_scr):
        @pl.when(pl.program_id(0) == 0)
        def _():
            dws_ref[...] = jnp.zeros_like(dws_ref)
            dbf_ref[...] = jnp.zeros_like(dbf_ref)
            acc_ref[...] = jnp.zeros_like(acc_ref)

        dyl, dgl = _rms_bwd(yl_ref[...], gl_ref[...], dl_ref[...])
        dyl_ref[...] = dyl
        dygm, dgg = _rms_bwd(ygm_ref[...], gg_ref[...], dg_ref[...])
        ug, vhat, rstd, vcats = _gmlp_core(u_ref, v_ref, gv_ref, seg_ref, ws_ref, bfull_ref, z_scr, hd)
        duv_ref[0] = (dygm * z_scr[...] * _gelu_grad(u_ref[...])).astype(ACT_DTYPE)
        dz = dygm * ug
        lane = lax.broadcasted_iota(jnp.int32, (CHUNK, LANES), 1)
        dbf = dz[0:CHUNK, :]
        for ci in range(1, tm // CHUNK):
            dbf += dz[ci * CHUNK:(ci + 1) * CHUNK, :]
        dbf_ref[...] += dbf
        for ci in range(tm // CHUNK):
            for p in range(nb):
                dzb = dz[ci * CHUNK:(ci + 1) * CHUNK, p * LANES:(p + 1) * LANES].astype(MXU_DTYPE)
                dws_ref[p] += _dot_nt(dzb, vcats[ci, p])
                dvc = jnp.dot(wst_ref[p], dzb, preferred_element_type=F32)
                dvh_scr[ci * CHUNK:(ci + 1) * CHUNK, p * LANES:(p + 1) * LANES] = jnp.where(lane < hd, dvc[0:CHUNK], dvc[CHUNK:2 * CHUNK])
        dvh = dvh_scr[...]
        dvn = dvh * gv_ref[...]
        dvg = rstd * (dvn - _seg_mean(dvn, seg_ref, hd) - vhat * _seg_mean(dvn * vhat, seg_ref, hd))
        duv_ref[1] = (dvg * _gelu_grad(v_ref[...])).astype(ACT_DTYPE)
        acc_ref[0:1, :] += dgl
        acc_ref[1:2, :] += dgg
        acc_ref[2:3, :] += _rowsum(dvh * vhat)

    rows, vec, segs, wsps, bfulls = _gmlp_specs(tm, W, nb)
    wspt_spec = pl.BlockSpec((nb, 2 * CHUNK, CHUNK), lambda i: (0, 0, 0))
    return pl.pallas_call(
        body, name="gmlp_bwd", grid=(S // tm,),
        in_specs=[rows(2), rows(3), rows(0), rows(0), rows(0), rows(1), vec, segs, wsps, wspt_spec, bfulls, vec, vec],
        out_specs=[rows(0), pl.BlockSpec((2, tm, W), lambda i: (0, i, 0)), wsps, bfulls, pl.BlockSpec((SUBLANES, W), lambda i: (0, 0))],
        out_shape=[SDS((S, W), F32), SDS((2, S, W), ACT_DTYPE), SDS((nb, CHUNK, 2 * CHUNK), F32), SDS((CHUNK, W), F32), SDS((SUBLANES, W), F32)],
        scratch_shapes=[pltpu.VMEM((tm, W), F32), pltpu.VMEM((tm, W), F32)],
        compiler_params=_params("arbitrary"),
    )(proj, proj, ylru, ygm, dyn, dyn, gv, seg, wsp, wspt, bfull, g_lru, g_gm)


def _ada_fwd(c_all, w_ada, b_shard):
    L, D, N = w_ada.shape
    R = c_all.shape[0]
    tn = N // 2

    def body(c_ref, w_ref, b_ref, o_ref):
        cv = c_ref[...]
        o_ref[...] = _dot(cv * _sigmoid(cv), w_ref[...]) + b_ref[...]

    return pl.pallas_call(
        body, name="ada_fwd", grid=(L, N // tn),
        in_specs=[pl.BlockSpec((R, D), lambda l, j: (0, 0)), pl.BlockSpec((None, D, tn), lambda l, j: (l, 0, j)),
                  pl.BlockSpec((None, 1, tn), lambda l, j: (l, 0, j))],
        out_specs=pl.BlockSpec((None, R, tn), lambda l, j: (l, 0, j)),
        out_shape=SDS((L, R, N), F32), compiler_params=_params("parallel", "parallel"),
    )(c_all, w_ada, b_shard)


def _ada_grad(c_all_t, dmod):
    D, B = c_all_t.shape
    L, _, N = dmod.shape
    tn = N // 2

    def body(c_ref, d_ref, o_ref):
        cv = c_ref[...]
        sc = cv * _sigmoid(cv)
        acc = sc[:, 0:1] * d_ref[0:1, :]
        for b in range(1, B):
            acc += sc[:, b:b + 1] * d_ref[b:b + 1, :]
        o_ref[...] = acc

    return pl.pallas_call(
        body, name="ada_grad", grid=(L, N // tn),
        in_specs=[pl.BlockSpec((D, B), lambda l, j: (0, 0)), pl.BlockSpec((None, B, tn), lambda l, j: (l, 0, j))],
        out_specs=pl.BlockSpec((None, D, tn), lambda l, j: (l, 0, j)),
        out_shape=SDS((L, D, N), F32), compiler_params=_params("parallel", "parallel"),
    )(c_all_t, dmod)


def _adamw(w, g, m, v):
    R, C = w.shape
    tr = _row_tile(R, C * 4)

    def body(w_ref, g_ref, m_ref, v_ref, d_ref, mo_ref, vo_ref):
        gv = g_ref[...]
        mn = ADAM_B1 * m_ref[...] + (1.0 - ADAM_B1) * gv
        vn = ADAM_B2 * v_ref[...] + (1.0 - ADAM_B2) * (gv * gv)
        mo_ref[...] = mn
        vo_ref[...] = vn
        m_hat = mn / (1.0 - ADAM_B1 ** ADAM_STEP)
        v_hat = vn / (1.0 - ADAM_B2 ** ADAM_STEP)
        d_ref[...] = -ADAM_LR * (m_hat / (jnp.sqrt(v_hat) + ADAM_EPS) + ADAM_WD * w_ref[...])

    tile = pl.BlockSpec((tr, C), lambda i: (i, 0))
    return pl.pallas_call(
        body, name=f"adamw_r{R}c{C}", grid=(R // tr,), in_specs=[tile] * 4, out_specs=[tile] * 3,
        out_shape=[SDS((R, C), F32)] * 3, compiler_params=_params("parallel"),
    )(w, g, m, v)


def _sum_leading(a):
    P, R, C = a.shape
    tr = _row_tile(R, P * C * 4)

    def body(a_ref, o_ref):
        acc = a_ref[0]
        for p in range(1, P):
            acc = acc + a_ref[p]
        o_ref[...] = acc

    return pl.pallas_call(
        body, name=f"sum{P}_r{R}c{C}", grid=(R // tr,),
        in_specs=[pl.BlockSpec((P, tr, C), lambda i: (0, i, 0))],
        out_specs=pl.BlockSpec((tr, C), lambda i: (i, 0)),
        out_shape=SDS((R, C), F32), compiler_params=_params("parallel"),
    )(a)


def _add_half(g4, r1, core):
    _, _, R, C = g4.shape
    tr = _row_tile(R, C * 4)

    def body(core_ref, g_ref, r_ref, o_ref):
        o_ref[...] = g_ref[...] + r_ref[...]

    return pl.pallas_call(
        body, name=f"add_half_r{R}c{C}",
        grid_spec=pltpu.PrefetchScalarGridSpec(
            num_scalar_prefetch=1, grid=(N_CHIPS, R // tr),
            in_specs=[pl.BlockSpec((None, None, tr, C), lambda p, i, core_ref: (p, core_ref[0], i, 0)),
                      pl.BlockSpec((None, tr, C), lambda p, i, core_ref: (p, i, 0))],
            out_specs=pl.BlockSpec((None, tr, C), lambda p, i, core_ref: (p, i, 0)),
        ),
        out_shape=SDS((N_CHIPS, R, C), F32), compiler_params=_params("parallel", "parallel"),
    )(core, g4, r1)


def _place():
    x, y, c = lax.axis_index("x"), lax.axis_index("y"), lax.axis_index("c")
    chips = [(1 - x, y), (x, 1 - y), (1 - x, 1 - y)]
    return x, y, c, chips


def _remote(src, dst, send_sem, recv_sem, to):
    return pltpu.make_async_remote_copy(src_ref=src, dst_ref=dst, send_sem=send_sem, recv_sem=recv_sem, device_id=to, device_id_type=MESH)


def _all_gather8(v):
    R, N = v.shape

    def body(v_ref, out_ref, send_sems, recv_sems, local_sem):
        x, y, c, chips = _place()
        me, sibling = (x, y, c), (x, y, 1 - c)

        def slot(px, py, pc):
            return out_ref.at[4 * px + 2 * py + pc]

        def copy(k, block, to, src=None):
            return _remote(slot(*block) if src is None else src, slot(*block), send_sems.at[k], recv_sems.at[k], to)

        mine = pltpu.make_async_copy(v_ref, slot(*me), local_sem)
        mine.start()
        first = [copy(0, me, sibling, src=v_ref)] + [copy(1 + j, me, (*chip, c), src=v_ref) for j, chip in enumerate(chips)]
        for cp in first:
            cp.start()
        passed = [copy(4 + j, (*chip, c), sibling) for j, chip in enumerate(chips)]
        for j, chip in enumerate(chips):
            copy(1 + j, (*chip, c), me).wait_recv()
            passed[j].start()
        copy(0, sibling, me).wait_recv()
        for j, chip in enumerate(chips):
            copy(4 + j, (*chip, 1 - c), me).wait_recv()
        for cp in first + passed:
            cp.wait_send()
        mine.wait()

    return pl.pallas_call(
        body, name=f"all_gather8_r{R}n{N}", out_shape=SDS((N_DEV, R, N), v.dtype), in_specs=[ANY], out_specs=ANY,
        scratch_shapes=[pltpu.SemaphoreType.DMA((7,)), pltpu.SemaphoreType.DMA((7,)), pltpu.SemaphoreType.DMA],
    )(v)


def _gather_weights(shards):
    n = len(shards)

    def body(*refs):
        ins, outs = refs[:n], refs[n:2 * n]
        send_sems, recv_sems, local_sems = refs[2 * n:]
        x, y, c, chips = _place()
        q = 2 * x + y
        sibling = (x, y, 1 - c)
        local = [pltpu.make_async_copy(ins[k], outs[k].at[q], local_sems.at[k]) for k in range(n)]
        for cp in local:
            cp.start()
        first = []
        for k in range(n):
            for j, chip in enumerate(chips):
                first.append(_remote(ins[k].at[c], outs[k].at[q, c], send_sems.at[k, j], recv_sems.at[k, j], (*chip, c)))
                first[-1].start()
        passed = []
        for k in range(n):
            for j, chip in enumerate(chips):
                half = outs[k].at[2 * chip[0] + chip[1], c]
                _remote(half, half, send_sems.at[k, j], recv_sems.at[k, j], sibling).wait_recv()
                passed.append(_remote(half, half, send_sems.at[k, 3 + j], recv_sems.at[k, 3 + j], sibling))
                passed[-1].start()
        for k in range(n):
            for j, chip in enumerate(chips):
                half = outs[k].at[2 * chip[0] + chip[1], 1 - c]
                _remote(half, half, send_sems.at[k, 3 + j], recv_sems.at[k, 3 + j], sibling).wait_recv()
        for cp in first + passed:
            cp.wait_send()
        for cp in local:
            cp.wait()

    return pl.pallas_call(
        body, name="gather_weights", out_shape=[SDS((N_CHIPS,) + s.shape, s.dtype) for s in shards],
        in_specs=[ANY] * n, out_specs=[ANY] * n,
        scratch_shapes=[pltpu.SemaphoreType.DMA((n, 6)), pltpu.SemaphoreType.DMA((n, 6)), pltpu.SemaphoreType.DMA((n,))],
    )(*shards)


def _swap_halves(g4s):
    n = len(g4s)

    def body(*refs):
        ins, outs = refs[:n], refs[n:2 * n]
        send_sems, recv_sems = refs[2 * n:]
        x, y, c, _ = _place()
        sibling = (x, y, 1 - c)
        for k in range(n):
            for p in range(N_CHIPS):
                _remote(ins[k].at[p, 1 - c], outs[k].at[p], send_sems.at[k], recv_sems.at[k], sibling).start()
        for k in range(n):
            _remote(outs[k], outs[k], send_sems.at[k], recv_sems.at[k], sibling).wait()

    return pl.pallas_call(
        body, name="swap_halves", out_shape=[SDS((N_CHIPS,) + g.shape[2:], g.dtype) for g in g4s],
        in_specs=[ANY] * n, out_specs=[ANY] * n,
        scratch_shapes=[pltpu.SemaphoreType.DMA((n,)), pltpu.SemaphoreType.DMA((n,))],
    )(*g4s)


def _scatter_regions(hs):
    n = len(hs)

    def body(*refs):
        ins, outs = refs[:n], refs[n:2 * n]
        send_sems, recv_sems, local_sems = refs[2 * n:]
        x, y, c, chips = _place()
        q = 2 * x + y
        local = [pltpu.make_async_copy(ins[k].at[q], outs[k].at[q], local_sems.at[k]) for k in range(n)]
        for cp in local:
            cp.start()
        sent = []
        for k in range(n):
            for j, chip in enumerate(chips):
                sent.append(_remote(ins[k].at[2 * chip[0] + chip[1]], outs[k].at[q], send_sems.at[k, j], recv_sems.at[k, j], (*chip, c)))
                sent[-1].start()
        for k in range(n):
            for j, chip in enumerate(chips):
                got = outs[k].at[2 * chip[0] + chip[1]]
                _remote(got, got, send_sems.at[k, j], recv_sems.at[k, j], (x, y, c)).wait_recv()
        for cp in sent:
            cp.wait_send()
        for cp in local:
            cp.wait()

    return pl.pallas_call(
        body, name="scatter_regions", out_shape=[SDS(h.shape, h.dtype) for h in hs],
        in_specs=[ANY] * n, out_specs=[ANY] * n,
        scratch_shapes=[pltpu.SemaphoreType.DMA((n, 3)), pltpu.SemaphoreType.DMA((n, 3)), pltpu.SemaphoreType.DMA((n,))],
    )(*hs)


def _share_halves(ts):
    n = len(ts)

    def body(*refs):
        ins, outs = refs[:n], refs[n:2 * n]
        send_sems, recv_sems, local_sems = refs[2 * n:]
        x, y, c, _ = _place()
        sibling = (x, y, 1 - c)
        local = [pltpu.make_async_copy(ins[k], outs[k].at[c], local_sems.at[k]) for k in range(n)]
        sent = [_remote(ins[k], outs[k].at[c], send_sems.at[k], recv_sems.at[k], sibling) for k in range(n)]
        for cp in local + sent:
            cp.start()
        for k in range(n):
            got = outs[k].at[1 - c]
            _remote(got, got, send_sems.at[k], recv_sems.at[k], sibling).wait_recv()
        for cp in sent:
            cp.wait_send()
        for cp in local:
            cp.wait()

    return pl.pallas_call(
        body, name="share_halves", out_shape=[SDS((2,) + t.shape, t.dtype) for t in ts],
        in_specs=[ANY] * n, out_specs=[ANY] * n,
        scratch_shapes=[pltpu.SemaphoreType.DMA((n,)), pltpu.SemaphoreType.DMA((n,)), pltpu.SemaphoreType.DMA((n,))],
    )(*ts)


def _reduce_scatter(grads, core):
    g4s = [g.reshape(N_CHIPS, 2, g.shape[1] // 2, g.shape[2]) for g in grads]
    r1s = _swap_halves(g4s)
    hs = [_add_half(g4, r1, core) for g4, r1 in zip(g4s, r1s)]
    r2s = _scatter_regions(hs)
    ts = [_sum_leading(r2) for r2 in r2s]
    fins = _share_halves(ts)
    return [f.reshape(g.shape[1], g.shape[2]) for f, g in zip(fins, grads)]


def _pair_blocks(w):
    h, d, _ = w.shape
    z = jnp.zeros((h // 2, d, d), w.dtype)
    return jnp.concatenate([jnp.concatenate([w[0::2], z], axis=2), jnp.concatenate([z, w[1::2]], axis=2)], axis=1)


def _unpair_blocks(b):
    n, dd, _ = b.shape
    d = dd // 2
    return jnp.stack([b[:, :d, :d], b[:, d:, d:]], axis=1).reshape(2 * n, d, d)


def _pad_rows(a, rows):
    return jnp.pad(a, ((0, rows - a.shape[0]), (0, 0)))


class _Packer:
    def __init__(self, shapes, width=1024, row_multiple=64):
        self.shapes = shapes
        self.sizes = [math.prod(s) for s in shapes]
        total = sum(self.sizes)
        self.width = width
        self.rows = -(-total // (width * row_multiple)) * row_multiple
        self.pad = self.rows * width - total

    def pack(self, arrays):
        flat = jnp.concatenate([a.reshape(-1).astype(F32) for a in arrays] + [jnp.zeros((self.pad,), F32)])
        return flat.reshape(self.rows, self.width)

    def unpack(self, packed):
        flat = packed.reshape(-1)
        out, off = [], 0
        for s, n in zip(self.shapes, self.sizes):
            out.append(flat[off:off + n].reshape(s))
            off += n
        return out


SMALL = ["b_ada", "ffn1_norm", "mix_norm", "conv_w", "conv_b", "gate_a_w", "gate_a_b", "gate_x_w", "gate_x_b", "lru_lambda",
         "v_norm", "spatial_w", "spatial_b", "lru_out_norm", "gmlp_out_norm", "ffn2_norm", "final_norm"]
BIG = ["ffn1_w_gu", "ffn1_w_down", "w_in", "w_out", "ffn2_w_gu", "ffn2_w_down"]
WEIGHTS = ["w_ada", "b_ada", "ffn1_norm", "ffn1_w_gu", "ffn1_w_down", "mix_norm", "w_in", "conv_w", "conv_b", "gate_a_w", "gate_a_b",
           "gate_x_w", "gate_x_b", "lru_lambda", "v_norm", "spatial_w", "spatial_b", "lru_out_norm", "gmlp_out_norm", "w_out",
           "ffn2_norm", "ffn2_w_gu", "ffn2_w_down", "final_norm"]


def kernel(x, c, w_ada, b_ada, ffn1_norm, ffn1_w_gu, ffn1_w_down, mix_norm, w_in, conv_w, conv_b, gate_a_w, gate_a_b, gate_x_w, gate_x_b, lru_lambda, v_norm, spatial_w, spatial_b, lru_out_norm, gmlp_out_norm, w_out, ffn2_norm, ffn2_w_gu, ffn2_w_down, final_norm, loss_target, m_w_ada, m_b_ada, m_ffn1_norm, m_ffn1_w_gu, m_ffn1_w_down, m_mix_norm, m_w_in, m_conv_w, m_conv_b, m_gate_a_w, m_gate_a_b, m_gate_x_w, m_gate_x_b, m_lru_lambda, m_v_norm, m_spatial_w, m_spatial_b, m_lru_out_norm, m_gmlp_out_norm, m_w_out, m_ffn2_norm, m_ffn2_w_gu, m_ffn2_w_down, m_final_norm, v_w_ada, v_b_ada, v_ffn1_norm, v_ffn1_w_gu, v_ffn1_w_down, v_mix_norm, v_w_in, v_conv_w, v_conv_b, v_gate_a_w, v_gate_a_b, v_gate_x_w, v_gate_x_b, v_lru_lambda, v_v_norm, v_spatial_w, v_spatial_b, v_lru_out_norm, v_gmlp_out_norm, v_w_out, v_ffn2_norm, v_ffn2_w_gu, v_ffn2_w_down, v_final_norm):
    given = dict(locals())
    W = {n: given[n] for n in WEIGHTS}
    L = w_ada.shape[0]
    S, D = x.shape[1], x.shape[2]
    LW = conv_b.shape[1]
    hd = LW // HEADS
    xi, yi, ci = lax.axis_index("x"), lax.axis_index("y"), lax.axis_index("c")
    chip = 2 * xi + yi
    dev = 2 * chip + ci
    core = ci.astype(jnp.int32).reshape(1)
    xs = x.reshape(S, D)
    tgt = loss_target.reshape(S, D)

    c_all = _all_gather8(_pad_rows(c, SUBLANES))[:, 0, :]
    n_ada = w_ada.shape[2]
    b_shard = lax.dynamic_slice_in_dim(b_ada, chip * n_ada, n_ada, axis=1)
    mod_shard = _ada_fwd(_pad_rows(c_all, 2 * SUBLANES), w_ada, b_shard[:, None, :])
    mod_all = _all_gather8(mod_shard.reshape(L * 2 * SUBLANES, n_ada))
    mod_rows = lax.dynamic_index_in_dim(mod_all.reshape(N_CHIPS, 2, L, 2 * SUBLANES, n_ada)[:, 0], dev, axis=2, keepdims=False)
    mod = mod_rows.transpose(1, 0, 2).reshape(L, N_MOD, 1, D)

    full = dict(zip(BIG, _gather_weights([W[n].astype(MXU_DTYPE) for n in BIG])))

    def layer_weights(l):
        return dict(
            gu1=full["ffn1_w_gu"][:, l], d1=full["ffn1_w_down"][:, l].reshape(-1, D), win=full["w_in"][:, l],
            wout=full["w_out"][:, l].reshape(-1, D), gu2=full["ffn2_w_gu"][:, l], d2=full["ffn2_w_down"][:, l].reshape(-1, D))

    cws = LW // N_CHIPS
    conv_all = _all_gather8(_pad_rows(conv_w.reshape(L * CONV_WIDTH, cws), -(-L * CONV_WIDTH // SUBLANES) * SUBLANES))
    conv_full = conv_all.reshape(N_CHIPS, 2, -1, cws)[:, 0, :L * CONV_WIDTH].reshape(N_CHIPS, L, CONV_WIDTH, cws)
    conv_full = conv_full.transpose(1, 2, 0, 3).reshape(L, CONV_WIDTH, LW)

    tril = jnp.tril(jnp.ones((CHUNK, CHUNK), F32))
    seg = (jnp.arange(LW)[:, None] // hd == jnp.arange(LW)[None, :] // hd).astype(jnp.bfloat16)

    def mixer_params(l):
        ws = spatial_w[l] * tril
        wsp = jnp.concatenate([ws[0::2], ws[1::2]], axis=2)
        wa, wx = _pair_blocks(gate_a_w[l]), _pair_blocks(gate_x_w[l])
        return dict(
            cw=conv_full[l], cb=conv_b[l][None],
            wa=wa.astype(MXU_DTYPE), wx=wx.astype(MXU_DTYPE), wat=wa.transpose(0, 2, 1).astype(MXU_DTYPE), wxt=wx.transpose(0, 2, 1).astype(MXU_DTYPE),
            ba=gate_a_b[l].reshape(1, LW), bx=gate_x_b[l].reshape(1, LW), lam=lru_lambda[l][None], gv=v_norm[l][None],
            wsp=wsp.astype(MXU_DTYPE), wspt=wsp.transpose(0, 2, 1).astype(MXU_DTYPE),
            bfull=jnp.repeat(spatial_b[l].T, hd, axis=1), g_lru=lru_out_norm[l][None], g_gm=gmlp_out_norm[l][None])

    saved = []
    xcur = xs
    for l in range(L):
        lw, mp, md = layer_weights(l), mixer_params(l), mod[l]
        s = dict(lw=lw, mp=mp, md=md)
        s["x0"] = xcur
        s["h1"] = _modnorm(xcur, ffn1_norm[l][None], md[0], md[1])
        s["a1"], s["gu1"] = _ffn_up(s["h1"], lw["gu1"])
        s["f1"], xcur = _mm_res(s["a1"], lw["d1"], xcur, md[2], 0.5)
        s["x1"] = xcur
        s["h2"] = _modnorm(xcur, mix_norm[l][None], md[3], md[4])
        s["proj"] = _mm_chunks(s["h2"], lw["win"])
        s["ylru"] = _lru_fwd(s["proj"], mp["cw"], mp["cb"], mp["wa"], mp["ba"], mp["wx"], mp["bx"], mp["lam"])
        s["yn"], s["ygm"] = _gmlp_fwd(s["proj"], s["ylru"], mp["gv"], seg, mp["wsp"], mp["bfull"], mp["g_lru"], mp["g_gm"])
        s["f2"], xcur = _mm_res(s["yn"], lw["wout"], xcur, md[5], 1.0)
        s["x2"] = xcur
        s["h3"] = _modnorm(xcur, ffn2_norm[l][None], md[6], md[7])
        s["a3"], s["gu3"] = _ffn_up(s["h3"], lw["gu2"])
        s["f3"], xcur = _mm_res(s["a3"], lw["d2"], xcur, md[8], 0.5)
        saved.append(s)

    dx, dq, head_acc = _loss_head(xcur, tgt, final_norm[None], saved[-1]["md"][8], 0.5)
    loss = lax.psum(jnp.sum(head_acc[1]), ("x", "y", "c"))
    small_grads = {}
    big_grads = {n: [None] * L for n in BIG}
    dmods = [None] * L
    zero_row = jnp.zeros((1, D), F32)

    def ffn_bwd(dx, dq, x_in, h, a, gu, f, wgu, wd, gn, sc, next_gate, next_scale):
        d_wd = _mm_tn_chunks(a, dq[None], 1408, 512)[0]
        dgu = _ffn_bwd_act(dq, wd, gu)
        C = dgu.shape[3]
        dgu4 = dgu.reshape(N_CHIPS, S, C)
        d_wgu = _mm_tn_chunks(h, dgu4, 512, C)
        dh = _mm_nt_chunks(dgu4, wgu)
        dx, dq, acc = _norm_bwd(x_in, dh, dx, f, gn, sc, 0.5, next_gate, next_scale)
        return dx, dq, acc, d_wgu, d_wd.reshape(N_CHIPS, -1, D)

    for l in reversed(range(L)):
        s = saved[l]
        lw, mp, md = s["lw"], s["mp"], s["md"]
        dx, dq, acc3, big_grads["ffn2_w_gu"][l], big_grads["ffn2_w_down"][l] = ffn_bwd(
            dx, dq, s["x2"], s["h3"], s["a3"], s["gu3"], s["f3"], lw["gu2"], lw["d2"], ffn2_norm[l][None], md[7], md[5], 1.0)
        big_grads["w_out"][l] = _mm_tn_chunks(s["yn"], dq[None], 512, 512)[0].reshape(N_CHIPS, -1, D)
        dyn = _mm_nt_chunks(dq[None], lw["wout"][None])
        dylru, duv, dwsp, dbfull, gacc = _gmlp_bwd(s["proj"], s["ylru"], s["ygm"], dyn, mp["gv"], seg, mp["wsp"], mp["wspt"], mp["bfull"], mp["g_lru"], mp["g_gm"])
        dxg, dwa, dwx, lvec = _lru_bwd(s["proj"], dylru, mp["cw"], mp["cb"], mp["wa"], mp["ba"], mp["wx"], mp["bx"], mp["lam"], mp["wat"], mp["wxt"])
        dproj = jnp.concatenate([dxg, duv], axis=0)
        big_grads["w_in"][l] = _mm_tn_chunks(s["h2"], dproj, 512, LW)
        dh2 = _mm_nt_chunks(dproj, lw["win"])
        dx, dq, acc2 = _norm_bwd(s["x1"], dh2, dx, s["f2"], mix_norm[l][None], md[4], 1.0, md[2], 0.5)
        if l > 0:
            ng, ns = saved[l - 1]["md"][8], 0.5
        else:
            ng, ns = zero_row, 0.0
        dx, dq, acc1, big_grads["ffn1_w_gu"][l], big_grads["ffn1_w_down"][l] = ffn_bwd(
            dx, dq, s["x0"], s["h1"], s["a1"], s["gu1"], s["f1"], lw["gu1"], lw["d1"], ffn1_norm[l][None], md[1], ng, ns)

        dmods[l] = jnp.concatenate([acc1[0:2], acc1[3:4], acc2[0:2], acc2[3:4], acc3[0:2], acc3[3:4]], axis=0)
        dws = jnp.stack([dwsp[:, :, :CHUNK], dwsp[:, :, CHUNK:]], axis=1).reshape(HEADS, CHUNK, CHUNK) * tril
        lg = {"ffn1_norm": acc1[2], "mix_norm": acc2[2], "ffn2_norm": acc3[2],
              "conv_w": lvec[4:8], "conv_b": lvec[3], "gate_a_w": _unpair_blocks(dwa), "gate_a_b": lvec[0].reshape(HEADS, hd),
              "gate_x_w": _unpair_blocks(dwx), "gate_x_b": lvec[1].reshape(HEADS, hd), "lru_lambda": lvec[2], "v_norm": gacc[2],
              "spatial_w": dws, "spatial_b": dbfull.reshape(CHUNK, HEADS, hd).sum(-1).T, "lru_out_norm": gacc[0], "gmlp_out_norm": gacc[1]}
        for n, g in lg.items():
            small_grads.setdefault(n, [None] * L)[l] = g
    grad_x = dx.reshape(x.shape)

    per_layer = [n for n in SMALL if n not in ("b_ada", "final_norm")]
    part = [jnp.stack(small_grads[n]) for n in per_layer] + [head_acc[0], jnp.stack(dmods)]
    packer = _Packer([p.shape for p in part])
    gathered = _all_gather8(packer.pack(part))
    summed = packer.unpack(_sum_leading(gathered))
    grads = dict(zip(per_layer + ["final_norm"], summed[:-1]))
    grads["b_ada"] = summed[-1].reshape(L, N_MOD * D)
    off = sum(packer.sizes[:-1])
    dmod_rows = gathered.reshape(N_DEV, -1)[:, off:off + L * N_MOD * D].reshape(N_DEV, L, N_MOD * D)
    dmod_shard = lax.dynamic_slice_in_dim(dmod_rows, chip * n_ada, n_ada, axis=2).transpose(1, 0, 2)
    grads["w_ada"] = _ada_grad(c_all.T, dmod_shard)
    grads["conv_w"] = lax.dynamic_slice_in_dim(grads["conv_w"], chip * cws, cws, axis=2)

    order = [(n, l) for n in BIG for l in range(L)]
    reduced = dict(zip(order, _reduce_scatter([big_grads[n][l] for n, l in order], core)))
    for n in BIG:
        grads[n] = jnp.stack([reduced[n, l] for l in range(L)])

    delta, new_m, new_v = {}, {}, {}
    for n in BIG + ["w_ada"]:
        shp = W[n].shape
        d_, m_, v_ = _adamw(*[a.reshape(-1, shp[-1]) for a in (W[n], grads[n], given["m_" + n], given["v_" + n])])
        delta[n], new_m[n], new_v[n] = d_.reshape(shp), m_.reshape(shp), v_.reshape(shp)
    spk = _Packer([W[n].shape for n in SMALL])
    d_, m_, v_ = _adamw(spk.pack([W[n] for n in SMALL]), spk.pack([grads[n] for n in SMALL]),
                        spk.pack([given["m_" + n] for n in SMALL]), spk.pack([given["v_" + n] for n in SMALL]))
    for n, a, b, e in zip(SMALL, spk.unpack(d_), spk.unpack(m_), spk.unpack(v_)):
        delta[n], new_m[n], new_v[n] = a, b, e
    grads = {n: grads[n].reshape(W[n].shape) for n in WEIGHTS}
    return (loss, grad_x, *[grads[n] for n in WEIGHTS], *[delta[n] for n in WEIGHTS], *[new_m[n] for n in WEIGHTS], *[new_v[n] for n in WEIGHTS])
```

```python
import math

import jax
import jax.numpy as jnp
from jax import lax
from jax.experimental import pallas as pl
from jax.experimental.pallas import tpu as pltpu

F32 = jnp.float32
MXU_DTYPE = jnp.bfloat16
ACT_DTYPE = jnp.bfloat16
XFER_DTYPE = jnp.bfloat16
EPS = 1e-6
RG_LRU_C = 8.0
N_MOD = 9
CONV_WIDTH = 4
HEADS = 8
CHUNK = 128
LANES = 128
SUBLANES = 8
N_CHIPS = 4
N_DEV = 8
ADAM_LR, ADAM_B1, ADAM_B2, ADAM_EPS, ADAM_WD, ADAM_STEP = 0.001, 0.9, 0.999, 1e-08, 0.01, 10
VMEM_LIMIT_BYTES = 60 * 1024 * 1024
ROW_TILE_BYTES = 1 << 20
GELU_C = math.sqrt(2.0 / math.pi)
GELU_A = 0.044715

ANY = pl.BlockSpec(memory_space=pl.ANY)
MESH = pl.DeviceIdType.MESH
SDS = jax.ShapeDtypeStruct


def _params(*sem):
    return pltpu.CompilerParams(dimension_semantics=sem, vmem_limit_bytes=VMEM_LIMIT_BYTES)


def _dot(a, b):
    return jnp.dot(a.astype(MXU_DTYPE), b.astype(MXU_DTYPE), preferred_element_type=F32)


def _dot_nt(a, b):
    return lax.dot_general(a.astype(MXU_DTYPE), b.astype(MXU_DTYPE), (((1,), (1,)), ((), ())), preferred_element_type=F32)


def _dot_tn(a, b):
    return lax.dot_general(a.astype(MXU_DTYPE), b.astype(MXU_DTYPE), (((0,), (0,)), ((), ())), preferred_element_type=F32)


def _gelu(x):
    return x * (0.5 * (1.0 + jnp.tanh(GELU_C * (x + GELU_A * (x * x * x)))))


def _gelu_grad(x):
    t = jnp.tanh(GELU_C * (x + GELU_A * (x * x * x)))
    return 0.5 * (1.0 + t) + 0.5 * x * (1.0 - t * t) * (GELU_C * (1.0 + 3.0 * GELU_A * x * x))


def _sigmoid(x):
    return jax.nn.sigmoid(x)


def _rsqrt_ms(x):
    return lax.rsqrt(jnp.mean(x * x, axis=-1, keepdims=True) + EPS)


def _rowsum(x):
    return jnp.sum(x, axis=0, keepdims=True)


def _tile(n, want):
    t = min(n, want)
    assert n % t == 0, (n, want)
    return t


def _row_tile(rows, row_bytes):
    step = 2 * SUBLANES
    cap = max(step, ROW_TILE_BYTES // row_bytes)
    best = None
    for t in range(step, min(rows, cap) + 1, step):
        if rows % t == 0:
            best = t
    assert best is not None, (rows, row_bytes)
    return best


def _modnorm(x, gn, sh, sc):
    S, D = x.shape
    tm = _tile(S, 512)

    def body(x_ref, gn_ref, sh_ref, sc_ref, h_ref):
        xv = x_ref[...]
        h = (xv * _rsqrt_ms(xv) * gn_ref[...]) * (1.0 + sc_ref[...]) + sh_ref[...]
        h_ref[...] = h.astype(ACT_DTYPE)

    row = pl.BlockSpec((1, D), lambda i: (0, 0))
    return pl.pallas_call(
        body, name="modnorm", grid=(S // tm,),
        in_specs=[pl.BlockSpec((tm, D), lambda i: (i, 0)), row, row, row],
        out_specs=pl.BlockSpec((tm, D), lambda i: (i, 0)),
        out_shape=SDS((S, D), ACT_DTYPE), compiler_params=_params("parallel"),
    )(x, gn, sh, sc)


def _norm_bwd(x, dh, dxo, f, gn, sc, res_scale, next_gate, next_scale):
    S, D = x.shape
    tm = _tile(S, 256)

    def body(x_ref, dh_ref, dxo_ref, f_ref, gn_ref, sc_ref, ng_ref, dx_ref, dq_ref, acc_ref):
        @pl.when(pl.program_id(0) == 0)
        def _():
            acc_ref[...] = jnp.zeros_like(acc_ref)

        xv, dh, dxo = x_ref[...], dh_ref[...], dxo_ref[...]
        r = _rsqrt_ms(xv)
        xhat = xv * r
        gn = gn_ref[...]
        dn = dh * (1.0 + sc_ref[...])
        dxh = dn * gn
        dx = dxo + r * (dxh - xhat * jnp.mean(dxh * xhat, axis=-1, keepdims=True))
        dx_ref[...] = dx
        dq_ref[...] = ((next_scale * ng_ref[...]) * dx).astype(ACT_DTYPE)
        acc_ref[0:1, :] += _rowsum(dh)
        acc_ref[1:2, :] += _rowsum(dh * (xhat * gn))
        acc_ref[2:3, :] += _rowsum(dn * xhat)
        acc_ref[3:4, :] += _rowsum((res_scale * f_ref[...]) * dxo)

    tile = pl.BlockSpec((tm, D), lambda i: (i, 0))
    row = pl.BlockSpec((1, D), lambda i: (0, 0))
    return pl.pallas_call(
        body, name="norm_bwd", grid=(S // tm,),
        in_specs=[tile, tile, tile, tile, row, row, row],
        out_specs=[tile, tile, pl.BlockSpec((SUBLANES, D), lambda i: (0, 0))],
        out_shape=[SDS((S, D), F32), SDS((S, D), ACT_DTYPE), SDS((SUBLANES, D), F32)],
        compiler_params=_params("arbitrary"),
    )(x, dh, dxo, f, gn, sc, next_gate)


def _loss_head(x, target, gn, next_gate, next_scale):
    S, D = x.shape
    tm = _tile(S, 256)

    def body(x_ref, t_ref, gn_ref, ng_ref, dx_ref, dq_ref, acc_ref):
        @pl.when(pl.program_id(0) == 0)
        def _():
            acc_ref[...] = jnp.zeros_like(acc_ref)

        xv = x_ref[...]
        r = _rsqrt_ms(xv)
        xhat = xv * r
        gn = gn_ref[...]
        err = xhat * gn - t_ref[...]
        dy = err * (1.0 / D)
        dxh = dy * gn
        dx = r * (dxh - xhat * jnp.mean(dxh * xhat, axis=-1, keepdims=True))
        dx_ref[...] = dx
        dq_ref[...] = ((next_scale * ng_ref[...]) * dx).astype(ACT_DTYPE)
        acc_ref[0:1, :] += _rowsum(dy * xhat)
        acc_ref[1:2, :] += _rowsum(err * err) * (0.5 / D)

    tile = pl.BlockSpec((tm, D), lambda i: (i, 0))
    row = pl.BlockSpec((1, D), lambda i: (0, 0))
    return pl.pallas_call(
        body, name="loss_head", grid=(S // tm,),
        in_specs=[tile, tile, row, row],
        out_specs=[tile, tile, pl.BlockSpec((SUBLANES, D), lambda i: (0, 0))],
        out_shape=[SDS((S, D), F32), SDS((S, D), ACT_DTYPE), SDS((SUBLANES, D), F32)],
        compiler_params=_params("arbitrary"),
    )(x, target, gn, next_gate)


def _ffn_up(h, wgu):
    S, D = h.shape
    C = wgu.shape[2]
    tm = _tile(S, 256)

    def body(h_ref, wg_ref, wu_ref, a_ref, gu_ref):
        hv = h_ref[...]
        g = _dot(hv, wg_ref[...])
        u = _dot(hv, wu_ref[...])
        a_ref[...] = (g * _sigmoid(g) * u).astype(ACT_DTYPE)
        gu_ref[0] = g.astype(ACT_DTYPE)
        gu_ref[1] = u.astype(ACT_DTYPE)

    return pl.pallas_call(
        body, name="ffn_up", grid=(2, S // tm),
        in_specs=[
            pl.BlockSpec((tm, D), lambda j, i: (i, 0)),
            pl.BlockSpec((None, D, C), lambda j, i: (j, 0, 0)),
            pl.BlockSpec((None, D, C), lambda j, i: (2 + j, 0, 0)),
        ],
        out_specs=[
            pl.BlockSpec((tm, C), lambda j, i: (i, j)),
            pl.BlockSpec((2, None, tm, C), lambda j, i: (0, j, i, 0)),
        ],
        out_shape=[SDS((S, 2 * C), ACT_DTYPE), SDS((2, 2, S, C), ACT_DTYPE)],
        compiler_params=_params("parallel", "parallel"),
    )(h, wgu, wgu)


def _ffn_bwd_act(dq, wd, gu):
    S, D = dq.shape
    C = gu.shape[3]
    tm = _tile(S, 256)

    def body(dq_ref, wd_ref, gu_ref, dgu_ref):
        da = _dot_nt(dq_ref[...], wd_ref[...])
        g = gu_ref[0].astype(F32)
        u = gu_ref[1].astype(F32)
        s = _sigmoid(g)
        dgu_ref[0] = (da * u * (s * (1.0 + g * (1.0 - s)))).astype(ACT_DTYPE)
        dgu_ref[1] = (da * (g * s)).astype(ACT_DTYPE)

    gu_spec = pl.BlockSpec((2, None, tm, C), lambda j, i: (0, j, i, 0))
    return pl.pallas_call(
        body, name="ffn_bwd_act", grid=(2, S // tm),
        in_specs=[pl.BlockSpec((tm, D), lambda j, i: (i, 0)), pl.BlockSpec((C, D), lambda j, i: (j, 0)), gu_spec],
        out_specs=gu_spec,
        out_shape=SDS(gu.shape, ACT_DTYPE),
        compiler_params=_params("parallel", "parallel"),
    )(dq, wd, gu)


def _mm_res(a, w, x, gate, scale):
    S, K = a.shape
    D = w.shape[1]
    tm, tn = _tile(S, 512), _tile(D, 512)

    def body(a_ref, w_ref, x_ref, g_ref, f_ref, xo_ref):
        f = _dot(a_ref[...], w_ref[...])
        f_ref[...] = f
        xo_ref[...] = x_ref[...] + (scale * g_ref[...]) * f

    tile = pl.BlockSpec((tm, tn), lambda j, i: (i, j))
    return pl.pallas_call(
        body, name=f"mm_res_k{K}", grid=(D // tn, S // tm),
        in_specs=[pl.BlockSpec((tm, K), lambda j, i: (i, 0)), pl.BlockSpec((K, tn), lambda j, i: (0, j)), tile,
                  pl.BlockSpec((1, tn), lambda j, i: (0, j))],
        out_specs=[tile, tile],
        out_shape=[SDS((S, D), F32), SDS((S, D), F32)],
        compiler_params=_params("parallel", "parallel"),
    )(a, w, x, gate)


def _mm_chunks(h, wc):
    S, K = h.shape
    P, _, N = wc.shape
    tm = _tile(S, 512)

    def body(h_ref, w_ref, o_ref):
        o_ref[...] = _dot(h_ref[...], w_ref[...])

    return pl.pallas_call(
        body, name="mm_chunks", grid=(P, S // tm),
        in_specs=[pl.BlockSpec((tm, K), lambda j, i: (i, 0)), pl.BlockSpec((None, K, N), lambda j, i: (j, 0, 0))],
        out_specs=pl.BlockSpec((tm, N), lambda j, i: (i, j)),
        out_shape=SDS((S, P * N), F32),
        compiler_params=_params("parallel", "parallel"),
    )(h, wc)


def _mm_nt_chunks(ac, wc):
    P, S, K = ac.shape
    N = wc.shape[1]
    tm, tn = _tile(S, 256), _tile(N, 512)

    def body(a_ref, w_ref, o_ref):
        acc = _dot_nt(a_ref[0], w_ref[0])
        for p in range(1, P):
            acc += _dot_nt(a_ref[p], w_ref[p])
        o_ref[...] = acc

    return pl.pallas_call(
        body, name=f"mm_nt_p{P}k{K}", grid=(S // tm, N // tn),
        in_specs=[pl.BlockSpec((P, tm, K), lambda i, j: (0, i, 0)), pl.BlockSpec((P, tn, K), lambda i, j: (0, j, 0))],
        out_specs=pl.BlockSpec((tm, tn), lambda i, j: (i, j)),
        out_shape=SDS((S, N), F32),
        compiler_params=_params("parallel", "parallel"),
    )(ac, wc)


def _mm_tn_chunks(a, bc, tile_m, tile_n):
    S, M = a.shape
    P, _, N = bc.shape
    ts, tm, tn = _tile(S, 512), _tile(M, tile_m), _tile(N, tile_n)

    def body(a_ref, b_ref, o_ref):
        @pl.when(pl.program_id(3) == 0)
        def _():
            o_ref[...] = jnp.zeros_like(o_ref)

        o_ref[...] += _dot_tn(a_ref[...], b_ref[...])

    return pl.pallas_call(
        body, name=f"mm_tn_m{M}n{N}", grid=(P, M // tm, N // tn, S // ts),
        in_specs=[pl.BlockSpec((ts, tm), lambda p, m, n, k: (k, m)), pl.BlockSpec((None, ts, tn), lambda p, m, n, k: (p, k, n))],
        out_specs=pl.BlockSpec((None, tm, tn), lambda p, m, n, k: (p, m, n)),
        out_shape=SDS((P, M, N), F32),
        compiler_params=_params("parallel", "parallel", "parallel", "arbitrary"),
    )(a, bc)


def _shift_down(x, s, row, fill):
    return jnp.where(row >= s, pltpu.roll(x, s, 0), fill)


def _shift_up(x, s, row, fill):
    n = x.shape[0]
    return jnp.where(row < n - s, pltpu.roll(x, n - s, 0), fill)


def _scan_down(a, b, row):
    s = 1
    while s < a.shape[0]:
        b = a * _shift_down(b, s, row, 0.0) + b
        a = a * _shift_down(a, s, row, 1.0)
        s *= 2
    return b


def _scan_up(a, b, row):
    s = 1
    while s < a.shape[0]:
        b = a * _shift_up(b, s, row, 0.0) + b
        a = a * _shift_up(a, s, row, 1.0)
        s *= 2
    return b


def _conv(xl, cw_ref, cb_ref, row):
    y = cb_ref[...] + _shift_down(xl, 3, row, 0.0) * cw_ref[0:1, :]
    y = y + _shift_down(xl, 2, row, 0.0) * cw_ref[1:2, :]
    y = y + _shift_down(xl, 1, row, 0.0) * cw_ref[2:3, :]
    return y + xl * cw_ref[3:4, :]


def _lru_gates(xc, wa_ref, ba_ref, wx_ref, bx_ref, lam_ref):
    ra = _sigmoid(_dot(xc, wa_ref[...]) + ba_ref[...])
    ri = _sigmoid(_dot(xc, wx_ref[...]) + bx_ref[...])
    ls = jax.nn.log_sigmoid(lam_ref[...])
    a = jnp.exp((RG_LRU_C * ra) * ls)
    mult = jnp.sqrt(1.0 - a * a)
    return ra, ri, ls, a, mult


def _lru_specs(S):
    col = lambda off: pl.BlockSpec((S, LANES), lambda j: (0, off + j))
    vec = pl.BlockSpec((1, LANES), lambda j: (0, j))
    blk = pl.BlockSpec((None, LANES, LANES), lambda j: (j, 0, 0))
    cw = pl.BlockSpec((CONV_WIDTH, LANES), lambda j: (0, j))
    return col, vec, blk, cw


def _lru_fwd(proj, cw, cb, wa, ba, wx, bx, lam):
    S = proj.shape[0]
    W = cb.shape[1]
    nb = W // LANES

    def body(xl_ref, gl_ref, cw_ref, cb_ref, wa_ref, ba_ref, wx_ref, bx_ref, lam_ref, y_ref):
        row = lax.broadcasted_iota(jnp.int32, (S, LANES), 0)
        xc = _conv(xl_ref[...], cw_ref, cb_ref, row)
        _, ri, _, a, mult = _lru_gates(xc, wa_ref, ba_ref, wx_ref, bx_ref, lam_ref)
        h = _scan_down(a, mult * (ri * xc), row)
        y_ref[...] = h * _gelu(gl_ref[...])

    col, vec, blk, cws = _lru_specs(S)
    return pl.pallas_call(
        body, name="lru_fwd", grid=(nb,),
        in_specs=[col(0), col(nb), cws, vec, blk, vec, blk, vec, vec],
        out_specs=pl.BlockSpec((S, LANES), lambda j: (0, j)),
        out_shape=SDS((S, W), F32), compiler_params=_params("parallel"),
    )(proj, proj, cw, cb, wa, ba, wx, bx, lam)


def _lru_bwd(proj, dy, cw, cb, wa, ba, wx, bx, lam, wat, wxt):
    S = proj.shape[0]
    W = cb.shape[1]
    nb = W // LANES

    def body(xl_ref, gl_ref, dy_ref, cw_ref, cb_ref, wa_ref, ba_ref, wx_ref, bx_ref, lam_ref, wat_ref, wxt_ref,
             dp_ref, dwa_ref, dwx_ref, vec_ref):
        row = lax.broadcasted_iota(jnp.int32, (S, LANES), 0)
        xl = xl_ref[...]
        xc = _conv(xl, cw_ref, cb_ref, row)
        ra, ri, ls, a, mult = _lru_gates(xc, wa_ref, ba_ref, wx_ref, bx_ref, lam_ref)
        h = _scan_down(a, mult * (ri * xc), row)
        gl = gl_ref[...]
        dyv = dy_ref[...]
        dp_ref[1] = (dyv * h * _gelu_grad(gl)).astype(ACT_DTYPE)
        adj = _scan_up(_shift_up(a, 1, row, 0.0), dyv * _gelu(gl), row)
        da = adj * _shift_down(h, 1, row, 0.0)
        dmult = adj * (ri * xc)
        dlog_a = da * a - dmult * (a * a) / mult
        dra = dlog_a * (RG_LRU_C * ls)
        dpa = dra * ra * (1.0 - ra)
        dpi = (adj * mult * xc) * ri * (1.0 - ri)
        dxc = adj * mult * ri + _dot(dpa, wat_ref[...]) + _dot(dpi, wxt_ref[...])
        dwa_ref[...] = _dot_tn(xc, dpa)
        dwx_ref[...] = _dot_tn(xc, dpi)
        dxl = dxc * cw_ref[3:4, :]
        dxl = dxl + _shift_up(dxc, 1, row, 0.0) * cw_ref[2:3, :]
        dxl = dxl + _shift_up(dxc, 2, row, 0.0) * cw_ref[1:2, :]
        dxl = dxl + _shift_up(dxc, 3, row, 0.0) * cw_ref[0:1, :]
        dp_ref[0] = dxl.astype(ACT_DTYPE)
        vec_ref[...] = jnp.zeros_like(vec_ref)
        vec_ref[0:1, :] = _rowsum(dpa)
        vec_ref[1:2, :] = _rowsum(dpi)
        vec_ref[2:3, :] = _rowsum(dlog_a * (RG_LRU_C * ra)) * _sigmoid(-lam_ref[...])
        vec_ref[3:4, :] = _rowsum(dxc)
        vec_ref[4:5, :] = _rowsum(dxc * _shift_down(xl, 3, row, 0.0))
        vec_ref[5:6, :] = _rowsum(dxc * _shift_down(xl, 2, row, 0.0))
        vec_ref[6:7, :] = _rowsum(dxc * _shift_down(xl, 1, row, 0.0))
        vec_ref[7:8, :] = _rowsum(dxc * xl)

    col, vec, blk, cws = _lru_specs(S)
    return pl.pallas_call(
        body, name="lru_bwd", grid=(nb,),
        in_specs=[col(0), col(nb), col(0), cws, vec, blk, vec, blk, vec, vec, blk, blk],
        out_specs=[pl.BlockSpec((2, S, LANES), lambda j: (0, 0, j)), blk, blk, pl.BlockSpec((2 * SUBLANES, LANES), lambda j: (0, j))],
        out_shape=[SDS((2, S, W), ACT_DTYPE), SDS((nb, LANES, LANES), F32), SDS((nb, LANES, LANES), F32), SDS((2 * SUBLANES, W), F32)],
        compiler_params=_params("parallel"),
    )(proj, proj, dy, cw, cb, wa, ba, wx, bx, lam, wat, wxt)


def _seg_mean(x, seg_ref, width):
    hi = x.astype(jnp.bfloat16)
    lo = (x - hi.astype(F32)).astype(jnp.bfloat16)
    ones = seg_ref[...]
    s = jnp.dot(hi, ones, preferred_element_type=F32) + jnp.dot(lo, ones, preferred_element_type=F32)
    return s * (1.0 / width)


def _gmlp_core(u_ref, v_ref, gv_ref, seg_ref, ws_ref, bfull_ref, z_scr, hd):
    tm, W = u_ref.shape
    lane = lax.broadcasted_iota(jnp.int32, (CHUNK, LANES), 1)
    ug = _gelu(u_ref[...])
    vg = _gelu(v_ref[...])
    cen = vg - _seg_mean(vg, seg_ref, hd)
    rstd = lax.rsqrt(_seg_mean(cen * cen, seg_ref, hd) + EPS)
    vhat = cen * rstd
    vh = vhat * gv_ref[...]
    vcats = {}
    for ci in range(tm // CHUNK):
        for p in range(W // LANES):
            blk = vh[ci * CHUNK:(ci + 1) * CHUNK, p * LANES:(p + 1) * LANES]
            vcat = jnp.concatenate([jnp.where(lane < hd, blk, 0.0), jnp.where(lane >= hd, blk, 0.0)], axis=0).astype(MXU_DTYPE)
            vcats[ci, p] = vcat
            z_scr[ci * CHUNK:(ci + 1) * CHUNK, p * LANES:(p + 1) * LANES] = (
                jnp.dot(ws_ref[p], vcat, preferred_element_type=F32) + bfull_ref[:, p * LANES:(p + 1) * LANES])
    return ug, vhat, rstd, vcats


def _gmlp_specs(tm, W, nb):
    rows = lambda off: pl.BlockSpec((tm, W), lambda i: (i, off))
    vec = pl.BlockSpec((1, W), lambda i: (0, 0))
    seg = pl.BlockSpec((W, W), lambda i: (0, 0))
    wsp = pl.BlockSpec((nb, CHUNK, 2 * CHUNK), lambda i: (0, 0, 0))
    bfull = pl.BlockSpec((CHUNK, W), lambda i: (0, 0))
    return rows, vec, seg, wsp, bfull


def _gmlp_fwd(proj, ylru, gv, seg, wsp, bfull, g_lru, g_gm):
    S, W = ylru.shape
    nb = W // LANES
    hd = W // HEADS
    tm = _tile(S, 512)

    def body(u_ref, v_ref, yl_ref, gv_ref, seg_ref, ws_ref, bfull_ref, gl_ref, gg_ref, yn_ref, ygm_ref, z_scr):
        ug, _, _, _ = _gmlp_core(u_ref, v_ref, gv_ref, seg_ref, ws_ref, bfull_ref, z_scr, hd)
        ygm = ug * z_scr[...]
        ygm_ref[...] = ygm
        yl = yl_ref[...]
        yn_ref[:, 0:W] = (yl * _rsqrt_ms(yl) * gl_ref[...]).astype(ACT_DTYPE)
        yn_ref[:, W:2 * W] = (ygm * _rsqrt_ms(ygm) * gg_ref[...]).astype(ACT_DTYPE)

    rows, vec, segs, wsps, bfulls = _gmlp_specs(tm, W, nb)
    return pl.pallas_call(
        body, name="gmlp_fwd", grid=(S // tm,),
        in_specs=[rows(2), rows(3), rows(0), vec, segs, wsps, bfulls, vec, vec],
        out_specs=[pl.BlockSpec((tm, 2 * W), lambda i: (i, 0)), rows(0)],
        out_shape=[SDS((S, 2 * W), ACT_DTYPE), SDS((S, W), F32)],
        scratch_shapes=[pltpu.VMEM((tm, W), F32)],
        compiler_params=_params("parallel"),
    )(proj, proj, ylru, gv, seg, wsp, bfull, g_lru, g_gm)


def _rms_bwd(y, g, dyn):
    r = _rsqrt_ms(y)
    yhat = y * r
    dyh = dyn * g
    return r * (dyh - yhat * jnp.mean(dyh * yhat, axis=-1, keepdims=True)), _rowsum(dyn * yhat)


def _gmlp_bwd(proj, ylru, ygm, dyn, gv, seg, wsp, wspt, bfull, g_lru, g_gm):
    S, W = ylru.shape
    nb = W // LANES
    hd = W // HEADS
    tm = _tile(S, 256)

    def body(u_ref, v_ref, yl_ref, ygm_ref, dl_ref, dg_ref, gv_ref, seg_ref, ws_ref, wst_ref, bfull_ref, gl_ref, gg_ref,
             dyl_ref, duv_ref, dws_ref, dbf_ref, acc_ref, z_scr, dvh_scr):
        @pl.when(pl.program_id(0) == 0)
        def _():
            dws_ref[...] = jnp.zeros_like(dws_ref)
            dbf_ref[...] = jnp.zeros_like(dbf_ref)
            acc_ref[...] = jnp.zeros_like(acc_ref)

        dyl, dgl = _rms_bwd(yl_ref[...], gl_ref[...], dl_ref[...])
        dyl_ref[...] = dyl
        dygm, dgg = _rms_bwd(ygm_ref[...], gg_ref[...], dg_ref[...])
        ug, vhat, rstd, vcats = _gmlp_core(u_ref, v_ref, gv_ref, seg_ref, ws_ref, bfull_ref, z_scr, hd)
        duv_ref[0] = (dygm * z_scr[...] * _gelu_grad(u_ref[...])).astype(ACT_DTYPE)
        dz = dygm * ug
        lane = lax.broadcasted_iota(jnp.int32, (CHUNK, LANES), 1)
        dbf = dz[0:CHUNK, :]
        for ci in range(1, tm // CHUNK):
            dbf += dz[ci * CHUNK:(ci + 1) * CHUNK, :]
        dbf_ref[...] += dbf
        for ci in range(tm // CHUNK):
            for p in range(nb):
                dzb = dz[ci * CHUNK:(ci + 1) * CHUNK, p * LANES:(p + 1) * LANES].astype(MXU_DTYPE)
                dws_ref[p] += _dot_nt(dzb, vcats[ci, p])
                dvc = jnp.dot(wst_ref[p], dzb, preferred_element_type=F32)
                dvh_scr[ci * CHUNK:(ci + 1) * CHUNK, p * LANES:(p + 1) * LANES] = jnp.where(lane < hd, dvc[0:CHUNK], dvc[CHUNK:2 * CHUNK])
        dvh = dvh_scr[...]
        dvn = dvh * gv_ref[...]
        dvg = rstd * (dvn - _seg_mean(dvn, seg_ref, hd) - vhat * _seg_mean(dvn * vhat, seg_ref, hd))
        duv_ref[1] = (dvg * _gelu_grad(v_ref[...])).astype(ACT_DTYPE)
        acc_ref[0:1, :] += dgl
        acc_ref[1:2, :] += dgg
        acc_ref[2:3, :] += _rowsum(dvh * vhat)

    rows, vec, segs, wsps, bfulls = _gmlp_specs(tm, W, nb)
    wspt_spec = pl.BlockSpec((nb, 2 * CHUNK, CHUNK), lambda i: (0, 0, 0))
    return pl.pallas_call(
        body, name="gmlp_bwd", grid=(S // tm,),
        in_specs=[rows(2), rows(3), rows(0), rows(0), rows(0), rows(1), vec, segs, wsps, wspt_spec, bfulls, vec, vec],
        out_specs=[rows(0), pl.BlockSpec((2, tm, W), lambda i: (0, i, 0)), wsps, bfulls, pl.BlockSpec((SUBLANES, W), lambda i: (0, 0))],
        out_shape=[SDS((S, W), F32), SDS((2, S, W), ACT_DTYPE), SDS((nb, CHUNK, 2 * CHUNK), F32), SDS((CHUNK, W), F32), SDS((SUBLANES, W), F32)],
        scratch_shapes=[pltpu.VMEM((tm, W), F32), pltpu.VMEM((tm, W), F32)],
        compiler_params=_params("arbitrary"),
    )(proj, proj, ylru, ygm, dyn, dyn, gv, seg, wsp, wspt, bfull, g_lru, g_gm)


def _ada_fwd(c_all, w_ada, b_shard):
    L, D, N = w_ada.shape
    R = c_all.shape[0]
    tn = N // 2

    def body(c_ref, w_ref, b_ref, o_ref):
        cv = c_ref[...]
        o_ref[...] = _dot(cv * _sigmoid(cv), w_ref[...]) + b_ref[...]

    return pl.pallas_call(
        body, name="ada_fwd", grid=(L, N // tn),
        in_specs=[pl.BlockSpec((R, D), lambda l, j: (0, 0)), pl.BlockSpec((None, D, tn), lambda l, j: (l, 0, j)),
                  pl.BlockSpec((None, 1, tn), lambda l, j: (l, 0, j))],
        out_specs=pl.BlockSpec((None, R, tn), lambda l, j: (l, 0, j)),
        out_shape=SDS((L, R, N), F32), compiler_params=_params("parallel", "parallel"),
    )(c_all, w_ada, b_shard)


def _ada_grad(c_all_t, dmod):
    D, B = c_all_t.shape
    L, _, N = dmod.shape
    tn = N // 2

    def body(c_ref, d_ref, o_ref):
        cv = c_ref[...]
        sc = cv * _sigmoid(cv)
        acc = sc[:, 0:1] * d_ref[0:1, :]
        for b in range(1, B):
            acc += sc[:, b:b + 1] * d_ref[b:b + 1, :]
        o_ref[...] = acc

    return pl.pallas_call(
        body, name="ada_grad", grid=(L, N // tn),
        in_specs=[pl.BlockSpec((D, B), lambda l, j: (0, 0)), pl.BlockSpec((None, B, tn), lambda l, j: (l, 0, j))],
        out_specs=pl.BlockSpec((None, D, tn), lambda l, j: (l, 0, j)),
        out_shape=SDS((L, D, N), F32), compiler_params=_params("parallel", "parallel"),
    )(c_all_t, dmod)


def _adamw(w, g, m, v):
    R, C = w.shape
    tr = _row_tile(R, C * 4)

    def body(w_ref, g_ref, m_ref, v_ref, d_ref, mo_ref, vo_ref):
        gv = g_ref[...]
        mn = ADAM_B1 * m_ref[...] + (1.0 - ADAM_B1) * gv
        vn = ADAM_B2 * v_ref[...] + (1.0 - ADAM_B2) * (gv * gv)
        mo_ref[...] = mn
        vo_ref[...] = vn
        m_hat = mn / (1.0 - ADAM_B1 ** ADAM_STEP)
        v_hat = vn / (1.0 - ADAM_B2 ** ADAM_STEP)
        d_ref[...] = -ADAM_LR * (m_hat / (jnp.sqrt(v_hat) + ADAM_EPS) + ADAM_WD * w_ref[...])

    tile = pl.BlockSpec((tr, C), lambda i: (i, 0))
    return pl.pallas_call(
        body, name=f"adamw_r{R}c{C}", grid=(R // tr,), in_specs=[tile] * 4, out_specs=[tile] * 3,
        out_shape=[SDS((R, C), F32)] * 3, compiler_params=_params("parallel"),
    )(w, g, m, v)


def _sum_leading(a):
    P, R, C = a.shape
    tr = _row_tile(R, P * C * 4)

    def body(a_ref, o_ref):
        acc = a_ref[0]
        for p in range(1, P):
            acc = acc + a_ref[p]
        o_ref[...] = acc

    return pl.pallas_call(
        body, name=f"sum{P}_r{R}c{C}", grid=(R // tr,),
        in_specs=[pl.BlockSpec((P, tr, C), lambda i: (0, i, 0))],
        out_specs=pl.BlockSpec((tr, C), lambda i: (i, 0)),
        out_shape=SDS((R, C), F32), compiler_params=_params("parallel"),
    )(a)


def _add_half(g4, r1, place):
    _, _, R, C = g4.shape
    tr = _row_tile(R, C * 4)

    def body(place_ref, g_ref, r_ref, h_ref, own_ref):
        s = (g_ref[...] + r_ref[...]).astype(XFER_DTYPE)
        h_ref[...] = s

        @pl.when(pl.program_id(1) == place_ref[1])
        def _():
            own_ref[...] = s

    return pl.pallas_call(
        body, name=f"add_half_r{R}c{C}",
        grid_spec=pltpu.PrefetchScalarGridSpec(
            num_scalar_prefetch=1, grid=(R // tr, N_CHIPS),
            in_specs=[pl.BlockSpec((None, None, tr, C), lambda i, p, place_ref: (p, place_ref[0], i, 0)),
                      pl.BlockSpec((None, tr, C), lambda i, p, place_ref: (p, i, 0))],
            out_specs=[pl.BlockSpec((None, tr, C), lambda i, p, place_ref: (p, i, 0)),
                       pl.BlockSpec((None, tr, C), lambda i, p, place_ref: (place_ref[1], i, 0))],
        ),
        out_shape=[SDS((N_CHIPS, R, C), XFER_DTYPE)] * 2, compiler_params=_params("parallel", "arbitrary"),
    )(place, g4, r1)


def _sum4_into_half(r2, place):
    P, R, C = r2.shape
    tr = _row_tile(R, P * C * 4)

    def body(place_ref, a_ref, o_ref):
        acc = a_ref[0].astype(F32)
        for p in range(1, P):
            acc = acc + a_ref[p].astype(F32)
        o_ref[...] = acc

    return pl.pallas_call(
        body, name=f"sum4_r{R}c{C}",
        grid_spec=pltpu.PrefetchScalarGridSpec(
            num_scalar_prefetch=1, grid=(R // tr,),
            in_specs=[pl.BlockSpec((P, tr, C), lambda i, place_ref: (0, i, 0))],
            out_specs=pl.BlockSpec((None, tr, C), lambda i, place_ref: (place_ref[0], i, 0)),
        ),
        out_shape=SDS((2, R, C), F32), compiler_params=_params("parallel"),
    )(place, r2)


def _cast_into_slot(w, place):
    L, R, C = w.shape
    tr = _row_tile(R, C * 4)

    def body(place_ref, w_ref, o_ref):
        o_ref[...] = w_ref[...].astype(MXU_DTYPE)

    return pl.pallas_call(
        body, name=f"cast_r{R}c{C}",
        grid_spec=pltpu.PrefetchScalarGridSpec(
            num_scalar_prefetch=1, grid=(L, R // tr),
            in_specs=[pl.BlockSpec((None, tr, C), lambda l, i, place_ref: (l, i, 0))],
            out_specs=pl.BlockSpec((None, None, tr, C), lambda l, i, place_ref: (place_ref[1], l, i, 0)),
        ),
        out_shape=SDS((N_CHIPS, L, R, C), MXU_DTYPE), compiler_params=_params("parallel", "parallel"),
    )(place, w)


def _place():
    x, y, c = lax.axis_index("x"), lax.axis_index("y"), lax.axis_index("c")
    chips = [(1 - x, y), (x, 1 - y), (1 - x, 1 - y)]
    return x, y, c, chips


def _remote(src, dst, send_sem, recv_sem, to):
    return pltpu.make_async_remote_copy(src_ref=src, dst_ref=dst, send_sem=send_sem, recv_sem=recv_sem, device_id=to, device_id_type=MESH)


def _all_gather8(v):
    R, N = v.shape

    def body(v_ref, out_ref, send_sems, recv_sems, local_sem):
        x, y, c, chips = _place()
        me, sibling = (x, y, c), (x, y, 1 - c)

        def slot(px, py, pc):
            return out_ref.at[4 * px + 2 * py + pc]

        def copy(k, block, to, src=None):
            return _remote(slot(*block) if src is None else src, slot(*block), send_sems.at[k], recv_sems.at[k], to)

        mine = pltpu.make_async_copy(v_ref, slot(*me), local_sem)
        mine.start()
        first = [copy(0, me, sibling, src=v_ref)] + [copy(1 + j, me, (*chip, c), src=v_ref) for j, chip in enumerate(chips)]
        for cp in first:
            cp.start()
        passed = [copy(4 + j, (*chip, c), sibling) for j, chip in enumerate(chips)]
        for j, chip in enumerate(chips):
            copy(1 + j, (*chip, c), me).wait_recv()
            passed[j].start()
        copy(0, sibling, me).wait_recv()
        for j, chip in enumerate(chips):
            copy(4 + j, (*chip, 1 - c), me).wait_recv()
        for cp in first + passed:
            cp.wait_send()
        mine.wait()

    return pl.pallas_call(
        body, name=f"all_gather8_r{R}n{N}", out_shape=SDS((N_DEV, R, N), v.dtype), in_specs=[ANY], out_specs=ANY,
        scratch_shapes=[pltpu.SemaphoreType.DMA((7,)), pltpu.SemaphoreType.DMA((7,)), pltpu.SemaphoreType.DMA],
    )(v)


def _gather_weights(slots):
    n = len(slots)

    def body(*refs):
        ins, outs = refs[:n], refs[n:2 * n]
        send_sems, recv_sems = refs[2 * n:]
        x, y, c, chips = _place()
        q = 2 * x + y
        sibling = (x, y, 1 - c)
        first = []
        for k in range(n):
            for j, chip in enumerate(chips):
                first.append(_remote(ins[k].at[q, c], outs[k].at[q, c], send_sems.at[k, j], recv_sems.at[k, j], (*chip, c)))
                first[-1].start()
        passed = []
        for k in range(n):
            for j, chip in enumerate(chips):
                half = outs[k].at[2 * chip[0] + chip[1], c]
                _remote(half, half, send_sems.at[k, j], recv_sems.at[k, j], sibling).wait_recv()
                passed.append(_remote(half, half, send_sems.at[k, 3 + j], recv_sems.at[k, 3 + j], sibling))
                passed[-1].start()
        for k in range(n):
            for j, chip in enumerate(chips):
                half = outs[k].at[2 * chip[0] + chip[1], 1 - c]
                _remote(half, half, send_sems.at[k, 3 + j], recv_sems.at[k, 3 + j], sibling).wait_recv()
        for cp in first + passed:
            cp.wait_send()

    return pl.pallas_call(
        body, name="gather_weights", out_shape=[SDS(s.shape, s.dtype) for s in slots],
        in_specs=[ANY] * n, out_specs=[ANY] * n, input_output_aliases={k: k for k in range(n)},
        scratch_shapes=[pltpu.SemaphoreType.DMA((n, 6)), pltpu.SemaphoreType.DMA((n, 6))],
    )(*slots)


def _swap_halves(g4s):
    n = len(g4s)

    def body(*refs):
        ins, outs = refs[:n], refs[n:2 * n]
        send_sems, recv_sems = refs[2 * n:]
        x, y, c, _ = _place()
        sibling = (x, y, 1 - c)
        for k in range(n):
            for p in range(N_CHIPS):
                _remote(ins[k].at[p, 1 - c], outs[k].at[p], send_sems.at[k], recv_sems.at[k], sibling).start()
        for k in range(n):
            _remote(outs[k], outs[k], send_sems.at[k], recv_sems.at[k], sibling).wait()

    return pl.pallas_call(
        body, name="swap_halves", out_shape=[SDS((N_CHIPS,) + g.shape[2:], g.dtype) for g in g4s],
        in_specs=[ANY] * n, out_specs=[ANY] * n,
        scratch_shapes=[pltpu.SemaphoreType.DMA((n,)), pltpu.SemaphoreType.DMA((n,))],
    )(*g4s)


def _scatter_regions(hs, lands):
    n = len(hs)

    def body(*refs):
        ins, outs = refs[:n], refs[2 * n:3 * n]
        send_sems, recv_sems = refs[3 * n:]
        x, y, c, chips = _place()
        q = 2 * x + y
        sent = []
        for k in range(n):
            for j, chip in enumerate(chips):
                sent.append(_remote(ins[k].at[2 * chip[0] + chip[1]], outs[k].at[q], send_sems.at[k, j], recv_sems.at[k, j], (*chip, c)))
                sent[-1].start()
        for k in range(n):
            for j, chip in enumerate(chips):
                got = outs[k].at[2 * chip[0] + chip[1]]
                _remote(got, got, send_sems.at[k, j], recv_sems.at[k, j], (x, y, c)).wait_recv()
        for cp in sent:
            cp.wait_send()

    return pl.pallas_call(
        body, name="scatter_regions", out_shape=[SDS(h.shape, h.dtype) for h in lands],
        in_specs=[ANY] * (2 * n), out_specs=[ANY] * n, input_output_aliases={n + k: k for k in range(n)},
        scratch_shapes=[pltpu.SemaphoreType.DMA((n, 3)), pltpu.SemaphoreType.DMA((n, 3))],
    )(*hs, *lands)


def _share_halves(fins):
    n = len(fins)

    def body(*refs):
        ins, outs = refs[:n], refs[n:2 * n]
        send_sems, recv_sems = refs[2 * n:]
        x, y, c, _ = _place()
        sibling = (x, y, 1 - c)
        sent = [_remote(ins[k].at[c], outs[k].at[c], send_sems.at[k], recv_sems.at[k], sibling) for k in range(n)]
        for cp in sent:
            cp.start()
        for k in range(n):
            got = outs[k].at[1 - c]
            _remote(got, got, send_sems.at[k], recv_sems.at[k], sibling).wait_recv()
        for cp in sent:
            cp.wait_send()

    return pl.pallas_call(
        body, name="share_halves", out_shape=[SDS(t.shape, t.dtype) for t in fins],
        in_specs=[ANY] * n, out_specs=[ANY] * n, input_output_aliases={k: k for k in range(n)},
        scratch_shapes=[pltpu.SemaphoreType.DMA((n,)), pltpu.SemaphoreType.DMA((n,))],
    )(*fins)


def _reduce_scatter(grads, place):
    g4s = [g.reshape(N_CHIPS, 2, g.shape[1] // 2, g.shape[2]) for g in grads]
    r1s = _swap_halves(g4s)
    pairs = [_add_half(g4, r1, place) for g4, r1 in zip(g4s, r1s)]
    r2s = _scatter_regions([h for h, _ in pairs], [own for _, own in pairs])
    fins = _share_halves([_sum4_into_half(r2, place) for r2 in r2s])
    return [f.reshape(g.shape[1], g.shape[2]) for f, g in zip(fins, grads)]


def _pair_blocks(w):
    h, d, _ = w.shape
    z = jnp.zeros((h // 2, d, d), w.dtype)
    return jnp.concatenate([jnp.concatenate([w[0::2], z], axis=2), jnp.concatenate([z, w[1::2]], axis=2)], axis=1)


def _unpair_blocks(b):
    n, dd, _ = b.shape
    d = dd // 2
    return jnp.stack([b[:, :d, :d], b[:, d:, d:]], axis=1).reshape(2 * n, d, d)


def _pad_rows(a, rows):
    return jnp.pad(a, ((0, rows - a.shape[0]), (0, 0)))


class _Packer:
    def __init__(self, shapes, width=1024, row_multiple=64):
        self.shapes = shapes
        self.sizes = [math.prod(s) for s in shapes]
        total = sum(self.sizes)
        self.width = width
        self.rows = -(-total // (width * row_multiple)) * row_multiple
        self.pad = self.rows * width - total

    def pack(self, arrays):
        flat = jnp.concatenate([a.reshape(-1).astype(F32) for a in arrays] + [jnp.zeros((self.pad,), F32)])
        return flat.reshape(self.rows, self.width)

    def unpack(self, packed):
        flat = packed.reshape(-1)
        out, off = [], 0
        for s, n in zip(self.shapes, self.sizes):
            out.append(flat[off:off + n].reshape(s))
            off += n
        return out


SMALL = ["b_ada", "ffn1_norm", "mix_norm", "conv_w", "conv_b", "gate_a_w", "gate_a_b", "gate_x_w", "gate_x_b", "lru_lambda",
         "v_norm", "spatial_w", "spatial_b", "lru_out_norm", "gmlp_out_norm", "ffn2_norm", "final_norm"]
BIG = ["ffn1_w_gu", "ffn1_w_down", "w_in", "w_out", "ffn2_w_gu", "ffn2_w_down"]
WEIGHTS = ["w_ada", "b_ada", "ffn1_norm", "ffn1_w_gu", "ffn1_w_down", "mix_norm", "w_in", "conv_w", "conv_b", "gate_a_w", "gate_a_b",
           "gate_x_w", "gate_x_b", "lru_lambda", "v_norm", "spatial_w", "spatial_b", "lru_out_norm", "gmlp_out_norm", "w_out",
           "ffn2_norm", "ffn2_w_gu", "ffn2_w_down", "final_norm"]


def kernel(x, c, w_ada, b_ada, ffn1_norm, ffn1_w_gu, ffn1_w_down, mix_norm, w_in, conv_w, conv_b, gate_a_w, gate_a_b, gate_x_w, gate_x_b, lru_lambda, v_norm, spatial_w, spatial_b, lru_out_norm, gmlp_out_norm, w_out, ffn2_norm, ffn2_w_gu, ffn2_w_down, final_norm, loss_target, m_w_ada, m_b_ada, m_ffn1_norm, m_ffn1_w_gu, m_ffn1_w_down, m_mix_norm, m_w_in, m_conv_w, m_conv_b, m_gate_a_w, m_gate_a_b, m_gate_x_w, m_gate_x_b, m_lru_lambda, m_v_norm, m_spatial_w, m_spatial_b, m_lru_out_norm, m_gmlp_out_norm, m_w_out, m_ffn2_norm, m_ffn2_w_gu, m_ffn2_w_down, m_final_norm, v_w_ada, v_b_ada, v_ffn1_norm, v_ffn1_w_gu, v_ffn1_w_down, v_mix_norm, v_w_in, v_conv_w, v_conv_b, v_gate_a_w, v_gate_a_b, v_gate_x_w, v_gate_x_b, v_lru_lambda, v_v_norm, v_spatial_w, v_spatial_b, v_lru_out_norm, v_gmlp_out_norm, v_w_out, v_ffn2_norm, v_ffn2_w_gu, v_ffn2_w_down, v_final_norm):
    given = dict(locals())
    W = {n: given[n] for n in WEIGHTS}
    L = w_ada.shape[0]
    S, D = x.shape[1], x.shape[2]
    LW = conv_b.shape[1]
    hd = LW // HEADS
    xi, yi, ci = lax.axis_index("x"), lax.axis_index("y"), lax.axis_index("c")
    chip = 2 * xi + yi
    dev = 2 * chip + ci
    place = jnp.stack([ci, chip]).astype(jnp.int32)
    xs = x.reshape(S, D)
    tgt = loss_target.reshape(S, D)

    c_all = _all_gather8(_pad_rows(c, SUBLANES))[:, 0, :]
    n_ada = w_ada.shape[2]
    b_shard = lax.dynamic_slice_in_dim(b_ada, chip * n_ada, n_ada, axis=1)
    mod_shard = _ada_fwd(_pad_rows(c_all, 2 * SUBLANES), w_ada, b_shard[:, None, :])
    mod_all = _all_gather8(mod_shard.reshape(L * 2 * SUBLANES, n_ada))
    mod_rows = lax.dynamic_index_in_dim(mod_all.reshape(N_CHIPS, 2, L, 2 * SUBLANES, n_ada)[:, 0], dev, axis=2, keepdims=False)
    mod = mod_rows.transpose(1, 0, 2).reshape(L, N_MOD, 1, D)

    full = dict(zip(BIG, _gather_weights([_cast_into_slot(W[n], place) for n in BIG])))

    def layer_weights(l):
        return dict(
            gu1=full["ffn1_w_gu"][:, l], d1=full["ffn1_w_down"][:, l].reshape(-1, D), win=full["w_in"][:, l],
            wout=full["w_out"][:, l].reshape(-1, D), gu2=full["ffn2_w_gu"][:, l], d2=full["ffn2_w_down"][:, l].reshape(-1, D))

    cws = LW // N_CHIPS
    conv_all = _all_gather8(_pad_rows(conv_w.reshape(L * CONV_WIDTH, cws), -(-L * CONV_WIDTH // SUBLANES) * SUBLANES))
    conv_full = conv_all.reshape(N_CHIPS, 2, -1, cws)[:, 0, :L * CONV_WIDTH].reshape(N_CHIPS, L, CONV_WIDTH, cws)
    conv_full = conv_full.transpose(1, 2, 0, 3).reshape(L, CONV_WIDTH, LW)

    tril = jnp.tril(jnp.ones((CHUNK, CHUNK), F32))
    seg = (jnp.arange(LW)[:, None] // hd == jnp.arange(LW)[None, :] // hd).astype(jnp.bfloat16)

    def mixer_params(l):
        ws = spatial_w[l] * tril
        wsp = jnp.concatenate([ws[0::2], ws[1::2]], axis=2)
        wa, wx = _pair_blocks(gate_a_w[l]), _pair_blocks(gate_x_w[l])
        return dict(
            cw=conv_full[l], cb=conv_b[l][None],
            wa=wa.astype(MXU_DTYPE), wx=wx.astype(MXU_DTYPE), wat=wa.transpose(0, 2, 1).astype(MXU_DTYPE), wxt=wx.transpose(0, 2, 1).astype(MXU_DTYPE),
            ba=gate_a_b[l].reshape(1, LW), bx=gate_x_b[l].reshape(1, LW), lam=lru_lambda[l][None], gv=v_norm[l][None],
            wsp=wsp.astype(MXU_DTYPE), wspt=wsp.transpose(0, 2, 1).astype(MXU_DTYPE),
            bfull=jnp.repeat(spatial_b[l].T, hd, axis=1), g_lru=lru_out_norm[l][None], g_gm=gmlp_out_norm[l][None])

    saved = []
    xcur = xs
    for l in range(L):
        lw, mp, md = layer_weights(l), mixer_params(l), mod[l]
        s = dict(lw=lw, mp=mp, md=md)
        s["x0"] = xcur
        s["h1"] = _modnorm(xcur, ffn1_norm[l][None], md[0], md[1])
        s["a1"], s["gu1"] = _ffn_up(s["h1"], lw["gu1"])
        s["f1"], xcur = _mm_res(s["a1"], lw["d1"], xcur, md[2], 0.5)
        s["x1"] = xcur
        s["h2"] = _modnorm(xcur, mix_norm[l][None], md[3], md[4])
        s["proj"] = _mm_chunks(s["h2"], lw["win"])
        s["ylru"] = _lru_fwd(s["proj"], mp["cw"], mp["cb"], mp["wa"], mp["ba"], mp["wx"], mp["bx"], mp["lam"])
        s["yn"], s["ygm"] = _gmlp_fwd(s["proj"], s["ylru"], mp["gv"], seg, mp["wsp"], mp["bfull"], mp["g_lru"], mp["g_gm"])
        s["f2"], xcur = _mm_res(s["yn"], lw["wout"], xcur, md[5], 1.0)
        s["x2"] = xcur
        s["h3"] = _modnorm(xcur, ffn2_norm[l][None], md[6], md[7])
        s["a3"], s["gu3"] = _ffn_up(s["h3"], lw["gu2"])
        s["f3"], xcur = _mm_res(s["a3"], lw["d2"], xcur, md[8], 0.5)
        saved.append(s)

    dx, dq, head_acc = _loss_head(xcur, tgt, final_norm[None], saved[-1]["md"][8], 0.5)
    loss = lax.psum(jnp.sum(head_acc[1]), ("x", "y", "c"))
    small_grads = {}
    big_grads = {n: [None] * L for n in BIG}
    dmods = [None] * L
    zero_row = jnp.zeros((1, D), F32)

    def ffn_bwd(dx, dq, x_in, h, a, gu, f, wgu, wd, gn, sc, next_gate, next_scale):
        d_wd = _mm_tn_chunks(a, dq[None], 1408, 512)[0]
        dgu = _ffn_bwd_act(dq, wd, gu)
        C = dgu.shape[3]
        dgu4 = dgu.reshape(N_CHIPS, S, C)
        d_wgu = _mm_tn_chunks(h, dgu4, 512, C)
        dh = _mm_nt_chunks(dgu4, wgu)
        dx, dq, acc = _norm_bwd(x_in, dh, dx, f, gn, sc, 0.5, next_gate, next_scale)
        return dx, dq, acc, d_wgu, d_wd.reshape(N_CHIPS, -1, D)

    for l in reversed(range(L)):
        s = saved[l]
        lw, mp, md = s["lw"], s["mp"], s["md"]
        dx, dq, acc3, big_grads["ffn2_w_gu"][l], big_grads["ffn2_w_down"][l] = ffn_bwd(
            dx, dq, s["x2"], s["h3"], s["a3"], s["gu3"], s["f3"], lw["gu2"], lw["d2"], ffn2_norm[l][None], md[7], md[5], 1.0)
        big_grads["w_out"][l] = _mm_tn_chunks(s["yn"], dq[None], 512, 512)[0].reshape(N_CHIPS, -1, D)
        dyn = _mm_nt_chunks(dq[None], lw["wout"][None])
        dylru, duv, dwsp, dbfull, gacc = _gmlp_bwd(s["proj"], s["ylru"], s["ygm"], dyn, mp["gv"], seg, mp["wsp"], mp["wspt"], mp["bfull"], mp["g_lru"], mp["g_gm"])
        dxg, dwa, dwx, lvec = _lru_bwd(s["proj"], dylru, mp["cw"], mp["cb"], mp["wa"], mp["ba"], mp["wx"], mp["bx"], mp["lam"], mp["wat"], mp["wxt"])
        dproj = jnp.concatenate([dxg, duv], axis=0)
        big_grads["w_in"][l] = _mm_tn_chunks(s["h2"], dproj, 512, LW)
        dh2 = _mm_nt_chunks(dproj, lw["win"])
        dx, dq, acc2 = _norm_bwd(s["x1"], dh2, dx, s["f2"], mix_norm[l][None], md[4], 1.0, md[2], 0.5)
        if l > 0:
            ng, ns = saved[l - 1]["md"][8], 0.5
        else:
            ng, ns = zero_row, 0.0
        dx, dq, acc1, big_grads["ffn1_w_gu"][l], big_grads["ffn1_w_down"][l] = ffn_bwd(
            dx, dq, s["x0"], s["h1"], s["a1"], s["gu1"], s["f1"], lw["gu1"], lw["d1"], ffn1_norm[l][None], md[1], ng, ns)

        dmods[l] = jnp.concatenate([acc1[0:2], acc1[3:4], acc2[0:2], acc2[3:4], acc3[0:2], acc3[3:4]], axis=0)
        dws = jnp.stack([dwsp[:, :, :CHUNK], dwsp[:, :, CHUNK:]], axis=1).reshape(HEADS, CHUNK, CHUNK) * tril
        lg = {"ffn1_norm": acc1[2], "mix_norm": acc2[2], "ffn2_norm": acc3[2],
              "conv_w": lvec[4:8], "conv_b": lvec[3], "gate_a_w": _unpair_blocks(dwa), "gate_a_b": lvec[0].reshape(HEADS, hd),
              "gate_x_w": _unpair_blocks(dwx), "gate_x_b": lvec[1].reshape(HEADS, hd), "lru_lambda": lvec[2], "v_norm": gacc[2],
              "spatial_w": dws, "spatial_b": dbfull.reshape(CHUNK, HEADS, hd).sum(-1).T, "lru_out_norm": gacc[0], "gmlp_out_norm": gacc[1]}
        for n, g in lg.items():
            small_grads.setdefault(n, [None] * L)[l] = g
    grad_x = dx.reshape(x.shape)

    per_layer = [n for n in SMALL if n not in ("b_ada", "final_norm")]
    part = [jnp.stack(small_grads[n]) for n in per_layer] + [head_acc[0], jnp.stack(dmods)]
    packer = _Packer([p.shape for p in part])
    gathered = _all_gather8(packer.pack(part))
    summed = packer.unpack(_sum_leading(gathered))
    grads = dict(zip(per_layer + ["final_norm"], summed[:-1]))
    grads["b_ada"] = summed[-1].reshape(L, N_MOD * D)
    off = sum(packer.sizes[:-1])
    dmod_rows = gathered.reshape(N_DEV, -1)[:, off:off + L * N_MOD * D].reshape(N_DEV, L, N_MOD * D)
    dmod_shard = lax.dynamic_slice_in_dim(dmod_rows, chip * n_ada, n_ada, axis=2).transpose(1, 0, 2)
    grads["w_ada"] = _ada_grad(c_all.T, dmod_shard)
    grads["conv_w"] = lax.dynamic_slice_in_dim(grads["conv_w"], chip * cws, cws, axis=2)

    order = [(n, l) for n in BIG for l in range(L)]
    reduced = dict(zip(order, _reduce_scatter([big_grads[n][l] for n, l in order], place)))
    for n in BIG:
        grads[n] = jnp.stack([reduced[n, l] for l in range(L)])

    delta, new_m, new_v = {}, {}, {}
    for n in BIG + ["w_ada"]:
        shp = W[n].shape
        d_, m_, v_ = _adamw(*[a.reshape(-1, shp[-1]) for a in (W[n], grads[n], given["m_" + n], given["v_" + n])])
        delta[n], new_m[n], new_v[n] = d_.reshape(shp), m_.reshape(shp), v_.reshape(shp)
    spk = _Packer([W[n].shape for n in SMALL])
    d_, m_, v_ = _adamw(spk.pack([W[n] for n in SMALL]), spk.pack([grads[n] for n in SMALL]),
                        spk.pack([given["m_" + n] for n in SMALL]), spk.pack([given["v_" + n] for n in SMALL]))
    for n, a, b, e in zip(SMALL, spk.unpack(d_), spk.unpack(m_), spk.unpack(v_)):
        delta[n], new_m[n], new_v[n] = a, b, e
    grads = {n: grads[n].reshape(W[n].shape) for n in WEIGHTS}
    return (loss, grad_x, *[grads[n] for n in WEIGHTS], *[delta[n] for n in WEIGHTS], *[new_m[n] for n in WEIGHTS], *[new_v[n] for n in WEIGHTS])
```

```python
import math

import jax
import jax.numpy as jnp
from jax import lax
from jax.experimental import pallas as pl
from jax.experimental.pallas import tpu as pltpu

F32 = jnp.float32
MXU_DTYPE = jnp.bfloat16
ACT_DTYPE = jnp.bfloat16
XFER_DTYPE = jnp.bfloat16
EPS = 1e-6
RG_LRU_C = 8.0
N_MOD = 9
CONV_WIDTH = 4
HEADS = 8
CHUNK = 128
LANES = 128
SUBLANES = 8
N_CHIPS = 4
N_DEV = 8
ADAM_LR, ADAM_B1, ADAM_B2, ADAM_EPS, ADAM_WD, ADAM_STEP = 0.001, 0.9, 0.999, 1e-08, 0.01, 10
VMEM_LIMIT_BYTES = 60 * 1024 * 1024
ROW_TILE_BYTES = 1 << 20
GELU_C = math.sqrt(2.0 / math.pi)
GELU_A = 0.044715

ANY = pl.BlockSpec(memory_space=pl.ANY)
MESH = pl.DeviceIdType.MESH
SDS = jax.ShapeDtypeStruct


def _params(*sem):
    return pltpu.CompilerParams(dimension_semantics=sem, vmem_limit_bytes=VMEM_LIMIT_BYTES)


def _dot(a, b):
    return jnp.dot(a.astype(MXU_DTYPE), b.astype(MXU_DTYPE), preferred_element_type=F32)


def _dot_nt(a, b):
    return lax.dot_general(a.astype(MXU_DTYPE), b.astype(MXU_DTYPE), (((1,), (1,)), ((), ())), preferred_element_type=F32)


def _dot_tn(a, b):
    return lax.dot_general(a.astype(MXU_DTYPE), b.astype(MXU_DTYPE), (((0,), (0,)), ((), ())), preferred_element_type=F32)


def _gelu(x):
    return x * (0.5 * (1.0 + jnp.tanh(GELU_C * (x + GELU_A * (x * x * x)))))


def _gelu_grad(x):
    t = jnp.tanh(GELU_C * (x + GELU_A * (x * x * x)))
    return 0.5 * (1.0 + t) + 0.5 * x * (1.0 - t * t) * (GELU_C * (1.0 + 3.0 * GELU_A * x * x))


def _sigmoid(x):
    return jax.nn.sigmoid(x)


def _rsqrt_ms(x):
    return lax.rsqrt(jnp.mean(x * x, axis=-1, keepdims=True) + EPS)


def _rowsum(x):
    return jnp.sum(x, axis=0, keepdims=True)


def _tile(n, want):
    t = min(n, want)
    assert n % t == 0, (n, want)
    return t


def _row_tile(rows, row_bytes):
    step = 2 * SUBLANES
    cap = max(step, ROW_TILE_BYTES // row_bytes)
    best = None
    for t in range(step, min(rows, cap) + 1, step):
        if rows % t == 0:
            best = t
    assert best is not None, (rows, row_bytes)
    return best


def _modnorm(x, gn, sh, sc):
    S, D = x.shape
    tm = _tile(S, 512)

    def body(x_ref, gn_ref, sh_ref, sc_ref, h_ref):
        xv = x_ref[...]
        h = (xv * _rsqrt_ms(xv) * gn_ref[...]) * (1.0 + sc_ref[...]) + sh_ref[...]
        h_ref[...] = h.astype(ACT_DTYPE)

    row = pl.BlockSpec((1, D), lambda i: (0, 0))
    return pl.pallas_call(
        body, name="modnorm", grid=(S // tm,),
        in_specs=[pl.BlockSpec((tm, D), lambda i: (i, 0)), row, row, row],
        out_specs=pl.BlockSpec((tm, D), lambda i: (i, 0)),
        out_shape=SDS((S, D), ACT_DTYPE), compiler_params=_params("parallel"),
    )(x, gn, sh, sc)


def _norm_bwd(x, dh, dxo, f, gn, sc, res_scale, next_gate, next_scale):
    S, D = x.shape
    tm = _tile(S, 256)

    def body(x_ref, dh_ref, dxo_ref, f_ref, gn_ref, sc_ref, ng_ref, dx_ref, dq_ref, acc_ref):
        @pl.when(pl.program_id(0) == 0)
        def _():
            acc_ref[...] = jnp.zeros_like(acc_ref)

        xv, dh, dxo = x_ref[...], dh_ref[...], dxo_ref[...]
        r = _rsqrt_ms(xv)
        xhat = xv * r
        gn = gn_ref[...]
        dn = dh * (1.0 + sc_ref[...])
        dxh = dn * gn
        dx = dxo + r * (dxh - xhat * jnp.mean(dxh * xhat, axis=-1, keepdims=True))
        dx_ref[...] = dx
        dq_ref[...] = ((next_scale * ng_ref[...]) * dx).astype(ACT_DTYPE)
        acc_ref[0:1, :] += _rowsum(dh)
        acc_ref[1:2, :] += _rowsum(dh * (xhat * gn))
        acc_ref[2:3, :] += _rowsum(dn * xhat)
        acc_ref[3:4, :] += _rowsum((res_scale * f_ref[...]) * dxo)

    tile = pl.BlockSpec((tm, D), lambda i: (i, 0))
    row = pl.BlockSpec((1, D), lambda i: (0, 0))
    return pl.pallas_call(
        body, name="norm_bwd", grid=(S // tm,),
        in_specs=[tile, tile, tile, tile, row, row, row],
        out_specs=[tile, tile, pl.BlockSpec((SUBLANES, D), lambda i: (0, 0))],
        out_shape=[SDS((S, D), F32), SDS((S, D), ACT_DTYPE), SDS((SUBLANES, D), F32)],
        compiler_params=_params("arbitrary"),
    )(x, dh, dxo, f, gn, sc, next_gate)


def _loss_head(x, target, gn, next_gate, next_scale):
    S, D = x.shape
    tm = _tile(S, 256)

    def body(x_ref, t_ref, gn_ref, ng_ref, dx_ref, dq_ref, acc_ref):
        @pl.when(pl.program_id(0) == 0)
        def _():
            acc_ref[...] = jnp.zeros_like(acc_ref)

        xv = x_ref[...]
        r = _rsqrt_ms(xv)
        xhat = xv * r
        gn = gn_ref[...]
        err = xhat * gn - t_ref[...]
        dy = err * (1.0 / D)
        dxh = dy * gn
        dx = r * (dxh - xhat * jnp.mean(dxh * xhat, axis=-1, keepdims=True))
        dx_ref[...] = dx
        dq_ref[...] = ((next_scale * ng_ref[...]) * dx).astype(ACT_DTYPE)
        acc_ref[0:1, :] += _rowsum(dy * xhat)
        acc_ref[1:2, :] += _rowsum(err * err) * (0.5 / D)

    tile = pl.BlockSpec((tm, D), lambda i: (i, 0))
    row = pl.BlockSpec((1, D), lambda i: (0, 0))
    return pl.pallas_call(
        body, name="loss_head", grid=(S // tm,),
        in_specs=[tile, tile, row, row],
        out_specs=[tile, tile, pl.BlockSpec((SUBLANES, D), lambda i: (0, 0))],
        out_shape=[SDS((S, D), F32), SDS((S, D), ACT_DTYPE), SDS((SUBLANES, D), F32)],
        compiler_params=_params("arbitrary"),
    )(x, target, gn, next_gate)


def _ffn_up(h, wgu):
    S, D = h.shape
    C = wgu.shape[2]
    tm = _tile(S, 512)

    def body(h_ref, wg_ref, wu_ref, a_ref, gu_ref):
        hv = h_ref[...]
        g = _dot(hv, wg_ref[...])
        u = _dot(hv, wu_ref[...])
        a_ref[...] = (g * _sigmoid(g) * u).astype(ACT_DTYPE)
        gu_ref[0] = g.astype(ACT_DTYPE)
        gu_ref[1] = u.astype(ACT_DTYPE)

    return pl.pallas_call(
        body, name="ffn_up", grid=(2, S // tm),
        in_specs=[
            pl.BlockSpec((tm, D), lambda j, i: (i, 0)),
            pl.BlockSpec((None, D, C), lambda j, i: (j, 0, 0)),
            pl.BlockSpec((None, D, C), lambda j, i: (2 + j, 0, 0)),
        ],
        out_specs=[
            pl.BlockSpec((tm, C), lambda j, i: (i, j)),
            pl.BlockSpec((2, None, tm, C), lambda j, i: (0, j, i, 0)),
        ],
        out_shape=[SDS((S, 2 * C), ACT_DTYPE), SDS((2, 2, S, C), ACT_DTYPE)],
        compiler_params=_params("parallel", "parallel"),
    )(h, wgu, wgu)


def _ffn_bwd_act(dq, wd, gu):
    S, D = dq.shape
    C = gu.shape[3]
    tm = _tile(S, 512)

    def body(dq_ref, wd_ref, gu_ref, dgu_ref):
        da = _dot_nt(dq_ref[...], wd_ref[...])
        g = gu_ref[0].astype(F32)
        u = gu_ref[1].astype(F32)
        s = _sigmoid(g)
        dgu_ref[0] = (da * u * (s * (1.0 + g * (1.0 - s)))).astype(ACT_DTYPE)
        dgu_ref[1] = (da * (g * s)).astype(ACT_DTYPE)

    gu_spec = pl.BlockSpec((2, None, tm, C), lambda j, i: (0, j, i, 0))
    return pl.pallas_call(
        body, name="ffn_bwd_act", grid=(2, S // tm),
        in_specs=[pl.BlockSpec((tm, D), lambda j, i: (i, 0)), pl.BlockSpec((C, D), lambda j, i: (j, 0)), gu_spec],
        out_specs=gu_spec,
        out_shape=SDS(gu.shape, ACT_DTYPE),
        compiler_params=_params("parallel", "parallel"),
    )(dq, wd, gu)


def _mm_res(a, w, x, gate, scale):
    S, K = a.shape
    D = w.shape[1]
    tm, tn = _tile(S, 1024), _tile(D, 512)

    def body(a_ref, w_ref, x_ref, g_ref, f_ref, xo_ref):
        f = _dot(a_ref[...], w_ref[...])
        f_ref[...] = f
        xo_ref[...] = x_ref[...] + (scale * g_ref[...]) * f

    tile = pl.BlockSpec((tm, tn), lambda j, i: (i, j))
    return pl.pallas_call(
        body, name=f"mm_res_k{K}", grid=(D // tn, S // tm),
        in_specs=[pl.BlockSpec((tm, K), lambda j, i: (i, 0)), pl.BlockSpec((K, tn), lambda j, i: (0, j)), tile,
                  pl.BlockSpec((1, tn), lambda j, i: (0, j))],
        out_specs=[tile, tile],
        out_shape=[SDS((S, D), F32), SDS((S, D), F32)],
        compiler_params=_params("parallel", "parallel"),
    )(a, w, x, gate)


def _mm_chunks(h, wc):
    S, K = h.shape
    P, _, N = wc.shape
    tm = _tile(S, 1024)

    def body(h_ref, w_ref, o_ref):
        o_ref[...] = _dot(h_ref[...], w_ref[...])

    return pl.pallas_call(
        body, name="mm_chunks", grid=(P, S // tm),
        in_specs=[pl.BlockSpec((tm, K), lambda j, i: (i, 0)), pl.BlockSpec((None, K, N), lambda j, i: (j, 0, 0))],
        out_specs=pl.BlockSpec((tm, N), lambda j, i: (i, j)),
        out_shape=SDS((S, P * N), F32),
        compiler_params=_params("parallel", "parallel"),
    )(h, wc)


def _mm_nt_chunks(ac, wc):
    P, S, K = ac.shape
    N = wc.shape[1]
    tm, tn = _tile(S, 1024), _tile(N, 512)

    def body(a_ref, w_ref, o_ref):
        acc = _dot_nt(a_ref[0], w_ref[0])
        for p in range(1, P):
            acc += _dot_nt(a_ref[p], w_ref[p])
        o_ref[...] = acc

    return pl.pallas_call(
        body, name=f"mm_nt_p{P}k{K}", grid=(S // tm, N // tn),
        in_specs=[pl.BlockSpec((P, tm, K), lambda i, j: (0, i, 0)), pl.BlockSpec((P, tn, K), lambda i, j: (0, j, 0))],
        out_specs=pl.BlockSpec((tm, tn), lambda i, j: (i, j)),
        out_shape=SDS((S, N), F32),
        compiler_params=_params("parallel", "parallel"),
    )(ac, wc)


def _mm_tn_chunks(a, bc, tile_m, tile_n):
    S, M = a.shape
    P, _, N = bc.shape
    ts, tm, tn = _tile(S, 512), _tile(M, tile_m), _tile(N, tile_n)

    def body(a_ref, b_ref, o_ref):
        @pl.when(pl.program_id(3) == 0)
        def _():
            o_ref[...] = jnp.zeros_like(o_ref)

        o_ref[...] += _dot_tn(a_ref[...], b_ref[...])

    return pl.pallas_call(
        body, name=f"mm_tn_m{M}n{N}", grid=(P, M // tm, N // tn, S // ts),
        in_specs=[pl.BlockSpec((ts, tm), lambda p, m, n, k: (k, m)), pl.BlockSpec((None, ts, tn), lambda p, m, n, k: (p, k, n))],
        out_specs=pl.BlockSpec((None, tm, tn), lambda p, m, n, k: (p, m, n)),
        out_shape=SDS((P, M, N), F32),
        compiler_params=_params("parallel", "parallel", "parallel", "arbitrary"),
    )(a, bc)


def _shift_down(x, s, row, fill):
    return jnp.where(row >= s, pltpu.roll(x, s, 0), fill)


def _shift_up(x, s, row, fill):
    n = x.shape[0]
    return jnp.where(row < n - s, pltpu.roll(x, n - s, 0), fill)


def _scan_down(a, b, row):
    s = 1
    while s < a.shape[0]:
        b = a * _shift_down(b, s, row, 0.0) + b
        a = a * _shift_down(a, s, row, 1.0)
        s *= 2
    return b


def _scan_up(a, b, row):
    s = 1
    while s < a.shape[0]:
        b = a * _shift_up(b, s, row, 0.0) + b
        a = a * _shift_up(a, s, row, 1.0)
        s *= 2
    return b


def _conv(xl, cw_ref, cb_ref, row):
    y = cb_ref[...] + _shift_down(xl, 3, row, 0.0) * cw_ref[0:1, :]
    y = y + _shift_down(xl, 2, row, 0.0) * cw_ref[1:2, :]
    y = y + _shift_down(xl, 1, row, 0.0) * cw_ref[2:3, :]
    return y + xl * cw_ref[3:4, :]


def _lru_gates(xc, wa_ref, ba_ref, wx_ref, bx_ref, lam_ref):
    ra = _sigmoid(_dot(xc, wa_ref[...]) + ba_ref[...])
    ri = _sigmoid(_dot(xc, wx_ref[...]) + bx_ref[...])
    ls = jax.nn.log_sigmoid(lam_ref[...])
    a = jnp.exp((RG_LRU_C * ra) * ls)
    mult = jnp.sqrt(1.0 - a * a)
    return ra, ri, ls, a, mult


def _lru_specs(S):
    col = lambda off: pl.BlockSpec((S, LANES), lambda j: (0, off + j))
    vec = pl.BlockSpec((1, LANES), lambda j: (0, j))
    blk = pl.BlockSpec((None, LANES, LANES), lambda j: (j, 0, 0))
    cw = pl.BlockSpec((CONV_WIDTH, LANES), lambda j: (0, j))
    return col, vec, blk, cw


def _lru_fwd(proj, cw, cb, wa, ba, wx, bx, lam):
    S = proj.shape[0]
    W = cb.shape[1]
    nb = W // LANES

    def body(xl_ref, gl_ref, cw_ref, cb_ref, wa_ref, ba_ref, wx_ref, bx_ref, lam_ref, y_ref):
        row = lax.broadcasted_iota(jnp.int32, (S, LANES), 0)
        xc = _conv(xl_ref[...], cw_ref, cb_ref, row)
        _, ri, _, a, mult = _lru_gates(xc, wa_ref, ba_ref, wx_ref, bx_ref, lam_ref)
        h = _scan_down(a, mult * (ri * xc), row)
        y_ref[...] = h * _gelu(gl_ref[...])

    col, vec, blk, cws = _lru_specs(S)
    return pl.pallas_call(
        body, name="lru_fwd", grid=(nb,),
        in_specs=[col(0), col(nb), cws, vec, blk, vec, blk, vec, vec],
        out_specs=pl.BlockSpec((S, LANES), lambda j: (0, j)),
        out_shape=SDS((S, W), F32), compiler_params=_params("parallel"),
    )(proj, proj, cw, cb, wa, ba, wx, bx, lam)


def _lru_bwd(proj, dy, cw, cb, wa, ba, wx, bx, lam, wat, wxt):
    S = proj.shape[0]
    W = cb.shape[1]
    nb = W // LANES

    def body(xl_ref, gl_ref, dy_ref, cw_ref, cb_ref, wa_ref, ba_ref, wx_ref, bx_ref, lam_ref, wat_ref, wxt_ref,
             dp_ref, dwa_ref, dwx_ref, vec_ref):
        row = lax.broadcasted_iota(jnp.int32, (S, LANES), 0)
        xl = xl_ref[...]
        xc = _conv(xl, cw_ref, cb_ref, row)
        ra, ri, ls, a, mult = _lru_gates(xc, wa_ref, ba_ref, wx_ref, bx_ref, lam_ref)
        h = _scan_down(a, mult * (ri * xc), row)
        gl = gl_ref[...]
        dyv = dy_ref[...]
        dp_ref[1] = (dyv * h * _gelu_grad(gl)).astype(ACT_DTYPE)
        adj = _scan_up(_shift_up(a, 1, row, 0.0), dyv * _gelu(gl), row)
        da = adj * _shift_down(h, 1, row, 0.0)
        dmult = adj * (ri * xc)
        dlog_a = da * a - dmult * (a * a) / mult
        dra = dlog_a * (RG_LRU_C * ls)
        dpa = dra * ra * (1.0 - ra)
        dpi = (adj * mult * xc) * ri * (1.0 - ri)
        dxc = adj * mult * ri + _dot(dpa, wat_ref[...]) + _dot(dpi, wxt_ref[...])
        dwa_ref[...] = _dot_tn(xc, dpa)
        dwx_ref[...] = _dot_tn(xc, dpi)
        dxl = dxc * cw_ref[3:4, :]
        dxl = dxl + _shift_up(dxc, 1, row, 0.0) * cw_ref[2:3, :]
        dxl = dxl + _shift_up(dxc, 2, row, 0.0) * cw_ref[1:2, :]
        dxl = dxl + _shift_up(dxc, 3, row, 0.0) * cw_ref[0:1, :]
        dp_ref[0] = dxl.astype(ACT_DTYPE)
        vec_ref[...] = jnp.zeros_like(vec_ref)
        vec_ref[0:1, :] = _rowsum(dpa)
        vec_ref[1:2, :] = _rowsum(dpi)
        vec_ref[2:3, :] = _rowsum(dlog_a * (RG_LRU_C * ra)) * _sigmoid(-lam_ref[...])
        vec_ref[3:4, :] = _rowsum(dxc)
        vec_ref[4:5, :] = _rowsum(dxc * _shift_down(xl, 3, row, 0.0))
        vec_ref[5:6, :] = _rowsum(dxc * _shift_down(xl, 2, row, 0.0))
        vec_ref[6:7, :] = _rowsum(dxc * _shift_down(xl, 1, row, 0.0))
        vec_ref[7:8, :] = _rowsum(dxc * xl)

    col, vec, blk, cws = _lru_specs(S)
    return pl.pallas_call(
        body, name="lru_bwd", grid=(nb,),
        in_specs=[col(0), col(nb), col(0), cws, vec, blk, vec, blk, vec, vec, blk, blk],
        out_specs=[pl.BlockSpec((2, S, LANES), lambda j: (0, 0, j)), blk, blk, pl.BlockSpec((2 * SUBLANES, LANES), lambda j: (0, j))],
        out_shape=[SDS((2, S, W), ACT_DTYPE), SDS((nb, LANES, LANES), F32), SDS((nb, LANES, LANES), F32), SDS((2 * SUBLANES, W), F32)],
        compiler_params=_params("parallel"),
    )(proj, proj, dy, cw, cb, wa, ba, wx, bx, lam, wat, wxt)


def _seg_mean(x, seg_ref, width):
    hi = x.astype(jnp.bfloat16)
    lo = (x - hi.astype(F32)).astype(jnp.bfloat16)
    ones = seg_ref[...]
    s = jnp.dot(hi, ones, preferred_element_type=F32) + jnp.dot(lo, ones, preferred_element_type=F32)
    return s * (1.0 / width)


def _gmlp_core(u_ref, v_ref, gv_ref, seg_ref, ws_ref, bfull_ref, z_scr, hd):
    tm, W = u_ref.shape
    lane = lax.broadcasted_iota(jnp.int32, (CHUNK, LANES), 1)
    ug = _gelu(u_ref[...])
    vg = _gelu(v_ref[...])
    cen = vg - _seg_mean(vg, seg_ref, hd)
    rstd = lax.rsqrt(_seg_mean(cen * cen, seg_ref, hd) + EPS)
    vhat = cen * rstd
    vh = vhat * gv_ref[...]
    vcats = {}
    for ci in range(tm // CHUNK):
        for p in range(W // LANES):
            blk = vh[ci * CHUNK:(ci + 1) * CHUNK, p * LANES:(p + 1) * LANES]
            vcat = jnp.concatenate([jnp.where(lane < hd, blk, 0.0), jnp.where(lane >= hd, blk, 0.0)], axis=0).astype(MXU_DTYPE)
            vcats[ci, p] = vcat
            z_scr[ci * CHUNK:(ci + 1) * CHUNK, p * LANES:(p + 1) * LANES] = (
                jnp.dot(ws_ref[p], vcat, preferred_element_type=F32) + bfull_ref[:, p * LANES:(p + 1) * LANES])
    return ug, vhat, rstd, vcats


def _gmlp_specs(tm, W, nb):
    rows = lambda off: pl.BlockSpec((tm, W), lambda i: (i, off))
    vec = pl.BlockSpec((1, W), lambda i: (0, 0))
    seg = pl.BlockSpec((W, W), lambda i: (0, 0))
    wsp = pl.BlockSpec((nb, CHUNK, 2 * CHUNK), lambda i: (0, 0, 0))
    bfull = pl.BlockSpec((CHUNK, W), lambda i: (0, 0))
    return rows, vec, seg, wsp, bfull


def _gmlp_fwd(proj, ylru, gv, seg, wsp, bfull, g_lru, g_gm):
    S, W = ylru.shape
    nb = W // LANES
    hd = W // HEADS
    tm = _tile(S, 512)

    def body(u_ref, v_ref, yl_ref, gv_ref, seg_ref, ws_ref, bfull_ref, gl_ref, gg_ref, yn_ref, ygm_ref, z_scr):
        ug, _, _, _ = _gmlp_core(u_ref, v_ref, gv_ref, seg_ref, ws_ref, bfull_ref, z_scr, hd)
        ygm = ug * z_scr[...]
        ygm_ref[...] = ygm
        yl = yl_ref[...]
        yn_ref[:, 0:W] = (yl * _rsqrt_ms(yl) * gl_ref[...]).astype(ACT_DTYPE)
        yn_ref[:, W:2 * W] = (ygm * _rsqrt_ms(ygm) * gg_ref[...]).astype(ACT_DTYPE)

    rows, vec, segs, wsps, bfulls = _gmlp_specs(tm, W, nb)
    return pl.pallas_call(
        body, name="gmlp_fwd", grid=(S // tm,),
        in_specs=[rows(2), rows(3), rows(0), vec, segs, wsps, bfulls, vec, vec],
        out_specs=[pl.BlockSpec((tm, 2 * W), lambda i: (i, 0)), rows(0)],
        out_shape=[SDS((S, 2 * W), ACT_DTYPE), SDS((S, W), F32)],
        scratch_shapes=[pltpu.VMEM((tm, W), F32)],
        compiler_params=_params("parallel"),
    )(proj, proj, ylru, gv, seg, wsp, bfull, g_lru, g_gm)


def _rms_bwd(y, g, dyn):
    r = _rsqrt_ms(y)
    yhat = y * r
    dyh = dyn * g
    return r * (dyh - yhat * jnp.mean(dyh * yhat, axis=-1, keepdims=True)), _rowsum(dyn * yhat)


def _gmlp_bwd(proj, ylru, ygm, dyn, gv, seg, wsp, wspt, bfull, g_lru, g_gm):
    S, W = ylru.shape
    nb = W // LANES
    hd = W // HEADS
    tm = _tile(S, 256)

    def body(u_ref, v_ref, yl_ref, ygm_ref, dl_ref, dg_ref, gv_ref, seg_ref, ws_ref, wst_ref, bfull_ref, gl_ref, gg_ref,
             dyl_ref, duv_ref, dws_ref, dbf_ref, acc_ref, z_scr, dvh_scr):
        @pl.when(pl.program_id(0) == 0)
        def _():
            dws_ref[...] = jnp.zeros_like(dws_ref)
            dbf_ref[...] = jnp.zeros_like(dbf_ref)
            acc_ref[...] = jnp.zeros_like(acc_ref)

        dyl, dgl = _rms_bwd(yl_ref[...], gl_ref[...], dl_ref[...])
        dyl_ref[...] = dyl
        dygm, dgg = _rms_bwd(ygm_ref[...], gg_ref[...], dg_ref[...])
        ug, vhat, rstd, vcats = _gmlp_core(u_ref, v_ref, gv_ref, seg_ref, ws_ref, bfull_ref, z_scr, hd)
        duv_ref[0] = (dygm * z_scr[...] * _gelu_grad(u_ref[...])).astype(ACT_DTYPE)
        dz = dygm * ug
        lane = lax.broadcasted_iota(jnp.int32, (CHUNK, LANES), 1)
        dbf = dz[0:CHUNK, :]
        for ci in range(1, tm // CHUNK):
            dbf += dz[ci * CHUNK:(ci + 1) * CHUNK, :]
        dbf_ref[...] += dbf
        for ci in range(tm // CHUNK):
            for p in range(nb):
                dzb = dz[ci * CHUNK:(ci + 1) * CHUNK, p * LANES:(p + 1) * LANES].astype(MXU_DTYPE)
                dws_ref[p] += _dot_nt(dzb, vcats[ci, p])
                dvc = jnp.dot(wst_ref[p], dzb, preferred_element_type=F32)
                dvh_scr[ci * CHUNK:(ci + 1) * CHUNK, p * LANES:(p + 1) * LANES] = jnp.where(lane < hd, dvc[0:CHUNK], dvc[CHUNK:2 * CHUNK])
        dvh = dvh_scr[...]
        dvn = dvh * gv_ref[...]
        dvg = rstd * (dvn - _seg_mean(dvn, seg_ref, hd) - vhat * _seg_mean(dvn * vhat, seg_ref, hd))
        duv_ref[1] = (dvg * _gelu_grad(v_ref[...])).astype(ACT_DTYPE)
        acc_ref[0:1, :] += dgl
        acc_ref[1:2, :] += dgg
        acc_ref[2:3, :] += _rowsum(dvh * vhat)

    rows, vec, segs, wsps, bfulls = _gmlp_specs(tm, W, nb)
    wspt_spec = pl.BlockSpec((nb, 2 * CHUNK, CHUNK), lambda i: (0, 0, 0))
    return pl.pallas_call(
        body, name="gmlp_bwd", grid=(S // tm,),
        in_specs=[rows(2), rows(3), rows(0), rows(0), rows(0), rows(1), vec, segs, wsps, wspt_spec, bfulls, vec, vec],
        out_specs=[rows(0), pl.BlockSpec((2, tm, W), lambda i: (0, i, 0)), wsps, bfulls, pl.BlockSpec((SUBLANES, W), lambda i: (0, 0))],
        out_shape=[SDS((S, W), F32), SDS((2, S, W), ACT_DTYPE), SDS((nb, CHUNK, 2 * CHUNK), F32), SDS((CHUNK, W), F32), SDS((SUBLANES, W), F32)],
        scratch_shapes=[pltpu.VMEM((tm, W), F32), pltpu.VMEM((tm, W), F32)],
        compiler_params=_params("arbitrary"),
    )(proj, proj, ylru, ygm, dyn, dyn, gv, seg, wsp, wspt, bfull, g_lru, g_gm)


def _ada_fwd(c_all, w_ada, b_shard):
    L, D, N = w_ada.shape
    R = c_all.shape[0]
    tn = N // 2

    def body(c_ref, w_ref, b_ref, o_ref):
        cv = c_ref[...]
        o_ref[...] = _dot(cv * _sigmoid(cv), w_ref[...]) + b_ref[...]

    return pl.pallas_call(
        body, name="ada_fwd", grid=(L, N // tn),
        in_specs=[pl.BlockSpec((R, D), lambda l, j: (0, 0)), pl.BlockSpec((None, D, tn), lambda l, j: (l, 0, j)),
                  pl.BlockSpec((None, 1, tn), lambda l, j: (l, 0, j))],
        out_specs=pl.BlockSpec((None, R, tn), lambda l, j: (l, 0, j)),
        out_shape=SDS((L, R, N), F32), compiler_params=_params("parallel", "parallel"),
    )(c_all, w_ada, b_shard)


def _ada_grad(c_all_t, dmod):
    D, B = c_all_t.shape
    L, _, N = dmod.shape
    tn = N // 2

    def body(c_ref, d_ref, o_ref):
        cv = c_ref[...]
        sc = cv * _sigmoid(cv)
        acc = sc[:, 0:1] * d_ref[0:1, :]
        for b in range(1, B):
            acc += sc[:, b:b + 1] * d_ref[b:b + 1, :]
        o_ref[...] = acc

    return pl.pallas_call(
        body, name="ada_grad", grid=(L, N // tn),
        in_specs=[pl.BlockSpec((D, B), lambda l, j: (0, 0)), pl.BlockSpec((None, B, tn), lambda l, j: (l, 0, j))],
        out_specs=pl.BlockSpec((None, D, tn), lambda l, j: (l, 0, j)),
        out_shape=SDS((L, D, N), F32), compiler_params=_params("parallel", "parallel"),
    )(c_all_t, dmod)


def _adamw(w, g, m, v):
    R, C = w.shape
    tr = _row_tile(R, C * 4)

    def body(w_ref, g_ref, m_ref, v_ref, d_ref, mo_ref, vo_ref):
        gv = g_ref[...]
        mn = ADAM_B1 * m_ref[...] + (1.0 - ADAM_B1) * gv
        vn = ADAM_B2 * v_ref[...] + (1.0 - ADAM_B2) * (gv * gv)
        mo_ref[...] = mn
        vo_ref[...] = vn
        m_hat = mn / (1.0 - ADAM_B1 ** ADAM_STEP)
        v_hat = vn / (1.0 - ADAM_B2 ** ADAM_STEP)
        d_ref[...] = -ADAM_LR * (m_hat / (jnp.sqrt(v_hat) + ADAM_EPS) + ADAM_WD * w_ref[...])

    tile = pl.BlockSpec((tr, C), lambda i: (i, 0))
    return pl.pallas_call(
        body, name=f"adamw_r{R}c{C}", grid=(R // tr,), in_specs=[tile] * 4, out_specs=[tile] * 3,
        out_shape=[SDS((R, C), F32)] * 3, compiler_params=_params("parallel"),
    )(w, g, m, v)


def _sum_leading(a):
    P, R, C = a.shape
    tr = _row_tile(R, P * C * 4)

    def body(a_ref, o_ref):
        acc = a_ref[0]
        for p in range(1, P):
            acc = acc + a_ref[p]
        o_ref[...] = acc

    return pl.pallas_call(
        body, name=f"sum{P}_r{R}c{C}", grid=(R // tr,),
        in_specs=[pl.BlockSpec((P, tr, C), lambda i: (0, i, 0))],
        out_specs=pl.BlockSpec((tr, C), lambda i: (i, 0)),
        out_shape=SDS((R, C), F32), compiler_params=_params("parallel"),
    )(a)


def _add_half(g4, r1, place):
    _, _, R, C = g4.shape
    tr = _row_tile(R, C * 4)

    def body(place_ref, g_ref, r_ref, h_ref, own_ref):
        s = (g_ref[...] + r_ref[...]).astype(XFER_DTYPE)
        h_ref[...] = s

        @pl.when(pl.program_id(1) == place_ref[1])
        def _():
            own_ref[...] = s

    return pl.pallas_call(
        body, name=f"add_half_r{R}c{C}",
        grid_spec=pltpu.PrefetchScalarGridSpec(
            num_scalar_prefetch=1, grid=(R // tr, N_CHIPS),
            in_specs=[pl.BlockSpec((None, None, tr, C), lambda i, p, place_ref: (p, place_ref[0], i, 0)),
                      pl.BlockSpec((None, tr, C), lambda i, p, place_ref: (p, i, 0))],
            out_specs=[pl.BlockSpec((None, tr, C), lambda i, p, place_ref: (p, i, 0)),
                       pl.BlockSpec((None, tr, C), lambda i, p, place_ref: (place_ref[1], i, 0))],
        ),
        out_shape=[SDS((N_CHIPS, R, C), XFER_DTYPE)] * 2, compiler_params=_params("parallel", "arbitrary"),
    )(place, g4, r1)


def _sum4_into_half(r2, place):
    P, R, C = r2.shape
    tr = _row_tile(R, P * C * 4)

    def body(place_ref, a_ref, o_ref):
        acc = a_ref[0].astype(F32)
        for p in range(1, P):
            acc = acc + a_ref[p].astype(F32)
        o_ref[...] = acc

    return pl.pallas_call(
        body, name=f"sum4_r{R}c{C}",
        grid_spec=pltpu.PrefetchScalarGridSpec(
            num_scalar_prefetch=1, grid=(R // tr,),
            in_specs=[pl.BlockSpec((P, tr, C), lambda i, place_ref: (0, i, 0))],
            out_specs=pl.BlockSpec((None, tr, C), lambda i, place_ref: (place_ref[0], i, 0)),
        ),
        out_shape=SDS((2, R, C), F32), compiler_params=_params("parallel"),
    )(place, r2)


def _cast_into_slot(w, place):
    L, R, C = w.shape
    tr = _row_tile(R, C * 4)

    def body(place_ref, w_ref, o_ref):
        o_ref[...] = w_ref[...].astype(MXU_DTYPE)

    return pl.pallas_call(
        body, name=f"cast_r{R}c{C}",
        grid_spec=pltpu.PrefetchScalarGridSpec(
            num_scalar_prefetch=1, grid=(L, R // tr),
            in_specs=[pl.BlockSpec((None, tr, C), lambda l, i, place_ref: (l, i, 0))],
            out_specs=pl.BlockSpec((None, None, tr, C), lambda l, i, place_ref: (place_ref[1], l, i, 0)),
        ),
        out_shape=SDS((N_CHIPS, L, R, C), MXU_DTYPE), compiler_params=_params("parallel", "parallel"),
    )(place, w)


def _place():
    x, y, c = lax.axis_index("x"), lax.axis_index("y"), lax.axis_index("c")
    chips = [(1 - x, y), (x, 1 - y), (1 - x, 1 - y)]
    return x, y, c, chips


def _remote(src, dst, send_sem, recv_sem, to):
    return pltpu.make_async_remote_copy(src_ref=src, dst_ref=dst, send_sem=send_sem, recv_sem=recv_sem, device_id=to, device_id_type=MESH)


def _all_gather8(v):
    R, N = v.shape

    def body(v_ref, out_ref, send_sems, recv_sems, local_sem):
        x, y, c, chips = _place()
        me, sibling = (x, y, c), (x, y, 1 - c)

        def slot(px, py, pc):
            return out_ref.at[4 * px + 2 * py + pc]

        def copy(k, block, to, src=None):
            return _remote(slot(*block) if src is None else src, slot(*block), send_sems.at[k], recv_sems.at[k], to)

        mine = pltpu.make_async_copy(v_ref, slot(*me), local_sem)
        mine.start()
        first = [copy(0, me, sibling, src=v_ref)] + [copy(1 + j, me, (*chip, c), src=v_ref) for j, chip in enumerate(chips)]
        for cp in first:
            cp.start()
        passed = [copy(4 + j, (*chip, c), sibling) for j, chip in enumerate(chips)]
        for j, chip in enumerate(chips):
            copy(1 + j, (*chip, c), me).wait_recv()
            passed[j].start()
        copy(0, sibling, me).wait_recv()
        for j, chip in enumerate(chips):
            copy(4 + j, (*chip, 1 - c), me).wait_recv()
        for cp in first + passed:
            cp.wait_send()
        mine.wait()

    return pl.pallas_call(
        body, name=f"all_gather8_r{R}n{N}", out_shape=SDS((N_DEV, R, N), v.dtype), in_specs=[ANY], out_specs=ANY,
        scratch_shapes=[pltpu.SemaphoreType.DMA((7,)), pltpu.SemaphoreType.DMA((7,)), pltpu.SemaphoreType.DMA],
    )(v)


def _gather_weights(slots):
    n = len(slots)

    def body(*refs):
        ins, outs = refs[:n], refs[n:2 * n]
        send_sems, recv_sems = refs[2 * n:]
        x, y, c, chips = _place()
        q = 2 * x + y
        sibling = (x, y, 1 - c)
        first = []
        for k in range(n):
            for j, chip in enumerate(chips):
                first.append(_remote(ins[k].at[q, c], outs[k].at[q, c], send_sems.at[k, j], recv_sems.at[k, j], (*chip, c)))
                first[-1].start()
        passed = []
        for k in range(n):
            for j, chip in enumerate(chips):
                half = outs[k].at[2 * chip[0] + chip[1], c]
                _remote(half, half, send_sems.at[k, j], recv_sems.at[k, j], sibling).wait_recv()
                passed.append(_remote(half, half, send_sems.at[k, 3 + j], recv_sems.at[k, 3 + j], sibling))
                passed[-1].start()
        for k in range(n):
            for j, chip in enumerate(chips):
                half = outs[k].at[2 * chip[0] + chip[1], 1 - c]
                _remote(half, half, send_sems.at[k, 3 + j], recv_sems.at[k, 3 + j], sibling).wait_recv()
        for cp in first + passed:
            cp.wait_send()

    return pl.pallas_call(
        body, name="gather_weights", out_shape=[SDS(s.shape, s.dtype) for s in slots],
        in_specs=[ANY] * n, out_specs=[ANY] * n, input_output_aliases={k: k for k in range(n)},
        scratch_shapes=[pltpu.SemaphoreType.DMA((n, 6)), pltpu.SemaphoreType.DMA((n, 6))],
    )(*slots)


def _swap_halves(g4s):
    n = len(g4s)

    def body(*refs):
        ins, outs = refs[:n], refs[n:2 * n]
        send_sems, recv_sems = refs[2 * n:]
        x, y, c, _ = _place()
        sibling = (x, y, 1 - c)
        for k in range(n):
            for p in range(N_CHIPS):
                _remote(ins[k].at[p, 1 - c], outs[k].at[p], send_sems.at[k], recv_sems.at[k], sibling).start()
        for k in range(n):
            _remote(outs[k], outs[k], send_sems.at[k], recv_sems.at[k], sibling).wait()

    return pl.pallas_call(
        body, name="swap_halves", out_shape=[SDS((N_CHIPS,) + g.shape[2:], g.dtype) for g in g4s],
        in_specs=[ANY] * n, out_specs=[ANY] * n,
        scratch_shapes=[pltpu.SemaphoreType.DMA((n,)), pltpu.SemaphoreType.DMA((n,))],
    )(*g4s)


def _scatter_regions(hs, lands):
    n = len(hs)

    def body(*refs):
        ins, outs = refs[:n], refs[2 * n:3 * n]
        send_sems, recv_sems = refs[3 * n:]
        x, y, c, chips = _place()
        q = 2 * x + y
        sent = []
        for k in range(n):
            for j, chip in enumerate(chips):
                sent.append(_remote(ins[k].at[2 * chip[0] + chip[1]], outs[k].at[q], send_sems.at[k, j], recv_sems.at[k, j], (*chip, c)))
                sent[-1].start()
        for k in range(n):
            for j, chip in enumerate(chips):
                got = outs[k].at[2 * chip[0] + chip[1]]
                _remote(got, got, send_sems.at[k, j], recv_sems.at[k, j], (x, y, c)).wait_recv()
        for cp in sent:
            cp.wait_send()

    return pl.pallas_call(
        body, name="scatter_regions", out_shape=[SDS(h.shape, h.dtype) for h in lands],
        in_specs=[ANY] * (2 * n), out_specs=[ANY] * n, input_output_aliases={n + k: k for k in range(n)},
        scratch_shapes=[pltpu.SemaphoreType.DMA((n, 3)), pltpu.SemaphoreType.DMA((n, 3))],
    )(*hs, *lands)


def _share_halves(fins):
    n = len(fins)

    def body(*refs):
        ins, outs = refs[:n], refs[n:2 * n]
        send_sems, recv_sems = refs[2 * n:]
        x, y, c, _ = _place()
        sibling = (x, y, 1 - c)
        sent = [_remote(ins[k].at[c], outs[k].at[c], send_sems.at[k], recv_sems.at[k], sibling) for k in range(n)]
        for cp in sent:
            cp.start()
        for k in range(n):
            got = outs[k].at[1 - c]
            _remote(got, got, send_sems.at[k], recv_sems.at[k], sibling).wait_recv()
        for cp in sent:
            cp.wait_send()

    return pl.pallas_call(
        body, name="share_halves", out_shape=[SDS(t.shape, t.dtype) for t in fins],
        in_specs=[ANY] * n, out_specs=[ANY] * n, input_output_aliases={k: k for k in range(n)},
        scratch_shapes=[pltpu.SemaphoreType.DMA((n,)), pltpu.SemaphoreType.DMA((n,))],
    )(*fins)


def _reduce_scatter(grads, place):
    g4s = [g.reshape(N_CHIPS, 2, g.shape[1] // 2, g.shape[2]) for g in grads]
    r1s = _swap_halves(g4s)
    pairs = [_add_half(g4, r1, place) for g4, r1 in zip(g4s, r1s)]
    r2s = _scatter_regions([h for h, _ in pairs], [own for _, own in pairs])
    fins = _share_halves([_sum4_into_half(r2, place) for r2 in r2s])
    return [f.reshape(g.shape[1], g.shape[2]) for f, g in zip(fins, grads)]


def _pair_blocks(w):
    h, d, _ = w.shape
    z = jnp.zeros((h // 2, d, d), w.dtype)
    return jnp.concatenate([jnp.concatenate([w[0::2], z], axis=2), jnp.concatenate([z, w[1::2]], axis=2)], axis=1)


def _unpair_blocks(b):
    n, dd, _ = b.shape
    d = dd // 2
    return jnp.stack([b[:, :d, :d], b[:, d:, d:]], axis=1).reshape(2 * n, d, d)


def _pad_rows(a, rows):
    return jnp.pad(a, ((0, rows - a.shape[0]), (0, 0)))


class _Packer:
    def __init__(self, shapes, width=1024, row_multiple=64):
        self.shapes = shapes
        self.sizes = [math.prod(s) for s in shapes]
        total = sum(self.sizes)
        self.width = width
        self.rows = -(-total // (width * row_multiple)) * row_multiple
        self.pad = self.rows * width - total

    def pack(self, arrays):
        flat = jnp.concatenate([a.reshape(-1).astype(F32) for a in arrays] + [jnp.zeros((self.pad,), F32)])
        return flat.reshape(self.rows, self.width)

    def unpack(self, packed):
        flat = packed.reshape(-1)
        out, off = [], 0
        for s, n in zip(self.shapes, self.sizes):
            out.append(flat[off:off + n].reshape(s))
            off += n
        return out


SMALL = ["b_ada", "ffn1_norm", "mix_norm", "conv_w", "conv_b", "gate_a_w", "gate_a_b", "gate_x_w", "gate_x_b", "lru_lambda",
         "v_norm", "spatial_w", "spatial_b", "lru_out_norm", "gmlp_out_norm", "ffn2_norm", "final_norm"]
BIG = ["ffn1_w_gu", "ffn1_w_down", "w_in", "w_out", "ffn2_w_gu", "ffn2_w_down"]
WEIGHTS = ["w_ada", "b_ada", "ffn1_norm", "ffn1_w_gu", "ffn1_w_down", "mix_norm", "w_in", "conv_w", "conv_b", "gate_a_w", "gate_a_b",
           "gate_x_w", "gate_x_b", "lru_lambda", "v_norm", "spatial_w", "spatial_b", "lru_out_norm", "gmlp_out_norm", "w_out",
           "ffn2_norm", "ffn2_w_gu", "ffn2_w_down", "final_norm"]


def kernel(x, c, w_ada, b_ada, ffn1_norm, ffn1_w_gu, ffn1_w_down, mix_norm, w_in, conv_w, conv_b, gate_a_w, gate_a_b, gate_x_w, gate_x_b, lru_lambda, v_norm, spatial_w, spatial_b, lru_out_norm, gmlp_out_norm, w_out, ffn2_norm, ffn2_w_gu, ffn2_w_down, final_norm, loss_target, m_w_ada, m_b_ada, m_ffn1_norm, m_ffn1_w_gu, m_ffn1_w_down, m_mix_norm, m_w_in, m_conv_w, m_conv_b, m_gate_a_w, m_gate_a_b, m_gate_x_w, m_gate_x_b, m_lru_lambda, m_v_norm, m_spatial_w, m_spatial_b, m_lru_out_norm, m_gmlp_out_norm, m_w_out, m_ffn2_norm, m_ffn2_w_gu, m_ffn2_w_down, m_final_norm, v_w_ada, v_b_ada, v_ffn1_norm, v_ffn1_w_gu, v_ffn1_w_down, v_mix_norm, v_w_in, v_conv_w, v_conv_b, v_gate_a_w, v_gate_a_b, v_gate_x_w, v_gate_x_b, v_lru_lambda, v_v_norm, v_spatial_w, v_spatial_b, v_lru_out_norm, v_gmlp_out_norm, v_w_out, v_ffn2_norm, v_ffn2_w_gu, v_ffn2_w_down, v_final_norm):
    given = dict(locals())
    W = {n: given[n] for n in WEIGHTS}
    L = w_ada.shape[0]
    S, D = x.shape[1], x.shape[2]
    LW = conv_b.shape[1]
    hd = LW // HEADS
    xi, yi, ci = lax.axis_index("x"), lax.axis_index("y"), lax.axis_index("c")
    chip = 2 * xi + yi
    dev = 2 * chip + ci
    place = jnp.stack([ci, chip]).astype(jnp.int32)
    xs = x.reshape(S, D)
    tgt = loss_target.reshape(S, D)

    c_all = _all_gather8(_pad_rows(c, SUBLANES))[:, 0, :]
    n_ada = w_ada.shape[2]
    b_shard = lax.dynamic_slice_in_dim(b_ada, chip * n_ada, n_ada, axis=1)
    mod_shard = _ada_fwd(_pad_rows(c_all, 2 * SUBLANES), w_ada, b_shard[:, None, :])
    mod_all = _all_gather8(mod_shard.reshape(L * 2 * SUBLANES, n_ada))
    mod_rows = lax.dynamic_index_in_dim(mod_all.reshape(N_CHIPS, 2, L, 2 * SUBLANES, n_ada)[:, 0], dev, axis=2, keepdims=False)
    mod = mod_rows.transpose(1, 0, 2).reshape(L, N_MOD, 1, D)

    full = dict(zip(BIG, _gather_weights([_cast_into_slot(W[n], place) for n in BIG])))

    def layer_weights(l):
        return dict(
            gu1=full["ffn1_w_gu"][:, l], d1=full["ffn1_w_down"][:, l].reshape(-1, D), win=full["w_in"][:, l],
            wout=full["w_out"][:, l].reshape(-1, D), gu2=full["ffn2_w_gu"][:, l], d2=full["ffn2_w_down"][:, l].reshape(-1, D))

    cws = LW // N_CHIPS
    conv_all = _all_gather8(_pad_rows(conv_w.reshape(L * CONV_WIDTH, cws), -(-L * CONV_WIDTH // SUBLANES) * SUBLANES))
    conv_full = conv_all.reshape(N_CHIPS, 2, -1, cws)[:, 0, :L * CONV_WIDTH].reshape(N_CHIPS, L, CONV_WIDTH, cws)
    conv_full = conv_full.transpose(1, 2, 0, 3).reshape(L, CONV_WIDTH, LW)

    tril = jnp.tril(jnp.ones((CHUNK, CHUNK), F32))
    seg = (jnp.arange(LW)[:, None] // hd == jnp.arange(LW)[None, :] // hd).astype(jnp.bfloat16)

    def mixer_params(l):
        ws = spatial_w[l] * tril
        wsp = jnp.concatenate([ws[0::2], ws[1::2]], axis=2)
        wa, wx = _pair_blocks(gate_a_w[l]), _pair_blocks(gate_x_w[l])
        return dict(
            cw=conv_full[l], cb=conv_b[l][None],
            wa=wa.astype(MXU_DTYPE), wx=wx.astype(MXU_DTYPE), wat=wa.transpose(0, 2, 1).astype(MXU_DTYPE), wxt=wx.transpose(0, 2, 1).astype(MXU_DTYPE),
            ba=gate_a_b[l].reshape(1, LW), bx=gate_x_b[l].reshape(1, LW), lam=lru_lambda[l][None], gv=v_norm[l][None],
            wsp=wsp.astype(MXU_DTYPE), wspt=wsp.transpose(0, 2, 1).astype(MXU_DTYPE),
            bfull=jnp.repeat(spatial_b[l].T, hd, axis=1), g_lru=lru_out_norm[l][None], g_gm=gmlp_out_norm[l][None])

    saved = []
    xcur = xs
    for l in range(L):
        lw, mp, md = layer_weights(l), mixer_params(l), mod[l]
        s = dict(lw=lw, mp=mp, md=md)
        s["x0"] = xcur
        s["h1"] = _modnorm(xcur, ffn1_norm[l][None], md[0], md[1])
        s["a1"], s["gu1"] = _ffn_up(s["h1"], lw["gu1"])
        s["f1"], xcur = _mm_res(s["a1"], lw["d1"], xcur, md[2], 0.5)
        s["x1"] = xcur
        s["h2"] = _modnorm(xcur, mix_norm[l][None], md[3], md[4])
        s["proj"] = _mm_chunks(s["h2"], lw["win"])
        s["ylru"] = _lru_fwd(s["proj"], mp["cw"], mp["cb"], mp["wa"], mp["ba"], mp["wx"], mp["bx"], mp["lam"])
        s["yn"], s["ygm"] = _gmlp_fwd(s["proj"], s["ylru"], mp["gv"], seg, mp["wsp"], mp["bfull"], mp["g_lru"], mp["g_gm"])
        s["f2"], xcur = _mm_res(s["yn"], lw["wout"], xcur, md[5], 1.0)
        s["x2"] = xcur
        s["h3"] = _modnorm(xcur, ffn2_norm[l][None], md[6], md[7])
        s["a3"], s["gu3"] = _ffn_up(s["h3"], lw["gu2"])
        s["f3"], xcur = _mm_res(s["a3"], lw["d2"], xcur, md[8], 0.5)
        saved.append(s)

    dx, dq, head_acc = _loss_head(xcur, tgt, final_norm[None], saved[-1]["md"][8], 0.5)
    loss = lax.psum(jnp.sum(head_acc[1]), ("x", "y", "c"))
    small_grads = {}
    big_grads = {n: [None] * L for n in BIG}
    dmods = [None] * L
    zero_row = jnp.zeros((1, D), F32)

    def ffn_bwd(dx, dq, x_in, h, a, gu, f, wgu, wd, gn, sc, next_gate, next_scale):
        d_wd = _mm_tn_chunks(a, dq[None], 1408, 1024)[0]
        dgu = _ffn_bwd_act(dq, wd, gu)
        C = dgu.shape[3]
        dgu4 = dgu.reshape(N_CHIPS, S, C)
        d_wgu = _mm_tn_chunks(h, dgu4, 1024, C)
        dh = _mm_nt_chunks(dgu4, wgu)
        dx, dq, acc = _norm_bwd(x_in, dh, dx, f, gn, sc, 0.5, next_gate, next_scale)
        return dx, dq, acc, d_wgu, d_wd.reshape(N_CHIPS, -1, D)

    for l in reversed(range(L)):
        s = saved[l]
        lw, mp, md = s["lw"], s["mp"], s["md"]
        dx, dq, acc3, big_grads["ffn2_w_gu"][l], big_grads["ffn2_w_down"][l] = ffn_bwd(
            dx, dq, s["x2"], s["h3"], s["a3"], s["gu3"], s["f3"], lw["gu2"], lw["d2"], ffn2_norm[l][None], md[7], md[5], 1.0)
        big_grads["w_out"][l] = _mm_tn_chunks(s["yn"], dq[None], 1024, 1024)[0].reshape(N_CHIPS, -1, D)
        dyn = _mm_nt_chunks(dq[None], lw["wout"][None])
        dylru, duv, dwsp, dbfull, gacc = _gmlp_bwd(s["proj"], s["ylru"], s["ygm"], dyn, mp["gv"], seg, mp["wsp"], mp["wspt"], mp["bfull"], mp["g_lru"], mp["g_gm"])
        dxg, dwa, dwx, lvec = _lru_bwd(s["proj"], dylru, mp["cw"], mp["cb"], mp["wa"], mp["ba"], mp["wx"], mp["bx"], mp["lam"], mp["wat"], mp["wxt"])
        dproj = jnp.concatenate([dxg, duv], axis=0)
        big_grads["w_in"][l] = _mm_tn_chunks(s["h2"], dproj, 1024, LW)
        dh2 = _mm_nt_chunks(dproj, lw["win"])
        dx, dq, acc2 = _norm_bwd(s["x1"], dh2, dx, s["f2"], mix_norm[l][None], md[4], 1.0, md[2], 0.5)
        if l > 0:
            ng, ns = saved[l - 1]["md"][8], 0.5
        else:
            ng, ns = zero_row, 0.0
        dx, dq, acc1, big_grads["ffn1_w_gu"][l], big_grads["ffn1_w_down"][l] = ffn_bwd(
            dx, dq, s["x0"], s["h1"], s["a1"], s["gu1"], s["f1"], lw["gu1"], lw["d1"], ffn1_norm[l][None], md[1], ng, ns)

        dmods[l] = jnp.concatenate([acc1[0:2], acc1[3:4], acc2[0:2], acc2[3:4], acc3[0:2], acc3[3:4]], axis=0)
        dws = jnp.stack([dwsp[:, :, :CHUNK], dwsp[:, :, CHUNK:]], axis=1).reshape(HEADS, CHUNK, CHUNK) * tril
        lg = {"ffn1_norm": acc1[2], "mix_norm": acc2[2], "ffn2_norm": acc3[2],
              "conv_w": lvec[4:8], "conv_b": lvec[3], "gate_a_w": _unpair_blocks(dwa), "gate_a_b": lvec[0].reshape(HEADS, hd),
              "gate_x_w": _unpair_blocks(dwx), "gate_x_b": lvec[1].reshape(HEADS, hd), "lru_lambda": lvec[2], "v_norm": gacc[2],
              "spatial_w": dws, "spatial_b": dbfull.reshape(CHUNK, HEADS, hd).sum(-1).T, "lru_out_norm": gacc[0], "gmlp_out_norm": gacc[1]}
        for n, g in lg.items():
            small_grads.setdefault(n, [None] * L)[l] = g
    grad_x = dx.reshape(x.shape)

    per_layer = [n for n in SMALL if n not in ("b_ada", "final_norm")]
    part = [jnp.stack(small_grads[n]) for n in per_layer] + [head_acc[0], jnp.stack(dmods)]
    packer = _Packer([p.shape for p in part])
    gathered = _all_gather8(packer.pack(part))
    summed = packer.unpack(_sum_leading(gathered))
    grads = dict(zip(per_layer + ["final_norm"], summed[:-1]))
    grads["b_ada"] = summed[-1].reshape(L, N_MOD * D)
    off = sum(packer.sizes[:-1])
    dmod_rows = gathered.reshape(N_DEV, -1)[:, off:off + L * N_MOD * D].reshape(N_DEV, L, N_MOD * D)
    dmod_shard = lax.dynamic_slice_in_dim(dmod_rows, chip * n_ada, n_ada, axis=2).transpose(1, 0, 2)
    grads["w_ada"] = _ada_grad(c_all.T, dmod_shard)
    grads["conv_w"] = lax.dynamic_slice_in_dim(grads["conv_w"], chip * cws, cws, axis=2)

    order = [(n, l) for n in BIG for l in range(L)]
    reduced = dict(zip(order, _reduce_scatter([big_grads[n][l] for n, l in order], place)))
    for n in BIG:
        grads[n] = jnp.stack([reduced[n, l] for l in range(L)])

    delta, new_m, new_v = {}, {}, {}
    for n in BIG + ["w_ada"]:
        shp = W[n].shape
        d_, m_, v_ = _adamw(*[a.reshape(-1, shp[-1]) for a in (W[n], grads[n], given["m_" + n], given["v_" + n])])
        delta[n], new_m[n], new_v[n] = d_.reshape(shp), m_.reshape(shp), v_.reshape(shp)
    spk = _Packer([W[n].shape for n in SMALL])
    d_, m_, v_ = _adamw(spk.pack([W[n] for n in SMALL]), spk.pack([grads[n] for n in SMALL]),
                        spk.pack([given["m_" + n] for n in SMALL]), spk.pack([given["v_" + n] for n in SMALL]))
    for n, a, b, e in zip(SMALL, spk.unpack(d_), spk.unpack(m_), spk.unpack(v_)):
        delta[n], new_m[n], new_v[n] = a, b, e
    grads = {n: grads[n].reshape(W[n].shape) for n in WEIGHTS}
    return (loss, grad_x, *[grads[n] for n in WEIGHTS], *[delta[n] for n in WEIGHTS], *[new_m[n] for n in WEIGHTS], *[new_v[n] for n in WEIGHTS])
```

```python
import math

import jax
import jax.numpy as jnp
from jax import lax
from jax.experimental import pallas as pl
from jax.experimental.pallas import tpu as pltpu

F32 = jnp.float32
MXU_DTYPE = jnp.bfloat16
ACT_DTYPE = jnp.bfloat16
XFER_DTYPE = jnp.bfloat16
EPS = 1e-6
RG_LRU_C = 8.0
N_MOD = 9
CONV_WIDTH = 4
HEADS = 8
CHUNK = 128
LANES = 128
SUBLANES = 8
N_CHIPS = 4
N_DEV = 8
ADAM_LR, ADAM_B1, ADAM_B2, ADAM_EPS, ADAM_WD, ADAM_STEP = 0.001, 0.9, 0.999, 1e-08, 0.01, 10
VMEM_LIMIT_BYTES = 60 * 1024 * 1024
ROW_TILE_BYTES = 1 << 20
GELU_C = math.sqrt(2.0 / math.pi)
GELU_A = 0.044715

ANY = pl.BlockSpec(memory_space=pl.ANY)
MESH = pl.DeviceIdType.MESH
SDS = jax.ShapeDtypeStruct


def _params(*sem):
    return pltpu.CompilerParams(dimension_semantics=sem, vmem_limit_bytes=VMEM_LIMIT_BYTES)


def _dot(a, b):
    return jnp.dot(a.astype(MXU_DTYPE), b.astype(MXU_DTYPE), preferred_element_type=F32)


def _dot_nt(a, b):
    return lax.dot_general(a.astype(MXU_DTYPE), b.astype(MXU_DTYPE), (((1,), (1,)), ((), ())), preferred_element_type=F32)


def _dot_tn(a, b):
    return lax.dot_general(a.astype(MXU_DTYPE), b.astype(MXU_DTYPE), (((0,), (0,)), ((), ())), preferred_element_type=F32)


def _gelu(x):
    return x * (0.5 * (1.0 + jnp.tanh(GELU_C * (x + GELU_A * (x * x * x)))))


def _gelu_grad(x):
    t = jnp.tanh(GELU_C * (x + GELU_A * (x * x * x)))
    return 0.5 * (1.0 + t) + 0.5 * x * (1.0 - t * t) * (GELU_C * (1.0 + 3.0 * GELU_A * x * x))


def _sigmoid(x):
    return jax.nn.sigmoid(x)


def _rsqrt_ms(x):
    return lax.rsqrt(jnp.mean(x * x, axis=-1, keepdims=True) + EPS)


def _rowsum(x):
    return jnp.sum(x, axis=0, keepdims=True)


def _tile(n, want):
    t = min(n, want)
    assert n % t == 0, (n, want)
    return t


def _row_tile(rows, row_bytes):
    step = 2 * SUBLANES
    cap = max(step, ROW_TILE_BYTES // row_bytes)
    best = None
    for t in range(step, min(rows, cap) + 1, step):
        if rows % t == 0:
            best = t
    assert best is not None, (rows, row_bytes)
    return best


def _modnorm(x, gn, sh, sc):
    S, D = x.shape
    tm = _tile(S, 512)

    def body(x_ref, gn_ref, sh_ref, sc_ref, h_ref):
        xv = x_ref[...]
        h = (xv * _rsqrt_ms(xv) * gn_ref[...]) * (1.0 + sc_ref[...]) + sh_ref[...]
        h_ref[...] = h.astype(ACT_DTYPE)

    row = pl.BlockSpec((1, D), lambda i: (0, 0))
    return pl.pallas_call(
        body, name="modnorm", grid=(S // tm,),
        in_specs=[pl.BlockSpec((tm, D), lambda i: (i, 0)), row, row, row],
        out_specs=pl.BlockSpec((tm, D), lambda i: (i, 0)),
        out_shape=SDS((S, D), ACT_DTYPE), compiler_params=_params("parallel"),
    )(x, gn, sh, sc)


def _norm_bwd(x, dh, dxo, f, gn, sc, res_scale, next_gate, next_scale):
    S, D = x.shape
    tm = _tile(S, 256)

    def body(x_ref, dh_ref, dxo_ref, f_ref, gn_ref, sc_ref, ng_ref, dx_ref, dq_ref, acc_ref):
        @pl.when(pl.program_id(0) == 0)
        def _():
            acc_ref[...] = jnp.zeros_like(acc_ref)

        xv, dh, dxo = x_ref[...], dh_ref[...], dxo_ref[...]
        r = _rsqrt_ms(xv)
        xhat = xv * r
        gn = gn_ref[...]
        dn = dh * (1.0 + sc_ref[...])
        dxh = dn * gn
        dx = dxo + r * (dxh - xhat * jnp.mean(dxh * xhat, axis=-1, keepdims=True))
        dx_ref[...] = dx
        dq_ref[...] = ((next_scale * ng_ref[...]) * dx).astype(ACT_DTYPE)
        acc_ref[0:1, :] += _rowsum(dh)
        acc_ref[1:2, :] += _rowsum(dh * (xhat * gn))
        acc_ref[2:3, :] += _rowsum(dn * xhat)
        acc_ref[3:4, :] += _rowsum((res_scale * f_ref[...]) * dxo)

    tile = pl.BlockSpec((tm, D), lambda i: (i, 0))
    row = pl.BlockSpec((1, D), lambda i: (0, 0))
    return pl.pallas_call(
        body, name="norm_bwd", grid=(S // tm,),
        in_specs=[tile, tile, tile, tile, row, row, row],
        out_specs=[tile, tile, pl.BlockSpec((SUBLANES, D), lambda i: (0, 0))],
        out_shape=[SDS((S, D), F32), SDS((S, D), ACT_DTYPE), SDS((SUBLANES, D), F32)],
        compiler_params=_params("arbitrary"),
    )(x, dh, dxo, f, gn, sc, next_gate)


def _loss_head(x, target, gn, next_gate, next_scale):
    S, D = x.shape
    tm = _tile(S, 256)

    def body(x_ref, t_ref, gn_ref, ng_ref, dx_ref, dq_ref, acc_ref):
        @pl.when(pl.program_id(0) == 0)
        def _():
            acc_ref[...] = jnp.zeros_like(acc_ref)

        xv = x_ref[...]
        r = _rsqrt_ms(xv)
        xhat = xv * r
        gn = gn_ref[...]
        err = xhat * gn - t_ref[...]
        dy = err * (1.0 / D)
        dxh = dy * gn
        dx = r * (dxh - xhat * jnp.mean(dxh * xhat, axis=-1, keepdims=True))
        dx_ref[...] = dx
        dq_ref[...] = ((next_scale * ng_ref[...]) * dx).astype(ACT_DTYPE)
        acc_ref[0:1, :] += _rowsum(dy * xhat)
        acc_ref[1:2, :] += _rowsum(err * err) * (0.5 / D)

    tile = pl.BlockSpec((tm, D), lambda i: (i, 0))
    row = pl.BlockSpec((1, D), lambda i: (0, 0))
    return pl.pallas_call(
        body, name="loss_head", grid=(S // tm,),
        in_specs=[tile, tile, row, row],
        out_specs=[tile, tile, pl.BlockSpec((SUBLANES, D), lambda i: (0, 0))],
        out_shape=[SDS((S, D), F32), SDS((S, D), ACT_DTYPE), SDS((SUBLANES, D), F32)],
        compiler_params=_params("arbitrary"),
    )(x, target, gn, next_gate)


def _ffn_up(h, wgu):
    S, D = h.shape
    C = wgu.shape[2]
    tm = _tile(S, 512)

    def body(h_ref, wg_ref, wu_ref, a_ref, gu_ref):
        hv = h_ref[...]
        g = _dot(hv, wg_ref[...])
        u = _dot(hv, wu_ref[...])
        a_ref[...] = (g * _sigmoid(g) * u).astype(ACT_DTYPE)
        gu_ref[0] = g.astype(ACT_DTYPE)
        gu_ref[1] = u.astype(ACT_DTYPE)

    return pl.pallas_call(
        body, name="ffn_up", grid=(2, S // tm),
        in_specs=[
            pl.BlockSpec((tm, D), lambda j, i: (i, 0)),
            pl.BlockSpec((None, D, C), lambda j, i: (j, 0, 0)),
            pl.BlockSpec((None, D, C), lambda j, i: (2 + j, 0, 0)),
        ],
        out_specs=[
            pl.BlockSpec((tm, C), lambda j, i: (i, j)),
            pl.BlockSpec((2, None, tm, C), lambda j, i: (0, j, i, 0)),
        ],
        out_shape=[SDS((S, 2 * C), ACT_DTYPE), SDS((2, 2, S, C), ACT_DTYPE)],
        compiler_params=_params("parallel", "parallel"),
    )(h, wgu, wgu)


def _ffn_bwd_act(dq, wd, gu):
    S, D = dq.shape
    C = gu.shape[3]
    tm = _tile(S, 512)

    def body(dq_ref, wd_ref, gu_ref, dgu_ref):
        da = _dot_nt(dq_ref[...], wd_ref[...])
        g = gu_ref[0].astype(F32)
        u = gu_ref[1].astype(F32)
        s = _sigmoid(g)
        dgu_ref[0] = (da * u * (s * (1.0 + g * (1.0 - s)))).astype(ACT_DTYPE)
        dgu_ref[1] = (da * (g * s)).astype(ACT_DTYPE)

    gu_spec = pl.BlockSpec((2, None, tm, C), lambda j, i: (0, j, i, 0))
    return pl.pallas_call(
        body, name="ffn_bwd_act", grid=(2, S // tm),
        in_specs=[pl.BlockSpec((tm, D), lambda j, i: (i, 0)), pl.BlockSpec((C, D), lambda j, i: (j, 0)), gu_spec],
        out_specs=gu_spec,
        out_shape=SDS(gu.shape, ACT_DTYPE),
        compiler_params=_params("parallel", "parallel"),
    )(dq, wd, gu)


def _mm_res(a, w, x, gate, scale):
    S, K = a.shape
    D = w.shape[1]
    tm, tn = _tile(S, 1024), _tile(D, 512)

    def body(a_ref, w_ref, x_ref, g_ref, f_ref, xo_ref):
        f = _dot(a_ref[...], w_ref[...])
        f_ref[...] = f
        xo_ref[...] = x_ref[...] + (scale * g_ref[...]) * f

    tile = pl.BlockSpec((tm, tn), lambda j, i: (i, j))
    return pl.pallas_call(
        body, name=f"mm_res_k{K}", grid=(D // tn, S // tm),
        in_specs=[pl.BlockSpec((tm, K), lambda j, i: (i, 0)), pl.BlockSpec((K, tn), lambda j, i: (0, j)), tile,
                  pl.BlockSpec((1, tn), lambda j, i: (0, j))],
        out_specs=[tile, tile],
        out_shape=[SDS((S, D), F32), SDS((S, D), F32)],
        compiler_params=_params("parallel", "parallel"),
    )(a, w, x, gate)


def _mm_chunks(h, wc):
    S, K = h.shape
    P, _, N = wc.shape
    tm = _tile(S, 1024)

    def body(h_ref, w_ref, o_ref):
        o_ref[...] = _dot(h_ref[...], w_ref[...])

    return pl.pallas_call(
        body, name="mm_chunks", grid=(P, S // tm),
        in_specs=[pl.BlockSpec((tm, K), lambda j, i: (i, 0)), pl.BlockSpec((None, K, N), lambda j, i: (j, 0, 0))],
        out_specs=pl.BlockSpec((tm, N), lambda j, i: (i, j)),
        out_shape=SDS((S, P * N), F32),
        compiler_params=_params("parallel", "parallel"),
    )(h, wc)


def _mm_nt_chunks(ac, wc):
    P, S, K = ac.shape
    N = wc.shape[1]
    tm, tn = _tile(S, 1024), _tile(N, 512)

    def body(a_ref, w_ref, o_ref):
        acc = _dot_nt(a_ref[0], w_ref[0])
        for p in range(1, P):
            acc += _dot_nt(a_ref[p], w_ref[p])
        o_ref[...] = acc

    return pl.pallas_call(
        body, name=f"mm_nt_p{P}k{K}", grid=(S // tm, N // tn),
        in_specs=[pl.BlockSpec((P, tm, K), lambda i, j: (0, i, 0)), pl.BlockSpec((P, tn, K), lambda i, j: (0, j, 0))],
        out_specs=pl.BlockSpec((tm, tn), lambda i, j: (i, j)),
        out_shape=SDS((S, N), F32),
        compiler_params=_params("parallel", "parallel"),
    )(ac, wc)


def _mm_tn_chunks(a, bc, tile_m, tile_n):
    S, M = a.shape
    P, _, N = bc.shape
    ts, tm, tn = _tile(S, 512), _tile(M, tile_m), _tile(N, tile_n)

    def body(a_ref, b_ref, o_ref):
        @pl.when(pl.program_id(3) == 0)
        def _():
            o_ref[...] = jnp.zeros_like(o_ref)

        o_ref[...] += _dot_tn(a_ref[...], b_ref[...])

    return pl.pallas_call(
        body, name=f"mm_tn_m{M}n{N}", grid=(P, M // tm, N // tn, S // ts),
        in_specs=[pl.BlockSpec((ts, tm), lambda p, m, n, k: (k, m)), pl.BlockSpec((None, ts, tn), lambda p, m, n, k: (p, k, n))],
        out_specs=pl.BlockSpec((None, tm, tn), lambda p, m, n, k: (p, m, n)),
        out_shape=SDS((P, M, N), F32),
        compiler_params=_params("parallel", "parallel", "parallel", "arbitrary"),
    )(a, bc)


def _shift_down(x, s, row, fill):
    return jnp.where(row >= s, pltpu.roll(x, s, 0), fill)


def _shift_up(x, s, row, fill):
    n = x.shape[0]
    return jnp.where(row < n - s, pltpu.roll(x, n - s, 0), fill)


def _scan_down(a, b, row):
    s = 1
    while s < a.shape[0]:
        b = a * _shift_down(b, s, row, 0.0) + b
        a = a * _shift_down(a, s, row, 1.0)
        s *= 2
    return b


def _scan_up(a, b, row):
    s = 1
    while s < a.shape[0]:
        b = a * _shift_up(b, s, row, 0.0) + b
        a = a * _shift_up(a, s, row, 1.0)
        s *= 2
    return b


def _conv(xl, cw_ref, cb_ref, row):
    y = cb_ref[...] + _shift_down(xl, 3, row, 0.0) * cw_ref[0:1, :]
    y = y + _shift_down(xl, 2, row, 0.0) * cw_ref[1:2, :]
    y = y + _shift_down(xl, 1, row, 0.0) * cw_ref[2:3, :]
    return y + xl * cw_ref[3:4, :]


def _lru_gates(xc, wa_ref, ba_ref, wx_ref, bx_ref, lam_ref):
    ra = _sigmoid(_dot(xc, wa_ref[...]) + ba_ref[...])
    ri = _sigmoid(_dot(xc, wx_ref[...]) + bx_ref[...])
    ls = jax.nn.log_sigmoid(lam_ref[...])
    a = jnp.exp((RG_LRU_C * ra) * ls)
    mult = jnp.sqrt(1.0 - a * a)
    return ra, ri, ls, a, mult


def _lru_specs(S):
    col = lambda off: pl.BlockSpec((S, LANES), lambda j: (0, off + j))
    vec = pl.BlockSpec((1, LANES), lambda j: (0, j))
    blk = pl.BlockSpec((None, LANES, LANES), lambda j: (j, 0, 0))
    cw = pl.BlockSpec((CONV_WIDTH, LANES), lambda j: (0, j))
    return col, vec, blk, cw


def _lru_fwd(proj, cw, cb, wa, ba, wx, bx, lam):
    S = proj.shape[0]
    W = cb.shape[1]
    nb = W // LANES

    def body(xl_ref, gl_ref, cw_ref, cb_ref, wa_ref, ba_ref, wx_ref, bx_ref, lam_ref, y_ref):
        row = lax.broadcasted_iota(jnp.int32, (S, LANES), 0)
        xc = _conv(xl_ref[...], cw_ref, cb_ref, row)
        _, ri, _, a, mult = _lru_gates(xc, wa_ref, ba_ref, wx_ref, bx_ref, lam_ref)
        h = _scan_down(a, mult * (ri * xc), row)
        y_ref[...] = h * _gelu(gl_ref[...])

    col, vec, blk, cws = _lru_specs(S)
    return pl.pallas_call(
        body, name="lru_fwd", grid=(nb,),
        in_specs=[col(0), col(nb), cws, vec, blk, vec, blk, vec, vec],
        out_specs=pl.BlockSpec((S, LANES), lambda j: (0, j)),
        out_shape=SDS((S, W), F32), compiler_params=_params("parallel"),
    )(proj, proj, cw, cb, wa, ba, wx, bx, lam)


def _lru_bwd(proj, dy, cw, cb, wa, ba, wx, bx, lam, wat, wxt):
    S = proj.shape[0]
    W = cb.shape[1]
    nb = W // LANES

    def body(xl_ref, gl_ref, dy_ref, cw_ref, cb_ref, wa_ref, ba_ref, wx_ref, bx_ref, lam_ref, wat_ref, wxt_ref,
             dp_ref, dwa_ref, dwx_ref, vec_ref):
        row = lax.broadcasted_iota(jnp.int32, (S, LANES), 0)
        xl = xl_ref[...]
        xc = _conv(xl, cw_ref, cb_ref, row)
        ra, ri, ls, a, mult = _lru_gates(xc, wa_ref, ba_ref, wx_ref, bx_ref, lam_ref)
        h = _scan_down(a, mult * (ri * xc), row)
        gl = gl_ref[...]
        dyv = dy_ref[...]
        dp_ref[1] = (dyv * h * _gelu_grad(gl)).astype(ACT_DTYPE)
        adj = _scan_up(_shift_up(a, 1, row, 0.0), dyv * _gelu(gl), row)
        da = adj * _shift_down(h, 1, row, 0.0)
        dmult = adj * (ri * xc)
        dlog_a = da * a - dmult * (a * a) / mult
        dra = dlog_a * (RG_LRU_C * ls)
        dpa = dra * ra * (1.0 - ra)
        dpi = (adj * mult * xc) * ri * (1.0 - ri)
        dxc = adj * mult * ri + _dot(dpa, wat_ref[...]) + _dot(dpi, wxt_ref[...])
        dwa_ref[...] = _dot_tn(xc, dpa)
        dwx_ref[...] = _dot_tn(xc, dpi)
        dxl = dxc * cw_ref[3:4, :]
        dxl = dxl + _shift_up(dxc, 1, row, 0.0) * cw_ref[2:3, :]
        dxl = dxl + _shift_up(dxc, 2, row, 0.0) * cw_ref[1:2, :]
        dxl = dxl + _shift_up(dxc, 3, row, 0.0) * cw_ref[0:1, :]
        dp_ref[0] = dxl.astype(ACT_DTYPE)
        vec_ref[...] = jnp.zeros_like(vec_ref)
        vec_ref[0:1, :] = _rowsum(dpa)
        vec_ref[1:2, :] = _rowsum(dpi)
        vec_ref[2:3, :] = _rowsum(dlog_a * (RG_LRU_C * ra)) * _sigmoid(-lam_ref[...])
        vec_ref[3:4, :] = _rowsum(dxc)
        vec_ref[4:5, :] = _rowsum(dxc * _shift_down(xl, 3, row, 0.0))
        vec_ref[5:6, :] = _rowsum(dxc * _shift_down(xl, 2, row, 0.0))
        vec_ref[6:7, :] = _rowsum(dxc * _shift_down(xl, 1, row, 0.0))
        vec_ref[7:8, :] = _rowsum(dxc * xl)

    col, vec, blk, cws = _lru_specs(S)
    return pl.pallas_call(
        body, name="lru_bwd", grid=(nb,),
        in_specs=[col(0), col(nb), col(0), cws, vec, blk, vec, blk, vec, vec, blk, blk],
        out_specs=[pl.BlockSpec((2, S, LANES), lambda j: (0, 0, j)), blk, blk, pl.BlockSpec((2 * SUBLANES, LANES), lambda j: (0, j))],
        out_shape=[SDS((2, S, W), ACT_DTYPE), SDS((nb, LANES, LANES), F32), SDS((nb, LANES, LANES), F32), SDS((2 * SUBLANES, W), F32)],
        compiler_params=_params("parallel"),
    )(proj, proj, dy, cw, cb, wa, ba, wx, bx, lam, wat, wxt)


def _seg_mean(x, seg_ref, width):
    hi = x.astype(jnp.bfloat16)
    lo = (x - hi.astype(F32)).astype(jnp.bfloat16)
    ones = seg_ref[...]
    s = jnp.dot(hi, ones, preferred_element_type=F32) + jnp.dot(lo, ones, preferred_element_type=F32)
    return s * (1.0 / width)


def _gmlp_core(u_ref, v_ref, gv_ref, seg_ref, ws_ref, bfull_ref, z_scr, hd):
    tm, W = u_ref.shape
    lane = lax.broadcasted_iota(jnp.int32, (CHUNK, LANES), 1)
    ug = _gelu(u_ref[...])
    vg = _gelu(v_ref[...])
    cen = vg - _seg_mean(vg, seg_ref, hd)
    rstd = lax.rsqrt(_seg_mean(cen * cen, seg_ref, hd) + EPS)
    vhat = cen * rstd
    vh = vhat * gv_ref[...]
    vcats = {}
    for ci in range(tm // CHUNK):
        for p in range(W // LANES):
            blk = vh[ci * CHUNK:(ci + 1) * CHUNK, p * LANES:(p + 1) * LANES]
            vcat = jnp.concatenate([jnp.where(lane < hd, blk, 0.0), jnp.where(lane >= hd, blk, 0.0)], axis=0).astype(MXU_DTYPE)
            vcats[ci, p] = vcat
            z_scr[ci * CHUNK:(ci + 1) * CHUNK, p * LANES:(p + 1) * LANES] = (
                jnp.dot(ws_ref[p], vcat, preferred_element_type=F32) + bfull_ref[:, p * LANES:(p + 1) * LANES])
    return ug, vhat, rstd, vcats


def _gmlp_specs(tm, W, nb):
    rows = lambda off: pl.BlockSpec((tm, W), lambda i: (i, off))
    vec = pl.BlockSpec((1, W), lambda i: (0, 0))
    seg = pl.BlockSpec((W, W), lambda i: (0, 0))
    wsp = pl.BlockSpec((nb, CHUNK, 2 * CHUNK), lambda i: (0, 0, 0))
    bfull = pl.BlockSpec((CHUNK, W), lambda i: (0, 0))
    return rows, vec, seg, wsp, bfull


def _gmlp_fwd(proj, ylru, gv, seg, wsp, bfull, g_lru, g_gm):
    S, W = ylru.shape
    nb = W // LANES
    hd = W // HEADS
    tm = _tile(S, 512)

    def body(u_ref, v_ref, yl_ref, gv_ref, seg_ref, ws_ref, bfull_ref, gl_ref, gg_ref, yn_ref, ygm_ref, z_scr):
        ug, _, _, _ = _gmlp_core(u_ref, v_ref, gv_ref, seg_ref, ws_ref, bfull_ref, z_scr, hd)
        ygm = ug * z_scr[...]
        ygm_ref[...] = ygm
        yl = yl_ref[...]
        yn_ref[:, 0:W] = (yl * _rsqrt_ms(yl) * gl_ref[...]).astype(ACT_DTYPE)
        yn_ref[:, W:2 * W] = (ygm * _rsqrt_ms(ygm) * gg_ref[...]).astype(ACT_DTYPE)

    rows, vec, segs, wsps, bfulls = _gmlp_specs(tm, W, nb)
    return pl.pallas_call(
        body, name="gmlp_fwd", grid=(S // tm,),
        in_specs=[rows(2), rows(3), rows(0), vec, segs, wsps, bfulls, vec, vec],
        out_specs=[pl.BlockSpec((tm, 2 * W), lambda i: (i, 0)), rows(0)],
        out_shape=[SDS((S, 2 * W), ACT_DTYPE), SDS((S, W), F32)],
        scratch_shapes=[pltpu.VMEM((tm, W), F32)],
        compiler_params=_params("parallel"),
    )(proj, proj, ylru, gv, seg, wsp, bfull, g_lru, g_gm)


def _rms_bwd(y, g, dyn):
    r = _rsqrt_ms(y)
    yhat = y * r
    dyh = dyn * g
    return r * (dyh - yhat * jnp.mean(dyh * yhat, axis=-1, keepdims=True)), _rowsum(dyn * yhat)


def _gmlp_bwd(proj, ylru, ygm, dyn, gv, seg, wsp, wspt, bfull, g_lru, g_gm):
    S, W = ylru.shape
    nb = W // LANES
    hd = W // HEADS
    tm = _tile(S, 256)

    def body(u_ref, v_ref, yl_ref, ygm_ref, dl_ref, dg_ref, gv_ref, seg_ref, ws_ref, wst_ref, bfull_ref, gl_ref, gg_ref,
             dyl_ref, duv_ref, dws_ref, dbf_ref, acc_ref, z_scr, dvh_scr):
        @pl.when(pl.program_id(0) == 0)
        def _():
            dws_ref[...] = jnp.zeros_like(dws_ref)
            dbf_ref[...] = jnp.zeros_like(dbf_ref)
            acc_ref[...] = jnp.zeros_like(acc_ref)

        dyl, dgl = _rms_bwd(yl_ref[...], gl_ref[...], dl_ref[...])
        dyl_ref[...] = dyl
        dygm, dgg = _rms_bwd(ygm_ref[...], gg_ref[...], dg_ref[...])
        ug, vhat, rstd, vcats = _gmlp_core(u_ref, v_ref, gv_ref, seg_ref, ws_ref, bfull_ref, z_scr, hd)
        duv_ref[0] = (dygm * z_scr[...] * _gelu_grad(u_ref[...])).astype(ACT_DTYPE)
        dz = dygm * ug
        lane = lax.broadcasted_iota(jnp.int32, (CHUNK, LANES), 1)
        dbf = dz[0:CHUNK, :]
        for ci in range(1, tm // CHUNK):
            dbf += dz[ci * CHUNK:(ci + 1) * CHUNK, :]
        dbf_ref[...] += dbf
        for ci in range(tm // CHUNK):
            for p in range(nb):
                dzb = dz[ci * CHUNK:(ci + 1) * CHUNK, p * LANES:(p + 1) * LANES].astype(MXU_DTYPE)
                dws_ref[p] += _dot_nt(dzb, vcats[ci, p])
                dvc = jnp.dot(wst_ref[p], dzb, preferred_element_type=F32)
                dvh_scr[ci * CHUNK:(ci + 1) * CHUNK, p * LANES:(p + 1) * LANES] = jnp.where(lane < hd, dvc[0:CHUNK], dvc[CHUNK:2 * CHUNK])
        dvh = dvh_scr[...]
        dvn = dvh * gv_ref[...]
        dvg = rstd * (dvn - _seg_mean(dvn, seg_ref, hd) - vhat * _seg_mean(dvn * vhat, seg_ref, hd))
        duv_ref[1] = (dvg * _gelu_grad(v_ref[...])).astype(ACT_DTYPE)
        acc_ref[0:1, :] += dgl
        acc_ref[1:2, :] += dgg
        acc_ref[2:3, :] += _rowsum(dvh * vhat)

    rows, vec, segs, wsps, bfulls = _gmlp_specs(tm, W, nb)
    wspt_spec = pl.BlockSpec((nb, 2 * CHUNK, CHUNK), lambda i: (0, 0, 0))
    return pl.pallas_call(
        body, name="gmlp_bwd", grid=(S // tm,),
        in_specs=[rows(2), rows(3), rows(0), rows(0), rows(0), rows(1), vec, segs, wsps, wspt_spec, bfulls, vec, vec],
        out_specs=[rows(0), pl.BlockSpec((2, tm, W), lambda i: (0, i, 0)), wsps, bfulls, pl.BlockSpec((SUBLANES, W), lambda i: (0, 0))],
        out_shape=[SDS((S, W), F32), SDS((2, S, W), ACT_DTYPE), SDS((nb, CHUNK, 2 * CHUNK), F32), SDS((CHUNK, W), F32), SDS((SUBLANES, W), F32)],
        scratch_shapes=[pltpu.VMEM((tm, W), F32), pltpu.VMEM((tm, W), F32)],
        compiler_params=_params("arbitrary"),
    )(proj, proj, ylru, ygm, dyn, dyn, gv, seg, wsp, wspt, bfull, g_lru, g_gm)


def _ada_fwd(c_all, w_ada, b_shard):
    L, D, N = w_ada.shape
    R = c_all.shape[0]
    tn = N // 2

    def body(c_ref, w_ref, b_ref, o_ref):
        cv = c_ref[...]
        o_ref[...] = _dot(cv * _sigmoid(cv), w_ref[...]) + b_ref[...]

    return pl.pallas_call(
        body, name="ada_fwd", grid=(L, N // tn),
        in_specs=[pl.BlockSpec((R, D), lambda l, j: (0, 0)), pl.BlockSpec((None, D, tn), lambda l, j: (l, 0, j)),
                  pl.BlockSpec((None, 1, tn), lambda l, j: (l, 0, j))],
        out_specs=pl.BlockSpec((None, R, tn), lambda l, j: (l, 0, j)),
        out_shape=SDS((L, R, N), F32), compiler_params=_params("parallel", "parallel"),
    )(c_all, w_ada, b_shard)


def _ada_grad(c_all_t, dmod):
    D, B = c_all_t.shape
    L, _, N = dmod.shape
    tn = N // 2

    def body(c_ref, d_ref, o_ref):
        cv = c_ref[...]
        sc = cv * _sigmoid(cv)
        acc = sc[:, 0:1] * d_ref[0:1, :]
        for b in range(1, B):
            acc += sc[:, b:b + 1] * d_ref[b:b + 1, :]
        o_ref[...] = acc

    return pl.pallas_call(
        body, name="ada_grad", grid=(L, N // tn),
        in_specs=[pl.BlockSpec((D, B), lambda l, j: (0, 0)), pl.BlockSpec((None, B, tn), lambda l, j: (l, 0, j))],
        out_specs=pl.BlockSpec((None, D, tn), lambda l, j: (l, 0, j)),
        out_shape=SDS((L, D, N), F32), compiler_params=_params("parallel", "parallel"),
    )(c_all_t, dmod)


def _adamw(w, g, m, v):
    R, C = w.shape
    tr = _row_tile(R, C * 4)

    def body(w_ref, g_ref, m_ref, v_ref, d_ref, mo_ref, vo_ref):
        d_ref[...], mo_ref[...], vo_ref[...] = _adam_math(w_ref[...], g_ref[...], m_ref[...], v_ref[...])

    tile = pl.BlockSpec((tr, C), lambda i: (i, 0))
    return pl.pallas_call(
        body, name=f"adamw_r{R}c{C}", grid=(R // tr,), in_specs=[tile] * 4, out_specs=[tile] * 3,
        out_shape=[SDS((R, C), F32)] * 3, compiler_params=_params("parallel"),
    )(w, g, m, v)


def _adam_math(w, g, m, v):
    mn = ADAM_B1 * m + (1.0 - ADAM_B1) * g
    vn = ADAM_B2 * v + (1.0 - ADAM_B2) * (g * g)
    m_hat = mn / (1.0 - ADAM_B1 ** ADAM_STEP)
    v_hat = vn / (1.0 - ADAM_B2 ** ADAM_STEP)
    return -ADAM_LR * (m_hat / (jnp.sqrt(v_hat) + ADAM_EPS) + ADAM_WD * w), mn, vn


def _adamw_layer(w, g, m, v, l, prev):
    L, R, C = w.shape
    tr = _row_tile(R, C * 4)
    prev = () if prev is None else tuple(prev)

    def body(w_ref, g_ref, m_ref, v_ref, *rest):
        go_ref, d_ref, mo_ref, vo_ref = rest[len(prev):]
        gv = g_ref[...]
        go_ref[...] = gv
        d_ref[...], mo_ref[...], vo_ref[...] = _adam_math(w_ref[...], gv, m_ref[...], v_ref[...])

    lay = pl.BlockSpec((None, tr, C), lambda i: (l, i, 0))
    return pl.pallas_call(
        body, name=f"adamw_layer_r{R}c{C}", grid=(R // tr,),
        in_specs=[lay, pl.BlockSpec((tr, C), lambda i: (i, 0)), lay, lay] + [ANY] * len(prev), out_specs=[lay] * 4,
        out_shape=[SDS((L, R, C), F32)] * 4, input_output_aliases={4 + k: k for k in range(len(prev))},
        compiler_params=_params("parallel"),
    )(w, g, m, v, *prev)


def _sum_leading(a):
    P, R, C = a.shape
    tr = _row_tile(R, P * C * 4)

    def body(a_ref, o_ref):
        acc = a_ref[0]
        for p in range(1, P):
            acc = acc + a_ref[p]
        o_ref[...] = acc

    return pl.pallas_call(
        body, name=f"sum{P}_r{R}c{C}", grid=(R // tr,),
        in_specs=[pl.BlockSpec((P, tr, C), lambda i: (0, i, 0))],
        out_specs=pl.BlockSpec((tr, C), lambda i: (i, 0)),
        out_shape=SDS((R, C), F32), compiler_params=_params("parallel"),
    )(a)


def _add_half(g4, r1, place):
    _, _, R, C = g4.shape
    tr = _row_tile(R, C * 4)

    def body(place_ref, g_ref, r_ref, h_ref, own_ref):
        s = (g_ref[...] + r_ref[...]).astype(XFER_DTYPE)
        h_ref[...] = s

        @pl.when(pl.program_id(1) == place_ref[1])
        def _():
            own_ref[...] = s

    return pl.pallas_call(
        body, name=f"add_half_r{R}c{C}",
        grid_spec=pltpu.PrefetchScalarGridSpec(
            num_scalar_prefetch=1, grid=(R // tr, N_CHIPS),
            in_specs=[pl.BlockSpec((None, None, tr, C), lambda i, p, place_ref: (p, place_ref[0], i, 0)),
                      pl.BlockSpec((None, tr, C), lambda i, p, place_ref: (p, i, 0))],
            out_specs=[pl.BlockSpec((None, tr, C), lambda i, p, place_ref: (p, i, 0)),
                       pl.BlockSpec((None, tr, C), lambda i, p, place_ref: (place_ref[1], i, 0))],
        ),
        out_shape=[SDS((N_CHIPS, R, C), XFER_DTYPE)] * 2, compiler_params=_params("parallel", "arbitrary"),
    )(place, g4, r1)


def _sum4_into_half(r2, place):
    P, R, C = r2.shape
    tr = _row_tile(R, P * C * 4)

    def body(place_ref, a_ref, o_ref):
        acc = a_ref[0].astype(F32)
        for p in range(1, P):
            acc = acc + a_ref[p].astype(F32)
        o_ref[...] = acc

    return pl.pallas_call(
        body, name=f"sum4_r{R}c{C}",
        grid_spec=pltpu.PrefetchScalarGridSpec(
            num_scalar_prefetch=1, grid=(R // tr,),
            in_specs=[pl.BlockSpec((P, tr, C), lambda i, place_ref: (0, i, 0))],
            out_specs=pl.BlockSpec((None, tr, C), lambda i, place_ref: (place_ref[0], i, 0)),
        ),
        out_shape=SDS((2, R, C), F32), compiler_params=_params("parallel"),
    )(place, r2)


def _cast_into_slot(w, l, place):
    _, R, C = w.shape
    tr = _row_tile(R, C * 4)

    def body(place_ref, w_ref, o_ref):
        o_ref[...] = w_ref[...].astype(MXU_DTYPE)

    return pl.pallas_call(
        body, name=f"cast_r{R}c{C}",
        grid_spec=pltpu.PrefetchScalarGridSpec(
            num_scalar_prefetch=1, grid=(R // tr,),
            in_specs=[pl.BlockSpec((None, tr, C), lambda i, place_ref: (l, i, 0))],
            out_specs=pl.BlockSpec((None, tr, C), lambda i, place_ref: (place_ref[1], i, 0)),
        ),
        out_shape=SDS((N_CHIPS, R, C), MXU_DTYPE), compiler_params=_params("parallel"),
    )(place, w)


def _place():
    x, y, c = lax.axis_index("x"), lax.axis_index("y"), lax.axis_index("c")
    chips = [(1 - x, y), (x, 1 - y), (1 - x, 1 - y)]
    return x, y, c, chips


def _remote(src, dst, send_sem, recv_sem, to):
    return pltpu.make_async_remote_copy(src_ref=src, dst_ref=dst, send_sem=send_sem, recv_sem=recv_sem, device_id=to, device_id_type=MESH)


def _all_gather8(v):
    R, N = v.shape

    def body(v_ref, out_ref, send_sems, recv_sems, local_sem):
        x, y, c, chips = _place()
        me, sibling = (x, y, c), (x, y, 1 - c)

        def slot(px, py, pc):
            return out_ref.at[4 * px + 2 * py + pc]

        def copy(k, block, to, src=None):
            return _remote(slot(*block) if src is None else src, slot(*block), send_sems.at[k], recv_sems.at[k], to)

        mine = pltpu.make_async_copy(v_ref, slot(*me), local_sem)
        mine.start()
        first = [copy(0, me, sibling, src=v_ref)] + [copy(1 + j, me, (*chip, c), src=v_ref) for j, chip in enumerate(chips)]
        for cp in first:
            cp.start()
        passed = [copy(4 + j, (*chip, c), sibling) for j, chip in enumerate(chips)]
        for j, chip in enumerate(chips):
            copy(1 + j, (*chip, c), me).wait_recv()
            passed[j].start()
        copy(0, sibling, me).wait_recv()
        for j, chip in enumerate(chips):
            copy(4 + j, (*chip, 1 - c), me).wait_recv()
        for cp in first + passed:
            cp.wait_send()
        mine.wait()

    return pl.pallas_call(
        body, name=f"all_gather8_r{R}n{N}", out_shape=SDS((N_DEV, R, N), v.dtype), in_specs=[ANY], out_specs=ANY,
        scratch_shapes=[pltpu.SemaphoreType.DMA((7,)), pltpu.SemaphoreType.DMA((7,)), pltpu.SemaphoreType.DMA],
    )(v)


def _gather_weights(slots):
    n = len(slots)

    def body(*refs):
        ins, outs = refs[:n], refs[n:2 * n]
        send_sems, recv_sems = refs[2 * n:]
        x, y, c, chips = _place()
        q = 2 * x + y
        sibling = (x, y, 1 - c)
        first = []
        for k in range(n):
            for j, chip in enumerate(chips):
                first.append(_remote(ins[k].at[q, c], outs[k].at[q, c], send_sems.at[k, j], recv_sems.at[k, j], (*chip, c)))
                first[-1].start()
        passed = []
        for k in range(n):
            for j, chip in enumerate(chips):
                half = outs[k].at[2 * chip[0] + chip[1], c]
                _remote(half, half, send_sems.at[k, j], recv_sems.at[k, j], sibling).wait_recv()
                passed.append(_remote(half, half, send_sems.at[k, 3 + j], recv_sems.at[k, 3 + j], sibling))
                passed[-1].start()
        for k in range(n):
            for j, chip in enumerate(chips):
                half = outs[k].at[2 * chip[0] + chip[1], 1 - c]
                _remote(half, half, send_sems.at[k, 3 + j], recv_sems.at[k, 3 + j], sibling).wait_recv()
        for cp in first + passed:
            cp.wait_send()

    return pl.pallas_call(
        body, name="gather_weights", out_shape=[SDS(s.shape, s.dtype) for s in slots],
        in_specs=[ANY] * n, out_specs=[ANY] * n, input_output_aliases={k: k for k in range(n)},
        scratch_shapes=[pltpu.SemaphoreType.DMA((n, 6)), pltpu.SemaphoreType.DMA((n, 6))],
    )(*slots)


def _swap_halves(g4s):
    n = len(g4s)

    def body(*refs):
        ins, outs = refs[:n], refs[n:2 * n]
        send_sems, recv_sems = refs[2 * n:]
        x, y, c, _ = _place()
        sibling = (x, y, 1 - c)
        for k in range(n):
            for p in range(N_CHIPS):
                _remote(ins[k].at[p, 1 - c], outs[k].at[p], send_sems.at[k], recv_sems.at[k], sibling).start()
        for k in range(n):
            _remote(outs[k], outs[k], send_sems.at[k], recv_sems.at[k], sibling).wait()

    return pl.pallas_call(
        body, name="swap_halves", out_shape=[SDS((N_CHIPS,) + g.shape[2:], g.dtype) for g in g4s],
        in_specs=[ANY] * n, out_specs=[ANY] * n,
        scratch_shapes=[pltpu.SemaphoreType.DMA((n,)), pltpu.SemaphoreType.DMA((n,))],
    )(*g4s)


def _scatter_regions(hs, lands):
    n = len(hs)

    def body(*refs):
        ins, outs = refs[:n], refs[2 * n:3 * n]
        send_sems, recv_sems = refs[3 * n:]
        x, y, c, chips = _place()
        q = 2 * x + y
        sent = []
        for k in range(n):
            for j, chip in enumerate(chips):
                sent.append(_remote(ins[k].at[2 * chip[0] + chip[1]], outs[k].at[q], send_sems.at[k, j], recv_sems.at[k, j], (*chip, c)))
                sent[-1].start()
        for k in range(n):
            for j, chip in enumerate(chips):
                got = outs[k].at[2 * chip[0] + chip[1]]
                _remote(got, got, send_sems.at[k, j], recv_sems.at[k, j], (x, y, c)).wait_recv()
        for cp in sent:
            cp.wait_send()

    return pl.pallas_call(
        body, name="scatter_regions", out_shape=[SDS(h.shape, h.dtype) for h in lands],
        in_specs=[ANY] * (2 * n), out_specs=[ANY] * n, input_output_aliases={n + k: k for k in range(n)},
        scratch_shapes=[pltpu.SemaphoreType.DMA((n, 3)), pltpu.SemaphoreType.DMA((n, 3))],
    )(*hs, *lands)


def _share_halves(fins):
    n = len(fins)

    def body(*refs):
        ins, outs = refs[:n], refs[n:2 * n]
        send_sems, recv_sems = refs[2 * n:]
        x, y, c, _ = _place()
        sibling = (x, y, 1 - c)
        sent = [_remote(ins[k].at[c], outs[k].at[c], send_sems.at[k], recv_sems.at[k], sibling) for k in range(n)]
        for cp in sent:
            cp.start()
        for k in range(n):
            got = outs[k].at[1 - c]
            _remote(got, got, send_sems.at[k], recv_sems.at[k], sibling).wait_recv()
        for cp in sent:
            cp.wait_send()

    return pl.pallas_call(
        body, name="share_halves", out_shape=[SDS(t.shape, t.dtype) for t in fins],
        in_specs=[ANY] * n, out_specs=[ANY] * n, input_output_aliases={k: k for k in range(n)},
        scratch_shapes=[pltpu.SemaphoreType.DMA((n,)), pltpu.SemaphoreType.DMA((n,))],
    )(*fins)


def _chip_sums(grads, place):
    g4s = [g.reshape(N_CHIPS, 2, g.shape[1] // 2, g.shape[2]) for g in grads]
    pairs = [_add_half(g4, r1, place) for g4, r1 in zip(g4s, _swap_halves(g4s))]
    return [h for h, _ in pairs], [own for _, own in pairs]


def _reduce_finish(lands, place):
    fins = _share_halves([_sum4_into_half(r2, place) for r2 in lands])
    return [f.reshape(2 * f.shape[1], f.shape[2]) for f in fins]


def _reduce_scatter(grads, place):
    hs, lands = _chip_sums(grads, place)
    return _reduce_finish(_scatter_regions(hs, lands), place)


HBM_SPEC = pl.BlockSpec(memory_space=pltpu.HBM)
SEM_SPEC = pl.BlockSpec(memory_space=pltpu.SEMAPHORE)
DATAFLOW = pltpu.SideEffectType.DATAFLOW_SIDE_EFFECTING


def _in_hbm(a):
    return pltpu.with_memory_space_constraint(a, pltpu.HBM)


def _hbm_like(a):
    return pltpu.HBM(a.shape, a.dtype)


def _gather_ici_start(slots, thru, after):
    n = len(slots)

    def body(*refs):
        ins = refs[:n]
        send_sems, recv_sems = refs[n + 2], refs[n + 3]
        x, y, c, chips = _place()
        q = 2 * x + y
        for k in range(n):
            for j, chip in enumerate(chips):
                _remote(ins[k].at[q, c], ins[k].at[q, c], send_sems.at[3 * k + j], recv_sems.at[3 * k + j], (*chip, c)).start()

    out = pl.pallas_call(
        body, name="gather_ici_start",
        out_shape=(pltpu.SemaphoreType.DMA((3 * n,)), pltpu.SemaphoreType.DMA((3 * n,)), *[_hbm_like(s) for s in slots], _hbm_like(thru)),
        in_specs=[HBM_SPEC] * (n + 1) + [ANY], out_specs=(SEM_SPEC, SEM_SPEC, *[HBM_SPEC] * (n + 1)),
        input_output_aliases={k: 2 + k for k in range(n + 1)},
        compiler_params=pltpu.CompilerParams(has_side_effects=DATAFLOW),
    )(*[_in_hbm(s) for s in slots], _in_hbm(thru), after)
    return out[0], out[1], list(out[2:2 + n]), out[2 + n]


def _gather_ici_wait(send_sems, recv_sems, slots, after):
    n = len(slots)

    def body(*refs):
        ins = refs[:n]
        send_sems, recv_sems = refs[n], refs[n + 1]
        x, y, c, chips = _place()
        for k in range(n):
            for j, chip in enumerate(chips):
                got = ins[k].at[2 * chip[0] + chip[1], c]
                cp = _remote(got, got, send_sems.at[3 * k + j], recv_sems.at[3 * k + j], (x, y, c))
                cp.wait_send()
                cp.wait_recv()

    out = pl.pallas_call(
        body, name="gather_ici_wait", out_shape=[_hbm_like(s) for s in slots],
        in_specs=[HBM_SPEC] * n + [SEM_SPEC, SEM_SPEC, ANY], out_specs=[HBM_SPEC] * n,
        input_output_aliases={k: k for k in range(n)},
        compiler_params=pltpu.CompilerParams(has_side_effects=DATAFLOW),
    )(*slots, send_sems, recv_sems, after)
    return list(out)


def _gather_pass_on(slots):
    n = len(slots)

    def body(*refs):
        ins, outs = refs[:n], refs[n:2 * n]
        send_sems, recv_sems = refs[2 * n:]
        x, y, c, chips = _place()
        sibling = (x, y, 1 - c)
        passed = []
        for k in range(n):
            for j, chip in enumerate(chips):
                passed.append(_remote(ins[k].at[2 * chip[0] + chip[1], c], outs[k].at[2 * chip[0] + chip[1], c],
                                      send_sems.at[k, j], recv_sems.at[k, j], sibling))
                passed[-1].start()
        for k in range(n):
            for j, chip in enumerate(chips):
                half = outs[k].at[2 * chip[0] + chip[1], 1 - c]
                _remote(half, half, send_sems.at[k, j], recv_sems.at[k, j], sibling).wait_recv()
        for cp in passed:
            cp.wait_send()

    return pl.pallas_call(
        body, name="gather_pass_on", out_shape=[SDS(s.shape, s.dtype) for s in slots],
        in_specs=[ANY] * n, out_specs=[ANY] * n, input_output_aliases={k: k for k in range(n)},
        scratch_shapes=[pltpu.SemaphoreType.DMA((n, 3)), pltpu.SemaphoreType.DMA((n, 3))],
    )(*slots)


def _scatter_start(hs, lands, thru):
    n = len(hs)

    def body(*refs):
        ins, zones = refs[:n], refs[n:2 * n]
        send_sems, recv_sems = refs[2 * n + 1], refs[2 * n + 2]
        x, y, c, chips = _place()
        q = 2 * x + y
        for k in range(n):
            for j, chip in enumerate(chips):
                _remote(ins[k].at[2 * chip[0] + chip[1]], zones[k].at[q], send_sems.at[3 * k + j], recv_sems.at[3 * k + j], (*chip, c)).start()

    arrays = [*hs, *lands, thru]
    out = pl.pallas_call(
        body, name="scatter_start",
        out_shape=(pltpu.SemaphoreType.DMA((3 * n,)), pltpu.SemaphoreType.DMA((3 * n,)), *[_hbm_like(a) for a in arrays]),
        in_specs=[HBM_SPEC] * len(arrays), out_specs=(SEM_SPEC, SEM_SPEC, *[HBM_SPEC] * len(arrays)),
        input_output_aliases={k: 2 + k for k in range(len(arrays))},
        compiler_params=pltpu.CompilerParams(has_side_effects=DATAFLOW),
    )(*[_in_hbm(a) for a in arrays])
    return out[0], out[1], list(out[2:2 + n]), list(out[2 + n:2 + 2 * n]), out[2 + 2 * n]


def _scatter_wait(send_sems, recv_sems, hs, lands, after):
    n = len(hs)

    def body(*refs):
        ins, zones = refs[:n], refs[n:2 * n]
        send_sems, recv_sems = refs[2 * n], refs[2 * n + 1]
        x, y, c, chips = _place()
        for k in range(n):
            for j, chip in enumerate(chips):
                p = 2 * chip[0] + chip[1]
                cp = _remote(ins[k].at[p], zones[k].at[p], send_sems.at[3 * k + j], recv_sems.at[3 * k + j], (x, y, c))
                cp.wait_send()
                cp.wait_recv()

    out = pl.pallas_call(
        body, name="scatter_wait", out_shape=[_hbm_like(a) for a in [*hs, *lands]],
        in_specs=[HBM_SPEC] * (2 * n) + [SEM_SPEC, SEM_SPEC, ANY], out_specs=[HBM_SPEC] * (2 * n),
        input_output_aliases={k: k for k in range(2 * n)},
        compiler_params=pltpu.CompilerParams(has_side_effects=DATAFLOW),
    )(*hs, *lands, send_sems, recv_sems, after)
    return list(out[n:])


def _pair_blocks(w):
    h, d, _ = w.shape
    z = jnp.zeros((h // 2, d, d), w.dtype)
    return jnp.concatenate([jnp.concatenate([w[0::2], z], axis=2), jnp.concatenate([z, w[1::2]], axis=2)], axis=1)


def _unpair_blocks(b):
    n, dd, _ = b.shape
    d = dd // 2
    return jnp.stack([b[:, :d, :d], b[:, d:, d:]], axis=1).reshape(2 * n, d, d)


def _pad_rows(a, rows):
    return jnp.pad(a, ((0, rows - a.shape[0]), (0, 0)))


class _Packer:
    def __init__(self, shapes, width=1024, row_multiple=64):
        self.shapes = shapes
        self.sizes = [math.prod(s) for s in shapes]
        total = sum(self.sizes)
        self.width = width
        self.rows = -(-total // (width * row_multiple)) * row_multiple
        self.pad = self.rows * width - total

    def pack(self, arrays):
        flat = jnp.concatenate([a.reshape(-1).astype(F32) for a in arrays] + [jnp.zeros((self.pad,), F32)])
        return flat.reshape(self.rows, self.width)

    def unpack(self, packed):
        flat = packed.reshape(-1)
        out, off = [], 0
        for s, n in zip(self.shapes, self.sizes):
            out.append(flat[off:off + n].reshape(s))
            off += n
        return out


SMALL = ["b_ada", "ffn1_norm", "mix_norm", "conv_w", "conv_b", "gate_a_w", "gate_a_b", "gate_x_w", "gate_x_b", "lru_lambda",
         "v_norm", "spatial_w", "spatial_b", "lru_out_norm", "gmlp_out_norm", "ffn2_norm", "final_norm"]
BIG = ["ffn1_w_gu", "ffn1_w_down", "w_in", "w_out", "ffn2_w_gu", "ffn2_w_down"]
WEIGHTS = ["w_ada", "b_ada", "ffn1_norm", "ffn1_w_gu", "ffn1_w_down", "mix_norm", "w_in", "conv_w", "conv_b", "gate_a_w", "gate_a_b",
           "gate_x_w", "gate_x_b", "lru_lambda", "v_norm", "spatial_w", "spatial_b", "lru_out_norm", "gmlp_out_norm", "w_out",
           "ffn2_norm", "ffn2_w_gu", "ffn2_w_down", "final_norm"]


def kernel(x, c, w_ada, b_ada, ffn1_norm, ffn1_w_gu, ffn1_w_down, mix_norm, w_in, conv_w, conv_b, gate_a_w, gate_a_b, gate_x_w, gate_x_b, lru_lambda, v_norm, spatial_w, spatial_b, lru_out_norm, gmlp_out_norm, w_out, ffn2_norm, ffn2_w_gu, ffn2_w_down, final_norm, loss_target, m_w_ada, m_b_ada, m_ffn1_norm, m_ffn1_w_gu, m_ffn1_w_down, m_mix_norm, m_w_in, m_conv_w, m_conv_b, m_gate_a_w, m_gate_a_b, m_gate_x_w, m_gate_x_b, m_lru_lambda, m_v_norm, m_spatial_w, m_spatial_b, m_lru_out_norm, m_gmlp_out_norm, m_w_out, m_ffn2_norm, m_ffn2_w_gu, m_ffn2_w_down, m_final_norm, v_w_ada, v_b_ada, v_ffn1_norm, v_ffn1_w_gu, v_ffn1_w_down, v_mix_norm, v_w_in, v_conv_w, v_conv_b, v_gate_a_w, v_gate_a_b, v_gate_x_w, v_gate_x_b, v_lru_lambda, v_v_norm, v_spatial_w, v_spatial_b, v_lru_out_norm, v_gmlp_out_norm, v_w_out, v_ffn2_norm, v_ffn2_w_gu, v_ffn2_w_down, v_final_norm):
    given = dict(locals())
    W = {n: given[n] for n in WEIGHTS}
    L = w_ada.shape[0]
    S, D = x.shape[1], x.shape[2]
    LW = conv_b.shape[1]
    hd = LW // HEADS
    xi, yi, ci = lax.axis_index("x"), lax.axis_index("y"), lax.axis_index("c")
    chip = 2 * xi + yi
    dev = 2 * chip + ci
    place = jnp.stack([ci, chip]).astype(jnp.int32)
    xs = x.reshape(S, D)
    tgt = loss_target.reshape(S, D)

    c_all = _all_gather8(_pad_rows(c, SUBLANES))[:, 0, :]
    n_ada = w_ada.shape[2]
    b_shard = lax.dynamic_slice_in_dim(b_ada, chip * n_ada, n_ada, axis=1)
    mod_shard = _ada_fwd(_pad_rows(c_all, 2 * SUBLANES), w_ada, b_shard[:, None, :])
    mod_all = _all_gather8(mod_shard.reshape(L * 2 * SUBLANES, n_ada))
    mod_rows = lax.dynamic_index_in_dim(mod_all.reshape(N_CHIPS, 2, L, 2 * SUBLANES, n_ada)[:, 0], dev, axis=2, keepdims=False)
    mod = mod_rows.transpose(1, 0, 2).reshape(L, N_MOD, 1, D)

    def half_view(s):
        return s.reshape(N_CHIPS, 2, s.shape[1] // 2, s.shape[2])

    slots = [[half_view(_cast_into_slot(W[n], l, place)) for n in BIG] for l in range(L)]
    slots[0] = _gather_weights(slots[0])

    def layer_weights(l):
        w = {n: s.reshape(N_CHIPS, -1, s.shape[3]) for n, s in zip(BIG, slots[l])}
        return dict(
            gu1=w["ffn1_w_gu"], d1=w["ffn1_w_down"].reshape(-1, D), win=w["w_in"],
            wout=w["w_out"].reshape(-1, D), gu2=w["ffn2_w_gu"], d2=w["ffn2_w_down"].reshape(-1, D))

    cws = LW // N_CHIPS
    conv_all = _all_gather8(_pad_rows(conv_w.reshape(L * CONV_WIDTH, cws), -(-L * CONV_WIDTH // SUBLANES) * SUBLANES))
    conv_full = conv_all.reshape(N_CHIPS, 2, -1, cws)[:, 0, :L * CONV_WIDTH].reshape(N_CHIPS, L, CONV_WIDTH, cws)
    conv_full = conv_full.transpose(1, 2, 0, 3).reshape(L, CONV_WIDTH, LW)

    tril = jnp.tril(jnp.ones((CHUNK, CHUNK), F32))
    seg = (jnp.arange(LW)[:, None] // hd == jnp.arange(LW)[None, :] // hd).astype(jnp.bfloat16)

    def mixer_params(l):
        ws = spatial_w[l] * tril
        wsp = jnp.concatenate([ws[0::2], ws[1::2]], axis=2)
        wa, wx = _pair_blocks(gate_a_w[l]), _pair_blocks(gate_x_w[l])
        return dict(
            cw=conv_full[l], cb=conv_b[l][None],
            wa=wa.astype(MXU_DTYPE), wx=wx.astype(MXU_DTYPE), wat=wa.transpose(0, 2, 1).astype(MXU_DTYPE), wxt=wx.transpose(0, 2, 1).astype(MXU_DTYPE),
            ba=gate_a_b[l].reshape(1, LW), bx=gate_x_b[l].reshape(1, LW), lam=lru_lambda[l][None], gv=v_norm[l][None],
            wsp=wsp.astype(MXU_DTYPE), wspt=wsp.transpose(0, 2, 1).astype(MXU_DTYPE),
            bfull=jnp.repeat(spatial_b[l].T, hd, axis=1), g_lru=lru_out_norm[l][None], g_gm=gmlp_out_norm[l][None])

    saved = []
    xcur = xs
    for l in range(L):
        lw, mp, md = layer_weights(l), mixer_params(l), mod[l]
        s = dict(lw=lw, mp=mp, md=md)
        s["x0"] = xcur
        s["h1"] = _modnorm(xcur, ffn1_norm[l][None], md[0], md[1])
        if l + 1 < L:
            send_sems, recv_sems, slots[l + 1], s["h1"] = _gather_ici_start(slots[l + 1], s["h1"], slots[l][0])
        s["a1"], s["gu1"] = _ffn_up(s["h1"], lw["gu1"])
        s["f1"], xcur = _mm_res(s["a1"], lw["d1"], xcur, md[2], 0.5)
        s["x1"] = xcur
        s["h2"] = _modnorm(xcur, mix_norm[l][None], md[3], md[4])
        s["proj"] = _mm_chunks(s["h2"], lw["win"])
        s["ylru"] = _lru_fwd(s["proj"], mp["cw"], mp["cb"], mp["wa"], mp["ba"], mp["wx"], mp["bx"], mp["lam"])
        s["yn"], s["ygm"] = _gmlp_fwd(s["proj"], s["ylru"], mp["gv"], seg, mp["wsp"], mp["bfull"], mp["g_lru"], mp["g_gm"])
        s["f2"], xcur = _mm_res(s["yn"], lw["wout"], xcur, md[5], 1.0)
        s["x2"] = xcur
        s["h3"] = _modnorm(xcur, ffn2_norm[l][None], md[6], md[7])
        s["a3"], s["gu3"] = _ffn_up(s["h3"], lw["gu2"])
        s["f3"], xcur = _mm_res(s["a3"], lw["d2"], xcur, md[8], 0.5)
        saved.append(s)
        if l + 1 < L:
            slots[l + 1] = _gather_pass_on(_gather_ici_wait(send_sems, recv_sems, slots[l + 1], xcur))

    dx, dq, head_acc = _loss_head(xcur, tgt, final_norm[None], saved[-1]["md"][8], 0.5)
    loss = lax.psum(jnp.sum(head_acc[1]), ("x", "y", "c"))
    small_grads = {}
    big_grads = {n: [None] * L for n in BIG}
    dmods = [None] * L
    zero_row = jnp.zeros((1, D), F32)

    def ffn_bwd(dx, dq, x_in, h, a, gu, f, wgu, wd, gn, sc, next_gate, next_scale):
        d_wd = _mm_tn_chunks(a, dq[None], 1408, 1024)[0]
        dgu = _ffn_bwd_act(dq, wd, gu)
        C = dgu.shape[3]
        dgu4 = dgu.reshape(N_CHIPS, S, C)
        d_wgu = _mm_tn_chunks(h, dgu4, 1024, C)
        dh = _mm_nt_chunks(dgu4, wgu)
        dx, dq, acc = _norm_bwd(x_in, dh, dx, f, gn, sc, 0.5, next_gate, next_scale)
        return dx, dq, acc, d_wgu, d_wd.reshape(N_CHIPS, -1, D)

    stepped = {n: None for n in BIG}

    def step_layer(l, reduced):
        for n, g in zip(BIG, reduced):
            stepped[n] = _adamw_layer(W[n], g, given["m_" + n], given["v_" + n], l, stepped[n])

    in_flight = None
    for l in reversed(range(L)):
        s = saved[l]
        lw, mp, md = s["lw"], s["mp"], s["md"]
        dx, dq, acc3, big_grads["ffn2_w_gu"][l], big_grads["ffn2_w_down"][l] = ffn_bwd(
            dx, dq, s["x2"], s["h3"], s["a3"], s["gu3"], s["f3"], lw["gu2"], lw["d2"], ffn2_norm[l][None], md[7], md[5], 1.0)
        big_grads["w_out"][l] = _mm_tn_chunks(s["yn"], dq[None], 1024, 1024)[0].reshape(N_CHIPS, -1, D)
        dyn = _mm_nt_chunks(dq[None], lw["wout"][None])
        dylru, duv, dwsp, dbfull, gacc = _gmlp_bwd(s["proj"], s["ylru"], s["ygm"], dyn, mp["gv"], seg, mp["wsp"], mp["wspt"], mp["bfull"], mp["g_lru"], mp["g_gm"])
        dxg, dwa, dwx, lvec = _lru_bwd(s["proj"], dylru, mp["cw"], mp["cb"], mp["wa"], mp["ba"], mp["wx"], mp["bx"], mp["lam"], mp["wat"], mp["wxt"])
        dproj = jnp.concatenate([dxg, duv], axis=0)
        big_grads["w_in"][l] = _mm_tn_chunks(s["h2"], dproj, 1024, LW)
        dh2 = _mm_nt_chunks(dproj, lw["win"])
        dx, dq, acc2 = _norm_bwd(s["x1"], dh2, dx, s["f2"], mix_norm[l][None], md[4], 1.0, md[2], 0.5)
        if l > 0:
            ng, ns = saved[l - 1]["md"][8], 0.5
        else:
            ng, ns = zero_row, 0.0
        dx, dq, acc1, big_grads["ffn1_w_gu"][l], big_grads["ffn1_w_down"][l] = ffn_bwd(
            dx, dq, s["x0"], s["h1"], s["a1"], s["gu1"], s["f1"], lw["gu1"], lw["d1"], ffn1_norm[l][None], md[1], ng, ns)

        dmods[l] = jnp.concatenate([acc1[0:2], acc1[3:4], acc2[0:2], acc2[3:4], acc3[0:2], acc3[3:4]], axis=0)
        dws = jnp.stack([dwsp[:, :, :CHUNK], dwsp[:, :, CHUNK:]], axis=1).reshape(HEADS, CHUNK, CHUNK) * tril
        lg = {"ffn1_norm": acc1[2], "mix_norm": acc2[2], "ffn2_norm": acc3[2],
              "conv_w": lvec[4:8], "conv_b": lvec[3], "gate_a_w": _unpair_blocks(dwa), "gate_a_b": lvec[0].reshape(HEADS, hd),
              "gate_x_w": _unpair_blocks(dwx), "gate_x_b": lvec[1].reshape(HEADS, hd), "lru_lambda": lvec[2], "v_norm": gacc[2],
              "spatial_w": dws, "spatial_b": dbfull.reshape(CHUNK, HEADS, hd).sum(-1).T, "lru_out_norm": gacc[0], "gmlp_out_norm": gacc[1]}
        for n, g in lg.items():
            small_grads.setdefault(n, [None] * L)[l] = g

        if in_flight is not None:
            step_layer(l + 1, _reduce_finish(_scatter_wait(*in_flight, dx), place))
            in_flight = None
        layer_grads = [big_grads[n][l] for n in BIG]
        if l > 0:
            hs, lands = _chip_sums(layer_grads, place)
            *in_flight, dq = _scatter_start(hs, lands, dq)
        else:
            step_layer(l, _reduce_scatter(layer_grads, place))
    grad_x = dx.reshape(x.shape)

    per_layer = [n for n in SMALL if n not in ("b_ada", "final_norm")]
    part = [jnp.stack(small_grads[n]) for n in per_layer] + [head_acc[0], jnp.stack(dmods)]
    packer = _Packer([p.shape for p in part])
    gathered = _all_gather8(packer.pack(part))
    summed = packer.unpack(_sum_leading(gathered))
    grads = dict(zip(per_layer + ["final_norm"], summed[:-1]))
    grads["b_ada"] = summed[-1].reshape(L, N_MOD * D)
    off = sum(packer.sizes[:-1])
    dmod_rows = gathered.reshape(N_DEV, -1)[:, off:off + L * N_MOD * D].reshape(N_DEV, L, N_MOD * D)
    dmod_shard = lax.dynamic_slice_in_dim(dmod_rows, chip * n_ada, n_ada, axis=2).transpose(1, 0, 2)
    grads["w_ada"] = _ada_grad(c_all.T, dmod_shard)
    grads["conv_w"] = lax.dynamic_slice_in_dim(grads["conv_w"], chip * cws, cws, axis=2)

    delta, new_m, new_v = {}, {}, {}
    for n in BIG:
        grads[n], delta[n], new_m[n], new_v[n] = stepped[n]
    shp = w_ada.shape
    d_, m_, v_ = _adamw(*[a.reshape(-1, shp[-1]) for a in (w_ada, grads["w_ada"], m_w_ada, v_w_ada)])
    delta["w_ada"], new_m["w_ada"], new_v["w_ada"] = d_.reshape(shp), m_.reshape(shp), v_.reshape(shp)
    spk = _Packer([W[n].shape for n in SMALL])
    d_, m_, v_ = _adamw(spk.pack([W[n] for n in SMALL]), spk.pack([grads[n] for n in SMALL]),
                        spk.pack([given["m_" + n] for n in SMALL]), spk.pack([given["v_" + n] for n in SMALL]))
    for n, a, b, e in zip(SMALL, spk.unpack(d_), spk.unpack(m_), spk.unpack(v_)):
        delta[n], new_m[n], new_v[n] = a, b, e
    grads = {n: grads[n].reshape(W[n].shape) for n in WEIGHTS}
    return (loss, grad_x, *[grads[n] for n in WEIGHTS], *[delta[n] for n in WEIGHTS], *[new_m[n] for n in WEIGHTS], *[new_v[n] for n in WEIGHTS])
```

```python
import math

import jax
import jax.numpy as jnp
from jax import lax
from jax.experimental import pallas as pl
from jax.experimental.pallas import tpu as pltpu

F32 = jnp.float32
MXU_DTYPE = jnp.bfloat16
ACT_DTYPE = jnp.bfloat16
XFER_DTYPE = jnp.bfloat16
EPS = 1e-6
RG_LRU_C = 8.0
N_MOD = 9
CONV_WIDTH = 4
HEADS = 8
CHUNK = 128
LANES = 128
SUBLANES = 8
N_CHIPS = 4
N_DEV = 8
ADAM_LR, ADAM_B1, ADAM_B2, ADAM_EPS, ADAM_WD, ADAM_STEP = 0.001, 0.9, 0.999, 1e-08, 0.01, 10
VMEM_LIMIT_BYTES = 60 * 1024 * 1024
ROW_TILE_BYTES = 1 << 20
GELU_C = math.sqrt(2.0 / math.pi)
GELU_A = 0.044715

ANY = pl.BlockSpec(memory_space=pl.ANY)
MESH = pl.DeviceIdType.MESH
SDS = jax.ShapeDtypeStruct


def _params(*sem):
    return pltpu.CompilerParams(dimension_semantics=sem, vmem_limit_bytes=VMEM_LIMIT_BYTES)


def _dot(a, b):
    return jnp.dot(a.astype(MXU_DTYPE), b.astype(MXU_DTYPE), preferred_element_type=F32)


def _dot_nt(a, b):
    return lax.dot_general(a.astype(MXU_DTYPE), b.astype(MXU_DTYPE), (((1,), (1,)), ((), ())), preferred_element_type=F32)


def _dot_tn(a, b):
    return lax.dot_general(a.astype(MXU_DTYPE), b.astype(MXU_DTYPE), (((0,), (0,)), ((), ())), preferred_element_type=F32)


def _gelu(x):
    return x * (0.5 * (1.0 + jnp.tanh(GELU_C * (x + GELU_A * (x * x * x)))))


def _gelu_grad(x):
    t = jnp.tanh(GELU_C * (x + GELU_A * (x * x * x)))
    return 0.5 * (1.0 + t) + 0.5 * x * (1.0 - t * t) * (GELU_C * (1.0 + 3.0 * GELU_A * x * x))


def _sigmoid(x):
    return jax.nn.sigmoid(x)


def _rsqrt_ms(x):
    return lax.rsqrt(jnp.mean(x * x, axis=-1, keepdims=True) + EPS)


def _rowsum(x):
    return jnp.sum(x, axis=0, keepdims=True)


def _tile(n, want):
    t = min(n, want)
    assert n % t == 0, (n, want)
    return t


def _row_tile(rows, row_bytes):
    step = 2 * SUBLANES
    cap = max(step, ROW_TILE_BYTES // row_bytes)
    best = None
    for t in range(step, min(rows, cap) + 1, step):
        if rows % t == 0:
            best = t
    assert best is not None, (rows, row_bytes)
    return best


def _modnorm(x, gn, sh, sc):
    S, D = x.shape
    tm = _tile(S, 512)

    def body(x_ref, gn_ref, sh_ref, sc_ref, h_ref):
        xv = x_ref[...]
        h = (xv * _rsqrt_ms(xv) * gn_ref[...]) * (1.0 + sc_ref[...]) + sh_ref[...]
        h_ref[...] = h.astype(ACT_DTYPE)

    row = pl.BlockSpec((1, D), lambda i: (0, 0))
    return pl.pallas_call(
        body, name="modnorm", grid=(S // tm,),
        in_specs=[pl.BlockSpec((tm, D), lambda i: (i, 0)), row, row, row],
        out_specs=pl.BlockSpec((tm, D), lambda i: (i, 0)),
        out_shape=SDS((S, D), ACT_DTYPE), compiler_params=_params("parallel"),
    )(x, gn, sh, sc)


def _norm_bwd(x, dh, dxo, f, gn, sc, res_scale, next_gate, next_scale):
    S, D = x.shape
    tm = _tile(S, 256)

    def body(x_ref, dh_ref, dxo_ref, f_ref, gn_ref, sc_ref, ng_ref, dx_ref, dq_ref, acc_ref):
        @pl.when(pl.program_id(0) == 0)
        def _():
            acc_ref[...] = jnp.zeros_like(acc_ref)

        xv, dh, dxo = x_ref[...], dh_ref[...], dxo_ref[...]
        r = _rsqrt_ms(xv)
        xhat = xv * r
        gn = gn_ref[...]
        dn = dh * (1.0 + sc_ref[...])
        dxh = dn * gn
        dx = dxo + r * (dxh - xhat * jnp.mean(dxh * xhat, axis=-1, keepdims=True))
        dx_ref[...] = dx
        dq_ref[...] = ((next_scale * ng_ref[...]) * dx).astype(ACT_DTYPE)
        acc_ref[0:1, :] += _rowsum(dh)
        acc_ref[1:2, :] += _rowsum(dh * (xhat * gn))
        acc_ref[2:3, :] += _rowsum(dn * xhat)
        acc_ref[3:4, :] += _rowsum((res_scale * f_ref[...]) * dxo)

    tile = pl.BlockSpec((tm, D), lambda i: (i, 0))
    row = pl.BlockSpec((1, D), lambda i: (0, 0))
    return pl.pallas_call(
        body, name="norm_bwd", grid=(S // tm,),
        in_specs=[tile, tile, tile, tile, row, row, row],
        out_specs=[tile, tile, pl.BlockSpec((SUBLANES, D), lambda i: (0, 0))],
        out_shape=[SDS((S, D), F32), SDS((S, D), ACT_DTYPE), SDS((SUBLANES, D), F32)],
        compiler_params=_params("arbitrary"),
    )(x, dh, dxo, f, gn, sc, next_gate)


def _loss_head(x, target, gn, next_gate, next_scale):
    S, D = x.shape
    tm = _tile(S, 256)

    def body(x_ref, t_ref, gn_ref, ng_ref, dx_ref, dq_ref, acc_ref):
        @pl.when(pl.program_id(0) == 0)
        def _():
            acc_ref[...] = jnp.zeros_like(acc_ref)

        xv = x_ref[...]
        r = _rsqrt_ms(xv)
        xhat = xv * r
        gn = gn_ref[...]
        err = xhat * gn - t_ref[...]
        dy = err * (1.0 / D)
        dxh = dy * gn
        dx = r * (dxh - xhat * jnp.mean(dxh * xhat, axis=-1, keepdims=True))
        dx_ref[...] = dx
        dq_ref[...] = ((next_scale * ng_ref[...]) * dx).astype(ACT_DTYPE)
        acc_ref[0:1, :] += _rowsum(dy * xhat)
        acc_ref[1:2, :] += _rowsum(err * err) * (0.5 / D)

    tile = pl.BlockSpec((tm, D), lambda i: (i, 0))
    row = pl.BlockSpec((1, D), lambda i: (0, 0))
    return pl.pallas_call(
        body, name="loss_head", grid=(S // tm,),
        in_specs=[tile, tile, row, row],
        out_specs=[tile, tile, pl.BlockSpec((SUBLANES, D), lambda i: (0, 0))],
        out_shape=[SDS((S, D), F32), SDS((S, D), ACT_DTYPE), SDS((SUBLANES, D), F32)],
        compiler_params=_params("arbitrary"),
    )(x, target, gn, next_gate)


def _ffn_up(h, wgu):
    S, D = h.shape
    C = wgu.shape[2]
    tm = _tile(S, 512)

    def body(h_ref, wg_ref, wu_ref, a_ref, gu_ref):
        hv = h_ref[...]
        g = _dot(hv, wg_ref[...])
        u = _dot(hv, wu_ref[...])
        a_ref[...] = (g * _sigmoid(g) * u).astype(ACT_DTYPE)
        gu_ref[0] = g.astype(ACT_DTYPE)
        gu_ref[1] = u.astype(ACT_DTYPE)

    return pl.pallas_call(
        body, name="ffn_up", grid=(2, S // tm),
        in_specs=[
            pl.BlockSpec((tm, D), lambda j, i: (i, 0)),
            pl.BlockSpec((None, D, C), lambda j, i: (j, 0, 0)),
            pl.BlockSpec((None, D, C), lambda j, i: (2 + j, 0, 0)),
        ],
        out_specs=[
            pl.BlockSpec((tm, C), lambda j, i: (i, j)),
            pl.BlockSpec((2, None, tm, C), lambda j, i: (0, j, i, 0)),
        ],
        out_shape=[SDS((S, 2 * C), ACT_DTYPE), SDS((2, 2, S, C), ACT_DTYPE)],
        compiler_params=_params("parallel", "parallel"),
    )(h, wgu, wgu)


def _ffn_bwd_act(dq, wd, gu):
    S, D = dq.shape
    C = gu.shape[3]
    tm = _tile(S, 512)

    def body(dq_ref, wd_ref, gu_ref, dgu_ref):
        da = _dot_nt(dq_ref[...], wd_ref[...])
        g = gu_ref[0].astype(F32)
        u = gu_ref[1].astype(F32)
        s = _sigmoid(g)
        dgu_ref[0] = (da * u * (s * (1.0 + g * (1.0 - s)))).astype(ACT_DTYPE)
        dgu_ref[1] = (da * (g * s)).astype(ACT_DTYPE)

    gu_spec = pl.BlockSpec((2, None, tm, C), lambda j, i: (0, j, i, 0))
    return pl.pallas_call(
        body, name="ffn_bwd_act", grid=(2, S // tm),
        in_specs=[pl.BlockSpec((tm, D), lambda j, i: (i, 0)), pl.BlockSpec((C, D), lambda j, i: (j, 0)), gu_spec],
        out_specs=gu_spec,
        out_shape=SDS(gu.shape, ACT_DTYPE),
        compiler_params=_params("parallel", "parallel"),
    )(dq, wd, gu)


def _mm_res(a, w, x, gate, scale):
    S, K = a.shape
    D = w.shape[1]
    tm, tn = _tile(S, 1024), _tile(D, 512)

    def body(a_ref, w_ref, x_ref, g_ref, f_ref, xo_ref):
        f = _dot(a_ref[...], w_ref[...])
        f_ref[...] = f
        xo_ref[...] = x_ref[...] + (scale * g_ref[...]) * f

    tile = pl.BlockSpec((tm, tn), lambda j, i: (i, j))
    return pl.pallas_call(
        body, name=f"mm_res_k{K}", grid=(D // tn, S // tm),
        in_specs=[pl.BlockSpec((tm, K), lambda j, i: (i, 0)), pl.BlockSpec((K, tn), lambda j, i: (0, j)), tile,
                  pl.BlockSpec((1, tn), lambda j, i: (0, j))],
        out_specs=[tile, tile],
        out_shape=[SDS((S, D), F32), SDS((S, D), F32)],
        compiler_params=_params("parallel", "parallel"),
    )(a, w, x, gate)


def _mm_chunks(h, wc):
    S, K = h.shape
    P, _, N = wc.shape
    tm = _tile(S, 1024)

    def body(h_ref, w_ref, o_ref):
        o_ref[...] = _dot(h_ref[...], w_ref[...])

    return pl.pallas_call(
        body, name="mm_chunks", grid=(P, S // tm),
        in_specs=[pl.BlockSpec((tm, K), lambda j, i: (i, 0)), pl.BlockSpec((None, K, N), lambda j, i: (j, 0, 0))],
        out_specs=pl.BlockSpec((tm, N), lambda j, i: (i, j)),
        out_shape=SDS((S, P * N), F32),
        compiler_params=_params("parallel", "parallel"),
    )(h, wc)


def _mm_nt_chunks(ac, wc):
    P, S, K = ac.shape
    N = wc.shape[1]
    tm, tn = _tile(S, 1024), _tile(N, 512)

    def body(a_ref, w_ref, o_ref):
        acc = _dot_nt(a_ref[0], w_ref[0])
        for p in range(1, P):
            acc += _dot_nt(a_ref[p], w_ref[p])
        o_ref[...] = acc

    return pl.pallas_call(
        body, name=f"mm_nt_p{P}k{K}", grid=(S // tm, N // tn),
        in_specs=[pl.BlockSpec((P, tm, K), lambda i, j: (0, i, 0)), pl.BlockSpec((P, tn, K), lambda i, j: (0, j, 0))],
        out_specs=pl.BlockSpec((tm, tn), lambda i, j: (i, j)),
        out_shape=SDS((S, N), F32),
        compiler_params=_params("parallel", "parallel"),
    )(ac, wc)


def _mm_tn_chunks(a, bc, tile_m, tile_n):
    S, M = a.shape
    P, _, N = bc.shape
    ts, tm, tn = _tile(S, 512), _tile(M, tile_m), _tile(N, tile_n)

    def body(a_ref, b_ref, o_ref):
        @pl.when(pl.program_id(3) == 0)
        def _():
            o_ref[...] = jnp.zeros_like(o_ref)

        o_ref[...] += _dot_tn(a_ref[...], b_ref[...])

    return pl.pallas_call(
        body, name=f"mm_tn_m{M}n{N}", grid=(P, M // tm, N // tn, S // ts),
        in_specs=[pl.BlockSpec((ts, tm), lambda p, m, n, k: (k, m)), pl.BlockSpec((None, ts, tn), lambda p, m, n, k: (p, k, n))],
        out_specs=pl.BlockSpec((None, tm, tn), lambda p, m, n, k: (p, m, n)),
        out_shape=SDS((P, M, N), F32),
        compiler_params=_params("parallel", "parallel", "parallel", "arbitrary"),
    )(a, bc)


def _shift_down(x, s, row, fill):
    return jnp.where(row >= s, pltpu.roll(x, s, 0), fill)


def _shift_up(x, s, row, fill):
    n = x.shape[0]
    return jnp.where(row < n - s, pltpu.roll(x, n - s, 0), fill)


def _scan_down(a, b, row):
    s = 1
    while s < a.shape[0]:
        b = a * _shift_down(b, s, row, 0.0) + b
        a = a * _shift_down(a, s, row, 1.0)
        s *= 2
    return b


def _scan_up(a, b, row):
    s = 1
    while s < a.shape[0]:
        b = a * _shift_up(b, s, row, 0.0) + b
        a = a * _shift_up(a, s, row, 1.0)
        s *= 2
    return b


def _conv(xl, cw_ref, cb_ref, row):
    y = cb_ref[...] + _shift_down(xl, 3, row, 0.0) * cw_ref[0:1, :]
    y = y + _shift_down(xl, 2, row, 0.0) * cw_ref[1:2, :]
    y = y + _shift_down(xl, 1, row, 0.0) * cw_ref[2:3, :]
    return y + xl * cw_ref[3:4, :]


def _lru_gates(xc, wa_ref, ba_ref, wx_ref, bx_ref, lam_ref):
    ra = _sigmoid(_dot(xc, wa_ref[...]) + ba_ref[...])
    ri = _sigmoid(_dot(xc, wx_ref[...]) + bx_ref[...])
    ls = jax.nn.log_sigmoid(lam_ref[...])
    a = jnp.exp((RG_LRU_C * ra) * ls)
    mult = jnp.sqrt(1.0 - a * a)
    return ra, ri, ls, a, mult


def _lru_specs(S):
    col = lambda off: pl.BlockSpec((S, LANES), lambda j: (0, off + j))
    vec = pl.BlockSpec((1, LANES), lambda j: (0, j))
    blk = pl.BlockSpec((None, LANES, LANES), lambda j: (j, 0, 0))
    cw = pl.BlockSpec((CONV_WIDTH, LANES), lambda j: (0, j))
    return col, vec, blk, cw


def _lru_fwd(proj, cw, cb, wa, ba, wx, bx, lam):
    S = proj.shape[0]
    W = cb.shape[1]
    nb = W // LANES

    def body(xl_ref, gl_ref, cw_ref, cb_ref, wa_ref, ba_ref, wx_ref, bx_ref, lam_ref, y_ref):
        row = lax.broadcasted_iota(jnp.int32, (S, LANES), 0)
        xc = _conv(xl_ref[...], cw_ref, cb_ref, row)
        _, ri, _, a, mult = _lru_gates(xc, wa_ref, ba_ref, wx_ref, bx_ref, lam_ref)
        h = _scan_down(a, mult * (ri * xc), row)
        y_ref[...] = h * _gelu(gl_ref[...])

    col, vec, blk, cws = _lru_specs(S)
    return pl.pallas_call(
        body, name="lru_fwd", grid=(nb,),
        in_specs=[col(0), col(nb), cws, vec, blk, vec, blk, vec, vec],
        out_specs=pl.BlockSpec((S, LANES), lambda j: (0, j)),
        out_shape=SDS((S, W), F32), compiler_params=_params("parallel"),
    )(proj, proj, cw, cb, wa, ba, wx, bx, lam)


def _lru_bwd(proj, dy, cw, cb, wa, ba, wx, bx, lam, wat, wxt):
    S = proj.shape[0]
    W = cb.shape[1]
    nb = W // LANES

    def body(xl_ref, gl_ref, dy_ref, cw_ref, cb_ref, wa_ref, ba_ref, wx_ref, bx_ref, lam_ref, wat_ref, wxt_ref,
             dp_ref, dwa_ref, dwx_ref, vec_ref):
        row = lax.broadcasted_iota(jnp.int32, (S, LANES), 0)
        xl = xl_ref[...]
        xc = _conv(xl, cw_ref, cb_ref, row)
        ra, ri, ls, a, mult = _lru_gates(xc, wa_ref, ba_ref, wx_ref, bx_ref, lam_ref)
        h = _scan_down(a, mult * (ri * xc), row)
        gl = gl_ref[...]
        dyv = dy_ref[...]
        dp_ref[1] = (dyv * h * _gelu_grad(gl)).astype(ACT_DTYPE)
        adj = _scan_up(_shift_up(a, 1, row, 0.0), dyv * _gelu(gl), row)
        da = adj * _shift_down(h, 1, row, 0.0)
        dmult = adj * (ri * xc)
        dlog_a = da * a - dmult * (a * a) / mult
        dra = dlog_a * (RG_LRU_C * ls)
        dpa = dra * ra * (1.0 - ra)
        dpi = (adj * mult * xc) * ri * (1.0 - ri)
        dxc = adj * mult * ri + _dot(dpa, wat_ref[...]) + _dot(dpi, wxt_ref[...])
        dwa_ref[...] = _dot_tn(xc, dpa)
        dwx_ref[...] = _dot_tn(xc, dpi)
        dxl = dxc * cw_ref[3:4, :]
        dxl = dxl + _shift_up(dxc, 1, row, 0.0) * cw_ref[2:3, :]
        dxl = dxl + _shift_up(dxc, 2, row, 0.0) * cw_ref[1:2, :]
        dxl = dxl + _shift_up(dxc, 3, row, 0.0) * cw_ref[0:1, :]
        dp_ref[0] = dxl.astype(ACT_DTYPE)
        vec_ref[...] = jnp.zeros_like(vec_ref)
        vec_ref[0:1, :] = _rowsum(dpa)
        vec_ref[1:2, :] = _rowsum(dpi)
        vec_ref[2:3, :] = _rowsum(dlog_a * (RG_LRU_C * ra)) * _sigmoid(-lam_ref[...])
        vec_ref[3:4, :] = _rowsum(dxc)
        vec_ref[4:5, :] = _rowsum(dxc * _shift_down(xl, 3, row, 0.0))
        vec_ref[5:6, :] = _rowsum(dxc * _shift_down(xl, 2, row, 0.0))
        vec_ref[6:7, :] = _rowsum(dxc * _shift_down(xl, 1, row, 0.0))
        vec_ref[7:8, :] = _rowsum(dxc * xl)

    col, vec, blk, cws = _lru_specs(S)
    return pl.pallas_call(
        body, name="lru_bwd", grid=(nb,),
        in_specs=[col(0), col(nb), col(0), cws, vec, blk, vec, blk, vec, vec, blk, blk],
        out_specs=[pl.BlockSpec((2, S, LANES), lambda j: (0, 0, j)), blk, blk, pl.BlockSpec((2 * SUBLANES, LANES), lambda j: (0, j))],
        out_shape=[SDS((2, S, W), ACT_DTYPE), SDS((nb, LANES, LANES), F32), SDS((nb, LANES, LANES), F32), SDS((2 * SUBLANES, W), F32)],
        compiler_params=_params("parallel"),
    )(proj, proj, dy, cw, cb, wa, ba, wx, bx, lam, wat, wxt)


def _seg_mean(x, seg_ref, width):
    hi = x.astype(jnp.bfloat16)
    lo = (x - hi.astype(F32)).astype(jnp.bfloat16)
    ones = seg_ref[...]
    s = jnp.dot(hi, ones, preferred_element_type=F32) + jnp.dot(lo, ones, preferred_element_type=F32)
    return s * (1.0 / width)


def _gmlp_core(u_ref, v_ref, gv_ref, seg_ref, ws_ref, bfull_ref, z_scr, hd):
    tm, W = u_ref.shape
    lane = lax.broadcasted_iota(jnp.int32, (CHUNK, LANES), 1)
    ug = _gelu(u_ref[...])
    vg = _gelu(v_ref[...])
    cen = vg - _seg_mean(vg, seg_ref, hd)
    rstd = lax.rsqrt(_seg_mean(cen * cen, seg_ref, hd) + EPS)
    vhat = cen * rstd
    vh = vhat * gv_ref[...]
    vcats = {}
    for ci in range(tm // CHUNK):
        for p in range(W // LANES):
            blk = vh[ci * CHUNK:(ci + 1) * CHUNK, p * LANES:(p + 1) * LANES]
            vcat = jnp.concatenate([jnp.where(lane < hd, blk, 0.0), jnp.where(lane >= hd, blk, 0.0)], axis=0).astype(MXU_DTYPE)
            vcats[ci, p] = vcat
            z_scr[ci * CHUNK:(ci + 1) * CHUNK, p * LANES:(p + 1) * LANES] = (
                jnp.dot(ws_ref[p], vcat, preferred_element_type=F32) + bfull_ref[:, p * LANES:(p + 1) * LANES])
    return ug, vhat, rstd, vcats


def _gmlp_specs(tm, W, nb):
    rows = lambda off: pl.BlockSpec((tm, W), lambda i: (i, off))
    vec = pl.BlockSpec((1, W), lambda i: (0, 0))
    seg = pl.BlockSpec((W, W), lambda i: (0, 0))
    wsp = pl.BlockSpec((nb, CHUNK, 2 * CHUNK), lambda i: (0, 0, 0))
    bfull = pl.BlockSpec((CHUNK, W), lambda i: (0, 0))
    return rows, vec, seg, wsp, bfull


def _gmlp_fwd(proj, ylru, gv, seg, wsp, bfull, g_lru, g_gm):
    S, W = ylru.shape
    nb = W // LANES
    hd = W // HEADS
    tm = _tile(S, 512)

    def body(u_ref, v_ref, yl_ref, gv_ref, seg_ref, ws_ref, bfull_ref, gl_ref, gg_ref, yn_ref, ygm_ref, z_scr):
        ug, _, _, _ = _gmlp_core(u_ref, v_ref, gv_ref, seg_ref, ws_ref, bfull_ref, z_scr, hd)
        ygm = ug * z_scr[...]
        ygm_ref[...] = ygm
        yl = yl_ref[...]
        yn_ref[:, 0:W] = (yl * _rsqrt_ms(yl) * gl_ref[...]).astype(ACT_DTYPE)
        yn_ref[:, W:2 * W] = (ygm * _rsqrt_ms(ygm) * gg_ref[...]).astype(ACT_DTYPE)

    rows, vec, segs, wsps, bfulls = _gmlp_specs(tm, W, nb)
    return pl.pallas_call(
        body, name="gmlp_fwd", grid=(S // tm,),
        in_specs=[rows(2), rows(3), rows(0), vec, segs, wsps, bfulls, vec, vec],
        out_specs=[pl.BlockSpec((tm, 2 * W), lambda i: (i, 0)), rows(0)],
        out_shape=[SDS((S, 2 * W), ACT_DTYPE), SDS((S, W), F32)],
        scratch_shapes=[pltpu.VMEM((tm, W), F32)],
        compiler_params=_params("parallel"),
    )(proj, proj, ylru, gv, seg, wsp, bfull, g_lru, g_gm)


def _rms_bwd(y, g, dyn):
    r = _rsqrt_ms(y)
    yhat = y * r
    dyh = dyn * g
    return r * (dyh - yhat * jnp.mean(dyh * yhat, axis=-1, keepdims=True)), _rowsum(dyn * yhat)


def _gmlp_bwd(proj, ylru, ygm, dyn, gv, seg, wsp, wspt, bfull, g_lru, g_gm):
    S, W = ylru.shape
    nb = W // LANES
    hd = W // HEADS
    tm = _tile(S, 256)

    def body(u_ref, v_ref, yl_ref, ygm_ref, dl_ref, dg_ref, gv_ref, seg_ref, ws_ref, wst_ref, bfull_ref, gl_ref, gg_ref,
             dyl_ref, duv_ref, dws_ref, dbf_ref, acc_ref, z_scr, dvh_scr):
        @pl.when(pl.program_id(0) == 0)
        def _():
            dws_ref[...] = jnp.zeros_like(dws_ref)
            dbf_ref[...] = jnp.zeros_like(dbf_ref)
            acc_ref[...] = jnp.zeros_like(acc_ref)

        dyl, dgl = _rms_bwd(yl_ref[...], gl_ref[...], dl_ref[...])
        dyl_ref[...] = dyl
        dygm, dgg = _rms_bwd(ygm_ref[...], gg_ref[...], dg_ref[...])
        ug, vhat, rstd, vcats = _gmlp_core(u_ref, v_ref, gv_ref, seg_ref, ws_ref, bfull_ref, z_scr, hd)
        duv_ref[0] = (dygm * z_scr[...] * _gelu_grad(u_ref[...])).astype(ACT_DTYPE)
        dz = dygm * ug
        lane = lax.broadcasted_iota(jnp.int32, (CHUNK, LANES), 1)
        dbf = dz[0:CHUNK, :]
        for ci in range(1, tm // CHUNK):
            dbf += dz[ci * CHUNK:(ci + 1) * CHUNK, :]
        dbf_ref[...] += dbf
        for ci in range(tm // CHUNK):
            for p in range(nb):
                dzb = dz[ci * CHUNK:(ci + 1) * CHUNK, p * LANES:(p + 1) * LANES].astype(MXU_DTYPE)
                dws_ref[p] += _dot_nt(dzb, vcats[ci, p])
                dvc = jnp.dot(wst_ref[p], dzb, preferred_element_type=F32)
                dvh_scr[ci * CHUNK:(ci + 1) * CHUNK, p * LANES:(p + 1) * LANES] = jnp.where(lane < hd, dvc[0:CHUNK], dvc[CHUNK:2 * CHUNK])
        dvh = dvh_scr[...]
        dvn = dvh * gv_ref[...]
        dvg = rstd * (dvn - _seg_mean(dvn, seg_ref, hd) - vhat * _seg_mean(dvn * vhat, seg_ref, hd))
        duv_ref[1] = (dvg * _gelu_grad(v_ref[...])).astype(ACT_DTYPE)
        acc_ref[0:1, :] += dgl
        acc_ref[1:2, :] += dgg
        acc_ref[2:3, :] += _rowsum(dvh * vhat)

    rows, vec, segs, wsps, bfulls = _gmlp_specs(tm, W, nb)
    wspt_spec = pl.BlockSpec((nb, 2 * CHUNK, CHUNK), lambda i: (0, 0, 0))
    return pl.pallas_call(
        body, name="gmlp_bwd", grid=(S // tm,),
        in_specs=[rows(2), rows(3), rows(0), rows(0), rows(0), rows(1), vec, segs, wsps, wspt_spec, bfulls, vec, vec],
        out_specs=[rows(0), pl.BlockSpec((2, tm, W), lambda i: (0, i, 0)), wsps, bfulls, pl.BlockSpec((SUBLANES, W), lambda i: (0, 0))],
        out_shape=[SDS((S, W), F32), SDS((2, S, W), ACT_DTYPE), SDS((nb, CHUNK, 2 * CHUNK), F32), SDS((CHUNK, W), F32), SDS((SUBLANES, W), F32)],
        scratch_shapes=[pltpu.VMEM((tm, W), F32), pltpu.VMEM((tm, W), F32)],
        compiler_params=_params("arbitrary"),
    )(proj, proj, ylru, ygm, dyn, dyn, gv, seg, wsp, wspt, bfull, g_lru, g_gm)


def _ada_fwd(c_all, w_ada, b_shard):
    L, D, N = w_ada.shape
    R = c_all.shape[0]
    tn = N // 2

    def body(c_ref, w_ref, b_ref, o_ref):
        cv = c_ref[...]
        o_ref[...] = _dot(cv * _sigmoid(cv), w_ref[...]) + b_ref[...]

    return pl.pallas_call(
        body, name="ada_fwd", grid=(L, N // tn),
        in_specs=[pl.BlockSpec((R, D), lambda l, j: (0, 0)), pl.BlockSpec((None, D, tn), lambda l, j: (l, 0, j)),
                  pl.BlockSpec((None, 1, tn), lambda l, j: (l, 0, j))],
        out_specs=pl.BlockSpec((None, R, tn), lambda l, j: (l, 0, j)),
        out_shape=SDS((L, R, N), F32), compiler_params=_params("parallel", "parallel"),
    )(c_all, w_ada, b_shard)


def _ada_grad(c_all_t, dmod):
    D, B = c_all_t.shape
    L, _, N = dmod.shape
    tn = N // 2

    def body(c_ref, d_ref, o_ref):
        cv = c_ref[...]
        sc = cv * _sigmoid(cv)
        acc = sc[:, 0:1] * d_ref[0:1, :]
        for b in range(1, B):
            acc += sc[:, b:b + 1] * d_ref[b:b + 1, :]
        o_ref[...] = acc

    return pl.pallas_call(
        body, name="ada_grad", grid=(L, N // tn),
        in_specs=[pl.BlockSpec((D, B), lambda l, j: (0, 0)), pl.BlockSpec((None, B, tn), lambda l, j: (l, 0, j))],
        out_specs=pl.BlockSpec((None, D, tn), lambda l, j: (l, 0, j)),
        out_shape=SDS((L, D, N), F32), compiler_params=_params("parallel", "parallel"),
    )(c_all_t, dmod)


def _adamw(w, g, m, v):
    R, C = w.shape
    tr = _row_tile(R, C * 4)

    def body(w_ref, g_ref, m_ref, v_ref, d_ref, mo_ref, vo_ref):
        d_ref[...], mo_ref[...], vo_ref[...] = _adam_math(w_ref[...], g_ref[...], m_ref[...], v_ref[...])

    tile = pl.BlockSpec((tr, C), lambda i: (i, 0))
    return pl.pallas_call(
        body, name=f"adamw_r{R}c{C}", grid=(R // tr,), in_specs=[tile] * 4, out_specs=[tile] * 3,
        out_shape=[SDS((R, C), F32)] * 3, compiler_params=_params("parallel"),
    )(w, g, m, v)


def _adam_math(w, g, m, v):
    mn = ADAM_B1 * m + (1.0 - ADAM_B1) * g
    vn = ADAM_B2 * v + (1.0 - ADAM_B2) * (g * g)
    m_hat = mn / (1.0 - ADAM_B1 ** ADAM_STEP)
    v_hat = vn / (1.0 - ADAM_B2 ** ADAM_STEP)
    return -ADAM_LR * (m_hat / (jnp.sqrt(v_hat) + ADAM_EPS) + ADAM_WD * w), mn, vn


def _adamw_layer(w, g, m, v, l, prev):
    L, R, C = w.shape
    tr = _row_tile(R, C * 4)
    prev = () if prev is None else tuple(prev)

    def body(w_ref, g_ref, m_ref, v_ref, *rest):
        go_ref, d_ref, mo_ref, vo_ref = rest[len(prev):]
        gv = g_ref[...]
        go_ref[...] = gv
        d_ref[...], mo_ref[...], vo_ref[...] = _adam_math(w_ref[...], gv, m_ref[...], v_ref[...])

    lay = pl.BlockSpec((None, tr, C), lambda i: (l, i, 0))
    return pl.pallas_call(
        body, name=f"adamw_layer_r{R}c{C}", grid=(R // tr,),
        in_specs=[lay, pl.BlockSpec((tr, C), lambda i: (i, 0)), lay, lay] + [ANY] * len(prev), out_specs=[lay] * 4,
        out_shape=[SDS((L, R, C), F32)] * 4, input_output_aliases={4 + k: k for k in range(len(prev))},
        compiler_params=_params("parallel"),
    )(w, g, m, v, *prev)


def _sum_leading(a):
    P, R, C = a.shape
    tr = _row_tile(R, P * C * 4)

    def body(a_ref, o_ref):
        acc = a_ref[0]
        for p in range(1, P):
            acc = acc + a_ref[p]
        o_ref[...] = acc

    return pl.pallas_call(
        body, name=f"sum{P}_r{R}c{C}", grid=(R // tr,),
        in_specs=[pl.BlockSpec((P, tr, C), lambda i: (0, i, 0))],
        out_specs=pl.BlockSpec((tr, C), lambda i: (i, 0)),
        out_shape=SDS((R, C), F32), compiler_params=_params("parallel"),
    )(a)


def _add_half(g4, r1, place):
    _, _, R, C = g4.shape
    tr = _row_tile(R, C * 4)

    def body(place_ref, g_ref, r_ref, h_ref, own_ref):
        s = (g_ref[...] + r_ref[...]).astype(XFER_DTYPE)
        h_ref[...] = s

        @pl.when(pl.program_id(1) == place_ref[1])
        def _():
            own_ref[...] = s

    return pl.pallas_call(
        body, name=f"add_half_r{R}c{C}",
        grid_spec=pltpu.PrefetchScalarGridSpec(
            num_scalar_prefetch=1, grid=(R // tr, N_CHIPS),
            in_specs=[pl.BlockSpec((None, None, tr, C), lambda i, p, place_ref: (p, place_ref[0], i, 0)),
                      pl.BlockSpec((None, tr, C), lambda i, p, place_ref: (p, i, 0))],
            out_specs=[pl.BlockSpec((None, tr, C), lambda i, p, place_ref: (p, i, 0)),
                       pl.BlockSpec((None, tr, C), lambda i, p, place_ref: (place_ref[1], i, 0))],
        ),
        out_shape=[SDS((N_CHIPS, R, C), XFER_DTYPE)] * 2, compiler_params=_params("parallel", "arbitrary"),
    )(place, g4, r1)


def _sum4_into_half(r2, place):
    P, R, C = r2.shape
    tr = _row_tile(R, P * C * 4)

    def body(place_ref, a_ref, o_ref):
        acc = a_ref[0].astype(F32)
        for p in range(1, P):
            acc = acc + a_ref[p].astype(F32)
        o_ref[...] = acc

    return pl.pallas_call(
        body, name=f"sum4_r{R}c{C}",
        grid_spec=pltpu.PrefetchScalarGridSpec(
            num_scalar_prefetch=1, grid=(R // tr,),
            in_specs=[pl.BlockSpec((P, tr, C), lambda i, place_ref: (0, i, 0))],
            out_specs=pl.BlockSpec((None, tr, C), lambda i, place_ref: (place_ref[0], i, 0)),
        ),
        out_shape=SDS((2, R, C), F32), compiler_params=_params("parallel"),
    )(place, r2)


def _cast_into_slot(w, l, place):
    _, R, C = w.shape
    tr = _row_tile(R, C * 4)

    def body(place_ref, w_ref, o_ref):
        o_ref[...] = w_ref[...].astype(MXU_DTYPE)

    return pl.pallas_call(
        body, name=f"cast_r{R}c{C}",
        grid_spec=pltpu.PrefetchScalarGridSpec(
            num_scalar_prefetch=1, grid=(R // tr,),
            in_specs=[pl.BlockSpec((None, tr, C), lambda i, place_ref: (l, i, 0))],
            out_specs=pl.BlockSpec((None, tr, C), lambda i, place_ref: (place_ref[1], i, 0)),
        ),
        out_shape=SDS((N_CHIPS, R, C), MXU_DTYPE), compiler_params=_params("parallel"),
    )(place, w)


def _place():
    x, y, c = lax.axis_index("x"), lax.axis_index("y"), lax.axis_index("c")
    chips = [(1 - x, y), (x, 1 - y), (1 - x, 1 - y)]
    return x, y, c, chips


def _remote(src, dst, send_sem, recv_sem, to):
    return pltpu.make_async_remote_copy(src_ref=src, dst_ref=dst, send_sem=send_sem, recv_sem=recv_sem, device_id=to, device_id_type=MESH)


def _all_gather8(v):
    R, N = v.shape

    def body(v_ref, out_ref, send_sems, recv_sems, local_sem):
        x, y, c, chips = _place()
        me, sibling = (x, y, c), (x, y, 1 - c)

        def slot(px, py, pc):
            return out_ref.at[4 * px + 2 * py + pc]

        def copy(k, block, to, src=None):
            return _remote(slot(*block) if src is None else src, slot(*block), send_sems.at[k], recv_sems.at[k], to)

        mine = pltpu.make_async_copy(v_ref, slot(*me), local_sem)
        mine.start()
        first = [copy(0, me, sibling, src=v_ref)] + [copy(1 + j, me, (*chip, c), src=v_ref) for j, chip in enumerate(chips)]
        for cp in first:
            cp.start()
        passed = [copy(4 + j, (*chip, c), sibling) for j, chip in enumerate(chips)]
        for j, chip in enumerate(chips):
            copy(1 + j, (*chip, c), me).wait_recv()
            passed[j].start()
        copy(0, sibling, me).wait_recv()
        for j, chip in enumerate(chips):
            copy(4 + j, (*chip, 1 - c), me).wait_recv()
        for cp in first + passed:
            cp.wait_send()
        mine.wait()

    return pl.pallas_call(
        body, name=f"all_gather8_r{R}n{N}", out_shape=SDS((N_DEV, R, N), v.dtype), in_specs=[ANY], out_specs=ANY,
        scratch_shapes=[pltpu.SemaphoreType.DMA((7,)), pltpu.SemaphoreType.DMA((7,)), pltpu.SemaphoreType.DMA],
    )(v)


def _gather_weights(slots):
    n = len(slots)

    def body(*refs):
        ins, outs = refs[:n], refs[n:2 * n]
        send_sems, recv_sems = refs[2 * n:]
        x, y, c, chips = _place()
        q = 2 * x + y
        sibling = (x, y, 1 - c)
        first = []
        for k in range(n):
            for j, chip in enumerate(chips):
                first.append(_remote(ins[k].at[q, c], outs[k].at[q, c], send_sems.at[k, j], recv_sems.at[k, j], (*chip, c)))
                first[-1].start()
        passed = []
        for k in range(n):
            for j, chip in enumerate(chips):
                half = outs[k].at[2 * chip[0] + chip[1], c]
                _remote(half, half, send_sems.at[k, j], recv_sems.at[k, j], sibling).wait_recv()
                passed.append(_remote(half, half, send_sems.at[k, 3 + j], recv_sems.at[k, 3 + j], sibling))
                passed[-1].start()
        for k in range(n):
            for j, chip in enumerate(chips):
                half = outs[k].at[2 * chip[0] + chip[1], 1 - c]
                _remote(half, half, send_sems.at[k, 3 + j], recv_sems.at[k, 3 + j], sibling).wait_recv()
        for cp in first + passed:
            cp.wait_send()

    return pl.pallas_call(
        body, name="gather_weights", out_shape=[SDS(s.shape, s.dtype) for s in slots],
        in_specs=[ANY] * n, out_specs=[ANY] * n, input_output_aliases={k: k for k in range(n)},
        scratch_shapes=[pltpu.SemaphoreType.DMA((n, 6)), pltpu.SemaphoreType.DMA((n, 6))],
    )(*slots)


def _swap_halves(g4s):
    n = len(g4s)

    def body(*refs):
        ins, outs = refs[:n], refs[n:2 * n]
        send_sems, recv_sems = refs[2 * n:]
        x, y, c, _ = _place()
        sibling = (x, y, 1 - c)
        for k in range(n):
            for p in range(N_CHIPS):
                _remote(ins[k].at[p, 1 - c], outs[k].at[p], send_sems.at[k], recv_sems.at[k], sibling).start()
        for k in range(n):
            _remote(outs[k], outs[k], send_sems.at[k], recv_sems.at[k], sibling).wait()

    return pl.pallas_call(
        body, name="swap_halves", out_shape=[SDS((N_CHIPS,) + g.shape[2:], g.dtype) for g in g4s],
        in_specs=[ANY] * n, out_specs=[ANY] * n,
        scratch_shapes=[pltpu.SemaphoreType.DMA((n,)), pltpu.SemaphoreType.DMA((n,))],
    )(*g4s)


def _scatter_regions(hs, lands):
    n = len(hs)

    def body(*refs):
        ins, outs = refs[:n], refs[2 * n:3 * n]
        send_sems, recv_sems = refs[3 * n:]
        x, y, c, chips = _place()
        q = 2 * x + y
        sent = []
        for k in range(n):
            for j, chip in enumerate(chips):
                sent.append(_remote(ins[k].at[2 * chip[0] + chip[1]], outs[k].at[q], send_sems.at[k, j], recv_sems.at[k, j], (*chip, c)))
                sent[-1].start()
        for k in range(n):
            for j, chip in enumerate(chips):
                got = outs[k].at[2 * chip[0] + chip[1]]
                _remote(got, got, send_sems.at[k, j], recv_sems.at[k, j], (x, y, c)).wait_recv()
        for cp in sent:
            cp.wait_send()

    return pl.pallas_call(
        body, name="scatter_regions", out_shape=[SDS(h.shape, h.dtype) for h in lands],
        in_specs=[ANY] * (2 * n), out_specs=[ANY] * n, input_output_aliases={n + k: k for k in range(n)},
        scratch_shapes=[pltpu.SemaphoreType.DMA((n, 3)), pltpu.SemaphoreType.DMA((n, 3))],
    )(*hs, *lands)


def _share_halves(fins):
    n = len(fins)

    def body(*refs):
        ins, outs = refs[:n], refs[n:2 * n]
        send_sems, recv_sems = refs[2 * n:]
        x, y, c, _ = _place()
        sibling = (x, y, 1 - c)
        sent = [_remote(ins[k].at[c], outs[k].at[c], send_sems.at[k], recv_sems.at[k], sibling) for k in range(n)]
        for cp in sent:
            cp.start()
        for k in range(n):
            got = outs[k].at[1 - c]
            _remote(got, got, send_sems.at[k], recv_sems.at[k], sibling).wait_recv()
        for cp in sent:
            cp.wait_send()

    return pl.pallas_call(
        body, name="share_halves", out_shape=[SDS(t.shape, t.dtype) for t in fins],
        in_specs=[ANY] * n, out_specs=[ANY] * n, input_output_aliases={k: k for k in range(n)},
        scratch_shapes=[pltpu.SemaphoreType.DMA((n,)), pltpu.SemaphoreType.DMA((n,))],
    )(*fins)


def _chip_sums(grads, place):
    g4s = [g.reshape(N_CHIPS, 2, g.shape[1] // 2, g.shape[2]) for g in grads]
    pairs = [_add_half(g4, r1, place) for g4, r1 in zip(g4s, _swap_halves(g4s))]
    return [h for h, _ in pairs], [own for _, own in pairs]


def _reduce_finish(lands, place):
    fins = _share_halves([_sum4_into_half(r2, place) for r2 in lands])
    return [f.reshape(2 * f.shape[1], f.shape[2]) for f in fins]


def _reduce_scatter(grads, place):
    hs, lands = _chip_sums(grads, place)
    return _reduce_finish(_scatter_regions(hs, lands), place)


HBM_SPEC = pl.BlockSpec(memory_space=pltpu.HBM)
SEM_SPEC = pl.BlockSpec(memory_space=pltpu.SEMAPHORE)
DATAFLOW = pltpu.SideEffectType.DATAFLOW_SIDE_EFFECTING


def _in_hbm(a):
    return pltpu.with_memory_space_constraint(a, pltpu.HBM)


def _hbm_like(a):
    return pltpu.HBM(a.shape, a.dtype)


def _gather_ici_start(slots, thru, after):
    n = len(slots)

    def body(*refs):
        ins = refs[:n]
        send_sems, recv_sems = refs[n + 2], refs[n + 3]
        x, y, c, chips = _place()
        q = 2 * x + y
        for k in range(n):
            for j, chip in enumerate(chips):
                _remote(ins[k].at[q, c], ins[k].at[q, c], send_sems.at[3 * k + j], recv_sems.at[3 * k + j], (*chip, c)).start()

    out = pl.pallas_call(
        body, name="gather_ici_start",
        out_shape=(pltpu.SemaphoreType.DMA((3 * n,)), pltpu.SemaphoreType.DMA((3 * n,)), *[_hbm_like(s) for s in slots], _hbm_like(thru)),
        in_specs=[HBM_SPEC] * (n + 1) + [ANY], out_specs=(SEM_SPEC, SEM_SPEC, *[HBM_SPEC] * (n + 1)),
        input_output_aliases={k: 2 + k for k in range(n + 1)},
        compiler_params=pltpu.CompilerParams(has_side_effects=DATAFLOW),
    )(*[_in_hbm(s) for s in slots], _in_hbm(thru), after)
    return out[0], out[1], list(out[2:2 + n]), out[2 + n]


def _gather_ici_wait(send_sems, recv_sems, slots, after):
    n = len(slots)

    def body(*refs):
        ins = refs[:n]
        send_sems, recv_sems = refs[n], refs[n + 1]
        x, y, c, chips = _place()
        for k in range(n):
            for j, chip in enumerate(chips):
                got = ins[k].at[2 * chip[0] + chip[1], c]
                cp = _remote(got, got, send_sems.at[3 * k + j], recv_sems.at[3 * k + j], (x, y, c))
                cp.wait_send()
                cp.wait_recv()

    out = pl.pallas_call(
        body, name="gather_ici_wait", out_shape=[_hbm_like(s) for s in slots],
        in_specs=[HBM_SPEC] * n + [SEM_SPEC, SEM_SPEC, ANY], out_specs=[HBM_SPEC] * n,
        input_output_aliases={k: k for k in range(n)},
        compiler_params=pltpu.CompilerParams(has_side_effects=DATAFLOW),
    )(*slots, send_sems, recv_sems, after)
    return list(out)


def _gather_pass_on(slots):
    n = len(slots)

    def body(*refs):
        ins, outs = refs[:n], refs[n:2 * n]
        send_sems, recv_sems = refs[2 * n:]
        x, y, c, chips = _place()
        sibling = (x, y, 1 - c)
        passed = []
        for k in range(n):
            for j, chip in enumerate(chips):
                passed.append(_remote(ins[k].at[2 * chip[0] + chip[1], c], outs[k].at[2 * chip[0] + chip[1], c],
                                      send_sems.at[k, j], recv_sems.at[k, j], sibling))
                passed[-1].start()
        for k in range(n):
            for j, chip in enumerate(chips):
                half = outs[k].at[2 * chip[0] + chip[1], 1 - c]
                _remote(half, half, send_sems.at[k, j], recv_sems.at[k, j], sibling).wait_recv()
        for cp in passed:
            cp.wait_send()

    return pl.pallas_call(
        body, name="gather_pass_on", out_shape=[SDS(s.shape, s.dtype) for s in slots],
        in_specs=[ANY] * n, out_specs=[ANY] * n, input_output_aliases={k: k for k in range(n)},
        scratch_shapes=[pltpu.SemaphoreType.DMA((n, 3)), pltpu.SemaphoreType.DMA((n, 3))],
    )(*slots)


def _scatter_start(hs, lands, thru):
    n = len(hs)

    def body(*refs):
        ins, zones = refs[:n], refs[n:2 * n]
        send_sems, recv_sems = refs[2 * n + 1], refs[2 * n + 2]
        x, y, c, chips = _place()
        q = 2 * x + y
        for k in range(n):
            for j, chip in enumerate(chips):
                _remote(ins[k].at[2 * chip[0] + chip[1]], zones[k].at[q], send_sems.at[3 * k + j], recv_sems.at[3 * k + j], (*chip, c)).start()

    arrays = [*hs, *lands, thru]
    out = pl.pallas_call(
        body, name="scatter_start",
        out_shape=(pltpu.SemaphoreType.DMA((3 * n,)), pltpu.SemaphoreType.DMA((3 * n,)), *[_hbm_like(a) for a in arrays]),
        in_specs=[HBM_SPEC] * len(arrays), out_specs=(SEM_SPEC, SEM_SPEC, *[HBM_SPEC] * len(arrays)),
        input_output_aliases={k: 2 + k for k in range(len(arrays))},
        compiler_params=pltpu.CompilerParams(has_side_effects=DATAFLOW),
    )(*[_in_hbm(a) for a in arrays])
    return out[0], out[1], list(out[2:2 + n]), list(out[2 + n:2 + 2 * n]), out[2 + 2 * n]


def _scatter_wait(send_sems, recv_sems, hs, lands, after):
    n = len(hs)

    def body(*refs):
        ins, zones = refs[:n], refs[n:2 * n]
        send_sems, recv_sems = refs[2 * n], refs[2 * n + 1]
        x, y, c, chips = _place()
        for k in range(n):
            for j, chip in enumerate(chips):
                p = 2 * chip[0] + chip[1]
                cp = _remote(ins[k].at[p], zones[k].at[p], send_sems.at[3 * k + j], recv_sems.at[3 * k + j], (x, y, c))
                cp.wait_send()
                cp.wait_recv()

    out = pl.pallas_call(
        body, name="scatter_wait", out_shape=[_hbm_like(a) for a in [*hs, *lands]],
        in_specs=[HBM_SPEC] * (2 * n) + [SEM_SPEC, SEM_SPEC, ANY], out_specs=[HBM_SPEC] * (2 * n),
        input_output_aliases={k: k for k in range(2 * n)},
        compiler_params=pltpu.CompilerParams(has_side_effects=DATAFLOW),
    )(*hs, *lands, send_sems, recv_sems, after)
    return list(out[n:])


def _split_start(name, arrays, n_sems, issue, extra=()):
    m = len(arrays)

    def body(*refs):
        issue(refs[:m], refs[m + len(extra)], refs[m + len(extra) + 1])

    out = pl.pallas_call(
        body, name=name,
        out_shape=(pltpu.SemaphoreType.DMA((n_sems,)), pltpu.SemaphoreType.DMA((n_sems,)), *[_hbm_like(a) for a in arrays]),
        in_specs=[HBM_SPEC] * m + [ANY] * len(extra), out_specs=(SEM_SPEC, SEM_SPEC, *[HBM_SPEC] * m),
        input_output_aliases={k: 2 + k for k in range(m)},
        compiler_params=pltpu.CompilerParams(has_side_effects=DATAFLOW),
    )(*[_in_hbm(a) for a in arrays], *extra)
    return out[0], out[1], list(out[2:])


def _split_wait(name, send_sems, recv_sems, arrays, after, drain):
    m = len(arrays)

    def body(*refs):
        drain(refs[:m], refs[m], refs[m + 1])

    out = pl.pallas_call(
        body, name=name, out_shape=[_hbm_like(a) for a in arrays],
        in_specs=[HBM_SPEC] * m + [SEM_SPEC, SEM_SPEC, ANY], out_specs=[HBM_SPEC] * m,
        input_output_aliases={k: k for k in range(m)},
        compiler_params=pltpu.CompilerParams(has_side_effects=DATAFLOW),
    )(*arrays, send_sems, recv_sems, after)
    return list(out)


def _wait_both(cp):
    cp.wait_send()
    cp.wait_recv()


class _Flight:
    def __init__(self, name, arrays, n_sems, issue, drain, thru, extra=()):
        self.name, self.drain, self.n = name, drain, len(arrays)
        self.send, self.recv, out = _split_start(name + "_start", [*arrays, thru], n_sems, issue, extra)
        self.arrays, self.thru = out[:-1], out[-1]

    def land(self, after):
        return _split_wait(self.name + "_wait", self.send, self.recv, self.arrays, after, self.drain)


def _gather_flight(tag, ici, d2d, thru, extra=()):
    ni = len(ici)

    def issue(refs, send_sems, recv_sems):
        x, y, c, chips = _place()
        q = 2 * x + y
        for k in range(len(ici) + len(d2d)):
            for j, chip in enumerate(chips):
                if k < ni:
                    src, to = refs[k].at[q, c], (*chip, c)
                else:
                    src, to = refs[k].at[2 * chip[0] + chip[1], c], (x, y, 1 - c)
                _remote(src, src, send_sems.at[3 * k + j], recv_sems.at[3 * k + j], to).start()

    def drain(refs, send_sems, recv_sems):
        x, y, c, chips = _place()
        for k in range(len(ici) + len(d2d)):
            for j, chip in enumerate(chips):
                got = refs[k].at[2 * chip[0] + chip[1], c if k < ni else 1 - c]
                _wait_both(_remote(got, got, send_sems.at[3 * k + j], recv_sems.at[3 * k + j], (x, y, c)))

    return _Flight(f"gather{tag}", [*ici, *d2d], 3 * (len(ici) + len(d2d)), issue, drain, thru, extra)


def _swap_flight(tag, g4s, thru):
    n = len(g4s)
    zones = [lax.empty((N_CHIPS,) + g.shape[2:], g.dtype) for g in g4s]

    def issue(refs, send_sems, recv_sems):
        x, y, c, _ = _place()
        for k in range(n):
            for p in range(N_CHIPS):
                _remote(refs[k].at[p, 1 - c], refs[n + k].at[p], send_sems.at[N_CHIPS * k + p], recv_sems.at[N_CHIPS * k + p], (x, y, 1 - c)).start()

    def drain(refs, send_sems, recv_sems):
        x, y, c, _ = _place()
        for k in range(n):
            for p in range(N_CHIPS):
                got = refs[n + k].at[p]
                _wait_both(_remote(got, got, send_sems.at[N_CHIPS * k + p], recv_sems.at[N_CHIPS * k + p], (x, y, c)))

    return _Flight(f"swap{tag}", [*g4s, *zones], N_CHIPS * n, issue, drain, thru)


def _scatter_flight(tag, hs, lands, thru):
    n = len(hs)

    def issue(refs, send_sems, recv_sems):
        x, y, c, chips = _place()
        q = 2 * x + y
        for k in range(n):
            for j, chip in enumerate(chips):
                _remote(refs[k].at[2 * chip[0] + chip[1]], refs[n + k].at[q], send_sems.at[3 * k + j], recv_sems.at[3 * k + j], (*chip, c)).start()

    def drain(refs, send_sems, recv_sems):
        x, y, c, chips = _place()
        for k in range(n):
            for j, chip in enumerate(chips):
                got = refs[n + k].at[2 * chip[0] + chip[1]]
                _wait_both(_remote(got, got, send_sems.at[3 * k + j], recv_sems.at[3 * k + j], (x, y, c)))

    return _Flight(f"scatter{tag}", [*hs, *lands], 3 * n, issue, drain, thru)


def _share_flight(tag, fins, thru):
    n = len(fins)

    def issue(refs, send_sems, recv_sems):
        x, y, c, _ = _place()
        for k in range(n):
            _remote(refs[k].at[c], refs[k].at[c], send_sems.at[k], recv_sems.at[k], (x, y, 1 - c)).start()

    def drain(refs, send_sems, recv_sems):
        x, y, c, _ = _place()
        for k in range(n):
            got = refs[k].at[1 - c]
            _wait_both(_remote(got, got, send_sems.at[k], recv_sems.at[k], (x, y, c)))

    return _Flight(f"share{tag}", fins, n, issue, drain, thru)


def _pair_blocks(w):
    h, d, _ = w.shape
    z = jnp.zeros((h // 2, d, d), w.dtype)
    return jnp.concatenate([jnp.concatenate([w[0::2], z], axis=2), jnp.concatenate([z, w[1::2]], axis=2)], axis=1)


def _unpair_blocks(b):
    n, dd, _ = b.shape
    d = dd // 2
    return jnp.stack([b[:, :d, :d], b[:, d:, d:]], axis=1).reshape(2 * n, d, d)


def _pad_rows(a, rows):
    return jnp.pad(a, ((0, rows - a.shape[0]), (0, 0)))


class _Packer:
    def __init__(self, shapes, width=1024, row_multiple=64):
        self.shapes = shapes
        self.sizes = [math.prod(s) for s in shapes]
        total = sum(self.sizes)
        self.width = width
        self.rows = -(-total // (width * row_multiple)) * row_multiple
        self.pad = self.rows * width - total

    def pack(self, arrays):
        flat = jnp.concatenate([a.reshape(-1).astype(F32) for a in arrays] + [jnp.zeros((self.pad,), F32)])
        return flat.reshape(self.rows, self.width)

    def unpack(self, packed):
        flat = packed.reshape(-1)
        out, off = [], 0
        for s, n in zip(self.shapes, self.sizes):
            out.append(flat[off:off + n].reshape(s))
            off += n
        return out


SMALL = ["b_ada", "ffn1_norm", "mix_norm", "conv_w", "conv_b", "gate_a_w", "gate_a_b", "gate_x_w", "gate_x_b", "lru_lambda",
         "v_norm", "spatial_w", "spatial_b", "lru_out_norm", "gmlp_out_norm", "ffn2_norm", "final_norm"]
BIG = ["ffn1_w_gu", "ffn1_w_down", "w_in", "w_out", "ffn2_w_gu", "ffn2_w_down"]
GROUPS = (("ffn1_w_gu", "ffn1_w_down"), ("w_in", "w_out"), ("ffn2_w_gu", "ffn2_w_down"))
WEIGHTS = ["w_ada", "b_ada", "ffn1_norm", "ffn1_w_gu", "ffn1_w_down", "mix_norm", "w_in", "conv_w", "conv_b", "gate_a_w", "gate_a_b",
           "gate_x_w", "gate_x_b", "lru_lambda", "v_norm", "spatial_w", "spatial_b", "lru_out_norm", "gmlp_out_norm", "w_out",
           "ffn2_norm", "ffn2_w_gu", "ffn2_w_down", "final_norm"]


def kernel(x, c, w_ada, b_ada, ffn1_norm, ffn1_w_gu, ffn1_w_down, mix_norm, w_in, conv_w, conv_b, gate_a_w, gate_a_b, gate_x_w, gate_x_b, lru_lambda, v_norm, spatial_w, spatial_b, lru_out_norm, gmlp_out_norm, w_out, ffn2_norm, ffn2_w_gu, ffn2_w_down, final_norm, loss_target, m_w_ada, m_b_ada, m_ffn1_norm, m_ffn1_w_gu, m_ffn1_w_down, m_mix_norm, m_w_in, m_conv_w, m_conv_b, m_gate_a_w, m_gate_a_b, m_gate_x_w, m_gate_x_b, m_lru_lambda, m_v_norm, m_spatial_w, m_spatial_b, m_lru_out_norm, m_gmlp_out_norm, m_w_out, m_ffn2_norm, m_ffn2_w_gu, m_ffn2_w_down, m_final_norm, v_w_ada, v_b_ada, v_ffn1_norm, v_ffn1_w_gu, v_ffn1_w_down, v_mix_norm, v_w_in, v_conv_w, v_conv_b, v_gate_a_w, v_gate_a_b, v_gate_x_w, v_gate_x_b, v_lru_lambda, v_v_norm, v_spatial_w, v_spatial_b, v_lru_out_norm, v_gmlp_out_norm, v_w_out, v_ffn2_norm, v_ffn2_w_gu, v_ffn2_w_down, v_final_norm):
    given = dict(locals())
    W = {n: given[n] for n in WEIGHTS}
    L = w_ada.shape[0]
    S, D = x.shape[1], x.shape[2]
    LW = conv_b.shape[1]
    hd = LW // HEADS
    xi, yi, ci = lax.axis_index("x"), lax.axis_index("y"), lax.axis_index("c")
    chip = 2 * xi + yi
    dev = 2 * chip + ci
    place = jnp.stack([ci, chip]).astype(jnp.int32)
    xs = x.reshape(S, D)
    tgt = loss_target.reshape(S, D)

    def half_view(s):
        return s.reshape(N_CHIPS, 2, s.shape[1] // 2, s.shape[2])

    seq = [[half_view(_cast_into_slot(W[n], l, place)) for n in names] for l in range(L) for names in GROUPS]
    flights = {}

    def launch(t, thru, extra=()):
        ici = seq[t] if t < len(seq) else []
        d2d = seq[t - 1] if 1 <= t <= len(seq) else []
        if ici or d2d:
            flights[t] = _gather_flight(t, ici, d2d, thru, extra)
            thru = flights[t].thru
        return thru

    def land(t, after):
        if t in flights:
            out = flights.pop(t).land(after)
            ni = len(seq[t]) if t < len(seq) else 0
            if ni:
                seq[t] = out[:ni]
            if t >= 1:
                seq[t - 1] = out[ni:]

    def group_weights(t):
        return [s.reshape(N_CHIPS, -1, s.shape[3]) for s in seq[t]]

    c_all = _all_gather8(launch(0, _pad_rows(c, SUBLANES)))[:, 0, :]
    n_ada = w_ada.shape[2]
    b_shard = lax.dynamic_slice_in_dim(b_ada, chip * n_ada, n_ada, axis=1)
    mod_shard = _ada_fwd(_pad_rows(c_all, 2 * SUBLANES), w_ada, b_shard[:, None, :])
    mod_all = _all_gather8(mod_shard.reshape(L * 2 * SUBLANES, n_ada))
    mod_rows = lax.dynamic_index_in_dim(mod_all.reshape(N_CHIPS, 2, L, 2 * SUBLANES, n_ada)[:, 0], dev, axis=2, keepdims=False)
    mod = mod_rows.transpose(1, 0, 2).reshape(L, N_MOD, 1, D)

    land(0, mod)
    mod = launch(1, mod)
    land(1, mod)

    cws = LW // N_CHIPS
    conv_all = _all_gather8(_pad_rows(conv_w.reshape(L * CONV_WIDTH, cws), -(-L * CONV_WIDTH // SUBLANES) * SUBLANES))
    conv_full = conv_all.reshape(N_CHIPS, 2, -1, cws)[:, 0, :L * CONV_WIDTH].reshape(N_CHIPS, L, CONV_WIDTH, cws)
    conv_full = conv_full.transpose(1, 2, 0, 3).reshape(L, CONV_WIDTH, LW)

    tril = jnp.tril(jnp.ones((CHUNK, CHUNK), F32))
    seg = (jnp.arange(LW)[:, None] // hd == jnp.arange(LW)[None, :] // hd).astype(jnp.bfloat16)

    def mixer_params(l):
        ws = spatial_w[l] * tril
        wsp = jnp.concatenate([ws[0::2], ws[1::2]], axis=2)
        wa, wx = _pair_blocks(gate_a_w[l]), _pair_blocks(gate_x_w[l])
        return dict(
            cw=conv_full[l], cb=conv_b[l][None],
            wa=wa.astype(MXU_DTYPE), wx=wx.astype(MXU_DTYPE), wat=wa.transpose(0, 2, 1).astype(MXU_DTYPE), wxt=wx.transpose(0, 2, 1).astype(MXU_DTYPE),
            ba=gate_a_b[l].reshape(1, LW), bx=gate_x_b[l].reshape(1, LW), lam=lru_lambda[l][None], gv=v_norm[l][None],
            wsp=wsp.astype(MXU_DTYPE), wspt=wsp.transpose(0, 2, 1).astype(MXU_DTYPE),
            bfull=jnp.repeat(spatial_b[l].T, hd, axis=1), g_lru=lru_out_norm[l][None], g_gm=gmlp_out_norm[l][None])

    saved = []
    xcur = xs
    for l in range(L):
        mp, md = mixer_params(l), mod[l]
        s = dict(lw={}, mp=mp, md=md)
        lw = s["lw"]
        t = len(GROUPS) * l
        s["x0"] = xcur
        s["h1"] = launch(t + 2, _modnorm(xcur, ffn1_norm[l][None], md[0], md[1]))
        lw["gu1"], d1 = group_weights(t)
        lw["d1"] = d1.reshape(-1, D)
        s["a1"], s["gu1"] = _ffn_up(s["h1"], lw["gu1"])
        s["f1"], xcur = _mm_res(s["a1"], lw["d1"], xcur, md[2], 0.5)
        land(t + 2, xcur)
        s["x1"] = xcur
        s["h2"] = launch(t + 3, _modnorm(xcur, mix_norm[l][None], md[3], md[4]))
        lw["win"], wout = group_weights(t + 1)
        lw["wout"] = wout.reshape(-1, D)
        s["proj"] = _mm_chunks(s["h2"], lw["win"])
        s["ylru"] = _lru_fwd(s["proj"], mp["cw"], mp["cb"], mp["wa"], mp["ba"], mp["wx"], mp["bx"], mp["lam"])
        s["yn"], s["ygm"] = _gmlp_fwd(s["proj"], s["ylru"], mp["gv"], seg, mp["wsp"], mp["bfull"], mp["g_lru"], mp["g_gm"])
        s["f2"], xcur = _mm_res(s["yn"], lw["wout"], xcur, md[5], 1.0)
        land(t + 3, xcur)
        s["x2"] = xcur
        s["h3"] = launch(t + 4, _modnorm(xcur, ffn2_norm[l][None], md[6], md[7]))
        lw["gu2"], d2 = group_weights(t + 2)
        lw["d2"] = d2.reshape(-1, D)
        s["a3"], s["gu3"] = _ffn_up(s["h3"], lw["gu2"])
        s["f3"], xcur = _mm_res(s["a3"], lw["d2"], xcur, md[8], 0.5)
        land(t + 4, xcur)
        saved.append(s)

    dx, dq, head_acc = _loss_head(xcur, tgt, final_norm[None], saved[-1]["md"][8], 0.5)
    loss = lax.psum(jnp.sum(head_acc[1]), ("x", "y", "c"))
    small_grads = {}
    big_grads = {n: [None] * L for n in BIG}
    dmods = [None] * L
    zero_row = jnp.zeros((1, D), F32)

    def ffn_bwd(dx, dq, x_in, h, a, gu, f, wgu, wd, gn, sc, next_gate, next_scale):
        d_wd = _mm_tn_chunks(a, dq[None], 1408, 1024)[0]
        dgu = _ffn_bwd_act(dq, wd, gu)
        C = dgu.shape[3]
        dgu4 = dgu.reshape(N_CHIPS, S, C)
        d_wgu = _mm_tn_chunks(h, dgu4, 1024, C)
        dh = _mm_nt_chunks(dgu4, wgu)
        dx, dq, acc = _norm_bwd(x_in, dh, dx, f, gn, sc, 0.5, next_gate, next_scale)
        return dx, dq, acc, d_wgu, d_wd.reshape(N_CHIPS, -1, D)

    stepped = {n: None for n in BIG}
    reducing = []

    def move_on(after, thru):
        for grp in list(reducing):
            landed = grp["flight"].land(after)
            n = len(grp["names"])
            if grp["step"] == "swap":
                pairs = [_add_half(g4, r1, place) for g4, r1 in zip(landed[:n], landed[n:])]
                grp.update(step="scatter", flight=_scatter_flight(grp["tag"], [h for h, _ in pairs], [own for _, own in pairs], thru))
            elif grp["step"] == "scatter":
                grp.update(step="share", flight=_share_flight(grp["tag"], [_sum4_into_half(r2, place) for r2 in landed[n:]], thru))
            else:
                for name, fin in zip(grp["names"], landed):
                    g = fin.reshape(2 * fin.shape[1], fin.shape[2])
                    stepped[name] = _adamw_layer(W[name], g, given["m_" + name], given["v_" + name], grp["l"], stepped[name])
                reducing.remove(grp)
                continue
            thru = grp["flight"].thru
        return thru

    def reduce_group(names, l, after, thru):
        thru = move_on(after, thru)
        g4s = [big_grads[n][l].reshape(N_CHIPS, 2, big_grads[n][l].shape[1] // 2, big_grads[n][l].shape[2]) for n in names]
        tag = f"{l}{GROUPS.index(names)}"
        reducing.append(dict(names=names, l=l, tag=tag, step="swap", flight=_swap_flight(tag, g4s, thru)))
        return reducing[-1]["flight"].thru
    for l in reversed(range(L)):
        s = saved[l]
        lw, mp, md = s["lw"], s["mp"], s["md"]
        dx, dq, acc3, big_grads["ffn2_w_gu"][l], big_grads["ffn2_w_down"][l] = ffn_bwd(
            dx, dq, s["x2"], s["h3"], s["a3"], s["gu3"], s["f3"], lw["gu2"], lw["d2"], ffn2_norm[l][None], md[7], md[5], 1.0)
        dq = reduce_group(GROUPS[2], l, dx, dq)
        big_grads["w_out"][l] = _mm_tn_chunks(s["yn"], dq[None], 1024, 1024)[0].reshape(N_CHIPS, -1, D)
        dyn = _mm_nt_chunks(dq[None], lw["wout"][None])
        dylru, duv, dwsp, dbfull, gacc = _gmlp_bwd(s["proj"], s["ylru"], s["ygm"], dyn, mp["gv"], seg, mp["wsp"], mp["wspt"], mp["bfull"], mp["g_lru"], mp["g_gm"])
        dxg, dwa, dwx, lvec = _lru_bwd(s["proj"], dylru, mp["cw"], mp["cb"], mp["wa"], mp["ba"], mp["wx"], mp["bx"], mp["lam"], mp["wat"], mp["wxt"])
        dproj = jnp.concatenate([dxg, duv], axis=0)
        big_grads["w_in"][l] = _mm_tn_chunks(s["h2"], dproj, 1024, LW)
        dh2 = _mm_nt_chunks(dproj, lw["win"])
        dx, dq, acc2 = _norm_bwd(s["x1"], dh2, dx, s["f2"], mix_norm[l][None], md[4], 1.0, md[2], 0.5)
        dq = reduce_group(GROUPS[1], l, dx, dq)
        if l > 0:
            ng, ns = saved[l - 1]["md"][8], 0.5
        else:
            ng, ns = zero_row, 0.0
        dx, dq, acc1, big_grads["ffn1_w_gu"][l], big_grads["ffn1_w_down"][l] = ffn_bwd(
            dx, dq, s["x0"], s["h1"], s["a1"], s["gu1"], s["f1"], lw["gu1"], lw["d1"], ffn1_norm[l][None], md[1], ng, ns)

        dmods[l] = jnp.concatenate([acc1[0:2], acc1[3:4], acc2[0:2], acc2[3:4], acc3[0:2], acc3[3:4]], axis=0)
        dws = jnp.stack([dwsp[:, :, :CHUNK], dwsp[:, :, CHUNK:]], axis=1).reshape(HEADS, CHUNK, CHUNK) * tril
        lg = {"ffn1_norm": acc1[2], "mix_norm": acc2[2], "ffn2_norm": acc3[2],
              "conv_w": lvec[4:8], "conv_b": lvec[3], "gate_a_w": _unpair_blocks(dwa), "gate_a_b": lvec[0].reshape(HEADS, hd),
              "gate_x_w": _unpair_blocks(dwx), "gate_x_b": lvec[1].reshape(HEADS, hd), "lru_lambda": lvec[2], "v_norm": gacc[2],
              "spatial_w": dws, "spatial_b": dbfull.reshape(CHUNK, HEADS, hd).sum(-1).T, "lru_out_norm": gacc[0], "gmlp_out_norm": gacc[1]}
        for n, g in lg.items():
            small_grads.setdefault(n, [None] * L)[l] = g

        dq = reduce_group(GROUPS[0], l, dx, dq)
    while reducing:
        dq = move_on(dq, dq)
    grad_x = dx.reshape(x.shape)

    per_layer = [n for n in SMALL if n not in ("b_ada", "final_norm")]
    part = [jnp.stack(small_grads[n]) for n in per_layer] + [head_acc[0], jnp.stack(dmods)]
    packer = _Packer([p.shape for p in part])
    gathered = _all_gather8(packer.pack(part))
    summed = packer.unpack(_sum_leading(gathered))
    grads = dict(zip(per_layer + ["final_norm"], summed[:-1]))
    grads["b_ada"] = summed[-1].reshape(L, N_MOD * D)
    off = sum(packer.sizes[:-1])
    dmod_rows = gathered.reshape(N_DEV, -1)[:, off:off + L * N_MOD * D].reshape(N_DEV, L, N_MOD * D)
    dmod_shard = lax.dynamic_slice_in_dim(dmod_rows, chip * n_ada, n_ada, axis=2).transpose(1, 0, 2)
    grads["w_ada"] = _ada_grad(c_all.T, dmod_shard)
    grads["conv_w"] = lax.dynamic_slice_in_dim(grads["conv_w"], chip * cws, cws, axis=2)

    delta, new_m, new_v = {}, {}, {}
    for n in BIG:
        grads[n], delta[n], new_m[n], new_v[n] = stepped[n]
    shp = w_ada.shape
    d_, m_, v_ = _adamw(*[a.reshape(-1, shp[-1]) for a in (w_ada, grads["w_ada"], m_w_ada, v_w_ada)])
    delta["w_ada"], new_m["w_ada"], new_v["w_ada"] = d_.reshape(shp), m_.reshape(shp), v_.reshape(shp)
    spk = _Packer([W[n].shape for n in SMALL])
    d_, m_, v_ = _adamw(spk.pack([W[n] for n in SMALL]), spk.pack([grads[n] for n in SMALL]),
                        spk.pack([given["m_" + n] for n in SMALL]), spk.pack([given["v_" + n] for n in SMALL]))
    for n, a, b, e in zip(SMALL, spk.unpack(d_), spk.unpack(m_), spk.unpack(v_)):
        delta[n], new_m[n], new_v[n] = a, b, e
    grads = {n: grads[n].reshape(W[n].shape) for n in WEIGHTS}
    return (loss, grad_x, *[grads[n] for n in WEIGHTS], *[delta[n] for n in WEIGHTS], *[new_m[n] for n in WEIGHTS], *[new_v[n] for n in WEIGHTS])
```

```python
import math

import jax
import jax.numpy as jnp
from jax import lax
from jax.experimental import pallas as pl
from jax.experimental.pallas import tpu as pltpu

F32 = jnp.float32
MXU_DTYPE = jnp.bfloat16
ACT_DTYPE = jnp.bfloat16
XFER_DTYPE = jnp.bfloat16
EPS = 1e-6
RG_LRU_C = 8.0
N_MOD = 9
CONV_WIDTH = 4
HEADS = 8
CHUNK = 128
LANES = 128
SUBLANES = 8
N_CHIPS = 4
N_DEV = 8
ADAM_LR, ADAM_B1, ADAM_B2, ADAM_EPS, ADAM_WD, ADAM_STEP = 0.001, 0.9, 0.999, 1e-08, 0.01, 10
VMEM_LIMIT_BYTES = 60 * 1024 * 1024
ROW_TILE_BYTES = 1 << 20
GELU_C = math.sqrt(2.0 / math.pi)
GELU_A = 0.044715

ANY = pl.BlockSpec(memory_space=pl.ANY)
MESH = pl.DeviceIdType.MESH
SDS = jax.ShapeDtypeStruct


def _params(*sem):
    return pltpu.CompilerParams(dimension_semantics=sem, vmem_limit_bytes=VMEM_LIMIT_BYTES)


def _dot(a, b):
    return jnp.dot(a.astype(MXU_DTYPE), b.astype(MXU_DTYPE), preferred_element_type=F32)


def _dot_nt(a, b):
    return lax.dot_general(a.astype(MXU_DTYPE), b.astype(MXU_DTYPE), (((1,), (1,)), ((), ())), preferred_element_type=F32)


def _dot_tn(a, b):
    return lax.dot_general(a.astype(MXU_DTYPE), b.astype(MXU_DTYPE), (((0,), (0,)), ((), ())), preferred_element_type=F32)


def _gelu(x):
    return x * (0.5 * (1.0 + jnp.tanh(GELU_C * (x + GELU_A * (x * x * x)))))


def _gelu_grad(x):
    t = jnp.tanh(GELU_C * (x + GELU_A * (x * x * x)))
    return 0.5 * (1.0 + t) + 0.5 * x * (1.0 - t * t) * (GELU_C * (1.0 + 3.0 * GELU_A * x * x))


def _sigmoid(x):
    return jax.nn.sigmoid(x)


def _rsqrt_ms(x):
    return lax.rsqrt(jnp.mean(x * x, axis=-1, keepdims=True) + EPS)


def _rowsum(x):
    return jnp.sum(x, axis=0, keepdims=True)


def _tile(n, want):
    t = min(n, want)
    assert n % t == 0, (n, want)
    return t


def _row_tile(rows, row_bytes):
    step = 2 * SUBLANES
    cap = max(step, ROW_TILE_BYTES // row_bytes)
    best = None
    for t in range(step, min(rows, cap) + 1, step):
        if rows % t == 0:
            best = t
    assert best is not None, (rows, row_bytes)
    return best


def _modnorm(x, gn, sh, sc):
    S, D = x.shape
    tm = _tile(S, 512)

    def body(x_ref, gn_ref, sh_ref, sc_ref, h_ref):
        xv = x_ref[...]
        h = (xv * _rsqrt_ms(xv) * gn_ref[...]) * (1.0 + sc_ref[...]) + sh_ref[...]
        h_ref[...] = h.astype(ACT_DTYPE)

    row = pl.BlockSpec((1, D), lambda i: (0, 0))
    return pl.pallas_call(
        body, name="modnorm", grid=(S // tm,),
        in_specs=[pl.BlockSpec((tm, D), lambda i: (i, 0)), row, row, row],
        out_specs=pl.BlockSpec((tm, D), lambda i: (i, 0)),
        out_shape=SDS((S, D), ACT_DTYPE), compiler_params=_params("parallel"),
    )(x, gn, sh, sc)


def _norm_bwd(x, dh, dxo, f, gn, sc, res_scale, next_gate, next_scale):
    S, D = x.shape
    tm = _tile(S, 256)

    def body(x_ref, dh_ref, dxo_ref, f_ref, gn_ref, sc_ref, ng_ref, dx_ref, dq_ref, acc_ref):
        @pl.when(pl.program_id(0) == 0)
        def _():
            acc_ref[...] = jnp.zeros_like(acc_ref)

        xv, dh, dxo = x_ref[...], dh_ref[...], dxo_ref[...]
        r = _rsqrt_ms(xv)
        xhat = xv * r
        gn = gn_ref[...]
        dn = dh * (1.0 + sc_ref[...])
        dxh = dn * gn
        dx = dxo + r * (dxh - xhat * jnp.mean(dxh * xhat, axis=-1, keepdims=True))
        dx_ref[...] = dx
        dq_ref[...] = ((next_scale * ng_ref[...]) * dx).astype(ACT_DTYPE)
        acc_ref[0:1, :] += _rowsum(dh)
        acc_ref[1:2, :] += _rowsum(dh * (xhat * gn))
        acc_ref[2:3, :] += _rowsum(dn * xhat)
        acc_ref[3:4, :] += _rowsum((res_scale * f_ref[...]) * dxo)

    tile = pl.BlockSpec((tm, D), lambda i: (i, 0))
    row = pl.BlockSpec((1, D), lambda i: (0, 0))
    return pl.pallas_call(
        body, name="norm_bwd", grid=(S // tm,),
        in_specs=[tile, tile, tile, tile, row, row, row],
        out_specs=[tile, tile, pl.BlockSpec((SUBLANES, D), lambda i: (0, 0))],
        out_shape=[SDS((S, D), F32), SDS((S, D), ACT_DTYPE), SDS((SUBLANES, D), F32)],
        compiler_params=_params("arbitrary"),
    )(x, dh, dxo, f, gn, sc, next_gate)


def _loss_head(x, target, gn, next_gate, next_scale):
    S, D = x.shape
    tm = _tile(S, 256)

    def body(x_ref, t_ref, gn_ref, ng_ref, dx_ref, dq_ref, acc_ref):
        @pl.when(pl.program_id(0) == 0)
        def _():
            acc_ref[...] = jnp.zeros_like(acc_ref)

        xv = x_ref[...]
        r = _rsqrt_ms(xv)
        xhat = xv * r
        gn = gn_ref[...]
        err = xhat * gn - t_ref[...]
        dy = err * (1.0 / D)
        dxh = dy * gn
        dx = r * (dxh - xhat * jnp.mean(dxh * xhat, axis=-1, keepdims=True))
        dx_ref[...] = dx
        dq_ref[...] = ((next_scale * ng_ref[...]) * dx).astype(ACT_DTYPE)
        acc_ref[0:1, :] += _rowsum(dy * xhat)
        acc_ref[1:2, :] += _rowsum(err * err) * (0.5 / D)

    tile = pl.BlockSpec((tm, D), lambda i: (i, 0))
    row = pl.BlockSpec((1, D), lambda i: (0, 0))
    return pl.pallas_call(
        body, name="loss_head", grid=(S // tm,),
        in_specs=[tile, tile, row, row],
        out_specs=[tile, tile, pl.BlockSpec((SUBLANES, D), lambda i: (0, 0))],
        out_shape=[SDS((S, D), F32), SDS((S, D), ACT_DTYPE), SDS((SUBLANES, D), F32)],
        compiler_params=_params("arbitrary"),
    )(x, target, gn, next_gate)


def _ffn_up(h, wgu):
    S, D = h.shape
    C = wgu.shape[2]
    tm = _tile(S, 512)

    def body(h_ref, wg_ref, wu_ref, a_ref, gu_ref):
        hv = h_ref[...]
        g = _dot(hv, wg_ref[...])
        u = _dot(hv, wu_ref[...])
        a_ref[...] = (g * _sigmoid(g) * u).astype(ACT_DTYPE)
        gu_ref[0] = g.astype(ACT_DTYPE)
        gu_ref[1] = u.astype(ACT_DTYPE)

    return pl.pallas_call(
        body, name="ffn_up", grid=(2, S // tm),
        in_specs=[
            pl.BlockSpec((tm, D), lambda j, i: (i, 0)),
            pl.BlockSpec((None, D, C), lambda j, i: (j, 0, 0)),
            pl.BlockSpec((None, D, C), lambda j, i: (2 + j, 0, 0)),
        ],
        out_specs=[
            pl.BlockSpec((tm, C), lambda j, i: (i, j)),
            pl.BlockSpec((2, None, tm, C), lambda j, i: (0, j, i, 0)),
        ],
        out_shape=[SDS((S, 2 * C), ACT_DTYPE), SDS((2, 2, S, C), ACT_DTYPE)],
        compiler_params=_params("parallel", "parallel"),
    )(h, wgu, wgu)


def _ffn_bwd_act(dq, wd, gu):
    S, D = dq.shape
    C = gu.shape[3]
    tm = _tile(S, 512)

    def body(dq_ref, wd_ref, gu_ref, dgu_ref):
        da = _dot_nt(dq_ref[...], wd_ref[...])
        g = gu_ref[0].astype(F32)
        u = gu_ref[1].astype(F32)
        s = _sigmoid(g)
        dgu_ref[0] = (da * u * (s * (1.0 + g * (1.0 - s)))).astype(ACT_DTYPE)
        dgu_ref[1] = (da * (g * s)).astype(ACT_DTYPE)

    gu_spec = pl.BlockSpec((2, None, tm, C), lambda j, i: (0, j, i, 0))
    return pl.pallas_call(
        body, name="ffn_bwd_act", grid=(2, S // tm),
        in_specs=[pl.BlockSpec((tm, D), lambda j, i: (i, 0)), pl.BlockSpec((C, D), lambda j, i: (j, 0)), gu_spec],
        out_specs=gu_spec,
        out_shape=SDS(gu.shape, ACT_DTYPE),
        compiler_params=_params("parallel", "parallel"),
    )(dq, wd, gu)


def _mm_res(a, w, x, gate, scale):
    S, K = a.shape
    D = w.shape[1]
    tm, tn = _tile(S, 1024), _tile(D, 512)

    def body(a_ref, w_ref, x_ref, g_ref, f_ref, xo_ref):
        f = _dot(a_ref[...], w_ref[...])
        f_ref[...] = f
        xo_ref[...] = x_ref[...] + (scale * g_ref[...]) * f

    tile = pl.BlockSpec((tm, tn), lambda j, i: (i, j))
    return pl.pallas_call(
        body, name=f"mm_res_k{K}", grid=(D // tn, S // tm),
        in_specs=[pl.BlockSpec((tm, K), lambda j, i: (i, 0)), pl.BlockSpec((K, tn), lambda j, i: (0, j)), tile,
                  pl.BlockSpec((1, tn), lambda j, i: (0, j))],
        out_specs=[tile, tile],
        out_shape=[SDS((S, D), F32), SDS((S, D), F32)],
        compiler_params=_params("parallel", "parallel"),
    )(a, w, x, gate)


def _mm_chunks(h, wc):
    S, K = h.shape
    P, _, N = wc.shape
    tm = _tile(S, 1024)

    def body(h_ref, w_ref, o_ref):
        o_ref[...] = _dot(h_ref[...], w_ref[...])

    return pl.pallas_call(
        body, name="mm_chunks", grid=(P, S // tm),
        in_specs=[pl.BlockSpec((tm, K), lambda j, i: (i, 0)), pl.BlockSpec((None, K, N), lambda j, i: (j, 0, 0))],
        out_specs=pl.BlockSpec((tm, N), lambda j, i: (i, j)),
        out_shape=SDS((S, P * N), F32),
        compiler_params=_params("parallel", "parallel"),
    )(h, wc)


def _mm_nt_chunks(ac, wc):
    P, S, K = ac.shape
    N = wc.shape[1]
    tm, tn = _tile(S, 1024), _tile(N, 512)

    def body(a_ref, w_ref, o_ref):
        acc = _dot_nt(a_ref[0], w_ref[0])
        for p in range(1, P):
            acc += _dot_nt(a_ref[p], w_ref[p])
        o_ref[...] = acc

    return pl.pallas_call(
        body, name=f"mm_nt_p{P}k{K}", grid=(S // tm, N // tn),
        in_specs=[pl.BlockSpec((P, tm, K), lambda i, j: (0, i, 0)), pl.BlockSpec((P, tn, K), lambda i, j: (0, j, 0))],
        out_specs=pl.BlockSpec((tm, tn), lambda i, j: (i, j)),
        out_shape=SDS((S, N), F32),
        compiler_params=_params("parallel", "parallel"),
    )(ac, wc)


def _mm_tn_chunks(a, bc, tile_m, tile_n):
    S, M = a.shape
    P, _, N = bc.shape
    ts, tm, tn = _tile(S, 512), _tile(M, tile_m), _tile(N, tile_n)

    def body(a_ref, b_ref, o_ref):
        @pl.when(pl.program_id(3) == 0)
        def _():
            o_ref[...] = jnp.zeros_like(o_ref)

        o_ref[...] += _dot_tn(a_ref[...], b_ref[...])

    return pl.pallas_call(
        body, name=f"mm_tn_m{M}n{N}", grid=(P, M // tm, N // tn, S // ts),
        in_specs=[pl.BlockSpec((ts, tm), lambda p, m, n, k: (k, m)), pl.BlockSpec((None, ts, tn), lambda p, m, n, k: (p, k, n))],
        out_specs=pl.BlockSpec((None, tm, tn), lambda p, m, n, k: (p, m, n)),
        out_shape=SDS((P, M, N), F32),
        compiler_params=_params("parallel", "parallel", "parallel", "arbitrary"),
    )(a, bc)


def _shift_down(x, s, row, fill):
    return jnp.where(row >= s, pltpu.roll(x, s, 0), fill)


def _shift_up(x, s, row, fill):
    n = x.shape[0]
    return jnp.where(row < n - s, pltpu.roll(x, n - s, 0), fill)


def _scan_down(a, b, row):
    s = 1
    while s < a.shape[0]:
        b = a * _shift_down(b, s, row, 0.0) + b
        a = a * _shift_down(a, s, row, 1.0)
        s *= 2
    return b


def _scan_up(a, b, row):
    s = 1
    while s < a.shape[0]:
        b = a * _shift_up(b, s, row, 0.0) + b
        a = a * _shift_up(a, s, row, 1.0)
        s *= 2
    return b


def _conv(xl, cw_ref, cb_ref, row):
    y = cb_ref[...] + _shift_down(xl, 3, row, 0.0) * cw_ref[0:1, :]
    y = y + _shift_down(xl, 2, row, 0.0) * cw_ref[1:2, :]
    y = y + _shift_down(xl, 1, row, 0.0) * cw_ref[2:3, :]
    return y + xl * cw_ref[3:4, :]


def _lru_gates(xc, wa_ref, ba_ref, wx_ref, bx_ref, lam_ref):
    ra = _sigmoid(_dot(xc, wa_ref[...]) + ba_ref[...])
    ri = _sigmoid(_dot(xc, wx_ref[...]) + bx_ref[...])
    ls = jax.nn.log_sigmoid(lam_ref[...])
    a = jnp.exp((RG_LRU_C * ra) * ls)
    mult = jnp.sqrt(1.0 - a * a)
    return ra, ri, ls, a, mult


def _lru_specs(S):
    col = lambda off: pl.BlockSpec((S, LANES), lambda j: (0, off + j))
    vec = pl.BlockSpec((1, LANES), lambda j: (0, j))
    blk = pl.BlockSpec((None, LANES, LANES), lambda j: (j, 0, 0))
    cw = pl.BlockSpec((CONV_WIDTH, LANES), lambda j: (0, j))
    return col, vec, blk, cw


def _lru_fwd(proj, cw, cb, wa, ba, wx, bx, lam):
    S = proj.shape[0]
    W = cb.shape[1]
    nb = W // LANES

    def body(xl_ref, gl_ref, cw_ref, cb_ref, wa_ref, ba_ref, wx_ref, bx_ref, lam_ref, y_ref):
        row = lax.broadcasted_iota(jnp.int32, (S, LANES), 0)
        xc = _conv(xl_ref[...], cw_ref, cb_ref, row)
        _, ri, _, a, mult = _lru_gates(xc, wa_ref, ba_ref, wx_ref, bx_ref, lam_ref)
        h = _scan_down(a, mult * (ri * xc), row)
        y_ref[...] = h * _gelu(gl_ref[...])

    col, vec, blk, cws = _lru_specs(S)
    return pl.pallas_call(
        body, name="lru_fwd", grid=(nb,),
        in_specs=[col(0), col(nb), cws, vec, blk, vec, blk, vec, vec],
        out_specs=pl.BlockSpec((S, LANES), lambda j: (0, j)),
        out_shape=SDS((S, W), F32), compiler_params=_params("parallel"),
    )(proj, proj, cw, cb, wa, ba, wx, bx, lam)


def _lru_bwd(proj, dy, cw, cb, wa, ba, wx, bx, lam, wat, wxt):
    S = proj.shape[0]
    W = cb.shape[1]
    nb = W // LANES

    def body(xl_ref, gl_ref, dy_ref, cw_ref, cb_ref, wa_ref, ba_ref, wx_ref, bx_ref, lam_ref, wat_ref, wxt_ref,
             dp_ref, dwa_ref, dwx_ref, vec_ref):
        row = lax.broadcasted_iota(jnp.int32, (S, LANES), 0)
        xl = xl_ref[...]
        xc = _conv(xl, cw_ref, cb_ref, row)
        ra, ri, ls, a, mult = _lru_gates(xc, wa_ref, ba_ref, wx_ref, bx_ref, lam_ref)
        h = _scan_down(a, mult * (ri * xc), row)
        gl = gl_ref[...]
        dyv = dy_ref[...]
        dp_ref[1] = (dyv * h * _gelu_grad(gl)).astype(ACT_DTYPE)
        adj = _scan_up(_shift_up(a, 1, row, 0.0), dyv * _gelu(gl), row)
        da = adj * _shift_down(h, 1, row, 0.0)
        dmult = adj * (ri * xc)
        dlog_a = da * a - dmult * (a * a) / mult
        dra = dlog_a * (RG_LRU_C * ls)
        dpa = dra * ra * (1.0 - ra)
        dpi = (adj * mult * xc) * ri * (1.0 - ri)
        dxc = adj * mult * ri + _dot(dpa, wat_ref[...]) + _dot(dpi, wxt_ref[...])
        dwa_ref[...] = _dot_tn(xc, dpa)
        dwx_ref[...] = _dot_tn(xc, dpi)
        dxl = dxc * cw_ref[3:4, :]
        dxl = dxl + _shift_up(dxc, 1, row, 0.0) * cw_ref[2:3, :]
        dxl = dxl + _shift_up(dxc, 2, row, 0.0) * cw_ref[1:2, :]
        dxl = dxl + _shift_up(dxc, 3, row, 0.0) * cw_ref[0:1, :]
        dp_ref[0] = dxl.astype(ACT_DTYPE)
        vec_ref[...] = jnp.zeros_like(vec_ref)
        vec_ref[0:1, :] = _rowsum(dpa)
        vec_ref[1:2, :] = _rowsum(dpi)
        vec_ref[2:3, :] = _rowsum(dlog_a * (RG_LRU_C * ra)) * _sigmoid(-lam_ref[...])
        vec_ref[3:4, :] = _rowsum(dxc)
        vec_ref[4:5, :] = _rowsum(dxc * _shift_down(xl, 3, row, 0.0))
        vec_ref[5:6, :] = _rowsum(dxc * _shift_down(xl, 2, row, 0.0))
        vec_ref[6:7, :] = _rowsum(dxc * _shift_down(xl, 1, row, 0.0))
        vec_ref[7:8, :] = _rowsum(dxc * xl)

    col, vec, blk, cws = _lru_specs(S)
    return pl.pallas_call(
        body, name="lru_bwd", grid=(nb,),
        in_specs=[col(0), col(nb), col(0), cws, vec, blk, vec, blk, vec, vec, blk, blk],
        out_specs=[pl.BlockSpec((2, S, LANES), lambda j: (0, 0, j)), blk, blk, pl.BlockSpec((2 * SUBLANES, LANES), lambda j: (0, j))],
        out_shape=[SDS((2, S, W), ACT_DTYPE), SDS((nb, LANES, LANES), F32), SDS((nb, LANES, LANES), F32), SDS((2 * SUBLANES, W), F32)],
        compiler_params=_params("parallel"),
    )(proj, proj, dy, cw, cb, wa, ba, wx, bx, lam, wat, wxt)


def _seg_mean(x, seg_ref, width):
    hi = x.astype(jnp.bfloat16)
    lo = (x - hi.astype(F32)).astype(jnp.bfloat16)
    ones = seg_ref[...]
    s = jnp.dot(hi, ones, preferred_element_type=F32) + jnp.dot(lo, ones, preferred_element_type=F32)
    return s * (1.0 / width)


def _gmlp_core(u_ref, v_ref, gv_ref, seg_ref, ws_ref, bfull_ref, z_scr, hd):
    tm, W = u_ref.shape
    lane = lax.broadcasted_iota(jnp.int32, (CHUNK, LANES), 1)
    ug = _gelu(u_ref[...])
    vg = _gelu(v_ref[...])
    cen = vg - _seg_mean(vg, seg_ref, hd)
    rstd = lax.rsqrt(_seg_mean(cen * cen, seg_ref, hd) + EPS)
    vhat = cen * rstd
    vh = vhat * gv_ref[...]
    vcats = {}
    for ci in range(tm // CHUNK):
        for p in range(W // LANES):
            blk = vh[ci * CHUNK:(ci + 1) * CHUNK, p * LANES:(p + 1) * LANES]
            vcat = jnp.concatenate([jnp.where(lane < hd, blk, 0.0), jnp.where(lane >= hd, blk, 0.0)], axis=0).astype(MXU_DTYPE)
            vcats[ci, p] = vcat
            z_scr[ci * CHUNK:(ci + 1) * CHUNK, p * LANES:(p + 1) * LANES] = (
                jnp.dot(ws_ref[p], vcat, preferred_element_type=F32) + bfull_ref[:, p * LANES:(p + 1) * LANES])
    return ug, vhat, rstd, vcats


def _gmlp_specs(tm, W, nb):
    rows = lambda off: pl.BlockSpec((tm, W), lambda i: (i, off))
    vec = pl.BlockSpec((1, W), lambda i: (0, 0))
    seg = pl.BlockSpec((W, W), lambda i: (0, 0))
    wsp = pl.BlockSpec((nb, CHUNK, 2 * CHUNK), lambda i: (0, 0, 0))
    bfull = pl.BlockSpec((CHUNK, W), lambda i: (0, 0))
    return rows, vec, seg, wsp, bfull


def _gmlp_fwd(proj, ylru, gv, seg, wsp, bfull, g_lru, g_gm):
    S, W = ylru.shape
    nb = W // LANES
    hd = W // HEADS
    tm = _tile(S, 512)

    def body(u_ref, v_ref, yl_ref, gv_ref, seg_ref, ws_ref, bfull_ref, gl_ref, gg_ref, yn_ref, ygm_ref, z_scr):
        ug, _, _, _ = _gmlp_core(u_ref, v_ref, gv_ref, seg_ref, ws_ref, bfull_ref, z_scr, hd)
        ygm = ug * z_scr[...]
        ygm_ref[...] = ygm
        yl = yl_ref[...]
        yn_ref[:, 0:W] = (yl * _rsqrt_ms(yl) * gl_ref[...]).astype(ACT_DTYPE)
        yn_ref[:, W:2 * W] = (ygm * _rsqrt_ms(ygm) * gg_ref[...]).astype(ACT_DTYPE)

    rows, vec, segs, wsps, bfulls = _gmlp_specs(tm, W, nb)
    return pl.pallas_call(
        body, name="gmlp_fwd", grid=(S // tm,),
        in_specs=[rows(2), rows(3), rows(0), vec, segs, wsps, bfulls, vec, vec],
        out_specs=[pl.BlockSpec((tm, 2 * W), lambda i: (i, 0)), rows(0)],
        out_shape=[SDS((S, 2 * W), ACT_DTYPE), SDS((S, W), F32)],
        scratch_shapes=[pltpu.VMEM((tm, W), F32)],
        compiler_params=_params("parallel"),
    )(proj, proj, ylru, gv, seg, wsp, bfull, g_lru, g_gm)


def _rms_bwd(y, g, dyn):
    r = _rsqrt_ms(y)
    yhat = y * r
    dyh = dyn * g
    return r * (dyh - yhat * jnp.mean(dyh * yhat, axis=-1, keepdims=True)), _rowsum(dyn * yhat)


def _gmlp_bwd(proj, ylru, ygm, dyn, gv, seg, wsp, wspt, bfull, g_lru, g_gm):
    S, W = ylru.shape
    nb = W // LANES
    hd = W // HEADS
    tm = _tile(S, 256)

    def body(u_ref, v_ref, yl_ref, ygm_ref, dl_ref, dg_ref, gv_ref, seg_ref, ws_ref, wst_ref, bfull_ref, gl_ref, gg_ref,
             dyl_ref, duv_ref, dws_ref, dbf_ref, acc_ref, z_scr, dvh_scr):
        @pl.when(pl.program_id(0) == 0)
        def _():
            dws_ref[...] = jnp.zeros_like(dws_ref)
            dbf_ref[...] = jnp.zeros_like(dbf_ref)
            acc_ref[...] = jnp.zeros_like(acc_ref)

        dyl, dgl = _rms_bwd(yl_ref[...], gl_ref[...], dl_ref[...])
        dyl_ref[...] = dyl
        dygm, dgg = _rms_bwd(ygm_ref[...], gg_ref[...], dg_ref[...])
        ug, vhat, rstd, vcats = _gmlp_core(u_ref, v_ref, gv_ref, seg_ref, ws_ref, bfull_ref, z_scr, hd)
        duv_ref[0] = (dygm * z_scr[...] * _gelu_grad(u_ref[...])).astype(ACT_DTYPE)
        dz = dygm * ug
        lane = lax.broadcasted_iota(jnp.int32, (CHUNK, LANES), 1)
        dbf = dz[0:CHUNK, :]
        for ci in range(1, tm // CHUNK):
            dbf += dz[ci * CHUNK:(ci + 1) * CHUNK, :]
        dbf_ref[...] += dbf
        for ci in range(tm // CHUNK):
            for p in range(nb):
                dzb = dz[ci * CHUNK:(ci + 1) * CHUNK, p * LANES:(p + 1) * LANES].astype(MXU_DTYPE)
                dws_ref[p] += _dot_nt(dzb, vcats[ci, p])
                dvc = jnp.dot(wst_ref[p], dzb, preferred_element_type=F32)
                dvh_scr[ci * CHUNK:(ci + 1) * CHUNK, p * LANES:(p + 1) * LANES] = jnp.where(lane < hd, dvc[0:CHUNK], dvc[CHUNK:2 * CHUNK])
        dvh = dvh_scr[...]
        dvn = dvh * gv_ref[...]
        dvg = rstd * (dvn - _seg_mean(dvn, seg_ref, hd) - vhat * _seg_mean(dvn * vhat, seg_ref, hd))
        duv_ref[1] = (dvg * _gelu_grad(v_ref[...])).astype(ACT_DTYPE)
        acc_ref[0:1, :] += dgl
        acc_ref[1:2, :] += dgg
        acc_ref[2:3, :] += _rowsum(dvh * vhat)

    rows, vec, segs, wsps, bfulls = _gmlp_specs(tm, W, nb)
    wspt_spec = pl.BlockSpec((nb, 2 * CHUNK, CHUNK), lambda i: (0, 0, 0))
    return pl.pallas_call(
        body, name="gmlp_bwd", grid=(S // tm,),
        in_specs=[rows(2), rows(3), rows(0), rows(0), rows(0), rows(1), vec, segs, wsps, wspt_spec, bfulls, vec, vec],
        out_specs=[rows(0), pl.BlockSpec((2, tm, W), lambda i: (0, i, 0)), wsps, bfulls, pl.BlockSpec((SUBLANES, W), lambda i: (0, 0))],
        out_shape=[SDS((S, W), F32), SDS((2, S, W), ACT_DTYPE), SDS((nb, CHUNK, 2 * CHUNK), F32), SDS((CHUNK, W), F32), SDS((SUBLANES, W), F32)],
        scratch_shapes=[pltpu.VMEM((tm, W), F32), pltpu.VMEM((tm, W), F32)],
        compiler_params=_params("arbitrary"),
    )(proj, proj, ylru, ygm, dyn, dyn, gv, seg, wsp, wspt, bfull, g_lru, g_gm)


def _ada_fwd(c_all, w_ada, b_shard):
    L, D, N = w_ada.shape
    R = c_all.shape[0]
    tn = N // 2

    def body(c_ref, w_ref, b_ref, o_ref):
        cv = c_ref[...]
        o_ref[...] = _dot(cv * _sigmoid(cv), w_ref[...]) + b_ref[...]

    return pl.pallas_call(
        body, name="ada_fwd", grid=(L, N // tn),
        in_specs=[pl.BlockSpec((R, D), lambda l, j: (0, 0)), pl.BlockSpec((None, D, tn), lambda l, j: (l, 0, j)),
                  pl.BlockSpec((None, 1, tn), lambda l, j: (l, 0, j))],
        out_specs=pl.BlockSpec((None, R, tn), lambda l, j: (l, 0, j)),
        out_shape=SDS((L, R, N), F32), compiler_params=_params("parallel", "parallel"),
    )(c_all, w_ada, b_shard)


def _ada_grad(c_all_t, dmod):
    D, B = c_all_t.shape
    L, _, N = dmod.shape
    tn = N // 2

    def body(c_ref, d_ref, o_ref):
        cv = c_ref[...]
        sc = cv * _sigmoid(cv)
        acc = sc[:, 0:1] * d_ref[0:1, :]
        for b in range(1, B):
            acc += sc[:, b:b + 1] * d_ref[b:b + 1, :]
        o_ref[...] = acc

    return pl.pallas_call(
        body, name="ada_grad", grid=(L, N // tn),
        in_specs=[pl.BlockSpec((D, B), lambda l, j: (0, 0)), pl.BlockSpec((None, B, tn), lambda l, j: (l, 0, j))],
        out_specs=pl.BlockSpec((None, D, tn), lambda l, j: (l, 0, j)),
        out_shape=SDS((L, D, N), F32), compiler_params=_params("parallel", "parallel"),
    )(c_all_t, dmod)


def _adamw(w, g, m, v):
    R, C = w.shape
    tr = _row_tile(R, C * 4)

    def body(w_ref, g_ref, m_ref, v_ref, d_ref, mo_ref, vo_ref):
        d_ref[...], mo_ref[...], vo_ref[...] = _adam_math(w_ref[...], g_ref[...], m_ref[...], v_ref[...])

    tile = pl.BlockSpec((tr, C), lambda i: (i, 0))
    return pl.pallas_call(
        body, name=f"adamw_r{R}c{C}", grid=(R // tr,), in_specs=[tile] * 4, out_specs=[tile] * 3,
        out_shape=[SDS((R, C), F32)] * 3, compiler_params=_params("parallel"),
    )(w, g, m, v)


def _adam_math(w, g, m, v):
    mn = ADAM_B1 * m + (1.0 - ADAM_B1) * g
    vn = ADAM_B2 * v + (1.0 - ADAM_B2) * (g * g)
    m_hat = mn / (1.0 - ADAM_B1 ** ADAM_STEP)
    v_hat = vn / (1.0 - ADAM_B2 ** ADAM_STEP)
    return -ADAM_LR * (m_hat / (jnp.sqrt(v_hat) + ADAM_EPS) + ADAM_WD * w), mn, vn


def _adamw_layer(w, g, m, v, l, prev):
    L, R, C = w.shape
    tr = _row_tile(R, C * 4)
    prev = () if prev is None else tuple(prev)

    def body(w_ref, g_ref, m_ref, v_ref, *rest):
        go_ref, d_ref, mo_ref, vo_ref = rest[len(prev):]
        gv = g_ref[...]
        go_ref[...] = gv
        d_ref[...], mo_ref[...], vo_ref[...] = _adam_math(w_ref[...], gv, m_ref[...], v_ref[...])

    lay = pl.BlockSpec((None, tr, C), lambda i: (l, i, 0))
    return pl.pallas_call(
        body, name=f"adamw_layer_r{R}c{C}", grid=(R // tr,),
        in_specs=[lay, pl.BlockSpec((tr, C), lambda i: (i, 0)), lay, lay] + [ANY] * len(prev), out_specs=[lay] * 4,
        out_shape=[SDS((L, R, C), F32)] * 4, input_output_aliases={4 + k: k for k in range(len(prev))},
        compiler_params=_params("parallel"),
    )(w, g, m, v, *prev)


def _sum_leading(a):
    P, R, C = a.shape
    tr = _row_tile(R, P * C * 4)

    def body(a_ref, o_ref):
        acc = a_ref[0]
        for p in range(1, P):
            acc = acc + a_ref[p]
        o_ref[...] = acc

    return pl.pallas_call(
        body, name=f"sum{P}_r{R}c{C}", grid=(R // tr,),
        in_specs=[pl.BlockSpec((P, tr, C), lambda i: (0, i, 0))],
        out_specs=pl.BlockSpec((tr, C), lambda i: (i, 0)),
        out_shape=SDS((R, C), F32), compiler_params=_params("parallel"),
    )(a)


def _add_half(g4, r1, place):
    _, _, R, C = g4.shape
    tr = _row_tile(R, C * 4)

    def body(place_ref, g_ref, r_ref, h_ref, own_ref):
        s = (g_ref[...] + r_ref[...]).astype(XFER_DTYPE)
        h_ref[...] = s

        @pl.when(pl.program_id(1) == place_ref[1])
        def _():
            own_ref[...] = s

    return pl.pallas_call(
        body, name=f"add_half_r{R}c{C}",
        grid_spec=pltpu.PrefetchScalarGridSpec(
            num_scalar_prefetch=1, grid=(R // tr, N_CHIPS),
            in_specs=[pl.BlockSpec((None, None, tr, C), lambda i, p, place_ref: (p, place_ref[0], i, 0)),
                      pl.BlockSpec((None, tr, C), lambda i, p, place_ref: (p, i, 0))],
            out_specs=[pl.BlockSpec((None, tr, C), lambda i, p, place_ref: (p, i, 0)),
                       pl.BlockSpec((None, tr, C), lambda i, p, place_ref: (place_ref[1], i, 0))],
        ),
        out_shape=[SDS((N_CHIPS, R, C), XFER_DTYPE)] * 2, compiler_params=_params("parallel", "arbitrary"),
    )(place, g4, r1)


def _sum4_into_half(r2, place):
    P, R, C = r2.shape
    tr = _row_tile(R, P * C * 4)

    def body(place_ref, a_ref, o_ref):
        acc = a_ref[0].astype(F32)
        for p in range(1, P):
            acc = acc + a_ref[p].astype(F32)
        o_ref[...] = acc

    return pl.pallas_call(
        body, name=f"sum4_r{R}c{C}",
        grid_spec=pltpu.PrefetchScalarGridSpec(
            num_scalar_prefetch=1, grid=(R // tr,),
            in_specs=[pl.BlockSpec((P, tr, C), lambda i, place_ref: (0, i, 0))],
            out_specs=pl.BlockSpec((None, tr, C), lambda i, place_ref: (place_ref[0], i, 0)),
        ),
        out_shape=SDS((2, R, C), F32), compiler_params=_params("parallel"),
    )(place, r2)


def _cast_into_slot(w, l, place, after):
    _, R, C = w.shape
    tr = _row_tile(R, C * 4)

    def body(place_ref, w_ref, after_ref, o_ref):
        o_ref[...] = w_ref[...].astype(MXU_DTYPE)

    return pl.pallas_call(
        body, name=f"cast_r{R}c{C}",
        grid_spec=pltpu.PrefetchScalarGridSpec(
            num_scalar_prefetch=1, grid=(R // tr,),
            in_specs=[pl.BlockSpec((None, tr, C), lambda i, place_ref: (l, i, 0)), ANY],
            out_specs=pl.BlockSpec((None, tr, C), lambda i, place_ref: (place_ref[1], i, 0)),
        ),
        out_shape=SDS((N_CHIPS, R, C), MXU_DTYPE), compiler_params=_params("parallel"),
    )(place, w, after)


def _place():
    x, y, c = lax.axis_index("x"), lax.axis_index("y"), lax.axis_index("c")
    chips = [(1 - x, y), (x, 1 - y), (1 - x, 1 - y)]
    return x, y, c, chips


def _remote(src, dst, send_sem, recv_sem, to):
    return pltpu.make_async_remote_copy(src_ref=src, dst_ref=dst, send_sem=send_sem, recv_sem=recv_sem, device_id=to, device_id_type=MESH)


def _all_gather8(v):
    R, N = v.shape

    def body(v_ref, out_ref, send_sems, recv_sems, local_sem):
        x, y, c, chips = _place()
        me, sibling = (x, y, c), (x, y, 1 - c)

        def slot(px, py, pc):
            return out_ref.at[4 * px + 2 * py + pc]

        def copy(k, block, to, src=None):
            return _remote(slot(*block) if src is None else src, slot(*block), send_sems.at[k], recv_sems.at[k], to)

        mine = pltpu.make_async_copy(v_ref, slot(*me), local_sem)
        mine.start()
        first = [copy(0, me, sibling, src=v_ref)] + [copy(1 + j, me, (*chip, c), src=v_ref) for j, chip in enumerate(chips)]
        for cp in first:
            cp.start()
        passed = [copy(4 + j, (*chip, c), sibling) for j, chip in enumerate(chips)]
        for j, chip in enumerate(chips):
            copy(1 + j, (*chip, c), me).wait_recv()
            passed[j].start()
        copy(0, sibling, me).wait_recv()
        for j, chip in enumerate(chips):
            copy(4 + j, (*chip, 1 - c), me).wait_recv()
        for cp in first + passed:
            cp.wait_send()
        mine.wait()

    return pl.pallas_call(
        body, name=f"all_gather8_r{R}n{N}", out_shape=SDS((N_DEV, R, N), v.dtype), in_specs=[ANY], out_specs=ANY,
        scratch_shapes=[pltpu.SemaphoreType.DMA((7,)), pltpu.SemaphoreType.DMA((7,)), pltpu.SemaphoreType.DMA],
    )(v)


def _gather_weights(slots):
    n = len(slots)

    def body(*refs):
        ins, outs = refs[:n], refs[n:2 * n]
        send_sems, recv_sems = refs[2 * n:]
        x, y, c, chips = _place()
        q = 2 * x + y
        sibling = (x, y, 1 - c)
        first = []
        for k in range(n):
            for j, chip in enumerate(chips):
                first.append(_remote(ins[k].at[q, c], outs[k].at[q, c], send_sems.at[k, j], recv_sems.at[k, j], (*chip, c)))
                first[-1].start()
        passed = []
        for k in range(n):
            for j, chip in enumerate(chips):
                half = outs[k].at[2 * chip[0] + chip[1], c]
                _remote(half, half, send_sems.at[k, j], recv_sems.at[k, j], sibling).wait_recv()
                passed.append(_remote(half, half, send_sems.at[k, 3 + j], recv_sems.at[k, 3 + j], sibling))
                passed[-1].start()
        for k in range(n):
            for j, chip in enumerate(chips):
                half = outs[k].at[2 * chip[0] + chip[1], 1 - c]
                _remote(half, half, send_sems.at[k, 3 + j], recv_sems.at[k, 3 + j], sibling).wait_recv()
        for cp in first + passed:
            cp.wait_send()

    return pl.pallas_call(
        body, name="gather_weights", out_shape=[SDS(s.shape, s.dtype) for s in slots],
        in_specs=[ANY] * n, out_specs=[ANY] * n, input_output_aliases={k: k for k in range(n)},
        scratch_shapes=[pltpu.SemaphoreType.DMA((n, 6)), pltpu.SemaphoreType.DMA((n, 6))],
    )(*slots)


def _swap_halves(g4s):
    n = len(g4s)

    def body(*refs):
        ins, outs = refs[:n], refs[n:2 * n]
        send_sems, recv_sems = refs[2 * n:]
        x, y, c, _ = _place()
        sibling = (x, y, 1 - c)
        for k in range(n):
            for p in range(N_CHIPS):
                _remote(ins[k].at[p, 1 - c], outs[k].at[p], send_sems.at[k], recv_sems.at[k], sibling).start()
        for k in range(n):
            _remote(outs[k], outs[k], send_sems.at[k], recv_sems.at[k], sibling).wait()

    return pl.pallas_call(
        body, name="swap_halves", out_shape=[SDS((N_CHIPS,) + g.shape[2:], g.dtype) for g in g4s],
        in_specs=[ANY] * n, out_specs=[ANY] * n,
        scratch_shapes=[pltpu.SemaphoreType.DMA((n,)), pltpu.SemaphoreType.DMA((n,))],
    )(*g4s)


def _scatter_regions(hs, lands):
    n = len(hs)

    def body(*refs):
        ins, outs = refs[:n], refs[2 * n:3 * n]
        send_sems, recv_sems = refs[3 * n:]
        x, y, c, chips = _place()
        q = 2 * x + y
        sent = []
        for k in range(n):
            for j, chip in enumerate(chips):
                sent.append(_remote(ins[k].at[2 * chip[0] + chip[1]], outs[k].at[q], send_sems.at[k, j], recv_sems.at[k, j], (*chip, c)))
                sent[-1].start()
        for k in range(n):
            for j, chip in enumerate(chips):
                got = outs[k].at[2 * chip[0] + chip[1]]
                _remote(got, got, send_sems.at[k, j], recv_sems.at[k, j], (x, y, c)).wait_recv()
        for cp in sent:
            cp.wait_send()

    return pl.pallas_call(
        body, name="scatter_regions", out_shape=[SDS(h.shape, h.dtype) for h in lands],
        in_specs=[ANY] * (2 * n), out_specs=[ANY] * n, input_output_aliases={n + k: k for k in range(n)},
        scratch_shapes=[pltpu.SemaphoreType.DMA((n, 3)), pltpu.SemaphoreType.DMA((n, 3))],
    )(*hs, *lands)


def _share_halves(fins):
    n = len(fins)

    def body(*refs):
        ins, outs = refs[:n], refs[n:2 * n]
        send_sems, recv_sems = refs[2 * n:]
        x, y, c, _ = _place()
        sibling = (x, y, 1 - c)
        sent = [_remote(ins[k].at[c], outs[k].at[c], send_sems.at[k], recv_sems.at[k], sibling) for k in range(n)]
        for cp in sent:
            cp.start()
        for k in range(n):
            got = outs[k].at[1 - c]
            _remote(got, got, send_sems.at[k], recv_sems.at[k], sibling).wait_recv()
        for cp in sent:
            cp.wait_send()

    return pl.pallas_call(
        body, name="share_halves", out_shape=[SDS(t.shape, t.dtype) for t in fins],
        in_specs=[ANY] * n, out_specs=[ANY] * n, input_output_aliases={k: k for k in range(n)},
        scratch_shapes=[pltpu.SemaphoreType.DMA((n,)), pltpu.SemaphoreType.DMA((n,))],
    )(*fins)


def _chip_sums(grads, place):
    g4s = [g.reshape(N_CHIPS, 2, g.shape[1] // 2, g.shape[2]) for g in grads]
    pairs = [_add_half(g4, r1, place) for g4, r1 in zip(g4s, _swap_halves(g4s))]
    return [h for h, _ in pairs], [own for _, own in pairs]


def _reduce_finish(lands, place):
    fins = _share_halves([_sum4_into_half(r2, place) for r2 in lands])
    return [f.reshape(2 * f.shape[1], f.shape[2]) for f in fins]


def _reduce_scatter(grads, place):
    hs, lands = _chip_sums(grads, place)
    return _reduce_finish(_scatter_regions(hs, lands), place)


HBM_SPEC = pl.BlockSpec(memory_space=pltpu.HBM)
SEM_SPEC = pl.BlockSpec(memory_space=pltpu.SEMAPHORE)
DATAFLOW = pltpu.SideEffectType.DATAFLOW_SIDE_EFFECTING


def _in_hbm(a):
    return pltpu.with_memory_space_constraint(a, pltpu.HBM)


def _hbm_like(a):
    return pltpu.HBM(a.shape, a.dtype)


def _gather_ici_start(slots, thru, after):
    n = len(slots)

    def body(*refs):
        ins = refs[:n]
        send_sems, recv_sems = refs[n + 2], refs[n + 3]
        x, y, c, chips = _place()
        q = 2 * x + y
        for k in range(n):
            for j, chip in enumerate(chips):
                _remote(ins[k].at[q, c], ins[k].at[q, c], send_sems.at[3 * k + j], recv_sems.at[3 * k + j], (*chip, c)).start()

    out = pl.pallas_call(
        body, name="gather_ici_start",
        out_shape=(pltpu.SemaphoreType.DMA((3 * n,)), pltpu.SemaphoreType.DMA((3 * n,)), *[_hbm_like(s) for s in slots], _hbm_like(thru)),
        in_specs=[HBM_SPEC] * (n + 1) + [ANY], out_specs=(SEM_SPEC, SEM_SPEC, *[HBM_SPEC] * (n + 1)),
        input_output_aliases={k: 2 + k for k in range(n + 1)},
        compiler_params=pltpu.CompilerParams(has_side_effects=DATAFLOW),
    )(*[_in_hbm(s) for s in slots], _in_hbm(thru), after)
    return out[0], out[1], list(out[2:2 + n]), out[2 + n]


def _gather_ici_wait(send_sems, recv_sems, slots, after):
    n = len(slots)

    def body(*refs):
        ins = refs[:n]
        send_sems, recv_sems = refs[n], refs[n + 1]
        x, y, c, chips = _place()
        for k in range(n):
            for j, chip in enumerate(chips):
                got = ins[k].at[2 * chip[0] + chip[1], c]
                cp = _remote(got, got, send_sems.at[3 * k + j], recv_sems.at[3 * k + j], (x, y, c))
                cp.wait_send()
                cp.wait_recv()

    out = pl.pallas_call(
        body, name="gather_ici_wait", out_shape=[_hbm_like(s) for s in slots],
        in_specs=[HBM_SPEC] * n + [SEM_SPEC, SEM_SPEC, ANY], out_specs=[HBM_SPEC] * n,
        input_output_aliases={k: k for k in range(n)},
        compiler_params=pltpu.CompilerParams(has_side_effects=DATAFLOW),
    )(*slots, send_sems, recv_sems, after)
    return list(out)


def _gather_pass_on(slots):
    n = len(slots)

    def body(*refs):
        ins, outs = refs[:n], refs[n:2 * n]
        send_sems, recv_sems = refs[2 * n:]
        x, y, c, chips = _place()
        sibling = (x, y, 1 - c)
        passed = []
        for k in range(n):
            for j, chip in enumerate(chips):
                passed.append(_remote(ins[k].at[2 * chip[0] + chip[1], c], outs[k].at[2 * chip[0] + chip[1], c],
                                      send_sems.at[k, j], recv_sems.at[k, j], sibling))
                passed[-1].start()
        for k in range(n):
            for j, chip in enumerate(chips):
                half = outs[k].at[2 * chip[0] + chip[1], 1 - c]
                _remote(half, half, send_sems.at[k, j], recv_sems.at[k, j], sibling).wait_recv()
        for cp in passed:
            cp.wait_send()

    return pl.pallas_call(
        body, name="gather_pass_on", out_shape=[SDS(s.shape, s.dtype) for s in slots],
        in_specs=[ANY] * n, out_specs=[ANY] * n, input_output_aliases={k: k for k in range(n)},
        scratch_shapes=[pltpu.SemaphoreType.DMA((n, 3)), pltpu.SemaphoreType.DMA((n, 3))],
    )(*slots)


def _scatter_start(hs, lands, thru):
    n = len(hs)

    def body(*refs):
        ins, zones = refs[:n], refs[n:2 * n]
        send_sems, recv_sems = refs[2 * n + 1], refs[2 * n + 2]
        x, y, c, chips = _place()
        q = 2 * x + y
        for k in range(n):
            for j, chip in enumerate(chips):
                _remote(ins[k].at[2 * chip[0] + chip[1]], zones[k].at[q], send_sems.at[3 * k + j], recv_sems.at[3 * k + j], (*chip, c)).start()

    arrays = [*hs, *lands, thru]
    out = pl.pallas_call(
        body, name="scatter_start",
        out_shape=(pltpu.SemaphoreType.DMA((3 * n,)), pltpu.SemaphoreType.DMA((3 * n,)), *[_hbm_like(a) for a in arrays]),
        in_specs=[HBM_SPEC] * len(arrays), out_specs=(SEM_SPEC, SEM_SPEC, *[HBM_SPEC] * len(arrays)),
        input_output_aliases={k: 2 + k for k in range(len(arrays))},
        compiler_params=pltpu.CompilerParams(has_side_effects=DATAFLOW),
    )(*[_in_hbm(a) for a in arrays])
    return out[0], out[1], list(out[2:2 + n]), list(out[2 + n:2 + 2 * n]), out[2 + 2 * n]


def _scatter_wait(send_sems, recv_sems, hs, lands, after):
    n = len(hs)

    def body(*refs):
        ins, zones = refs[:n], refs[n:2 * n]
        send_sems, recv_sems = refs[2 * n], refs[2 * n + 1]
        x, y, c, chips = _place()
        for k in range(n):
            for j, chip in enumerate(chips):
                p = 2 * chip[0] + chip[1]
                cp = _remote(ins[k].at[p], zones[k].at[p], send_sems.at[3 * k + j], recv_sems.at[3 * k + j], (x, y, c))
                cp.wait_send()
                cp.wait_recv()

    out = pl.pallas_call(
        body, name="scatter_wait", out_shape=[_hbm_like(a) for a in [*hs, *lands]],
        in_specs=[HBM_SPEC] * (2 * n) + [SEM_SPEC, SEM_SPEC, ANY], out_specs=[HBM_SPEC] * (2 * n),
        input_output_aliases={k: k for k in range(2 * n)},
        compiler_params=pltpu.CompilerParams(has_side_effects=DATAFLOW),
    )(*hs, *lands, send_sems, recv_sems, after)
    return list(out[n:])


def _split_start(name, arrays, n_sems, issue, extra=()):
    m = len(arrays)

    def body(*refs):
        issue(refs[:m], refs[m + len(extra)], refs[m + len(extra) + 1])

    out = pl.pallas_call(
        body, name=name,
        out_shape=(pltpu.SemaphoreType.DMA((n_sems,)), pltpu.SemaphoreType.DMA((n_sems,)), *[_hbm_like(a) for a in arrays]),
        in_specs=[HBM_SPEC] * m + [ANY] * len(extra), out_specs=(SEM_SPEC, SEM_SPEC, *[HBM_SPEC] * m),
        input_output_aliases={k: 2 + k for k in range(m)},
        compiler_params=pltpu.CompilerParams(has_side_effects=DATAFLOW),
    )(*[_in_hbm(a) for a in arrays], *extra)
    return out[0], out[1], list(out[2:])


def _split_wait(name, send_sems, recv_sems, arrays, after, drain):
    m = len(arrays)

    def body(*refs):
        drain(refs[:m], refs[m], refs[m + 1])

    out = pl.pallas_call(
        body, name=name, out_shape=[_hbm_like(a) for a in arrays],
        in_specs=[HBM_SPEC] * m + [SEM_SPEC, SEM_SPEC, ANY], out_specs=[HBM_SPEC] * m,
        input_output_aliases={k: k for k in range(m)},
        compiler_params=pltpu.CompilerParams(has_side_effects=DATAFLOW),
    )(*arrays, send_sems, recv_sems, after)
    return list(out)


def _wait_both(cp):
    cp.wait_send()
    cp.wait_recv()


class _Flight:
    def __init__(self, name, arrays, n_sems, issue, drain, thru, extra=()):
        self.name, self.drain, self.n = name, drain, len(arrays)
        self.send, self.recv, out = _split_start(name + "_start", [*arrays, thru], n_sems, issue, extra)
        self.arrays, self.thru = out[:-1], out[-1]

    def land(self, after):
        return _split_wait(self.name + "_wait", self.send, self.recv, self.arrays, after, self.drain)


def _gather_flight(tag, ici, d2d, direct, thru):
    kinds = ["ici"] * len(ici) + ["d2d"] * len(d2d) + ["direct"] * len(direct)

    def issue(refs, send_sems, recv_sems):
        x, y, c, chips = _place()
        q = 2 * x + y
        for k, kind in enumerate(kinds):
            for j, chip in enumerate(chips):
                if kind == "ici":
                    src, to = refs[k].at[q, c], (*chip, c)
                elif kind == "d2d":
                    src, to = refs[k].at[2 * chip[0] + chip[1], c], (x, y, 1 - c)
                else:
                    src, to = refs[k].at[q], (*chip, c)
                _remote(src, src, send_sems.at[3 * k + j], recv_sems.at[3 * k + j], to).start()

    def drain(refs, send_sems, recv_sems):
        x, y, c, chips = _place()
        for k, kind in enumerate(kinds):
            for j, chip in enumerate(chips):
                p = 2 * chip[0] + chip[1]
                got = refs[k].at[p] if kind == "direct" else refs[k].at[p, c if kind == "ici" else 1 - c]
                _wait_both(_remote(got, got, send_sems.at[3 * k + j], recv_sems.at[3 * k + j], (x, y, c)))

    return _Flight(f"gather{tag}", [*ici, *d2d, *direct], 3 * len(kinds), issue, drain, thru)


def _swap_flight(tag, g4s, thru):
    n = len(g4s)
    zones = [lax.empty((N_CHIPS,) + g.shape[2:], g.dtype) for g in g4s]

    def issue(refs, send_sems, recv_sems):
        x, y, c, _ = _place()
        for k in range(n):
            for p in range(N_CHIPS):
                _remote(refs[k].at[p, 1 - c], refs[n + k].at[p], send_sems.at[N_CHIPS * k + p], recv_sems.at[N_CHIPS * k + p], (x, y, 1 - c)).start()

    def drain(refs, send_sems, recv_sems):
        x, y, c, _ = _place()
        for k in range(n):
            for p in range(N_CHIPS):
                got = refs[n + k].at[p]
                _wait_both(_remote(got, got, send_sems.at[N_CHIPS * k + p], recv_sems.at[N_CHIPS * k + p], (x, y, c)))

    return _Flight(f"swap{tag}", [*g4s, *zones], N_CHIPS * n, issue, drain, thru)


def _scatter_flight(tag, hs, lands, thru):
    n = len(hs)

    def issue(refs, send_sems, recv_sems):
        x, y, c, chips = _place()
        q = 2 * x + y
        for k in range(n):
            for j, chip in enumerate(chips):
                _remote(refs[k].at[2 * chip[0] + chip[1]], refs[n + k].at[q], send_sems.at[3 * k + j], recv_sems.at[3 * k + j], (*chip, c)).start()

    def drain(refs, send_sems, recv_sems):
        x, y, c, chips = _place()
        for k in range(n):
            for j, chip in enumerate(chips):
                got = refs[n + k].at[2 * chip[0] + chip[1]]
                _wait_both(_remote(got, got, send_sems.at[3 * k + j], recv_sems.at[3 * k + j], (x, y, c)))

    return _Flight(f"scatter{tag}", [*hs, *lands], 3 * n, issue, drain, thru)


def _share_flight(tag, fins, thru):
    n = len(fins)

    def issue(refs, send_sems, recv_sems):
        x, y, c, _ = _place()
        for k in range(n):
            _remote(refs[k].at[c], refs[k].at[c], send_sems.at[k], recv_sems.at[k], (x, y, 1 - c)).start()

    def drain(refs, send_sems, recv_sems):
        x, y, c, _ = _place()
        for k in range(n):
            got = refs[k].at[1 - c]
            _wait_both(_remote(got, got, send_sems.at[k], recv_sems.at[k], (x, y, c)))

    return _Flight(f"share{tag}", fins, n, issue, drain, thru)


def _pair_blocks(w):
    h, d, _ = w.shape
    z = jnp.zeros((h // 2, d, d), w.dtype)
    return jnp.concatenate([jnp.concatenate([w[0::2], z], axis=2), jnp.concatenate([z, w[1::2]], axis=2)], axis=1)


def _unpair_blocks(b):
    n, dd, _ = b.shape
    d = dd // 2
    return jnp.stack([b[:, :d, :d], b[:, d:, d:]], axis=1).reshape(2 * n, d, d)


def _pad_rows(a, rows):
    return jnp.pad(a, ((0, rows - a.shape[0]), (0, 0)))


class _Packer:
    def __init__(self, shapes, width=1024, row_multiple=64):
        self.shapes = shapes
        self.sizes = [math.prod(s) for s in shapes]
        total = sum(self.sizes)
        self.width = width
        self.rows = -(-total // (width * row_multiple)) * row_multiple
        self.pad = self.rows * width - total

    def pack(self, arrays):
        flat = jnp.concatenate([a.reshape(-1).astype(F32) for a in arrays] + [jnp.zeros((self.pad,), F32)])
        return flat.reshape(self.rows, self.width)

    def unpack(self, packed):
        flat = packed.reshape(-1)
        out, off = [], 0
        for s, n in zip(self.shapes, self.sizes):
            out.append(flat[off:off + n].reshape(s))
            off += n
        return out


SMALL = ["b_ada", "ffn1_norm", "mix_norm", "conv_w", "conv_b", "gate_a_w", "gate_a_b", "gate_x_w", "gate_x_b", "lru_lambda",
         "v_norm", "spatial_w", "spatial_b", "lru_out_norm", "gmlp_out_norm", "ffn2_norm", "final_norm"]
BIG = ["ffn1_w_gu", "ffn1_w_down", "w_in", "w_out", "ffn2_w_gu", "ffn2_w_down"]
GROUPS = (("ffn1_w_gu", "ffn1_w_down"), ("w_in", "w_out"), ("ffn2_w_gu", "ffn2_w_down"))
FWD_GROUPS = (("ffn1_w_gu",), ("ffn1_w_down",), ("w_in", "w_out"), ("ffn2_w_gu",), ("ffn2_w_down",))
MIN_AGE = {"swap": 1, "scatter": 2, "share": 1}
WEIGHTS = ["w_ada", "b_ada", "ffn1_norm", "ffn1_w_gu", "ffn1_w_down", "mix_norm", "w_in", "conv_w", "conv_b", "gate_a_w", "gate_a_b",
           "gate_x_w", "gate_x_b", "lru_lambda", "v_norm", "spatial_w", "spatial_b", "lru_out_norm", "gmlp_out_norm", "w_out",
           "ffn2_norm", "ffn2_w_gu", "ffn2_w_down", "final_norm"]


def kernel(x, c, w_ada, b_ada, ffn1_norm, ffn1_w_gu, ffn1_w_down, mix_norm, w_in, conv_w, conv_b, gate_a_w, gate_a_b, gate_x_w, gate_x_b, lru_lambda, v_norm, spatial_w, spatial_b, lru_out_norm, gmlp_out_norm, w_out, ffn2_norm, ffn2_w_gu, ffn2_w_down, final_norm, loss_target, m_w_ada, m_b_ada, m_ffn1_norm, m_ffn1_w_gu, m_ffn1_w_down, m_mix_norm, m_w_in, m_conv_w, m_conv_b, m_gate_a_w, m_gate_a_b, m_gate_x_w, m_gate_x_b, m_lru_lambda, m_v_norm, m_spatial_w, m_spatial_b, m_lru_out_norm, m_gmlp_out_norm, m_w_out, m_ffn2_norm, m_ffn2_w_gu, m_ffn2_w_down, m_final_norm, v_w_ada, v_b_ada, v_ffn1_norm, v_ffn1_w_gu, v_ffn1_w_down, v_mix_norm, v_w_in, v_conv_w, v_conv_b, v_gate_a_w, v_gate_a_b, v_gate_x_w, v_gate_x_b, v_lru_lambda, v_v_norm, v_spatial_w, v_spatial_b, v_lru_out_norm, v_gmlp_out_norm, v_w_out, v_ffn2_norm, v_ffn2_w_gu, v_ffn2_w_down, v_final_norm):
    given = dict(locals())
    W = {n: given[n] for n in WEIGHTS}
    L = w_ada.shape[0]
    S, D = x.shape[1], x.shape[2]
    LW = conv_b.shape[1]
    hd = LW // HEADS
    xi, yi, ci = lax.axis_index("x"), lax.axis_index("y"), lax.axis_index("c")
    chip = 2 * xi + yi
    dev = 2 * chip + ci
    place = jnp.stack([ci, chip]).astype(jnp.int32)
    xs = x.reshape(S, D)
    tgt = loss_target.reshape(S, D)

    c_all = _all_gather8(_pad_rows(c, SUBLANES))[:, 0, :]
    n_ada = w_ada.shape[2]
    b_shard = lax.dynamic_slice_in_dim(b_ada, chip * n_ada, n_ada, axis=1)
    mod_shard = _ada_fwd(_pad_rows(c_all, 2 * SUBLANES), w_ada, b_shard[:, None, :])

    def in_slot(block):
        return lax.dynamic_update_index_in_dim(jnp.zeros((N_CHIPS,) + block.shape, block.dtype), block, chip, 0)

    cws = LW // N_CHIPS
    small = [in_slot(mod_shard.reshape(L * 2 * SUBLANES, n_ada)), in_slot(conv_w.reshape(L * CONV_WIDTH, cws))]

    def half_view(s):
        return s.reshape(N_CHIPS, 2, s.shape[1] // 2, s.shape[2])

    stages = [(l, names) for l in range(L) for names in FWD_GROUPS]
    seq = [[half_view(_cast_into_slot(W[n], l, place, place)) for n in names] for l, names in stages[:1]]
    flights = {}

    def launch(t, thru, direct=()):
        ici = seq[t] if t < len(seq) else []
        d2d = seq[t - 1] if 1 <= t <= len(seq) else []
        if ici or d2d or direct:
            flights[t] = _gather_flight(t, ici, d2d, list(direct), thru)
            thru = flights[t].thru
        return thru

    def land(t, after):
        if t not in flights:
            return []
        out = flights.pop(t).land(after)
        ni = len(seq[t]) if t < len(seq) else 0
        nd = len(seq[t - 1]) if 1 <= t <= len(seq) else 0
        if ni:
            seq[t] = out[:ni]
        if nd:
            seq[t - 1] = out[ni:ni + nd]
        return out[ni + nd:]

    def group_weights(t):
        return [s.reshape(N_CHIPS, -1, s.shape[3]) for s in seq[t]]

    c_all = launch(0, c_all, small)
    seq += [[half_view(_cast_into_slot(W[n], l, place, c_all)) for n in names] for l, names in stages[1:]]
    mod_all, conv_all = land(0, seq[-1][-1])
    mod_all = launch(1, mod_all)
    land(1, mod_all)
    mod_rows = lax.dynamic_index_in_dim(mod_all.reshape(N_CHIPS, L, 2 * SUBLANES, n_ada), dev, axis=2, keepdims=False)
    mod = mod_rows.transpose(1, 0, 2).reshape(L, N_MOD, 1, D)
    conv_full = conv_all.reshape(N_CHIPS, L, CONV_WIDTH, cws).transpose(1, 2, 0, 3).reshape(L, CONV_WIDTH, LW)

    tril = jnp.tril(jnp.ones((CHUNK, CHUNK), F32))
    seg = (jnp.arange(LW)[:, None] // hd == jnp.arange(LW)[None, :] // hd).astype(jnp.bfloat16)

    def mixer_params(l):
        ws = spatial_w[l] * tril
        wsp = jnp.concatenate([ws[0::2], ws[1::2]], axis=2)
        wa, wx = _pair_blocks(gate_a_w[l]), _pair_blocks(gate_x_w[l])
        return dict(
            cw=conv_full[l], cb=conv_b[l][None],
            wa=wa.astype(MXU_DTYPE), wx=wx.astype(MXU_DTYPE), wat=wa.transpose(0, 2, 1).astype(MXU_DTYPE), wxt=wx.transpose(0, 2, 1).astype(MXU_DTYPE),
            ba=gate_a_b[l].reshape(1, LW), bx=gate_x_b[l].reshape(1, LW), lam=lru_lambda[l][None], gv=v_norm[l][None],
            wsp=wsp.astype(MXU_DTYPE), wspt=wsp.transpose(0, 2, 1).astype(MXU_DTYPE),
            bfull=jnp.repeat(spatial_b[l].T, hd, axis=1), g_lru=lru_out_norm[l][None], g_gm=gmlp_out_norm[l][None])

    saved = []
    xcur = xs
    for l in range(L):
        mp, md = mixer_params(l), mod[l]
        s = dict(lw={}, mp=mp, md=md)
        lw = s["lw"]
        t = len(FWD_GROUPS) * l
        s["x0"] = xcur
        s["h1"] = launch(t + 2, _modnorm(xcur, ffn1_norm[l][None], md[0], md[1]))
        lw["gu1"], = group_weights(t)
        s["a1"], s["gu1"] = _ffn_up(s["h1"], lw["gu1"])
        land(t + 2, s["a1"])
        s["a1"] = launch(t + 3, s["a1"])
        lw["d1"] = group_weights(t + 1)[0].reshape(-1, D)
        s["f1"], xcur = _mm_res(s["a1"], lw["d1"], xcur, md[2], 0.5)
        land(t + 3, xcur)
        s["x1"] = xcur
        s["h2"] = launch(t + 4, _modnorm(xcur, mix_norm[l][None], md[3], md[4]))
        lw["win"], wout = group_weights(t + 2)
        lw["wout"] = wout.reshape(-1, D)
        s["proj"] = _mm_chunks(s["h2"], lw["win"])
        s["ylru"] = _lru_fwd(s["proj"], mp["cw"], mp["cb"], mp["wa"], mp["ba"], mp["wx"], mp["bx"], mp["lam"])
        s["yn"], s["ygm"] = _gmlp_fwd(s["proj"], s["ylru"], mp["gv"], seg, mp["wsp"], mp["bfull"], mp["g_lru"], mp["g_gm"])
        s["f2"], xcur = _mm_res(s["yn"], lw["wout"], xcur, md[5], 1.0)
        land(t + 4, xcur)
        s["x2"] = xcur
        s["h3"] = launch(t + 5, _modnorm(xcur, ffn2_norm[l][None], md[6], md[7]))
        lw["gu2"], = group_weights(t + 3)
        s["a3"], s["gu3"] = _ffn_up(s["h3"], lw["gu2"])
        land(t + 5, s["a3"])
        s["a3"] = launch(t + 6, s["a3"])
        lw["d2"] = group_weights(t + 4)[0].reshape(-1, D)
        s["f3"], xcur = _mm_res(s["a3"], lw["d2"], xcur, md[8], 0.5)
        land(t + 6, xcur)
        saved.append(s)

    dx, dq, head_acc = _loss_head(xcur, tgt, final_norm[None], saved[-1]["md"][8], 0.5)
    loss = lax.psum(jnp.sum(head_acc[1]), ("x", "y", "c"))
    small_grads = {}
    big_grads = {n: [None] * L for n in BIG}
    dmods = [None] * L
    zero_row = jnp.zeros((1, D), F32)

    def ffn_bwd(names, l, dx, dq, x_in, h, a, gu, f, wgu, wd, gn, sc, next_gate, next_scale):
        big_grads[names[1]][l] = _mm_tn_chunks(a, dq[None], 1408, 1024)[0].reshape(N_CHIPS, -1, D)
        dgu = _ffn_bwd_act(dq, wd, gu)
        C = dgu.shape[3]
        dgu4 = dgu.reshape(N_CHIPS, S, C)
        big_grads[names[0]][l] = _mm_tn_chunks(h, dgu4, 1024, C)
        dgu4 = reduce_group(names, l, big_grads[names[0]][l], dgu4)
        dh = _mm_nt_chunks(dgu4, wgu)
        dh = move_on(dh, dh)
        dx, dq, acc = _norm_bwd(x_in, dh, dx, f, gn, sc, 0.5, next_gate, next_scale)
        return dx, move_on(dx, dq), acc

    stepped = {n: None for n in BIG}
    reducing = []

    clock = [0]

    def move_on(after, thru, force=False):
        clock[0] += 1
        for grp in list(reducing):
            if not force and clock[0] - grp["since"] < MIN_AGE[grp["step"]]:
                continue
            grp["since"] = clock[0]
            landed = grp["flight"].land(after)
            n = len(grp["names"])
            if grp["step"] == "swap":
                pairs = [_add_half(g4, r1, place) for g4, r1 in zip(landed[:n], landed[n:])]
                grp.update(step="scatter", flight=_scatter_flight(grp["tag"], [h for h, _ in pairs], [own for _, own in pairs], thru))
            elif grp["step"] == "scatter":
                grp.update(step="share", flight=_share_flight(grp["tag"], [_sum4_into_half(r2, place) for r2 in landed[n:]], thru))
            else:
                for name, fin in zip(grp["names"], landed):
                    g = fin.reshape(2 * fin.shape[1], fin.shape[2])
                    stepped[name] = _adamw_layer(W[name], g, given["m_" + name], given["v_" + name], grp["l"], stepped[name])
                reducing.remove(grp)
                continue
            thru = grp["flight"].thru
        return thru

    def reduce_group(names, l, after, thru):
        thru = move_on(after, thru)
        g4s = [big_grads[n][l].reshape(N_CHIPS, 2, big_grads[n][l].shape[1] // 2, big_grads[n][l].shape[2]) for n in names]
        tag = f"{l}{GROUPS.index(names)}"
        reducing.append(dict(names=names, l=l, tag=tag, step="swap", since=clock[0], flight=_swap_flight(tag, g4s, thru)))
        return reducing[-1]["flight"].thru
    for l in reversed(range(L)):
        s = saved[l]
        lw, mp, md = s["lw"], s["mp"], s["md"]
        dx, dq, acc3 = ffn_bwd(
            GROUPS[2], l, dx, dq, s["x2"], s["h3"], s["a3"], s["gu3"], s["f3"], lw["gu2"], lw["d2"], ffn2_norm[l][None], md[7], md[5], 1.0)
        big_grads["w_out"][l] = _mm_tn_chunks(s["yn"], dq[None], 1024, 1024)[0].reshape(N_CHIPS, -1, D)
        dyn = _mm_nt_chunks(dq[None], lw["wout"][None])
        dylru, duv, dwsp, dbfull, gacc = _gmlp_bwd(s["proj"], s["ylru"], s["ygm"], dyn, mp["gv"], seg, mp["wsp"], mp["wspt"], mp["bfull"], mp["g_lru"], mp["g_gm"])
        dxg, dwa, dwx, lvec = _lru_bwd(s["proj"], dylru, mp["cw"], mp["cb"], mp["wa"], mp["ba"], mp["wx"], mp["bx"], mp["lam"], mp["wat"], mp["wxt"])
        dproj = jnp.concatenate([dxg, duv], axis=0)
        big_grads["w_in"][l] = _mm_tn_chunks(s["h2"], dproj, 1024, LW)
        dproj = reduce_group(GROUPS[1], l, big_grads["w_in"][l], dproj)
        dh2 = _mm_nt_chunks(dproj, lw["win"])
        dh2 = move_on(dh2, dh2)
        dx, dq, acc2 = _norm_bwd(s["x1"], dh2, dx, s["f2"], mix_norm[l][None], md[4], 1.0, md[2], 0.5)
        dq = move_on(dx, dq)
        if l > 0:
            ng, ns = saved[l - 1]["md"][8], 0.5
        else:
            ng, ns = zero_row, 0.0
        dx, dq, acc1 = ffn_bwd(
            GROUPS[0], l, dx, dq, s["x0"], s["h1"], s["a1"], s["gu1"], s["f1"], lw["gu1"], lw["d1"], ffn1_norm[l][None], md[1], ng, ns)

        dmods[l] = jnp.concatenate([acc1[0:2], acc1[3:4], acc2[0:2], acc2[3:4], acc3[0:2], acc3[3:4]], axis=0)
        dws = jnp.stack([dwsp[:, :, :CHUNK], dwsp[:, :, CHUNK:]], axis=1).reshape(HEADS, CHUNK, CHUNK) * tril
        lg = {"ffn1_norm": acc1[2], "mix_norm": acc2[2], "ffn2_norm": acc3[2],
              "conv_w": lvec[4:8], "conv_b": lvec[3], "gate_a_w": _unpair_blocks(dwa), "gate_a_b": lvec[0].reshape(HEADS, hd),
              "gate_x_w": _unpair_blocks(dwx), "gate_x_b": lvec[1].reshape(HEADS, hd), "lru_lambda": lvec[2], "v_norm": gacc[2],
              "spatial_w": dws, "spatial_b": dbfull.reshape(CHUNK, HEADS, hd).sum(-1).T, "lru_out_norm": gacc[0], "gmlp_out_norm": gacc[1]}
        for n, g in lg.items():
            small_grads.setdefault(n, [None] * L)[l] = g

    while reducing:
        dq = move_on(dq, dq, force=True)
    grad_x = dx.reshape(x.shape)

    per_layer = [n for n in SMALL if n not in ("b_ada", "final_norm")]
    part = [jnp.stack(small_grads[n]) for n in per_layer] + [head_acc[0], jnp.stack(dmods)]
    packer = _Packer([p.shape for p in part])
    gathered = _all_gather8(packer.pack(part))
    summed = packer.unpack(_sum_leading(gathered))
    grads = dict(zip(per_layer + ["final_norm"], summed[:-1]))
    grads["b_ada"] = summed[-1].reshape(L, N_MOD * D)
    off = sum(packer.sizes[:-1])
    dmod_rows = gathered.reshape(N_DEV, -1)[:, off:off + L * N_MOD * D].reshape(N_DEV, L, N_MOD * D)
    dmod_shard = lax.dynamic_slice_in_dim(dmod_rows, chip * n_ada, n_ada, axis=2).transpose(1, 0, 2)
    grads["w_ada"] = _ada_grad(c_all.T, dmod_shard)
    grads["conv_w"] = lax.dynamic_slice_in_dim(grads["conv_w"], chip * cws, cws, axis=2)

    delta, new_m, new_v = {}, {}, {}
    for n in BIG:
        grads[n], delta[n], new_m[n], new_v[n] = stepped[n]
    shp = w_ada.shape
    d_, m_, v_ = _adamw(*[a.reshape(-1, shp[-1]) for a in (w_ada, grads["w_ada"], m_w_ada, v_w_ada)])
    delta["w_ada"], new_m["w_ada"], new_v["w_ada"] = d_.reshape(shp), m_.reshape(shp), v_.reshape(shp)
    spk = _Packer([W[n].shape for n in SMALL])
    d_, m_, v_ = _adamw(spk.pack([W[n] for n in SMALL]), spk.pack([grads[n] for n in SMALL]),
                        spk.pack([given["m_" + n] for n in SMALL]), spk.pack([given["v_" + n] for n in SMALL]))
    for n, a, b, e in zip(SMALL, spk.unpack(d_), spk.unpack(m_), spk.unpack(v_)):
        delta[n], new_m[n], new_v[n] = a, b, e
    grads = {n: grads[n].reshape(W[n].shape) for n in WEIGHTS}
    return (loss, grad_x, *[grads[n] for n in WEIGHTS], *[delta[n] for n in WEIGHTS], *[new_m[n] for n in WEIGHTS], *[new_v[n] for n in WEIGHTS])
```

```python
import math

import jax
import jax.numpy as jnp
from jax import lax
from jax.experimental import pallas as pl
from jax.experimental.pallas import tpu as pltpu

F32 = jnp.float32
MXU_DTYPE = jnp.bfloat16
ACT_DTYPE = jnp.bfloat16
XFER_DTYPE = jnp.bfloat16
EPS = 1e-6
RG_LRU_C = 8.0
N_MOD = 9
CONV_WIDTH = 4
HEADS = 8
CHUNK = 128
LANES = 128
SUBLANES = 8
N_CHIPS = 4
N_DEV = 8
ADAM_LR, ADAM_B1, ADAM_B2, ADAM_EPS, ADAM_WD, ADAM_STEP = 0.001, 0.9, 0.999, 1e-08, 0.01, 10
VMEM_LIMIT_BYTES = 60 * 1024 * 1024
ROW_TILE_BYTES = 1 << 20
GELU_C = math.sqrt(2.0 / math.pi)
GELU_A = 0.044715

ANY = pl.BlockSpec(memory_space=pl.ANY)
MESH = pl.DeviceIdType.MESH
SDS = jax.ShapeDtypeStruct


def _params(*sem):
    return pltpu.CompilerParams(dimension_semantics=sem, vmem_limit_bytes=VMEM_LIMIT_BYTES)


def _dot(a, b):
    return jnp.dot(a.astype(MXU_DTYPE), b.astype(MXU_DTYPE), preferred_element_type=F32)


def _dot_nt(a, b):
    return lax.dot_general(a.astype(MXU_DTYPE), b.astype(MXU_DTYPE), (((1,), (1,)), ((), ())), preferred_element_type=F32)


def _dot_tn(a, b):
    return lax.dot_general(a.astype(MXU_DTYPE), b.astype(MXU_DTYPE), (((0,), (0,)), ((), ())), preferred_element_type=F32)


def _gelu(x):
    return x * (0.5 * (1.0 + jnp.tanh(GELU_C * (x + GELU_A * (x * x * x)))))


def _gelu_grad(x):
    t = jnp.tanh(GELU_C * (x + GELU_A * (x * x * x)))
    return 0.5 * (1.0 + t) + 0.5 * x * (1.0 - t * t) * (GELU_C * (1.0 + 3.0 * GELU_A * x * x))


def _sigmoid(x):
    return jax.nn.sigmoid(x)


def _rsqrt_ms(x):
    return lax.rsqrt(jnp.mean(x * x, axis=-1, keepdims=True) + EPS)


def _rowsum(x):
    return jnp.sum(x, axis=0, keepdims=True)


def _tile(n, want):
    t = min(n, want)
    assert n % t == 0, (n, want)
    return t


def _row_tile(rows, row_bytes):
    step = 2 * SUBLANES
    cap = max(step, ROW_TILE_BYTES // row_bytes)
    best = None
    for t in range(step, min(rows, cap) + 1, step):
        if rows % t == 0:
            best = t
    assert best is not None, (rows, row_bytes)
    return best


def _modnorm(x, gn, sh, sc):
    S, D = x.shape
    tm = _tile(S, 512)

    def body(x_ref, gn_ref, sh_ref, sc_ref, h_ref):
        xv = x_ref[...]
        h = (xv * _rsqrt_ms(xv) * gn_ref[...]) * (1.0 + sc_ref[...]) + sh_ref[...]
        h_ref[...] = h.astype(ACT_DTYPE)

    row = pl.BlockSpec((1, D), lambda i: (0, 0))
    return pl.pallas_call(
        body, name="modnorm", grid=(S // tm,),
        in_specs=[pl.BlockSpec((tm, D), lambda i: (i, 0)), row, row, row],
        out_specs=pl.BlockSpec((tm, D), lambda i: (i, 0)),
        out_shape=SDS((S, D), ACT_DTYPE), compiler_params=_params("parallel"),
    )(x, gn, sh, sc)


def _norm_bwd(x, dh, dxo, f, gn, sc, res_scale, next_gate, next_scale):
    S, D = x.shape
    tm = _tile(S, 256)

    def body(x_ref, dh_ref, dxo_ref, f_ref, gn_ref, sc_ref, ng_ref, dx_ref, dq_ref, acc_ref):
        @pl.when(pl.program_id(0) == 0)
        def _():
            acc_ref[...] = jnp.zeros_like(acc_ref)

        xv, dh, dxo = x_ref[...], dh_ref[...], dxo_ref[...]
        r = _rsqrt_ms(xv)
        xhat = xv * r
        gn = gn_ref[...]
        dn = dh * (1.0 + sc_ref[...])
        dxh = dn * gn
        dx = dxo + r * (dxh - xhat * jnp.mean(dxh * xhat, axis=-1, keepdims=True))
        dx_ref[...] = dx
        dq_ref[...] = ((next_scale * ng_ref[...]) * dx).astype(ACT_DTYPE)
        acc_ref[0:1, :] += _rowsum(dh)
        acc_ref[1:2, :] += _rowsum(dh * (xhat * gn))
        acc_ref[2:3, :] += _rowsum(dn * xhat)
        acc_ref[3:4, :] += _rowsum((res_scale * f_ref[...]) * dxo)

    tile = pl.BlockSpec((tm, D), lambda i: (i, 0))
    row = pl.BlockSpec((1, D), lambda i: (0, 0))
    return pl.pallas_call(
        body, name="norm_bwd", grid=(S // tm,),
        in_specs=[tile, tile, tile, tile, row, row, row],
        out_specs=[tile, tile, pl.BlockSpec((SUBLANES, D), lambda i: (0, 0))],
        out_shape=[SDS((S, D), F32), SDS((S, D), ACT_DTYPE), SDS((SUBLANES, D), F32)],
        compiler_params=_params("arbitrary"),
    )(x, dh, dxo, f, gn, sc, next_gate)


def _loss_head(x, target, gn, next_gate, next_scale):
    S, D = x.shape
    tm = _tile(S, 256)

    def body(x_ref, t_ref, gn_ref, ng_ref, dx_ref, dq_ref, acc_ref):
        @pl.when(pl.program_id(0) == 0)
        def _():
            acc_ref[...] = jnp.zeros_like(acc_ref)

        xv = x_ref[...]
        r = _rsqrt_ms(xv)
        xhat = xv * r
        gn = gn_ref[...]
        err = xhat * gn - t_ref[...]
        dy = err * (1.0 / D)
        dxh = dy * gn
        dx = r * (dxh - xhat * jnp.mean(dxh * xhat, axis=-1, keepdims=True))
        dx_ref[...] = dx
        dq_ref[...] = ((next_scale * ng_ref[...]) * dx).astype(ACT_DTYPE)
        acc_ref[0:1, :] += _rowsum(dy * xhat)
        acc_ref[1:2, :] += _rowsum(err * err) * (0.5 / D)

    tile = pl.BlockSpec((tm, D), lambda i: (i, 0))
    row = pl.BlockSpec((1, D), lambda i: (0, 0))
    return pl.pallas_call(
        body, name="loss_head", grid=(S // tm,),
        in_specs=[tile, tile, row, row],
        out_specs=[tile, tile, pl.BlockSpec((SUBLANES, D), lambda i: (0, 0))],
        out_shape=[SDS((S, D), F32), SDS((S, D), ACT_DTYPE), SDS((SUBLANES, D), F32)],
        compiler_params=_params("arbitrary"),
    )(x, target, gn, next_gate)


def _ffn_up(h, wgu):
    S, D = h.shape
    C = wgu.shape[2]
    tm = _tile(S, 512)

    def body(h_ref, wg_ref, wu_ref, a_ref, gu_ref):
        hv = h_ref[...]
        g = _dot(hv, wg_ref[...])
        u = _dot(hv, wu_ref[...])
        a_ref[...] = (g * _sigmoid(g) * u).astype(ACT_DTYPE)
        gu_ref[0] = g.astype(ACT_DTYPE)
        gu_ref[1] = u.astype(ACT_DTYPE)

    return pl.pallas_call(
        body, name="ffn_up", grid=(2, S // tm),
        in_specs=[
            pl.BlockSpec((tm, D), lambda j, i: (i, 0)),
            pl.BlockSpec((None, D, C), lambda j, i: (j, 0, 0)),
            pl.BlockSpec((None, D, C), lambda j, i: (2 + j, 0, 0)),
        ],
        out_specs=[
            pl.BlockSpec((tm, C), lambda j, i: (i, j)),
            pl.BlockSpec((2, None, tm, C), lambda j, i: (0, j, i, 0)),
        ],
        out_shape=[SDS((S, 2 * C), ACT_DTYPE), SDS((2, 2, S, C), ACT_DTYPE)],
        compiler_params=_params("parallel", "parallel"),
    )(h, wgu, wgu)


def _ffn_bwd_act(dq, wd, gu):
    S, D = dq.shape
    C = gu.shape[3]
    tm = _tile(S, 512)

    def body(dq_ref, wd_ref, gu_ref, dgu_ref):
        da = _dot_nt(dq_ref[...], wd_ref[...])
        g = gu_ref[0].astype(F32)
        u = gu_ref[1].astype(F32)
        s = _sigmoid(g)
        dgu_ref[0] = (da * u * (s * (1.0 + g * (1.0 - s)))).astype(ACT_DTYPE)
        dgu_ref[1] = (da * (g * s)).astype(ACT_DTYPE)

    gu_spec = pl.BlockSpec((2, None, tm, C), lambda j, i: (0, j, i, 0))
    return pl.pallas_call(
        body, name="ffn_bwd_act", grid=(2, S // tm),
        in_specs=[pl.BlockSpec((tm, D), lambda j, i: (i, 0)), pl.BlockSpec((C, D), lambda j, i: (j, 0)), gu_spec],
        out_specs=gu_spec,
        out_shape=SDS(gu.shape, ACT_DTYPE),
        compiler_params=_params("parallel", "parallel"),
    )(dq, wd, gu)


def _mm_res(a, w, x, gate, scale):
    S, K = a.shape
    D = w.shape[1]
    tm, tn = _tile(S, 1024), _tile(D, 512)

    def body(a_ref, w_ref, x_ref, g_ref, f_ref, xo_ref):
        f = _dot(a_ref[...], w_ref[...])
        f_ref[...] = f
        xo_ref[...] = x_ref[...] + (scale * g_ref[...]) * f

    tile = pl.BlockSpec((tm, tn), lambda j, i: (i, j))
    return pl.pallas_call(
        body, name=f"mm_res_k{K}", grid=(D // tn, S // tm),
        in_specs=[pl.BlockSpec((tm, K), lambda j, i: (i, 0)), pl.BlockSpec((K, tn), lambda j, i: (0, j)), tile,
                  pl.BlockSpec((1, tn), lambda j, i: (0, j))],
        out_specs=[tile, tile],
        out_shape=[SDS((S, D), F32), SDS((S, D), F32)],
        compiler_params=_params("parallel", "parallel"),
    )(a, w, x, gate)


def _mm_chunks(h, wc):
    S, K = h.shape
    P, _, N = wc.shape
    tm = _tile(S, 1024)

    def body(h_ref, w_ref, o_ref):
        o_ref[...] = _dot(h_ref[...], w_ref[...])

    return pl.pallas_call(
        body, name="mm_chunks", grid=(P, S // tm),
        in_specs=[pl.BlockSpec((tm, K), lambda j, i: (i, 0)), pl.BlockSpec((None, K, N), lambda j, i: (j, 0, 0))],
        out_specs=pl.BlockSpec((tm, N), lambda j, i: (i, j)),
        out_shape=SDS((S, P * N), F32),
        compiler_params=_params("parallel", "parallel"),
    )(h, wc)


def _mm_nt_chunks(ac, wc):
    P, S, K = ac.shape
    N = wc.shape[1]
    tm, tn = _tile(S, 1024), _tile(N, 512)

    def body(a_ref, w_ref, o_ref):
        acc = _dot_nt(a_ref[0], w_ref[0])
        for p in range(1, P):
            acc += _dot_nt(a_ref[p], w_ref[p])
        o_ref[...] = acc

    return pl.pallas_call(
        body, name=f"mm_nt_p{P}k{K}", grid=(S // tm, N // tn),
        in_specs=[pl.BlockSpec((P, tm, K), lambda i, j: (0, i, 0)), pl.BlockSpec((P, tn, K), lambda i, j: (0, j, 0))],
        out_specs=pl.BlockSpec((tm, tn), lambda i, j: (i, j)),
        out_shape=SDS((S, N), F32),
        compiler_params=_params("parallel", "parallel"),
    )(ac, wc)


def _mm_tn_chunks(a, bc, tile_m, tile_n):
    S, M = a.shape
    P, _, N = bc.shape
    ts, tm, tn = _tile(S, 512), _tile(M, tile_m), _tile(N, tile_n)

    def body(a_ref, b_ref, o_ref):
        @pl.when(pl.program_id(3) == 0)
        def _():
            o_ref[...] = jnp.zeros_like(o_ref)

        o_ref[...] += _dot_tn(a_ref[...], b_ref[...])

    return pl.pallas_call(
        body, name=f"mm_tn_m{M}n{N}", grid=(P, M // tm, N // tn, S // ts),
        in_specs=[pl.BlockSpec((ts, tm), lambda p, m, n, k: (k, m)), pl.BlockSpec((None, ts, tn), lambda p, m, n, k: (p, k, n))],
        out_specs=pl.BlockSpec((None, tm, tn), lambda p, m, n, k: (p, m, n)),
        out_shape=SDS((P, M, N), F32),
        compiler_params=_params("parallel", "parallel", "parallel", "arbitrary"),
    )(a, bc)


def _shift_down(x, s, row, fill):
    return jnp.where(row >= s, pltpu.roll(x, s, 0), fill)


def _shift_up(x, s, row, fill):
    n = x.shape[0]
    return jnp.where(row < n - s, pltpu.roll(x, n - s, 0), fill)


def _scan_down(a, b, row):
    s = 1
    while s < a.shape[0]:
        b = a * _shift_down(b, s, row, 0.0) + b
        a = a * _shift_down(a, s, row, 1.0)
        s *= 2
    return b


def _scan_up(a, b, row):
    s = 1
    while s < a.shape[0]:
        b = a * _shift_up(b, s, row, 0.0) + b
        a = a * _shift_up(a, s, row, 1.0)
        s *= 2
    return b


def _conv(xl, cw_ref, cb_ref, row):
    y = cb_ref[...] + _shift_down(xl, 3, row, 0.0) * cw_ref[0:1, :]
    y = y + _shift_down(xl, 2, row, 0.0) * cw_ref[1:2, :]
    y = y + _shift_down(xl, 1, row, 0.0) * cw_ref[2:3, :]
    return y + xl * cw_ref[3:4, :]


def _lru_gates(xc, wa_ref, ba_ref, wx_ref, bx_ref, lam_ref):
    ra = _sigmoid(_dot(xc, wa_ref[...]) + ba_ref[...])
    ri = _sigmoid(_dot(xc, wx_ref[...]) + bx_ref[...])
    ls = jax.nn.log_sigmoid(lam_ref[...])
    a = jnp.exp((RG_LRU_C * ra) * ls)
    mult = jnp.sqrt(1.0 - a * a)
    return ra, ri, ls, a, mult


def _lru_specs(S):
    col = lambda off: pl.BlockSpec((S, LANES), lambda j: (0, off + j))
    vec = pl.BlockSpec((1, LANES), lambda j: (0, j))
    blk = pl.BlockSpec((None, LANES, LANES), lambda j: (j, 0, 0))
    cw = pl.BlockSpec((CONV_WIDTH, LANES), lambda j: (0, j))
    return col, vec, blk, cw


def _lru_fwd(proj, cw, cb, wa, ba, wx, bx, lam):
    S = proj.shape[0]
    W = cb.shape[1]
    nb = W // LANES

    def body(xl_ref, gl_ref, cw_ref, cb_ref, wa_ref, ba_ref, wx_ref, bx_ref, lam_ref, y_ref):
        row = lax.broadcasted_iota(jnp.int32, (S, LANES), 0)
        xc = _conv(xl_ref[...], cw_ref, cb_ref, row)
        _, ri, _, a, mult = _lru_gates(xc, wa_ref, ba_ref, wx_ref, bx_ref, lam_ref)
        h = _scan_down(a, mult * (ri * xc), row)
        y_ref[...] = h * _gelu(gl_ref[...])

    col, vec, blk, cws = _lru_specs(S)
    return pl.pallas_call(
        body, name="lru_fwd", grid=(nb,),
        in_specs=[col(0), col(nb), cws, vec, blk, vec, blk, vec, vec],
        out_specs=pl.BlockSpec((S, LANES), lambda j: (0, j)),
        out_shape=SDS((S, W), F32), compiler_params=_params("parallel"),
    )(proj, proj, cw, cb, wa, ba, wx, bx, lam)


def _lru_bwd(proj, dy, cw, cb, wa, ba, wx, bx, lam, wat, wxt):
    S = proj.shape[0]
    W = cb.shape[1]
    nb = W // LANES

    def body(xl_ref, gl_ref, dy_ref, cw_ref, cb_ref, wa_ref, ba_ref, wx_ref, bx_ref, lam_ref, wat_ref, wxt_ref,
             dp_ref, dwa_ref, dwx_ref, vec_ref):
        row = lax.broadcasted_iota(jnp.int32, (S, LANES), 0)
        xl = xl_ref[...]
        xc = _conv(xl, cw_ref, cb_ref, row)
        ra, ri, ls, a, mult = _lru_gates(xc, wa_ref, ba_ref, wx_ref, bx_ref, lam_ref)
        h = _scan_down(a, mult * (ri * xc), row)
        gl = gl_ref[...]
        dyv = dy_ref[...]
        dp_ref[1] = (dyv * h * _gelu_grad(gl)).astype(ACT_DTYPE)
        adj = _scan_up(_shift_up(a, 1, row, 0.0), dyv * _gelu(gl), row)
        da = adj * _shift_down(h, 1, row, 0.0)
        dmult = adj * (ri * xc)
        dlog_a = da * a - dmult * (a * a) / mult
        dra = dlog_a * (RG_LRU_C * ls)
        dpa = dra * ra * (1.0 - ra)
        dpi = (adj * mult * xc) * ri * (1.0 - ri)
        dxc = adj * mult * ri + _dot(dpa, wat_ref[...]) + _dot(dpi, wxt_ref[...])
        dwa_ref[...] = _dot_tn(xc, dpa)
        dwx_ref[...] = _dot_tn(xc, dpi)
        dxl = dxc * cw_ref[3:4, :]
        dxl = dxl + _shift_up(dxc, 1, row, 0.0) * cw_ref[2:3, :]
        dxl = dxl + _shift_up(dxc, 2, row, 0.0) * cw_ref[1:2, :]
        dxl = dxl + _shift_up(dxc, 3, row, 0.0) * cw_ref[0:1, :]
        dp_ref[0] = dxl.astype(ACT_DTYPE)
        vec_ref[...] = jnp.zeros_like(vec_ref)
        vec_ref[0:1, :] = _rowsum(dpa)
        vec_ref[1:2, :] = _rowsum(dpi)
        vec_ref[2:3, :] = _rowsum(dlog_a * (RG_LRU_C * ra)) * _sigmoid(-lam_ref[...])
        vec_ref[3:4, :] = _rowsum(dxc)
        vec_ref[4:5, :] = _rowsum(dxc * _shift_down(xl, 3, row, 0.0))
        vec_ref[5:6, :] = _rowsum(dxc * _shift_down(xl, 2, row, 0.0))
        vec_ref[6:7, :] = _rowsum(dxc * _shift_down(xl, 1, row, 0.0))
        vec_ref[7:8, :] = _rowsum(dxc * xl)

    col, vec, blk, cws = _lru_specs(S)
    return pl.pallas_call(
        body, name="lru_bwd", grid=(nb,),
        in_specs=[col(0), col(nb), col(0), cws, vec, blk, vec, blk, vec, vec, blk, blk],
        out_specs=[pl.BlockSpec((2, S, LANES), lambda j: (0, 0, j)), blk, blk, pl.BlockSpec((2 * SUBLANES, LANES), lambda j: (0, j))],
        out_shape=[SDS((2, S, W), ACT_DTYPE), SDS((nb, LANES, LANES), F32), SDS((nb, LANES, LANES), F32), SDS((2 * SUBLANES, W), F32)],
        compiler_params=_params("parallel"),
    )(proj, proj, dy, cw, cb, wa, ba, wx, bx, lam, wat, wxt)


def _seg_mean(x, seg_ref, width):
    hi = x.astype(jnp.bfloat16)
    lo = (x - hi.astype(F32)).astype(jnp.bfloat16)
    ones = seg_ref[...]
    s = jnp.dot(hi, ones, preferred_element_type=F32) + jnp.dot(lo, ones, preferred_element_type=F32)
    return s * (1.0 / width)


def _gmlp_core(u_ref, v_ref, gv_ref, seg_ref, ws_ref, bfull_ref, z_scr, hd):
    tm, W = u_ref.shape
    lane = lax.broadcasted_iota(jnp.int32, (CHUNK, LANES), 1)
    ug = _gelu(u_ref[...])
    vg = _gelu(v_ref[...])
    cen = vg - _seg_mean(vg, seg_ref, hd)
    rstd = lax.rsqrt(_seg_mean(cen * cen, seg_ref, hd) + EPS)
    vhat = cen * rstd
    vh = vhat * gv_ref[...]
    vcats = {}
    for ci in range(tm // CHUNK):
        for p in range(W // LANES):
            blk = vh[ci * CHUNK:(ci + 1) * CHUNK, p * LANES:(p + 1) * LANES]
            vcat = jnp.concatenate([jnp.where(lane < hd, blk, 0.0), jnp.where(lane >= hd, blk, 0.0)], axis=0).astype(MXU_DTYPE)
            vcats[ci, p] = vcat
            z_scr[ci * CHUNK:(ci + 1) * CHUNK, p * LANES:(p + 1) * LANES] = (
                jnp.dot(ws_ref[p], vcat, preferred_element_type=F32) + bfull_ref[:, p * LANES:(p + 1) * LANES])
    return ug, vhat, rstd, vcats


def _gmlp_specs(tm, W, nb):
    rows = lambda off: pl.BlockSpec((tm, W), lambda i: (i, off))
    vec = pl.BlockSpec((1, W), lambda i: (0, 0))
    seg = pl.BlockSpec((W, W), lambda i: (0, 0))
    wsp = pl.BlockSpec((nb, CHUNK, 2 * CHUNK), lambda i: (0, 0, 0))
    bfull = pl.BlockSpec((CHUNK, W), lambda i: (0, 0))
    return rows, vec, seg, wsp, bfull


def _gmlp_fwd(proj, ylru, gv, seg, wsp, bfull, g_lru, g_gm):
    S, W = ylru.shape
    nb = W // LANES
    hd = W // HEADS
    tm = _tile(S, 512)

    def body(u_ref, v_ref, yl_ref, gv_ref, seg_ref, ws_ref, bfull_ref, gl_ref, gg_ref, yn_ref, ygm_ref, z_scr):
        ug, _, _, _ = _gmlp_core(u_ref, v_ref, gv_ref, seg_ref, ws_ref, bfull_ref, z_scr, hd)
        ygm = ug * z_scr[...]
        ygm_ref[...] = ygm
        yl = yl_ref[...]
        yn_ref[:, 0:W] = (yl * _rsqrt_ms(yl) * gl_ref[...]).astype(ACT_DTYPE)
        yn_ref[:, W:2 * W] = (ygm * _rsqrt_ms(ygm) * gg_ref[...]).astype(ACT_DTYPE)

    rows, vec, segs, wsps, bfulls = _gmlp_specs(tm, W, nb)
    return pl.pallas_call(
        body, name="gmlp_fwd", grid=(S // tm,),
        in_specs=[rows(2), rows(3), rows(0), vec, segs, wsps, bfulls, vec, vec],
        out_specs=[pl.BlockSpec((tm, 2 * W), lambda i: (i, 0)), rows(0)],
        out_shape=[SDS((S, 2 * W), ACT_DTYPE), SDS((S, W), F32)],
        scratch_shapes=[pltpu.VMEM((tm, W), F32)],
        compiler_params=_params("parallel"),
    )(proj, proj, ylru, gv, seg, wsp, bfull, g_lru, g_gm)


def _rms_bwd(y, g, dyn):
    r = _rsqrt_ms(y)
    yhat = y * r
    dyh = dyn * g
    return r * (dyh - yhat * jnp.mean(dyh * yhat, axis=-1, keepdims=True)), _rowsum(dyn * yhat)


def _gmlp_bwd(proj, ylru, ygm, dyn, gv, seg, wsp, wspt, bfull, g_lru, g_gm):
    S, W = ylru.shape
    nb = W // LANES
    hd = W // HEADS
    tm = _tile(S, 256)

    def body(u_ref, v_ref, yl_ref, ygm_ref, dl_ref, dg_ref, gv_ref, seg_ref, ws_ref, wst_ref, bfull_ref, gl_ref, gg_ref,
             dyl_ref, duv_ref, dws_ref, dbf_ref, acc_ref, z_scr, dvh_scr):
        @pl.when(pl.program_id(0) == 0)
        def _():
            dws_ref[...] = jnp.zeros_like(dws_ref)
            dbf_ref[...] = jnp.zeros_like(dbf_ref)
            acc_ref[...] = jnp.zeros_like(acc_ref)

        dyl, dgl = _rms_bwd(yl_ref[...], gl_ref[...], dl_ref[...])
        dyl_ref[...] = dyl
        dygm, dgg = _rms_bwd(ygm_ref[...], gg_ref[...], dg_ref[...])
        ug, vhat, rstd, vcats = _gmlp_core(u_ref, v_ref, gv_ref, seg_ref, ws_ref, bfull_ref, z_scr, hd)
        duv_ref[0] = (dygm * z_scr[...] * _gelu_grad(u_ref[...])).astype(ACT_DTYPE)
        dz = dygm * ug
        lane = lax.broadcasted_iota(jnp.int32, (CHUNK, LANES), 1)
        dbf = dz[0:CHUNK, :]
        for ci in range(1, tm // CHUNK):
            dbf += dz[ci * CHUNK:(ci + 1) * CHUNK, :]
        dbf_ref[...] += dbf
        for ci in range(tm // CHUNK):
            for p in range(nb):
                dzb = dz[ci * CHUNK:(ci + 1) * CHUNK, p * LANES:(p + 1) * LANES].astype(MXU_DTYPE)
                dws_ref[p] += _dot_nt(dzb, vcats[ci, p])
                dvc = jnp.dot(wst_ref[p], dzb, preferred_element_type=F32)
                dvh_scr[ci * CHUNK:(ci + 1) * CHUNK, p * LANES:(p + 1) * LANES] = jnp.where(lane < hd, dvc[0:CHUNK], dvc[CHUNK:2 * CHUNK])
        dvh = dvh_scr[...]
        dvn = dvh * gv_ref[...]
        dvg = rstd * (dvn - _seg_mean(dvn, seg_ref, hd) - vhat * _seg_mean(dvn * vhat, seg_ref, hd))
        duv_ref[1] = (dvg * _gelu_grad(v_ref[...])).astype(ACT_DTYPE)
        acc_ref[0:1, :] += dgl
        acc_ref[1:2, :] += dgg
        acc_ref[2:3, :] += _rowsum(dvh * vhat)

    rows, vec, segs, wsps, bfulls = _gmlp_specs(tm, W, nb)
    wspt_spec = pl.BlockSpec((nb, 2 * CHUNK, CHUNK), lambda i: (0, 0, 0))
    return pl.pallas_call(
        body, name="gmlp_bwd", grid=(S // tm,),
        in_specs=[rows(2), rows(3), rows(0), rows(0), rows(0), rows(1), vec, segs, wsps, wspt_spec, bfulls, vec, vec],
        out_specs=[rows(0), pl.BlockSpec((2, tm, W), lambda i: (0, i, 0)), wsps, bfulls, pl.BlockSpec((SUBLANES, W), lambda i: (0, 0))],
        out_shape=[SDS((S, W), F32), SDS((2, S, W), ACT_DTYPE), SDS((nb, CHUNK, 2 * CHUNK), F32), SDS((CHUNK, W), F32), SDS((SUBLANES, W), F32)],
        scratch_shapes=[pltpu.VMEM((tm, W), F32), pltpu.VMEM((tm, W), F32)],
        compiler_params=_params("arbitrary"),
    )(proj, proj, ylru, ygm, dyn, dyn, gv, seg, wsp, wspt, bfull, g_lru, g_gm)


def _ada_fwd(c_all, w_ada, b_shard):
    L, D, N = w_ada.shape
    R = c_all.shape[0]
    tn = N // 2

    def body(c_ref, w_ref, b_ref, o_ref):
        cv = c_ref[...]
        o_ref[...] = _dot(cv * _sigmoid(cv), w_ref[...]) + b_ref[...]

    return pl.pallas_call(
        body, name="ada_fwd", grid=(L, N // tn),
        in_specs=[pl.BlockSpec((R, D), lambda l, j: (0, 0)), pl.BlockSpec((None, D, tn), lambda l, j: (l, 0, j)),
                  pl.BlockSpec((None, 1, tn), lambda l, j: (l, 0, j))],
        out_specs=pl.BlockSpec((None, R, tn), lambda l, j: (l, 0, j)),
        out_shape=SDS((L, R, N), F32), compiler_params=_params("parallel", "parallel"),
    )(c_all, w_ada, b_shard)


def _ada_grad(c_all_t, dmod):
    D, B = c_all_t.shape
    L, _, N = dmod.shape
    tn = N // 2

    def body(c_ref, d_ref, o_ref):
        cv = c_ref[...]
        sc = cv * _sigmoid(cv)
        acc = sc[:, 0:1] * d_ref[0:1, :]
        for b in range(1, B):
            acc += sc[:, b:b + 1] * d_ref[b:b + 1, :]
        o_ref[...] = acc

    return pl.pallas_call(
        body, name="ada_grad", grid=(L, N // tn),
        in_specs=[pl.BlockSpec((D, B), lambda l, j: (0, 0)), pl.BlockSpec((None, B, tn), lambda l, j: (l, 0, j))],
        out_specs=pl.BlockSpec((None, D, tn), lambda l, j: (l, 0, j)),
        out_shape=SDS((L, D, N), F32), compiler_params=_params("parallel", "parallel"),
    )(c_all_t, dmod)


def _adamw(w, g, m, v):
    R, C = w.shape
    tr = _row_tile(R, C * 4)

    def body(w_ref, g_ref, m_ref, v_ref, d_ref, mo_ref, vo_ref):
        d_ref[...], mo_ref[...], vo_ref[...] = _adam_math(w_ref[...], g_ref[...], m_ref[...], v_ref[...])

    tile = pl.BlockSpec((tr, C), lambda i: (i, 0))
    return pl.pallas_call(
        body, name=f"adamw_r{R}c{C}", grid=(R // tr,), in_specs=[tile] * 4, out_specs=[tile] * 3,
        out_shape=[SDS((R, C), F32)] * 3, compiler_params=_params("parallel"),
    )(w, g, m, v)


def _adam_math(w, g, m, v):
    mn = ADAM_B1 * m + (1.0 - ADAM_B1) * g
    vn = ADAM_B2 * v + (1.0 - ADAM_B2) * (g * g)
    m_hat = mn / (1.0 - ADAM_B1 ** ADAM_STEP)
    v_hat = vn / (1.0 - ADAM_B2 ** ADAM_STEP)
    return -ADAM_LR * (m_hat / (jnp.sqrt(v_hat) + ADAM_EPS) + ADAM_WD * w), mn, vn


def _adamw_layer(w, g, m, v, l, prev, after):
    L, R, C = w.shape
    tr = _row_tile(R, C * 4)
    prev = (after,) + (() if prev is None else tuple(prev))

    def body(w_ref, g_ref, m_ref, v_ref, *rest):
        go_ref, d_ref, mo_ref, vo_ref = rest[len(prev):]
        gv = g_ref[...]
        go_ref[...] = gv
        d_ref[...], mo_ref[...], vo_ref[...] = _adam_math(w_ref[...], gv, m_ref[...], v_ref[...])

    lay = pl.BlockSpec((None, tr, C), lambda i: (l, i, 0))
    return pl.pallas_call(
        body, name=f"adamw_layer_r{R}c{C}", grid=(R // tr,),
        in_specs=[lay, pl.BlockSpec((tr, C), lambda i: (i, 0)), lay, lay] + [ANY] * len(prev), out_specs=[lay] * 4,
        out_shape=[SDS((L, R, C), F32)] * 4, input_output_aliases={5 + k: k for k in range(len(prev) - 1)},
        compiler_params=_params("parallel"),
    )(w, g, m, v, *prev)


def _sum_leading(a):
    P, R, C = a.shape
    tr = _row_tile(R, P * C * 4)

    def body(a_ref, o_ref):
        acc = a_ref[0]
        for p in range(1, P):
            acc = acc + a_ref[p]
        o_ref[...] = acc

    return pl.pallas_call(
        body, name=f"sum{P}_r{R}c{C}", grid=(R // tr,),
        in_specs=[pl.BlockSpec((P, tr, C), lambda i: (0, i, 0))],
        out_specs=pl.BlockSpec((tr, C), lambda i: (i, 0)),
        out_shape=SDS((R, C), F32), compiler_params=_params("parallel"),
    )(a)


def _add_half(g4, r1, place):
    _, _, R, C = g4.shape
    tr = _row_tile(R, C * 4)

    def body(place_ref, g_ref, r_ref, h_ref, own_ref):
        s = (g_ref[...] + r_ref[...]).astype(XFER_DTYPE)
        h_ref[...] = s

        @pl.when(pl.program_id(1) == place_ref[1])
        def _():
            own_ref[...] = s

    return pl.pallas_call(
        body, name=f"add_half_r{R}c{C}",
        grid_spec=pltpu.PrefetchScalarGridSpec(
            num_scalar_prefetch=1, grid=(R // tr, N_CHIPS),
            in_specs=[pl.BlockSpec((None, None, tr, C), lambda i, p, place_ref: (p, place_ref[0], i, 0)),
                      pl.BlockSpec((None, tr, C), lambda i, p, place_ref: (p, i, 0))],
            out_specs=[pl.BlockSpec((None, tr, C), lambda i, p, place_ref: (p, i, 0)),
                       pl.BlockSpec((None, tr, C), lambda i, p, place_ref: (place_ref[1], i, 0))],
        ),
        out_shape=[SDS((N_CHIPS, R, C), XFER_DTYPE)] * 2, compiler_params=_params("parallel", "arbitrary"),
    )(place, g4, r1)


def _sum4_into_half(r2, place):
    P, R, C = r2.shape
    tr = _row_tile(R, P * C * 4)

    def body(place_ref, a_ref, o_ref):
        acc = a_ref[0].astype(F32)
        for p in range(1, P):
            acc = acc + a_ref[p].astype(F32)
        o_ref[...] = acc

    return pl.pallas_call(
        body, name=f"sum4_r{R}c{C}",
        grid_spec=pltpu.PrefetchScalarGridSpec(
            num_scalar_prefetch=1, grid=(R // tr,),
            in_specs=[pl.BlockSpec((P, tr, C), lambda i, place_ref: (0, i, 0))],
            out_specs=pl.BlockSpec((None, tr, C), lambda i, place_ref: (place_ref[0], i, 0)),
        ),
        out_shape=SDS((2, R, C), F32), compiler_params=_params("parallel"),
    )(place, r2)


def _cast_into_slot(w, l, place, after):
    _, R, C = w.shape
    tr = _row_tile(R, C * 4)

    def body(place_ref, w_ref, after_ref, o_ref):
        o_ref[...] = w_ref[...].astype(MXU_DTYPE)

    return pl.pallas_call(
        body, name=f"cast_r{R}c{C}",
        grid_spec=pltpu.PrefetchScalarGridSpec(
            num_scalar_prefetch=1, grid=(R // tr,),
            in_specs=[pl.BlockSpec((None, tr, C), lambda i, place_ref: (l, i, 0)), ANY],
            out_specs=pl.BlockSpec((None, tr, C), lambda i, place_ref: (place_ref[1], i, 0)),
        ),
        out_shape=SDS((N_CHIPS, R, C), MXU_DTYPE), compiler_params=_params("parallel"),
    )(place, w, after)


def _place():
    x, y, c = lax.axis_index("x"), lax.axis_index("y"), lax.axis_index("c")
    chips = [(1 - x, y), (x, 1 - y), (1 - x, 1 - y)]
    return x, y, c, chips


def _remote(src, dst, send_sem, recv_sem, to):
    return pltpu.make_async_remote_copy(src_ref=src, dst_ref=dst, send_sem=send_sem, recv_sem=recv_sem, device_id=to, device_id_type=MESH)


def _all_gather8(v):
    R, N = v.shape

    def body(v_ref, out_ref, send_sems, recv_sems, local_sem):
        x, y, c, chips = _place()
        me, sibling = (x, y, c), (x, y, 1 - c)

        def slot(px, py, pc):
            return out_ref.at[4 * px + 2 * py + pc]

        def copy(k, block, to, src=None):
            return _remote(slot(*block) if src is None else src, slot(*block), send_sems.at[k], recv_sems.at[k], to)

        mine = pltpu.make_async_copy(v_ref, slot(*me), local_sem)
        mine.start()
        first = [copy(0, me, sibling, src=v_ref)] + [copy(1 + j, me, (*chip, c), src=v_ref) for j, chip in enumerate(chips)]
        for cp in first:
            cp.start()
        passed = [copy(4 + j, (*chip, c), sibling) for j, chip in enumerate(chips)]
        for j, chip in enumerate(chips):
            copy(1 + j, (*chip, c), me).wait_recv()
            passed[j].start()
        copy(0, sibling, me).wait_recv()
        for j, chip in enumerate(chips):
            copy(4 + j, (*chip, 1 - c), me).wait_recv()
        for cp in first + passed:
            cp.wait_send()
        mine.wait()

    return pl.pallas_call(
        body, name=f"all_gather8_r{R}n{N}", out_shape=SDS((N_DEV, R, N), v.dtype), in_specs=[ANY], out_specs=ANY,
        scratch_shapes=[pltpu.SemaphoreType.DMA((7,)), pltpu.SemaphoreType.DMA((7,)), pltpu.SemaphoreType.DMA],
    )(v)


def _gather_weights(slots):
    n = len(slots)

    def body(*refs):
        ins, outs = refs[:n], refs[n:2 * n]
        send_sems, recv_sems = refs[2 * n:]
        x, y, c, chips = _place()
        q = 2 * x + y
        sibling = (x, y, 1 - c)
        first = []
        for k in range(n):
            for j, chip in enumerate(chips):
                first.append(_remote(ins[k].at[q, c], outs[k].at[q, c], send_sems.at[k, j], recv_sems.at[k, j], (*chip, c)))
                first[-1].start()
        passed = []
        for k in range(n):
            for j, chip in enumerate(chips):
                half = outs[k].at[2 * chip[0] + chip[1], c]
                _remote(half, half, send_sems.at[k, j], recv_sems.at[k, j], sibling).wait_recv()
                passed.append(_remote(half, half, send_sems.at[k, 3 + j], recv_sems.at[k, 3 + j], sibling))
                passed[-1].start()
        for k in range(n):
            for j, chip in enumerate(chips):
                half = outs[k].at[2 * chip[0] + chip[1], 1 - c]
                _remote(half, half, send_sems.at[k, 3 + j], recv_sems.at[k, 3 + j], sibling).wait_recv()
        for cp in first + passed:
            cp.wait_send()

    return pl.pallas_call(
        body, name="gather_weights", out_shape=[SDS(s.shape, s.dtype) for s in slots],
        in_specs=[ANY] * n, out_specs=[ANY] * n, input_output_aliases={k: k for k in range(n)},
        scratch_shapes=[pltpu.SemaphoreType.DMA((n, 6)), pltpu.SemaphoreType.DMA((n, 6))],
    )(*slots)


def _swap_halves(g4s):
    n = len(g4s)

    def body(*refs):
        ins, outs = refs[:n], refs[n:2 * n]
        send_sems, recv_sems = refs[2 * n:]
        x, y, c, _ = _place()
        sibling = (x, y, 1 - c)
        for k in range(n):
            for p in range(N_CHIPS):
                _remote(ins[k].at[p, 1 - c], outs[k].at[p], send_sems.at[k], recv_sems.at[k], sibling).start()
        for k in range(n):
            _remote(outs[k], outs[k], send_sems.at[k], recv_sems.at[k], sibling).wait()

    return pl.pallas_call(
        body, name="swap_halves", out_shape=[SDS((N_CHIPS,) + g.shape[2:], g.dtype) for g in g4s],
        in_specs=[ANY] * n, out_specs=[ANY] * n,
        scratch_shapes=[pltpu.SemaphoreType.DMA((n,)), pltpu.SemaphoreType.DMA((n,))],
    )(*g4s)


def _scatter_regions(hs, lands):
    n = len(hs)

    def body(*refs):
        ins, outs = refs[:n], refs[2 * n:3 * n]
        send_sems, recv_sems = refs[3 * n:]
        x, y, c, chips = _place()
        q = 2 * x + y
        sent = []
        for k in range(n):
            for j, chip in enumerate(chips):
                sent.append(_remote(ins[k].at[2 * chip[0] + chip[1]], outs[k].at[q], send_sems.at[k, j], recv_sems.at[k, j], (*chip, c)))
                sent[-1].start()
        for k in range(n):
            for j, chip in enumerate(chips):
                got = outs[k].at[2 * chip[0] + chip[1]]
                _remote(got, got, send_sems.at[k, j], recv_sems.at[k, j], (x, y, c)).wait_recv()
        for cp in sent:
            cp.wait_send()

    return pl.pallas_call(
        body, name="scatter_regions", out_shape=[SDS(h.shape, h.dtype) for h in lands],
        in_specs=[ANY] * (2 * n), out_specs=[ANY] * n, input_output_aliases={n + k: k for k in range(n)},
        scratch_shapes=[pltpu.SemaphoreType.DMA((n, 3)), pltpu.SemaphoreType.DMA((n, 3))],
    )(*hs, *lands)


def _share_halves(fins):
    n = len(fins)

    def body(*refs):
        ins, outs = refs[:n], refs[n:2 * n]
        send_sems, recv_sems = refs[2 * n:]
        x, y, c, _ = _place()
        sibling = (x, y, 1 - c)
        sent = [_remote(ins[k].at[c], outs[k].at[c], send_sems.at[k], recv_sems.at[k], sibling) for k in range(n)]
        for cp in sent:
            cp.start()
        for k in range(n):
            got = outs[k].at[1 - c]
            _remote(got, got, send_sems.at[k], recv_sems.at[k], sibling).wait_recv()
        for cp in sent:
            cp.wait_send()

    return pl.pallas_call(
        body, name="share_halves", out_shape=[SDS(t.shape, t.dtype) for t in fins],
        in_specs=[ANY] * n, out_specs=[ANY] * n, input_output_aliases={k: k for k in range(n)},
        scratch_shapes=[pltpu.SemaphoreType.DMA((n,)), pltpu.SemaphoreType.DMA((n,))],
    )(*fins)


def _chip_sums(grads, place):
    g4s = [g.reshape(N_CHIPS, 2, g.shape[1] // 2, g.shape[2]) for g in grads]
    pairs = [_add_half(g4, r1, place) for g4, r1 in zip(g4s, _swap_halves(g4s))]
    return [h for h, _ in pairs], [own for _, own in pairs]


def _reduce_finish(lands, place):
    fins = _share_halves([_sum4_into_half(r2, place) for r2 in lands])
    return [f.reshape(2 * f.shape[1], f.shape[2]) for f in fins]


def _reduce_scatter(grads, place):
    hs, lands = _chip_sums(grads, place)
    return _reduce_finish(_scatter_regions(hs, lands), place)


HBM_SPEC = pl.BlockSpec(memory_space=pltpu.HBM)
SEM_SPEC = pl.BlockSpec(memory_space=pltpu.SEMAPHORE)
DATAFLOW = pltpu.SideEffectType.DATAFLOW_SIDE_EFFECTING


def _in_hbm(a):
    return pltpu.with_memory_space_constraint(a, pltpu.HBM)


def _hbm_like(a):
    return pltpu.HBM(a.shape, a.dtype)


def _gather_ici_start(slots, thru, after):
    n = len(slots)

    def body(*refs):
        ins = refs[:n]
        send_sems, recv_sems = refs[n + 2], refs[n + 3]
        x, y, c, chips = _place()
        q = 2 * x + y
        for k in range(n):
            for j, chip in enumerate(chips):
                _remote(ins[k].at[q, c], ins[k].at[q, c], send_sems.at[3 * k + j], recv_sems.at[3 * k + j], (*chip, c)).start()

    out = pl.pallas_call(
        body, name="gather_ici_start",
        out_shape=(pltpu.SemaphoreType.DMA((3 * n,)), pltpu.SemaphoreType.DMA((3 * n,)), *[_hbm_like(s) for s in slots], _hbm_like(thru)),
        in_specs=[HBM_SPEC] * (n + 1) + [ANY], out_specs=(SEM_SPEC, SEM_SPEC, *[HBM_SPEC] * (n + 1)),
        input_output_aliases={k: 2 + k for k in range(n + 1)},
        compiler_params=pltpu.CompilerParams(has_side_effects=DATAFLOW),
    )(*[_in_hbm(s) for s in slots], _in_hbm(thru), after)
    return out[0], out[1], list(out[2:2 + n]), out[2 + n]


def _gather_ici_wait(send_sems, recv_sems, slots, after):
    n = len(slots)

    def body(*refs):
        ins = refs[:n]
        send_sems, recv_sems = refs[n], refs[n + 1]
        x, y, c, chips = _place()
        for k in range(n):
            for j, chip in enumerate(chips):
                got = ins[k].at[2 * chip[0] + chip[1], c]
                cp = _remote(got, got, send_sems.at[3 * k + j], recv_sems.at[3 * k + j], (x, y, c))
                cp.wait_send()
                cp.wait_recv()

    out = pl.pallas_call(
        body, name="gather_ici_wait", out_shape=[_hbm_like(s) for s in slots],
        in_specs=[HBM_SPEC] * n + [SEM_SPEC, SEM_SPEC, ANY], out_specs=[HBM_SPEC] * n,
        input_output_aliases={k: k for k in range(n)},
        compiler_params=pltpu.CompilerParams(has_side_effects=DATAFLOW),
    )(*slots, send_sems, recv_sems, after)
    return list(out)


def _gather_pass_on(slots):
    n = len(slots)

    def body(*refs):
        ins, outs = refs[:n], refs[n:2 * n]
        send_sems, recv_sems = refs[2 * n:]
        x, y, c, chips = _place()
        sibling = (x, y, 1 - c)
        passed = []
        for k in range(n):
            for j, chip in enumerate(chips):
                passed.append(_remote(ins[k].at[2 * chip[0] + chip[1], c], outs[k].at[2 * chip[0] + chip[1], c],
                                      send_sems.at[k, j], recv_sems.at[k, j], sibling))
                passed[-1].start()
        for k in range(n):
            for j, chip in enumerate(chips):
                half = outs[k].at[2 * chip[0] + chip[1], 1 - c]
                _remote(half, half, send_sems.at[k, j], recv_sems.at[k, j], sibling).wait_recv()
        for cp in passed:
            cp.wait_send()

    return pl.pallas_call(
        body, name="gather_pass_on", out_shape=[SDS(s.shape, s.dtype) for s in slots],
        in_specs=[ANY] * n, out_specs=[ANY] * n, input_output_aliases={k: k for k in range(n)},
        scratch_shapes=[pltpu.SemaphoreType.DMA((n, 3)), pltpu.SemaphoreType.DMA((n, 3))],
    )(*slots)


def _scatter_start(hs, lands, thru):
    n = len(hs)

    def body(*refs):
        ins, zones = refs[:n], refs[n:2 * n]
        send_sems, recv_sems = refs[2 * n + 1], refs[2 * n + 2]
        x, y, c, chips = _place()
        q = 2 * x + y
        for k in range(n):
            for j, chip in enumerate(chips):
                _remote(ins[k].at[2 * chip[0] + chip[1]], zones[k].at[q], send_sems.at[3 * k + j], recv_sems.at[3 * k + j], (*chip, c)).start()

    arrays = [*hs, *lands, thru]
    out = pl.pallas_call(
        body, name="scatter_start",
        out_shape=(pltpu.SemaphoreType.DMA((3 * n,)), pltpu.SemaphoreType.DMA((3 * n,)), *[_hbm_like(a) for a in arrays]),
        in_specs=[HBM_SPEC] * len(arrays), out_specs=(SEM_SPEC, SEM_SPEC, *[HBM_SPEC] * len(arrays)),
        input_output_aliases={k: 2 + k for k in range(len(arrays))},
        compiler_params=pltpu.CompilerParams(has_side_effects=DATAFLOW),
    )(*[_in_hbm(a) for a in arrays])
    return out[0], out[1], list(out[2:2 + n]), list(out[2 + n:2 + 2 * n]), out[2 + 2 * n]


def _scatter_wait(send_sems, recv_sems, hs, lands, after):
    n = len(hs)

    def body(*refs):
        ins, zones = refs[:n], refs[n:2 * n]
        send_sems, recv_sems = refs[2 * n], refs[2 * n + 1]
        x, y, c, chips = _place()
        for k in range(n):
            for j, chip in enumerate(chips):
                p = 2 * chip[0] + chip[1]
                cp = _remote(ins[k].at[p], zones[k].at[p], send_sems.at[3 * k + j], recv_sems.at[3 * k + j], (x, y, c))
                cp.wait_send()
                cp.wait_recv()

    out = pl.pallas_call(
        body, name="scatter_wait", out_shape=[_hbm_like(a) for a in [*hs, *lands]],
        in_specs=[HBM_SPEC] * (2 * n) + [SEM_SPEC, SEM_SPEC, ANY], out_specs=[HBM_SPEC] * (2 * n),
        input_output_aliases={k: k for k in range(2 * n)},
        compiler_params=pltpu.CompilerParams(has_side_effects=DATAFLOW),
    )(*hs, *lands, send_sems, recv_sems, after)
    return list(out[n:])


def _split_start(name, arrays, n_sems, issue, extra=()):
    m = len(arrays)

    def body(*refs):
        issue(refs[:m], refs[m + len(extra)], refs[m + len(extra) + 1])

    out = pl.pallas_call(
        body, name=name,
        out_shape=(pltpu.SemaphoreType.DMA((n_sems,)), pltpu.SemaphoreType.DMA((n_sems,)), *[_hbm_like(a) for a in arrays]),
        in_specs=[HBM_SPEC] * m + [ANY] * len(extra), out_specs=(SEM_SPEC, SEM_SPEC, *[HBM_SPEC] * m),
        input_output_aliases={k: 2 + k for k in range(m)},
        compiler_params=pltpu.CompilerParams(has_side_effects=DATAFLOW),
    )(*[_in_hbm(a) for a in arrays], *extra)
    return out[0], out[1], list(out[2:])


def _split_wait(name, send_sems, recv_sems, arrays, after, drain):
    m = len(arrays)

    def body(*refs):
        drain(refs[:m], refs[m], refs[m + 1])

    out = pl.pallas_call(
        body, name=name, out_shape=[_hbm_like(a) for a in arrays],
        in_specs=[HBM_SPEC] * m + [SEM_SPEC, SEM_SPEC, ANY], out_specs=[HBM_SPEC] * m,
        input_output_aliases={k: k for k in range(m)},
        compiler_params=pltpu.CompilerParams(has_side_effects=DATAFLOW),
    )(*arrays, send_sems, recv_sems, after)
    return list(out)


def _wait_both(cp):
    cp.wait_send()
    cp.wait_recv()


class _Flight:
    def __init__(self, name, arrays, n_sems, issue, drain, thru, extra=()):
        self.name, self.drain, self.n = name, drain, len(arrays)
        self.send, self.recv, out = _split_start(name + "_start", [*arrays, thru], n_sems, issue, extra)
        self.arrays, self.thru = out[:-1], out[-1]

    def land(self, after):
        return _split_wait(self.name + "_wait", self.send, self.recv, self.arrays, after, self.drain)


def _gather_flight(tag, ici, d2d, direct, thru):
    kinds = ["ici"] * len(ici) + ["d2d"] * len(d2d) + ["direct"] * len(direct)

    def issue(refs, send_sems, recv_sems):
        x, y, c, chips = _place()
        q = 2 * x + y
        for k, kind in enumerate(kinds):
            for j, chip in enumerate(chips):
                if kind == "ici":
                    src, to = refs[k].at[q, c], (*chip, c)
                elif kind == "d2d":
                    src, to = refs[k].at[2 * chip[0] + chip[1], c], (x, y, 1 - c)
                else:
                    src, to = refs[k].at[q], (*chip, c)
                _remote(src, src, send_sems.at[3 * k + j], recv_sems.at[3 * k + j], to).start()

    def drain(refs, send_sems, recv_sems):
        x, y, c, chips = _place()
        for k, kind in enumerate(kinds):
            for j, chip in enumerate(chips):
                p = 2 * chip[0] + chip[1]
                got = refs[k].at[p] if kind == "direct" else refs[k].at[p, c if kind == "ici" else 1 - c]
                _wait_both(_remote(got, got, send_sems.at[3 * k + j], recv_sems.at[3 * k + j], (x, y, c)))

    return _Flight(f"gather{tag}", [*ici, *d2d, *direct], 3 * len(kinds), issue, drain, thru)


def _swap_flight(tag, g4s, thru):
    n = len(g4s)
    zones = [lax.empty((N_CHIPS,) + g.shape[2:], g.dtype) for g in g4s]

    def issue(refs, send_sems, recv_sems):
        x, y, c, _ = _place()
        for k in range(n):
            for p in range(N_CHIPS):
                _remote(refs[k].at[p, 1 - c], refs[n + k].at[p], send_sems.at[N_CHIPS * k + p], recv_sems.at[N_CHIPS * k + p], (x, y, 1 - c)).start()

    def drain(refs, send_sems, recv_sems):
        x, y, c, _ = _place()
        for k in range(n):
            for p in range(N_CHIPS):
                got = refs[n + k].at[p]
                _wait_both(_remote(got, got, send_sems.at[N_CHIPS * k + p], recv_sems.at[N_CHIPS * k + p], (x, y, c)))

    return _Flight(f"swap{tag}", [*g4s, *zones], N_CHIPS * n, issue, drain, thru)


def _scatter_flight(tag, hs, lands, thru):
    n = len(hs)

    def issue(refs, send_sems, recv_sems):
        x, y, c, chips = _place()
        q = 2 * x + y
        for k in range(n):
            for j, chip in enumerate(chips):
                _remote(refs[k].at[2 * chip[0] + chip[1]], refs[n + k].at[q], send_sems.at[3 * k + j], recv_sems.at[3 * k + j], (*chip, c)).start()

    def drain(refs, send_sems, recv_sems):
        x, y, c, chips = _place()
        for k in range(n):
            for j, chip in enumerate(chips):
                got = refs[n + k].at[2 * chip[0] + chip[1]]
                _wait_both(_remote(got, got, send_sems.at[3 * k + j], recv_sems.at[3 * k + j], (x, y, c)))

    return _Flight(f"scatter{tag}", [*hs, *lands], 3 * n, issue, drain, thru)


def _exchange_flight(buf, thru):
    flips = [(fx, fy, fc) for fx in (0, 1) for fy in (0, 1) for fc in (0, 1)][1:]

    def peers():
        x, y, c, _ = _place()
        return (x, y, c), [((1 - x) if fx else x, (1 - y) if fy else y, (1 - c) if fc else c) for fx, fy, fc in flips]

    def slot(ref, dev):
        return ref.at[4 * dev[0] + 2 * dev[1] + dev[2]]

    def issue(refs, send_sems, recv_sems):
        me, others = peers()
        for j, to in enumerate(others):
            _remote(slot(refs[0], me), slot(refs[0], me), send_sems.at[j], recv_sems.at[j], to).start()

    def drain(refs, send_sems, recv_sems):
        me, others = peers()
        for j, frm in enumerate(others):
            got = slot(refs[0], frm)
            _wait_both(_remote(got, got, send_sems.at[j], recv_sems.at[j], me))

    return _Flight("exchange", [buf], len(flips), issue, drain, thru)


def _share_flight(tag, fins, thru):
    n = len(fins)

    def issue(refs, send_sems, recv_sems):
        x, y, c, _ = _place()
        for k in range(n):
            _remote(refs[k].at[c], refs[k].at[c], send_sems.at[k], recv_sems.at[k], (x, y, 1 - c)).start()

    def drain(refs, send_sems, recv_sems):
        x, y, c, _ = _place()
        for k in range(n):
            got = refs[k].at[1 - c]
            _wait_both(_remote(got, got, send_sems.at[k], recv_sems.at[k], (x, y, c)))

    return _Flight(f"share{tag}", fins, n, issue, drain, thru)


def _pair_blocks(w):
    h, d, _ = w.shape
    z = jnp.zeros((h // 2, d, d), w.dtype)
    return jnp.concatenate([jnp.concatenate([w[0::2], z], axis=2), jnp.concatenate([z, w[1::2]], axis=2)], axis=1)


def _unpair_blocks(b):
    n, dd, _ = b.shape
    d = dd // 2
    return jnp.stack([b[:, :d, :d], b[:, d:, d:]], axis=1).reshape(2 * n, d, d)


def _pad_rows(a, rows):
    return jnp.pad(a, ((0, rows - a.shape[0]), (0, 0)))


class _Packer:
    def __init__(self, shapes, width=1024, row_multiple=64):
        self.shapes = shapes
        self.sizes = [math.prod(s) for s in shapes]
        total = sum(self.sizes)
        self.width = width
        self.rows = -(-total // (width * row_multiple)) * row_multiple
        self.pad = self.rows * width - total

    def pack(self, arrays):
        flat = jnp.concatenate([a.reshape(-1).astype(F32) for a in arrays] + [jnp.zeros((self.pad,), F32)])
        return flat.reshape(self.rows, self.width)

    def unpack(self, packed):
        flat = packed.reshape(-1)
        out, off = [], 0
        for s, n in zip(self.shapes, self.sizes):
            out.append(flat[off:off + n].reshape(s))
            off += n
        return out


SMALL = ["b_ada", "ffn1_norm", "mix_norm", "conv_w", "conv_b", "gate_a_w", "gate_a_b", "gate_x_w", "gate_x_b", "lru_lambda",
         "v_norm", "spatial_w", "spatial_b", "lru_out_norm", "gmlp_out_norm", "ffn2_norm", "final_norm"]
BIG = ["ffn1_w_gu", "ffn1_w_down", "w_in", "w_out", "ffn2_w_gu", "ffn2_w_down"]
GROUPS = (("ffn1_w_gu", "ffn1_w_down"), ("w_in", "w_out"), ("ffn2_w_gu", "ffn2_w_down"))
FWD_GROUPS = (("ffn1_w_gu",), ("ffn1_w_down",), ("w_in", "w_out"), ("ffn2_w_gu",), ("ffn2_w_down",))
MIN_AGE = {"swap": 1, "scatter": 2, "share": 1}
WEIGHTS = ["w_ada", "b_ada", "ffn1_norm", "ffn1_w_gu", "ffn1_w_down", "mix_norm", "w_in", "conv_w", "conv_b", "gate_a_w", "gate_a_b",
           "gate_x_w", "gate_x_b", "lru_lambda", "v_norm", "spatial_w", "spatial_b", "lru_out_norm", "gmlp_out_norm", "w_out",
           "ffn2_norm", "ffn2_w_gu", "ffn2_w_down", "final_norm"]


def kernel(x, c, w_ada, b_ada, ffn1_norm, ffn1_w_gu, ffn1_w_down, mix_norm, w_in, conv_w, conv_b, gate_a_w, gate_a_b, gate_x_w, gate_x_b, lru_lambda, v_norm, spatial_w, spatial_b, lru_out_norm, gmlp_out_norm, w_out, ffn2_norm, ffn2_w_gu, ffn2_w_down, final_norm, loss_target, m_w_ada, m_b_ada, m_ffn1_norm, m_ffn1_w_gu, m_ffn1_w_down, m_mix_norm, m_w_in, m_conv_w, m_conv_b, m_gate_a_w, m_gate_a_b, m_gate_x_w, m_gate_x_b, m_lru_lambda, m_v_norm, m_spatial_w, m_spatial_b, m_lru_out_norm, m_gmlp_out_norm, m_w_out, m_ffn2_norm, m_ffn2_w_gu, m_ffn2_w_down, m_final_norm, v_w_ada, v_b_ada, v_ffn1_norm, v_ffn1_w_gu, v_ffn1_w_down, v_mix_norm, v_w_in, v_conv_w, v_conv_b, v_gate_a_w, v_gate_a_b, v_gate_x_w, v_gate_x_b, v_lru_lambda, v_v_norm, v_spatial_w, v_spatial_b, v_lru_out_norm, v_gmlp_out_norm, v_w_out, v_ffn2_norm, v_ffn2_w_gu, v_ffn2_w_down, v_final_norm):
    given = dict(locals())
    W = {n: given[n] for n in WEIGHTS}
    L = w_ada.shape[0]
    S, D = x.shape[1], x.shape[2]
    LW = conv_b.shape[1]
    hd = LW // HEADS
    xi, yi, ci = lax.axis_index("x"), lax.axis_index("y"), lax.axis_index("c")
    chip = 2 * xi + yi
    dev = 2 * chip + ci
    place = jnp.stack([ci, chip]).astype(jnp.int32)
    xs = x.reshape(S, D)
    tgt = loss_target.reshape(S, D)

    c_all = _all_gather8(_pad_rows(c, SUBLANES))[:, 0, :]
    n_ada = w_ada.shape[2]
    b_shard = lax.dynamic_slice_in_dim(b_ada, chip * n_ada, n_ada, axis=1)
    mod_shard = _ada_fwd(_pad_rows(c_all, 2 * SUBLANES), w_ada, b_shard[:, None, :])

    def in_slot(block):
        return lax.dynamic_update_index_in_dim(jnp.zeros((N_CHIPS,) + block.shape, block.dtype), block, chip, 0)

    cws = LW // N_CHIPS
    small = [in_slot(mod_shard.reshape(L * 2 * SUBLANES, n_ada)), in_slot(conv_w.reshape(L * CONV_WIDTH, cws))]

    def half_view(s):
        return s.reshape(N_CHIPS, 2, s.shape[1] // 2, s.shape[2])

    stages = [(l, names) for l in range(L) for names in FWD_GROUPS]
    seq = [[half_view(_cast_into_slot(W[n], l, place, place)) for n in names] for l, names in stages[:1]]
    flights = {}

    def launch(t, thru, direct=()):
        ici = seq[t] if t < len(seq) else []
        d2d = seq[t - 1] if 1 <= t <= len(seq) else []
        if ici or d2d or direct:
            flights[t] = _gather_flight(t, ici, d2d, list(direct), thru)
            thru = flights[t].thru
        return thru

    def land(t, after):
        if t not in flights:
            return []
        out = flights.pop(t).land(after)
        ni = len(seq[t]) if t < len(seq) else 0
        nd = len(seq[t - 1]) if 1 <= t <= len(seq) else 0
        if ni:
            seq[t] = out[:ni]
        if nd:
            seq[t - 1] = out[ni:ni + nd]
        return out[ni + nd:]

    def group_weights(t):
        return [s.reshape(N_CHIPS, -1, s.shape[3]) for s in seq[t]]

    c_all = launch(0, c_all, small)
    seq += [[half_view(_cast_into_slot(W[n], l, place, c_all)) for n in names] for l, names in stages[1:]]
    mod_all, conv_all = land(0, seq[-1][-1])
    mod_all = launch(1, mod_all)
    land(1, mod_all)
    mod_rows = lax.dynamic_index_in_dim(mod_all.reshape(N_CHIPS, L, 2 * SUBLANES, n_ada), dev, axis=2, keepdims=False)
    mod = mod_rows.transpose(1, 0, 2).reshape(L, N_MOD, 1, D)
    conv_full = conv_all.reshape(N_CHIPS, L, CONV_WIDTH, cws).transpose(1, 2, 0, 3).reshape(L, CONV_WIDTH, LW)

    tril = jnp.tril(jnp.ones((CHUNK, CHUNK), F32))
    seg = (jnp.arange(LW)[:, None] // hd == jnp.arange(LW)[None, :] // hd).astype(jnp.bfloat16)

    def mixer_params(l):
        ws = spatial_w[l] * tril
        wsp = jnp.concatenate([ws[0::2], ws[1::2]], axis=2)
        wa, wx = _pair_blocks(gate_a_w[l]), _pair_blocks(gate_x_w[l])
        return dict(
            cw=conv_full[l], cb=conv_b[l][None],
            wa=wa.astype(MXU_DTYPE), wx=wx.astype(MXU_DTYPE), wat=wa.transpose(0, 2, 1).astype(MXU_DTYPE), wxt=wx.transpose(0, 2, 1).astype(MXU_DTYPE),
            ba=gate_a_b[l].reshape(1, LW), bx=gate_x_b[l].reshape(1, LW), lam=lru_lambda[l][None], gv=v_norm[l][None],
            wsp=wsp.astype(MXU_DTYPE), wspt=wsp.transpose(0, 2, 1).astype(MXU_DTYPE),
            bfull=jnp.repeat(spatial_b[l].T, hd, axis=1), g_lru=lru_out_norm[l][None], g_gm=gmlp_out_norm[l][None])

    saved = []
    xcur = xs
    for l in range(L):
        mp, md = mixer_params(l), mod[l]
        s = dict(lw={}, mp=mp, md=md)
        lw = s["lw"]
        t = len(FWD_GROUPS) * l
        s["x0"] = xcur
        s["h1"] = launch(t + 2, _modnorm(xcur, ffn1_norm[l][None], md[0], md[1]))
        lw["gu1"], = group_weights(t)
        s["a1"], s["gu1"] = _ffn_up(s["h1"], lw["gu1"])
        land(t + 2, s["a1"])
        s["a1"] = launch(t + 3, s["a1"])
        lw["d1"] = group_weights(t + 1)[0].reshape(-1, D)
        s["f1"], xcur = _mm_res(s["a1"], lw["d1"], xcur, md[2], 0.5)
        land(t + 3, xcur)
        s["x1"] = xcur
        s["h2"] = launch(t + 4, _modnorm(xcur, mix_norm[l][None], md[3], md[4]))
        lw["win"], wout = group_weights(t + 2)
        lw["wout"] = wout.reshape(-1, D)
        s["proj"] = _mm_chunks(s["h2"], lw["win"])
        s["ylru"] = _lru_fwd(s["proj"], mp["cw"], mp["cb"], mp["wa"], mp["ba"], mp["wx"], mp["bx"], mp["lam"])
        s["yn"], s["ygm"] = _gmlp_fwd(s["proj"], s["ylru"], mp["gv"], seg, mp["wsp"], mp["bfull"], mp["g_lru"], mp["g_gm"])
        s["f2"], xcur = _mm_res(s["yn"], lw["wout"], xcur, md[5], 1.0)
        land(t + 4, xcur)
        s["x2"] = xcur
        s["h3"] = launch(t + 5, _modnorm(xcur, ffn2_norm[l][None], md[6], md[7]))
        lw["gu2"], = group_weights(t + 3)
        s["a3"], s["gu3"] = _ffn_up(s["h3"], lw["gu2"])
        land(t + 5, s["a3"])
        s["a3"] = launch(t + 6, s["a3"])
        lw["d2"] = group_weights(t + 4)[0].reshape(-1, D)
        s["f3"], xcur = _mm_res(s["a3"], lw["d2"], xcur, md[8], 0.5)
        land(t + 6, xcur)
        saved.append(s)

    dx, dq, head_acc = _loss_head(xcur, tgt, final_norm[None], saved[-1]["md"][8], 0.5)
    loss = lax.psum(jnp.sum(head_acc[1]), ("x", "y", "c"))
    small_grads = {}
    big_grads = {n: [None] * L for n in BIG}
    dmods = [None] * L
    zero_row = jnp.zeros((1, D), F32)

    def ffn_bwd(names, l, dx, dq, x_in, h, a, gu, f, wgu, wd, gn, sc, next_gate, next_scale):
        big_grads[names[1]][l] = _mm_tn_chunks(a, dq[None], 1408, 1024)[0].reshape(N_CHIPS, -1, D)
        dgu = _ffn_bwd_act(dq, wd, gu)
        C = dgu.shape[3]
        dgu4 = dgu.reshape(N_CHIPS, S, C)
        big_grads[names[0]][l] = _mm_tn_chunks(h, dgu4, 1024, C)
        dgu4 = reduce_group(names, l, big_grads[names[0]][l], dgu4)
        dh = _mm_nt_chunks(dgu4, wgu)
        dh = move_on(dh, dh)
        dx, dq, acc = _norm_bwd(x_in, dh, dx, f, gn, sc, 0.5, next_gate, next_scale)
        return dx, move_on(dx, dq), acc

    stepped = {n: None for n in BIG}
    reducing = []

    clock = [0]
    to_step = []

    def step_reduced(after):
        while to_step:
            name, l, g = to_step.pop(0)
            stepped[name] = _adamw_layer(W[name], g, given["m_" + name], given["v_" + name], l, stepped[name], after)
            after = stepped[name][1]
        return after

    def move_on(after, thru, force=False):
        clock[0] += 1
        for grp in list(reducing):
            if not force and clock[0] - grp["since"] < MIN_AGE[grp["step"]]:
                continue
            grp["since"] = clock[0]
            landed = grp["flight"].land(after)
            n = len(grp["names"])
            if grp["step"] == "swap":
                pairs = [_add_half(g4, r1, place) for g4, r1 in zip(landed[:n], landed[n:])]
                grp.update(step="scatter", flight=_scatter_flight(grp["tag"], [h for h, _ in pairs], [own for _, own in pairs], thru))
            elif grp["step"] == "scatter":
                grp.update(step="share", flight=_share_flight(grp["tag"], [_sum4_into_half(r2, place) for r2 in landed[n:]], thru))
            else:
                to_step.extend((name, grp["l"], fin.reshape(2 * fin.shape[1], fin.shape[2])) for name, fin in zip(grp["names"], landed))
                reducing.remove(grp)
                continue
            thru = grp["flight"].thru
        return thru

    def reduce_group(names, l, after, thru):
        thru = move_on(after, thru)
        g4s = [big_grads[n][l].reshape(N_CHIPS, 2, big_grads[n][l].shape[1] // 2, big_grads[n][l].shape[2]) for n in names]
        tag = f"{l}{GROUPS.index(names)}"
        reducing.append(dict(names=names, l=l, tag=tag, step="swap", since=clock[0], flight=_swap_flight(tag, g4s, thru)))
        return reducing[-1]["flight"].thru
    for l in reversed(range(L)):
        s = saved[l]
        lw, mp, md = s["lw"], s["mp"], s["md"]
        dx, dq, acc3 = ffn_bwd(
            GROUPS[2], l, dx, dq, s["x2"], s["h3"], s["a3"], s["gu3"], s["f3"], lw["gu2"], lw["d2"], ffn2_norm[l][None], md[7], md[5], 1.0)
        big_grads["w_out"][l] = _mm_tn_chunks(s["yn"], dq[None], 1024, 1024)[0].reshape(N_CHIPS, -1, D)
        dyn = _mm_nt_chunks(dq[None], lw["wout"][None])
        dylru, duv, dwsp, dbfull, gacc = _gmlp_bwd(s["proj"], s["ylru"], s["ygm"], dyn, mp["gv"], seg, mp["wsp"], mp["wspt"], mp["bfull"], mp["g_lru"], mp["g_gm"])
        dxg, dwa, dwx, lvec = _lru_bwd(s["proj"], dylru, mp["cw"], mp["cb"], mp["wa"], mp["ba"], mp["wx"], mp["bx"], mp["lam"], mp["wat"], mp["wxt"])
        dproj = jnp.concatenate([dxg, duv], axis=0)
        big_grads["w_in"][l] = _mm_tn_chunks(s["h2"], dproj, 1024, LW)
        dproj = reduce_group(GROUPS[1], l, big_grads["w_in"][l], dproj)
        dh2 = _mm_nt_chunks(dproj, lw["win"])
        dh2 = move_on(dh2, dh2)
        dx, dq, acc2 = _norm_bwd(s["x1"], dh2, dx, s["f2"], mix_norm[l][None], md[4], 1.0, md[2], 0.5)
        dq = move_on(dx, dq)
        if l > 0:
            ng, ns = saved[l - 1]["md"][8], 0.5
        else:
            ng, ns = zero_row, 0.0
        dx, dq, acc1 = ffn_bwd(
            GROUPS[0], l, dx, dq, s["x0"], s["h1"], s["a1"], s["gu1"], s["f1"], lw["gu1"], lw["d1"], ffn1_norm[l][None], md[1], ng, ns)

        dmods[l] = jnp.concatenate([acc1[0:2], acc1[3:4], acc2[0:2], acc2[3:4], acc3[0:2], acc3[3:4]], axis=0)
        dws = jnp.stack([dwsp[:, :, :CHUNK], dwsp[:, :, CHUNK:]], axis=1).reshape(HEADS, CHUNK, CHUNK) * tril
        lg = {"ffn1_norm": acc1[2], "mix_norm": acc2[2], "ffn2_norm": acc3[2],
              "conv_w": lvec[4:8], "conv_b": lvec[3], "gate_a_w": _unpair_blocks(dwa), "gate_a_b": lvec[0].reshape(HEADS, hd),
              "gate_x_w": _unpair_blocks(dwx), "gate_x_b": lvec[1].reshape(HEADS, hd), "lru_lambda": lvec[2], "v_norm": gacc[2],
              "spatial_w": dws, "spatial_b": dbfull.reshape(CHUNK, HEADS, hd).sum(-1).T, "lru_out_norm": gacc[0], "gmlp_out_norm": gacc[1]}
        for n, g in lg.items():
            small_grads.setdefault(n, [None] * L)[l] = g

    grad_x = dx.reshape(x.shape)

    per_layer = [n for n in SMALL if n not in ("b_ada", "final_norm")]
    part = [jnp.stack(small_grads[n]) for n in per_layer] + [head_acc[0], jnp.stack(dmods)]
    packer = _Packer([p.shape for p in part])
    packed = packer.pack(part)
    exchange = _exchange_flight(lax.dynamic_update_index_in_dim(jnp.zeros((N_DEV,) + packed.shape, F32), packed, dev, 0), dq)
    dq = exchange.thru
    done = step_reduced(dq)
    while reducing:
        dq = move_on(done, dq, force=True)
        done = step_reduced(dq)
    gathered, = exchange.land(done)
    summed = packer.unpack(_sum_leading(gathered))
    grads = dict(zip(per_layer + ["final_norm"], summed[:-1]))
    grads["b_ada"] = summed[-1].reshape(L, N_MOD * D)
    off = sum(packer.sizes[:-1])
    dmod_rows = gathered.reshape(N_DEV, -1)[:, off:off + L * N_MOD * D].reshape(N_DEV, L, N_MOD * D)
    dmod_shard = lax.dynamic_slice_in_dim(dmod_rows, chip * n_ada, n_ada, axis=2).transpose(1, 0, 2)
    grads["w_ada"] = _ada_grad(c_all.T, dmod_shard)
    grads["conv_w"] = lax.dynamic_slice_in_dim(grads["conv_w"], chip * cws, cws, axis=2)

    delta, new_m, new_v = {}, {}, {}
    for n in BIG:
        grads[n], delta[n], new_m[n], new_v[n] = stepped[n]
    shp = w_ada.shape
    d_, m_, v_ = _adamw(*[a.reshape(-1, shp[-1]) for a in (w_ada, grads["w_ada"], m_w_ada, v_w_ada)])
    delta["w_ada"], new_m["w_ada"], new_v["w_ada"] = d_.reshape(shp), m_.reshape(shp), v_.reshape(shp)
    spk = _Packer([W[n].shape for n in SMALL])
    d_, m_, v_ = _adamw(spk.pack([W[n] for n in SMALL]), spk.pack([grads[n] for n in SMALL]),
                        spk.pack([given["m_" + n] for n in SMALL]), spk.pack([given["v_" + n] for n in SMALL]))
    for n, a, b, e in zip(SMALL, spk.unpack(d_), spk.unpack(m_), spk.unpack(v_)):
        delta[n], new_m[n], new_v[n] = a, b, e
    grads = {n: grads[n].reshape(W[n].shape) for n in WEIGHTS}
    return (loss, grad_x, *[grads[n] for n in WEIGHTS], *[delta[n] for n in WEIGHTS], *[new_m[n] for n in WEIGHTS], *[new_v[n] for n in WEIGHTS])
```

```python
import math

import jax
import jax.numpy as jnp
from jax import lax
from jax.experimental import pallas as pl
from jax.experimental.pallas import tpu as pltpu

F32 = jnp.float32
MXU_DTYPE = jnp.bfloat16
ACT_DTYPE = jnp.bfloat16
XFER_DTYPE = jnp.bfloat16
EPS = 1e-6
RG_LRU_C = 8.0
N_MOD = 9
CONV_WIDTH = 4
HEADS = 8
CHUNK = 128
LANES = 128
SUBLANES = 8
N_CHIPS = 4
N_DEV = 8
ADAM_LR, ADAM_B1, ADAM_B2, ADAM_EPS, ADAM_WD, ADAM_STEP = 0.001, 0.9, 0.999, 1e-08, 0.01, 10
VMEM_LIMIT_BYTES = 60 * 1024 * 1024
ROW_TILE_BYTES = 1 << 20
GELU_C = math.sqrt(2.0 / math.pi)
GELU_A = 0.044715

ANY = pl.BlockSpec(memory_space=pl.ANY)
MESH = pl.DeviceIdType.MESH
SDS = jax.ShapeDtypeStruct


def _params(*sem):
    return pltpu.CompilerParams(dimension_semantics=sem, vmem_limit_bytes=VMEM_LIMIT_BYTES)


def _dot(a, b):
    return jnp.dot(a.astype(MXU_DTYPE), b.astype(MXU_DTYPE), preferred_element_type=F32)


def _dot_nt(a, b):
    return lax.dot_general(a.astype(MXU_DTYPE), b.astype(MXU_DTYPE), (((1,), (1,)), ((), ())), preferred_element_type=F32)


def _dot_tn(a, b):
    return lax.dot_general(a.astype(MXU_DTYPE), b.astype(MXU_DTYPE), (((0,), (0,)), ((), ())), preferred_element_type=F32)


def _gelu(x):
    return x * (0.5 * (1.0 + jnp.tanh(GELU_C * (x + GELU_A * (x * x * x)))))


def _gelu_grad(x):
    t = jnp.tanh(GELU_C * (x + GELU_A * (x * x * x)))
    return 0.5 * (1.0 + t) + 0.5 * x * (1.0 - t * t) * (GELU_C * (1.0 + 3.0 * GELU_A * x * x))


def _sigmoid(x):
    return jax.nn.sigmoid(x)


def _rsqrt_ms(x):
    return lax.rsqrt(jnp.mean(x * x, axis=-1, keepdims=True) + EPS)


def _rowsum(x):
    return jnp.sum(x, axis=0, keepdims=True)


def _tile(n, want):
    t = min(n, want)
    assert n % t == 0, (n, want)
    return t


def _row_tile(rows, row_bytes):
    step = 2 * SUBLANES
    cap = max(step, ROW_TILE_BYTES // row_bytes)
    best = None
    for t in range(step, min(rows, cap) + 1, step):
        if rows % t == 0:
            best = t
    assert best is not None, (rows, row_bytes)
    return best


def _modnorm(x, gn, sh, sc):
    S, D = x.shape
    tm = _tile(S, 512)

    def body(x_ref, gn_ref, sh_ref, sc_ref, h_ref):
        xv = x_ref[...]
        h = (xv * _rsqrt_ms(xv) * gn_ref[...]) * (1.0 + sc_ref[...]) + sh_ref[...]
        h_ref[...] = h.astype(ACT_DTYPE)

    row = pl.BlockSpec((1, D), lambda i: (0, 0))
    return pl.pallas_call(
        body, name="modnorm", grid=(S // tm,),
        in_specs=[pl.BlockSpec((tm, D), lambda i: (i, 0)), row, row, row],
        out_specs=pl.BlockSpec((tm, D), lambda i: (i, 0)),
        out_shape=SDS((S, D), ACT_DTYPE), compiler_params=_params("parallel"),
    )(x, gn, sh, sc)


def _norm_bwd(x, dh, dxo, f, gn, sc, res_scale, next_gate, next_scale):
    S, D = x.shape
    tm = _tile(S, 256)

    def body(x_ref, dh_ref, dxo_ref, f_ref, gn_ref, sc_ref, ng_ref, dx_ref, dq_ref, acc_ref):
        @pl.when(pl.program_id(0) == 0)
        def _():
            acc_ref[...] = jnp.zeros_like(acc_ref)

        xv, dh, dxo = x_ref[...], dh_ref[...], dxo_ref[...]
        r = _rsqrt_ms(xv)
        xhat = xv * r
        gn = gn_ref[...]
        dn = dh * (1.0 + sc_ref[...])
        dxh = dn * gn
        dx = dxo + r * (dxh - xhat * jnp.mean(dxh * xhat, axis=-1, keepdims=True))
        dx_ref[...] = dx
        dq_ref[...] = ((next_scale * ng_ref[...]) * dx).astype(ACT_DTYPE)
        acc_ref[0:1, :] += _rowsum(dh)
        acc_ref[1:2, :] += _rowsum(dh * (xhat * gn))
        acc_ref[2:3, :] += _rowsum(dn * xhat)
        acc_ref[3:4, :] += _rowsum((res_scale * f_ref[...]) * dxo)

    tile = pl.BlockSpec((tm, D), lambda i: (i, 0))
    row = pl.BlockSpec((1, D), lambda i: (0, 0))
    return pl.pallas_call(
        body, name="norm_bwd", grid=(S // tm,),
        in_specs=[tile, tile, tile, tile, row, row, row],
        out_specs=[tile, tile, pl.BlockSpec((SUBLANES, D), lambda i: (0, 0))],
        out_shape=[SDS((S, D), F32), SDS((S, D), ACT_DTYPE), SDS((SUBLANES, D), F32)],
        compiler_params=_params("arbitrary"),
    )(x, dh, dxo, f, gn, sc, next_gate)


def _loss_head(x, target, gn, next_gate, next_scale):
    S, D = x.shape
    tm = _tile(S, 256)

    def body(x_ref, t_ref, gn_ref, ng_ref, dx_ref, dq_ref, acc_ref):
        @pl.when(pl.program_id(0) == 0)
        def _():
            acc_ref[...] = jnp.zeros_like(acc_ref)

        xv = x_ref[...]
        r = _rsqrt_ms(xv)
        xhat = xv * r
        gn = gn_ref[...]
        err = xhat * gn - t_ref[...]
        dy = err * (1.0 / D)
        dxh = dy * gn
        dx = r * (dxh - xhat * jnp.mean(dxh * xhat, axis=-1, keepdims=True))
        dx_ref[...] = dx
        dq_ref[...] = ((next_scale * ng_ref[...]) * dx).astype(ACT_DTYPE)
        acc_ref[0:1, :] += _rowsum(dy * xhat)
        acc_ref[1:2, :] += _rowsum(err * err) * (0.5 / D)

    tile = pl.BlockSpec((tm, D), lambda i: (i, 0))
    row = pl.BlockSpec((1, D), lambda i: (0, 0))
    return pl.pallas_call(
        body, name="loss_head", grid=(S // tm,),
        in_specs=[tile, tile, row, row],
        out_specs=[tile, tile, pl.BlockSpec((SUBLANES, D), lambda i: (0, 0))],
        out_shape=[SDS((S, D), F32), SDS((S, D), ACT_DTYPE), SDS((SUBLANES, D), F32)],
        compiler_params=_params("arbitrary"),
    )(x, target, gn, next_gate)


def _ffn_up(h, wgu):
    S, D = h.shape
    C = wgu.shape[2]
    tm = _tile(S, 512)

    def body(h_ref, wg_ref, wu_ref, a_ref, gu_ref):
        hv = h_ref[...]
        g = _dot(hv, wg_ref[...])
        u = _dot(hv, wu_ref[...])
        a_ref[...] = (g * _sigmoid(g) * u).astype(ACT_DTYPE)
        gu_ref[0] = g.astype(ACT_DTYPE)
        gu_ref[1] = u.astype(ACT_DTYPE)

    return pl.pallas_call(
        body, name="ffn_up", grid=(2, S // tm),
        in_specs=[
            pl.BlockSpec((tm, D), lambda j, i: (i, 0)),
            pl.BlockSpec((None, D, C), lambda j, i: (j, 0, 0)),
            pl.BlockSpec((None, D, C), lambda j, i: (2 + j, 0, 0)),
        ],
        out_specs=[
            pl.BlockSpec((tm, C), lambda j, i: (i, j)),
            pl.BlockSpec((2, None, tm, C), lambda j, i: (0, j, i, 0)),
        ],
        out_shape=[SDS((S, 2 * C), ACT_DTYPE), SDS((2, 2, S, C), ACT_DTYPE)],
        compiler_params=_params("parallel", "parallel"),
    )(h, wgu, wgu)


def _ffn_bwd_act(dq, wd, gu):
    S, D = dq.shape
    C = gu.shape[3]
    tm = _tile(S, 512)

    def body(dq_ref, wd_ref, gu_ref, dgu_ref):
        da = _dot_nt(dq_ref[...], wd_ref[...])
        g = gu_ref[0].astype(F32)
        u = gu_ref[1].astype(F32)
        s = _sigmoid(g)
        dgu_ref[0] = (da * u * (s * (1.0 + g * (1.0 - s)))).astype(ACT_DTYPE)
        dgu_ref[1] = (da * (g * s)).astype(ACT_DTYPE)

    gu_spec = pl.BlockSpec((2, None, tm, C), lambda j, i: (0, j, i, 0))
    return pl.pallas_call(
        body, name="ffn_bwd_act", grid=(2, S // tm),
        in_specs=[pl.BlockSpec((tm, D), lambda j, i: (i, 0)), pl.BlockSpec((C, D), lambda j, i: (j, 0)), gu_spec],
        out_specs=gu_spec,
        out_shape=SDS(gu.shape, ACT_DTYPE),
        compiler_params=_params("parallel", "parallel"),
    )(dq, wd, gu)


def _mm_res(a, w, x, gate, scale):
    S, K = a.shape
    D = w.shape[1]
    tm, tn = _tile(S, 512), _tile(D, 1024)

    def body(a_ref, w_ref, x_ref, g_ref, f_ref, xo_ref):
        f = _dot(a_ref[...], w_ref[...])
        f_ref[...] = f
        xo_ref[...] = x_ref[...] + (scale * g_ref[...]) * f

    tile = pl.BlockSpec((tm, tn), lambda j, i: (i, j))
    return pl.pallas_call(
        body, name=f"mm_res_k{K}", grid=(D // tn, S // tm),
        in_specs=[pl.BlockSpec((tm, K), lambda j, i: (i, 0)), pl.BlockSpec((K, tn), lambda j, i: (0, j)), tile,
                  pl.BlockSpec((1, tn), lambda j, i: (0, j))],
        out_specs=[tile, tile],
        out_shape=[SDS((S, D), F32), SDS((S, D), F32)],
        compiler_params=_params("parallel", "parallel"),
    )(a, w, x, gate)


def _mm_chunks(h, wc):
    S, K = h.shape
    P, _, N = wc.shape
    tm = _tile(S, 512)

    def body(h_ref, w_ref, o_ref):
        hv = h_ref[...]
        for p in range(P):
            o_ref[:, p * N:(p + 1) * N] = _dot(hv, w_ref[p])

    return pl.pallas_call(
        body, name="mm_chunks", grid=(S // tm,),
        in_specs=[pl.BlockSpec((tm, K), lambda i: (i, 0)), pl.BlockSpec((P, K, N), lambda i: (0, 0, 0))],
        out_specs=pl.BlockSpec((tm, P * N), lambda i: (i, 0)),
        out_shape=SDS((S, P * N), F32),
        compiler_params=_params("parallel"),
    )(h, wc)


def _mm_nt_chunks(ac, wc):
    P, S, K = ac.shape
    N = wc.shape[1]
    tm, tn = _tile(S, 512), _tile(N, 1024)

    def body(a_ref, w_ref, o_ref):
        acc = _dot_nt(a_ref[0], w_ref[0])
        for p in range(1, P):
            acc += _dot_nt(a_ref[p], w_ref[p])
        o_ref[...] = acc

    return pl.pallas_call(
        body, name=f"mm_nt_p{P}k{K}", grid=(S // tm, N // tn),
        in_specs=[pl.BlockSpec((P, tm, K), lambda i, j: (0, i, 0)), pl.BlockSpec((P, tn, K), lambda i, j: (0, j, 0))],
        out_specs=pl.BlockSpec((tm, tn), lambda i, j: (i, j)),
        out_shape=SDS((S, N), F32),
        compiler_params=_params("parallel", "parallel"),
    )(ac, wc)


def _mm_tn_chunks(a, bc, tile_m, tile_n):
    S, M = a.shape
    P, _, N = bc.shape
    ts, tm, tn = _tile(S, 2048), _tile(M, tile_m), _tile(N, tile_n)

    def body(a_ref, b_ref, o_ref):
        @pl.when(pl.program_id(3) == 0)
        def _():
            o_ref[...] = jnp.zeros_like(o_ref)

        o_ref[...] += _dot_tn(a_ref[...], b_ref[...])

    return pl.pallas_call(
        body, name=f"mm_tn_m{M}n{N}", grid=(P, M // tm, N // tn, S // ts),
        in_specs=[pl.BlockSpec((ts, tm), lambda p, m, n, k: (k, m)), pl.BlockSpec((None, ts, tn), lambda p, m, n, k: (p, k, n))],
        out_specs=pl.BlockSpec((None, tm, tn), lambda p, m, n, k: (p, m, n)),
        out_shape=SDS((P, M, N), F32),
        compiler_params=_params("parallel", "parallel", "parallel", "arbitrary"),
    )(a, bc)


def _shift_down(x, s, row, fill):
    return jnp.where(row >= s, pltpu.roll(x, s, 0), fill)


def _shift_up(x, s, row, fill):
    n = x.shape[0]
    return jnp.where(row < n - s, pltpu.roll(x, n - s, 0), fill)


def _scan_down(a, b, row):
    s = 1
    while s < a.shape[0]:
        b = a * _shift_down(b, s, row, 0.0) + b
        a = a * _shift_down(a, s, row, 1.0)
        s *= 2
    return b


def _scan_up(a, b, row):
    s = 1
    while s < a.shape[0]:
        b = a * _shift_up(b, s, row, 0.0) + b
        a = a * _shift_up(a, s, row, 1.0)
        s *= 2
    return b


def _conv(xl, cw_ref, cb_ref, row):
    y = cb_ref[...] + _shift_down(xl, 3, row, 0.0) * cw_ref[0:1, :]
    y = y + _shift_down(xl, 2, row, 0.0) * cw_ref[1:2, :]
    y = y + _shift_down(xl, 1, row, 0.0) * cw_ref[2:3, :]
    return y + xl * cw_ref[3:4, :]


def _lru_gates(xc, wa_ref, ba_ref, wx_ref, bx_ref, lam_ref):
    ra = _sigmoid(_dot(xc, wa_ref[...]) + ba_ref[...])
    ri = _sigmoid(_dot(xc, wx_ref[...]) + bx_ref[...])
    ls = jax.nn.log_sigmoid(lam_ref[...])
    a = jnp.exp((RG_LRU_C * ra) * ls)
    mult = jnp.sqrt(1.0 - a * a)
    return ra, ri, ls, a, mult


def _lru_specs(S):
    col = lambda off: pl.BlockSpec((S, LANES), lambda j: (0, off + j))
    vec = pl.BlockSpec((1, LANES), lambda j: (0, j))
    blk = pl.BlockSpec((None, LANES, LANES), lambda j: (j, 0, 0))
    cw = pl.BlockSpec((CONV_WIDTH, LANES), lambda j: (0, j))
    return col, vec, blk, cw


def _lru_fwd(proj, cw, cb, wa, ba, wx, bx, lam):
    S = proj.shape[0]
    W = cb.shape[1]
    nb = W // LANES

    def body(xl_ref, gl_ref, cw_ref, cb_ref, wa_ref, ba_ref, wx_ref, bx_ref, lam_ref, y_ref):
        row = lax.broadcasted_iota(jnp.int32, (S, LANES), 0)
        xc = _conv(xl_ref[...], cw_ref, cb_ref, row)
        _, ri, _, a, mult = _lru_gates(xc, wa_ref, ba_ref, wx_ref, bx_ref, lam_ref)
        h = _scan_down(a, mult * (ri * xc), row)
        y_ref[...] = h * _gelu(gl_ref[...])

    col, vec, blk, cws = _lru_specs(S)
    return pl.pallas_call(
        body, name="lru_fwd", grid=(nb,),
        in_specs=[col(0), col(nb), cws, vec, blk, vec, blk, vec, vec],
        out_specs=pl.BlockSpec((S, LANES), lambda j: (0, j)),
        out_shape=SDS((S, W), F32), compiler_params=_params("parallel"),
    )(proj, proj, cw, cb, wa, ba, wx, bx, lam)


def _lru_bwd(proj, dy, cw, cb, wa, ba, wx, bx, lam, wat, wxt):
    S = proj.shape[0]
    W = cb.shape[1]
    nb = W // LANES

    def body(xl_ref, gl_ref, dy_ref, cw_ref, cb_ref, wa_ref, ba_ref, wx_ref, bx_ref, lam_ref, wat_ref, wxt_ref,
             dp_ref, dwa_ref, dwx_ref, vec_ref):
        row = lax.broadcasted_iota(jnp.int32, (S, LANES), 0)
        xl = xl_ref[...]
        xc = _conv(xl, cw_ref, cb_ref, row)
        ra, ri, ls, a, mult = _lru_gates(xc, wa_ref, ba_ref, wx_ref, bx_ref, lam_ref)
        h = _scan_down(a, mult * (ri * xc), row)
        gl = gl_ref[...]
        dyv = dy_ref[...]
        dp_ref[1] = (dyv * h * _gelu_grad(gl)).astype(ACT_DTYPE)
        adj = _scan_up(_shift_up(a, 1, row, 0.0), dyv * _gelu(gl), row)
        da = adj * _shift_down(h, 1, row, 0.0)
        dmult = adj * (ri * xc)
        dlog_a = da * a - dmult * (a * a) / mult
        dra = dlog_a * (RG_LRU_C * ls)
        dpa = dra * ra * (1.0 - ra)
        dpi = (adj * mult * xc) * ri * (1.0 - ri)
        dxc = adj * mult * ri + _dot(dpa, wat_ref[...]) + _dot(dpi, wxt_ref[...])
        dwa_ref[...] = _dot_tn(xc, dpa)
        dwx_ref[...] = _dot_tn(xc, dpi)
        dxl = dxc * cw_ref[3:4, :]
        dxl = dxl + _shift_up(dxc, 1, row, 0.0) * cw_ref[2:3, :]
        dxl = dxl + _shift_up(dxc, 2, row, 0.0) * cw_ref[1:2, :]
        dxl = dxl + _shift_up(dxc, 3, row, 0.0) * cw_ref[0:1, :]
        dp_ref[0] = dxl.astype(ACT_DTYPE)
        vec_ref[...] = jnp.zeros_like(vec_ref)
        vec_ref[0:1, :] = _rowsum(dpa)
        vec_ref[1:2, :] = _rowsum(dpi)
        vec_ref[2:3, :] = _rowsum(dlog_a * (RG_LRU_C * ra)) * _sigmoid(-lam_ref[...])
        vec_ref[3:4, :] = _rowsum(dxc)
        vec_ref[4:5, :] = _rowsum(dxc * _shift_down(xl, 3, row, 0.0))
        vec_ref[5:6, :] = _rowsum(dxc * _shift_down(xl, 2, row, 0.0))
        vec_ref[6:7, :] = _rowsum(dxc * _shift_down(xl, 1, row, 0.0))
        vec_ref[7:8, :] = _rowsum(dxc * xl)

    col, vec, blk, cws = _lru_specs(S)
    return pl.pallas_call(
        body, name="lru_bwd", grid=(nb,),
        in_specs=[col(0), col(nb), col(0), cws, vec, blk, vec, blk, vec, vec, blk, blk],
        out_specs=[pl.BlockSpec((2, S, LANES), lambda j: (0, 0, j)), blk, blk, pl.BlockSpec((2 * SUBLANES, LANES), lambda j: (0, j))],
        out_shape=[SDS((2, S, W), ACT_DTYPE), SDS((nb, LANES, LANES), F32), SDS((nb, LANES, LANES), F32), SDS((2 * SUBLANES, W), F32)],
        compiler_params=_params("parallel"),
    )(proj, proj, dy, cw, cb, wa, ba, wx, bx, lam, wat, wxt)


def _seg_mean(x, seg_ref, width):
    hi = x.astype(jnp.bfloat16)
    lo = (x - hi.astype(F32)).astype(jnp.bfloat16)
    ones = seg_ref[...]
    s = jnp.dot(hi, ones, preferred_element_type=F32) + jnp.dot(lo, ones, preferred_element_type=F32)
    return s * (1.0 / width)


def _gmlp_core(u_ref, v_ref, gv_ref, seg_ref, ws_ref, bfull_ref, z_scr, hd):
    tm, W = u_ref.shape
    lane = lax.broadcasted_iota(jnp.int32, (CHUNK, LANES), 1)
    ug = _gelu(u_ref[...])
    vg = _gelu(v_ref[...])
    cen = vg - _seg_mean(vg, seg_ref, hd)
    rstd = lax.rsqrt(_seg_mean(cen * cen, seg_ref, hd) + EPS)
    vhat = cen * rstd
    vh = vhat * gv_ref[...]
    vcats = {}
    for ci in range(tm // CHUNK):
        for p in range(W // LANES):
            blk = vh[ci * CHUNK:(ci + 1) * CHUNK, p * LANES:(p + 1) * LANES]
            vcat = jnp.concatenate([jnp.where(lane < hd, blk, 0.0), jnp.where(lane >= hd, blk, 0.0)], axis=0).astype(MXU_DTYPE)
            vcats[ci, p] = vcat
            z_scr[ci * CHUNK:(ci + 1) * CHUNK, p * LANES:(p + 1) * LANES] = (
                jnp.dot(ws_ref[p], vcat, preferred_element_type=F32) + bfull_ref[:, p * LANES:(p + 1) * LANES])
    return ug, vhat, rstd, vcats


def _gmlp_specs(tm, W, nb):
    rows = lambda off: pl.BlockSpec((tm, W), lambda i: (i, off))
    vec = pl.BlockSpec((1, W), lambda i: (0, 0))
    seg = pl.BlockSpec((W, W), lambda i: (0, 0))
    wsp = pl.BlockSpec((nb, CHUNK, 2 * CHUNK), lambda i: (0, 0, 0))
    bfull = pl.BlockSpec((CHUNK, W), lambda i: (0, 0))
    return rows, vec, seg, wsp, bfull


def _gmlp_fwd(proj, ylru, gv, seg, wsp, bfull, g_lru, g_gm):
    S, W = ylru.shape
    nb = W // LANES
    hd = W // HEADS
    tm = _tile(S, 512)

    def body(u_ref, v_ref, yl_ref, gv_ref, seg_ref, ws_ref, bfull_ref, gl_ref, gg_ref, yn_ref, ygm_ref, z_scr):
        ug, _, _, _ = _gmlp_core(u_ref, v_ref, gv_ref, seg_ref, ws_ref, bfull_ref, z_scr, hd)
        ygm = ug * z_scr[...]
        ygm_ref[...] = ygm
        yl = yl_ref[...]
        yn_ref[:, 0:W] = (yl * _rsqrt_ms(yl) * gl_ref[...]).astype(ACT_DTYPE)
        yn_ref[:, W:2 * W] = (ygm * _rsqrt_ms(ygm) * gg_ref[...]).astype(ACT_DTYPE)

    rows, vec, segs, wsps, bfulls = _gmlp_specs(tm, W, nb)
    return pl.pallas_call(
        body, name="gmlp_fwd", grid=(S // tm,),
        in_specs=[rows(2), rows(3), rows(0), vec, segs, wsps, bfulls, vec, vec],
        out_specs=[pl.BlockSpec((tm, 2 * W), lambda i: (i, 0)), rows(0)],
        out_shape=[SDS((S, 2 * W), ACT_DTYPE), SDS((S, W), F32)],
        scratch_shapes=[pltpu.VMEM((tm, W), F32)],
        compiler_params=_params("parallel"),
    )(proj, proj, ylru, gv, seg, wsp, bfull, g_lru, g_gm)


def _rms_bwd(y, g, dyn):
    r = _rsqrt_ms(y)
    yhat = y * r
    dyh = dyn * g
    return r * (dyh - yhat * jnp.mean(dyh * yhat, axis=-1, keepdims=True)), _rowsum(dyn * yhat)


def _gmlp_bwd(proj, ylru, ygm, dyn, gv, seg, wsp, wspt, bfull, g_lru, g_gm):
    S, W = ylru.shape
    nb = W // LANES
    hd = W // HEADS
    tm = _tile(S, 256)

    def body(u_ref, v_ref, yl_ref, ygm_ref, dl_ref, dg_ref, gv_ref, seg_ref, ws_ref, wst_ref, bfull_ref, gl_ref, gg_ref,
             dyl_ref, duv_ref, dws_ref, dbf_ref, acc_ref, z_scr, dvh_scr):
        @pl.when(pl.program_id(0) == 0)
        def _():
            dws_ref[...] = jnp.zeros_like(dws_ref)
            dbf_ref[...] = jnp.zeros_like(dbf_ref)
            acc_ref[...] = jnp.zeros_like(acc_ref)

        dyl, dgl = _rms_bwd(yl_ref[...], gl_ref[...], dl_ref[...])
        dyl_ref[...] = dyl
        dygm, dgg = _rms_bwd(ygm_ref[...], gg_ref[...], dg_ref[...])
        ug, vhat, rstd, vcats = _gmlp_core(u_ref, v_ref, gv_ref, seg_ref, ws_ref, bfull_ref, z_scr, hd)
        duv_ref[0] = (dygm * z_scr[...] * _gelu_grad(u_ref[...])).astype(ACT_DTYPE)
        dz = dygm * ug
        lane = lax.broadcasted_iota(jnp.int32, (CHUNK, LANES), 1)
        dbf = dz[0:CHUNK, :]
        for ci in range(1, tm // CHUNK):
            dbf += dz[ci * CHUNK:(ci + 1) * CHUNK, :]
        dbf_ref[...] += dbf
        for ci in range(tm // CHUNK):
            for p in range(nb):
                dzb = dz[ci * CHUNK:(ci + 1) * CHUNK, p * LANES:(p + 1) * LANES].astype(MXU_DTYPE)
                dws_ref[p] += _dot_nt(dzb, vcats[ci, p])
                dvc = jnp.dot(wst_ref[p], dzb, preferred_element_type=F32)
                dvh_scr[ci * CHUNK:(ci + 1) * CHUNK, p * LANES:(p + 1) * LANES] = jnp.where(lane < hd, dvc[0:CHUNK], dvc[CHUNK:2 * CHUNK])
        dvh = dvh_scr[...]
        dvn = dvh * gv_ref[...]
        dvg = rstd * (dvn - _seg_mean(dvn, seg_ref, hd) - vhat * _seg_mean(dvn * vhat, seg_ref, hd))
        duv_ref[1] = (dvg * _gelu_grad(v_ref[...])).astype(ACT_DTYPE)
        acc_ref[0:1, :] += dgl
        acc_ref[1:2, :] += dgg
        acc_ref[2:3, :] += _rowsum(dvh * vhat)

    rows, vec, segs, wsps, bfulls = _gmlp_specs(tm, W, nb)
    wspt_spec = pl.BlockSpec((nb, 2 * CHUNK, CHUNK), lambda i: (0, 0, 0))
    return pl.pallas_call(
        body, name="gmlp_bwd", grid=(S // tm,),
        in_specs=[rows(2), rows(3), rows(0), rows(0), rows(0), rows(1), vec, segs, wsps, wspt_spec, bfulls, vec, vec],
        out_specs=[rows(0), pl.BlockSpec((2, tm, W), lambda i: (0, i, 0)), wsps, bfulls, pl.BlockSpec((SUBLANES, W), lambda i: (0, 0))],
        out_shape=[SDS((S, W), F32), SDS((2, S, W), ACT_DTYPE), SDS((nb, CHUNK, 2 * CHUNK), F32), SDS((CHUNK, W), F32), SDS((SUBLANES, W), F32)],
        scratch_shapes=[pltpu.VMEM((tm, W), F32), pltpu.VMEM((tm, W), F32)],
        compiler_params=_params("arbitrary"),
    )(proj, proj, ylru, ygm, dyn, dyn, gv, seg, wsp, wspt, bfull, g_lru, g_gm)


def _ada_fwd(c_all, w_ada, b_shard):
    L, D, N = w_ada.shape
    R = c_all.shape[0]
    tn = N // 2

    def body(c_ref, w_ref, b_ref, o_ref):
        cv = c_ref[...]
        o_ref[...] = _dot(cv * _sigmoid(cv), w_ref[...]) + b_ref[...]

    return pl.pallas_call(
        body, name="ada_fwd", grid=(L, N // tn),
        in_specs=[pl.BlockSpec((R, D), lambda l, j: (0, 0)), pl.BlockSpec((None, D, tn), lambda l, j: (l, 0, j)),
                  pl.BlockSpec((None, 1, tn), lambda l, j: (l, 0, j))],
        out_specs=pl.BlockSpec((None, R, tn), lambda l, j: (l, 0, j)),
        out_shape=SDS((L, R, N), F32), compiler_params=_params("parallel", "parallel"),
    )(c_all, w_ada, b_shard)


def _ada_grad(c_all_t, dmod):
    D, B = c_all_t.shape
    L, _, N = dmod.shape
    tn = N // 2

    def body(c_ref, d_ref, o_ref):
        cv = c_ref[...]
        sc = cv * _sigmoid(cv)
        acc = sc[:, 0:1] * d_ref[0:1, :]
        for b in range(1, B):
            acc += sc[:, b:b + 1] * d_ref[b:b + 1, :]
        o_ref[...] = acc

    return pl.pallas_call(
        body, name="ada_grad", grid=(L, N // tn),
        in_specs=[pl.BlockSpec((D, B), lambda l, j: (0, 0)), pl.BlockSpec((None, B, tn), lambda l, j: (l, 0, j))],
        out_specs=pl.BlockSpec((None, D, tn), lambda l, j: (l, 0, j)),
        out_shape=SDS((L, D, N), F32), compiler_params=_params("parallel", "parallel"),
    )(c_all_t, dmod)


def _adamw(w, g, m, v):
    R, C = w.shape
    tr = _row_tile(R, C * 4)

    def body(w_ref, g_ref, m_ref, v_ref, d_ref, mo_ref, vo_ref):
        d_ref[...], mo_ref[...], vo_ref[...] = _adam_math(w_ref[...], g_ref[...], m_ref[...], v_ref[...])

    tile = pl.BlockSpec((tr, C), lambda i: (i, 0))
    return pl.pallas_call(
        body, name=f"adamw_r{R}c{C}", grid=(R // tr,), in_specs=[tile] * 4, out_specs=[tile] * 3,
        out_shape=[SDS((R, C), F32)] * 3, compiler_params=_params("parallel"),
    )(w, g, m, v)


def _adam_math(w, g, m, v):
    mn = ADAM_B1 * m + (1.0 - ADAM_B1) * g
    vn = ADAM_B2 * v + (1.0 - ADAM_B2) * (g * g)
    m_hat = mn / (1.0 - ADAM_B1 ** ADAM_STEP)
    v_hat = vn / (1.0 - ADAM_B2 ** ADAM_STEP)
    return -ADAM_LR * (m_hat / (jnp.sqrt(v_hat) + ADAM_EPS) + ADAM_WD * w), mn, vn


def _adamw_layer(w, g, m, v, l, prev, after):
    L, R, C = w.shape
    tr = _row_tile(R, C * 4)
    prev = (after,) + (() if prev is None else tuple(prev))

    def body(w_ref, g_ref, m_ref, v_ref, *rest):
        go_ref, d_ref, mo_ref, vo_ref = rest[len(prev):]
        gv = g_ref[...]
        go_ref[...] = gv
        d_ref[...], mo_ref[...], vo_ref[...] = _adam_math(w_ref[...], gv, m_ref[...], v_ref[...])

    lay = pl.BlockSpec((None, tr, C), lambda i: (l, i, 0))
    return pl.pallas_call(
        body, name=f"adamw_layer_r{R}c{C}", grid=(R // tr,),
        in_specs=[lay, pl.BlockSpec((tr, C), lambda i: (i, 0)), lay, lay] + [ANY] * len(prev), out_specs=[lay] * 4,
        out_shape=[SDS((L, R, C), F32)] * 4, input_output_aliases={5 + k: k for k in range(len(prev) - 1)},
        compiler_params=_params("parallel"),
    )(w, g, m, v, *prev)


def _sum_leading(a):
    P, R, C = a.shape
    tr = _row_tile(R, P * C * 4)

    def body(a_ref, o_ref):
        acc = a_ref[0]
        for p in range(1, P):
            acc = acc + a_ref[p]
        o_ref[...] = acc

    return pl.pallas_call(
        body, name=f"sum{P}_r{R}c{C}", grid=(R // tr,),
        in_specs=[pl.BlockSpec((P, tr, C), lambda i: (0, i, 0))],
        out_specs=pl.BlockSpec((tr, C), lambda i: (i, 0)),
        out_shape=SDS((R, C), F32), compiler_params=_params("parallel"),
    )(a)


def _add_half(g4, r1, place):
    _, _, R, C = g4.shape
    tr = _row_tile(R, C * 4)

    def body(place_ref, g_ref, r_ref, h_ref, own_ref):
        s = (g_ref[...] + r_ref[...]).astype(XFER_DTYPE)
        h_ref[...] = s

        @pl.when(pl.program_id(1) == place_ref[1])
        def _():
            own_ref[...] = s

    return pl.pallas_call(
        body, name=f"add_half_r{R}c{C}",
        grid_spec=pltpu.PrefetchScalarGridSpec(
            num_scalar_prefetch=1, grid=(R // tr, N_CHIPS),
            in_specs=[pl.BlockSpec((None, None, tr, C), lambda i, p, place_ref: (p, place_ref[0], i, 0)),
                      pl.BlockSpec((None, tr, C), lambda i, p, place_ref: (p, i, 0))],
            out_specs=[pl.BlockSpec((None, tr, C), lambda i, p, place_ref: (p, i, 0)),
                       pl.BlockSpec((None, tr, C), lambda i, p, place_ref: (place_ref[1], i, 0))],
        ),
        out_shape=[SDS((N_CHIPS, R, C), XFER_DTYPE)] * 2, compiler_params=_params("parallel", "arbitrary"),
    )(place, g4, r1)


def _sum4_into_half(r2, place):
    P, R, C = r2.shape
    tr = _row_tile(R, P * C * 4)

    def body(place_ref, a_ref, o_ref):
        acc = a_ref[0].astype(F32)
        for p in range(1, P):
            acc = acc + a_ref[p].astype(F32)
        o_ref[...] = acc

    return pl.pallas_call(
        body, name=f"sum4_r{R}c{C}",
        grid_spec=pltpu.PrefetchScalarGridSpec(
            num_scalar_prefetch=1, grid=(R // tr,),
            in_specs=[pl.BlockSpec((P, tr, C), lambda i, place_ref: (0, i, 0))],
            out_specs=pl.BlockSpec((None, tr, C), lambda i, place_ref: (place_ref[0], i, 0)),
        ),
        out_shape=SDS((2, R, C), F32), compiler_params=_params("parallel"),
    )(place, r2)


def _cast_into_slot(w, l, place, after):
    _, R, C = w.shape
    tr = _row_tile(R, C * 4)

    def body(place_ref, w_ref, after_ref, o_ref):
        o_ref[...] = w_ref[...].astype(MXU_DTYPE)

    return pl.pallas_call(
        body, name=f"cast_r{R}c{C}",
        grid_spec=pltpu.PrefetchScalarGridSpec(
            num_scalar_prefetch=1, grid=(R // tr,),
            in_specs=[pl.BlockSpec((None, tr, C), lambda i, place_ref: (l, i, 0)), ANY],
            out_specs=pl.BlockSpec((None, tr, C), lambda i, place_ref: (place_ref[1], i, 0)),
        ),
        out_shape=SDS((N_CHIPS, R, C), MXU_DTYPE), compiler_params=_params("parallel"),
    )(place, w, after)


def _place():
    x, y, c = lax.axis_index("x"), lax.axis_index("y"), lax.axis_index("c")
    chips = [(1 - x, y), (x, 1 - y), (1 - x, 1 - y)]
    return x, y, c, chips


def _remote(src, dst, send_sem, recv_sem, to):
    return pltpu.make_async_remote_copy(src_ref=src, dst_ref=dst, send_sem=send_sem, recv_sem=recv_sem, device_id=to, device_id_type=MESH)


def _all_gather8(v):
    R, N = v.shape

    def body(v_ref, out_ref, send_sems, recv_sems, local_sem):
        x, y, c, chips = _place()
        me, sibling = (x, y, c), (x, y, 1 - c)

        def slot(px, py, pc):
            return out_ref.at[4 * px + 2 * py + pc]

        def copy(k, block, to, src=None):
            return _remote(slot(*block) if src is None else src, slot(*block), send_sems.at[k], recv_sems.at[k], to)

        mine = pltpu.make_async_copy(v_ref, slot(*me), local_sem)
        mine.start()
        first = [copy(0, me, sibling, src=v_ref)] + [copy(1 + j, me, (*chip, c), src=v_ref) for j, chip in enumerate(chips)]
        for cp in first:
            cp.start()
        passed = [copy(4 + j, (*chip, c), sibling) for j, chip in enumerate(chips)]
        for j, chip in enumerate(chips):
            copy(1 + j, (*chip, c), me).wait_recv()
            passed[j].start()
        copy(0, sibling, me).wait_recv()
        for j, chip in enumerate(chips):
            copy(4 + j, (*chip, 1 - c), me).wait_recv()
        for cp in first + passed:
            cp.wait_send()
        mine.wait()

    return pl.pallas_call(
        body, name=f"all_gather8_r{R}n{N}", out_shape=SDS((N_DEV, R, N), v.dtype), in_specs=[ANY], out_specs=ANY,
        scratch_shapes=[pltpu.SemaphoreType.DMA((7,)), pltpu.SemaphoreType.DMA((7,)), pltpu.SemaphoreType.DMA],
    )(v)


def _gather_weights(slots):
    n = len(slots)

    def body(*refs):
        ins, outs = refs[:n], refs[n:2 * n]
        send_sems, recv_sems = refs[2 * n:]
        x, y, c, chips = _place()
        q = 2 * x + y
        sibling = (x, y, 1 - c)
        first = []
        for k in range(n):
            for j, chip in enumerate(chips):
                first.append(_remote(ins[k].at[q, c], outs[k].at[q, c], send_sems.at[k, j], recv_sems.at[k, j], (*chip, c)))
                first[-1].start()
        passed = []
        for k in range(n):
            for j, chip in enumerate(chips):
                half = outs[k].at[2 * chip[0] + chip[1], c]
                _remote(half, half, send_sems.at[k, j], recv_sems.at[k, j], sibling).wait_recv()
                passed.append(_remote(half, half, send_sems.at[k, 3 + j], recv_sems.at[k, 3 + j], sibling))
                passed[-1].start()
        for k in range(n):
            for j, chip in enumerate(chips):
                half = outs[k].at[2 * chip[0] + chip[1], 1 - c]
                _remote(half, half, send_sems.at[k, 3 + j], recv_sems.at[k, 3 + j], sibling).wait_recv()
        for cp in first + passed:
            cp.wait_send()

    return pl.pallas_call(
        body, name="gather_weights", out_shape=[SDS(s.shape, s.dtype) for s in slots],
        in_specs=[ANY] * n, out_specs=[ANY] * n, input_output_aliases={k: k for k in range(n)},
        scratch_shapes=[pltpu.SemaphoreType.DMA((n, 6)), pltpu.SemaphoreType.DMA((n, 6))],
    )(*slots)


def _swap_halves(g4s):
    n = len(g4s)

    def body(*refs):
        ins, outs = refs[:n], refs[n:2 * n]
        send_sems, recv_sems = refs[2 * n:]
        x, y, c, _ = _place()
        sibling = (x, y, 1 - c)
        for k in range(n):
            for p in range(N_CHIPS):
                _remote(ins[k].at[p, 1 - c], outs[k].at[p], send_sems.at[k], recv_sems.at[k], sibling).start()
        for k in range(n):
            _remote(outs[k], outs[k], send_sems.at[k], recv_sems.at[k], sibling).wait()

    return pl.pallas_call(
        body, name="swap_halves", out_shape=[SDS((N_CHIPS,) + g.shape[2:], g.dtype) for g in g4s],
        in_specs=[ANY] * n, out_specs=[ANY] * n,
        scratch_shapes=[pltpu.SemaphoreType.DMA((n,)), pltpu.SemaphoreType.DMA((n,))],
    )(*g4s)


def _scatter_regions(hs, lands):
    n = len(hs)

    def body(*refs):
        ins, outs = refs[:n], refs[2 * n:3 * n]
        send_sems, recv_sems = refs[3 * n:]
        x, y, c, chips = _place()
        q = 2 * x + y
        sent = []
        for k in range(n):
            for j, chip in enumerate(chips):
                sent.append(_remote(ins[k].at[2 * chip[0] + chip[1]], outs[k].at[q], send_sems.at[k, j], recv_sems.at[k, j], (*chip, c)))
                sent[-1].start()
        for k in range(n):
            for j, chip in enumerate(chips):
                got = outs[k].at[2 * chip[0] + chip[1]]
                _remote(got, got, send_sems.at[k, j], recv_sems.at[k, j], (x, y, c)).wait_recv()
        for cp in sent:
            cp.wait_send()

    return pl.pallas_call(
        body, name="scatter_regions", out_shape=[SDS(h.shape, h.dtype) for h in lands],
        in_specs=[ANY] * (2 * n), out_specs=[ANY] * n, input_output_aliases={n + k: k for k in range(n)},
        scratch_shapes=[pltpu.SemaphoreType.DMA((n, 3)), pltpu.SemaphoreType.DMA((n, 3))],
    )(*hs, *lands)


def _share_halves(fins):
    n = len(fins)

    def body(*refs):
        ins, outs = refs[:n], refs[n:2 * n]
        send_sems, recv_sems = refs[2 * n:]
        x, y, c, _ = _place()
        sibling = (x, y, 1 - c)
        sent = [_remote(ins[k].at[c], outs[k].at[c], send_sems.at[k], recv_sems.at[k], sibling) for k in range(n)]
        for cp in sent:
            cp.start()
        for k in range(n):
            got = outs[k].at[1 - c]
            _remote(got, got, send_sems.at[k], recv_sems.at[k], sibling).wait_recv()
        for cp in sent:
            cp.wait_send()

    return pl.pallas_call(
        body, name="share_halves", out_shape=[SDS(t.shape, t.dtype) for t in fins],
        in_specs=[ANY] * n, out_specs=[ANY] * n, input_output_aliases={k: k for k in range(n)},
        scratch_shapes=[pltpu.SemaphoreType.DMA((n,)), pltpu.SemaphoreType.DMA((n,))],
    )(*fins)


def _chip_sums(grads, place):
    g4s = [g.reshape(N_CHIPS, 2, g.shape[1] // 2, g.shape[2]) for g in grads]
    pairs = [_add_half(g4, r1, place) for g4, r1 in zip(g4s, _swap_halves(g4s))]
    return [h for h, _ in pairs], [own for _, own in pairs]


def _reduce_finish(lands, place):
    fins = _share_halves([_sum4_into_half(r2, place) for r2 in lands])
    return [f.reshape(2 * f.shape[1], f.shape[2]) for f in fins]


def _reduce_scatter(grads, place):
    hs, lands = _chip_sums(grads, place)
    return _reduce_finish(_scatter_regions(hs, lands), place)


HBM_SPEC = pl.BlockSpec(memory_space=pltpu.HBM)
SEM_SPEC = pl.BlockSpec(memory_space=pltpu.SEMAPHORE)
DATAFLOW = pltpu.SideEffectType.DATAFLOW_SIDE_EFFECTING


def _in_hbm(a):
    return pltpu.with_memory_space_constraint(a, pltpu.HBM)


def _hbm_like(a):
    return pltpu.HBM(a.shape, a.dtype)


def _gather_ici_start(slots, thru, after):
    n = len(slots)

    def body(*refs):
        ins = refs[:n]
        send_sems, recv_sems = refs[n + 2], refs[n + 3]
        x, y, c, chips = _place()
        q = 2 * x + y
        for k in range(n):
            for j, chip in enumerate(chips):
                _remote(ins[k].at[q, c], ins[k].at[q, c], send_sems.at[3 * k + j], recv_sems.at[3 * k + j], (*chip, c)).start()

    out = pl.pallas_call(
        body, name="gather_ici_start",
        out_shape=(pltpu.SemaphoreType.DMA((3 * n,)), pltpu.SemaphoreType.DMA((3 * n,)), *[_hbm_like(s) for s in slots], _hbm_like(thru)),
        in_specs=[HBM_SPEC] * (n + 1) + [ANY], out_specs=(SEM_SPEC, SEM_SPEC, *[HBM_SPEC] * (n + 1)),
        input_output_aliases={k: 2 + k for k in range(n + 1)},
        compiler_params=pltpu.CompilerParams(has_side_effects=DATAFLOW),
    )(*[_in_hbm(s) for s in slots], _in_hbm(thru), after)
    return out[0], out[1], list(out[2:2 + n]), out[2 + n]


def _gather_ici_wait(send_sems, recv_sems, slots, after):
    n = len(slots)

    def body(*refs):
        ins = refs[:n]
        send_sems, recv_sems = refs[n], refs[n + 1]
        x, y, c, chips = _place()
        for k in range(n):
            for j, chip in enumerate(chips):
                got = ins[k].at[2 * chip[0] + chip[1], c]
                cp = _remote(got, got, send_sems.at[3 * k + j], recv_sems.at[3 * k + j], (x, y, c))
                cp.wait_send()
                cp.wait_recv()

    out = pl.pallas_call(
        body, name="gather_ici_wait", out_shape=[_hbm_like(s) for s in slots],
        in_specs=[HBM_SPEC] * n + [SEM_SPEC, SEM_SPEC, ANY], out_specs=[HBM_SPEC] * n,
        input_output_aliases={k: k for k in range(n)},
        compiler_params=pltpu.CompilerParams(has_side_effects=DATAFLOW),
    )(*slots, send_sems, recv_sems, after)
    return list(out)


def _gather_pass_on(slots):
    n = len(slots)

    def body(*refs):
        ins, outs = refs[:n], refs[n:2 * n]
        send_sems, recv_sems = refs[2 * n:]
        x, y, c, chips = _place()
        sibling = (x, y, 1 - c)
        passed = []
        for k in range(n):
            for j, chip in enumerate(chips):
                passed.append(_remote(ins[k].at[2 * chip[0] + chip[1], c], outs[k].at[2 * chip[0] + chip[1], c],
                                      send_sems.at[k, j], recv_sems.at[k, j], sibling))
                passed[-1].start()
        for k in range(n):
            for j, chip in enumerate(chips):
                half = outs[k].at[2 * chip[0] + chip[1], 1 - c]
                _remote(half, half, send_sems.at[k, j], recv_sems.at[k, j], sibling).wait_recv()
        for cp in passed:
            cp.wait_send()

    return pl.pallas_call(
        body, name="gather_pass_on", out_shape=[SDS(s.shape, s.dtype) for s in slots],
        in_specs=[ANY] * n, out_specs=[ANY] * n, input_output_aliases={k: k for k in range(n)},
        scratch_shapes=[pltpu.SemaphoreType.DMA((n, 3)), pltpu.SemaphoreType.DMA((n, 3))],
    )(*slots)


def _scatter_start(hs, lands, thru):
    n = len(hs)

    def body(*refs):
        ins, zones = refs[:n], refs[n:2 * n]
        send_sems, recv_sems = refs[2 * n + 1], refs[2 * n + 2]
        x, y, c, chips = _place()
        q = 2 * x + y
        for k in range(n):
            for j, chip in enumerate(chips):
                _remote(ins[k].at[2 * chip[0] + chip[1]], zones[k].at[q], send_sems.at[3 * k + j], recv_sems.at[3 * k + j], (*chip, c)).start()

    arrays = [*hs, *lands, thru]
    out = pl.pallas_call(
        body, name="scatter_start",
        out_shape=(pltpu.SemaphoreType.DMA((3 * n,)), pltpu.SemaphoreType.DMA((3 * n,)), *[_hbm_like(a) for a in arrays]),
        in_specs=[HBM_SPEC] * len(arrays), out_specs=(SEM_SPEC, SEM_SPEC, *[HBM_SPEC] * len(arrays)),
        input_output_aliases={k: 2 + k for k in range(len(arrays))},
        compiler_params=pltpu.CompilerParams(has_side_effects=DATAFLOW),
    )(*[_in_hbm(a) for a in arrays])
    return out[0], out[1], list(out[2:2 + n]), list(out[2 + n:2 + 2 * n]), out[2 + 2 * n]


def _scatter_wait(send_sems, recv_sems, hs, lands, after):
    n = len(hs)

    def body(*refs):
        ins, zones = refs[:n], refs[n:2 * n]
        send_sems, recv_sems = refs[2 * n], refs[2 * n + 1]
        x, y, c, chips = _place()
        for k in range(n):
            for j, chip in enumerate(chips):
                p = 2 * chip[0] + chip[1]
                cp = _remote(ins[k].at[p], zones[k].at[p], send_sems.at[3 * k + j], recv_sems.at[3 * k + j], (x, y, c))
                cp.wait_send()
                cp.wait_recv()

    out = pl.pallas_call(
        body, name="scatter_wait", out_shape=[_hbm_like(a) for a in [*hs, *lands]],
        in_specs=[HBM_SPEC] * (2 * n) + [SEM_SPEC, SEM_SPEC, ANY], out_specs=[HBM_SPEC] * (2 * n),
        input_output_aliases={k: k for k in range(2 * n)},
        compiler_params=pltpu.CompilerParams(has_side_effects=DATAFLOW),
    )(*hs, *lands, send_sems, recv_sems, after)
    return list(out[n:])


def _split_start(name, arrays, n_sems, issue, extra=()):
    m = len(arrays)

    def body(*refs):
        issue(refs[:m], refs[m + len(extra)], refs[m + len(extra) + 1])

    out = pl.pallas_call(
        body, name=name,
        out_shape=(pltpu.SemaphoreType.DMA((n_sems,)), pltpu.SemaphoreType.DMA((n_sems,)), *[_hbm_like(a) for a in arrays]),
        in_specs=[HBM_SPEC] * m + [ANY] * len(extra), out_specs=(SEM_SPEC, SEM_SPEC, *[HBM_SPEC] * m),
        input_output_aliases={k: 2 + k for k in range(m)},
        compiler_params=pltpu.CompilerParams(has_side_effects=DATAFLOW),
    )(*[_in_hbm(a) for a in arrays], *extra)
    return out[0], out[1], list(out[2:])


def _split_wait(name, send_sems, recv_sems, arrays, after, drain):
    m = len(arrays)

    def body(*refs):
        drain(refs[:m], refs[m], refs[m + 1])

    out = pl.pallas_call(
        body, name=name, out_shape=[_hbm_like(a) for a in arrays],
        in_specs=[HBM_SPEC] * m + [SEM_SPEC, SEM_SPEC, ANY], out_specs=[HBM_SPEC] * m,
        input_output_aliases={k: k for k in range(m)},
        compiler_params=pltpu.CompilerParams(has_side_effects=DATAFLOW),
    )(*arrays, send_sems, recv_sems, after)
    return list(out)


def _wait_both(cp):
    cp.wait_send()
    cp.wait_recv()


class _Flight:
    def __init__(self, name, arrays, n_sems, issue, drain, thru, extra=()):
        self.name, self.drain, self.n = name, drain, len(arrays)
        self.send, self.recv, out = _split_start(name + "_start", [*arrays, thru], n_sems, issue, extra)
        self.arrays, self.thru = out[:-1], out[-1]

    def land(self, after):
        return _split_wait(self.name + "_wait", self.send, self.recv, self.arrays, after, self.drain)


def _gather_flight(tag, ici, d2d, direct, thru):
    kinds = ["ici"] * len(ici) + ["d2d"] * len(d2d) + ["direct"] * len(direct)

    def issue(refs, send_sems, recv_sems):
        x, y, c, chips = _place()
        q = 2 * x + y
        for k, kind in enumerate(kinds):
            for j, chip in enumerate(chips):
                if kind == "ici":
                    src, to = refs[k].at[q, c], (*chip, c)
                elif kind == "d2d":
                    src, to = refs[k].at[2 * chip[0] + chip[1], c], (x, y, 1 - c)
                else:
                    src, to = refs[k].at[q], (*chip, c)
                _remote(src, src, send_sems.at[3 * k + j], recv_sems.at[3 * k + j], to).start()

    def drain(refs, send_sems, recv_sems):
        x, y, c, chips = _place()
        for k, kind in enumerate(kinds):
            for j, chip in enumerate(chips):
                p = 2 * chip[0] + chip[1]
                got = refs[k].at[p] if kind == "direct" else refs[k].at[p, c if kind == "ici" else 1 - c]
                _wait_both(_remote(got, got, send_sems.at[3 * k + j], recv_sems.at[3 * k + j], (x, y, c)))

    return _Flight(f"gather{tag}", [*ici, *d2d, *direct], 3 * len(kinds), issue, drain, thru)


def _swap_flight(tag, g4s, thru):
    n = len(g4s)
    zones = [lax.empty((N_CHIPS,) + g.shape[2:], g.dtype) for g in g4s]

    def issue(refs, send_sems, recv_sems):
        x, y, c, _ = _place()
        for k in range(n):
            for p in range(N_CHIPS):
                _remote(refs[k].at[p, 1 - c], refs[n + k].at[p], send_sems.at[N_CHIPS * k + p], recv_sems.at[N_CHIPS * k + p], (x, y, 1 - c)).start()

    def drain(refs, send_sems, recv_sems):
        x, y, c, _ = _place()
        for k in range(n):
            for p in range(N_CHIPS):
                got = refs[n + k].at[p]
                _wait_both(_remote(got, got, send_sems.at[N_CHIPS * k + p], recv_sems.at[N_CHIPS * k + p], (x, y, c)))

    return _Flight(f"swap{tag}", [*g4s, *zones], N_CHIPS * n, issue, drain, thru)


def _scatter_flight(tag, hs, lands, thru):
    n = len(hs)

    def issue(refs, send_sems, recv_sems):
        x, y, c, chips = _place()
        q = 2 * x + y
        for k in range(n):
            for j, chip in enumerate(chips):
                _remote(refs[k].at[2 * chip[0] + chip[1]], refs[n + k].at[q], send_sems.at[3 * k + j], recv_sems.at[3 * k + j], (*chip, c)).start()

    def drain(refs, send_sems, recv_sems):
        x, y, c, chips = _place()
        for k in range(n):
            for j, chip in enumerate(chips):
                got = refs[n + k].at[2 * chip[0] + chip[1]]
                _wait_both(_remote(got, got, send_sems.at[3 * k + j], recv_sems.at[3 * k + j], (x, y, c)))

    return _Flight(f"scatter{tag}", [*hs, *lands], 3 * n, issue, drain, thru)


def _exchange_flight(buf, thru):
    flips = [(fx, fy, fc) for fx in (0, 1) for fy in (0, 1) for fc in (0, 1)][1:]

    def peers():
        x, y, c, _ = _place()
        return (x, y, c), [((1 - x) if fx else x, (1 - y) if fy else y, (1 - c) if fc else c) for fx, fy, fc in flips]

    def slot(ref, dev):
        return ref.at[4 * dev[0] + 2 * dev[1] + dev[2]]

    def issue(refs, send_sems, recv_sems):
        me, others = peers()
        for j, to in enumerate(others):
            _remote(slot(refs[0], me), slot(refs[0], me), send_sems.at[j], recv_sems.at[j], to).start()

    def drain(refs, send_sems, recv_sems):
        me, others = peers()
        for j, frm in enumerate(others):
            got = slot(refs[0], frm)
            _wait_both(_remote(got, got, send_sems.at[j], recv_sems.at[j], me))

    return _Flight("exchange", [buf], len(flips), issue, drain, thru)


def _share_flight(tag, fins, thru):
    n = len(fins)

    def issue(refs, send_sems, recv_sems):
        x, y, c, _ = _place()
        for k in range(n):
            _remote(refs[k].at[c], refs[k].at[c], send_sems.at[k], recv_sems.at[k], (x, y, 1 - c)).start()

    def drain(refs, send_sems, recv_sems):
        x, y, c, _ = _place()
        for k in range(n):
            got = refs[k].at[1 - c]
            _wait_both(_remote(got, got, send_sems.at[k], recv_sems.at[k], (x, y, c)))

    return _Flight(f"share{tag}", fins, n, issue, drain, thru)


def _pair_blocks(w):
    h, d, _ = w.shape
    z = jnp.zeros((h // 2, d, d), w.dtype)
    return jnp.concatenate([jnp.concatenate([w[0::2], z], axis=2), jnp.concatenate([z, w[1::2]], axis=2)], axis=1)


def _unpair_blocks(b):
    n, dd, _ = b.shape
    d = dd // 2
    return jnp.stack([b[:, :d, :d], b[:, d:, d:]], axis=1).reshape(2 * n, d, d)


def _pad_rows(a, rows):
    return jnp.pad(a, ((0, rows - a.shape[0]), (0, 0)))


class _Packer:
    def __init__(self, shapes, width=1024, row_multiple=64):
        self.shapes = shapes
        self.sizes = [math.prod(s) for s in shapes]
        total = sum(self.sizes)
        self.width = width
        self.rows = -(-total // (width * row_multiple)) * row_multiple
        self.pad = self.rows * width - total

    def pack(self, arrays):
        flat = jnp.concatenate([a.reshape(-1).astype(F32) for a in arrays] + [jnp.zeros((self.pad,), F32)])
        return flat.reshape(self.rows, self.width)

    def unpack(self, packed):
        flat = packed.reshape(-1)
        out, off = [], 0
        for s, n in zip(self.shapes, self.sizes):
            out.append(flat[off:off + n].reshape(s))
            off += n
        return out


SMALL = ["b_ada", "ffn1_norm", "mix_norm", "conv_w", "conv_b", "gate_a_w", "gate_a_b", "gate_x_w", "gate_x_b", "lru_lambda",
         "v_norm", "spatial_w", "spatial_b", "lru_out_norm", "gmlp_out_norm", "ffn2_norm", "final_norm"]
BIG = ["ffn1_w_gu", "ffn1_w_down", "w_in", "w_out", "ffn2_w_gu", "ffn2_w_down"]
GROUPS = (("ffn1_w_gu", "ffn1_w_down"), ("w_in", "w_out"), ("ffn2_w_gu", "ffn2_w_down"))
FWD_GROUPS = (("ffn1_w_gu",), ("ffn1_w_down",), ("w_in", "w_out"), ("ffn2_w_gu",), ("ffn2_w_down",))
MIN_AGE = {"swap": 1, "scatter": 2, "share": 1}
WEIGHTS = ["w_ada", "b_ada", "ffn1_norm", "ffn1_w_gu", "ffn1_w_down", "mix_norm", "w_in", "conv_w", "conv_b", "gate_a_w", "gate_a_b",
           "gate_x_w", "gate_x_b", "lru_lambda", "v_norm", "spatial_w", "spatial_b", "lru_out_norm", "gmlp_out_norm", "w_out",
           "ffn2_norm", "ffn2_w_gu", "ffn2_w_down", "final_norm"]


def kernel(x, c, w_ada, b_ada, ffn1_norm, ffn1_w_gu, ffn1_w_down, mix_norm, w_in, conv_w, conv_b, gate_a_w, gate_a_b, gate_x_w, gate_x_b, lru_lambda, v_norm, spatial_w, spatial_b, lru_out_norm, gmlp_out_norm, w_out, ffn2_norm, ffn2_w_gu, ffn2_w_down, final_norm, loss_target, m_w_ada, m_b_ada, m_ffn1_norm, m_ffn1_w_gu, m_ffn1_w_down, m_mix_norm, m_w_in, m_conv_w, m_conv_b, m_gate_a_w, m_gate_a_b, m_gate_x_w, m_gate_x_b, m_lru_lambda, m_v_norm, m_spatial_w, m_spatial_b, m_lru_out_norm, m_gmlp_out_norm, m_w_out, m_ffn2_norm, m_ffn2_w_gu, m_ffn2_w_down, m_final_norm, v_w_ada, v_b_ada, v_ffn1_norm, v_ffn1_w_gu, v_ffn1_w_down, v_mix_norm, v_w_in, v_conv_w, v_conv_b, v_gate_a_w, v_gate_a_b, v_gate_x_w, v_gate_x_b, v_lru_lambda, v_v_norm, v_spatial_w, v_spatial_b, v_lru_out_norm, v_gmlp_out_norm, v_w_out, v_ffn2_norm, v_ffn2_w_gu, v_ffn2_w_down, v_final_norm):
    given = dict(locals())
    W = {n: given[n] for n in WEIGHTS}
    L = w_ada.shape[0]
    S, D = x.shape[1], x.shape[2]
    LW = conv_b.shape[1]
    hd = LW // HEADS
    xi, yi, ci = lax.axis_index("x"), lax.axis_index("y"), lax.axis_index("c")
    chip = 2 * xi + yi
    dev = 2 * chip + ci
    place = jnp.stack([ci, chip]).astype(jnp.int32)
    xs = x.reshape(S, D)
    tgt = loss_target.reshape(S, D)

    c_all = _all_gather8(_pad_rows(c, SUBLANES))[:, 0, :]
    n_ada = w_ada.shape[2]
    b_shard = lax.dynamic_slice_in_dim(b_ada, chip * n_ada, n_ada, axis=1)
    mod_shard = _ada_fwd(_pad_rows(c_all, 2 * SUBLANES), w_ada, b_shard[:, None, :])

    def in_slot(block):
        return lax.dynamic_update_index_in_dim(jnp.zeros((N_CHIPS,) + block.shape, block.dtype), block, chip, 0)

    cws = LW // N_CHIPS
    small = [in_slot(mod_shard.reshape(L * 2 * SUBLANES, n_ada)), in_slot(conv_w.reshape(L * CONV_WIDTH, cws))]

    def half_view(s):
        return s.reshape(N_CHIPS, 2, s.shape[1] // 2, s.shape[2])

    stages = [(l, names) for l in range(L) for names in FWD_GROUPS]
    seq = [[half_view(_cast_into_slot(W[n], l, place, place)) for n in names] for l, names in stages[:1]]
    flights = {}

    def launch(t, thru, direct=()):
        ici = seq[t] if t < len(seq) else []
        d2d = seq[t - 1] if 1 <= t <= len(seq) else []
        if ici or d2d or direct:
            flights[t] = _gather_flight(t, ici, d2d, list(direct), thru)
            thru = flights[t].thru
        return thru

    def land(t, after):
        if t not in flights:
            return []
        out = flights.pop(t).land(after)
        ni = len(seq[t]) if t < len(seq) else 0
        nd = len(seq[t - 1]) if 1 <= t <= len(seq) else 0
        if ni:
            seq[t] = out[:ni]
        if nd:
            seq[t - 1] = out[ni:ni + nd]
        return out[ni + nd:]

    def group_weights(t):
        return [s.reshape(N_CHIPS, -1, s.shape[3]) for s in seq[t]]

    c_all = launch(0, c_all, small)
    seq += [[half_view(_cast_into_slot(W[n], l, place, c_all)) for n in names] for l, names in stages[1:]]
    mod_all, conv_all = land(0, seq[-1][-1])
    mod_all = launch(1, mod_all)
    land(1, mod_all)
    mod_rows = lax.dynamic_index_in_dim(mod_all.reshape(N_CHIPS, L, 2 * SUBLANES, n_ada), dev, axis=2, keepdims=False)
    mod = mod_rows.transpose(1, 0, 2).reshape(L, N_MOD, 1, D)
    conv_full = conv_all.reshape(N_CHIPS, L, CONV_WIDTH, cws).transpose(1, 2, 0, 3).reshape(L, CONV_WIDTH, LW)

    tril = jnp.tril(jnp.ones((CHUNK, CHUNK), F32))
    seg = (jnp.arange(LW)[:, None] // hd == jnp.arange(LW)[None, :] // hd).astype(jnp.bfloat16)

    def mixer_params(l):
        ws = spatial_w[l] * tril
        wsp = jnp.concatenate([ws[0::2], ws[1::2]], axis=2)
        wa, wx = _pair_blocks(gate_a_w[l]), _pair_blocks(gate_x_w[l])
        return dict(
            cw=conv_full[l], cb=conv_b[l][None],
            wa=wa.astype(MXU_DTYPE), wx=wx.astype(MXU_DTYPE), wat=wa.transpose(0, 2, 1).astype(MXU_DTYPE), wxt=wx.transpose(0, 2, 1).astype(MXU_DTYPE),
            ba=gate_a_b[l].reshape(1, LW), bx=gate_x_b[l].reshape(1, LW), lam=lru_lambda[l][None], gv=v_norm[l][None],
            wsp=wsp.astype(MXU_DTYPE), wspt=wsp.transpose(0, 2, 1).astype(MXU_DTYPE),
            bfull=jnp.repeat(spatial_b[l].T, hd, axis=1), g_lru=lru_out_norm[l][None], g_gm=gmlp_out_norm[l][None])

    saved = []
    xcur = xs
    for l in range(L):
        mp, md = mixer_params(l), mod[l]
        s = dict(lw={}, mp=mp, md=md)
        lw = s["lw"]
        t = len(FWD_GROUPS) * l
        s["x0"] = xcur
        s["h1"] = launch(t + 2, _modnorm(xcur, ffn1_norm[l][None], md[0], md[1]))
        lw["gu1"], = group_weights(t)
        s["a1"], s["gu1"] = _ffn_up(s["h1"], lw["gu1"])
        land(t + 2, s["a1"])
        s["a1"] = launch(t + 3, s["a1"])
        lw["d1"] = group_weights(t + 1)[0].reshape(-1, D)
        s["f1"], xcur = _mm_res(s["a1"], lw["d1"], xcur, md[2], 0.5)
        land(t + 3, xcur)
        s["x1"] = xcur
        s["h2"] = launch(t + 4, _modnorm(xcur, mix_norm[l][None], md[3], md[4]))
        lw["win"], wout = group_weights(t + 2)
        lw["wout"] = wout.reshape(-1, D)
        s["proj"] = _mm_chunks(s["h2"], lw["win"])
        s["ylru"] = _lru_fwd(s["proj"], mp["cw"], mp["cb"], mp["wa"], mp["ba"], mp["wx"], mp["bx"], mp["lam"])
        s["yn"], s["ygm"] = _gmlp_fwd(s["proj"], s["ylru"], mp["gv"], seg, mp["wsp"], mp["bfull"], mp["g_lru"], mp["g_gm"])
        s["f2"], xcur = _mm_res(s["yn"], lw["wout"], xcur, md[5], 1.0)
        land(t + 4, xcur)
        s["x2"] = xcur
        s["h3"] = launch(t + 5, _modnorm(xcur, ffn2_norm[l][None], md[6], md[7]))
        lw["gu2"], = group_weights(t + 3)
        s["a3"], s["gu3"] = _ffn_up(s["h3"], lw["gu2"])
        land(t + 5, s["a3"])
        s["a3"] = launch(t + 6, s["a3"])
        lw["d2"] = group_weights(t + 4)[0].reshape(-1, D)
        s["f3"], xcur = _mm_res(s["a3"], lw["d2"], xcur, md[8], 0.5)
        land(t + 6, xcur)
        saved.append(s)

    dx, dq, head_acc = _loss_head(xcur, tgt, final_norm[None], saved[-1]["md"][8], 0.5)
    loss = lax.psum(jnp.sum(head_acc[1]), ("x", "y", "c"))
    small_grads = {}
    big_grads = {n: [None] * L for n in BIG}
    dmods = [None] * L
    zero_row = jnp.zeros((1, D), F32)

    def ffn_bwd(names, l, dx, dq, x_in, h, a, gu, f, wgu, wd, gn, sc, next_gate, next_scale):
        big_grads[names[1]][l] = _mm_tn_chunks(a, dq[None], 1408, 1024)[0].reshape(N_CHIPS, -1, D)
        dgu = _ffn_bwd_act(dq, wd, gu)
        C = dgu.shape[3]
        dgu4 = dgu.reshape(N_CHIPS, S, C)
        big_grads[names[0]][l] = _mm_tn_chunks(h, dgu4, 1024, C)
        dgu4 = reduce_group(names, l, big_grads[names[0]][l], dgu4)
        dh = _mm_nt_chunks(dgu4, wgu)
        dh = move_on(dh, dh)
        dx, dq, acc = _norm_bwd(x_in, dh, dx, f, gn, sc, 0.5, next_gate, next_scale)
        return dx, move_on(dx, dq), acc

    stepped = {n: None for n in BIG}
    reducing = []

    clock = [0]
    to_step = []

    def step_reduced(after):
        while to_step:
            name, l, g = to_step.pop(0)
            stepped[name] = _adamw_layer(W[name], g, given["m_" + name], given["v_" + name], l, stepped[name], after)
            after = stepped[name][1]
        return after

    def move_on(after, thru, force=False):
        clock[0] += 1
        for grp in list(reducing):
            if not force and clock[0] - grp["since"] < MIN_AGE[grp["step"]]:
                continue
            grp["since"] = clock[0]
            landed = grp["flight"].land(after)
            n = len(grp["names"])
            if grp["step"] == "swap":
                pairs = [_add_half(g4, r1, place) for g4, r1 in zip(landed[:n], landed[n:])]
                grp.update(step="scatter", flight=_scatter_flight(grp["tag"], [h for h, _ in pairs], [own for _, own in pairs], thru))
            elif grp["step"] == "scatter":
                grp.update(step="share", flight=_share_flight(grp["tag"], [_sum4_into_half(r2, place) for r2 in landed[n:]], thru))
            else:
                to_step.extend((name, grp["l"], fin.reshape(2 * fin.shape[1], fin.shape[2])) for name, fin in zip(grp["names"], landed))
                reducing.remove(grp)
                continue
            thru = grp["flight"].thru
        return thru

    def reduce_group(names, l, after, thru):
        thru = move_on(after, thru)
        g4s = [big_grads[n][l].reshape(N_CHIPS, 2, big_grads[n][l].shape[1] // 2, big_grads[n][l].shape[2]) for n in names]
        tag = f"{l}{GROUPS.index(names)}"
        reducing.append(dict(names=names, l=l, tag=tag, step="swap", since=clock[0], flight=_swap_flight(tag, g4s, thru)))
        return reducing[-1]["flight"].thru
    for l in reversed(range(L)):
        s = saved[l]
        lw, mp, md = s["lw"], s["mp"], s["md"]
        dx, dq, acc3 = ffn_bwd(
            GROUPS[2], l, dx, dq, s["x2"], s["h3"], s["a3"], s["gu3"], s["f3"], lw["gu2"], lw["d2"], ffn2_norm[l][None], md[7], md[5], 1.0)
        big_grads["w_out"][l] = _mm_tn_chunks(s["yn"], dq[None], 1024, 1024)[0].reshape(N_CHIPS, -1, D)
        dyn = _mm_nt_chunks(dq[None], lw["wout"][None])
        dylru, duv, dwsp, dbfull, gacc = _gmlp_bwd(s["proj"], s["ylru"], s["ygm"], dyn, mp["gv"], seg, mp["wsp"], mp["wspt"], mp["bfull"], mp["g_lru"], mp["g_gm"])
        dxg, dwa, dwx, lvec = _lru_bwd(s["proj"], dylru, mp["cw"], mp["cb"], mp["wa"], mp["ba"], mp["wx"], mp["bx"], mp["lam"], mp["wat"], mp["wxt"])
        dproj = jnp.concatenate([dxg, duv], axis=0)
        big_grads["w_in"][l] = _mm_tn_chunks(s["h2"], dproj, 1024, LW)
        dproj = reduce_group(GROUPS[1], l, big_grads["w_in"][l], dproj)
        dh2 = _mm_nt_chunks(dproj, lw["win"])
        dh2 = move_on(dh2, dh2)
        dx, dq, acc2 = _norm_bwd(s["x1"], dh2, dx, s["f2"], mix_norm[l][None], md[4], 1.0, md[2], 0.5)
        dq = move_on(dx, dq)
        if l > 0:
            ng, ns = saved[l - 1]["md"][8], 0.5
        else:
            ng, ns = zero_row, 0.0
        dx, dq, acc1 = ffn_bwd(
            GROUPS[0], l, dx, dq, s["x0"], s["h1"], s["a1"], s["gu1"], s["f1"], lw["gu1"], lw["d1"], ffn1_norm[l][None], md[1], ng, ns)

        dmods[l] = jnp.concatenate([acc1[0:2], acc1[3:4], acc2[0:2], acc2[3:4], acc3[0:2], acc3[3:4]], axis=0)
        dws = jnp.stack([dwsp[:, :, :CHUNK], dwsp[:, :, CHUNK:]], axis=1).reshape(HEADS, CHUNK, CHUNK) * tril
        lg = {"ffn1_norm": acc1[2], "mix_norm": acc2[2], "ffn2_norm": acc3[2],
              "conv_w": lvec[4:8], "conv_b": lvec[3], "gate_a_w": _unpair_blocks(dwa), "gate_a_b": lvec[0].reshape(HEADS, hd),
              "gate_x_w": _unpair_blocks(dwx), "gate_x_b": lvec[1].reshape(HEADS, hd), "lru_lambda": lvec[2], "v_norm": gacc[2],
              "spatial_w": dws, "spatial_b": dbfull.reshape(CHUNK, HEADS, hd).sum(-1).T, "lru_out_norm": gacc[0], "gmlp_out_norm": gacc[1]}
        for n, g in lg.items():
            small_grads.setdefault(n, [None] * L)[l] = g

    grad_x = dx.reshape(x.shape)

    per_layer = [n for n in SMALL if n not in ("b_ada", "final_norm")]
    part = [jnp.stack(small_grads[n]) for n in per_layer] + [head_acc[0], jnp.stack(dmods)]
    packer = _Packer([p.shape for p in part])
    packed = packer.pack(part)
    exchange = _exchange_flight(lax.dynamic_update_index_in_dim(jnp.zeros((N_DEV,) + packed.shape, F32), packed, dev, 0), dq)
    dq = exchange.thru
    done = step_reduced(dq)
    while reducing:
        dq = move_on(done, dq, force=True)
        done = step_reduced(dq)
    gathered, = exchange.land(done)
    summed = packer.unpack(_sum_leading(gathered))
    grads = dict(zip(per_layer + ["final_norm"], summed[:-1]))
    grads["b_ada"] = summed[-1].reshape(L, N_MOD * D)
    off = sum(packer.sizes[:-1])
    dmod_rows = gathered.reshape(N_DEV, -1)[:, off:off + L * N_MOD * D].reshape(N_DEV, L, N_MOD * D)
    dmod_shard = lax.dynamic_slice_in_dim(dmod_rows, chip * n_ada, n_ada, axis=2).transpose(1, 0, 2)
    grads["w_ada"] = _ada_grad(c_all.T, dmod_shard)
    grads["conv_w"] = lax.dynamic_slice_in_dim(grads["conv_w"], chip * cws, cws, axis=2)

    delta, new_m, new_v = {}, {}, {}
    for n in BIG:
        grads[n], delta[n], new_m[n], new_v[n] = stepped[n]
    shp = w_ada.shape
    d_, m_, v_ = _adamw(*[a.reshape(-1, shp[-1]) for a in (w_ada, grads["w_ada"], m_w_ada, v_w_ada)])
    delta["w_ada"], new_m["w_ada"], new_v["w_ada"] = d_.reshape(shp), m_.reshape(shp), v_.reshape(shp)
    spk = _Packer([W[n].shape for n in SMALL])
    d_, m_, v_ = _adamw(spk.pack([W[n] for n in SMALL]), spk.pack([grads[n] for n in SMALL]),
                        spk.pack([given["m_" + n] for n in SMALL]), spk.pack([given["v_" + n] for n in SMALL]))
    for n, a, b, e in zip(SMALL, spk.unpack(d_), spk.unpack(m_), spk.unpack(v_)):
        delta[n], new_m[n], new_v[n] = a, b, e
    grads = {n: grads[n].reshape(W[n].shape) for n in WEIGHTS}
    return (loss, grad_x, *[grads[n] for n in WEIGHTS], *[delta[n] for n in WEIGHTS], *[new_m[n] for n in WEIGHTS], *[new_v[n] for n in WEIGHTS])
```

```python
import math

import jax
import jax.numpy as jnp
from jax import lax
from jax.experimental import pallas as pl
from jax.experimental.pallas import tpu as pltpu

F32 = jnp.float32
MXU_DTYPE = jnp.bfloat16
ACT_DTYPE = jnp.bfloat16
XFER_DTYPE = jnp.bfloat16
EPS = 1e-6
RG_LRU_C = 8.0
N_MOD = 9
CONV_WIDTH = 4
HEADS = 8
CHUNK = 128
LANES = 128
SUBLANES = 8
N_CHIPS = 4
N_DEV = 8
ADAM_LR, ADAM_B1, ADAM_B2, ADAM_EPS, ADAM_WD, ADAM_STEP = 0.001, 0.9, 0.999, 1e-08, 0.01, 10
VMEM_LIMIT_BYTES = 60 * 1024 * 1024
ROW_TILE_BYTES = 2 << 20
GELU_C = math.sqrt(2.0 / math.pi)
GELU_A = 0.044715

ANY = pl.BlockSpec(memory_space=pl.ANY)
MESH = pl.DeviceIdType.MESH
SDS = jax.ShapeDtypeStruct


def _params(*sem):
    return pltpu.CompilerParams(dimension_semantics=sem, vmem_limit_bytes=VMEM_LIMIT_BYTES)


def _dot(a, b):
    return jnp.dot(a.astype(MXU_DTYPE), b.astype(MXU_DTYPE), preferred_element_type=F32)


def _dot_nt(a, b):
    return lax.dot_general(a.astype(MXU_DTYPE), b.astype(MXU_DTYPE), (((1,), (1,)), ((), ())), preferred_element_type=F32)


def _dot_tn(a, b):
    return lax.dot_general(a.astype(MXU_DTYPE), b.astype(MXU_DTYPE), (((0,), (0,)), ((), ())), preferred_element_type=F32)


def _gelu(x):
    return x * (0.5 * (1.0 + jnp.tanh(GELU_C * (x + GELU_A * (x * x * x)))))


def _gelu_grad(x):
    t = jnp.tanh(GELU_C * (x + GELU_A * (x * x * x)))
    return 0.5 * (1.0 + t) + 0.5 * x * (1.0 - t * t) * (GELU_C * (1.0 + 3.0 * GELU_A * x * x))


def _sigmoid(x):
    return jax.nn.sigmoid(x)


def _rsqrt_ms(x):
    return lax.rsqrt(jnp.mean(x * x, axis=-1, keepdims=True) + EPS)


def _rowsum(x):
    return jnp.sum(x, axis=0, keepdims=True)


def _tile(n, want):
    t = min(n, want)
    assert n % t == 0, (n, want)
    return t


def _row_tile(rows, row_bytes):
    step = 2 * SUBLANES
    cap = max(step, ROW_TILE_BYTES // row_bytes)
    best = None
    for t in range(step, min(rows, cap) + 1, step):
        if rows % t == 0:
            best = t
    assert best is not None, (rows, row_bytes)
    return best


def _modnorm(x, gn, sh, sc):
    S, D = x.shape
    tm = _tile(S, 1024)

    def body(x_ref, gn_ref, sh_ref, sc_ref, h_ref):
        xv = x_ref[...]
        h = (xv * _rsqrt_ms(xv) * gn_ref[...]) * (1.0 + sc_ref[...]) + sh_ref[...]
        h_ref[...] = h.astype(ACT_DTYPE)

    row = pl.BlockSpec((1, D), lambda i: (0, 0))
    return pl.pallas_call(
        body, name="modnorm", grid=(S // tm,),
        in_specs=[pl.BlockSpec((tm, D), lambda i: (i, 0)), row, row, row],
        out_specs=pl.BlockSpec((tm, D), lambda i: (i, 0)),
        out_shape=SDS((S, D), ACT_DTYPE), compiler_params=_params("parallel"),
    )(x, gn, sh, sc)


def _norm_bwd(x, dh, dxo, f, gn, sc, res_scale, next_gate, next_scale):
    S, D = x.shape
    tm = _tile(S, 512)

    def body(x_ref, dh_ref, dxo_ref, f_ref, gn_ref, sc_ref, ng_ref, dx_ref, dq_ref, acc_ref):
        @pl.when(pl.program_id(0) == 0)
        def _():
            acc_ref[...] = jnp.zeros_like(acc_ref)

        xv, dh, dxo = x_ref[...], dh_ref[...], dxo_ref[...]
        r = _rsqrt_ms(xv)
        xhat = xv * r
        gn = gn_ref[...]
        dn = dh * (1.0 + sc_ref[...])
        dxh = dn * gn
        dx = dxo + r * (dxh - xhat * jnp.mean(dxh * xhat, axis=-1, keepdims=True))
        dx_ref[...] = dx
        dq_ref[...] = ((next_scale * ng_ref[...]) * dx).astype(ACT_DTYPE)
        acc_ref[0:1, :] += _rowsum(dh)
        acc_ref[1:2, :] += _rowsum(dh * (xhat * gn))
        acc_ref[2:3, :] += _rowsum(dn * xhat)
        acc_ref[3:4, :] += _rowsum((res_scale * f_ref[...]) * dxo)

    tile = pl.BlockSpec((tm, D), lambda i: (i, 0))
    row = pl.BlockSpec((1, D), lambda i: (0, 0))
    return pl.pallas_call(
        body, name="norm_bwd", grid=(S // tm,),
        in_specs=[tile, tile, tile, tile, row, row, row],
        out_specs=[tile, tile, pl.BlockSpec((SUBLANES, D), lambda i: (0, 0))],
        out_shape=[SDS((S, D), F32), SDS((S, D), ACT_DTYPE), SDS((SUBLANES, D), F32)],
        compiler_params=_params("arbitrary"),
    )(x, dh, dxo, f, gn, sc, next_gate)


def _loss_head(x, target, gn, next_gate, next_scale):
    S, D = x.shape
    tm = _tile(S, 512)

    def body(x_ref, t_ref, gn_ref, ng_ref, dx_ref, dq_ref, acc_ref):
        @pl.when(pl.program_id(0) == 0)
        def _():
            acc_ref[...] = jnp.zeros_like(acc_ref)

        xv = x_ref[...]
        r = _rsqrt_ms(xv)
        xhat = xv * r
        gn = gn_ref[...]
        err = xhat * gn - t_ref[...]
        dy = err * (1.0 / D)
        dxh = dy * gn
        dx = r * (dxh - xhat * jnp.mean(dxh * xhat, axis=-1, keepdims=True))
        dx_ref[...] = dx
        dq_ref[...] = ((next_scale * ng_ref[...]) * dx).astype(ACT_DTYPE)
        acc_ref[0:1, :] += _rowsum(dy * xhat)
        acc_ref[1:2, :] += _rowsum(err * err) * (0.5 / D)

    tile = pl.BlockSpec((tm, D), lambda i: (i, 0))
    row = pl.BlockSpec((1, D), lambda i: (0, 0))
    return pl.pallas_call(
        body, name="loss_head", grid=(S // tm,),
        in_specs=[tile, tile, row, row],
        out_specs=[tile, tile, pl.BlockSpec((SUBLANES, D), lambda i: (0, 0))],
        out_shape=[SDS((S, D), F32), SDS((S, D), ACT_DTYPE), SDS((SUBLANES, D), F32)],
        compiler_params=_params("arbitrary"),
    )(x, target, gn, next_gate)


def _ffn_up(h, wgu):
    S, D = h.shape
    C = wgu.shape[2]
    tm = _tile(S, 512)

    def body(h_ref, wg_ref, wu_ref, a_ref, gu_ref):
        hv = h_ref[...]
        g = _dot(hv, wg_ref[...])
        u = _dot(hv, wu_ref[...])
        a_ref[...] = (g * _sigmoid(g) * u).astype(ACT_DTYPE)
        gu_ref[0] = g.astype(ACT_DTYPE)
        gu_ref[1] = u.astype(ACT_DTYPE)

    return pl.pallas_call(
        body, name="ffn_up", grid=(2, S // tm),
        in_specs=[
            pl.BlockSpec((tm, D), lambda j, i: (i, 0)),
            pl.BlockSpec((None, D, C), lambda j, i: (j, 0, 0)),
            pl.BlockSpec((None, D, C), lambda j, i: (2 + j, 0, 0)),
        ],
        out_specs=[
            pl.BlockSpec((tm, C), lambda j, i: (i, j)),
            pl.BlockSpec((2, None, tm, C), lambda j, i: (0, j, i, 0)),
        ],
        out_shape=[SDS((S, 2 * C), ACT_DTYPE), SDS((2, 2, S, C), ACT_DTYPE)],
        compiler_params=_params("parallel", "parallel"),
    )(h, wgu, wgu)


def _ffn_bwd_act(dq, wd, gu):
    S, D = dq.shape
    C = gu.shape[3]
    tm = _tile(S, 512)

    def body(dq_ref, wd_ref, gu_ref, dgu_ref):
        da = _dot_nt(dq_ref[...], wd_ref[...])
        g = gu_ref[0].astype(F32)
        u = gu_ref[1].astype(F32)
        s = _sigmoid(g)
        dgu_ref[0] = (da * u * (s * (1.0 + g * (1.0 - s)))).astype(ACT_DTYPE)
        dgu_ref[1] = (da * (g * s)).astype(ACT_DTYPE)

    gu_spec = pl.BlockSpec((2, None, tm, C), lambda j, i: (0, j, i, 0))
    return pl.pallas_call(
        body, name="ffn_bwd_act", grid=(2, S // tm),
        in_specs=[pl.BlockSpec((tm, D), lambda j, i: (i, 0)), pl.BlockSpec((C, D), lambda j, i: (j, 0)), gu_spec],
        out_specs=gu_spec,
        out_shape=SDS(gu.shape, ACT_DTYPE),
        compiler_params=_params("parallel", "parallel"),
    )(dq, wd, gu)


def _mm_res(a, w, x, gate, scale):
    S, K = a.shape
    D = w.shape[1]
    tm, tn = _tile(S, 512), _tile(D, 1024)

    def body(a_ref, w_ref, x_ref, g_ref, f_ref, xo_ref):
        f = _dot(a_ref[...], w_ref[...])
        f_ref[...] = f
        xo_ref[...] = x_ref[...] + (scale * g_ref[...]) * f

    tile = pl.BlockSpec((tm, tn), lambda j, i: (i, j))
    return pl.pallas_call(
        body, name=f"mm_res_k{K}", grid=(D // tn, S // tm),
        in_specs=[pl.BlockSpec((tm, K), lambda j, i: (i, 0)), pl.BlockSpec((K, tn), lambda j, i: (0, j)), tile,
                  pl.BlockSpec((1, tn), lambda j, i: (0, j))],
        out_specs=[tile, tile],
        out_shape=[SDS((S, D), F32), SDS((S, D), F32)],
        compiler_params=_params("parallel", "parallel"),
    )(a, w, x, gate)


def _mm_chunks(h, wc):
    S, K = h.shape
    P, _, N = wc.shape
    tm = _tile(S, 512)

    def body(h_ref, w_ref, o_ref):
        hv = h_ref[...]
        for p in range(P):
            o_ref[:, p * N:(p + 1) * N] = _dot(hv, w_ref[p])

    return pl.pallas_call(
        body, name="mm_chunks", grid=(S // tm,),
        in_specs=[pl.BlockSpec((tm, K), lambda i: (i, 0)), pl.BlockSpec((P, K, N), lambda i: (0, 0, 0))],
        out_specs=pl.BlockSpec((tm, P * N), lambda i: (i, 0)),
        out_shape=SDS((S, P * N), F32),
        compiler_params=_params("parallel"),
    )(h, wc)


def _mm_nt_chunks(ac, wc):
    P, S, K = ac.shape
    N = wc.shape[1]
    tm, tn = _tile(S, 512), _tile(N, 1024)

    def body(a_ref, w_ref, o_ref):
        acc = _dot_nt(a_ref[0], w_ref[0])
        for p in range(1, P):
            acc += _dot_nt(a_ref[p], w_ref[p])
        o_ref[...] = acc

    return pl.pallas_call(
        body, name=f"mm_nt_p{P}k{K}", grid=(S // tm, N // tn),
        in_specs=[pl.BlockSpec((P, tm, K), lambda i, j: (0, i, 0)), pl.BlockSpec((P, tn, K), lambda i, j: (0, j, 0))],
        out_specs=pl.BlockSpec((tm, tn), lambda i, j: (i, j)),
        out_shape=SDS((S, N), F32),
        compiler_params=_params("parallel", "parallel"),
    )(ac, wc)


def _mm_tn_chunks(a, bc, tile_m, tile_n):
    S, M = a.shape
    P, _, N = bc.shape
    ts, tm, tn = _tile(S, 2048), _tile(M, tile_m), _tile(N, tile_n)

    def body(a_ref, b_ref, o_ref):
        @pl.when(pl.program_id(3) == 0)
        def _():
            o_ref[...] = jnp.zeros_like(o_ref)

        o_ref[...] += _dot_tn(a_ref[...], b_ref[...])

    return pl.pallas_call(
        body, name=f"mm_tn_m{M}n{N}", grid=(P, M // tm, N // tn, S // ts),
        in_specs=[pl.BlockSpec((ts, tm), lambda p, m, n, k: (k, m)), pl.BlockSpec((None, ts, tn), lambda p, m, n, k: (p, k, n))],
        out_specs=pl.BlockSpec((None, tm, tn), lambda p, m, n, k: (p, m, n)),
        out_shape=SDS((P, M, N), F32),
        compiler_params=_params("parallel", "parallel", "parallel", "arbitrary"),
    )(a, bc)


def _shift_down(x, s, row, fill):
    return jnp.where(row >= s, pltpu.roll(x, s, 0), fill)


def _shift_up(x, s, row, fill):
    n = x.shape[0]
    return jnp.where(row < n - s, pltpu.roll(x, n - s, 0), fill)


def _scan_down(a, b, row):
    s = 1
    while s < a.shape[0]:
        b = a * _shift_down(b, s, row, 0.0) + b
        a = a * _shift_down(a, s, row, 1.0)
        s *= 2
    return b


def _scan_up(a, b, row):
    s = 1
    while s < a.shape[0]:
        b = a * _shift_up(b, s, row, 0.0) + b
        a = a * _shift_up(a, s, row, 1.0)
        s *= 2
    return b


def _conv(xl, cw_ref, cb_ref, row):
    y = cb_ref[...] + _shift_down(xl, 3, row, 0.0) * cw_ref[0:1, :]
    y = y + _shift_down(xl, 2, row, 0.0) * cw_ref[1:2, :]
    y = y + _shift_down(xl, 1, row, 0.0) * cw_ref[2:3, :]
    return y + xl * cw_ref[3:4, :]


def _lru_gates(xc, wa_ref, ba_ref, wx_ref, bx_ref, lam_ref):
    ra = _sigmoid(_dot(xc, wa_ref[...]) + ba_ref[...])
    ri = _sigmoid(_dot(xc, wx_ref[...]) + bx_ref[...])
    ls = jax.nn.log_sigmoid(lam_ref[...])
    a = jnp.exp((RG_LRU_C * ra) * ls)
    mult = jnp.sqrt(1.0 - a * a)
    return ra, ri, ls, a, mult


def _lru_specs(S):
    col = lambda off: pl.BlockSpec((S, LANES), lambda j: (0, off + j))
    vec = pl.BlockSpec((1, LANES), lambda j: (0, j))
    blk = pl.BlockSpec((None, LANES, LANES), lambda j: (j, 0, 0))
    cw = pl.BlockSpec((CONV_WIDTH, LANES), lambda j: (0, j))
    return col, vec, blk, cw


def _lru_fwd(proj, cw, cb, wa, ba, wx, bx, lam):
    S = proj.shape[0]
    W = cb.shape[1]
    nb = W // LANES

    def body(xl_ref, gl_ref, cw_ref, cb_ref, wa_ref, ba_ref, wx_ref, bx_ref, lam_ref, y_ref):
        row = lax.broadcasted_iota(jnp.int32, (S, LANES), 0)
        xc = _conv(xl_ref[...], cw_ref, cb_ref, row)
        _, ri, _, a, mult = _lru_gates(xc, wa_ref, ba_ref, wx_ref, bx_ref, lam_ref)
        h = _scan_down(a, mult * (ri * xc), row)
        y_ref[...] = h * _gelu(gl_ref[...])

    col, vec, blk, cws = _lru_specs(S)
    return pl.pallas_call(
        body, name="lru_fwd", grid=(nb,),
        in_specs=[col(0), col(nb), cws, vec, blk, vec, blk, vec, vec],
        out_specs=pl.BlockSpec((S, LANES), lambda j: (0, j)),
        out_shape=SDS((S, W), F32), compiler_params=_params("parallel"),
    )(proj, proj, cw, cb, wa, ba, wx, bx, lam)


def _lru_bwd(proj, dy, cw, cb, wa, ba, wx, bx, lam, wat, wxt):
    S = proj.shape[0]
    W = cb.shape[1]
    nb = W // LANES

    def body(xl_ref, gl_ref, dy_ref, cw_ref, cb_ref, wa_ref, ba_ref, wx_ref, bx_ref, lam_ref, wat_ref, wxt_ref,
             dp_ref, dwa_ref, dwx_ref, vec_ref):
        row = lax.broadcasted_iota(jnp.int32, (S, LANES), 0)
        xl = xl_ref[...]
        xc = _conv(xl, cw_ref, cb_ref, row)
        ra, ri, ls, a, mult = _lru_gates(xc, wa_ref, ba_ref, wx_ref, bx_ref, lam_ref)
        h = _scan_down(a, mult * (ri * xc), row)
        gl = gl_ref[...]
        dyv = dy_ref[...]
        dp_ref[1] = (dyv * h * _gelu_grad(gl)).astype(ACT_DTYPE)
        adj = _scan_up(_shift_up(a, 1, row, 0.0), dyv * _gelu(gl), row)
        da = adj * _shift_down(h, 1, row, 0.0)
        dmult = adj * (ri * xc)
        dlog_a = da * a - dmult * (a * a) / mult
        dra = dlog_a * (RG_LRU_C * ls)
        dpa = dra * ra * (1.0 - ra)
        dpi = (adj * mult * xc) * ri * (1.0 - ri)
        dxc = adj * mult * ri + _dot(dpa, wat_ref[...]) + _dot(dpi, wxt_ref[...])
        dwa_ref[...] = _dot_tn(xc, dpa)
        dwx_ref[...] = _dot_tn(xc, dpi)
        dxl = dxc * cw_ref[3:4, :]
        dxl = dxl + _shift_up(dxc, 1, row, 0.0) * cw_ref[2:3, :]
        dxl = dxl + _shift_up(dxc, 2, row, 0.0) * cw_ref[1:2, :]
        dxl = dxl + _shift_up(dxc, 3, row, 0.0) * cw_ref[0:1, :]
        dp_ref[0] = dxl.astype(ACT_DTYPE)
        vec_ref[...] = jnp.zeros_like(vec_ref)
        vec_ref[0:1, :] = _rowsum(dpa)
        vec_ref[1:2, :] = _rowsum(dpi)
        vec_ref[2:3, :] = _rowsum(dlog_a * (RG_LRU_C * ra)) * _sigmoid(-lam_ref[...])
        vec_ref[3:4, :] = _rowsum(dxc)
        vec_ref[4:5, :] = _rowsum(dxc * _shift_down(xl, 3, row, 0.0))
        vec_ref[5:6, :] = _rowsum(dxc * _shift_down(xl, 2, row, 0.0))
        vec_ref[6:7, :] = _rowsum(dxc * _shift_down(xl, 1, row, 0.0))
        vec_ref[7:8, :] = _rowsum(dxc * xl)

    col, vec, blk, cws = _lru_specs(S)
    return pl.pallas_call(
        body, name="lru_bwd", grid=(nb,),
        in_specs=[col(0), col(nb), col(0), cws, vec, blk, vec, blk, vec, vec, blk, blk],
        out_specs=[pl.BlockSpec((2, S, LANES), lambda j: (0, 0, j)), blk, blk, pl.BlockSpec((2 * SUBLANES, LANES), lambda j: (0, j))],
        out_shape=[SDS((2, S, W), ACT_DTYPE), SDS((nb, LANES, LANES), F32), SDS((nb, LANES, LANES), F32), SDS((2 * SUBLANES, W), F32)],
        compiler_params=_params("parallel"),
    )(proj, proj, dy, cw, cb, wa, ba, wx, bx, lam, wat, wxt)


def _seg_mean(x, seg_ref, width):
    hi = x.astype(jnp.bfloat16)
    lo = (x - hi.astype(F32)).astype(jnp.bfloat16)
    ones = seg_ref[...]
    s = jnp.dot(hi, ones, preferred_element_type=F32) + jnp.dot(lo, ones, preferred_element_type=F32)
    return s * (1.0 / width)


def _gmlp_core(u_ref, v_ref, gv_ref, seg_ref, ws_ref, bfull_ref, z_scr, hd):
    tm, W = u_ref.shape
    lane = lax.broadcasted_iota(jnp.int32, (CHUNK, LANES), 1)
    ug = _gelu(u_ref[...])
    vg = _gelu(v_ref[...])
    cen = vg - _seg_mean(vg, seg_ref, hd)
    rstd = lax.rsqrt(_seg_mean(cen * cen, seg_ref, hd) + EPS)
    vhat = cen * rstd
    vh = vhat * gv_ref[...]
    vcats = {}
    for ci in range(tm // CHUNK):
        for p in range(W // LANES):
            blk = vh[ci * CHUNK:(ci + 1) * CHUNK, p * LANES:(p + 1) * LANES]
            vcat = jnp.concatenate([jnp.where(lane < hd, blk, 0.0), jnp.where(lane >= hd, blk, 0.0)], axis=0).astype(MXU_DTYPE)
            vcats[ci, p] = vcat
            z_scr[ci * CHUNK:(ci + 1) * CHUNK, p * LANES:(p + 1) * LANES] = (
                jnp.dot(ws_ref[p], vcat, preferred_element_type=F32) + bfull_ref[:, p * LANES:(p + 1) * LANES])
    return ug, vhat, rstd, vcats


def _gmlp_specs(tm, W, nb):
    rows = lambda off: pl.BlockSpec((tm, W), lambda i: (i, off))
    vec = pl.BlockSpec((1, W), lambda i: (0, 0))
    seg = pl.BlockSpec((W, W), lambda i: (0, 0))
    wsp = pl.BlockSpec((nb, CHUNK, 2 * CHUNK), lambda i: (0, 0, 0))
    bfull = pl.BlockSpec((CHUNK, W), lambda i: (0, 0))
    return rows, vec, seg, wsp, bfull


def _gmlp_fwd(proj, ylru, gv, seg, wsp, bfull, g_lru, g_gm):
    S, W = ylru.shape
    nb = W // LANES
    hd = W // HEADS
    tm = _tile(S, 512)

    def body(u_ref, v_ref, yl_ref, gv_ref, seg_ref, ws_ref, bfull_ref, gl_ref, gg_ref, yn_ref, ygm_ref, z_scr):
        ug, _, _, _ = _gmlp_core(u_ref, v_ref, gv_ref, seg_ref, ws_ref, bfull_ref, z_scr, hd)
        ygm = ug * z_scr[...]
        ygm_ref[...] = ygm
        yl = yl_ref[...]
        yn_ref[:, 0:W] = (yl * _rsqrt_ms(yl) * gl_ref[...]).astype(ACT_DTYPE)
        yn_ref[:, W:2 * W] = (ygm * _rsqrt_ms(ygm) * gg_ref[...]).astype(ACT_DTYPE)

    rows, vec, segs, wsps, bfulls = _gmlp_specs(tm, W, nb)
    return pl.pallas_call(
        body, name="gmlp_fwd", grid=(S // tm,),
        in_specs=[rows(2), rows(3), rows(0), vec, segs, wsps, bfulls, vec, vec],
        out_specs=[pl.BlockSpec((tm, 2 * W), lambda i: (i, 0)), rows(0)],
        out_shape=[SDS((S, 2 * W), ACT_DTYPE), SDS((S, W), F32)],
        scratch_shapes=[pltpu.VMEM((tm, W), F32)],
        compiler_params=_params("parallel"),
    )(proj, proj, ylru, gv, seg, wsp, bfull, g_lru, g_gm)


def _rms_bwd(y, g, dyn):
    r = _rsqrt_ms(y)
    yhat = y * r
    dyh = dyn * g
    return r * (dyh - yhat * jnp.mean(dyh * yhat, axis=-1, keepdims=True)), _rowsum(dyn * yhat)


def _gmlp_bwd(proj, ylru, ygm, dyn, gv, seg, wsp, wspt, bfull, g_lru, g_gm):
    S, W = ylru.shape
    nb = W // LANES
    hd = W // HEADS
    tm = _tile(S, 256)

    def body(u_ref, v_ref, yl_ref, ygm_ref, dl_ref, dg_ref, gv_ref, seg_ref, ws_ref, wst_ref, bfull_ref, gl_ref, gg_ref,
             dyl_ref, duv_ref, dws_ref, dbf_ref, acc_ref, z_scr, dvh_scr):
        @pl.when(pl.program_id(0) == 0)
        def _():
            dws_ref[...] = jnp.zeros_like(dws_ref)
            dbf_ref[...] = jnp.zeros_like(dbf_ref)
            acc_ref[...] = jnp.zeros_like(acc_ref)

        dyl, dgl = _rms_bwd(yl_ref[...], gl_ref[...], dl_ref[...])
        dyl_ref[...] = dyl
        dygm, dgg = _rms_bwd(ygm_ref[...], gg_ref[...], dg_ref[...])
        ug, vhat, rstd, vcats = _gmlp_core(u_ref, v_ref, gv_ref, seg_ref, ws_ref, bfull_ref, z_scr, hd)
        duv_ref[0] = (dygm * z_scr[...] * _gelu_grad(u_ref[...])).astype(ACT_DTYPE)
        dz = dygm * ug
        lane = lax.broadcasted_iota(jnp.int32, (CHUNK, LANES), 1)
        dbf = dz[0:CHUNK, :]
        for ci in range(1, tm // CHUNK):
            dbf += dz[ci * CHUNK:(ci + 1) * CHUNK, :]
        dbf_ref[...] += dbf
        for ci in range(tm // CHUNK):
            for p in range(nb):
                dzb = dz[ci * CHUNK:(ci + 1) * CHUNK, p * LANES:(p + 1) * LANES].astype(MXU_DTYPE)
                dws_ref[p] += _dot_nt(dzb, vcats[ci, p])
                dvc = jnp.dot(wst_ref[p], dzb, preferred_element_type=F32)
                dvh_scr[ci * CHUNK:(ci + 1) * CHUNK, p * LANES:(p + 1) * LANES] = jnp.where(lane < hd, dvc[0:CHUNK], dvc[CHUNK:2 * CHUNK])
        dvh = dvh_scr[...]
        dvn = dvh * gv_ref[...]
        dvg = rstd * (dvn - _seg_mean(dvn, seg_ref, hd) - vhat * _seg_mean(dvn * vhat, seg_ref, hd))
        duv_ref[1] = (dvg * _gelu_grad(v_ref[...])).astype(ACT_DTYPE)
        acc_ref[0:1, :] += dgl
        acc_ref[1:2, :] += dgg
        acc_ref[2:3, :] += _rowsum(dvh * vhat)

    rows, vec, segs, wsps, bfulls = _gmlp_specs(tm, W, nb)
    wspt_spec = pl.BlockSpec((nb, 2 * CHUNK, CHUNK), lambda i: (0, 0, 0))
    return pl.pallas_call(
        body, name="gmlp_bwd", grid=(S // tm,),
        in_specs=[rows(2), rows(3), rows(0), rows(0), rows(0), rows(1), vec, segs, wsps, wspt_spec, bfulls, vec, vec],
        out_specs=[rows(0), pl.BlockSpec((2, tm, W), lambda i: (0, i, 0)), wsps, bfulls, pl.BlockSpec((SUBLANES, W), lambda i: (0, 0))],
        out_shape=[SDS((S, W), F32), SDS((2, S, W), ACT_DTYPE), SDS((nb, CHUNK, 2 * CHUNK), F32), SDS((CHUNK, W), F32), SDS((SUBLANES, W), F32)],
        scratch_shapes=[pltpu.VMEM((tm, W), F32), pltpu.VMEM((tm, W), F32)],
        compiler_params=_params("arbitrary"),
    )(proj, proj, ylru, ygm, dyn, dyn, gv, seg, wsp, wspt, bfull, g_lru, g_gm)


def _ada_fwd(c_all, w_ada, b_shard):
    L, D, N = w_ada.shape
    R = c_all.shape[0]
    tn = N // 2

    def body(c_ref, w_ref, b_ref, o_ref):
        cv = c_ref[...]
        o_ref[...] = _dot(cv * _sigmoid(cv), w_ref[...]) + b_ref[...]

    return pl.pallas_call(
        body, name="ada_fwd", grid=(L, N // tn),
        in_specs=[pl.BlockSpec((R, D), lambda l, j: (0, 0)), pl.BlockSpec((None, D, tn), lambda l, j: (l, 0, j)),
                  pl.BlockSpec((None, 1, tn), lambda l, j: (l, 0, j))],
        out_specs=pl.BlockSpec((None, R, tn), lambda l, j: (l, 0, j)),
        out_shape=SDS((L, R, N), F32), compiler_params=_params("parallel", "parallel"),
    )(c_all, w_ada, b_shard)


def _ada_grad(c_all_t, dmod):
    D, B = c_all_t.shape
    L, _, N = dmod.shape
    tn = N // 2

    def body(c_ref, d_ref, o_ref):
        cv = c_ref[...]
        sc = cv * _sigmoid(cv)
        acc = sc[:, 0:1] * d_ref[0:1, :]
        for b in range(1, B):
            acc += sc[:, b:b + 1] * d_ref[b:b + 1, :]
        o_ref[...] = acc

    return pl.pallas_call(
        body, name="ada_grad", grid=(L, N // tn),
        in_specs=[pl.BlockSpec((D, B), lambda l, j: (0, 0)), pl.BlockSpec((None, B, tn), lambda l, j: (l, 0, j))],
        out_specs=pl.BlockSpec((None, D, tn), lambda l, j: (l, 0, j)),
        out_shape=SDS((L, D, N), F32), compiler_params=_params("parallel", "parallel"),
    )(c_all_t, dmod)


def _adamw(w, g, m, v):
    R, C = w.shape
    tr = _row_tile(R, C * 4)

    def body(w_ref, g_ref, m_ref, v_ref, d_ref, mo_ref, vo_ref):
        d_ref[...], mo_ref[...], vo_ref[...] = _adam_math(w_ref[...], g_ref[...], m_ref[...], v_ref[...])

    tile = pl.BlockSpec((tr, C), lambda i: (i, 0))
    return pl.pallas_call(
        body, name=f"adamw_r{R}c{C}", grid=(R // tr,), in_specs=[tile] * 4, out_specs=[tile] * 3,
        out_shape=[SDS((R, C), F32)] * 3, compiler_params=_params("parallel"),
    )(w, g, m, v)


def _adam_math(w, g, m, v):
    mn = ADAM_B1 * m + (1.0 - ADAM_B1) * g
    vn = ADAM_B2 * v + (1.0 - ADAM_B2) * (g * g)
    m_hat = mn / (1.0 - ADAM_B1 ** ADAM_STEP)
    v_hat = vn / (1.0 - ADAM_B2 ** ADAM_STEP)
    return -ADAM_LR * (m_hat / (jnp.sqrt(v_hat) + ADAM_EPS) + ADAM_WD * w), mn, vn


def _adamw_layer(w, g, m, v, l, prev, after):
    L, R, C = w.shape
    tr = _row_tile(R, C * 4)
    prev = (after,) + (() if prev is None else tuple(prev))

    def body(w_ref, g_ref, m_ref, v_ref, *rest):
        go_ref, d_ref, mo_ref, vo_ref = rest[len(prev):]
        gv = g_ref[...]
        go_ref[...] = gv
        d_ref[...], mo_ref[...], vo_ref[...] = _adam_math(w_ref[...], gv, m_ref[...], v_ref[...])

    lay = pl.BlockSpec((None, tr, C), lambda i: (l, i, 0))
    return pl.pallas_call(
        body, name=f"adamw_layer_r{R}c{C}", grid=(R // tr,),
        in_specs=[lay, pl.BlockSpec((tr, C), lambda i: (i, 0)), lay, lay] + [ANY] * len(prev), out_specs=[lay] * 4,
        out_shape=[SDS((L, R, C), F32)] * 4, input_output_aliases={5 + k: k for k in range(len(prev) - 1)},
        compiler_params=_params("parallel"),
    )(w, g, m, v, *prev)


def _sum_leading(a):
    P, R, C = a.shape
    tr = _row_tile(R, P * C * 4)

    def body(a_ref, o_ref):
        acc = a_ref[0]
        for p in range(1, P):
            acc = acc + a_ref[p]
        o_ref[...] = acc

    return pl.pallas_call(
        body, name=f"sum{P}_r{R}c{C}", grid=(R // tr,),
        in_specs=[pl.BlockSpec((P, tr, C), lambda i: (0, i, 0))],
        out_specs=pl.BlockSpec((tr, C), lambda i: (i, 0)),
        out_shape=SDS((R, C), F32), compiler_params=_params("parallel"),
    )(a)


def _add_half(g4, r1, place):
    _, _, R, C = g4.shape
    tr = _row_tile(R, C * 4)

    def body(place_ref, g_ref, r_ref, h_ref, own_ref):
        s = (g_ref[...] + r_ref[...]).astype(XFER_DTYPE)
        h_ref[...] = s

        @pl.when(pl.program_id(1) == place_ref[1])
        def _():
            own_ref[...] = s

    return pl.pallas_call(
        body, name=f"add_half_r{R}c{C}",
        grid_spec=pltpu.PrefetchScalarGridSpec(
            num_scalar_prefetch=1, grid=(R // tr, N_CHIPS),
            in_specs=[pl.BlockSpec((None, None, tr, C), lambda i, p, place_ref: (p, place_ref[0], i, 0)),
                      pl.BlockSpec((None, tr, C), lambda i, p, place_ref: (p, i, 0))],
            out_specs=[pl.BlockSpec((None, tr, C), lambda i, p, place_ref: (p, i, 0)),
                       pl.BlockSpec((None, tr, C), lambda i, p, place_ref: (place_ref[1], i, 0))],
        ),
        out_shape=[SDS((N_CHIPS, R, C), XFER_DTYPE)] * 2, compiler_params=_params("parallel", "arbitrary"),
    )(place, g4, r1)


def _sum4_into_half(r2, place):
    P, R, C = r2.shape
    tr = _row_tile(R, P * C * 4)

    def body(place_ref, a_ref, o_ref):
        acc = a_ref[0].astype(F32)
        for p in range(1, P):
            acc = acc + a_ref[p].astype(F32)
        o_ref[...] = acc

    return pl.pallas_call(
        body, name=f"sum4_r{R}c{C}",
        grid_spec=pltpu.PrefetchScalarGridSpec(
            num_scalar_prefetch=1, grid=(R // tr,),
            in_specs=[pl.BlockSpec((P, tr, C), lambda i, place_ref: (0, i, 0))],
            out_specs=pl.BlockSpec((None, tr, C), lambda i, place_ref: (place_ref[0], i, 0)),
        ),
        out_shape=SDS((2, R, C), F32), compiler_params=_params("parallel"),
    )(place, r2)


def _cast_into_slot(w, l, place, after):
    _, R, C = w.shape
    tr = _row_tile(R, C * 4)

    def body(place_ref, w_ref, after_ref, o_ref):
        o_ref[...] = w_ref[...].astype(MXU_DTYPE)

    return pl.pallas_call(
        body, name=f"cast_r{R}c{C}",
        grid_spec=pltpu.PrefetchScalarGridSpec(
            num_scalar_prefetch=1, grid=(R // tr,),
            in_specs=[pl.BlockSpec((None, tr, C), lambda i, place_ref: (l, i, 0)), ANY],
            out_specs=pl.BlockSpec((None, tr, C), lambda i, place_ref: (place_ref[1], i, 0)),
        ),
        out_shape=SDS((N_CHIPS, R, C), MXU_DTYPE), compiler_params=_params("parallel"),
    )(place, w, after)


def _place():
    x, y, c = lax.axis_index("x"), lax.axis_index("y"), lax.axis_index("c")
    chips = [(1 - x, y), (x, 1 - y), (1 - x, 1 - y)]
    return x, y, c, chips


def _remote(src, dst, send_sem, recv_sem, to):
    return pltpu.make_async_remote_copy(src_ref=src, dst_ref=dst, send_sem=send_sem, recv_sem=recv_sem, device_id=to, device_id_type=MESH)


def _all_gather8(v):
    R, N = v.shape

    def body(v_ref, out_ref, send_sems, recv_sems, local_sem):
        x, y, c, chips = _place()
        me, sibling = (x, y, c), (x, y, 1 - c)

        def slot(px, py, pc):
            return out_ref.at[4 * px + 2 * py + pc]

        def copy(k, block, to, src=None):
            return _remote(slot(*block) if src is None else src, slot(*block), send_sems.at[k], recv_sems.at[k], to)

        mine = pltpu.make_async_copy(v_ref, slot(*me), local_sem)
        mine.start()
        first = [copy(0, me, sibling, src=v_ref)] + [copy(1 + j, me, (*chip, c), src=v_ref) for j, chip in enumerate(chips)]
        for cp in first:
            cp.start()
        passed = [copy(4 + j, (*chip, c), sibling) for j, chip in enumerate(chips)]
        for j, chip in enumerate(chips):
            copy(1 + j, (*chip, c), me).wait_recv()
            passed[j].start()
        copy(0, sibling, me).wait_recv()
        for j, chip in enumerate(chips):
            copy(4 + j, (*chip, 1 - c), me).wait_recv()
        for cp in first + passed:
            cp.wait_send()
        mine.wait()

    return pl.pallas_call(
        body, name=f"all_gather8_r{R}n{N}", out_shape=SDS((N_DEV, R, N), v.dtype), in_specs=[ANY], out_specs=ANY,
        scratch_shapes=[pltpu.SemaphoreType.DMA((7,)), pltpu.SemaphoreType.DMA((7,)), pltpu.SemaphoreType.DMA],
    )(v)


def _gather_weights(slots):
    n = len(slots)

    def body(*refs):
        ins, outs = refs[:n], refs[n:2 * n]
        send_sems, recv_sems = refs[2 * n:]
        x, y, c, chips = _place()
        q = 2 * x + y
        sibling = (x, y, 1 - c)
        first = []
        for k in range(n):
            for j, chip in enumerate(chips):
                first.append(_remote(ins[k].at[q, c], outs[k].at[q, c], send_sems.at[k, j], recv_sems.at[k, j], (*chip, c)))
                first[-1].start()
        passed = []
        for k in range(n):
            for j, chip in enumerate(chips):
                half = outs[k].at[2 * chip[0] + chip[1], c]
                _remote(half, half, send_sems.at[k, j], recv_sems.at[k, j], sibling).wait_recv()
                passed.append(_remote(half, half, send_sems.at[k, 3 + j], recv_sems.at[k, 3 + j], sibling))
                passed[-1].start()
        for k in range(n):
            for j, chip in enumerate(chips):
                half = outs[k].at[2 * chip[0] + chip[1], 1 - c]
                _remote(half, half, send_sems.at[k, 3 + j], recv_sems.at[k, 3 + j], sibling).wait_recv()
        for cp in first + passed:
            cp.wait_send()

    return pl.pallas_call(
        body, name="gather_weights", out_shape=[SDS(s.shape, s.dtype) for s in slots],
        in_specs=[ANY] * n, out_specs=[ANY] * n, input_output_aliases={k: k for k in range(n)},
        scratch_shapes=[pltpu.SemaphoreType.DMA((n, 6)), pltpu.SemaphoreType.DMA((n, 6))],
    )(*slots)


def _swap_halves(g4s):
    n = len(g4s)

    def body(*refs):
        ins, outs = refs[:n], refs[n:2 * n]
        send_sems, recv_sems = refs[2 * n:]
        x, y, c, _ = _place()
        sibling = (x, y, 1 - c)
        for k in range(n):
            for p in range(N_CHIPS):
                _remote(ins[k].at[p, 1 - c], outs[k].at[p], send_sems.at[k], recv_sems.at[k], sibling).start()
        for k in range(n):
            _remote(outs[k], outs[k], send_sems.at[k], recv_sems.at[k], sibling).wait()

    return pl.pallas_call(
        body, name="swap_halves", out_shape=[SDS((N_CHIPS,) + g.shape[2:], g.dtype) for g in g4s],
        in_specs=[ANY] * n, out_specs=[ANY] * n,
        scratch_shapes=[pltpu.SemaphoreType.DMA((n,)), pltpu.SemaphoreType.DMA((n,))],
    )(*g4s)


def _scatter_regions(hs, lands):
    n = len(hs)

    def body(*refs):
        ins, outs = refs[:n], refs[2 * n:3 * n]
        send_sems, recv_sems = refs[3 * n:]
        x, y, c, chips = _place()
        q = 2 * x + y
        sent = []
        for k in range(n):
            for j, chip in enumerate(chips):
                sent.append(_remote(ins[k].at[2 * chip[0] + chip[1]], outs[k].at[q], send_sems.at[k, j], recv_sems.at[k, j], (*chip, c)))
                sent[-1].start()
        for k in range(n):
            for j, chip in enumerate(chips):
                got = outs[k].at[2 * chip[0] + chip[1]]
                _remote(got, got, send_sems.at[k, j], recv_sems.at[k, j], (x, y, c)).wait_recv()
        for cp in sent:
            cp.wait_send()

    return pl.pallas_call(
        body, name="scatter_regions", out_shape=[SDS(h.shape, h.dtype) for h in lands],
        in_specs=[ANY] * (2 * n), out_specs=[ANY] * n, input_output_aliases={n + k: k for k in range(n)},
        scratch_shapes=[pltpu.SemaphoreType.DMA((n, 3)), pltpu.SemaphoreType.DMA((n, 3))],
    )(*hs, *lands)


def _share_halves(fins):
    n = len(fins)

    def body(*refs):
        ins, outs = refs[:n], refs[n:2 * n]
        send_sems, recv_sems = refs[2 * n:]
        x, y, c, _ = _place()
        sibling = (x, y, 1 - c)
        sent = [_remote(ins[k].at[c], outs[k].at[c], send_sems.at[k], recv_sems.at[k], sibling) for k in range(n)]
        for cp in sent:
            cp.start()
        for k in range(n):
            got = outs[k].at[1 - c]
            _remote(got, got, send_sems.at[k], recv_sems.at[k], sibling).wait_recv()
        for cp in sent:
            cp.wait_send()

    return pl.pallas_call(
        body, name="share_halves", out_shape=[SDS(t.shape, t.dtype) for t in fins],
        in_specs=[ANY] * n, out_specs=[ANY] * n, input_output_aliases={k: k for k in range(n)},
        scratch_shapes=[pltpu.SemaphoreType.DMA((n,)), pltpu.SemaphoreType.DMA((n,))],
    )(*fins)


def _chip_sums(grads, place):
    g4s = [g.reshape(N_CHIPS, 2, g.shape[1] // 2, g.shape[2]) for g in grads]
    pairs = [_add_half(g4, r1, place) for g4, r1 in zip(g4s, _swap_halves(g4s))]
    return [h for h, _ in pairs], [own for _, own in pairs]


def _reduce_finish(lands, place):
    fins = _share_halves([_sum4_into_half(r2, place) for r2 in lands])
    return [f.reshape(2 * f.shape[1], f.shape[2]) for f in fins]


def _reduce_scatter(grads, place):
    hs, lands = _chip_sums(grads, place)
    return _reduce_finish(_scatter_regions(hs, lands), place)


HBM_SPEC = pl.BlockSpec(memory_space=pltpu.HBM)
SEM_SPEC = pl.BlockSpec(memory_space=pltpu.SEMAPHORE)
DATAFLOW = pltpu.SideEffectType.DATAFLOW_SIDE_EFFECTING


def _in_hbm(a):
    return pltpu.with_memory_space_constraint(a, pltpu.HBM)


def _hbm_like(a):
    return pltpu.HBM(a.shape, a.dtype)


def _gather_ici_start(slots, thru, after):
    n = len(slots)

    def body(*refs):
        ins = refs[:n]
        send_sems, recv_sems = refs[n + 2], refs[n + 3]
        x, y, c, chips = _place()
        q = 2 * x + y
        for k in range(n):
            for j, chip in enumerate(chips):
                _remote(ins[k].at[q, c], ins[k].at[q, c], send_sems.at[3 * k + j], recv_sems.at[3 * k + j], (*chip, c)).start()

    out = pl.pallas_call(
        body, name="gather_ici_start",
        out_shape=(pltpu.SemaphoreType.DMA((3 * n,)), pltpu.SemaphoreType.DMA((3 * n,)), *[_hbm_like(s) for s in slots], _hbm_like(thru)),
        in_specs=[HBM_SPEC] * (n + 1) + [ANY], out_specs=(SEM_SPEC, SEM_SPEC, *[HBM_SPEC] * (n + 1)),
        input_output_aliases={k: 2 + k for k in range(n + 1)},
        compiler_params=pltpu.CompilerParams(has_side_effects=DATAFLOW),
    )(*[_in_hbm(s) for s in slots], _in_hbm(thru), after)
    return out[0], out[1], list(out[2:2 + n]), out[2 + n]


def _gather_ici_wait(send_sems, recv_sems, slots, after):
    n = len(slots)

    def body(*refs):
        ins = refs[:n]
        send_sems, recv_sems = refs[n], refs[n + 1]
        x, y, c, chips = _place()
        for k in range(n):
            for j, chip in enumerate(chips):
                got = ins[k].at[2 * chip[0] + chip[1], c]
                cp = _remote(got, got, send_sems.at[3 * k + j], recv_sems.at[3 * k + j], (x, y, c))
                cp.wait_send()
                cp.wait_recv()

    out = pl.pallas_call(
        body, name="gather_ici_wait", out_shape=[_hbm_like(s) for s in slots],
        in_specs=[HBM_SPEC] * n + [SEM_SPEC, SEM_SPEC, ANY], out_specs=[HBM_SPEC] * n,
        input_output_aliases={k: k for k in range(n)},
        compiler_params=pltpu.CompilerParams(has_side_effects=DATAFLOW),
    )(*slots, send_sems, recv_sems, after)
    return list(out)


def _gather_pass_on(slots):
    n = len(slots)

    def body(*refs):
        ins, outs = refs[:n], refs[n:2 * n]
        send_sems, recv_sems = refs[2 * n:]
        x, y, c, chips = _place()
        sibling = (x, y, 1 - c)
        passed = []
        for k in range(n):
            for j, chip in enumerate(chips):
                passed.append(_remote(ins[k].at[2 * chip[0] + chip[1], c], outs[k].at[2 * chip[0] + chip[1], c],
                                      send_sems.at[k, j], recv_sems.at[k, j], sibling))
                passed[-1].start()
        for k in range(n):
            for j, chip in enumerate(chips):
                half = outs[k].at[2 * chip[0] + chip[1], 1 - c]
                _remote(half, half, send_sems.at[k, j], recv_sems.at[k, j], sibling).wait_recv()
        for cp in passed:
            cp.wait_send()

    return pl.pallas_call(
        body, name="gather_pass_on", out_shape=[SDS(s.shape, s.dtype) for s in slots],
        in_specs=[ANY] * n, out_specs=[ANY] * n, input_output_aliases={k: k for k in range(n)},
        scratch_shapes=[pltpu.SemaphoreType.DMA((n, 3)), pltpu.SemaphoreType.DMA((n, 3))],
    )(*slots)


def _scatter_start(hs, lands, thru):
    n = len(hs)

    def body(*refs):
        ins, zones = refs[:n], refs[n:2 * n]
        send_sems, recv_sems = refs[2 * n + 1], refs[2 * n + 2]
        x, y, c, chips = _place()
        q = 2 * x + y
        for k in range(n):
            for j, chip in enumerate(chips):
                _remote(ins[k].at[2 * chip[0] + chip[1]], zones[k].at[q], send_sems.at[3 * k + j], recv_sems.at[3 * k + j], (*chip, c)).start()

    arrays = [*hs, *lands, thru]
    out = pl.pallas_call(
        body, name="scatter_start",
        out_shape=(pltpu.SemaphoreType.DMA((3 * n,)), pltpu.SemaphoreType.DMA((3 * n,)), *[_hbm_like(a) for a in arrays]),
        in_specs=[HBM_SPEC] * len(arrays), out_specs=(SEM_SPEC, SEM_SPEC, *[HBM_SPEC] * len(arrays)),
        input_output_aliases={k: 2 + k for k in range(len(arrays))},
        compiler_params=pltpu.CompilerParams(has_side_effects=DATAFLOW),
    )(*[_in_hbm(a) for a in arrays])
    return out[0], out[1], list(out[2:2 + n]), list(out[2 + n:2 + 2 * n]), out[2 + 2 * n]


def _scatter_wait(send_sems, recv_sems, hs, lands, after):
    n = len(hs)

    def body(*refs):
        ins, zones = refs[:n], refs[n:2 * n]
        send_sems, recv_sems = refs[2 * n], refs[2 * n + 1]
        x, y, c, chips = _place()
        for k in range(n):
            for j, chip in enumerate(chips):
                p = 2 * chip[0] + chip[1]
                cp = _remote(ins[k].at[p], zones[k].at[p], send_sems.at[3 * k + j], recv_sems.at[3 * k + j], (x, y, c))
                cp.wait_send()
                cp.wait_recv()

    out = pl.pallas_call(
        body, name="scatter_wait", out_shape=[_hbm_like(a) for a in [*hs, *lands]],
        in_specs=[HBM_SPEC] * (2 * n) + [SEM_SPEC, SEM_SPEC, ANY], out_specs=[HBM_SPEC] * (2 * n),
        input_output_aliases={k: k for k in range(2 * n)},
        compiler_params=pltpu.CompilerParams(has_side_effects=DATAFLOW),
    )(*hs, *lands, send_sems, recv_sems, after)
    return list(out[n:])


def _split_start(name, arrays, n_sems, issue, extra=()):
    m = len(arrays)

    def body(*refs):
        issue(refs[:m], refs[m + len(extra)], refs[m + len(extra) + 1])

    out = pl.pallas_call(
        body, name=name,
        out_shape=(pltpu.SemaphoreType.DMA((n_sems,)), pltpu.SemaphoreType.DMA((n_sems,)), *[_hbm_like(a) for a in arrays]),
        in_specs=[HBM_SPEC] * m + [ANY] * len(extra), out_specs=(SEM_SPEC, SEM_SPEC, *[HBM_SPEC] * m),
        input_output_aliases={k: 2 + k for k in range(m)},
        compiler_params=pltpu.CompilerParams(has_side_effects=DATAFLOW),
    )(*[_in_hbm(a) for a in arrays], *extra)
    return out[0], out[1], list(out[2:])


def _split_wait(name, send_sems, recv_sems, arrays, after, drain):
    m = len(arrays)

    def body(*refs):
        drain(refs[:m], refs[m], refs[m + 1])

    out = pl.pallas_call(
        body, name=name, out_shape=[_hbm_like(a) for a in arrays],
        in_specs=[HBM_SPEC] * m + [SEM_SPEC, SEM_SPEC, ANY], out_specs=[HBM_SPEC] * m,
        input_output_aliases={k: k for k in range(m)},
        compiler_params=pltpu.CompilerParams(has_side_effects=DATAFLOW),
    )(*arrays, send_sems, recv_sems, after)
    return list(out)


def _wait_both(cp):
    cp.wait_send()
    cp.wait_recv()


class _Flight:
    def __init__(self, name, arrays, n_sems, issue, drain, thru, extra=()):
        self.name, self.drain, self.n = name, drain, len(arrays)
        self.send, self.recv, out = _split_start(name + "_start", [*arrays, thru], n_sems, issue, extra)
        self.arrays, self.thru = out[:-1], out[-1]

    def land(self, after):
        return _split_wait(self.name + "_wait", self.send, self.recv, self.arrays, after, self.drain)


def _gather_flight(tag, ici, d2d, direct, thru):
    kinds = ["ici"] * len(ici) + ["d2d"] * len(d2d) + ["direct"] * len(direct)

    def issue(refs, send_sems, recv_sems):
        x, y, c, chips = _place()
        q = 2 * x + y
        for k, kind in enumerate(kinds):
            for j, chip in enumerate(chips):
                if kind == "ici":
                    src, to = refs[k].at[q, c], (*chip, c)
                elif kind == "d2d":
                    src, to = refs[k].at[2 * chip[0] + chip[1], c], (x, y, 1 - c)
                else:
                    src, to = refs[k].at[q], (*chip, c)
                _remote(src, src, send_sems.at[3 * k + j], recv_sems.at[3 * k + j], to).start()

    def drain(refs, send_sems, recv_sems):
        x, y, c, chips = _place()
        for k, kind in enumerate(kinds):
            for j, chip in enumerate(chips):
                p = 2 * chip[0] + chip[1]
                got = refs[k].at[p] if kind == "direct" else refs[k].at[p, c if kind == "ici" else 1 - c]
                _wait_both(_remote(got, got, send_sems.at[3 * k + j], recv_sems.at[3 * k + j], (x, y, c)))

    return _Flight(f"gather{tag}", [*ici, *d2d, *direct], 3 * len(kinds), issue, drain, thru)


def _swap_flight(tag, g4s, thru):
    n = len(g4s)
    zones = [lax.empty((N_CHIPS,) + g.shape[2:], g.dtype) for g in g4s]

    def issue(refs, send_sems, recv_sems):
        x, y, c, _ = _place()
        for k in range(n):
            for p in range(N_CHIPS):
                _remote(refs[k].at[p, 1 - c], refs[n + k].at[p], send_sems.at[N_CHIPS * k + p], recv_sems.at[N_CHIPS * k + p], (x, y, 1 - c)).start()

    def drain(refs, send_sems, recv_sems):
        x, y, c, _ = _place()
        for k in range(n):
            for p in range(N_CHIPS):
                got = refs[n + k].at[p]
                _wait_both(_remote(got, got, send_sems.at[N_CHIPS * k + p], recv_sems.at[N_CHIPS * k + p], (x, y, c)))

    return _Flight(f"swap{tag}", [*g4s, *zones], N_CHIPS * n, issue, drain, thru)


def _scatter_flight(tag, hs, lands, thru):
    n = len(hs)

    def issue(refs, send_sems, recv_sems):
        x, y, c, chips = _place()
        q = 2 * x + y
        for k in range(n):
            for j, chip in enumerate(chips):
                _remote(refs[k].at[2 * chip[0] + chip[1]], refs[n + k].at[q], send_sems.at[3 * k + j], recv_sems.at[3 * k + j], (*chip, c)).start()

    def drain(refs, send_sems, recv_sems):
        x, y, c, chips = _place()
        for k in range(n):
            for j, chip in enumerate(chips):
                got = refs[n + k].at[2 * chip[0] + chip[1]]
                _wait_both(_remote(got, got, send_sems.at[3 * k + j], recv_sems.at[3 * k + j], (x, y, c)))

    return _Flight(f"scatter{tag}", [*hs, *lands], 3 * n, issue, drain, thru)


def _exchange_flight(buf, thru):
    flips = [(fx, fy, fc) for fx in (0, 1) for fy in (0, 1) for fc in (0, 1)][1:]

    def peers():
        x, y, c, _ = _place()
        return (x, y, c), [((1 - x) if fx else x, (1 - y) if fy else y, (1 - c) if fc else c) for fx, fy, fc in flips]

    def slot(ref, dev):
        return ref.at[4 * dev[0] + 2 * dev[1] + dev[2]]

    def issue(refs, send_sems, recv_sems):
        me, others = peers()
        for j, to in enumerate(others):
            _remote(slot(refs[0], me), slot(refs[0], me), send_sems.at[j], recv_sems.at[j], to).start()

    def drain(refs, send_sems, recv_sems):
        me, others = peers()
        for j, frm in enumerate(others):
            got = slot(refs[0], frm)
            _wait_both(_remote(got, got, send_sems.at[j], recv_sems.at[j], me))

    return _Flight("exchange", [buf], len(flips), issue, drain, thru)


def _share_flight(tag, fins, thru):
    n = len(fins)

    def issue(refs, send_sems, recv_sems):
        x, y, c, _ = _place()
        for k in range(n):
            _remote(refs[k].at[c], refs[k].at[c], send_sems.at[k], recv_sems.at[k], (x, y, 1 - c)).start()

    def drain(refs, send_sems, recv_sems):
        x, y, c, _ = _place()
        for k in range(n):
            got = refs[k].at[1 - c]
            _wait_both(_remote(got, got, send_sems.at[k], recv_sems.at[k], (x, y, c)))

    return _Flight(f"share{tag}", fins, n, issue, drain, thru)


def _pair_blocks(w):
    h, d, _ = w.shape
    z = jnp.zeros((h // 2, d, d), w.dtype)
    return jnp.concatenate([jnp.concatenate([w[0::2], z], axis=2), jnp.concatenate([z, w[1::2]], axis=2)], axis=1)


def _unpair_blocks(b):
    n, dd, _ = b.shape
    d = dd // 2
    return jnp.stack([b[:, :d, :d], b[:, d:, d:]], axis=1).reshape(2 * n, d, d)


def _pad_rows(a, rows):
    return jnp.pad(a, ((0, rows - a.shape[0]), (0, 0)))


class _Packer:
    def __init__(self, shapes, width=1024, row_multiple=64):
        self.shapes = shapes
        self.sizes = [math.prod(s) for s in shapes]
        total = sum(self.sizes)
        self.width = width
        self.rows = -(-total // (width * row_multiple)) * row_multiple
        self.pad = self.rows * width - total

    def pack(self, arrays):
        flat = jnp.concatenate([a.reshape(-1).astype(F32) for a in arrays] + [jnp.zeros((self.pad,), F32)])
        return flat.reshape(self.rows, self.width)

    def unpack(self, packed):
        flat = packed.reshape(-1)
        out, off = [], 0
        for s, n in zip(self.shapes, self.sizes):
            out.append(flat[off:off + n].reshape(s))
            off += n
        return out


SMALL = ["b_ada", "ffn1_norm", "mix_norm", "conv_w", "conv_b", "gate_a_w", "gate_a_b", "gate_x_w", "gate_x_b", "lru_lambda",
         "v_norm", "spatial_w", "spatial_b", "lru_out_norm", "gmlp_out_norm", "ffn2_norm", "final_norm"]
BIG = ["ffn1_w_gu", "ffn1_w_down", "w_in", "w_out", "ffn2_w_gu", "ffn2_w_down"]
GROUPS = (("ffn1_w_gu", "ffn1_w_down"), ("w_in", "w_out"), ("ffn2_w_gu", "ffn2_w_down"))
FWD_GROUPS = (("ffn1_w_gu",), ("ffn1_w_down",), ("w_in", "w_out"), ("ffn2_w_gu",), ("ffn2_w_down",))
MIN_AGE = {"swap": 1, "scatter": 2, "share": 1}
WEIGHTS = ["w_ada", "b_ada", "ffn1_norm", "ffn1_w_gu", "ffn1_w_down", "mix_norm", "w_in", "conv_w", "conv_b", "gate_a_w", "gate_a_b",
           "gate_x_w", "gate_x_b", "lru_lambda", "v_norm", "spatial_w", "spatial_b", "lru_out_norm", "gmlp_out_norm", "w_out",
           "ffn2_norm", "ffn2_w_gu", "ffn2_w_down", "final_norm"]


def kernel(x, c, w_ada, b_ada, ffn1_norm, ffn1_w_gu, ffn1_w_down, mix_norm, w_in, conv_w, conv_b, gate_a_w, gate_a_b, gate_x_w, gate_x_b, lru_lambda, v_norm, spatial_w, spatial_b, lru_out_norm, gmlp_out_norm, w_out, ffn2_norm, ffn2_w_gu, ffn2_w_down, final_norm, loss_target, m_w_ada, m_b_ada, m_ffn1_norm, m_ffn1_w_gu, m_ffn1_w_down, m_mix_norm, m_w_in, m_conv_w, m_conv_b, m_gate_a_w, m_gate_a_b, m_gate_x_w, m_gate_x_b, m_lru_lambda, m_v_norm, m_spatial_w, m_spatial_b, m_lru_out_norm, m_gmlp_out_norm, m_w_out, m_ffn2_norm, m_ffn2_w_gu, m_ffn2_w_down, m_final_norm, v_w_ada, v_b_ada, v_ffn1_norm, v_ffn1_w_gu, v_ffn1_w_down, v_mix_norm, v_w_in, v_conv_w, v_conv_b, v_gate_a_w, v_gate_a_b, v_gate_x_w, v_gate_x_b, v_lru_lambda, v_v_norm, v_spatial_w, v_spatial_b, v_lru_out_norm, v_gmlp_out_norm, v_w_out, v_ffn2_norm, v_ffn2_w_gu, v_ffn2_w_down, v_final_norm):
    given = dict(locals())
    W = {n: given[n] for n in WEIGHTS}
    L = w_ada.shape[0]
    S, D = x.shape[1], x.shape[2]
    LW = conv_b.shape[1]
    hd = LW // HEADS
    xi, yi, ci = lax.axis_index("x"), lax.axis_index("y"), lax.axis_index("c")
    chip = 2 * xi + yi
    dev = 2 * chip + ci
    place = jnp.stack([ci, chip]).astype(jnp.int32)
    xs = x.reshape(S, D)
    tgt = loss_target.reshape(S, D)

    c_all = _all_gather8(_pad_rows(c, SUBLANES))[:, 0, :]
    n_ada = w_ada.shape[2]
    b_shard = lax.dynamic_slice_in_dim(b_ada, chip * n_ada, n_ada, axis=1)
    mod_shard = _ada_fwd(_pad_rows(c_all, 2 * SUBLANES), w_ada, b_shard[:, None, :])

    def in_slot(block):
        return lax.dynamic_update_index_in_dim(jnp.zeros((N_CHIPS,) + block.shape, block.dtype), block, chip, 0)

    cws = LW // N_CHIPS
    small = [in_slot(mod_shard.reshape(L * 2 * SUBLANES, n_ada)), in_slot(conv_w.reshape(L * CONV_WIDTH, cws))]

    def half_view(s):
        return s.reshape(N_CHIPS, 2, s.shape[1] // 2, s.shape[2])

    stages = [(l, names) for l in range(L) for names in FWD_GROUPS]
    seq = [[half_view(_cast_into_slot(W[n], l, place, place)) for n in names] for l, names in stages[:1]]
    flights = {}

    def launch(t, thru, direct=()):
        ici = seq[t] if t < len(seq) else []
        d2d = seq[t - 1] if 1 <= t <= len(seq) else []
        if ici or d2d or direct:
            flights[t] = _gather_flight(t, ici, d2d, list(direct), thru)
            thru = flights[t].thru
        return thru

    def land(t, after):
        if t not in flights:
            return []
        out = flights.pop(t).land(after)
        ni = len(seq[t]) if t < len(seq) else 0
        nd = len(seq[t - 1]) if 1 <= t <= len(seq) else 0
        if ni:
            seq[t] = out[:ni]
        if nd:
            seq[t - 1] = out[ni:ni + nd]
        return out[ni + nd:]

    def group_weights(t):
        return [s.reshape(N_CHIPS, -1, s.shape[3]) for s in seq[t]]

    c_all = launch(0, c_all, small)
    seq += [[half_view(_cast_into_slot(W[n], l, place, c_all)) for n in names] for l, names in stages[1:]]
    mod_all, conv_all = land(0, seq[-1][-1])
    mod_all = launch(1, mod_all)
    land(1, mod_all)
    mod_rows = lax.dynamic_index_in_dim(mod_all.reshape(N_CHIPS, L, 2 * SUBLANES, n_ada), dev, axis=2, keepdims=False)
    mod = mod_rows.transpose(1, 0, 2).reshape(L, N_MOD, 1, D)
    conv_full = conv_all.reshape(N_CHIPS, L, CONV_WIDTH, cws).transpose(1, 2, 0, 3).reshape(L, CONV_WIDTH, LW)

    tril = jnp.tril(jnp.ones((CHUNK, CHUNK), F32))
    seg = (jnp.arange(LW)[:, None] // hd == jnp.arange(LW)[None, :] // hd).astype(jnp.bfloat16)

    def mixer_params(l):
        ws = spatial_w[l] * tril
        wsp = jnp.concatenate([ws[0::2], ws[1::2]], axis=2)
        wa, wx = _pair_blocks(gate_a_w[l]), _pair_blocks(gate_x_w[l])
        return dict(
            cw=conv_full[l], cb=conv_b[l][None],
            wa=wa.astype(MXU_DTYPE), wx=wx.astype(MXU_DTYPE), wat=wa.transpose(0, 2, 1).astype(MXU_DTYPE), wxt=wx.transpose(0, 2, 1).astype(MXU_DTYPE),
            ba=gate_a_b[l].reshape(1, LW), bx=gate_x_b[l].reshape(1, LW), lam=lru_lambda[l][None], gv=v_norm[l][None],
            wsp=wsp.astype(MXU_DTYPE), wspt=wsp.transpose(0, 2, 1).astype(MXU_DTYPE),
            bfull=jnp.repeat(spatial_b[l].T, hd, axis=1), g_lru=lru_out_norm[l][None], g_gm=gmlp_out_norm[l][None])

    saved = []
    xcur = xs
    for l in range(L):
        mp, md = mixer_params(l), mod[l]
        s = dict(lw={}, mp=mp, md=md)
        lw = s["lw"]
        t = len(FWD_GROUPS) * l
        s["x0"] = xcur
        s["h1"] = launch(t + 2, _modnorm(xcur, ffn1_norm[l][None], md[0], md[1]))
        lw["gu1"], = group_weights(t)
        s["a1"], s["gu1"] = _ffn_up(s["h1"], lw["gu1"])
        land(t + 2, s["a1"])
        s["a1"] = launch(t + 3, s["a1"])
        lw["d1"] = group_weights(t + 1)[0].reshape(-1, D)
        s["f1"], xcur = _mm_res(s["a1"], lw["d1"], xcur, md[2], 0.5)
        land(t + 3, xcur)
        s["x1"] = xcur
        s["h2"] = launch(t + 4, _modnorm(xcur, mix_norm[l][None], md[3], md[4]))
        lw["win"], wout = group_weights(t + 2)
        lw["wout"] = wout.reshape(-1, D)
        s["proj"] = _mm_chunks(s["h2"], lw["win"])
        s["ylru"] = _lru_fwd(s["proj"], mp["cw"], mp["cb"], mp["wa"], mp["ba"], mp["wx"], mp["bx"], mp["lam"])
        s["yn"], s["ygm"] = _gmlp_fwd(s["proj"], s["ylru"], mp["gv"], seg, mp["wsp"], mp["bfull"], mp["g_lru"], mp["g_gm"])
        s["f2"], xcur = _mm_res(s["yn"], lw["wout"], xcur, md[5], 1.0)
        land(t + 4, xcur)
        s["x2"] = xcur
        s["h3"] = launch(t + 5, _modnorm(xcur, ffn2_norm[l][None], md[6], md[7]))
        lw["gu2"], = group_weights(t + 3)
        s["a3"], s["gu3"] = _ffn_up(s["h3"], lw["gu2"])
        land(t + 5, s["a3"])
        s["a3"] = launch(t + 6, s["a3"])
        lw["d2"] = group_weights(t + 4)[0].reshape(-1, D)
        s["f3"], xcur = _mm_res(s["a3"], lw["d2"], xcur, md[8], 0.5)
        land(t + 6, xcur)
        saved.append(s)

    dx, dq, head_acc = _loss_head(xcur, tgt, final_norm[None], saved[-1]["md"][8], 0.5)
    loss = lax.psum(jnp.sum(head_acc[1]), ("x", "y", "c"))
    small_grads = {}
    big_grads = {n: [None] * L for n in BIG}
    dmods = [None] * L
    zero_row = jnp.zeros((1, D), F32)

    def ffn_bwd(names, l, dx, dq, x_in, h, a, gu, f, wgu, wd, gn, sc, next_gate, next_scale):
        big_grads[names[1]][l] = _mm_tn_chunks(a, dq[None], 1408, 1024)[0].reshape(N_CHIPS, -1, D)
        dgu = _ffn_bwd_act(dq, wd, gu)
        C = dgu.shape[3]
        dgu4 = dgu.reshape(N_CHIPS, S, C)
        big_grads[names[0]][l] = _mm_tn_chunks(h, dgu4, 1024, C)
        dgu4 = reduce_group(names, l, big_grads[names[0]][l], dgu4)
        dh = _mm_nt_chunks(dgu4, wgu)
        dh = move_on(dh, dh)
        dx, dq, acc = _norm_bwd(x_in, dh, dx, f, gn, sc, 0.5, next_gate, next_scale)
        return dx, move_on(dx, dq), acc

    stepped = {n: None for n in BIG}
    reducing = []

    clock = [0]
    to_step = []

    def step_reduced(after):
        while to_step:
            name, l, g = to_step.pop(0)
            stepped[name] = _adamw_layer(W[name], g, given["m_" + name], given["v_" + name], l, stepped[name], after)
            after = stepped[name][1]
        return after

    def move_on(after, thru, force=False):
        clock[0] += 1
        for grp in list(reducing):
            if not force and clock[0] - grp["since"] < MIN_AGE[grp["step"]]:
                continue
            grp["since"] = clock[0]
            landed = grp["flight"].land(after)
            n = len(grp["names"])
            if grp["step"] == "swap":
                pairs = [_add_half(g4, r1, place) for g4, r1 in zip(landed[:n], landed[n:])]
                grp.update(step="scatter", flight=_scatter_flight(grp["tag"], [h for h, _ in pairs], [own for _, own in pairs], thru))
            elif grp["step"] == "scatter":
                grp.update(step="share", flight=_share_flight(grp["tag"], [_sum4_into_half(r2, place) for r2 in landed[n:]], thru))
            else:
                to_step.extend((name, grp["l"], fin.reshape(2 * fin.shape[1], fin.shape[2])) for name, fin in zip(grp["names"], landed))
                reducing.remove(grp)
                continue
            thru = grp["flight"].thru
        return thru

    def reduce_group(names, l, after, thru):
        thru = move_on(after, thru)
        g4s = [big_grads[n][l].reshape(N_CHIPS, 2, big_grads[n][l].shape[1] // 2, big_grads[n][l].shape[2]) for n in names]
        tag = f"{l}{GROUPS.index(names)}"
        reducing.append(dict(names=names, l=l, tag=tag, step="swap", since=clock[0], flight=_swap_flight(tag, g4s, thru)))
        return reducing[-1]["flight"].thru
    for l in reversed(range(L)):
        s = saved[l]
        lw, mp, md = s["lw"], s["mp"], s["md"]
        dx, dq, acc3 = ffn_bwd(
            GROUPS[2], l, dx, dq, s["x2"], s["h3"], s["a3"], s["gu3"], s["f3"], lw["gu2"], lw["d2"], ffn2_norm[l][None], md[7], md[5], 1.0)
        big_grads["w_out"][l] = _mm_tn_chunks(s["yn"], dq[None], 1024, 1024)[0].reshape(N_CHIPS, -1, D)
        dyn = _mm_nt_chunks(dq[None], lw["wout"][None])
        dylru, duv, dwsp, dbfull, gacc = _gmlp_bwd(s["proj"], s["ylru"], s["ygm"], dyn, mp["gv"], seg, mp["wsp"], mp["wspt"], mp["bfull"], mp["g_lru"], mp["g_gm"])
        dxg, dwa, dwx, lvec = _lru_bwd(s["proj"], dylru, mp["cw"], mp["cb"], mp["wa"], mp["ba"], mp["wx"], mp["bx"], mp["lam"], mp["wat"], mp["wxt"])
        dproj = jnp.concatenate([dxg, duv], axis=0)
        big_grads["w_in"][l] = _mm_tn_chunks(s["h2"], dproj, 1024, LW)
        dproj = reduce_group(GROUPS[1], l, big_grads["w_in"][l], dproj)
        dh2 = _mm_nt_chunks(dproj, lw["win"])
        dh2 = move_on(dh2, dh2)
        dx, dq, acc2 = _norm_bwd(s["x1"], dh2, dx, s["f2"], mix_norm[l][None], md[4], 1.0, md[2], 0.5)
        dq = move_on(dx, dq)
        if l > 0:
            ng, ns = saved[l - 1]["md"][8], 0.5
        else:
            ng, ns = zero_row, 0.0
        dx, dq, acc1 = ffn_bwd(
            GROUPS[0], l, dx, dq, s["x0"], s["h1"], s["a1"], s["gu1"], s["f1"], lw["gu1"], lw["d1"], ffn1_norm[l][None], md[1], ng, ns)

        dmods[l] = jnp.concatenate([acc1[0:2], acc1[3:4], acc2[0:2], acc2[3:4], acc3[0:2], acc3[3:4]], axis=0)
        dws = jnp.stack([dwsp[:, :, :CHUNK], dwsp[:, :, CHUNK:]], axis=1).reshape(HEADS, CHUNK, CHUNK) * tril
        lg = {"ffn1_norm": acc1[2], "mix_norm": acc2[2], "ffn2_norm": acc3[2],
              "conv_w": lvec[4:8], "conv_b": lvec[3], "gate_a_w": _unpair_blocks(dwa), "gate_a_b": lvec[0].reshape(HEADS, hd),
              "gate_x_w": _unpair_blocks(dwx), "gate_x_b": lvec[1].reshape(HEADS, hd), "lru_lambda": lvec[2], "v_norm": gacc[2],
              "spatial_w": dws, "spatial_b": dbfull.reshape(CHUNK, HEADS, hd).sum(-1).T, "lru_out_norm": gacc[0], "gmlp_out_norm": gacc[1]}
        for n, g in lg.items():
            small_grads.setdefault(n, [None] * L)[l] = g

    grad_x = dx.reshape(x.shape)

    per_layer = [n for n in SMALL if n not in ("b_ada", "final_norm")]
    part = [jnp.stack(small_grads[n]) for n in per_layer] + [head_acc[0], jnp.stack(dmods)]
    packer = _Packer([p.shape for p in part])
    packed = packer.pack(part)
    exchange = _exchange_flight(lax.dynamic_update_index_in_dim(jnp.zeros((N_DEV,) + packed.shape, F32), packed, dev, 0), dq)
    dq = exchange.thru
    done = step_reduced(dq)
    while reducing:
        dq = move_on(done, dq, force=True)
        done = step_reduced(dq)
    gathered, = exchange.land(done)
    summed = packer.unpack(_sum_leading(gathered))
    grads = dict(zip(per_layer + ["final_norm"], summed[:-1]))
    grads["b_ada"] = summed[-1].reshape(L, N_MOD * D)
    off = sum(packer.sizes[:-1])
    dmod_rows = gathered.reshape(N_DEV, -1)[:, off:off + L * N_MOD * D].reshape(N_DEV, L, N_MOD * D)
    dmod_shard = lax.dynamic_slice_in_dim(dmod_rows, chip * n_ada, n_ada, axis=2).transpose(1, 0, 2)
    grads["w_ada"] = _ada_grad(c_all.T, dmod_shard)
    grads["conv_w"] = lax.dynamic_slice_in_dim(grads["conv_w"], chip * cws, cws, axis=2)

    delta, new_m, new_v = {}, {}, {}
    for n in BIG:
        grads[n], delta[n], new_m[n], new_v[n] = stepped[n]
    shp = w_ada.shape
    d_, m_, v_ = _adamw(*[a.reshape(-1, shp[-1]) for a in (w_ada, grads["w_ada"], m_w_ada, v_w_ada)])
    delta["w_ada"], new_m["w_ada"], new_v["w_ada"] = d_.reshape(shp), m_.reshape(shp), v_.reshape(shp)
    spk = _Packer([W[n].shape for n in SMALL])
    d_, m_, v_ = _adamw(spk.pack([W[n] for n in SMALL]), spk.pack([grads[n] for n in SMALL]),
                        spk.pack([given["m_" + n] for n in SMALL]), spk.pack([given["v_" + n] for n in SMALL]))
    for n, a, b, e in zip(SMALL, spk.unpack(d_), spk.unpack(m_), spk.unpack(v_)):
        delta[n], new_m[n], new_v[n] = a, b, e
    grads = {n: grads[n].reshape(W[n].shape) for n in WEIGHTS}
    return (loss, grad_x, *[grads[n] for n in WEIGHTS], *[delta[n] for n in WEIGHTS], *[new_m[n] for n in WEIGHTS], *[new_v[n] for n in WEIGHTS])
```

```python
import math

import jax
import jax.numpy as jnp
from jax import lax
from jax.experimental import pallas as pl
from jax.experimental.pallas import tpu as pltpu

F32 = jnp.float32
MXU_DTYPE = jnp.bfloat16
ACT_DTYPE = jnp.bfloat16
XFER_DTYPE = jnp.bfloat16
EPS = 1e-6
RG_LRU_C = 8.0
N_MOD = 9
CONV_WIDTH = 4
HEADS = 8
CHUNK = 128
LANES = 128
SUBLANES = 8
N_CHIPS = 4
N_DEV = 8
ADAM_LR, ADAM_B1, ADAM_B2, ADAM_EPS, ADAM_WD, ADAM_STEP = 0.001, 0.9, 0.999, 1e-08, 0.01, 10
VMEM_LIMIT_BYTES = 60 * 1024 * 1024
ROW_TILE_BYTES = 2 << 20
GELU_C = math.sqrt(2.0 / math.pi)
GELU_A = 0.044715

ANY = pl.BlockSpec(memory_space=pl.ANY)
MESH = pl.DeviceIdType.MESH
SDS = jax.ShapeDtypeStruct


def _params(*sem):
    return pltpu.CompilerParams(dimension_semantics=sem, vmem_limit_bytes=VMEM_LIMIT_BYTES)


def _dot(a, b):
    return jnp.dot(a.astype(MXU_DTYPE), b.astype(MXU_DTYPE), preferred_element_type=F32)


def _dot_nt(a, b):
    return lax.dot_general(a.astype(MXU_DTYPE), b.astype(MXU_DTYPE), (((1,), (1,)), ((), ())), preferred_element_type=F32)


def _dot_tn(a, b):
    return lax.dot_general(a.astype(MXU_DTYPE), b.astype(MXU_DTYPE), (((0,), (0,)), ((), ())), preferred_element_type=F32)


def _gelu(x):
    return x * (0.5 * (1.0 + jnp.tanh(GELU_C * (x + GELU_A * (x * x * x)))))


def _gelu_grad(x):
    t = jnp.tanh(GELU_C * (x + GELU_A * (x * x * x)))
    return 0.5 * (1.0 + t) + 0.5 * x * (1.0 - t * t) * (GELU_C * (1.0 + 3.0 * GELU_A * x * x))


def _sigmoid(x):
    return jax.nn.sigmoid(x)


def _rsqrt_ms(x):
    return lax.rsqrt(jnp.mean(x * x, axis=-1, keepdims=True) + EPS)


def _rowsum(x):
    return jnp.sum(x, axis=0, keepdims=True)


def _tile(n, want):
    t = min(n, want)
    assert n % t == 0, (n, want)
    return t


def _row_tile(rows, row_bytes):
    step = 2 * SUBLANES
    cap = max(step, ROW_TILE_BYTES // row_bytes)
    best = None
    for t in range(step, min(rows, cap) + 1, step):
        if rows % t == 0:
            best = t
    assert best is not None, (rows, row_bytes)
    return best


def _modnorm(x, gn, sh, sc):
    S, D = x.shape
    tm = _tile(S, 1024)

    def body(x_ref, gn_ref, sh_ref, sc_ref, h_ref):
        xv = x_ref[...]
        h = (xv * _rsqrt_ms(xv) * gn_ref[...]) * (1.0 + sc_ref[...]) + sh_ref[...]
        h_ref[...] = h.astype(ACT_DTYPE)

    row = pl.BlockSpec((1, D), lambda i: (0, 0))
    return pl.pallas_call(
        body, name="modnorm", grid=(S // tm,),
        in_specs=[pl.BlockSpec((tm, D), lambda i: (i, 0)), row, row, row],
        out_specs=pl.BlockSpec((tm, D), lambda i: (i, 0)),
        out_shape=SDS((S, D), ACT_DTYPE), compiler_params=_params("parallel"),
    )(x, gn, sh, sc)


def _norm_bwd(x, dh, dxo, f, gn, sc, res_scale, next_gate, next_scale):
    S, D = x.shape
    tm = _tile(S, 512)

    def body(x_ref, dh_ref, dxo_ref, f_ref, gn_ref, sc_ref, ng_ref, dx_ref, dq_ref, acc_ref):
        @pl.when(pl.program_id(0) == 0)
        def _():
            acc_ref[...] = jnp.zeros_like(acc_ref)

        xv, dh, dxo = x_ref[...], dh_ref[...], dxo_ref[...]
        r = _rsqrt_ms(xv)
        xhat = xv * r
        gn = gn_ref[...]
        dn = dh * (1.0 + sc_ref[...])
        dxh = dn * gn
        dx = dxo + r * (dxh - xhat * jnp.mean(dxh * xhat, axis=-1, keepdims=True))
        dx_ref[...] = dx
        dq_ref[...] = ((next_scale * ng_ref[...]) * dx).astype(ACT_DTYPE)
        acc_ref[0:1, :] += _rowsum(dh)
        acc_ref[1:2, :] += _rowsum(dh * (xhat * gn))
        acc_ref[2:3, :] += _rowsum(dn * xhat)
        acc_ref[3:4, :] += _rowsum((res_scale * f_ref[...]) * dxo)

    tile = pl.BlockSpec((tm, D), lambda i: (i, 0))
    row = pl.BlockSpec((1, D), lambda i: (0, 0))
    return pl.pallas_call(
        body, name="norm_bwd", grid=(S // tm,),
        in_specs=[tile, tile, tile, tile, row, row, row],
        out_specs=[tile, tile, pl.BlockSpec((SUBLANES, D), lambda i: (0, 0))],
        out_shape=[SDS((S, D), F32), SDS((S, D), ACT_DTYPE), SDS((SUBLANES, D), F32)],
        compiler_params=_params("arbitrary"),
    )(x, dh, dxo, f, gn, sc, next_gate)


def _loss_head(x, target, gn, next_gate, next_scale):
    S, D = x.shape
    tm = _tile(S, 512)

    def body(x_ref, t_ref, gn_ref, ng_ref, dx_ref, dq_ref, acc_ref):
        @pl.when(pl.program_id(0) == 0)
        def _():
            acc_ref[...] = jnp.zeros_like(acc_ref)

        xv = x_ref[...]
        r = _rsqrt_ms(xv)
        xhat = xv * r
        gn = gn_ref[...]
        err = xhat * gn - t_ref[...]
        dy = err * (1.0 / D)
        dxh = dy * gn
        dx = r * (dxh - xhat * jnp.mean(dxh * xhat, axis=-1, keepdims=True))
        dx_ref[...] = dx
        dq_ref[...] = ((next_scale * ng_ref[...]) * dx).astype(ACT_DTYPE)
        acc_ref[0:1, :] += _rowsum(dy * xhat)
        acc_ref[1:2, :] += _rowsum(err * err) * (0.5 / D)

    tile = pl.BlockSpec((tm, D), lambda i: (i, 0))
    row = pl.BlockSpec((1, D), lambda i: (0, 0))
    return pl.pallas_call(
        body, name="loss_head", grid=(S // tm,),
        in_specs=[tile, tile, row, row],
        out_specs=[tile, tile, pl.BlockSpec((SUBLANES, D), lambda i: (0, 0))],
        out_shape=[SDS((S, D), F32), SDS((S, D), ACT_DTYPE), SDS((SUBLANES, D), F32)],
        compiler_params=_params("arbitrary"),
    )(x, target, gn, next_gate)


def _ffn_up(h, wgu):
    S, D = h.shape
    C = wgu.shape[2]
    tm = _tile(S, 512)

    def body(h_ref, wg_ref, wu_ref, a_ref, gu_ref):
        hv = h_ref[...]
        g = _dot(hv, wg_ref[...])
        u = _dot(hv, wu_ref[...])
        a_ref[...] = (g * _sigmoid(g) * u).astype(ACT_DTYPE)
        gu_ref[0] = g.astype(ACT_DTYPE)
        gu_ref[1] = u.astype(ACT_DTYPE)

    return pl.pallas_call(
        body, name="ffn_up", grid=(2, S // tm),
        in_specs=[
            pl.BlockSpec((tm, D), lambda j, i: (i, 0)),
            pl.BlockSpec((None, D, C), lambda j, i: (j, 0, 0)),
            pl.BlockSpec((None, D, C), lambda j, i: (2 + j, 0, 0)),
        ],
        out_specs=[
            pl.BlockSpec((tm, C), lambda j, i: (i, j)),
            pl.BlockSpec((2, None, tm, C), lambda j, i: (0, j, i, 0)),
        ],
        out_shape=[SDS((S, 2 * C), ACT_DTYPE), SDS((2, 2, S, C), ACT_DTYPE)],
        compiler_params=_params("parallel", "parallel"),
    )(h, wgu, wgu)


def _ffn_bwd_act(dq, wd, gu):
    S, D = dq.shape
    C = gu.shape[3]
    tm = _tile(S, 512)

    def body(dq_ref, wd_ref, gu_ref, dgu_ref):
        da = _dot_nt(dq_ref[...], wd_ref[...])
        g = gu_ref[0].astype(F32)
        u = gu_ref[1].astype(F32)
        s = _sigmoid(g)
        dgu_ref[0] = (da * u * (s * (1.0 + g * (1.0 - s)))).astype(ACT_DTYPE)
        dgu_ref[1] = (da * (g * s)).astype(ACT_DTYPE)

    gu_spec = pl.BlockSpec((2, None, tm, C), lambda j, i: (0, j, i, 0))
    return pl.pallas_call(
        body, name="ffn_bwd_act", grid=(2, S // tm),
        in_specs=[pl.BlockSpec((tm, D), lambda j, i: (i, 0)), pl.BlockSpec((C, D), lambda j, i: (j, 0)), gu_spec],
        out_specs=gu_spec,
        out_shape=SDS(gu.shape, ACT_DTYPE),
        compiler_params=_params("parallel", "parallel"),
    )(dq, wd, gu)


def _mm_res(a, w, x, gate, scale):
    S, K = a.shape
    D = w.shape[1]
    tm, tn = _tile(S, 512), _tile(D, 1024)

    def body(a_ref, w_ref, x_ref, g_ref, f_ref, xo_ref):
        f = _dot(a_ref[...], w_ref[...])
        f_ref[...] = f
        xo_ref[...] = x_ref[...] + (scale * g_ref[...]) * f

    tile = pl.BlockSpec((tm, tn), lambda j, i: (i, j))
    return pl.pallas_call(
        body, name=f"mm_res_k{K}", grid=(D // tn, S // tm),
        in_specs=[pl.BlockSpec((tm, K), lambda j, i: (i, 0)), pl.BlockSpec((K, tn), lambda j, i: (0, j)), tile,
                  pl.BlockSpec((1, tn), lambda j, i: (0, j))],
        out_specs=[tile, tile],
        out_shape=[SDS((S, D), F32), SDS((S, D), F32)],
        compiler_params=_params("parallel", "parallel"),
    )(a, w, x, gate)


def _mm_chunks(h, wc):
    S, K = h.shape
    P, _, N = wc.shape
    tm = _tile(S, 512)

    def body(h_ref, w_ref, o_ref):
        hv = h_ref[...]
        for p in range(P):
            o_ref[:, p * N:(p + 1) * N] = _dot(hv, w_ref[p])

    return pl.pallas_call(
        body, name="mm_chunks", grid=(S // tm,),
        in_specs=[pl.BlockSpec((tm, K), lambda i: (i, 0)), pl.BlockSpec((P, K, N), lambda i: (0, 0, 0))],
        out_specs=pl.BlockSpec((tm, P * N), lambda i: (i, 0)),
        out_shape=SDS((S, P * N), F32),
        compiler_params=_params("parallel"),
    )(h, wc)


def _mm_nt_chunks(ac, wc):
    P, S, K = ac.shape
    N = wc.shape[1]
    tm, tn = _tile(S, 512), _tile(N, 1024)

    def body(a_ref, w_ref, o_ref):
        acc = _dot_nt(a_ref[0], w_ref[0])
        for p in range(1, P):
            acc += _dot_nt(a_ref[p], w_ref[p])
        o_ref[...] = acc

    return pl.pallas_call(
        body, name=f"mm_nt_p{P}k{K}", grid=(S // tm, N // tn),
        in_specs=[pl.BlockSpec((P, tm, K), lambda i, j: (0, i, 0)), pl.BlockSpec((P, tn, K), lambda i, j: (0, j, 0))],
        out_specs=pl.BlockSpec((tm, tn), lambda i, j: (i, j)),
        out_shape=SDS((S, N), F32),
        compiler_params=_params("parallel", "parallel"),
    )(ac, wc)


def _mm_tn_chunks(a, bc, tile_m, tile_n):
    S, M = a.shape
    P, _, N = bc.shape
    ts, tm, tn = _tile(S, 2048), _tile(M, tile_m), _tile(N, tile_n)

    def body(a_ref, b_ref, o_ref):
        @pl.when(pl.program_id(3) == 0)
        def _():
            o_ref[...] = jnp.zeros_like(o_ref)

        o_ref[...] += _dot_tn(a_ref[...], b_ref[...])

    return pl.pallas_call(
        body, name=f"mm_tn_m{M}n{N}", grid=(P, M // tm, N // tn, S // ts),
        in_specs=[pl.BlockSpec((ts, tm), lambda p, m, n, k: (k, m)), pl.BlockSpec((None, ts, tn), lambda p, m, n, k: (p, k, n))],
        out_specs=pl.BlockSpec((None, tm, tn), lambda p, m, n, k: (p, m, n)),
        out_shape=SDS((P, M, N), F32),
        compiler_params=_params("parallel", "parallel", "parallel", "arbitrary"),
    )(a, bc)


def _shift_down(x, s, row, fill):
    return jnp.where(row >= s, pltpu.roll(x, s, 0), fill)


def _shift_up(x, s, row, fill):
    n = x.shape[0]
    return jnp.where(row < n - s, pltpu.roll(x, n - s, 0), fill)


def _scan(a, b, row, scratch, up):
    scr_a, scr_b, scr_c = scratch
    n = a.shape[0]
    g = n // SUBLANES
    in_group = row & (SUBLANES - 1)

    def steps(a, b, pos, size):
        s = 1
        while s < size:
            m = (pos + s < size) if up else (pos >= s)
            b = jnp.where(m, a, 0.0) * pltpu.roll(b, a.shape[0] - s if up else s, 0) + b
            a = jnp.where(m, a * pltpu.roll(a, a.shape[0] - s if up else s, 0), a)
            s *= 2
        return a, b

    a, b = steps(a, b, in_group, SUBLANES)
    scr_a[...] = a
    scr_b[...] = b
    edge = 0 if up else SUBLANES - 1
    at = scr_a[pl.ds(edge, g, stride=SUBLANES), :]
    bt = scr_b[pl.ds(edge, g, stride=SUBLANES), :]
    group = lax.broadcasted_iota(jnp.int32, at.shape, 0)
    _, state = steps(at, bt, group, g)
    carry = jnp.where((group + 1 < g) if up else (group >= 1), pltpu.roll(state, g - 1 if up else 1, 0), 0.0)
    for k in range(SUBLANES):
        scr_c[pl.ds(k, g, stride=SUBLANES), :] = carry
    return b + a * scr_c[...]


def _conv(xl, cw_ref, cb_ref, row):
    y = cb_ref[...] + _shift_down(xl, 3, row, 0.0) * cw_ref[0:1, :]
    y = y + _shift_down(xl, 2, row, 0.0) * cw_ref[1:2, :]
    y = y + _shift_down(xl, 1, row, 0.0) * cw_ref[2:3, :]
    return y + xl * cw_ref[3:4, :]


def _lru_gates(xc, wa_ref, ba_ref, wx_ref, bx_ref, lam_ref):
    ra = _sigmoid(_dot(xc, wa_ref[...]) + ba_ref[...])
    ri = _sigmoid(_dot(xc, wx_ref[...]) + bx_ref[...])
    ls = jax.nn.log_sigmoid(lam_ref[...])
    a = jnp.exp((RG_LRU_C * ra) * ls)
    mult = jnp.sqrt(1.0 - a * a)
    return ra, ri, ls, a, mult


def _lru_specs(S):
    col = lambda off: pl.BlockSpec((S, LANES), lambda j: (0, off + j))
    vec = pl.BlockSpec((1, LANES), lambda j: (0, j))
    blk = pl.BlockSpec((None, LANES, LANES), lambda j: (j, 0, 0))
    cw = pl.BlockSpec((CONV_WIDTH, LANES), lambda j: (0, j))
    return col, vec, blk, cw


def _lru_fwd(proj, cw, cb, wa, ba, wx, bx, lam):
    S = proj.shape[0]
    W = cb.shape[1]
    nb = W // LANES

    def body(xl_ref, gl_ref, cw_ref, cb_ref, wa_ref, ba_ref, wx_ref, bx_ref, lam_ref, y_ref, *scratch):
        row = lax.broadcasted_iota(jnp.int32, (S, LANES), 0)
        xc = _conv(xl_ref[...], cw_ref, cb_ref, row)
        _, ri, _, a, mult = _lru_gates(xc, wa_ref, ba_ref, wx_ref, bx_ref, lam_ref)
        h = _scan(a, mult * (ri * xc), row, scratch, up=False)
        y_ref[...] = h * _gelu(gl_ref[...])

    col, vec, blk, cws = _lru_specs(S)
    return pl.pallas_call(
        body, name="lru_fwd", grid=(nb,),
        in_specs=[col(0), col(nb), cws, vec, blk, vec, blk, vec, vec],
        out_specs=pl.BlockSpec((S, LANES), lambda j: (0, j)),
        out_shape=SDS((S, W), F32), scratch_shapes=[pltpu.VMEM((S, LANES), F32)] * 3, compiler_params=_params("parallel"),
    )(proj, proj, cw, cb, wa, ba, wx, bx, lam)


def _lru_bwd(proj, dy, cw, cb, wa, ba, wx, bx, lam, wat, wxt):
    S = proj.shape[0]
    W = cb.shape[1]
    nb = W // LANES

    def body(xl_ref, gl_ref, dy_ref, cw_ref, cb_ref, wa_ref, ba_ref, wx_ref, bx_ref, lam_ref, wat_ref, wxt_ref,
             dp_ref, dwa_ref, dwx_ref, vec_ref, *scratch):
        row = lax.broadcasted_iota(jnp.int32, (S, LANES), 0)
        xl = xl_ref[...]
        xc = _conv(xl, cw_ref, cb_ref, row)
        ra, ri, ls, a, mult = _lru_gates(xc, wa_ref, ba_ref, wx_ref, bx_ref, lam_ref)
        h = _scan(a, mult * (ri * xc), row, scratch, up=False)
        gl = gl_ref[...]
        dyv = dy_ref[...]
        dp_ref[1] = (dyv * h * _gelu_grad(gl)).astype(ACT_DTYPE)
        adj = _scan(_shift_up(a, 1, row, 0.0), dyv * _gelu(gl), row, scratch, up=True)
        da = adj * _shift_down(h, 1, row, 0.0)
        dmult = adj * (ri * xc)
        dlog_a = da * a - dmult * (a * a) / mult
        dra = dlog_a * (RG_LRU_C * ls)
        dpa = dra * ra * (1.0 - ra)
        dpi = (adj * mult * xc) * ri * (1.0 - ri)
        dxc = adj * mult * ri + _dot(dpa, wat_ref[...]) + _dot(dpi, wxt_ref[...])
        dwa_ref[...] = _dot_tn(xc, dpa)
        dwx_ref[...] = _dot_tn(xc, dpi)
        dxl = dxc * cw_ref[3:4, :]
        dxl = dxl + _shift_up(dxc, 1, row, 0.0) * cw_ref[2:3, :]
        dxl = dxl + _shift_up(dxc, 2, row, 0.0) * cw_ref[1:2, :]
        dxl = dxl + _shift_up(dxc, 3, row, 0.0) * cw_ref[0:1, :]
        dp_ref[0] = dxl.astype(ACT_DTYPE)
        vec_ref[...] = jnp.zeros_like(vec_ref)
        vec_ref[0:1, :] = _rowsum(dpa)
        vec_ref[1:2, :] = _rowsum(dpi)
        vec_ref[2:3, :] = _rowsum(dlog_a * (RG_LRU_C * ra)) * _sigmoid(-lam_ref[...])
        vec_ref[3:4, :] = _rowsum(dxc)
        vec_ref[4:5, :] = _rowsum(dxc * _shift_down(xl, 3, row, 0.0))
        vec_ref[5:6, :] = _rowsum(dxc * _shift_down(xl, 2, row, 0.0))
        vec_ref[6:7, :] = _rowsum(dxc * _shift_down(xl, 1, row, 0.0))
        vec_ref[7:8, :] = _rowsum(dxc * xl)

    col, vec, blk, cws = _lru_specs(S)
    return pl.pallas_call(
        body, name="lru_bwd", grid=(nb,),
        in_specs=[col(0), col(nb), col(0), cws, vec, blk, vec, blk, vec, vec, blk, blk],
        out_specs=[pl.BlockSpec((2, S, LANES), lambda j: (0, 0, j)), blk, blk, pl.BlockSpec((2 * SUBLANES, LANES), lambda j: (0, j))],
        out_shape=[SDS((2, S, W), ACT_DTYPE), SDS((nb, LANES, LANES), F32), SDS((nb, LANES, LANES), F32), SDS((2 * SUBLANES, W), F32)],
        scratch_shapes=[pltpu.VMEM((S, LANES), F32)] * 3, compiler_params=_params("parallel"),
    )(proj, proj, dy, cw, cb, wa, ba, wx, bx, lam, wat, wxt)


def _seg_mean(x, seg_ref, width):
    hi = x.astype(jnp.bfloat16)
    lo = (x - hi.astype(F32)).astype(jnp.bfloat16)
    ones = seg_ref[...]
    s = jnp.dot(hi, ones, preferred_element_type=F32) + jnp.dot(lo, ones, preferred_element_type=F32)
    return s * (1.0 / width)


def _gmlp_core(u_ref, v_ref, gv_ref, seg_ref, ws_ref, bfull_ref, z_scr, hd):
    tm, W = u_ref.shape
    lane = lax.broadcasted_iota(jnp.int32, (CHUNK, LANES), 1)
    ug = _gelu(u_ref[...])
    vg = _gelu(v_ref[...])
    cen = vg - _seg_mean(vg, seg_ref, hd)
    rstd = lax.rsqrt(_seg_mean(cen * cen, seg_ref, hd) + EPS)
    vhat = cen * rstd
    vh = vhat * gv_ref[...]
    vcats = {}
    for ci in range(tm // CHUNK):
        for p in range(W // LANES):
            blk = vh[ci * CHUNK:(ci + 1) * CHUNK, p * LANES:(p + 1) * LANES]
            vcat = jnp.concatenate([jnp.where(lane < hd, blk, 0.0), jnp.where(lane >= hd, blk, 0.0)], axis=0).astype(MXU_DTYPE)
            vcats[ci, p] = vcat
            z_scr[ci * CHUNK:(ci + 1) * CHUNK, p * LANES:(p + 1) * LANES] = (
                jnp.dot(ws_ref[p], vcat, preferred_element_type=F32) + bfull_ref[:, p * LANES:(p + 1) * LANES])
    return ug, vhat, rstd, vcats


def _gmlp_specs(tm, W, nb):
    rows = lambda off: pl.BlockSpec((tm, W), lambda i: (i, off))
    vec = pl.BlockSpec((1, W), lambda i: (0, 0))
    seg = pl.BlockSpec((W, W), lambda i: (0, 0))
    wsp = pl.BlockSpec((nb, CHUNK, 2 * CHUNK), lambda i: (0, 0, 0))
    bfull = pl.BlockSpec((CHUNK, W), lambda i: (0, 0))
    return rows, vec, seg, wsp, bfull


def _gmlp_fwd(proj, ylru, gv, seg, wsp, bfull, g_lru, g_gm):
    S, W = ylru.shape
    nb = W // LANES
    hd = W // HEADS
    tm = _tile(S, 512)

    def body(u_ref, v_ref, yl_ref, gv_ref, seg_ref, ws_ref, bfull_ref, gl_ref, gg_ref, yn_ref, ygm_ref, z_scr):
        ug, _, _, _ = _gmlp_core(u_ref, v_ref, gv_ref, seg_ref, ws_ref, bfull_ref, z_scr, hd)
        ygm = ug * z_scr[...]
        ygm_ref[...] = ygm
        yl = yl_ref[...]
        yn_ref[:, 0:W] = (yl * _rsqrt_ms(yl) * gl_ref[...]).astype(ACT_DTYPE)
        yn_ref[:, W:2 * W] = (ygm * _rsqrt_ms(ygm) * gg_ref[...]).astype(ACT_DTYPE)

    rows, vec, segs, wsps, bfulls = _gmlp_specs(tm, W, nb)
    return pl.pallas_call(
        body, name="gmlp_fwd", grid=(S // tm,),
        in_specs=[rows(2), rows(3), rows(0), vec, segs, wsps, bfulls, vec, vec],
        out_specs=[pl.BlockSpec((tm, 2 * W), lambda i: (i, 0)), rows(0)],
        out_shape=[SDS((S, 2 * W), ACT_DTYPE), SDS((S, W), F32)],
        scratch_shapes=[pltpu.VMEM((tm, W), F32)],
        compiler_params=_params("parallel"),
    )(proj, proj, ylru, gv, seg, wsp, bfull, g_lru, g_gm)


def _rms_bwd(y, g, dyn):
    r = _rsqrt_ms(y)
    yhat = y * r
    dyh = dyn * g
    return r * (dyh - yhat * jnp.mean(dyh * yhat, axis=-1, keepdims=True)), _rowsum(dyn * yhat)


def _gmlp_bwd(proj, ylru, ygm, dyn, gv, seg, wsp, wspt, bfull, g_lru, g_gm):
    S, W = ylru.shape
    nb = W // LANES
    hd = W // HEADS
    tm = _tile(S, 256)

    def body(u_ref, v_ref, yl_ref, ygm_ref, dl_ref, dg_ref, gv_ref, seg_ref, ws_ref, wst_ref, bfull_ref, gl_ref, gg_ref,
             dyl_ref, duv_ref, dws_ref, dbf_ref, acc_ref, z_scr, dvh_scr):
        @pl.when(pl.program_id(0) == 0)
        def _():
            dws_ref[...] = jnp.zeros_like(dws_ref)
            dbf_ref[...] = jnp.zeros_like(dbf_ref)
            acc_ref[...] = jnp.zeros_like(acc_ref)

        dyl, dgl = _rms_bwd(yl_ref[...], gl_ref[...], dl_ref[...])
        dyl_ref[...] = dyl
        dygm, dgg = _rms_bwd(ygm_ref[...], gg_ref[...], dg_ref[...])
        ug, vhat, rstd, vcats = _gmlp_core(u_ref, v_ref, gv_ref, seg_ref, ws_ref, bfull_ref, z_scr, hd)
        duv_ref[0] = (dygm * z_scr[...] * _gelu_grad(u_ref[...])).astype(ACT_DTYPE)
        dz = dygm * ug
        lane = lax.broadcasted_iota(jnp.int32, (CHUNK, LANES), 1)
        dbf = dz[0:CHUNK, :]
        for ci in range(1, tm // CHUNK):
            dbf += dz[ci * CHUNK:(ci + 1) * CHUNK, :]
        dbf_ref[...] += dbf
        for ci in range(tm // CHUNK):
            for p in range(nb):
                dzb = dz[ci * CHUNK:(ci + 1) * CHUNK, p * LANES:(p + 1) * LANES].astype(MXU_DTYPE)
                dws_ref[p] += _dot_nt(dzb, vcats[ci, p])
                dvc = jnp.dot(wst_ref[p], dzb, preferred_element_type=F32)
                dvh_scr[ci * CHUNK:(ci + 1) * CHUNK, p * LANES:(p + 1) * LANES] = jnp.where(lane < hd, dvc[0:CHUNK], dvc[CHUNK:2 * CHUNK])
        dvh = dvh_scr[...]
        dvn = dvh * gv_ref[...]
        dvg = rstd * (dvn - _seg_mean(dvn, seg_ref, hd) - vhat * _seg_mean(dvn * vhat, seg_ref, hd))
        duv_ref[1] = (dvg * _gelu_grad(v_ref[...])).astype(ACT_DTYPE)
        acc_ref[0:1, :] += dgl
        acc_ref[1:2, :] += dgg
        acc_ref[2:3, :] += _rowsum(dvh * vhat)

    rows, vec, segs, wsps, bfulls = _gmlp_specs(tm, W, nb)
    wspt_spec = pl.BlockSpec((nb, 2 * CHUNK, CHUNK), lambda i: (0, 0, 0))
    return pl.pallas_call(
        body, name="gmlp_bwd", grid=(S // tm,),
        in_specs=[rows(2), rows(3), rows(0), rows(0), rows(0), rows(1), vec, segs, wsps, wspt_spec, bfulls, vec, vec],
        out_specs=[rows(0), pl.BlockSpec((2, tm, W), lambda i: (0, i, 0)), wsps, bfulls, pl.BlockSpec((SUBLANES, W), lambda i: (0, 0))],
        out_shape=[SDS((S, W), F32), SDS((2, S, W), ACT_DTYPE), SDS((nb, CHUNK, 2 * CHUNK), F32), SDS((CHUNK, W), F32), SDS((SUBLANES, W), F32)],
        scratch_shapes=[pltpu.VMEM((tm, W), F32), pltpu.VMEM((tm, W), F32)],
        compiler_params=_params("arbitrary"),
    )(proj, proj, ylru, ygm, dyn, dyn, gv, seg, wsp, wspt, bfull, g_lru, g_gm)


def _ada_fwd(c_all, w_ada, b_shard):
    L, D, N = w_ada.shape
    R = c_all.shape[0]
    tn = N // 2

    def body(c_ref, w_ref, b_ref, o_ref):
        cv = c_ref[...]
        o_ref[...] = _dot(cv * _sigmoid(cv), w_ref[...]) + b_ref[...]

    return pl.pallas_call(
        body, name="ada_fwd", grid=(L, N // tn),
        in_specs=[pl.BlockSpec((R, D), lambda l, j: (0, 0)), pl.BlockSpec((None, D, tn), lambda l, j: (l, 0, j)),
                  pl.BlockSpec((None, 1, tn), lambda l, j: (l, 0, j))],
        out_specs=pl.BlockSpec((None, R, tn), lambda l, j: (l, 0, j)),
        out_shape=SDS((L, R, N), F32), compiler_params=_params("parallel", "parallel"),
    )(c_all, w_ada, b_shard)


def _ada_grad(c_all_t, dmod):
    D, B = c_all_t.shape
    L, _, N = dmod.shape
    tn = N // 2

    def body(c_ref, d_ref, o_ref):
        cv = c_ref[...]
        sc = cv * _sigmoid(cv)
        acc = sc[:, 0:1] * d_ref[0:1, :]
        for b in range(1, B):
            acc += sc[:, b:b + 1] * d_ref[b:b + 1, :]
        o_ref[...] = acc

    return pl.pallas_call(
        body, name="ada_grad", grid=(L, N // tn),
        in_specs=[pl.BlockSpec((D, B), lambda l, j: (0, 0)), pl.BlockSpec((None, B, tn), lambda l, j: (l, 0, j))],
        out_specs=pl.BlockSpec((None, D, tn), lambda l, j: (l, 0, j)),
        out_shape=SDS((L, D, N), F32), compiler_params=_params("parallel", "parallel"),
    )(c_all_t, dmod)


def _adamw(w, g, m, v):
    R, C = w.shape
    tr = _row_tile(R, C * 4)

    def body(w_ref, g_ref, m_ref, v_ref, d_ref, mo_ref, vo_ref):
        d_ref[...], mo_ref[...], vo_ref[...] = _adam_math(w_ref[...], g_ref[...], m_ref[...], v_ref[...])

    tile = pl.BlockSpec((tr, C), lambda i: (i, 0))
    return pl.pallas_call(
        body, name=f"adamw_r{R}c{C}", grid=(R // tr,), in_specs=[tile] * 4, out_specs=[tile] * 3,
        out_shape=[SDS((R, C), F32)] * 3, compiler_params=_params("parallel"),
    )(w, g, m, v)


def _adam_math(w, g, m, v):
    mn = ADAM_B1 * m + (1.0 - ADAM_B1) * g
    vn = ADAM_B2 * v + (1.0 - ADAM_B2) * (g * g)
    m_hat = mn / (1.0 - ADAM_B1 ** ADAM_STEP)
    v_hat = vn / (1.0 - ADAM_B2 ** ADAM_STEP)
    return -ADAM_LR * (m_hat / (jnp.sqrt(v_hat) + ADAM_EPS) + ADAM_WD * w), mn, vn


def _adamw_layer(w, g, m, v, l, prev, after):
    L, R, C = w.shape
    tr = _row_tile(R, C * 4)
    prev = (after,) + (() if prev is None else tuple(prev))

    def body(w_ref, g_ref, m_ref, v_ref, *rest):
        go_ref, d_ref, mo_ref, vo_ref = rest[len(prev):]
        gv = g_ref[...]
        go_ref[...] = gv
        d_ref[...], mo_ref[...], vo_ref[...] = _adam_math(w_ref[...], gv, m_ref[...], v_ref[...])

    lay = pl.BlockSpec((None, tr, C), lambda i: (l, i, 0))
    return pl.pallas_call(
        body, name=f"adamw_layer_r{R}c{C}", grid=(R // tr,),
        in_specs=[lay, pl.BlockSpec((tr, C), lambda i: (i, 0)), lay, lay] + [ANY] * len(prev), out_specs=[lay] * 4,
        out_shape=[SDS((L, R, C), F32)] * 4, input_output_aliases={5 + k: k for k in range(len(prev) - 1)},
        compiler_params=_params("parallel"),
    )(w, g, m, v, *prev)


def _sum_leading(a):
    P, R, C = a.shape
    tr = _row_tile(R, P * C * 4)

    def body(a_ref, o_ref):
        acc = a_ref[0]
        for p in range(1, P):
            acc = acc + a_ref[p]
        o_ref[...] = acc

    return pl.pallas_call(
        body, name=f"sum{P}_r{R}c{C}", grid=(R // tr,),
        in_specs=[pl.BlockSpec((P, tr, C), lambda i: (0, i, 0))],
        out_specs=pl.BlockSpec((tr, C), lambda i: (i, 0)),
        out_shape=SDS((R, C), F32), compiler_params=_params("parallel"),
    )(a)


def _add_half(g4, r1, place):
    _, _, R, C = g4.shape
    tr = _row_tile(R, C * 4)

    def body(place_ref, g_ref, r_ref, h_ref, own_ref):
        s = (g_ref[...] + r_ref[...]).astype(XFER_DTYPE)
        h_ref[...] = s

        @pl.when(pl.program_id(1) == place_ref[1])
        def _():
            own_ref[...] = s

    return pl.pallas_call(
        body, name=f"add_half_r{R}c{C}",
        grid_spec=pltpu.PrefetchScalarGridSpec(
            num_scalar_prefetch=1, grid=(R // tr, N_CHIPS),
            in_specs=[pl.BlockSpec((None, None, tr, C), lambda i, p, place_ref: (p, place_ref[0], i, 0)),
                      pl.BlockSpec((None, tr, C), lambda i, p, place_ref: (p, i, 0))],
            out_specs=[pl.BlockSpec((None, tr, C), lambda i, p, place_ref: (p, i, 0)),
                       pl.BlockSpec((None, tr, C), lambda i, p, place_ref: (place_ref[1], i, 0))],
        ),
        out_shape=[SDS((N_CHIPS, R, C), XFER_DTYPE)] * 2, compiler_params=_params("parallel", "arbitrary"),
    )(place, g4, r1)


def _sum4_into_half(r2, place):
    P, R, C = r2.shape
    tr = _row_tile(R, P * C * 4)

    def body(place_ref, a_ref, o_ref):
        acc = a_ref[0].astype(F32)
        for p in range(1, P):
            acc = acc + a_ref[p].astype(F32)
        o_ref[...] = acc

    return pl.pallas_call(
        body, name=f"sum4_r{R}c{C}",
        grid_spec=pltpu.PrefetchScalarGridSpec(
            num_scalar_prefetch=1, grid=(R // tr,),
            in_specs=[pl.BlockSpec((P, tr, C), lambda i, place_ref: (0, i, 0))],
            out_specs=pl.BlockSpec((None, tr, C), lambda i, place_ref: (place_ref[0], i, 0)),
        ),
        out_shape=SDS((2, R, C), F32), compiler_params=_params("parallel"),
    )(place, r2)


def _cast_into_slot(w, l, place, after):
    _, R, C = w.shape
    tr = _row_tile(R, C * 4)

    def body(place_ref, w_ref, after_ref, o_ref):
        o_ref[...] = w_ref[...].astype(MXU_DTYPE)

    return pl.pallas_call(
        body, name=f"cast_r{R}c{C}",
        grid_spec=pltpu.PrefetchScalarGridSpec(
            num_scalar_prefetch=1, grid=(R // tr,),
            in_specs=[pl.BlockSpec((None, tr, C), lambda i, place_ref: (l, i, 0)), ANY],
            out_specs=pl.BlockSpec((None, tr, C), lambda i, place_ref: (place_ref[1], i, 0)),
        ),
        out_shape=SDS((N_CHIPS, R, C), MXU_DTYPE), compiler_params=_params("parallel"),
    )(place, w, after)


def _place():
    x, y, c = lax.axis_index("x"), lax.axis_index("y"), lax.axis_index("c")
    chips = [(1 - x, y), (x, 1 - y), (1 - x, 1 - y)]
    return x, y, c, chips


def _remote(src, dst, send_sem, recv_sem, to):
    return pltpu.make_async_remote_copy(src_ref=src, dst_ref=dst, send_sem=send_sem, recv_sem=recv_sem, device_id=to, device_id_type=MESH)


def _all_gather8(v):
    R, N = v.shape

    def body(v_ref, out_ref, send_sems, recv_sems, local_sem):
        x, y, c, chips = _place()
        me, sibling = (x, y, c), (x, y, 1 - c)

        def slot(px, py, pc):
            return out_ref.at[4 * px + 2 * py + pc]

        def copy(k, block, to, src=None):
            return _remote(slot(*block) if src is None else src, slot(*block), send_sems.at[k], recv_sems.at[k], to)

        mine = pltpu.make_async_copy(v_ref, slot(*me), local_sem)
        mine.start()
        first = [copy(0, me, sibling, src=v_ref)] + [copy(1 + j, me, (*chip, c), src=v_ref) for j, chip in enumerate(chips)]
        for cp in first:
            cp.start()
        passed = [copy(4 + j, (*chip, c), sibling) for j, chip in enumerate(chips)]
        for j, chip in enumerate(chips):
            copy(1 + j, (*chip, c), me).wait_recv()
            passed[j].start()
        copy(0, sibling, me).wait_recv()
        for j, chip in enumerate(chips):
            copy(4 + j, (*chip, 1 - c), me).wait_recv()
        for cp in first + passed:
            cp.wait_send()
        mine.wait()

    return pl.pallas_call(
        body, name=f"all_gather8_r{R}n{N}", out_shape=SDS((N_DEV, R, N), v.dtype), in_specs=[ANY], out_specs=ANY,
        scratch_shapes=[pltpu.SemaphoreType.DMA((7,)), pltpu.SemaphoreType.DMA((7,)), pltpu.SemaphoreType.DMA],
    )(v)


def _gather_weights(slots):
    n = len(slots)

    def body(*refs):
        ins, outs = refs[:n], refs[n:2 * n]
        send_sems, recv_sems = refs[2 * n:]
        x, y, c, chips = _place()
        q = 2 * x + y
        sibling = (x, y, 1 - c)
        first = []
        for k in range(n):
            for j, chip in enumerate(chips):
                first.append(_remote(ins[k].at[q, c], outs[k].at[q, c], send_sems.at[k, j], recv_sems.at[k, j], (*chip, c)))
                first[-1].start()
        passed = []
        for k in range(n):
            for j, chip in enumerate(chips):
                half = outs[k].at[2 * chip[0] + chip[1], c]
                _remote(half, half, send_sems.at[k, j], recv_sems.at[k, j], sibling).wait_recv()
                passed.append(_remote(half, half, send_sems.at[k, 3 + j], recv_sems.at[k, 3 + j], sibling))
                passed[-1].start()
        for k in range(n):
            for j, chip in enumerate(chips):
                half = outs[k].at[2 * chip[0] + chip[1], 1 - c]
                _remote(half, half, send_sems.at[k, 3 + j], recv_sems.at[k, 3 + j], sibling).wait_recv()
        for cp in first + passed:
            cp.wait_send()

    return pl.pallas_call(
        body, name="gather_weights", out_shape=[SDS(s.shape, s.dtype) for s in slots],
        in_specs=[ANY] * n, out_specs=[ANY] * n, input_output_aliases={k: k for k in range(n)},
        scratch_shapes=[pltpu.SemaphoreType.DMA((n, 6)), pltpu.SemaphoreType.DMA((n, 6))],
    )(*slots)


def _swap_halves(g4s):
    n = len(g4s)

    def body(*refs):
        ins, outs = refs[:n], refs[n:2 * n]
        send_sems, recv_sems = refs[2 * n:]
        x, y, c, _ = _place()
        sibling = (x, y, 1 - c)
        for k in range(n):
            for p in range(N_CHIPS):
                _remote(ins[k].at[p, 1 - c], outs[k].at[p], send_sems.at[k], recv_sems.at[k], sibling).start()
        for k in range(n):
            _remote(outs[k], outs[k], send_sems.at[k], recv_sems.at[k], sibling).wait()

    return pl.pallas_call(
        body, name="swap_halves", out_shape=[SDS((N_CHIPS,) + g.shape[2:], g.dtype) for g in g4s],
        in_specs=[ANY] * n, out_specs=[ANY] * n,
        scratch_shapes=[pltpu.SemaphoreType.DMA((n,)), pltpu.SemaphoreType.DMA((n,))],
    )(*g4s)


def _scatter_regions(hs, lands):
    n = len(hs)

    def body(*refs):
        ins, outs = refs[:n], refs[2 * n:3 * n]
        send_sems, recv_sems = refs[3 * n:]
        x, y, c, chips = _place()
        q = 2 * x + y
        sent = []
        for k in range(n):
            for j, chip in enumerate(chips):
                sent.append(_remote(ins[k].at[2 * chip[0] + chip[1]], outs[k].at[q], send_sems.at[k, j], recv_sems.at[k, j], (*chip, c)))
                sent[-1].start()
        for k in range(n):
            for j, chip in enumerate(chips):
                got = outs[k].at[2 * chip[0] + chip[1]]
                _remote(got, got, send_sems.at[k, j], recv_sems.at[k, j], (x, y, c)).wait_recv()
        for cp in sent:
            cp.wait_send()

    return pl.pallas_call(
        body, name="scatter_regions", out_shape=[SDS(h.shape, h.dtype) for h in lands],
        in_specs=[ANY] * (2 * n), out_specs=[ANY] * n, input_output_aliases={n + k: k for k in range(n)},
        scratch_shapes=[pltpu.SemaphoreType.DMA((n, 3)), pltpu.SemaphoreType.DMA((n, 3))],
    )(*hs, *lands)


def _share_halves(fins):
    n = len(fins)

    def body(*refs):
        ins, outs = refs[:n], refs[n:2 * n]
        send_sems, recv_sems = refs[2 * n:]
        x, y, c, _ = _place()
        sibling = (x, y, 1 - c)
        sent = [_remote(ins[k].at[c], outs[k].at[c], send_sems.at[k], recv_sems.at[k], sibling) for k in range(n)]
        for cp in sent:
            cp.start()
        for k in range(n):
            got = outs[k].at[1 - c]
            _remote(got, got, send_sems.at[k], recv_sems.at[k], sibling).wait_recv()
        for cp in sent:
            cp.wait_send()

    return pl.pallas_call(
        body, name="share_halves", out_shape=[SDS(t.shape, t.dtype) for t in fins],
        in_specs=[ANY] * n, out_specs=[ANY] * n, input_output_aliases={k: k for k in range(n)},
        scratch_shapes=[pltpu.SemaphoreType.DMA((n,)), pltpu.SemaphoreType.DMA((n,))],
    )(*fins)


def _chip_sums(grads, place):
    g4s = [g.reshape(N_CHIPS, 2, g.shape[1] // 2, g.shape[2]) for g in grads]
    pairs = [_add_half(g4, r1, place) for g4, r1 in zip(g4s, _swap_halves(g4s))]
    return [h for h, _ in pairs], [own for _, own in pairs]


def _reduce_finish(lands, place):
    fins = _share_halves([_sum4_into_half(r2, place) for r2 in lands])
    return [f.reshape(2 * f.shape[1], f.shape[2]) for f in fins]


def _reduce_scatter(grads, place):
    hs, lands = _chip_sums(grads, place)
    return _reduce_finish(_scatter_regions(hs, lands), place)


HBM_SPEC = pl.BlockSpec(memory_space=pltpu.HBM)
SEM_SPEC = pl.BlockSpec(memory_space=pltpu.SEMAPHORE)
DATAFLOW = pltpu.SideEffectType.DATAFLOW_SIDE_EFFECTING


def _in_hbm(a):
    return pltpu.with_memory_space_constraint(a, pltpu.HBM)


def _hbm_like(a):
    return pltpu.HBM(a.shape, a.dtype)


def _gather_ici_start(slots, thru, after):
    n = len(slots)

    def body(*refs):
        ins = refs[:n]
        send_sems, recv_sems = refs[n + 2], refs[n + 3]
        x, y, c, chips = _place()
        q = 2 * x + y
        for k in range(n):
            for j, chip in enumerate(chips):
                _remote(ins[k].at[q, c], ins[k].at[q, c], send_sems.at[3 * k + j], recv_sems.at[3 * k + j], (*chip, c)).start()

    out = pl.pallas_call(
        body, name="gather_ici_start",
        out_shape=(pltpu.SemaphoreType.DMA((3 * n,)), pltpu.SemaphoreType.DMA((3 * n,)), *[_hbm_like(s) for s in slots], _hbm_like(thru)),
        in_specs=[HBM_SPEC] * (n + 1) + [ANY], out_specs=(SEM_SPEC, SEM_SPEC, *[HBM_SPEC] * (n + 1)),
        input_output_aliases={k: 2 + k for k in range(n + 1)},
        compiler_params=pltpu.CompilerParams(has_side_effects=DATAFLOW),
    )(*[_in_hbm(s) for s in slots], _in_hbm(thru), after)
    return out[0], out[1], list(out[2:2 + n]), out[2 + n]


def _gather_ici_wait(send_sems, recv_sems, slots, after):
    n = len(slots)

    def body(*refs):
        ins = refs[:n]
        send_sems, recv_sems = refs[n], refs[n + 1]
        x, y, c, chips = _place()
        for k in range(n):
            for j, chip in enumerate(chips):
                got = ins[k].at[2 * chip[0] + chip[1], c]
                cp = _remote(got, got, send_sems.at[3 * k + j], recv_sems.at[3 * k + j], (x, y, c))
                cp.wait_send()
                cp.wait_recv()

    out = pl.pallas_call(
        body, name="gather_ici_wait", out_shape=[_hbm_like(s) for s in slots],
        in_specs=[HBM_SPEC] * n + [SEM_SPEC, SEM_SPEC, ANY], out_specs=[HBM_SPEC] * n,
        input_output_aliases={k: k for k in range(n)},
        compiler_params=pltpu.CompilerParams(has_side_effects=DATAFLOW),
    )(*slots, send_sems, recv_sems, after)
    return list(out)


def _gather_pass_on(slots):
    n = len(slots)

    def body(*refs):
        ins, outs = refs[:n], refs[n:2 * n]
        send_sems, recv_sems = refs[2 * n:]
        x, y, c, chips = _place()
        sibling = (x, y, 1 - c)
        passed = []
        for k in range(n):
            for j, chip in enumerate(chips):
                passed.append(_remote(ins[k].at[2 * chip[0] + chip[1], c], outs[k].at[2 * chip[0] + chip[1], c],
                                      send_sems.at[k, j], recv_sems.at[k, j], sibling))
                passed[-1].start()
        for k in range(n):
            for j, chip in enumerate(chips):
                half = outs[k].at[2 * chip[0] + chip[1], 1 - c]
                _remote(half, half, send_sems.at[k, j], recv_sems.at[k, j], sibling).wait_recv()
        for cp in passed:
            cp.wait_send()

    return pl.pallas_call(
        body, name="gather_pass_on", out_shape=[SDS(s.shape, s.dtype) for s in slots],
        in_specs=[ANY] * n, out_specs=[ANY] * n, input_output_aliases={k: k for k in range(n)},
        scratch_shapes=[pltpu.SemaphoreType.DMA((n, 3)), pltpu.SemaphoreType.DMA((n, 3))],
    )(*slots)


def _scatter_start(hs, lands, thru):
    n = len(hs)

    def body(*refs):
        ins, zones = refs[:n], refs[n:2 * n]
        send_sems, recv_sems = refs[2 * n + 1], refs[2 * n + 2]
        x, y, c, chips = _place()
        q = 2 * x + y
        for k in range(n):
            for j, chip in enumerate(chips):
                _remote(ins[k].at[2 * chip[0] + chip[1]], zones[k].at[q], send_sems.at[3 * k + j], recv_sems.at[3 * k + j], (*chip, c)).start()

    arrays = [*hs, *lands, thru]
    out = pl.pallas_call(
        body, name="scatter_start",
        out_shape=(pltpu.SemaphoreType.DMA((3 * n,)), pltpu.SemaphoreType.DMA((3 * n,)), *[_hbm_like(a) for a in arrays]),
        in_specs=[HBM_SPEC] * len(arrays), out_specs=(SEM_SPEC, SEM_SPEC, *[HBM_SPEC] * len(arrays)),
        input_output_aliases={k: 2 + k for k in range(len(arrays))},
        compiler_params=pltpu.CompilerParams(has_side_effects=DATAFLOW),
    )(*[_in_hbm(a) for a in arrays])
    return out[0], out[1], list(out[2:2 + n]), list(out[2 + n:2 + 2 * n]), out[2 + 2 * n]


def _scatter_wait(send_sems, recv_sems, hs, lands, after):
    n = len(hs)

    def body(*refs):
        ins, zones = refs[:n], refs[n:2 * n]
        send_sems, recv_sems = refs[2 * n], refs[2 * n + 1]
        x, y, c, chips = _place()
        for k in range(n):
            for j, chip in enumerate(chips):
                p = 2 * chip[0] + chip[1]
                cp = _remote(ins[k].at[p], zones[k].at[p], send_sems.at[3 * k + j], recv_sems.at[3 * k + j], (x, y, c))
                cp.wait_send()
                cp.wait_recv()

    out = pl.pallas_call(
        body, name="scatter_wait", out_shape=[_hbm_like(a) for a in [*hs, *lands]],
        in_specs=[HBM_SPEC] * (2 * n) + [SEM_SPEC, SEM_SPEC, ANY], out_specs=[HBM_SPEC] * (2 * n),
        input_output_aliases={k: k for k in range(2 * n)},
        compiler_params=pltpu.CompilerParams(has_side_effects=DATAFLOW),
    )(*hs, *lands, send_sems, recv_sems, after)
    return list(out[n:])


def _split_start(name, arrays, n_sems, issue, extra=()):
    m = len(arrays)

    def body(*refs):
        issue(refs[:m], refs[m + len(extra)], refs[m + len(extra) + 1])

    out = pl.pallas_call(
        body, name=name,
        out_shape=(pltpu.SemaphoreType.DMA((n_sems,)), pltpu.SemaphoreType.DMA((n_sems,)), *[_hbm_like(a) for a in arrays]),
        in_specs=[HBM_SPEC] * m + [ANY] * len(extra), out_specs=(SEM_SPEC, SEM_SPEC, *[HBM_SPEC] * m),
        input_output_aliases={k: 2 + k for k in range(m)},
        compiler_params=pltpu.CompilerParams(has_side_effects=DATAFLOW),
    )(*[_in_hbm(a) for a in arrays], *extra)
    return out[0], out[1], list(out[2:])


def _split_wait(name, send_sems, recv_sems, arrays, after, drain):
    m = len(arrays)

    def body(*refs):
        drain(refs[:m], refs[m], refs[m + 1])

    out = pl.pallas_call(
        body, name=name, out_shape=[_hbm_like(a) for a in arrays],
        in_specs=[HBM_SPEC] * m + [SEM_SPEC, SEM_SPEC, ANY], out_specs=[HBM_SPEC] * m,
        input_output_aliases={k: k for k in range(m)},
        compiler_params=pltpu.CompilerParams(has_side_effects=DATAFLOW),
    )(*arrays, send_sems, recv_sems, after)
    return list(out)


def _wait_both(cp):
    cp.wait_send()
    cp.wait_recv()


class _Flight:
    def __init__(self, name, arrays, n_sems, issue, drain, thru, extra=()):
        self.name, self.drain, self.n = name, drain, len(arrays)
        self.send, self.recv, out = _split_start(name + "_start", [*arrays, thru], n_sems, issue, extra)
        self.arrays, self.thru = out[:-1], out[-1]

    def land(self, after):
        return _split_wait(self.name + "_wait", self.send, self.recv, self.arrays, after, self.drain)


def _gather_flight(tag, ici, d2d, direct, thru):
    kinds = ["ici"] * len(ici) + ["d2d"] * len(d2d) + ["direct"] * len(direct)

    def issue(refs, send_sems, recv_sems):
        x, y, c, chips = _place()
        q = 2 * x + y
        for k, kind in enumerate(kinds):
            for j, chip in enumerate(chips):
                if kind == "ici":
                    src, to = refs[k].at[q, c], (*chip, c)
                elif kind == "d2d":
                    src, to = refs[k].at[2 * chip[0] + chip[1], c], (x, y, 1 - c)
                else:
                    src, to = refs[k].at[q], (*chip, c)
                _remote(src, src, send_sems.at[3 * k + j], recv_sems.at[3 * k + j], to).start()

    def drain(refs, send_sems, recv_sems):
        x, y, c, chips = _place()
        for k, kind in enumerate(kinds):
            for j, chip in enumerate(chips):
                p = 2 * chip[0] + chip[1]
                got = refs[k].at[p] if kind == "direct" else refs[k].at[p, c if kind == "ici" else 1 - c]
                _wait_both(_remote(got, got, send_sems.at[3 * k + j], recv_sems.at[3 * k + j], (x, y, c)))

    return _Flight(f"gather{tag}", [*ici, *d2d, *direct], 3 * len(kinds), issue, drain, thru)


def _swap_flight(tag, g4s, thru):
    n = len(g4s)
    zones = [lax.empty((N_CHIPS,) + g.shape[2:], g.dtype) for g in g4s]

    def issue(refs, send_sems, recv_sems):
        x, y, c, _ = _place()
        for k in range(n):
            for p in range(N_CHIPS):
                _remote(refs[k].at[p, 1 - c], refs[n + k].at[p], send_sems.at[N_CHIPS * k + p], recv_sems.at[N_CHIPS * k + p], (x, y, 1 - c)).start()

    def drain(refs, send_sems, recv_sems):
        x, y, c, _ = _place()
        for k in range(n):
            for p in range(N_CHIPS):
                got = refs[n + k].at[p]
                _wait_both(_remote(got, got, send_sems.at[N_CHIPS * k + p], recv_sems.at[N_CHIPS * k + p], (x, y, c)))

    return _Flight(f"swap{tag}", [*g4s, *zones], N_CHIPS * n, issue, drain, thru)


def _scatter_flight(tag, hs, lands, thru):
    n = len(hs)

    def issue(refs, send_sems, recv_sems):
        x, y, c, chips = _place()
        q = 2 * x + y
        for k in range(n):
            for j, chip in enumerate(chips):
                _remote(refs[k].at[2 * chip[0] + chip[1]], refs[n + k].at[q], send_sems.at[3 * k + j], recv_sems.at[3 * k + j], (*chip, c)).start()

    def drain(refs, send_sems, recv_sems):
        x, y, c, chips = _place()
        for k in range(n):
            for j, chip in enumerate(chips):
                got = refs[n + k].at[2 * chip[0] + chip[1]]
                _wait_both(_remote(got, got, send_sems.at[3 * k + j], recv_sems.at[3 * k + j], (x, y, c)))

    return _Flight(f"scatter{tag}", [*hs, *lands], 3 * n, issue, drain, thru)


def _exchange_flight(buf, thru):
    flips = [(fx, fy, fc) for fx in (0, 1) for fy in (0, 1) for fc in (0, 1)][1:]

    def peers():
        x, y, c, _ = _place()
        return (x, y, c), [((1 - x) if fx else x, (1 - y) if fy else y, (1 - c) if fc else c) for fx, fy, fc in flips]

    def slot(ref, dev):
        return ref.at[4 * dev[0] + 2 * dev[1] + dev[2]]

    def issue(refs, send_sems, recv_sems):
        me, others = peers()
        for j, to in enumerate(others):
            _remote(slot(refs[0], me), slot(refs[0], me), send_sems.at[j], recv_sems.at[j], to).start()

    def drain(refs, send_sems, recv_sems):
        me, others = peers()
        for j, frm in enumerate(others):
            got = slot(refs[0], frm)
            _wait_both(_remote(got, got, send_sems.at[j], recv_sems.at[j], me))

    return _Flight("exchange", [buf], len(flips), issue, drain, thru)


def _share_flight(tag, fins, thru):
    n = len(fins)

    def issue(refs, send_sems, recv_sems):
        x, y, c, _ = _place()
        for k in range(n):
            _remote(refs[k].at[c], refs[k].at[c], send_sems.at[k], recv_sems.at[k], (x, y, 1 - c)).start()

    def drain(refs, send_sems, recv_sems):
        x, y, c, _ = _place()
        for k in range(n):
            got = refs[k].at[1 - c]
            _wait_both(_remote(got, got, send_sems.at[k], recv_sems.at[k], (x, y, c)))

    return _Flight(f"share{tag}", fins, n, issue, drain, thru)


def _pair_blocks(w):
    h, d, _ = w.shape
    z = jnp.zeros((h // 2, d, d), w.dtype)
    return jnp.concatenate([jnp.concatenate([w[0::2], z], axis=2), jnp.concatenate([z, w[1::2]], axis=2)], axis=1)


def _unpair_blocks(b):
    n, dd, _ = b.shape
    d = dd // 2
    return jnp.stack([b[:, :d, :d], b[:, d:, d:]], axis=1).reshape(2 * n, d, d)


def _pad_rows(a, rows):
    return jnp.pad(a, ((0, rows - a.shape[0]), (0, 0)))


class _Packer:
    def __init__(self, shapes, width=1024, row_multiple=64):
        self.shapes = shapes
        self.sizes = [math.prod(s) for s in shapes]
        total = sum(self.sizes)
        self.width = width
        self.rows = -(-total // (width * row_multiple)) * row_multiple
        self.pad = self.rows * width - total

    def pack(self, arrays):
        flat = jnp.concatenate([a.reshape(-1).astype(F32) for a in arrays] + [jnp.zeros((self.pad,), F32)])
        return flat.reshape(self.rows, self.width)

    def unpack(self, packed):
        flat = packed.reshape(-1)
        out, off = [], 0
        for s, n in zip(self.shapes, self.sizes):
            out.append(flat[off:off + n].reshape(s))
            off += n
        return out


SMALL = ["b_ada", "ffn1_norm", "mix_norm", "conv_w", "conv_b", "gate_a_w", "gate_a_b", "gate_x_w", "gate_x_b", "lru_lambda",
         "v_norm", "spatial_w", "spatial_b", "lru_out_norm", "gmlp_out_norm", "ffn2_norm", "final_norm"]
BIG = ["ffn1_w_gu", "ffn1_w_down", "w_in", "w_out", "ffn2_w_gu", "ffn2_w_down"]
GROUPS = (("ffn1_w_gu", "ffn1_w_down"), ("w_in", "w_out"), ("ffn2_w_gu", "ffn2_w_down"))
FWD_GROUPS = (("ffn1_w_gu",), ("ffn1_w_down",), ("w_in", "w_out"), ("ffn2_w_gu",), ("ffn2_w_down",))
MIN_AGE = {"swap": 1, "scatter": 2, "share": 1}
WEIGHTS = ["w_ada", "b_ada", "ffn1_norm", "ffn1_w_gu", "ffn1_w_down", "mix_norm", "w_in", "conv_w", "conv_b", "gate_a_w", "gate_a_b",
           "gate_x_w", "gate_x_b", "lru_lambda", "v_norm", "spatial_w", "spatial_b", "lru_out_norm", "gmlp_out_norm", "w_out",
           "ffn2_norm", "ffn2_w_gu", "ffn2_w_down", "final_norm"]


def kernel(x, c, w_ada, b_ada, ffn1_norm, ffn1_w_gu, ffn1_w_down, mix_norm, w_in, conv_w, conv_b, gate_a_w, gate_a_b, gate_x_w, gate_x_b, lru_lambda, v_norm, spatial_w, spatial_b, lru_out_norm, gmlp_out_norm, w_out, ffn2_norm, ffn2_w_gu, ffn2_w_down, final_norm, loss_target, m_w_ada, m_b_ada, m_ffn1_norm, m_ffn1_w_gu, m_ffn1_w_down, m_mix_norm, m_w_in, m_conv_w, m_conv_b, m_gate_a_w, m_gate_a_b, m_gate_x_w, m_gate_x_b, m_lru_lambda, m_v_norm, m_spatial_w, m_spatial_b, m_lru_out_norm, m_gmlp_out_norm, m_w_out, m_ffn2_norm, m_ffn2_w_gu, m_ffn2_w_down, m_final_norm, v_w_ada, v_b_ada, v_ffn1_norm, v_ffn1_w_gu, v_ffn1_w_down, v_mix_norm, v_w_in, v_conv_w, v_conv_b, v_gate_a_w, v_gate_a_b, v_gate_x_w, v_gate_x_b, v_lru_lambda, v_v_norm, v_spatial_w, v_spatial_b, v_lru_out_norm, v_gmlp_out_norm, v_w_out, v_ffn2_norm, v_ffn2_w_gu, v_ffn2_w_down, v_final_norm):
    given = dict(locals())
    W = {n: given[n] for n in WEIGHTS}
    L = w_ada.shape[0]
    S, D = x.shape[1], x.shape[2]
    LW = conv_b.shape[1]
    hd = LW // HEADS
    xi, yi, ci = lax.axis_index("x"), lax.axis_index("y"), lax.axis_index("c")
    chip = 2 * xi + yi
    dev = 2 * chip + ci
    place = jnp.stack([ci, chip]).astype(jnp.int32)
    xs = x.reshape(S, D)
    tgt = loss_target.reshape(S, D)

    c_all = _all_gather8(_pad_rows(c, SUBLANES))[:, 0, :]
    n_ada = w_ada.shape[2]
    b_shard = lax.dynamic_slice_in_dim(b_ada, chip * n_ada, n_ada, axis=1)
    mod_shard = _ada_fwd(_pad_rows(c_all, 2 * SUBLANES), w_ada, b_shard[:, None, :])

    def in_slot(block):
        return lax.dynamic_update_index_in_dim(jnp.zeros((N_CHIPS,) + block.shape, block.dtype), block, chip, 0)

    cws = LW // N_CHIPS
    small = [in_slot(mod_shard.reshape(L * 2 * SUBLANES, n_ada)), in_slot(conv_w.reshape(L * CONV_WIDTH, cws))]

    def half_view(s):
        return s.reshape(N_CHIPS, 2, s.shape[1] // 2, s.shape[2])

    stages = [(l, names) for l in range(L) for names in FWD_GROUPS]
    seq = [[half_view(_cast_into_slot(W[n], l, place, place)) for n in names] for l, names in stages[:1]]
    flights = {}

    def launch(t, thru, direct=()):
        ici = seq[t] if t < len(seq) else []
        d2d = seq[t - 1] if 1 <= t <= len(seq) else []
        if ici or d2d or direct:
            flights[t] = _gather_flight(t, ici, d2d, list(direct), thru)
            thru = flights[t].thru
        return thru

    def land(t, after):
        if t not in flights:
            return []
        out = flights.pop(t).land(after)
        ni = len(seq[t]) if t < len(seq) else 0
        nd = len(seq[t - 1]) if 1 <= t <= len(seq) else 0
        if ni:
            seq[t] = out[:ni]
        if nd:
            seq[t - 1] = out[ni:ni + nd]
        return out[ni + nd:]

    def group_weights(t):
        return [s.reshape(N_CHIPS, -1, s.shape[3]) for s in seq[t]]

    c_all = launch(0, c_all, small)
    seq += [[half_view(_cast_into_slot(W[n], l, place, c_all)) for n in names] for l, names in stages[1:]]
    mod_all, conv_all = land(0, seq[-1][-1])
    mod_all = launch(1, mod_all)
    land(1, mod_all)
    mod_rows = lax.dynamic_index_in_dim(mod_all.reshape(N_CHIPS, L, 2 * SUBLANES, n_ada), dev, axis=2, keepdims=False)
    mod = mod_rows.transpose(1, 0, 2).reshape(L, N_MOD, 1, D)
    conv_full = conv_all.reshape(N_CHIPS, L, CONV_WIDTH, cws).transpose(1, 2, 0, 3).reshape(L, CONV_WIDTH, LW)

    tril = jnp.tril(jnp.ones((CHUNK, CHUNK), F32))
    seg = (jnp.arange(LW)[:, None] // hd == jnp.arange(LW)[None, :] // hd).astype(jnp.bfloat16)

    def mixer_params(l):
        ws = spatial_w[l] * tril
        wsp = jnp.concatenate([ws[0::2], ws[1::2]], axis=2)
        wa, wx = _pair_blocks(gate_a_w[l]), _pair_blocks(gate_x_w[l])
        return dict(
            cw=conv_full[l], cb=conv_b[l][None],
            wa=wa.astype(MXU_DTYPE), wx=wx.astype(MXU_DTYPE), wat=wa.transpose(0, 2, 1).astype(MXU_DTYPE), wxt=wx.transpose(0, 2, 1).astype(MXU_DTYPE),
            ba=gate_a_b[l].reshape(1, LW), bx=gate_x_b[l].reshape(1, LW), lam=lru_lambda[l][None], gv=v_norm[l][None],
            wsp=wsp.astype(MXU_DTYPE), wspt=wsp.transpose(0, 2, 1).astype(MXU_DTYPE),
            bfull=jnp.repeat(spatial_b[l].T, hd, axis=1), g_lru=lru_out_norm[l][None], g_gm=gmlp_out_norm[l][None])

    saved = []
    xcur = xs
    for l in range(L):
        mp, md = mixer_params(l), mod[l]
        s = dict(lw={}, mp=mp, md=md)
        lw = s["lw"]
        t = len(FWD_GROUPS) * l
        s["x0"] = xcur
        s["h1"] = launch(t + 2, _modnorm(xcur, ffn1_norm[l][None], md[0], md[1]))
        lw["gu1"], = group_weights(t)
        s["a1"], s["gu1"] = _ffn_up(s["h1"], lw["gu1"])
        land(t + 2, s["a1"])
        s["a1"] = launch(t + 3, s["a1"])
        lw["d1"] = group_weights(t + 1)[0].reshape(-1, D)
        s["f1"], xcur = _mm_res(s["a1"], lw["d1"], xcur, md[2], 0.5)
        land(t + 3, xcur)
        s["x1"] = xcur
        s["h2"] = launch(t + 4, _modnorm(xcur, mix_norm[l][None], md[3], md[4]))
        lw["win"], wout = group_weights(t + 2)
        lw["wout"] = wout.reshape(-1, D)
        s["proj"] = _mm_chunks(s["h2"], lw["win"])
        s["ylru"] = _lru_fwd(s["proj"], mp["cw"], mp["cb"], mp["wa"], mp["ba"], mp["wx"], mp["bx"], mp["lam"])
        s["yn"], s["ygm"] = _gmlp_fwd(s["proj"], s["ylru"], mp["gv"], seg, mp["wsp"], mp["bfull"], mp["g_lru"], mp["g_gm"])
        s["f2"], xcur = _mm_res(s["yn"], lw["wout"], xcur, md[5], 1.0)
        land(t + 4, xcur)
        s["x2"] = xcur
        s["h3"] = launch(t + 5, _modnorm(xcur, ffn2_norm[l][None], md[6], md[7]))
        lw["gu2"], = group_weights(t + 3)
        s["a3"], s["gu3"] = _ffn_up(s["h3"], lw["gu2"])
        land(t + 5, s["a3"])
        s["a3"] = launch(t + 6, s["a3"])
        lw["d2"] = group_weights(t + 4)[0].reshape(-1, D)
        s["f3"], xcur = _mm_res(s["a3"], lw["d2"], xcur, md[8], 0.5)
        land(t + 6, xcur)
        saved.append(s)

    dx, dq, head_acc = _loss_head(xcur, tgt, final_norm[None], saved[-1]["md"][8], 0.5)
    loss = lax.psum(jnp.sum(head_acc[1]), ("x", "y", "c"))
    small_grads = {}
    big_grads = {n: [None] * L for n in BIG}
    dmods = [None] * L
    zero_row = jnp.zeros((1, D), F32)

    def ffn_bwd(names, l, dx, dq, x_in, h, a, gu, f, wgu, wd, gn, sc, next_gate, next_scale):
        big_grads[names[1]][l] = _mm_tn_chunks(a, dq[None], 1408, 1024)[0].reshape(N_CHIPS, -1, D)
        dgu = _ffn_bwd_act(dq, wd, gu)
        C = dgu.shape[3]
        dgu4 = dgu.reshape(N_CHIPS, S, C)
        big_grads[names[0]][l] = _mm_tn_chunks(h, dgu4, 1024, C)
        dgu4 = reduce_group(names, l, big_grads[names[0]][l], dgu4)
        dh = _mm_nt_chunks(dgu4, wgu)
        dh = move_on(dh, dh)
        dx, dq, acc = _norm_bwd(x_in, dh, dx, f, gn, sc, 0.5, next_gate, next_scale)
        return dx, move_on(dx, dq), acc

    stepped = {n: None for n in BIG}
    reducing = []

    clock = [0]
    to_step = []

    def step_reduced(after):
        while to_step:
            name, l, g = to_step.pop(0)
            stepped[name] = _adamw_layer(W[name], g, given["m_" + name], given["v_" + name], l, stepped[name], after)
            after = stepped[name][1]
        return after

    def move_on(after, thru, force=False):
        clock[0] += 1
        for grp in list(reducing):
            if not force and clock[0] - grp["since"] < MIN_AGE[grp["step"]]:
                continue
            grp["since"] = clock[0]
            landed = grp["flight"].land(after)
            n = len(grp["names"])
            if grp["step"] == "swap":
                pairs = [_add_half(g4, r1, place) for g4, r1 in zip(landed[:n], landed[n:])]
                grp.update(step="scatter", flight=_scatter_flight(grp["tag"], [h for h, _ in pairs], [own for _, own in pairs], thru))
            elif grp["step"] == "scatter":
                grp.update(step="share", flight=_share_flight(grp["tag"], [_sum4_into_half(r2, place) for r2 in landed[n:]], thru))
            else:
                to_step.extend((name, grp["l"], fin.reshape(2 * fin.shape[1], fin.shape[2])) for name, fin in zip(grp["names"], landed))
                reducing.remove(grp)
                continue
            thru = grp["flight"].thru
        return thru

    def reduce_group(names, l, after, thru):
        thru = move_on(after, thru)
        g4s = [big_grads[n][l].reshape(N_CHIPS, 2, big_grads[n][l].shape[1] // 2, big_grads[n][l].shape[2]) for n in names]
        tag = f"{l}{GROUPS.index(names)}"
        reducing.append(dict(names=names, l=l, tag=tag, step="swap", since=clock[0], flight=_swap_flight(tag, g4s, thru)))
        return reducing[-1]["flight"].thru
    for l in reversed(range(L)):
        s = saved[l]
        lw, mp, md = s["lw"], s["mp"], s["md"]
        dx, dq, acc3 = ffn_bwd(
            GROUPS[2], l, dx, dq, s["x2"], s["h3"], s["a3"], s["gu3"], s["f3"], lw["gu2"], lw["d2"], ffn2_norm[l][None], md[7], md[5], 1.0)
        big_grads["w_out"][l] = _mm_tn_chunks(s["yn"], dq[None], 1024, 1024)[0].reshape(N_CHIPS, -1, D)
        dyn = _mm_nt_chunks(dq[None], lw["wout"][None])
        dylru, duv, dwsp, dbfull, gacc = _gmlp_bwd(s["proj"], s["ylru"], s["ygm"], dyn, mp["gv"], seg, mp["wsp"], mp["wspt"], mp["bfull"], mp["g_lru"], mp["g_gm"])
        dxg, dwa, dwx, lvec = _lru_bwd(s["proj"], dylru, mp["cw"], mp["cb"], mp["wa"], mp["ba"], mp["wx"], mp["bx"], mp["lam"], mp["wat"], mp["wxt"])
        dproj = jnp.concatenate([dxg, duv], axis=0)
        big_grads["w_in"][l] = _mm_tn_chunks(s["h2"], dproj, 1024, LW)
        dproj = reduce_group(GROUPS[1], l, big_grads["w_in"][l], dproj)
        dh2 = _mm_nt_chunks(dproj, lw["win"])
        dh2 = move_on(dh2, dh2)
        dx, dq, acc2 = _norm_bwd(s["x1"], dh2, dx, s["f2"], mix_norm[l][None], md[4], 1.0, md[2], 0.5)
        dq = move_on(dx, dq)
        if l > 0:
            ng, ns = saved[l - 1]["md"][8], 0.5
        else:
            ng, ns = zero_row, 0.0
        dx, dq, acc1 = ffn_bwd(
            GROUPS[0], l, dx, dq, s["x0"], s["h1"], s["a1"], s["gu1"], s["f1"], lw["gu1"], lw["d1"], ffn1_norm[l][None], md[1], ng, ns)

        dmods[l] = jnp.concatenate([acc1[0:2], acc1[3:4], acc2[0:2], acc2[3:4], acc3[0:2], acc3[3:4]], axis=0)
        dws = jnp.stack([dwsp[:, :, :CHUNK], dwsp[:, :, CHUNK:]], axis=1).reshape(HEADS, CHUNK, CHUNK) * tril
        lg = {"ffn1_norm": acc1[2], "mix_norm": acc2[2], "ffn2_norm": acc3[2],
              "conv_w": lvec[4:8], "conv_b": lvec[3], "gate_a_w": _unpair_blocks(dwa), "gate_a_b": lvec[0].reshape(HEADS, hd),
              "gate_x_w": _unpair_blocks(dwx), "gate_x_b": lvec[1].reshape(HEADS, hd), "lru_lambda": lvec[2], "v_norm": gacc[2],
              "spatial_w": dws, "spatial_b": dbfull.reshape(CHUNK, HEADS, hd).sum(-1).T, "lru_out_norm": gacc[0], "gmlp_out_norm": gacc[1]}
        for n, g in lg.items():
            small_grads.setdefault(n, [None] * L)[l] = g

    grad_x = dx.reshape(x.shape)

    per_layer = [n for n in SMALL if n not in ("b_ada", "final_norm")]
    part = [jnp.stack(small_grads[n]) for n in per_layer] + [head_acc[0], jnp.stack(dmods)]
    packer = _Packer([p.shape for p in part])
    packed = packer.pack(part)
    exchange = _exchange_flight(lax.dynamic_update_index_in_dim(jnp.zeros((N_DEV,) + packed.shape, F32), packed, dev, 0), dq)
    dq = exchange.thru
    done = step_reduced(dq)
    while reducing:
        dq = move_on(done, dq, force=True)
        done = step_reduced(dq)
    gathered, = exchange.land(done)
    summed = packer.unpack(_sum_leading(gathered))
    grads = dict(zip(per_layer + ["final_norm"], summed[:-1]))
    grads["b_ada"] = summed[-1].reshape(L, N_MOD * D)
    off = sum(packer.sizes[:-1])
    dmod_rows = gathered.reshape(N_DEV, -1)[:, off:off + L * N_MOD * D].reshape(N_DEV, L, N_MOD * D)
    dmod_shard = lax.dynamic_slice_in_dim(dmod_rows, chip * n_ada, n_ada, axis=2).transpose(1, 0, 2)
    grads["w_ada"] = _ada_grad(c_all.T, dmod_shard)
    grads["conv_w"] = lax.dynamic_slice_in_dim(grads["conv_w"], chip * cws, cws, axis=2)

    delta, new_m, new_v = {}, {}, {}
    for n in BIG:
        grads[n], delta[n], new_m[n], new_v[n] = stepped[n]
    shp = w_ada.shape
    d_, m_, v_ = _adamw(*[a.reshape(-1, shp[-1]) for a in (w_ada, grads["w_ada"], m_w_ada, v_w_ada)])
    delta["w_ada"], new_m["w_ada"], new_v["w_ada"] = d_.reshape(shp), m_.reshape(shp), v_.reshape(shp)
    spk = _Packer([W[n].shape for n in SMALL])
    d_, m_, v_ = _adamw(spk.pack([W[n] for n in SMALL]), spk.pack([grads[n] for n in SMALL]),
                        spk.pack([given["m_" + n] for n in SMALL]), spk.pack([given["v_" + n] for n in SMALL]))
    for n, a, b, e in zip(SMALL, spk.unpack(d_), spk.unpack(m_), spk.unpack(v_)):
        delta[n], new_m[n], new_v[n] = a, b, e
    grads = {n: grads[n].reshape(W[n].shape) for n in WEIGHTS}
    return (loss, grad_x, *[grads[n] for n in WEIGHTS], *[delta[n] for n in WEIGHTS], *[new_m[n] for n in WEIGHTS], *[new_v[n] for n in WEIGHTS])
```

```python
import math

import jax
import jax.numpy as jnp
from jax import lax
from jax.experimental import pallas as pl
from jax.experimental.pallas import tpu as pltpu

F32 = jnp.float32
MXU_DTYPE = jnp.bfloat16
ACT_DTYPE = jnp.bfloat16
XFER_DTYPE = jnp.bfloat16
EPS = 1e-6
RG_LRU_C = 8.0
N_MOD = 9
CONV_WIDTH = 4
HEADS = 8
CHUNK = 128
LANES = 128
SUBLANES = 8
N_CHIPS = 4
N_DEV = 8
ADAM_LR, ADAM_B1, ADAM_B2, ADAM_EPS, ADAM_WD, ADAM_STEP = 0.001, 0.9, 0.999, 1e-08, 0.01, 10
VMEM_LIMIT_BYTES = 60 * 1024 * 1024
ROW_TILE_BYTES = 2 << 20
GELU_C = math.sqrt(2.0 / math.pi)
GELU_A = 0.044715

ANY = pl.BlockSpec(memory_space=pl.ANY)
MESH = pl.DeviceIdType.MESH
SDS = jax.ShapeDtypeStruct


def _params(*sem):
    return pltpu.CompilerParams(dimension_semantics=sem, vmem_limit_bytes=VMEM_LIMIT_BYTES)


def _dot(a, b):
    return jnp.dot(a.astype(MXU_DTYPE), b.astype(MXU_DTYPE), preferred_element_type=F32)


def _dot_nt(a, b):
    return lax.dot_general(a.astype(MXU_DTYPE), b.astype(MXU_DTYPE), (((1,), (1,)), ((), ())), preferred_element_type=F32)


def _dot_tn(a, b):
    return lax.dot_general(a.astype(MXU_DTYPE), b.astype(MXU_DTYPE), (((0,), (0,)), ((), ())), preferred_element_type=F32)


def _gelu(x):
    return x * (0.5 * (1.0 + jnp.tanh(GELU_C * (x + GELU_A * (x * x * x)))))


def _gelu_grad(x):
    t = jnp.tanh(GELU_C * (x + GELU_A * (x * x * x)))
    return 0.5 * (1.0 + t) + 0.5 * x * (1.0 - t * t) * (GELU_C * (1.0 + 3.0 * GELU_A * x * x))


def _sigmoid(x):
    return jax.nn.sigmoid(x)


def _rsqrt_ms(x):
    return lax.rsqrt(jnp.mean(x * x, axis=-1, keepdims=True) + EPS)


def _rowsum(x):
    return jnp.sum(x, axis=0, keepdims=True)


def _tile(n, want):
    t = min(n, want)
    assert n % t == 0, (n, want)
    return t


def _row_tile(rows, row_bytes):
    step = 2 * SUBLANES
    cap = max(step, ROW_TILE_BYTES // row_bytes)
    best = None
    for t in range(step, min(rows, cap) + 1, step):
        if rows % t == 0:
            best = t
    assert best is not None, (rows, row_bytes)
    return best


def _modnorm(x, gn, sh, sc):
    S, D = x.shape
    tm = _tile(S, 1024)

    def body(x_ref, gn_ref, sh_ref, sc_ref, h_ref):
        xv = x_ref[...]
        h = (xv * _rsqrt_ms(xv) * gn_ref[...]) * (1.0 + sc_ref[...]) + sh_ref[...]
        h_ref[...] = h.astype(ACT_DTYPE)

    row = pl.BlockSpec((1, D), lambda i: (0, 0))
    return pl.pallas_call(
        body, name="modnorm", grid=(S // tm,),
        in_specs=[pl.BlockSpec((tm, D), lambda i: (i, 0)), row, row, row],
        out_specs=pl.BlockSpec((tm, D), lambda i: (i, 0)),
        out_shape=SDS((S, D), ACT_DTYPE), compiler_params=_params("parallel"),
    )(x, gn, sh, sc)


def _mm_nt_norm_bwd(ac, wc, x, dxo, f, gn, sc, res_scale, next_gate, next_scale):
    P, S, K = ac.shape
    D = x.shape[1]
    tm = _tile(S, 512)

    def body(a_ref, w_ref, x_ref, dxo_ref, f_ref, gn_ref, sc_ref, ng_ref, dx_ref, dq_ref, acc_ref):
        @pl.when(pl.program_id(0) == 0)
        def _():
            acc_ref[...] = jnp.zeros_like(acc_ref)

        dh = _dot_nt(a_ref[0], w_ref[0])
        for p in range(1, P):
            dh += _dot_nt(a_ref[p], w_ref[p])
        xv, dxo = x_ref[...], dxo_ref[...]
        r = _rsqrt_ms(xv)
        xhat = xv * r
        gn = gn_ref[...]
        dn = dh * (1.0 + sc_ref[...])
        dxh = dn * gn
        dx = dxo + r * (dxh - xhat * jnp.mean(dxh * xhat, axis=-1, keepdims=True))
        dx_ref[...] = dx
        dq_ref[...] = ((next_scale * ng_ref[...]) * dx).astype(ACT_DTYPE)
        acc_ref[0:1, :] += _rowsum(dh)
        acc_ref[1:2, :] += _rowsum(dh * (xhat * gn))
        acc_ref[2:3, :] += _rowsum(dn * xhat)
        acc_ref[3:4, :] += _rowsum((res_scale * f_ref[...]) * dxo)

    tile = pl.BlockSpec((tm, D), lambda i: (i, 0))
    row = pl.BlockSpec((1, D), lambda i: (0, 0))
    return pl.pallas_call(
        body, name=f"mm_nt_norm_bwd_k{K}", grid=(S // tm,),
        in_specs=[pl.BlockSpec((P, tm, K), lambda i: (0, i, 0)),
                  pl.BlockSpec((P, D, K), lambda i: (0, 0, 0), pipeline_mode=pl.Buffered(1)),
                  tile, tile, tile, row, row, row],
        out_specs=[tile, tile, pl.BlockSpec((SUBLANES, D), lambda i: (0, 0))],
        out_shape=[SDS((S, D), F32), SDS((S, D), ACT_DTYPE), SDS((SUBLANES, D), F32)],
        compiler_params=_params("arbitrary"),
    )(ac, wc, x, dxo, f, gn, sc, next_gate)


def _loss_head(x, target, gn, next_gate, next_scale):
    S, D = x.shape
    tm = _tile(S, 512)

    def body(x_ref, t_ref, gn_ref, ng_ref, dx_ref, dq_ref, acc_ref):
        @pl.when(pl.program_id(0) == 0)
        def _():
            acc_ref[...] = jnp.zeros_like(acc_ref)

        xv = x_ref[...]
        r = _rsqrt_ms(xv)
        xhat = xv * r
        gn = gn_ref[...]
        err = xhat * gn - t_ref[...]
        dy = err * (1.0 / D)
        dxh = dy * gn
        dx = r * (dxh - xhat * jnp.mean(dxh * xhat, axis=-1, keepdims=True))
        dx_ref[...] = dx
        dq_ref[...] = ((next_scale * ng_ref[...]) * dx).astype(ACT_DTYPE)
        acc_ref[0:1, :] += _rowsum(dy * xhat)
        acc_ref[1:2, :] += _rowsum(err * err) * (0.5 / D)

    tile = pl.BlockSpec((tm, D), lambda i: (i, 0))
    row = pl.BlockSpec((1, D), lambda i: (0, 0))
    return pl.pallas_call(
        body, name="loss_head", grid=(S // tm,),
        in_specs=[tile, tile, row, row],
        out_specs=[tile, tile, pl.BlockSpec((SUBLANES, D), lambda i: (0, 0))],
        out_shape=[SDS((S, D), F32), SDS((S, D), ACT_DTYPE), SDS((SUBLANES, D), F32)],
        compiler_params=_params("arbitrary"),
    )(x, target, gn, next_gate)


def _ffn_up(h, wgu):
    S, D = h.shape
    C = wgu.shape[2]
    tm = _tile(S, 512)

    def body(h_ref, wg_ref, wu_ref, a_ref, gu_ref):
        hv = h_ref[...]
        g = _dot(hv, wg_ref[...])
        u = _dot(hv, wu_ref[...])
        a_ref[...] = (g * _sigmoid(g) * u).astype(ACT_DTYPE)
        gu_ref[0] = g.astype(ACT_DTYPE)
        gu_ref[1] = u.astype(ACT_DTYPE)

    return pl.pallas_call(
        body, name="ffn_up", grid=(2, S // tm),
        in_specs=[
            pl.BlockSpec((tm, D), lambda j, i: (i, 0)),
            pl.BlockSpec((None, D, C), lambda j, i: (j, 0, 0)),
            pl.BlockSpec((None, D, C), lambda j, i: (2 + j, 0, 0)),
        ],
        out_specs=[
            pl.BlockSpec((tm, C), lambda j, i: (i, j)),
            pl.BlockSpec((2, None, tm, C), lambda j, i: (0, j, i, 0)),
        ],
        out_shape=[SDS((S, 2 * C), ACT_DTYPE), SDS((2, 2, S, C), ACT_DTYPE)],
        compiler_params=_params("parallel", "parallel"),
    )(h, wgu, wgu)


def _ffn_bwd_act(dq, wd, gu):
    S, D = dq.shape
    C = gu.shape[3]
    tm = _tile(S, 512)

    def body(dq_ref, wd_ref, gu_ref, dgu_ref):
        da = _dot_nt(dq_ref[...], wd_ref[...])
        g = gu_ref[0].astype(F32)
        u = gu_ref[1].astype(F32)
        s = _sigmoid(g)
        dgu_ref[0] = (da * u * (s * (1.0 + g * (1.0 - s)))).astype(ACT_DTYPE)
        dgu_ref[1] = (da * (g * s)).astype(ACT_DTYPE)

    gu_spec = pl.BlockSpec((2, None, tm, C), lambda j, i: (0, j, i, 0))
    return pl.pallas_call(
        body, name="ffn_bwd_act", grid=(2, S // tm),
        in_specs=[pl.BlockSpec((tm, D), lambda j, i: (i, 0)), pl.BlockSpec((C, D), lambda j, i: (j, 0)), gu_spec],
        out_specs=gu_spec,
        out_shape=SDS(gu.shape, ACT_DTYPE),
        compiler_params=_params("parallel", "parallel"),
    )(dq, wd, gu)


def _mm_res(a, w, x, gate, scale, following):
    S, K = a.shape
    D = w.shape[1]
    tm = _tile(S, 512)

    def body(a_ref, w_ref, x_ref, g_ref, gn_ref, sh_ref, sc_ref, f_ref, xo_ref, h_ref):
        f = _dot(a_ref[...], w_ref[...])
        f_ref[...] = f
        xo = x_ref[...] + (scale * g_ref[...]) * f
        xo_ref[...] = xo
        h_ref[...] = ((xo * _rsqrt_ms(xo) * gn_ref[...]) * (1.0 + sc_ref[...]) + sh_ref[...]).astype(ACT_DTYPE)

    tile = pl.BlockSpec((tm, D), lambda i: (i, 0))
    row = pl.BlockSpec((1, D), lambda i: (0, 0))
    return pl.pallas_call(
        body, name=f"mm_res_k{K}", grid=(S // tm,),
        in_specs=[pl.BlockSpec((tm, K), lambda i: (i, 0)), pl.BlockSpec((K, D), lambda i: (0, 0)), tile, row, row, row, row],
        out_specs=[tile, tile, tile],
        out_shape=[SDS((S, D), F32), SDS((S, D), F32), SDS((S, D), ACT_DTYPE)],
        compiler_params=_params("parallel"),
    )(a, w, x, gate, *following)


def _mm_chunks(h, wc):
    S, K = h.shape
    P, _, N = wc.shape
    tm = _tile(S, 512)

    def body(h_ref, w_ref, o_ref):
        hv = h_ref[...]
        for p in range(P):
            o_ref[:, p * N:(p + 1) * N] = _dot(hv, w_ref[p])

    return pl.pallas_call(
        body, name="mm_chunks", grid=(S // tm,),
        in_specs=[pl.BlockSpec((tm, K), lambda i: (i, 0)), pl.BlockSpec((P, K, N), lambda i: (0, 0, 0))],
        out_specs=pl.BlockSpec((tm, P * N), lambda i: (i, 0)),
        out_shape=SDS((S, P * N), F32),
        compiler_params=_params("parallel"),
    )(h, wc)


def _mm_nt_chunks(ac, wc):
    P, S, K = ac.shape
    N = wc.shape[1]
    tm, tn = _tile(S, 512), _tile(N, 1024)

    def body(a_ref, w_ref, o_ref):
        acc = _dot_nt(a_ref[0], w_ref[0])
        for p in range(1, P):
            acc += _dot_nt(a_ref[p], w_ref[p])
        o_ref[...] = acc

    return pl.pallas_call(
        body, name=f"mm_nt_p{P}k{K}", grid=(S // tm, N // tn),
        in_specs=[pl.BlockSpec((P, tm, K), lambda i, j: (0, i, 0)), pl.BlockSpec((P, tn, K), lambda i, j: (0, j, 0))],
        out_specs=pl.BlockSpec((tm, tn), lambda i, j: (i, j)),
        out_shape=SDS((S, N), F32),
        compiler_params=_params("parallel", "parallel"),
    )(ac, wc)


def _mm_tn_chunks(a, bc, tile_m, tile_n):
    S, M = a.shape
    P, _, N = bc.shape
    ts, tm, tn = _tile(S, 2048), _tile(M, tile_m), _tile(N, tile_n)

    def body(a_ref, b_ref, o_ref):
        @pl.when(pl.program_id(3) == 0)
        def _():
            o_ref[...] = jnp.zeros_like(o_ref)

        o_ref[...] += _dot_tn(a_ref[...], b_ref[...])

    return pl.pallas_call(
        body, name=f"mm_tn_m{M}n{N}", grid=(P, M // tm, N // tn, S // ts),
        in_specs=[pl.BlockSpec((ts, tm), lambda p, m, n, k: (k, m)), pl.BlockSpec((None, ts, tn), lambda p, m, n, k: (p, k, n))],
        out_specs=pl.BlockSpec((None, tm, tn), lambda p, m, n, k: (p, m, n)),
        out_shape=SDS((P, M, N), F32),
        compiler_params=_params("parallel", "parallel", "parallel", "arbitrary"),
    )(a, bc)


def _shift_down(x, s, row, fill):
    return jnp.where(row >= s, pltpu.roll(x, s, 0), fill)


def _shift_up(x, s, row, fill):
    n = x.shape[0]
    return jnp.where(row < n - s, pltpu.roll(x, n - s, 0), fill)


def _scan(a, b, row, scratch, up):
    scr_a, scr_b, scr_c = scratch
    n = a.shape[0]
    g = n // SUBLANES
    in_group = row & (SUBLANES - 1)

    def steps(a, b, pos, size):
        s = 1
        while s < size:
            m = (pos + s < size) if up else (pos >= s)
            b = jnp.where(m, a, 0.0) * pltpu.roll(b, a.shape[0] - s if up else s, 0) + b
            a = jnp.where(m, a * pltpu.roll(a, a.shape[0] - s if up else s, 0), a)
            s *= 2
        return a, b

    a, b = steps(a, b, in_group, SUBLANES)
    scr_a[...] = a
    scr_b[...] = b
    edge = 0 if up else SUBLANES - 1
    at = scr_a[pl.ds(edge, g, stride=SUBLANES), :]
    bt = scr_b[pl.ds(edge, g, stride=SUBLANES), :]
    group = lax.broadcasted_iota(jnp.int32, at.shape, 0)
    _, state = steps(at, bt, group, g)
    carry = jnp.where((group + 1 < g) if up else (group >= 1), pltpu.roll(state, g - 1 if up else 1, 0), 0.0)
    for k in range(SUBLANES):
        scr_c[pl.ds(k, g, stride=SUBLANES), :] = carry
    return b + a * scr_c[...]


def _conv(xl, cw_ref, cb_ref, row):
    y = cb_ref[...] + _shift_down(xl, 3, row, 0.0) * cw_ref[0:1, :]
    y = y + _shift_down(xl, 2, row, 0.0) * cw_ref[1:2, :]
    y = y + _shift_down(xl, 1, row, 0.0) * cw_ref[2:3, :]
    return y + xl * cw_ref[3:4, :]


def _lru_gates(xc, wa_ref, ba_ref, wx_ref, bx_ref, lam_ref):
    ra = _sigmoid(_dot(xc, wa_ref[...]) + ba_ref[...])
    ri = _sigmoid(_dot(xc, wx_ref[...]) + bx_ref[...])
    ls = jax.nn.log_sigmoid(lam_ref[...])
    a = jnp.exp((RG_LRU_C * ra) * ls)
    mult = jnp.sqrt(1.0 - a * a)
    return ra, ri, ls, a, mult


def _lru_specs(S):
    col = lambda off: pl.BlockSpec((S, LANES), lambda j: (0, off + j))
    vec = pl.BlockSpec((1, LANES), lambda j: (0, j))
    blk = pl.BlockSpec((None, LANES, LANES), lambda j: (j, 0, 0))
    cw = pl.BlockSpec((CONV_WIDTH, LANES), lambda j: (0, j))
    return col, vec, blk, cw


def _lru_fwd(proj, cw, cb, wa, ba, wx, bx, lam):
    S = proj.shape[0]
    W = cb.shape[1]
    nb = W // LANES

    def body(xl_ref, gl_ref, cw_ref, cb_ref, wa_ref, ba_ref, wx_ref, bx_ref, lam_ref, y_ref, *scratch):
        row = lax.broadcasted_iota(jnp.int32, (S, LANES), 0)
        xc = _conv(xl_ref[...], cw_ref, cb_ref, row)
        _, ri, _, a, mult = _lru_gates(xc, wa_ref, ba_ref, wx_ref, bx_ref, lam_ref)
        h = _scan(a, mult * (ri * xc), row, scratch, up=False)
        y_ref[...] = h * _gelu(gl_ref[...])

    col, vec, blk, cws = _lru_specs(S)
    return pl.pallas_call(
        body, name="lru_fwd", grid=(nb,),
        in_specs=[col(0), col(nb), cws, vec, blk, vec, blk, vec, vec],
        out_specs=pl.BlockSpec((S, LANES), lambda j: (0, j)),
        out_shape=SDS((S, W), F32), scratch_shapes=[pltpu.VMEM((S, LANES), F32)] * 3, compiler_params=_params("parallel"),
    )(proj, proj, cw, cb, wa, ba, wx, bx, lam)


def _lru_bwd(proj, dy, cw, cb, wa, ba, wx, bx, lam, wat, wxt):
    S = proj.shape[0]
    W = cb.shape[1]
    nb = W // LANES

    def body(xl_ref, gl_ref, dy_ref, cw_ref, cb_ref, wa_ref, ba_ref, wx_ref, bx_ref, lam_ref, wat_ref, wxt_ref,
             dp_ref, dwa_ref, dwx_ref, vec_ref, *scratch):
        row = lax.broadcasted_iota(jnp.int32, (S, LANES), 0)
        xl = xl_ref[...]
        xc = _conv(xl, cw_ref, cb_ref, row)
        ra, ri, ls, a, mult = _lru_gates(xc, wa_ref, ba_ref, wx_ref, bx_ref, lam_ref)
        h = _scan(a, mult * (ri * xc), row, scratch, up=False)
        gl = gl_ref[...]
        dyv = dy_ref[...]
        dp_ref[1] = (dyv * h * _gelu_grad(gl)).astype(ACT_DTYPE)
        adj = _scan(_shift_up(a, 1, row, 0.0), dyv * _gelu(gl), row, scratch, up=True)
        da = adj * _shift_down(h, 1, row, 0.0)
        dmult = adj * (ri * xc)
        dlog_a = da * a - dmult * (a * a) / mult
        dra = dlog_a * (RG_LRU_C * ls)
        dpa = dra * ra * (1.0 - ra)
        dpi = (adj * mult * xc) * ri * (1.0 - ri)
        dxc = adj * mult * ri + _dot(dpa, wat_ref[...]) + _dot(dpi, wxt_ref[...])
        dwa_ref[...] = _dot_tn(xc, dpa)
        dwx_ref[...] = _dot_tn(xc, dpi)
        dxl = dxc * cw_ref[3:4, :]
        dxl = dxl + _shift_up(dxc, 1, row, 0.0) * cw_ref[2:3, :]
        dxl = dxl + _shift_up(dxc, 2, row, 0.0) * cw_ref[1:2, :]
        dxl = dxl + _shift_up(dxc, 3, row, 0.0) * cw_ref[0:1, :]
        dp_ref[0] = dxl.astype(ACT_DTYPE)
        vec_ref[...] = jnp.zeros_like(vec_ref)
        vec_ref[0:1, :] = _rowsum(dpa)
        vec_ref[1:2, :] = _rowsum(dpi)
        vec_ref[2:3, :] = _rowsum(dlog_a * (RG_LRU_C * ra)) * _sigmoid(-lam_ref[...])
        vec_ref[3:4, :] = _rowsum(dxc)
        vec_ref[4:5, :] = _rowsum(dxc * _shift_down(xl, 3, row, 0.0))
        vec_ref[5:6, :] = _rowsum(dxc * _shift_down(xl, 2, row, 0.0))
        vec_ref[6:7, :] = _rowsum(dxc * _shift_down(xl, 1, row, 0.0))
        vec_ref[7:8, :] = _rowsum(dxc * xl)

    col, vec, blk, cws = _lru_specs(S)
    return pl.pallas_call(
        body, name="lru_bwd", grid=(nb,),
        in_specs=[col(0), col(nb), col(0), cws, vec, blk, vec, blk, vec, vec, blk, blk],
        out_specs=[pl.BlockSpec((2, S, LANES), lambda j: (0, 0, j)), blk, blk, pl.BlockSpec((2 * SUBLANES, LANES), lambda j: (0, j))],
        out_shape=[SDS((2, S, W), ACT_DTYPE), SDS((nb, LANES, LANES), F32), SDS((nb, LANES, LANES), F32), SDS((2 * SUBLANES, W), F32)],
        scratch_shapes=[pltpu.VMEM((S, LANES), F32)] * 3, compiler_params=_params("parallel"),
    )(proj, proj, dy, cw, cb, wa, ba, wx, bx, lam, wat, wxt)


def _seg_mean(x, seg_ref, width):
    hi = x.astype(jnp.bfloat16)
    lo = (x - hi.astype(F32)).astype(jnp.bfloat16)
    ones = seg_ref[...]
    s = jnp.dot(hi, ones, preferred_element_type=F32) + jnp.dot(lo, ones, preferred_element_type=F32)
    return s * (1.0 / width)


def _gmlp_core(u_ref, v_ref, gv_ref, seg_ref, ws_ref, bfull_ref, z_scr, hd):
    tm, W = u_ref.shape
    lane = lax.broadcasted_iota(jnp.int32, (CHUNK, LANES), 1)
    ug = _gelu(u_ref[...])
    vg = _gelu(v_ref[...])
    cen = vg - _seg_mean(vg, seg_ref, hd)
    rstd = lax.rsqrt(_seg_mean(cen * cen, seg_ref, hd) + EPS)
    vhat = cen * rstd
    vh = vhat * gv_ref[...]
    vcats = {}
    for ci in range(tm // CHUNK):
        for p in range(W // LANES):
            blk = vh[ci * CHUNK:(ci + 1) * CHUNK, p * LANES:(p + 1) * LANES]
            vcat = jnp.concatenate([jnp.where(lane < hd, blk, 0.0), jnp.where(lane >= hd, blk, 0.0)], axis=0).astype(MXU_DTYPE)
            vcats[ci, p] = vcat
            z_scr[ci * CHUNK:(ci + 1) * CHUNK, p * LANES:(p + 1) * LANES] = (
                jnp.dot(ws_ref[p], vcat, preferred_element_type=F32) + bfull_ref[:, p * LANES:(p + 1) * LANES])
    return ug, vhat, rstd, vcats


def _gmlp_specs(tm, W, nb):
    rows = lambda off: pl.BlockSpec((tm, W), lambda i: (i, off))
    vec = pl.BlockSpec((1, W), lambda i: (0, 0))
    seg = pl.BlockSpec((W, W), lambda i: (0, 0))
    wsp = pl.BlockSpec((nb, CHUNK, 2 * CHUNK), lambda i: (0, 0, 0))
    bfull = pl.BlockSpec((CHUNK, W), lambda i: (0, 0))
    return rows, vec, seg, wsp, bfull


def _gmlp_fwd(proj, ylru, gv, seg, wsp, bfull, g_lru, g_gm):
    S, W = ylru.shape
    nb = W // LANES
    hd = W // HEADS
    tm = _tile(S, 512)

    def body(u_ref, v_ref, yl_ref, gv_ref, seg_ref, ws_ref, bfull_ref, gl_ref, gg_ref, yn_ref, ygm_ref, z_scr):
        ug, _, _, _ = _gmlp_core(u_ref, v_ref, gv_ref, seg_ref, ws_ref, bfull_ref, z_scr, hd)
        ygm = ug * z_scr[...]
        ygm_ref[...] = ygm
        yl = yl_ref[...]
        yn_ref[:, 0:W] = (yl * _rsqrt_ms(yl) * gl_ref[...]).astype(ACT_DTYPE)
        yn_ref[:, W:2 * W] = (ygm * _rsqrt_ms(ygm) * gg_ref[...]).astype(ACT_DTYPE)

    rows, vec, segs, wsps, bfulls = _gmlp_specs(tm, W, nb)
    return pl.pallas_call(
        body, name="gmlp_fwd", grid=(S // tm,),
        in_specs=[rows(2), rows(3), rows(0), vec, segs, wsps, bfulls, vec, vec],
        out_specs=[pl.BlockSpec((tm, 2 * W), lambda i: (i, 0)), rows(0)],
        out_shape=[SDS((S, 2 * W), ACT_DTYPE), SDS((S, W), F32)],
        scratch_shapes=[pltpu.VMEM((tm, W), F32)],
        compiler_params=_params("parallel"),
    )(proj, proj, ylru, gv, seg, wsp, bfull, g_lru, g_gm)


def _rms_bwd(y, g, dyn):
    r = _rsqrt_ms(y)
    yhat = y * r
    dyh = dyn * g
    return r * (dyh - yhat * jnp.mean(dyh * yhat, axis=-1, keepdims=True)), _rowsum(dyn * yhat)


def _gmlp_bwd(proj, ylru, ygm, dyn, gv, seg, wsp, wspt, bfull, g_lru, g_gm):
    S, W = ylru.shape
    nb = W // LANES
    hd = W // HEADS
    tm = _tile(S, 256)

    def body(u_ref, v_ref, yl_ref, ygm_ref, dl_ref, dg_ref, gv_ref, seg_ref, ws_ref, wst_ref, bfull_ref, gl_ref, gg_ref,
             dyl_ref, duv_ref, dws_ref, dbf_ref, acc_ref, z_scr, dvh_scr):
        @pl.when(pl.program_id(0) == 0)
        def _():
            dws_ref[...] = jnp.zeros_like(dws_ref)
            dbf_ref[...] = jnp.zeros_like(dbf_ref)
            acc_ref[...] = jnp.zeros_like(acc_ref)

        dyl, dgl = _rms_bwd(yl_ref[...], gl_ref[...], dl_ref[...])
        dyl_ref[...] = dyl
        dygm, dgg = _rms_bwd(ygm_ref[...], gg_ref[...], dg_ref[...])
        ug, vhat, rstd, vcats = _gmlp_core(u_ref, v_ref, gv_ref, seg_ref, ws_ref, bfull_ref, z_scr, hd)
        duv_ref[0] = (dygm * z_scr[...] * _gelu_grad(u_ref[...])).astype(ACT_DTYPE)
        dz = dygm * ug
        lane = lax.broadcasted_iota(jnp.int32, (CHUNK, LANES), 1)
        dbf = dz[0:CHUNK, :]
        for ci in range(1, tm // CHUNK):
            dbf += dz[ci * CHUNK:(ci + 1) * CHUNK, :]
        dbf_ref[...] += dbf
        for ci in range(tm // CHUNK):
            for p in range(nb):
                dzb = dz[ci * CHUNK:(ci + 1) * CHUNK, p * LANES:(p + 1) * LANES].astype(MXU_DTYPE)
                dws_ref[p] += _dot_nt(dzb, vcats[ci, p])
                dvc = jnp.dot(wst_ref[p], dzb, preferred_element_type=F32)
                dvh_scr[ci * CHUNK:(ci + 1) * CHUNK, p * LANES:(p + 1) * LANES] = jnp.where(lane < hd, dvc[0:CHUNK], dvc[CHUNK:2 * CHUNK])
        dvh = dvh_scr[...]
        dvn = dvh * gv_ref[...]
        dvg = rstd * (dvn - _seg_mean(dvn, seg_ref, hd) - vhat * _seg_mean(dvn * vhat, seg_ref, hd))
        duv_ref[1] = (dvg * _gelu_grad(v_ref[...])).astype(ACT_DTYPE)
        acc_ref[0:1, :] += dgl
        acc_ref[1:2, :] += dgg
        acc_ref[2:3, :] += _rowsum(dvh * vhat)

    rows, vec, segs, wsps, bfulls = _gmlp_specs(tm, W, nb)
    wspt_spec = pl.BlockSpec((nb, 2 * CHUNK, CHUNK), lambda i: (0, 0, 0))
    return pl.pallas_call(
        body, name="gmlp_bwd", grid=(S // tm,),
        in_specs=[rows(2), rows(3), rows(0), rows(0), rows(0), rows(1), vec, segs, wsps, wspt_spec, bfulls, vec, vec],
        out_specs=[rows(0), pl.BlockSpec((2, tm, W), lambda i: (0, i, 0)), wsps, bfulls, pl.BlockSpec((SUBLANES, W), lambda i: (0, 0))],
        out_shape=[SDS((S, W), F32), SDS((2, S, W), ACT_DTYPE), SDS((nb, CHUNK, 2 * CHUNK), F32), SDS((CHUNK, W), F32), SDS((SUBLANES, W), F32)],
        scratch_shapes=[pltpu.VMEM((tm, W), F32), pltpu.VMEM((tm, W), F32)],
        compiler_params=_params("arbitrary"),
    )(proj, proj, ylru, ygm, dyn, dyn, gv, seg, wsp, wspt, bfull, g_lru, g_gm)


def _ada_fwd(c_all, w_ada, b_shard):
    L, D, N = w_ada.shape
    R = c_all.shape[0]
    tn = N // 2

    def body(c_ref, w_ref, b_ref, o_ref):
        cv = c_ref[...]
        o_ref[...] = _dot(cv * _sigmoid(cv), w_ref[...]) + b_ref[...]

    return pl.pallas_call(
        body, name="ada_fwd", grid=(L, N // tn),
        in_specs=[pl.BlockSpec((R, D), lambda l, j: (0, 0)), pl.BlockSpec((None, D, tn), lambda l, j: (l, 0, j)),
                  pl.BlockSpec((None, 1, tn), lambda l, j: (l, 0, j))],
        out_specs=pl.BlockSpec((None, R, tn), lambda l, j: (l, 0, j)),
        out_shape=SDS((L, R, N), F32), compiler_params=_params("parallel", "parallel"),
    )(c_all, w_ada, b_shard)


def _ada_grad(c_all_t, dmod):
    D, B = c_all_t.shape
    L, _, N = dmod.shape
    tn = N // 2

    def body(c_ref, d_ref, o_ref):
        cv = c_ref[...]
        sc = cv * _sigmoid(cv)
        acc = sc[:, 0:1] * d_ref[0:1, :]
        for b in range(1, B):
            acc += sc[:, b:b + 1] * d_ref[b:b + 1, :]
        o_ref[...] = acc

    return pl.pallas_call(
        body, name="ada_grad", grid=(L, N // tn),
        in_specs=[pl.BlockSpec((D, B), lambda l, j: (0, 0)), pl.BlockSpec((None, B, tn), lambda l, j: (l, 0, j))],
        out_specs=pl.BlockSpec((None, D, tn), lambda l, j: (l, 0, j)),
        out_shape=SDS((L, D, N), F32), compiler_params=_params("parallel", "parallel"),
    )(c_all_t, dmod)


def _adamw(w, g, m, v):
    R, C = w.shape
    tr = _row_tile(R, C * 4)

    def body(w_ref, g_ref, m_ref, v_ref, d_ref, mo_ref, vo_ref):
        d_ref[...], mo_ref[...], vo_ref[...] = _adam_math(w_ref[...], g_ref[...], m_ref[...], v_ref[...])

    tile = pl.BlockSpec((tr, C), lambda i: (i, 0))
    return pl.pallas_call(
        body, name=f"adamw_r{R}c{C}", grid=(R // tr,), in_specs=[tile] * 4, out_specs=[tile] * 3,
        out_shape=[SDS((R, C), F32)] * 3, compiler_params=_params("parallel"),
    )(w, g, m, v)


def _adam_math(w, g, m, v):
    mn = ADAM_B1 * m + (1.0 - ADAM_B1) * g
    vn = ADAM_B2 * v + (1.0 - ADAM_B2) * (g * g)
    m_hat = mn / (1.0 - ADAM_B1 ** ADAM_STEP)
    v_hat = vn / (1.0 - ADAM_B2 ** ADAM_STEP)
    return -ADAM_LR * (m_hat / (jnp.sqrt(v_hat) + ADAM_EPS) + ADAM_WD * w), mn, vn


def _adamw_layer(w, g, m, v, l, prev, after):
    L, R, C = w.shape
    tr = _row_tile(R, C * 4)
    prev = (after,) + (() if prev is None else tuple(prev))

    def body(w_ref, g_ref, m_ref, v_ref, *rest):
        go_ref, d_ref, mo_ref, vo_ref = rest[len(prev):]
        gv = g_ref[...]
        go_ref[...] = gv
        d_ref[...], mo_ref[...], vo_ref[...] = _adam_math(w_ref[...], gv, m_ref[...], v_ref[...])

    lay = pl.BlockSpec((None, tr, C), lambda i: (l, i, 0))
    return pl.pallas_call(
        body, name=f"adamw_layer_r{R}c{C}", grid=(R // tr,),
        in_specs=[lay, pl.BlockSpec((tr, C), lambda i: (i, 0)), lay, lay] + [ANY] * len(prev), out_specs=[lay] * 4,
        out_shape=[SDS((L, R, C), F32)] * 4, input_output_aliases={5 + k: k for k in range(len(prev) - 1)},
        compiler_params=_params("parallel"),
    )(w, g, m, v, *prev)


def _sum_leading(a):
    P, R, C = a.shape
    tr = _row_tile(R, P * C * 4)

    def body(a_ref, o_ref):
        acc = a_ref[0]
        for p in range(1, P):
            acc = acc + a_ref[p]
        o_ref[...] = acc

    return pl.pallas_call(
        body, name=f"sum{P}_r{R}c{C}", grid=(R // tr,),
        in_specs=[pl.BlockSpec((P, tr, C), lambda i: (0, i, 0))],
        out_specs=pl.BlockSpec((tr, C), lambda i: (i, 0)),
        out_shape=SDS((R, C), F32), compiler_params=_params("parallel"),
    )(a)


def _add_half(g4, r1, place):
    _, _, R, C = g4.shape
    tr = _row_tile(R, C * 4)

    def body(place_ref, g_ref, r_ref, h_ref, own_ref):
        s = (g_ref[...] + r_ref[...]).astype(XFER_DTYPE)
        h_ref[...] = s

        @pl.when(pl.program_id(1) == place_ref[1])
        def _():
            own_ref[...] = s

    return pl.pallas_call(
        body, name=f"add_half_r{R}c{C}",
        grid_spec=pltpu.PrefetchScalarGridSpec(
            num_scalar_prefetch=1, grid=(R // tr, N_CHIPS),
            in_specs=[pl.BlockSpec((None, None, tr, C), lambda i, p, place_ref: (p, place_ref[0], i, 0)),
                      pl.BlockSpec((None, tr, C), lambda i, p, place_ref: (p, i, 0))],
            out_specs=[pl.BlockSpec((None, tr, C), lambda i, p, place_ref: (p, i, 0)),
                       pl.BlockSpec((None, tr, C), lambda i, p, place_ref: (place_ref[1], i, 0))],
        ),
        out_shape=[SDS((N_CHIPS, R, C), XFER_DTYPE)] * 2, compiler_params=_params("parallel", "arbitrary"),
    )(place, g4, r1)


def _sum4_into_half(r2, place):
    P, R, C = r2.shape
    tr = _row_tile(R, P * C * 4)

    def body(place_ref, a_ref, o_ref):
        acc = a_ref[0].astype(F32)
        for p in range(1, P):
            acc = acc + a_ref[p].astype(F32)
        o_ref[...] = acc

    return pl.pallas_call(
        body, name=f"sum4_r{R}c{C}",
        grid_spec=pltpu.PrefetchScalarGridSpec(
            num_scalar_prefetch=1, grid=(R // tr,),
            in_specs=[pl.BlockSpec((P, tr, C), lambda i, place_ref: (0, i, 0))],
            out_specs=pl.BlockSpec((None, tr, C), lambda i, place_ref: (place_ref[0], i, 0)),
        ),
        out_shape=SDS((2, R, C), F32), compiler_params=_params("parallel"),
    )(place, r2)


def _cast_into_slot(w, l, place, after):
    _, R, C = w.shape
    tr = _row_tile(R, C * 4)

    def body(place_ref, w_ref, after_ref, o_ref):
        o_ref[...] = w_ref[...].astype(MXU_DTYPE)

    return pl.pallas_call(
        body, name=f"cast_r{R}c{C}",
        grid_spec=pltpu.PrefetchScalarGridSpec(
            num_scalar_prefetch=1, grid=(R // tr,),
            in_specs=[pl.BlockSpec((None, tr, C), lambda i, place_ref: (l, i, 0)), ANY],
            out_specs=pl.BlockSpec((None, tr, C), lambda i, place_ref: (place_ref[1], i, 0)),
        ),
        out_shape=SDS((N_CHIPS, R, C), MXU_DTYPE), compiler_params=_params("parallel"),
    )(place, w, after)


def _place():
    x, y, c = lax.axis_index("x"), lax.axis_index("y"), lax.axis_index("c")
    chips = [(1 - x, y), (x, 1 - y), (1 - x, 1 - y)]
    return x, y, c, chips


def _remote(src, dst, send_sem, recv_sem, to):
    return pltpu.make_async_remote_copy(src_ref=src, dst_ref=dst, send_sem=send_sem, recv_sem=recv_sem, device_id=to, device_id_type=MESH)


def _all_gather8(v):
    R, N = v.shape

    def body(v_ref, out_ref, send_sems, recv_sems, local_sem):
        x, y, c, chips = _place()
        me, sibling = (x, y, c), (x, y, 1 - c)

        def slot(px, py, pc):
            return out_ref.at[4 * px + 2 * py + pc]

        def copy(k, block, to, src=None):
            return _remote(slot(*block) if src is None else src, slot(*block), send_sems.at[k], recv_sems.at[k], to)

        mine = pltpu.make_async_copy(v_ref, slot(*me), local_sem)
        mine.start()
        first = [copy(0, me, sibling, src=v_ref)] + [copy(1 + j, me, (*chip, c), src=v_ref) for j, chip in enumerate(chips)]
        for cp in first:
            cp.start()
        passed = [copy(4 + j, (*chip, c), sibling) for j, chip in enumerate(chips)]
        for j, chip in enumerate(chips):
            copy(1 + j, (*chip, c), me).wait_recv()
            passed[j].start()
        copy(0, sibling, me).wait_recv()
        for j, chip in enumerate(chips):
            copy(4 + j, (*chip, 1 - c), me).wait_recv()
        for cp in first + passed:
            cp.wait_send()
        mine.wait()

    return pl.pallas_call(
        body, name=f"all_gather8_r{R}n{N}", out_shape=SDS((N_DEV, R, N), v.dtype), in_specs=[ANY], out_specs=ANY,
        scratch_shapes=[pltpu.SemaphoreType.DMA((7,)), pltpu.SemaphoreType.DMA((7,)), pltpu.SemaphoreType.DMA],
    )(v)


def _gather_weights(slots):
    n = len(slots)

    def body(*refs):
        ins, outs = refs[:n], refs[n:2 * n]
        send_sems, recv_sems = refs[2 * n:]
        x, y, c, chips = _place()
        q = 2 * x + y
        sibling = (x, y, 1 - c)
        first = []
        for k in range(n):
            for j, chip in enumerate(chips):
                first.append(_remote(ins[k].at[q, c], outs[k].at[q, c], send_sems.at[k, j], recv_sems.at[k, j], (*chip, c)))
                first[-1].start()
        passed = []
        for k in range(n):
            for j, chip in enumerate(chips):
                half = outs[k].at[2 * chip[0] + chip[1], c]
                _remote(half, half, send_sems.at[k, j], recv_sems.at[k, j], sibling).wait_recv()
                passed.append(_remote(half, half, send_sems.at[k, 3 + j], recv_sems.at[k, 3 + j], sibling))
                passed[-1].start()
        for k in range(n):
            for j, chip in enumerate(chips):
                half = outs[k].at[2 * chip[0] + chip[1], 1 - c]
                _remote(half, half, send_sems.at[k, 3 + j], recv_sems.at[k, 3 + j], sibling).wait_recv()
        for cp in first + passed:
            cp.wait_send()

    return pl.pallas_call(
        body, name="gather_weights", out_shape=[SDS(s.shape, s.dtype) for s in slots],
        in_specs=[ANY] * n, out_specs=[ANY] * n, input_output_aliases={k: k for k in range(n)},
        scratch_shapes=[pltpu.SemaphoreType.DMA((n, 6)), pltpu.SemaphoreType.DMA((n, 6))],
    )(*slots)


def _swap_halves(g4s):
    n = len(g4s)

    def body(*refs):
        ins, outs = refs[:n], refs[n:2 * n]
        send_sems, recv_sems = refs[2 * n:]
        x, y, c, _ = _place()
        sibling = (x, y, 1 - c)
        for k in range(n):
            for p in range(N_CHIPS):
                _remote(ins[k].at[p, 1 - c], outs[k].at[p], send_sems.at[k], recv_sems.at[k], sibling).start()
        for k in range(n):
            _remote(outs[k], outs[k], send_sems.at[k], recv_sems.at[k], sibling).wait()

    return pl.pallas_call(
        body, name="swap_halves", out_shape=[SDS((N_CHIPS,) + g.shape[2:], g.dtype) for g in g4s],
        in_specs=[ANY] * n, out_specs=[ANY] * n,
        scratch_shapes=[pltpu.SemaphoreType.DMA((n,)), pltpu.SemaphoreType.DMA((n,))],
    )(*g4s)


def _scatter_regions(hs, lands):
    n = len(hs)

    def body(*refs):
        ins, outs = refs[:n], refs[2 * n:3 * n]
        send_sems, recv_sems = refs[3 * n:]
        x, y, c, chips = _place()
        q = 2 * x + y
        sent = []
        for k in range(n):
            for j, chip in enumerate(chips):
                sent.append(_remote(ins[k].at[2 * chip[0] + chip[1]], outs[k].at[q], send_sems.at[k, j], recv_sems.at[k, j], (*chip, c)))
                sent[-1].start()
        for k in range(n):
            for j, chip in enumerate(chips):
                got = outs[k].at[2 * chip[0] + chip[1]]
                _remote(got, got, send_sems.at[k, j], recv_sems.at[k, j], (x, y, c)).wait_recv()
        for cp in sent:
            cp.wait_send()

    return pl.pallas_call(
        body, name="scatter_regions", out_shape=[SDS(h.shape, h.dtype) for h in lands],
        in_specs=[ANY] * (2 * n), out_specs=[ANY] * n, input_output_aliases={n + k: k for k in range(n)},
        scratch_shapes=[pltpu.SemaphoreType.DMA((n, 3)), pltpu.SemaphoreType.DMA((n, 3))],
    )(*hs, *lands)


def _share_halves(fins):
    n = len(fins)

    def body(*refs):
        ins, outs = refs[:n], refs[n:2 * n]
        send_sems, recv_sems = refs[2 * n:]
        x, y, c, _ = _place()
        sibling = (x, y, 1 - c)
        sent = [_remote(ins[k].at[c], outs[k].at[c], send_sems.at[k], recv_sems.at[k], sibling) for k in range(n)]
        for cp in sent:
            cp.start()
        for k in range(n):
            got = outs[k].at[1 - c]
            _remote(got, got, send_sems.at[k], recv_sems.at[k], sibling).wait_recv()
        for cp in sent:
            cp.wait_send()

    return pl.pallas_call(
        body, name="share_halves", out_shape=[SDS(t.shape, t.dtype) for t in fins],
        in_specs=[ANY] * n, out_specs=[ANY] * n, input_output_aliases={k: k for k in range(n)},
        scratch_shapes=[pltpu.SemaphoreType.DMA((n,)), pltpu.SemaphoreType.DMA((n,))],
    )(*fins)


def _chip_sums(grads, place):
    g4s = [g.reshape(N_CHIPS, 2, g.shape[1] // 2, g.shape[2]) for g in grads]
    pairs = [_add_half(g4, r1, place) for g4, r1 in zip(g4s, _swap_halves(g4s))]
    return [h for h, _ in pairs], [own for _, own in pairs]


def _reduce_finish(lands, place):
    fins = _share_halves([_sum4_into_half(r2, place) for r2 in lands])
    return [f.reshape(2 * f.shape[1], f.shape[2]) for f in fins]


def _reduce_scatter(grads, place):
    hs, lands = _chip_sums(grads, place)
    return _reduce_finish(_scatter_regions(hs, lands), place)


HBM_SPEC = pl.BlockSpec(memory_space=pltpu.HBM)
SEM_SPEC = pl.BlockSpec(memory_space=pltpu.SEMAPHORE)
DATAFLOW = pltpu.SideEffectType.DATAFLOW_SIDE_EFFECTING


def _in_hbm(a):
    return pltpu.with_memory_space_constraint(a, pltpu.HBM)


def _hbm_like(a):
    return pltpu.HBM(a.shape, a.dtype)


def _gather_ici_start(slots, thru, after):
    n = len(slots)

    def body(*refs):
        ins = refs[:n]
        send_sems, recv_sems = refs[n + 2], refs[n + 3]
        x, y, c, chips = _place()
        q = 2 * x + y
        for k in range(n):
            for j, chip in enumerate(chips):
                _remote(ins[k].at[q, c], ins[k].at[q, c], send_sems.at[3 * k + j], recv_sems.at[3 * k + j], (*chip, c)).start()

    out = pl.pallas_call(
        body, name="gather_ici_start",
        out_shape=(pltpu.SemaphoreType.DMA((3 * n,)), pltpu.SemaphoreType.DMA((3 * n,)), *[_hbm_like(s) for s in slots], _hbm_like(thru)),
        in_specs=[HBM_SPEC] * (n + 1) + [ANY], out_specs=(SEM_SPEC, SEM_SPEC, *[HBM_SPEC] * (n + 1)),
        input_output_aliases={k: 2 + k for k in range(n + 1)},
        compiler_params=pltpu.CompilerParams(has_side_effects=DATAFLOW),
    )(*[_in_hbm(s) for s in slots], _in_hbm(thru), after)
    return out[0], out[1], list(out[2:2 + n]), out[2 + n]


def _gather_ici_wait(send_sems, recv_sems, slots, after):
    n = len(slots)

    def body(*refs):
        ins = refs[:n]
        send_sems, recv_sems = refs[n], refs[n + 1]
        x, y, c, chips = _place()
        for k in range(n):
            for j, chip in enumerate(chips):
                got = ins[k].at[2 * chip[0] + chip[1], c]
                cp = _remote(got, got, send_sems.at[3 * k + j], recv_sems.at[3 * k + j], (x, y, c))
                cp.wait_send()
                cp.wait_recv()

    out = pl.pallas_call(
        body, name="gather_ici_wait", out_shape=[_hbm_like(s) for s in slots],
        in_specs=[HBM_SPEC] * n + [SEM_SPEC, SEM_SPEC, ANY], out_specs=[HBM_SPEC] * n,
        input_output_aliases={k: k for k in range(n)},
        compiler_params=pltpu.CompilerParams(has_side_effects=DATAFLOW),
    )(*slots, send_sems, recv_sems, after)
    return list(out)


def _gather_pass_on(slots):
    n = len(slots)

    def body(*refs):
        ins, outs = refs[:n], refs[n:2 * n]
        send_sems, recv_sems = refs[2 * n:]
        x, y, c, chips = _place()
        sibling = (x, y, 1 - c)
        passed = []
        for k in range(n):
            for j, chip in enumerate(chips):
                passed.append(_remote(ins[k].at[2 * chip[0] + chip[1], c], outs[k].at[2 * chip[0] + chip[1], c],
                                      send_sems.at[k, j], recv_sems.at[k, j], sibling))
                passed[-1].start()
        for k in range(n):
            for j, chip in enumerate(chips):
                half = outs[k].at[2 * chip[0] + chip[1], 1 - c]
                _remote(half, half, send_sems.at[k, j], recv_sems.at[k, j], sibling).wait_recv()
        for cp in passed:
            cp.wait_send()

    return pl.pallas_call(
        body, name="gather_pass_on", out_shape=[SDS(s.shape, s.dtype) for s in slots],
        in_specs=[ANY] * n, out_specs=[ANY] * n, input_output_aliases={k: k for k in range(n)},
        scratch_shapes=[pltpu.SemaphoreType.DMA((n, 3)), pltpu.SemaphoreType.DMA((n, 3))],
    )(*slots)


def _scatter_start(hs, lands, thru):
    n = len(hs)

    def body(*refs):
        ins, zones = refs[:n], refs[n:2 * n]
        send_sems, recv_sems = refs[2 * n + 1], refs[2 * n + 2]
        x, y, c, chips = _place()
        q = 2 * x + y
        for k in range(n):
            for j, chip in enumerate(chips):
                _remote(ins[k].at[2 * chip[0] + chip[1]], zones[k].at[q], send_sems.at[3 * k + j], recv_sems.at[3 * k + j], (*chip, c)).start()

    arrays = [*hs, *lands, thru]
    out = pl.pallas_call(
        body, name="scatter_start",
        out_shape=(pltpu.SemaphoreType.DMA((3 * n,)), pltpu.SemaphoreType.DMA((3 * n,)), *[_hbm_like(a) for a in arrays]),
        in_specs=[HBM_SPEC] * len(arrays), out_specs=(SEM_SPEC, SEM_SPEC, *[HBM_SPEC] * len(arrays)),
        input_output_aliases={k: 2 + k for k in range(len(arrays))},
        compiler_params=pltpu.CompilerParams(has_side_effects=DATAFLOW),
    )(*[_in_hbm(a) for a in arrays])
    return out[0], out[1], list(out[2:2 + n]), list(out[2 + n:2 + 2 * n]), out[2 + 2 * n]


def _scatter_wait(send_sems, recv_sems, hs, lands, after):
    n = len(hs)

    def body(*refs):
        ins, zones = refs[:n], refs[n:2 * n]
        send_sems, recv_sems = refs[2 * n], refs[2 * n + 1]
        x, y, c, chips = _place()
        for k in range(n):
            for j, chip in enumerate(chips):
                p = 2 * chip[0] + chip[1]
                cp = _remote(ins[k].at[p], zones[k].at[p], send_sems.at[3 * k + j], recv_sems.at[3 * k + j], (x, y, c))
                cp.wait_send()
                cp.wait_recv()

    out = pl.pallas_call(
        body, name="scatter_wait", out_shape=[_hbm_like(a) for a in [*hs, *lands]],
        in_specs=[HBM_SPEC] * (2 * n) + [SEM_SPEC, SEM_SPEC, ANY], out_specs=[HBM_SPEC] * (2 * n),
        input_output_aliases={k: k for k in range(2 * n)},
        compiler_params=pltpu.CompilerParams(has_side_effects=DATAFLOW),
    )(*hs, *lands, send_sems, recv_sems, after)
    return list(out[n:])


def _split_start(name, arrays, n_sems, issue, extra=()):
    m = len(arrays)

    def body(*refs):
        issue(refs[:m], refs[m + len(extra)], refs[m + len(extra) + 1])

    out = pl.pallas_call(
        body, name=name,
        out_shape=(pltpu.SemaphoreType.DMA((n_sems,)), pltpu.SemaphoreType.DMA((n_sems,)), *[_hbm_like(a) for a in arrays]),
        in_specs=[HBM_SPEC] * m + [ANY] * len(extra), out_specs=(SEM_SPEC, SEM_SPEC, *[HBM_SPEC] * m),
        input_output_aliases={k: 2 + k for k in range(m)},
        compiler_params=pltpu.CompilerParams(has_side_effects=DATAFLOW),
    )(*[_in_hbm(a) for a in arrays], *extra)
    return out[0], out[1], list(out[2:])


def _split_wait(name, send_sems, recv_sems, arrays, after, drain):
    m = len(arrays)

    def body(*refs):
        drain(refs[:m], refs[m], refs[m + 1])

    out = pl.pallas_call(
        body, name=name, out_shape=[_hbm_like(a) for a in arrays],
        in_specs=[HBM_SPEC] * m + [SEM_SPEC, SEM_SPEC, ANY], out_specs=[HBM_SPEC] * m,
        input_output_aliases={k: k for k in range(m)},
        compiler_params=pltpu.CompilerParams(has_side_effects=DATAFLOW),
    )(*arrays, send_sems, recv_sems, after)
    return list(out)


def _wait_both(cp):
    cp.wait_send()
    cp.wait_recv()


class _Flight:
    def __init__(self, name, arrays, n_sems, issue, drain, thru, extra=()):
        self.name, self.drain, self.n = name, drain, len(arrays)
        self.send, self.recv, out = _split_start(name + "_start", [*arrays, thru], n_sems, issue, extra)
        self.arrays, self.thru = out[:-1], out[-1]

    def land(self, after):
        return _split_wait(self.name + "_wait", self.send, self.recv, self.arrays, after, self.drain)


def _gather_flight(tag, ici, d2d, direct, thru):
    kinds = ["ici"] * len(ici) + ["d2d"] * len(d2d) + ["direct"] * len(direct)

    def issue(refs, send_sems, recv_sems):
        x, y, c, chips = _place()
        q = 2 * x + y
        for k, kind in enumerate(kinds):
            for j, chip in enumerate(chips):
                if kind == "ici":
                    src, to = refs[k].at[q, c], (*chip, c)
                elif kind == "d2d":
                    src, to = refs[k].at[2 * chip[0] + chip[1], c], (x, y, 1 - c)
                else:
                    src, to = refs[k].at[q], (*chip, c)
                _remote(src, src, send_sems.at[3 * k + j], recv_sems.at[3 * k + j], to).start()

    def drain(refs, send_sems, recv_sems):
        x, y, c, chips = _place()
        for k, kind in enumerate(kinds):
            for j, chip in enumerate(chips):
                p = 2 * chip[0] + chip[1]
                got = refs[k].at[p] if kind == "direct" else refs[k].at[p, c if kind == "ici" else 1 - c]
                _wait_both(_remote(got, got, send_sems.at[3 * k + j], recv_sems.at[3 * k + j], (x, y, c)))

    return _Flight(f"gather{tag}", [*ici, *d2d, *direct], 3 * len(kinds), issue, drain, thru)


def _swap_flight(tag, g4s, thru):
    n = len(g4s)
    zones = [lax.empty((N_CHIPS,) + g.shape[2:], g.dtype) for g in g4s]

    def issue(refs, send_sems, recv_sems):
        x, y, c, _ = _place()
        for k in range(n):
            for p in range(N_CHIPS):
                _remote(refs[k].at[p, 1 - c], refs[n + k].at[p], send_sems.at[N_CHIPS * k + p], recv_sems.at[N_CHIPS * k + p], (x, y, 1 - c)).start()

    def drain(refs, send_sems, recv_sems):
        x, y, c, _ = _place()
        for k in range(n):
            for p in range(N_CHIPS):
                got = refs[n + k].at[p]
                _wait_both(_remote(got, got, send_sems.at[N_CHIPS * k + p], recv_sems.at[N_CHIPS * k + p], (x, y, c)))

    return _Flight(f"swap{tag}", [*g4s, *zones], N_CHIPS * n, issue, drain, thru)


def _scatter_flight(tag, hs, lands, thru):
    n = len(hs)

    def issue(refs, send_sems, recv_sems):
        x, y, c, chips = _place()
        q = 2 * x + y
        for k in range(n):
            for j, chip in enumerate(chips):
                _remote(refs[k].at[2 * chip[0] + chip[1]], refs[n + k].at[q], send_sems.at[3 * k + j], recv_sems.at[3 * k + j], (*chip, c)).start()

    def drain(refs, send_sems, recv_sems):
        x, y, c, chips = _place()
        for k in range(n):
            for j, chip in enumerate(chips):
                got = refs[n + k].at[2 * chip[0] + chip[1]]
                _wait_both(_remote(got, got, send_sems.at[3 * k + j], recv_sems.at[3 * k + j], (x, y, c)))

    return _Flight(f"scatter{tag}", [*hs, *lands], 3 * n, issue, drain, thru)


def _exchange_flight(buf, thru):
    flips = [(fx, fy, fc) for fx in (0, 1) for fy in (0, 1) for fc in (0, 1)][1:]

    def peers():
        x, y, c, _ = _place()
        return (x, y, c), [((1 - x) if fx else x, (1 - y) if fy else y, (1 - c) if fc else c) for fx, fy, fc in flips]

    def slot(ref, dev):
        return ref.at[4 * dev[0] + 2 * dev[1] + dev[2]]

    def issue(refs, send_sems, recv_sems):
        me, others = peers()
        for j, to in enumerate(others):
            _remote(slot(refs[0], me), slot(refs[0], me), send_sems.at[j], recv_sems.at[j], to).start()

    def drain(refs, send_sems, recv_sems):
        me, others = peers()
        for j, frm in enumerate(others):
            got = slot(refs[0], frm)
            _wait_both(_remote(got, got, send_sems.at[j], recv_sems.at[j], me))

    return _Flight("exchange", [buf], len(flips), issue, drain, thru)


def _share_flight(tag, fins, thru):
    n = len(fins)

    def issue(refs, send_sems, recv_sems):
        x, y, c, _ = _place()
        for k in range(n):
            _remote(refs[k].at[c], refs[k].at[c], send_sems.at[k], recv_sems.at[k], (x, y, 1 - c)).start()

    def drain(refs, send_sems, recv_sems):
        x, y, c, _ = _place()
        for k in range(n):
            got = refs[k].at[1 - c]
            _wait_both(_remote(got, got, send_sems.at[k], recv_sems.at[k], (x, y, c)))

    return _Flight(f"share{tag}", fins, n, issue, drain, thru)


def _pair_blocks(w):
    h, d, _ = w.shape
    z = jnp.zeros((h // 2, d, d), w.dtype)
    return jnp.concatenate([jnp.concatenate([w[0::2], z], axis=2), jnp.concatenate([z, w[1::2]], axis=2)], axis=1)


def _unpair_blocks(b):
    n, dd, _ = b.shape
    d = dd // 2
    return jnp.stack([b[:, :d, :d], b[:, d:, d:]], axis=1).reshape(2 * n, d, d)


def _pad_rows(a, rows):
    return jnp.pad(a, ((0, rows - a.shape[0]), (0, 0)))


class _Packer:
    def __init__(self, shapes, width=1024, row_multiple=64):
        self.shapes = shapes
        self.sizes = [math.prod(s) for s in shapes]
        total = sum(self.sizes)
        self.width = width
        self.rows = -(-total // (width * row_multiple)) * row_multiple
        self.pad = self.rows * width - total

    def pack(self, arrays):
        flat = jnp.concatenate([a.reshape(-1).astype(F32) for a in arrays] + [jnp.zeros((self.pad,), F32)])
        return flat.reshape(self.rows, self.width)

    def unpack(self, packed):
        flat = packed.reshape(-1)
        out, off = [], 0
        for s, n in zip(self.shapes, self.sizes):
            out.append(flat[off:off + n].reshape(s))
            off += n
        return out


SMALL = ["b_ada", "ffn1_norm", "mix_norm", "conv_w", "conv_b", "gate_a_w", "gate_a_b", "gate_x_w", "gate_x_b", "lru_lambda",
         "v_norm", "spatial_w", "spatial_b", "lru_out_norm", "gmlp_out_norm", "ffn2_norm", "final_norm"]
BIG = ["ffn1_w_gu", "ffn1_w_down", "w_in", "w_out", "ffn2_w_gu", "ffn2_w_down"]
GROUPS = (("ffn1_w_gu", "ffn1_w_down"), ("w_in", "w_out"), ("ffn2_w_gu", "ffn2_w_down"))
FWD_GROUPS = (("ffn1_w_gu",), ("ffn1_w_down",), ("w_in", "w_out"), ("ffn2_w_gu",), ("ffn2_w_down",))
MIN_AGE = {"swap": 1, "scatter": 1, "share": 1}
WEIGHTS = ["w_ada", "b_ada", "ffn1_norm", "ffn1_w_gu", "ffn1_w_down", "mix_norm", "w_in", "conv_w", "conv_b", "gate_a_w", "gate_a_b",
           "gate_x_w", "gate_x_b", "lru_lambda", "v_norm", "spatial_w", "spatial_b", "lru_out_norm", "gmlp_out_norm", "w_out",
           "ffn2_norm", "ffn2_w_gu", "ffn2_w_down", "final_norm"]


def kernel(x, c, w_ada, b_ada, ffn1_norm, ffn1_w_gu, ffn1_w_down, mix_norm, w_in, conv_w, conv_b, gate_a_w, gate_a_b, gate_x_w, gate_x_b, lru_lambda, v_norm, spatial_w, spatial_b, lru_out_norm, gmlp_out_norm, w_out, ffn2_norm, ffn2_w_gu, ffn2_w_down, final_norm, loss_target, m_w_ada, m_b_ada, m_ffn1_norm, m_ffn1_w_gu, m_ffn1_w_down, m_mix_norm, m_w_in, m_conv_w, m_conv_b, m_gate_a_w, m_gate_a_b, m_gate_x_w, m_gate_x_b, m_lru_lambda, m_v_norm, m_spatial_w, m_spatial_b, m_lru_out_norm, m_gmlp_out_norm, m_w_out, m_ffn2_norm, m_ffn2_w_gu, m_ffn2_w_down, m_final_norm, v_w_ada, v_b_ada, v_ffn1_norm, v_ffn1_w_gu, v_ffn1_w_down, v_mix_norm, v_w_in, v_conv_w, v_conv_b, v_gate_a_w, v_gate_a_b, v_gate_x_w, v_gate_x_b, v_lru_lambda, v_v_norm, v_spatial_w, v_spatial_b, v_lru_out_norm, v_gmlp_out_norm, v_w_out, v_ffn2_norm, v_ffn2_w_gu, v_ffn2_w_down, v_final_norm):
    given = dict(locals())
    W = {n: given[n] for n in WEIGHTS}
    L = w_ada.shape[0]
    S, D = x.shape[1], x.shape[2]
    LW = conv_b.shape[1]
    hd = LW // HEADS
    xi, yi, ci = lax.axis_index("x"), lax.axis_index("y"), lax.axis_index("c")
    chip = 2 * xi + yi
    dev = 2 * chip + ci
    place = jnp.stack([ci, chip]).astype(jnp.int32)
    xs = x.reshape(S, D)
    tgt = loss_target.reshape(S, D)

    c_all = _all_gather8(_pad_rows(c, SUBLANES))[:, 0, :]
    n_ada = w_ada.shape[2]
    b_shard = lax.dynamic_slice_in_dim(b_ada, chip * n_ada, n_ada, axis=1)
    mod_shard = _ada_fwd(_pad_rows(c_all, 2 * SUBLANES), w_ada, b_shard[:, None, :])

    def in_slot(block):
        return lax.dynamic_update_index_in_dim(jnp.zeros((N_CHIPS,) + block.shape, block.dtype), block, chip, 0)

    cws = LW // N_CHIPS
    small = [in_slot(mod_shard.reshape(L * 2 * SUBLANES, n_ada)), in_slot(conv_w.reshape(L * CONV_WIDTH, cws))]

    def half_view(s):
        return s.reshape(N_CHIPS, 2, s.shape[1] // 2, s.shape[2])

    stages = [(l, names) for l in range(L) for names in FWD_GROUPS]
    seq = [[half_view(_cast_into_slot(W[n], l, place, place)) for n in names] for l, names in stages[:1]]
    flights = {}

    def launch(t, thru, direct=()):
        ici = seq[t] if t < len(seq) else []
        d2d = seq[t - 1] if 1 <= t <= len(seq) else []
        if ici or d2d or direct:
            flights[t] = _gather_flight(t, ici, d2d, list(direct), thru)
            thru = flights[t].thru
        return thru

    def land(t, after):
        if t not in flights:
            return []
        out = flights.pop(t).land(after)
        ni = len(seq[t]) if t < len(seq) else 0
        nd = len(seq[t - 1]) if 1 <= t <= len(seq) else 0
        if ni:
            seq[t] = out[:ni]
        if nd:
            seq[t - 1] = out[ni:ni + nd]
        return out[ni + nd:]

    def group_weights(t):
        return [s.reshape(N_CHIPS, -1, s.shape[3]) for s in seq[t]]

    c_all = launch(0, c_all, small)
    seq += [[half_view(_cast_into_slot(W[n], l, place, c_all)) for n in names] for l, names in stages[1:]]
    mod_all, conv_all = land(0, seq[-1][-1])
    mod_all = launch(1, mod_all)
    land(1, mod_all)
    mod_rows = lax.dynamic_index_in_dim(mod_all.reshape(N_CHIPS, L, 2 * SUBLANES, n_ada), dev, axis=2, keepdims=False)
    mod = mod_rows.transpose(1, 0, 2).reshape(L, N_MOD, 1, D)
    conv_full = conv_all.reshape(N_CHIPS, L, CONV_WIDTH, cws).transpose(1, 2, 0, 3).reshape(L, CONV_WIDTH, LW)

    tril = jnp.tril(jnp.ones((CHUNK, CHUNK), F32))
    seg = (jnp.arange(LW)[:, None] // hd == jnp.arange(LW)[None, :] // hd).astype(jnp.bfloat16)

    def mixer_params(l):
        ws = spatial_w[l] * tril
        wsp = jnp.concatenate([ws[0::2], ws[1::2]], axis=2)
        wa, wx = _pair_blocks(gate_a_w[l]), _pair_blocks(gate_x_w[l])
        return dict(
            cw=conv_full[l], cb=conv_b[l][None],
            wa=wa.astype(MXU_DTYPE), wx=wx.astype(MXU_DTYPE), wat=wa.transpose(0, 2, 1).astype(MXU_DTYPE), wxt=wx.transpose(0, 2, 1).astype(MXU_DTYPE),
            ba=gate_a_b[l].reshape(1, LW), bx=gate_x_b[l].reshape(1, LW), lam=lru_lambda[l][None], gv=v_norm[l][None],
            wsp=wsp.astype(MXU_DTYPE), wspt=wsp.transpose(0, 2, 1).astype(MXU_DTYPE),
            bfull=jnp.repeat(spatial_b[l].T, hd, axis=1), g_lru=lru_out_norm[l][None], g_gm=gmlp_out_norm[l][None])

    saved = []
    xcur = xs
    zero_row = jnp.zeros((1, D), F32)
    h = _modnorm(xcur, ffn1_norm[0][None], mod[0][0], mod[0][1])
    for l in range(L):
        mp, md = mixer_params(l), mod[l]
        s = dict(lw={}, mp=mp, md=md)
        lw = s["lw"]
        t = len(FWD_GROUPS) * l
        s["x0"] = xcur
        s["h1"] = launch(t + 2, h)
        lw["gu1"], = group_weights(t)
        s["a1"], s["gu1"] = _ffn_up(s["h1"], lw["gu1"])
        land(t + 2, s["a1"])
        s["a1"] = launch(t + 3, s["a1"])
        lw["d1"] = group_weights(t + 1)[0].reshape(-1, D)
        s["f1"], xcur, h = _mm_res(s["a1"], lw["d1"], xcur, md[2], 0.5, (mix_norm[l][None], md[3], md[4]))
        land(t + 3, xcur)
        s["x1"] = xcur
        s["h2"] = launch(t + 4, h)
        lw["win"], wout = group_weights(t + 2)
        lw["wout"] = wout.reshape(-1, D)
        s["proj"] = _mm_chunks(s["h2"], lw["win"])
        s["ylru"] = _lru_fwd(s["proj"], mp["cw"], mp["cb"], mp["wa"], mp["ba"], mp["wx"], mp["bx"], mp["lam"])
        s["yn"], s["ygm"] = _gmlp_fwd(s["proj"], s["ylru"], mp["gv"], seg, mp["wsp"], mp["bfull"], mp["g_lru"], mp["g_gm"])
        s["f2"], xcur, h = _mm_res(s["yn"], lw["wout"], xcur, md[5], 1.0, (ffn2_norm[l][None], md[6], md[7]))
        land(t + 4, xcur)
        s["x2"] = xcur
        s["h3"] = launch(t + 5, h)
        lw["gu2"], = group_weights(t + 3)
        s["a3"], s["gu3"] = _ffn_up(s["h3"], lw["gu2"])
        land(t + 5, s["a3"])
        s["a3"] = launch(t + 6, s["a3"])
        lw["d2"] = group_weights(t + 4)[0].reshape(-1, D)
        following = (ffn1_norm[l + 1][None], mod[l + 1][0], mod[l + 1][1]) if l + 1 < L else (final_norm[None], zero_row, zero_row)
        s["f3"], xcur, h = _mm_res(s["a3"], lw["d2"], xcur, md[8], 0.5, following)
        land(t + 6, xcur)
        saved.append(s)

    dx, dq, head_acc = _loss_head(xcur, tgt, final_norm[None], saved[-1]["md"][8], 0.5)
    loss = lax.psum(jnp.sum(head_acc[1]), ("x", "y", "c"))
    small_grads = {}
    big_grads = {n: [None] * L for n in BIG}
    dmods = [None] * L
    zero_row = jnp.zeros((1, D), F32)

    def ffn_bwd(names, l, dx, dq, x_in, h, a, gu, f, wgu, wd, gn, sc, next_gate, next_scale):
        big_grads[names[1]][l] = _mm_tn_chunks(a, dq[None], 1408, 1024)[0].reshape(N_CHIPS, -1, D)
        dgu = _ffn_bwd_act(dq, wd, gu)
        C = dgu.shape[3]
        dgu4 = dgu.reshape(N_CHIPS, S, C)
        big_grads[names[0]][l] = _mm_tn_chunks(h, dgu4, 1024, C)
        dgu4 = reduce_group(names, l, big_grads[names[0]][l], dgu4)
        dx, dq, acc = _mm_nt_norm_bwd(dgu4, wgu, x_in, dx, f, gn, sc, 0.5, next_gate, next_scale)
        return dx, move_on(dx, dq), acc

    stepped = {n: None for n in BIG}
    reducing = []

    clock = [0]
    to_step = []

    def step_reduced(after):
        while to_step:
            name, l, g = to_step.pop(0)
            stepped[name] = _adamw_layer(W[name], g, given["m_" + name], given["v_" + name], l, stepped[name], after)
            after = stepped[name][1]
        return after

    def move_on(after, thru, force=False):
        clock[0] += 1
        for grp in list(reducing):
            if not force and clock[0] - grp["since"] < MIN_AGE[grp["step"]]:
                continue
            grp["since"] = clock[0]
            landed = grp["flight"].land(after)
            n = len(grp["names"])
            if grp["step"] == "swap":
                pairs = [_add_half(g4, r1, place) for g4, r1 in zip(landed[:n], landed[n:])]
                grp.update(step="scatter", flight=_scatter_flight(grp["tag"], [h for h, _ in pairs], [own for _, own in pairs], thru))
            elif grp["step"] == "scatter":
                grp.update(step="share", flight=_share_flight(grp["tag"], [_sum4_into_half(r2, place) for r2 in landed[n:]], thru))
            else:
                to_step.extend((name, grp["l"], fin.reshape(2 * fin.shape[1], fin.shape[2])) for name, fin in zip(grp["names"], landed))
                reducing.remove(grp)
                continue
            thru = grp["flight"].thru
        return thru

    def reduce_group(names, l, after, thru):
        thru = move_on(after, thru)
        g4s = [big_grads[n][l].reshape(N_CHIPS, 2, big_grads[n][l].shape[1] // 2, big_grads[n][l].shape[2]) for n in names]
        tag = f"{l}{GROUPS.index(names)}"
        reducing.append(dict(names=names, l=l, tag=tag, step="swap", since=clock[0], flight=_swap_flight(tag, g4s, thru)))
        return reducing[-1]["flight"].thru
    for l in reversed(range(L)):
        s = saved[l]
        lw, mp, md = s["lw"], s["mp"], s["md"]
        dx, dq, acc3 = ffn_bwd(
            GROUPS[2], l, dx, dq, s["x2"], s["h3"], s["a3"], s["gu3"], s["f3"], lw["gu2"], lw["d2"], ffn2_norm[l][None], md[7], md[5], 1.0)
        big_grads["w_out"][l] = _mm_tn_chunks(s["yn"], dq[None], 1024, 1024)[0].reshape(N_CHIPS, -1, D)
        dyn = _mm_nt_chunks(dq[None], lw["wout"][None])
        dylru, duv, dwsp, dbfull, gacc = _gmlp_bwd(s["proj"], s["ylru"], s["ygm"], dyn, mp["gv"], seg, mp["wsp"], mp["wspt"], mp["bfull"], mp["g_lru"], mp["g_gm"])
        dxg, dwa, dwx, lvec = _lru_bwd(s["proj"], dylru, mp["cw"], mp["cb"], mp["wa"], mp["ba"], mp["wx"], mp["bx"], mp["lam"], mp["wat"], mp["wxt"])
        dproj = jnp.concatenate([dxg, duv], axis=0)
        big_grads["w_in"][l] = _mm_tn_chunks(s["h2"], dproj, 1024, LW)
        dproj = reduce_group(GROUPS[1], l, big_grads["w_in"][l], dproj)
        dx, dq, acc2 = _mm_nt_norm_bwd(dproj, lw["win"], s["x1"], dx, s["f2"], mix_norm[l][None], md[4], 1.0, md[2], 0.5)
        dq = move_on(dx, dq)
        if l > 0:
            ng, ns = saved[l - 1]["md"][8], 0.5
        else:
            ng, ns = zero_row, 0.0
        dx, dq, acc1 = ffn_bwd(
            GROUPS[0], l, dx, dq, s["x0"], s["h1"], s["a1"], s["gu1"], s["f1"], lw["gu1"], lw["d1"], ffn1_norm[l][None], md[1], ng, ns)

        dmods[l] = jnp.concatenate([acc1[0:2], acc1[3:4], acc2[0:2], acc2[3:4], acc3[0:2], acc3[3:4]], axis=0)
        dws = jnp.stack([dwsp[:, :, :CHUNK], dwsp[:, :, CHUNK:]], axis=1).reshape(HEADS, CHUNK, CHUNK) * tril
        lg = {"ffn1_norm": acc1[2], "mix_norm": acc2[2], "ffn2_norm": acc3[2],
              "conv_w": lvec[4:8], "conv_b": lvec[3], "gate_a_w": _unpair_blocks(dwa), "gate_a_b": lvec[0].reshape(HEADS, hd),
              "gate_x_w": _unpair_blocks(dwx), "gate_x_b": lvec[1].reshape(HEADS, hd), "lru_lambda": lvec[2], "v_norm": gacc[2],
              "spatial_w": dws, "spatial_b": dbfull.reshape(CHUNK, HEADS, hd).sum(-1).T, "lru_out_norm": gacc[0], "gmlp_out_norm": gacc[1]}
        for n, g in lg.items():
            small_grads.setdefault(n, [None] * L)[l] = g

    grad_x = dx.reshape(x.shape)

    per_layer = [n for n in SMALL if n not in ("b_ada", "final_norm")]
    part = [jnp.stack(small_grads[n]) for n in per_layer] + [head_acc[0], jnp.stack(dmods)]
    packer = _Packer([p.shape for p in part])
    packed = packer.pack(part)
    exchange = _exchange_flight(lax.dynamic_update_index_in_dim(jnp.zeros((N_DEV,) + packed.shape, F32), packed, dev, 0), dq)
    dq = exchange.thru
    done = step_reduced(dq)
    while reducing:
        dq = move_on(done, dq, force=True)
        done = step_reduced(dq)
    gathered, = exchange.land(done)
    summed = packer.unpack(_sum_leading(gathered))
    grads = dict(zip(per_layer + ["final_norm"], summed[:-1]))
    grads["b_ada"] = summed[-1].reshape(L, N_MOD * D)
    off = sum(packer.sizes[:-1])
    dmod_rows = gathered.reshape(N_DEV, -1)[:, off:off + L * N_MOD * D].reshape(N_DEV, L, N_MOD * D)
    dmod_shard = lax.dynamic_slice_in_dim(dmod_rows, chip * n_ada, n_ada, axis=2).transpose(1, 0, 2)
    grads["w_ada"] = _ada_grad(c_all.T, dmod_shard)
    grads["conv_w"] = lax.dynamic_slice_in_dim(grads["conv_w"], chip * cws, cws, axis=2)

    delta, new_m, new_v = {}, {}, {}
    for n in BIG:
        grads[n], delta[n], new_m[n], new_v[n] = stepped[n]
    shp = w_ada.shape
    d_, m_, v_ = _adamw(*[a.reshape(-1, shp[-1]) for a in (w_ada, grads["w_ada"], m_w_ada, v_w_ada)])
    delta["w_ada"], new_m["w_ada"], new_v["w_ada"] = d_.reshape(shp), m_.reshape(shp), v_.reshape(shp)
    spk = _Packer([W[n].shape for n in SMALL])
    d_, m_, v_ = _adamw(spk.pack([W[n] for n in SMALL]), spk.pack([grads[n] for n in SMALL]),
                        spk.pack([given["m_" + n] for n in SMALL]), spk.pack([given["v_" + n] for n in SMALL]))
    for n, a, b, e in zip(SMALL, spk.unpack(d_), spk.unpack(m_), spk.unpack(v_)):
        delta[n], new_m[n], new_v[n] = a, b, e
    grads = {n: grads[n].reshape(W[n].shape) for n in WEIGHTS}
    return (loss, grad_x, *[grads[n] for n in WEIGHTS], *[delta[n] for n in WEIGHTS], *[new_m[n] for n in WEIGHTS], *[new_v[n] for n in WEIGHTS])
```

```python
import math

import jax
import jax.numpy as jnp
from jax import lax
from jax.experimental import pallas as pl
from jax.experimental.pallas import tpu as pltpu

F32 = jnp.float32
MXU_DTYPE = jnp.bfloat16
ACT_DTYPE = jnp.bfloat16
XFER_DTYPE = jnp.bfloat16
EPS = 1e-6
RG_LRU_C = 8.0
N_MOD = 9
CONV_WIDTH = 4
HEADS = 8
CHUNK = 128
LANES = 128
SUBLANES = 8
N_CHIPS = 4
N_DEV = 8
ADAM_LR, ADAM_B1, ADAM_B2, ADAM_EPS, ADAM_WD, ADAM_STEP = 0.001, 0.9, 0.999, 1e-08, 0.01, 10
VMEM_LIMIT_BYTES = 60 * 1024 * 1024
ROW_TILE_BYTES = 2 << 20
GELU_C = math.sqrt(2.0 / math.pi)
GELU_A = 0.044715

ANY = pl.BlockSpec(memory_space=pl.ANY)
MESH = pl.DeviceIdType.MESH
SDS = jax.ShapeDtypeStruct


def _params(*sem):
    return pltpu.CompilerParams(dimension_semantics=sem, vmem_limit_bytes=VMEM_LIMIT_BYTES)


def _dot(a, b):
    return jnp.dot(a.astype(MXU_DTYPE), b.astype(MXU_DTYPE), preferred_element_type=F32)


def _dot_nt(a, b):
    return lax.dot_general(a.astype(MXU_DTYPE), b.astype(MXU_DTYPE), (((1,), (1,)), ((), ())), preferred_element_type=F32)


def _dot_tn(a, b):
    return lax.dot_general(a.astype(MXU_DTYPE), b.astype(MXU_DTYPE), (((0,), (0,)), ((), ())), preferred_element_type=F32)


def _gelu(x):
    return x * (0.5 * (1.0 + jnp.tanh(GELU_C * (x + GELU_A * (x * x * x)))))


def _gelu_grad(x):
    t = jnp.tanh(GELU_C * (x + GELU_A * (x * x * x)))
    return 0.5 * (1.0 + t) + 0.5 * x * (1.0 - t * t) * (GELU_C * (1.0 + 3.0 * GELU_A * x * x))


def _sigmoid(x):
    return jax.nn.sigmoid(x)


def _rsqrt_ms(x):
    return lax.rsqrt(jnp.mean(x * x, axis=-1, keepdims=True) + EPS)


def _rowsum(x):
    return jnp.sum(x, axis=0, keepdims=True)


def _tile(n, want):
    t = min(n, want)
    assert n % t == 0, (n, want)
    return t


def _row_tile(rows, row_bytes):
    step = 2 * SUBLANES
    cap = max(step, ROW_TILE_BYTES // row_bytes)
    best = None
    for t in range(step, min(rows, cap) + 1, step):
        if rows % t == 0:
            best = t
    assert best is not None, (rows, row_bytes)
    return best


def _modnorm(x, gn, sh, sc):
    S, D = x.shape
    tm = _tile(S, 1024)

    def body(x_ref, gn_ref, sh_ref, sc_ref, h_ref):
        xv = x_ref[...]
        h = (xv * _rsqrt_ms(xv) * gn_ref[...]) * (1.0 + sc_ref[...]) + sh_ref[...]
        h_ref[...] = h.astype(ACT_DTYPE)

    row = pl.BlockSpec((1, D), lambda i: (0, 0))
    return pl.pallas_call(
        body, name="modnorm", grid=(S // tm,),
        in_specs=[pl.BlockSpec((tm, D), lambda i: (i, 0)), row, row, row],
        out_specs=pl.BlockSpec((tm, D), lambda i: (i, 0)),
        out_shape=SDS((S, D), ACT_DTYPE), compiler_params=_params("parallel"),
    )(x, gn, sh, sc)


def _mm_nt_norm_bwd(ac, wc, x, dxo, f, gn, sc, res_scale, next_gate, next_scale):
    P, S, K = ac.shape
    D = x.shape[1]
    tm = _tile(S, 512)

    def body(a_ref, w_ref, x_ref, dxo_ref, f_ref, gn_ref, sc_ref, ng_ref, dx_ref, dq_ref, acc_ref):
        @pl.when(pl.program_id(0) == 0)
        def _():
            acc_ref[...] = jnp.zeros_like(acc_ref)

        dh = _dot_nt(a_ref[0], w_ref[0])
        for p in range(1, P):
            dh += _dot_nt(a_ref[p], w_ref[p])
        xv, dxo = x_ref[...], dxo_ref[...]
        r = _rsqrt_ms(xv)
        xhat = xv * r
        gn = gn_ref[...]
        dn = dh * (1.0 + sc_ref[...])
        dxh = dn * gn
        dx = dxo + r * (dxh - xhat * jnp.mean(dxh * xhat, axis=-1, keepdims=True))
        dx_ref[...] = dx
        dq_ref[...] = ((next_scale * ng_ref[...]) * dx).astype(ACT_DTYPE)
        acc_ref[0:1, :] += _rowsum(dh)
        acc_ref[1:2, :] += _rowsum(dh * (xhat * gn))
        acc_ref[2:3, :] += _rowsum(dn * xhat)
        acc_ref[3:4, :] += _rowsum((res_scale * f_ref[...]) * dxo)

    tile = pl.BlockSpec((tm, D), lambda i: (i, 0))
    row = pl.BlockSpec((1, D), lambda i: (0, 0))
    return pl.pallas_call(
        body, name=f"mm_nt_norm_bwd_k{K}", grid=(S // tm,),
        in_specs=[pl.BlockSpec((P, tm, K), lambda i: (0, i, 0)),
                  pl.BlockSpec((P, D, K), lambda i: (0, 0, 0), pipeline_mode=pl.Buffered(1)),
                  tile, tile, tile, row, row, row],
        out_specs=[tile, tile, pl.BlockSpec((SUBLANES, D), lambda i: (0, 0))],
        out_shape=[SDS((S, D), F32), SDS((S, D), ACT_DTYPE), SDS((SUBLANES, D), F32)],
        compiler_params=_params("arbitrary"),
    )(ac, wc, x, dxo, f, gn, sc, next_gate)


def _loss_head(x, target, gn, next_gate, next_scale):
    S, D = x.shape
    tm = _tile(S, 512)

    def body(x_ref, t_ref, gn_ref, ng_ref, dx_ref, dq_ref, acc_ref):
        @pl.when(pl.program_id(0) == 0)
        def _():
            acc_ref[...] = jnp.zeros_like(acc_ref)

        xv = x_ref[...]
        r = _rsqrt_ms(xv)
        xhat = xv * r
        gn = gn_ref[...]
        err = xhat * gn - t_ref[...]
        dy = err * (1.0 / D)
        dxh = dy * gn
        dx = r * (dxh - xhat * jnp.mean(dxh * xhat, axis=-1, keepdims=True))
        dx_ref[...] = dx
        dq_ref[...] = ((next_scale * ng_ref[...]) * dx).astype(ACT_DTYPE)
        acc_ref[0:1, :] += _rowsum(dy * xhat)
        acc_ref[1:2, :] += _rowsum(err * err) * (0.5 / D)

    tile = pl.BlockSpec((tm, D), lambda i: (i, 0))
    row = pl.BlockSpec((1, D), lambda i: (0, 0))
    return pl.pallas_call(
        body, name="loss_head", grid=(S // tm,),
        in_specs=[tile, tile, row, row],
        out_specs=[tile, tile, pl.BlockSpec((SUBLANES, D), lambda i: (0, 0))],
        out_shape=[SDS((S, D), F32), SDS((S, D), ACT_DTYPE), SDS((SUBLANES, D), F32)],
        compiler_params=_params("arbitrary"),
    )(x, target, gn, next_gate)


def _ffn_up(h, wgu):
    S, D = h.shape
    C = wgu.shape[2]
    tm = _tile(S, 512)

    def body(h_ref, wg_ref, wu_ref, a_ref, gu_ref):
        hv = h_ref[...]
        g = _dot(hv, wg_ref[...])
        u = _dot(hv, wu_ref[...])
        a_ref[...] = (g * _sigmoid(g) * u).astype(ACT_DTYPE)
        gu_ref[0] = g.astype(ACT_DTYPE)
        gu_ref[1] = u.astype(ACT_DTYPE)

    return pl.pallas_call(
        body, name="ffn_up", grid=(2, S // tm),
        in_specs=[
            pl.BlockSpec((tm, D), lambda j, i: (i, 0)),
            pl.BlockSpec((None, D, C), lambda j, i: (j, 0, 0)),
            pl.BlockSpec((None, D, C), lambda j, i: (2 + j, 0, 0)),
        ],
        out_specs=[
            pl.BlockSpec((tm, C), lambda j, i: (i, j)),
            pl.BlockSpec((2, None, tm, C), lambda j, i: (0, j, i, 0)),
        ],
        out_shape=[SDS((S, 2 * C), ACT_DTYPE), SDS((2, 2, S, C), ACT_DTYPE)],
        compiler_params=_params("parallel", "parallel"),
    )(h, wgu, wgu)


def _ffn_bwd_act(dq, wd, gu):
    S, D = dq.shape
    C = gu.shape[3]
    tm = _tile(S, 512)

    def body(dq_ref, wd_ref, gu_ref, dgu_ref):
        da = _dot_nt(dq_ref[...], wd_ref[...])
        g = gu_ref[0].astype(F32)
        u = gu_ref[1].astype(F32)
        s = _sigmoid(g)
        dgu_ref[0] = (da * u * (s * (1.0 + g * (1.0 - s)))).astype(ACT_DTYPE)
        dgu_ref[1] = (da * (g * s)).astype(ACT_DTYPE)

    gu_spec = pl.BlockSpec((2, None, tm, C), lambda j, i: (0, j, i, 0))
    return pl.pallas_call(
        body, name="ffn_bwd_act", grid=(2, S // tm),
        in_specs=[pl.BlockSpec((tm, D), lambda j, i: (i, 0)), pl.BlockSpec((C, D), lambda j, i: (j, 0)), gu_spec],
        out_specs=gu_spec,
        out_shape=SDS(gu.shape, ACT_DTYPE),
        compiler_params=_params("parallel", "parallel"),
    )(dq, wd, gu)


def _mm_res(a, w, x, gate, scale, following):
    S, K = a.shape
    D = w.shape[1]
    tm = _tile(S, 512)

    def body(a_ref, w_ref, x_ref, g_ref, gn_ref, sh_ref, sc_ref, f_ref, xo_ref, h_ref):
        f = _dot(a_ref[...], w_ref[...])
        f_ref[...] = f
        xo = x_ref[...] + (scale * g_ref[...]) * f
        xo_ref[...] = xo
        h_ref[...] = ((xo * _rsqrt_ms(xo) * gn_ref[...]) * (1.0 + sc_ref[...]) + sh_ref[...]).astype(ACT_DTYPE)

    tile = pl.BlockSpec((tm, D), lambda i: (i, 0))
    row = pl.BlockSpec((1, D), lambda i: (0, 0))
    return pl.pallas_call(
        body, name=f"mm_res_k{K}", grid=(S // tm,),
        in_specs=[pl.BlockSpec((tm, K), lambda i: (i, 0)), pl.BlockSpec((K, D), lambda i: (0, 0)), tile, row, row, row, row],
        out_specs=[tile, tile, tile],
        out_shape=[SDS((S, D), F32), SDS((S, D), F32), SDS((S, D), ACT_DTYPE)],
        compiler_params=_params("parallel"),
    )(a, w, x, gate, *following)


def _mm_chunks(h, wc):
    S, K = h.shape
    P, _, N = wc.shape
    tm = _tile(S, 512)

    def body(h_ref, w_ref, o_ref):
        hv = h_ref[...]
        for p in range(P):
            o_ref[:, p * N:(p + 1) * N] = _dot(hv, w_ref[p])

    return pl.pallas_call(
        body, name="mm_chunks", grid=(S // tm,),
        in_specs=[pl.BlockSpec((tm, K), lambda i: (i, 0)), pl.BlockSpec((P, K, N), lambda i: (0, 0, 0))],
        out_specs=pl.BlockSpec((tm, P * N), lambda i: (i, 0)),
        out_shape=SDS((S, P * N), F32),
        compiler_params=_params("parallel"),
    )(h, wc)


def _mm_nt_chunks(ac, wc):
    P, S, K = ac.shape
    N = wc.shape[1]
    tm, tn = _tile(S, 512), _tile(N, 1024)

    def body(a_ref, w_ref, o_ref):
        acc = _dot_nt(a_ref[0], w_ref[0])
        for p in range(1, P):
            acc += _dot_nt(a_ref[p], w_ref[p])
        o_ref[...] = acc

    return pl.pallas_call(
        body, name=f"mm_nt_p{P}k{K}", grid=(S // tm, N // tn),
        in_specs=[pl.BlockSpec((P, tm, K), lambda i, j: (0, i, 0)), pl.BlockSpec((P, tn, K), lambda i, j: (0, j, 0))],
        out_specs=pl.BlockSpec((tm, tn), lambda i, j: (i, j)),
        out_shape=SDS((S, N), F32),
        compiler_params=_params("parallel", "parallel"),
    )(ac, wc)


def _mm_tn_chunks(a, bc, tile_m, tile_n):
    S, M = a.shape
    P, _, N = bc.shape
    ts, tm, tn = _tile(S, 2048), _tile(M, tile_m), _tile(N, tile_n)

    def body(a_ref, b_ref, o_ref):
        @pl.when(pl.program_id(3) == 0)
        def _():
            o_ref[...] = jnp.zeros_like(o_ref)

        o_ref[...] += _dot_tn(a_ref[...], b_ref[...])

    return pl.pallas_call(
        body, name=f"mm_tn_m{M}n{N}", grid=(P, M // tm, N // tn, S // ts),
        in_specs=[pl.BlockSpec((ts, tm), lambda p, m, n, k: (k, m)), pl.BlockSpec((None, ts, tn), lambda p, m, n, k: (p, k, n))],
        out_specs=pl.BlockSpec((None, tm, tn), lambda p, m, n, k: (p, m, n)),
        out_shape=SDS((P, M, N), F32),
        compiler_params=_params("parallel", "parallel", "parallel", "arbitrary"),
    )(a, bc)


def _shift_down(x, s, row, fill):
    return jnp.where(row >= s, pltpu.roll(x, s, 0), fill)


def _shift_up(x, s, row, fill):
    n = x.shape[0]
    return jnp.where(row < n - s, pltpu.roll(x, n - s, 0), fill)


def _scan(a, b, row, scratch, up):
    scr_a, scr_b, scr_c = scratch
    n = a.shape[0]
    g = n // SUBLANES
    in_group = row & (SUBLANES - 1)

    def steps(a, b, pos, size):
        s = 1
        while s < size:
            m = (pos + s < size) if up else (pos >= s)
            b = jnp.where(m, a, 0.0) * pltpu.roll(b, a.shape[0] - s if up else s, 0) + b
            a = jnp.where(m, a * pltpu.roll(a, a.shape[0] - s if up else s, 0), a)
            s *= 2
        return a, b

    a, b = steps(a, b, in_group, SUBLANES)
    scr_a[...] = a
    scr_b[...] = b
    edge = 0 if up else SUBLANES - 1
    at = scr_a[pl.ds(edge, g, stride=SUBLANES), :]
    bt = scr_b[pl.ds(edge, g, stride=SUBLANES), :]
    group = lax.broadcasted_iota(jnp.int32, at.shape, 0)
    _, state = steps(at, bt, group, g)
    carry = jnp.where((group + 1 < g) if up else (group >= 1), pltpu.roll(state, g - 1 if up else 1, 0), 0.0)
    for k in range(SUBLANES):
        scr_c[pl.ds(k, g, stride=SUBLANES), :] = carry
    return b + a * scr_c[...]


def _conv(xl, cw_ref, cb_ref, row):
    y = cb_ref[...] + _shift_down(xl, 3, row, 0.0) * cw_ref[0:1, :]
    y = y + _shift_down(xl, 2, row, 0.0) * cw_ref[1:2, :]
    y = y + _shift_down(xl, 1, row, 0.0) * cw_ref[2:3, :]
    return y + xl * cw_ref[3:4, :]


def _lru_gates(xc, wa_ref, ba_ref, wx_ref, bx_ref, lam_ref):
    ra = _sigmoid(_dot(xc, wa_ref[...]) + ba_ref[...])
    ri = _sigmoid(_dot(xc, wx_ref[...]) + bx_ref[...])
    ls = jax.nn.log_sigmoid(lam_ref[...])
    a = jnp.exp((RG_LRU_C * ra) * ls)
    mult = jnp.sqrt(1.0 - a * a)
    return ra, ri, ls, a, mult


def _lru_specs(S):
    col = lambda off: pl.BlockSpec((S, LANES), lambda j: (0, off + j))
    vec = pl.BlockSpec((1, LANES), lambda j: (0, j))
    blk = pl.BlockSpec((None, LANES, LANES), lambda j: (j, 0, 0))
    cw = pl.BlockSpec((CONV_WIDTH, LANES), lambda j: (0, j))
    return col, vec, blk, cw


def _lru_fwd(proj, cw, cb, wa, ba, wx, bx, lam):
    S = proj.shape[0]
    W = cb.shape[1]
    nb = W // LANES

    def body(xl_ref, gl_ref, cw_ref, cb_ref, wa_ref, ba_ref, wx_ref, bx_ref, lam_ref, y_ref, *scratch):
        row = lax.broadcasted_iota(jnp.int32, (S, LANES), 0)
        xc = _conv(xl_ref[...], cw_ref, cb_ref, row)
        _, ri, _, a, mult = _lru_gates(xc, wa_ref, ba_ref, wx_ref, bx_ref, lam_ref)
        h = _scan(a, mult * (ri * xc), row, scratch, up=False)
        y_ref[...] = h * _gelu(gl_ref[...])

    col, vec, blk, cws = _lru_specs(S)
    return pl.pallas_call(
        body, name="lru_fwd", grid=(nb,),
        in_specs=[col(0), col(nb), cws, vec, blk, vec, blk, vec, vec],
        out_specs=pl.BlockSpec((S, LANES), lambda j: (0, j)),
        out_shape=SDS((S, W), F32), scratch_shapes=[pltpu.VMEM((S, LANES), F32)] * 3, compiler_params=_params("parallel"),
    )(proj, proj, cw, cb, wa, ba, wx, bx, lam)


def _lru_bwd(proj, dy, cw, cb, wa, ba, wx, bx, lam, wat, wxt):
    S = proj.shape[0]
    W = cb.shape[1]
    nb = W // LANES

    def body(xl_ref, gl_ref, dy_ref, cw_ref, cb_ref, wa_ref, ba_ref, wx_ref, bx_ref, lam_ref, wat_ref, wxt_ref,
             dp_ref, dwa_ref, dwx_ref, vec_ref, *scratch):
        row = lax.broadcasted_iota(jnp.int32, (S, LANES), 0)
        xl = xl_ref[...]
        xc = _conv(xl, cw_ref, cb_ref, row)
        ra, ri, ls, a, mult = _lru_gates(xc, wa_ref, ba_ref, wx_ref, bx_ref, lam_ref)
        h = _scan(a, mult * (ri * xc), row, scratch, up=False)
        gl = gl_ref[...]
        dyv = dy_ref[...]
        dp_ref[1] = (dyv * h * _gelu_grad(gl)).astype(ACT_DTYPE)
        adj = _scan(_shift_up(a, 1, row, 0.0), dyv * _gelu(gl), row, scratch, up=True)
        da = adj * _shift_down(h, 1, row, 0.0)
        dmult = adj * (ri * xc)
        dlog_a = da * a - dmult * (a * a) / mult
        dra = dlog_a * (RG_LRU_C * ls)
        dpa = dra * ra * (1.0 - ra)
        dpi = (adj * mult * xc) * ri * (1.0 - ri)
        dxc = adj * mult * ri + _dot(dpa, wat_ref[...]) + _dot(dpi, wxt_ref[...])
        dwa_ref[...] = _dot_tn(xc, dpa)
        dwx_ref[...] = _dot_tn(xc, dpi)
        dxl = dxc * cw_ref[3:4, :]
        dxl = dxl + _shift_up(dxc, 1, row, 0.0) * cw_ref[2:3, :]
        dxl = dxl + _shift_up(dxc, 2, row, 0.0) * cw_ref[1:2, :]
        dxl = dxl + _shift_up(dxc, 3, row, 0.0) * cw_ref[0:1, :]
        dp_ref[0] = dxl.astype(ACT_DTYPE)
        vec_ref[...] = jnp.zeros_like(vec_ref)
        vec_ref[0:1, :] = _rowsum(dpa)
        vec_ref[1:2, :] = _rowsum(dpi)
        vec_ref[2:3, :] = _rowsum(dlog_a * (RG_LRU_C * ra)) * _sigmoid(-lam_ref[...])
        vec_ref[3:4, :] = _rowsum(dxc)
        vec_ref[4:5, :] = _rowsum(dxc * _shift_down(xl, 3, row, 0.0))
        vec_ref[5:6, :] = _rowsum(dxc * _shift_down(xl, 2, row, 0.0))
        vec_ref[6:7, :] = _rowsum(dxc * _shift_down(xl, 1, row, 0.0))
        vec_ref[7:8, :] = _rowsum(dxc * xl)

    col, vec, blk, cws = _lru_specs(S)
    return pl.pallas_call(
        body, name="lru_bwd", grid=(nb,),
        in_specs=[col(0), col(nb), col(0), cws, vec, blk, vec, blk, vec, vec, blk, blk],
        out_specs=[pl.BlockSpec((2, S, LANES), lambda j: (0, 0, j)), blk, blk, pl.BlockSpec((2 * SUBLANES, LANES), lambda j: (0, j))],
        out_shape=[SDS((2, S, W), ACT_DTYPE), SDS((nb, LANES, LANES), F32), SDS((nb, LANES, LANES), F32), SDS((2 * SUBLANES, W), F32)],
        scratch_shapes=[pltpu.VMEM((S, LANES), F32)] * 3, compiler_params=_params("parallel"),
    )(proj, proj, dy, cw, cb, wa, ba, wx, bx, lam, wat, wxt)


def _seg_mean(x, seg_ref, width):
    hi = x.astype(jnp.bfloat16)
    lo = (x - hi.astype(F32)).astype(jnp.bfloat16)
    ones = seg_ref[...]
    s = jnp.dot(hi, ones, preferred_element_type=F32) + jnp.dot(lo, ones, preferred_element_type=F32)
    return s * (1.0 / width)


def _gmlp_core(u_ref, v_ref, gv_ref, seg_ref, ws_ref, bfull_ref, z_scr, hd):
    tm, W = u_ref.shape
    lane = lax.broadcasted_iota(jnp.int32, (CHUNK, LANES), 1)
    ug = _gelu(u_ref[...])
    vg = _gelu(v_ref[...])
    cen = vg - _seg_mean(vg, seg_ref, hd)
    rstd = lax.rsqrt(_seg_mean(cen * cen, seg_ref, hd) + EPS)
    vhat = cen * rstd
    vh = vhat * gv_ref[...]
    vcats = {}
    for ci in range(tm // CHUNK):
        for p in range(W // LANES):
            blk = vh[ci * CHUNK:(ci + 1) * CHUNK, p * LANES:(p + 1) * LANES]
            vcat = jnp.concatenate([jnp.where(lane < hd, blk, 0.0), jnp.where(lane >= hd, blk, 0.0)], axis=0).astype(MXU_DTYPE)
            vcats[ci, p] = vcat
            z_scr[ci * CHUNK:(ci + 1) * CHUNK, p * LANES:(p + 1) * LANES] = (
                jnp.dot(ws_ref[p], vcat, preferred_element_type=F32) + bfull_ref[:, p * LANES:(p + 1) * LANES])
    return ug, vhat, rstd, vcats


def _gmlp_specs(tm, W, nb):
    rows = lambda off: pl.BlockSpec((tm, W), lambda i: (i, off))
    vec = pl.BlockSpec((1, W), lambda i: (0, 0))
    seg = pl.BlockSpec((W, W), lambda i: (0, 0))
    wsp = pl.BlockSpec((nb, CHUNK, 2 * CHUNK), lambda i: (0, 0, 0))
    bfull = pl.BlockSpec((CHUNK, W), lambda i: (0, 0))
    return rows, vec, seg, wsp, bfull


def _gmlp_fwd(proj, ylru, gv, seg, wsp, bfull, g_lru, g_gm):
    S, W = ylru.shape
    nb = W // LANES
    hd = W // HEADS
    tm = _tile(S, 512)

    def body(u_ref, v_ref, yl_ref, gv_ref, seg_ref, ws_ref, bfull_ref, gl_ref, gg_ref, yn_ref, ygm_ref, z_scr):
        ug, _, _, _ = _gmlp_core(u_ref, v_ref, gv_ref, seg_ref, ws_ref, bfull_ref, z_scr, hd)
        ygm = ug * z_scr[...]
        ygm_ref[...] = ygm
        yl = yl_ref[...]
        yn_ref[:, 0:W] = (yl * _rsqrt_ms(yl) * gl_ref[...]).astype(ACT_DTYPE)
        yn_ref[:, W:2 * W] = (ygm * _rsqrt_ms(ygm) * gg_ref[...]).astype(ACT_DTYPE)

    rows, vec, segs, wsps, bfulls = _gmlp_specs(tm, W, nb)
    return pl.pallas_call(
        body, name="gmlp_fwd", grid=(S // tm,),
        in_specs=[rows(2), rows(3), rows(0), vec, segs, wsps, bfulls, vec, vec],
        out_specs=[pl.BlockSpec((tm, 2 * W), lambda i: (i, 0)), rows(0)],
        out_shape=[SDS((S, 2 * W), ACT_DTYPE), SDS((S, W), F32)],
        scratch_shapes=[pltpu.VMEM((tm, W), F32)],
        compiler_params=_params("parallel"),
    )(proj, proj, ylru, gv, seg, wsp, bfull, g_lru, g_gm)


def _rms_bwd(y, g, dyn):
    r = _rsqrt_ms(y)
    yhat = y * r
    dyh = dyn * g
    return r * (dyh - yhat * jnp.mean(dyh * yhat, axis=-1, keepdims=True)), _rowsum(dyn * yhat)


def _gmlp_bwd(proj, ylru, ygm, dyn, gv, seg, wsp, wspt, bfull, g_lru, g_gm):
    S, W = ylru.shape
    nb = W // LANES
    hd = W // HEADS
    tm = _tile(S, 256)

    def body(u_ref, v_ref, yl_ref, ygm_ref, dl_ref, dg_ref, gv_ref, seg_ref, ws_ref, wst_ref, bfull_ref, gl_ref, gg_ref,
             dyl_ref, duv_ref, dws_ref, dbf_ref, acc_ref, z_scr, dvh_scr):
        @pl.when(pl.program_id(0) == 0)
        def _():
            dws_ref[...] = jnp.zeros_like(dws_ref)
            dbf_ref[...] = jnp.zeros_like(dbf_ref)
            acc_ref[...] = jnp.zeros_like(acc_ref)

        dyl, dgl = _rms_bwd(yl_ref[...], gl_ref[...], dl_ref[...])
        dyl_ref[...] = dyl
        dygm, dgg = _rms_bwd(ygm_ref[...], gg_ref[...], dg_ref[...])
        ug, vhat, rstd, vcats = _gmlp_core(u_ref, v_ref, gv_ref, seg_ref, ws_ref, bfull_ref, z_scr, hd)
        duv_ref[0] = (dygm * z_scr[...] * _gelu_grad(u_ref[...])).astype(ACT_DTYPE)
        dz = dygm * ug
        lane = lax.broadcasted_iota(jnp.int32, (CHUNK, LANES), 1)
        dbf = dz[0:CHUNK, :]
        for ci in range(1, tm // CHUNK):
            dbf += dz[ci * CHUNK:(ci + 1) * CHUNK, :]
        dbf_ref[...] += dbf
        for ci in range(tm // CHUNK):
            for p in range(nb):
                dzb = dz[ci * CHUNK:(ci + 1) * CHUNK, p * LANES:(p + 1) * LANES].astype(MXU_DTYPE)
                dws_ref[p] += _dot_nt(dzb, vcats[ci, p])
                dvc = jnp.dot(wst_ref[p], dzb, preferred_element_type=F32)
                dvh_scr[ci * CHUNK:(ci + 1) * CHUNK, p * LANES:(p + 1) * LANES] = jnp.where(lane < hd, dvc[0:CHUNK], dvc[CHUNK:2 * CHUNK])
        dvh = dvh_scr[...]
        dvn = dvh * gv_ref[...]
        dvg = rstd * (dvn - _seg_mean(dvn, seg_ref, hd) - vhat * _seg_mean(dvn * vhat, seg_ref, hd))
        duv_ref[1] = (dvg * _gelu_grad(v_ref[...])).astype(ACT_DTYPE)
        acc_ref[0:1, :] += dgl
        acc_ref[1:2, :] += dgg
        acc_ref[2:3, :] += _rowsum(dvh * vhat)

    rows, vec, segs, wsps, bfulls = _gmlp_specs(tm, W, nb)
    wspt_spec = pl.BlockSpec((nb, 2 * CHUNK, CHUNK), lambda i: (0, 0, 0))
    return pl.pallas_call(
        body, name="gmlp_bwd", grid=(S // tm,),
        in_specs=[rows(2), rows(3), rows(0), rows(0), rows(0), rows(1), vec, segs, wsps, wspt_spec, bfulls, vec, vec],
        out_specs=[rows(0), pl.BlockSpec((2, tm, W), lambda i: (0, i, 0)), wsps, bfulls, pl.BlockSpec((SUBLANES, W), lambda i: (0, 0))],
        out_shape=[SDS((S, W), F32), SDS((2, S, W), ACT_DTYPE), SDS((nb, CHUNK, 2 * CHUNK), F32), SDS((CHUNK, W), F32), SDS((SUBLANES, W), F32)],
        scratch_shapes=[pltpu.VMEM((tm, W), F32), pltpu.VMEM((tm, W), F32)],
        compiler_params=_params("arbitrary"),
    )(proj, proj, ylru, ygm, dyn, dyn, gv, seg, wsp, wspt, bfull, g_lru, g_gm)


def _ada_fwd(c_all, w_ada, b_shard):
    L, D, N = w_ada.shape
    R = c_all.shape[0]
    tn = N // 2

    def body(c_ref, w_ref, b_ref, o_ref):
        cv = c_ref[...]
        o_ref[...] = _dot(cv * _sigmoid(cv), w_ref[...]) + b_ref[...]

    return pl.pallas_call(
        body, name="ada_fwd", grid=(L, N // tn),
        in_specs=[pl.BlockSpec((R, D), lambda l, j: (0, 0)), pl.BlockSpec((None, D, tn), lambda l, j: (l, 0, j)),
                  pl.BlockSpec((None, 1, tn), lambda l, j: (l, 0, j))],
        out_specs=pl.BlockSpec((None, R, tn), lambda l, j: (l, 0, j)),
        out_shape=SDS((L, R, N), F32), compiler_params=_params("parallel", "parallel"),
    )(c_all, w_ada, b_shard)


def _ada_grad(c_all_t, dmod):
    D, B = c_all_t.shape
    L, _, N = dmod.shape
    tn = N // 2

    def body(c_ref, d_ref, o_ref):
        cv = c_ref[...]
        sc = cv * _sigmoid(cv)
        acc = sc[:, 0:1] * d_ref[0:1, :]
        for b in range(1, B):
            acc += sc[:, b:b + 1] * d_ref[b:b + 1, :]
        o_ref[...] = acc

    return pl.pallas_call(
        body, name="ada_grad", grid=(L, N // tn),
        in_specs=[pl.BlockSpec((D, B), lambda l, j: (0, 0)), pl.BlockSpec((None, B, tn), lambda l, j: (l, 0, j))],
        out_specs=pl.BlockSpec((None, D, tn), lambda l, j: (l, 0, j)),
        out_shape=SDS((L, D, N), F32), compiler_params=_params("parallel", "parallel"),
    )(c_all_t, dmod)


def _adamw(w, g, m, v):
    R, C = w.shape
    tr = _row_tile(R, C * 4)

    def body(w_ref, g_ref, m_ref, v_ref, d_ref, mo_ref, vo_ref):
        d_ref[...], mo_ref[...], vo_ref[...] = _adam_math(w_ref[...], g_ref[...], m_ref[...], v_ref[...])

    tile = pl.BlockSpec((tr, C), lambda i: (i, 0))
    return pl.pallas_call(
        body, name=f"adamw_r{R}c{C}", grid=(R // tr,), in_specs=[tile] * 4, out_specs=[tile] * 3,
        out_shape=[SDS((R, C), F32)] * 3, compiler_params=_params("parallel"),
    )(w, g, m, v)


def _adam_math(w, g, m, v):
    mn = ADAM_B1 * m + (1.0 - ADAM_B1) * g
    vn = ADAM_B2 * v + (1.0 - ADAM_B2) * (g * g)
    m_hat = mn / (1.0 - ADAM_B1 ** ADAM_STEP)
    v_hat = vn / (1.0 - ADAM_B2 ** ADAM_STEP)
    return -ADAM_LR * (m_hat / (jnp.sqrt(v_hat) + ADAM_EPS) + ADAM_WD * w), mn, vn


def _adamw_layer(w, g, m, v, l, prev, after):
    L, R, C = w.shape
    tr = _row_tile(R, C * 4)
    prev = (after,) + (() if prev is None else tuple(prev))

    def body(w_ref, g_ref, m_ref, v_ref, *rest):
        go_ref, d_ref, mo_ref, vo_ref = rest[len(prev):]
        gv = g_ref[...]
        go_ref[...] = gv
        d_ref[...], mo_ref[...], vo_ref[...] = _adam_math(w_ref[...], gv, m_ref[...], v_ref[...])

    lay = pl.BlockSpec((None, tr, C), lambda i: (l, i, 0))
    return pl.pallas_call(
        body, name=f"adamw_layer_r{R}c{C}", grid=(R // tr,),
        in_specs=[lay, pl.BlockSpec((tr, C), lambda i: (i, 0)), lay, lay] + [ANY] * len(prev), out_specs=[lay] * 4,
        out_shape=[SDS((L, R, C), F32)] * 4, input_output_aliases={5 + k: k for k in range(len(prev) - 1)},
        compiler_params=_params("parallel"),
    )(w, g, m, v, *prev)


def _sum_leading(a):
    P, R, C = a.shape
    tr = _row_tile(R, P * C * 4)

    def body(a_ref, o_ref):
        acc = a_ref[0]
        for p in range(1, P):
            acc = acc + a_ref[p]
        o_ref[...] = acc

    return pl.pallas_call(
        body, name=f"sum{P}_r{R}c{C}", grid=(R // tr,),
        in_specs=[pl.BlockSpec((P, tr, C), lambda i: (0, i, 0))],
        out_specs=pl.BlockSpec((tr, C), lambda i: (i, 0)),
        out_shape=SDS((R, C), F32), compiler_params=_params("parallel"),
    )(a)


def _add_half(g4, r1, place):
    _, _, R, C = g4.shape
    tr = _row_tile(R, C * 4)

    def body(place_ref, g_ref, r_ref, h_ref, own_ref):
        s = (g_ref[...] + r_ref[...]).astype(XFER_DTYPE)
        h_ref[...] = s

        @pl.when(pl.program_id(1) == place_ref[1])
        def _():
            own_ref[...] = s

    return pl.pallas_call(
        body, name=f"add_half_r{R}c{C}",
        grid_spec=pltpu.PrefetchScalarGridSpec(
            num_scalar_prefetch=1, grid=(R // tr, N_CHIPS),
            in_specs=[pl.BlockSpec((None, None, tr, C), lambda i, p, place_ref: (p, place_ref[0], i, 0)),
                      pl.BlockSpec((None, tr, C), lambda i, p, place_ref: (p, i, 0))],
            out_specs=[pl.BlockSpec((None, tr, C), lambda i, p, place_ref: (p, i, 0)),
                       pl.BlockSpec((None, tr, C), lambda i, p, place_ref: (place_ref[1], i, 0))],
        ),
        out_shape=[SDS((N_CHIPS, R, C), XFER_DTYPE)] * 2, compiler_params=_params("parallel", "arbitrary"),
    )(place, g4, r1)


def _sum4_into_half(r2, place):
    P, R, C = r2.shape
    tr = _row_tile(R, P * C * 4)

    def body(place_ref, a_ref, o_ref):
        acc = a_ref[0].astype(F32)
        for p in range(1, P):
            acc = acc + a_ref[p].astype(F32)
        o_ref[...] = acc

    return pl.pallas_call(
        body, name=f"sum4_r{R}c{C}",
        grid_spec=pltpu.PrefetchScalarGridSpec(
            num_scalar_prefetch=1, grid=(R // tr,),
            in_specs=[pl.BlockSpec((P, tr, C), lambda i, place_ref: (0, i, 0))],
            out_specs=pl.BlockSpec((None, tr, C), lambda i, place_ref: (place_ref[0], i, 0)),
        ),
        out_shape=SDS((2, R, C), F32), compiler_params=_params("parallel"),
    )(place, r2)


def _cast_into_slot(w, l, place, after):
    _, R, C = w.shape
    tr = _row_tile(R, C * 4)

    def body(place_ref, w_ref, after_ref, o_ref):
        o_ref[...] = w_ref[...].astype(MXU_DTYPE)

    return pl.pallas_call(
        body, name=f"cast_r{R}c{C}",
        grid_spec=pltpu.PrefetchScalarGridSpec(
            num_scalar_prefetch=1, grid=(R // tr,),
            in_specs=[pl.BlockSpec((None, tr, C), lambda i, place_ref: (l, i, 0)), ANY],
            out_specs=pl.BlockSpec((None, tr, C), lambda i, place_ref: (place_ref[1], i, 0)),
        ),
        out_shape=SDS((N_CHIPS, R, C), MXU_DTYPE), compiler_params=_params("parallel"),
    )(place, w, after)


def _place():
    x, y, c = lax.axis_index("x"), lax.axis_index("y"), lax.axis_index("c")
    chips = [(1 - x, y), (x, 1 - y), (1 - x, 1 - y)]
    return x, y, c, chips


def _remote(src, dst, send_sem, recv_sem, to):
    return pltpu.make_async_remote_copy(src_ref=src, dst_ref=dst, send_sem=send_sem, recv_sem=recv_sem, device_id=to, device_id_type=MESH)


def _all_gather8(v):
    R, N = v.shape

    def body(v_ref, out_ref, send_sems, recv_sems, local_sem):
        x, y, c, chips = _place()
        me, sibling = (x, y, c), (x, y, 1 - c)

        def slot(px, py, pc):
            return out_ref.at[4 * px + 2 * py + pc]

        def copy(k, block, to, src=None):
            return _remote(slot(*block) if src is None else src, slot(*block), send_sems.at[k], recv_sems.at[k], to)

        mine = pltpu.make_async_copy(v_ref, slot(*me), local_sem)
        mine.start()
        first = [copy(0, me, sibling, src=v_ref)] + [copy(1 + j, me, (*chip, c), src=v_ref) for j, chip in enumerate(chips)]
        for cp in first:
            cp.start()
        passed = [copy(4 + j, (*chip, c), sibling) for j, chip in enumerate(chips)]
        for j, chip in enumerate(chips):
            copy(1 + j, (*chip, c), me).wait_recv()
            passed[j].start()
        copy(0, sibling, me).wait_recv()
        for j, chip in enumerate(chips):
            copy(4 + j, (*chip, 1 - c), me).wait_recv()
        for cp in first + passed:
            cp.wait_send()
        mine.wait()

    return pl.pallas_call(
        body, name=f"all_gather8_r{R}n{N}", out_shape=SDS((N_DEV, R, N), v.dtype), in_specs=[ANY], out_specs=ANY,
        scratch_shapes=[pltpu.SemaphoreType.DMA((7,)), pltpu.SemaphoreType.DMA((7,)), pltpu.SemaphoreType.DMA],
    )(v)


def _gather_weights(slots):
    n = len(slots)

    def body(*refs):
        ins, outs = refs[:n], refs[n:2 * n]
        send_sems, recv_sems = refs[2 * n:]
        x, y, c, chips = _place()
        q = 2 * x + y
        sibling = (x, y, 1 - c)
        first = []
        for k in range(n):
            for j, chip in enumerate(chips):
                first.append(_remote(ins[k].at[q, c], outs[k].at[q, c], send_sems.at[k, j], recv_sems.at[k, j], (*chip, c)))
                first[-1].start()
        passed = []
        for k in range(n):
            for j, chip in enumerate(chips):
                half = outs[k].at[2 * chip[0] + chip[1], c]
                _remote(half, half, send_sems.at[k, j], recv_sems.at[k, j], sibling).wait_recv()
                passed.append(_remote(half, half, send_sems.at[k, 3 + j], recv_sems.at[k, 3 + j], sibling))
                passed[-1].start()
        for k in range(n):
            for j, chip in enumerate(chips):
                half = outs[k].at[2 * chip[0] + chip[1], 1 - c]
                _remote(half, half, send_sems.at[k, 3 + j], recv_sems.at[k, 3 + j], sibling).wait_recv()
        for cp in first + passed:
            cp.wait_send()

    return pl.pallas_call(
        body, name="gather_weights", out_shape=[SDS(s.shape, s.dtype) for s in slots],
        in_specs=[ANY] * n, out_specs=[ANY] * n, input_output_aliases={k: k for k in range(n)},
        scratch_shapes=[pltpu.SemaphoreType.DMA((n, 6)), pltpu.SemaphoreType.DMA((n, 6))],
    )(*slots)


def _swap_halves(g4s):
    n = len(g4s)

    def body(*refs):
        ins, outs = refs[:n], refs[n:2 * n]
        send_sems, recv_sems = refs[2 * n:]
        x, y, c, _ = _place()
        sibling = (x, y, 1 - c)
        for k in range(n):
            for p in range(N_CHIPS):
                _remote(ins[k].at[p, 1 - c], outs[k].at[p], send_sems.at[k], recv_sems.at[k], sibling).start()
        for k in range(n):
            _remote(outs[k], outs[k], send_sems.at[k], recv_sems.at[k], sibling).wait()

    return pl.pallas_call(
        body, name="swap_halves", out_shape=[SDS((N_CHIPS,) + g.shape[2:], g.dtype) for g in g4s],
        in_specs=[ANY] * n, out_specs=[ANY] * n,
        scratch_shapes=[pltpu.SemaphoreType.DMA((n,)), pltpu.SemaphoreType.DMA((n,))],
    )(*g4s)


def _scatter_regions(hs, lands):
    n = len(hs)

    def body(*refs):
        ins, outs = refs[:n], refs[2 * n:3 * n]
        send_sems, recv_sems = refs[3 * n:]
        x, y, c, chips = _place()
        q = 2 * x + y
        sent = []
        for k in range(n):
            for j, chip in enumerate(chips):
                sent.append(_remote(ins[k].at[2 * chip[0] + chip[1]], outs[k].at[q], send_sems.at[k, j], recv_sems.at[k, j], (*chip, c)))
                sent[-1].start()
        for k in range(n):
            for j, chip in enumerate(chips):
                got = outs[k].at[2 * chip[0] + chip[1]]
                _remote(got, got, send_sems.at[k, j], recv_sems.at[k, j], (x, y, c)).wait_recv()
        for cp in sent:
            cp.wait_send()

    return pl.pallas_call(
        body, name="scatter_regions", out_shape=[SDS(h.shape, h.dtype) for h in lands],
        in_specs=[ANY] * (2 * n), out_specs=[ANY] * n, input_output_aliases={n + k: k for k in range(n)},
        scratch_shapes=[pltpu.SemaphoreType.DMA((n, 3)), pltpu.SemaphoreType.DMA((n, 3))],
    )(*hs, *lands)


def _share_halves(fins):
    n = len(fins)

    def body(*refs):
        ins, outs = refs[:n], refs[n:2 * n]
        send_sems, recv_sems = refs[2 * n:]
        x, y, c, _ = _place()
        sibling = (x, y, 1 - c)
        sent = [_remote(ins[k].at[c], outs[k].at[c], send_sems.at[k], recv_sems.at[k], sibling) for k in range(n)]
        for cp in sent:
            cp.start()
        for k in range(n):
            got = outs[k].at[1 - c]
            _remote(got, got, send_sems.at[k], recv_sems.at[k], sibling).wait_recv()
        for cp in sent:
            cp.wait_send()

    return pl.pallas_call(
        body, name="share_halves", out_shape=[SDS(t.shape, t.dtype) for t in fins],
        in_specs=[ANY] * n, out_specs=[ANY] * n, input_output_aliases={k: k for k in range(n)},
        scratch_shapes=[pltpu.SemaphoreType.DMA((n,)), pltpu.SemaphoreType.DMA((n,))],
    )(*fins)


def _chip_sums(grads, place):
    g4s = [g.reshape(N_CHIPS, 2, g.shape[1] // 2, g.shape[2]) for g in grads]
    pairs = [_add_half(g4, r1, place) for g4, r1 in zip(g4s, _swap_halves(g4s))]
    return [h for h, _ in pairs], [own for _, own in pairs]


def _reduce_finish(lands, place):
    fins = _share_halves([_sum4_into_half(r2, place) for r2 in lands])
    return [f.reshape(2 * f.shape[1], f.shape[2]) for f in fins]


def _reduce_scatter(grads, place):
    hs, lands = _chip_sums(grads, place)
    return _reduce_finish(_scatter_regions(hs, lands), place)


HBM_SPEC = pl.BlockSpec(memory_space=pltpu.HBM)
SEM_SPEC = pl.BlockSpec(memory_space=pltpu.SEMAPHORE)
DATAFLOW = pltpu.SideEffectType.DATAFLOW_SIDE_EFFECTING


def _in_hbm(a):
    return pltpu.with_memory_space_constraint(a, pltpu.HBM)


def _hbm_like(a):
    return pltpu.HBM(a.shape, a.dtype)


def _gather_ici_start(slots, thru, after):
    n = len(slots)

    def body(*refs):
        ins = refs[:n]
        send_sems, recv_sems = refs[n + 2], refs[n + 3]
        x, y, c, chips = _place()
        q = 2 * x + y
        for k in range(n):
            for j, chip in enumerate(chips):
                _remote(ins[k].at[q, c], ins[k].at[q, c], send_sems.at[3 * k + j], recv_sems.at[3 * k + j], (*chip, c)).start()

    out = pl.pallas_call(
        body, name="gather_ici_start",
        out_shape=(pltpu.SemaphoreType.DMA((3 * n,)), pltpu.SemaphoreType.DMA((3 * n,)), *[_hbm_like(s) for s in slots], _hbm_like(thru)),
        in_specs=[HBM_SPEC] * (n + 1) + [ANY], out_specs=(SEM_SPEC, SEM_SPEC, *[HBM_SPEC] * (n + 1)),
        input_output_aliases={k: 2 + k for k in range(n + 1)},
        compiler_params=pltpu.CompilerParams(has_side_effects=DATAFLOW),
    )(*[_in_hbm(s) for s in slots], _in_hbm(thru), after)
    return out[0], out[1], list(out[2:2 + n]), out[2 + n]


def _gather_ici_wait(send_sems, recv_sems, slots, after):
    n = len(slots)

    def body(*refs):
        ins = refs[:n]
        send_sems, recv_sems = refs[n], refs[n + 1]
        x, y, c, chips = _place()
        for k in range(n):
            for j, chip in enumerate(chips):
                got = ins[k].at[2 * chip[0] + chip[1], c]
                cp = _remote(got, got, send_sems.at[3 * k + j], recv_sems.at[3 * k + j], (x, y, c))
                cp.wait_send()
                cp.wait_recv()

    out = pl.pallas_call(
        body, name="gather_ici_wait", out_shape=[_hbm_like(s) for s in slots],
        in_specs=[HBM_SPEC] * n + [SEM_SPEC, SEM_SPEC, ANY], out_specs=[HBM_SPEC] * n,
        input_output_aliases={k: k for k in range(n)},
        compiler_params=pltpu.CompilerParams(has_side_effects=DATAFLOW),
    )(*slots, send_sems, recv_sems, after)
    return list(out)


def _gather_pass_on(slots):
    n = len(slots)

    def body(*refs):
        ins, outs = refs[:n], refs[n:2 * n]
        send_sems, recv_sems = refs[2 * n:]
        x, y, c, chips = _place()
        sibling = (x, y, 1 - c)
        passed = []
        for k in range(n):
            for j, chip in enumerate(chips):
                passed.append(_remote(ins[k].at[2 * chip[0] + chip[1], c], outs[k].at[2 * chip[0] + chip[1], c],
                                      send_sems.at[k, j], recv_sems.at[k, j], sibling))
                passed[-1].start()
        for k in range(n):
            for j, chip in enumerate(chips):
                half = outs[k].at[2 * chip[0] + chip[1], 1 - c]
                _remote(half, half, send_sems.at[k, j], recv_sems.at[k, j], sibling).wait_recv()
        for cp in passed:
            cp.wait_send()

    return pl.pallas_call(
        body, name="gather_pass_on", out_shape=[SDS(s.shape, s.dtype) for s in slots],
        in_specs=[ANY] * n, out_specs=[ANY] * n, input_output_aliases={k: k for k in range(n)},
        scratch_shapes=[pltpu.SemaphoreType.DMA((n, 3)), pltpu.SemaphoreType.DMA((n, 3))],
    )(*slots)


def _scatter_start(hs, lands, thru):
    n = len(hs)

    def body(*refs):
        ins, zones = refs[:n], refs[n:2 * n]
        send_sems, recv_sems = refs[2 * n + 1], refs[2 * n + 2]
        x, y, c, chips = _place()
        q = 2 * x + y
        for k in range(n):
            for j, chip in enumerate(chips):
                _remote(ins[k].at[2 * chip[0] + chip[1]], zones[k].at[q], send_sems.at[3 * k + j], recv_sems.at[3 * k + j], (*chip, c)).start()

    arrays = [*hs, *lands, thru]
    out = pl.pallas_call(
        body, name="scatter_start",
        out_shape=(pltpu.SemaphoreType.DMA((3 * n,)), pltpu.SemaphoreType.DMA((3 * n,)), *[_hbm_like(a) for a in arrays]),
        in_specs=[HBM_SPEC] * len(arrays), out_specs=(SEM_SPEC, SEM_SPEC, *[HBM_SPEC] * len(arrays)),
        input_output_aliases={k: 2 + k for k in range(len(arrays))},
        compiler_params=pltpu.CompilerParams(has_side_effects=DATAFLOW),
    )(*[_in_hbm(a) for a in arrays])
    return out[0], out[1], list(out[2:2 + n]), list(out[2 + n:2 + 2 * n]), out[2 + 2 * n]


def _scatter_wait(send_sems, recv_sems, hs, lands, after):
    n = len(hs)

    def body(*refs):
        ins, zones = refs[:n], refs[n:2 * n]
        send_sems, recv_sems = refs[2 * n], refs[2 * n + 1]
        x, y, c, chips = _place()
        for k in range(n):
            for j, chip in enumerate(chips):
                p = 2 * chip[0] + chip[1]
                cp = _remote(ins[k].at[p], zones[k].at[p], send_sems.at[3 * k + j], recv_sems.at[3 * k + j], (x, y, c))
                cp.wait_send()
                cp.wait_recv()

    out = pl.pallas_call(
        body, name="scatter_wait", out_shape=[_hbm_like(a) for a in [*hs, *lands]],
        in_specs=[HBM_SPEC] * (2 * n) + [SEM_SPEC, SEM_SPEC, ANY], out_specs=[HBM_SPEC] * (2 * n),
        input_output_aliases={k: k for k in range(2 * n)},
        compiler_params=pltpu.CompilerParams(has_side_effects=DATAFLOW),
    )(*hs, *lands, send_sems, recv_sems, after)
    return list(out[n:])


def _split_start(name, arrays, n_sems, issue, extra=()):
    m = len(arrays)

    def body(*refs):
        issue(refs[:m], refs[m + len(extra)], refs[m + len(extra) + 1])

    out = pl.pallas_call(
        body, name=name,
        out_shape=(pltpu.SemaphoreType.DMA((n_sems,)), pltpu.SemaphoreType.DMA((n_sems,)), *[_hbm_like(a) for a in arrays]),
        in_specs=[HBM_SPEC] * m + [ANY] * len(extra), out_specs=(SEM_SPEC, SEM_SPEC, *[HBM_SPEC] * m),
        input_output_aliases={k: 2 + k for k in range(m)},
        compiler_params=pltpu.CompilerParams(has_side_effects=DATAFLOW),
    )(*[_in_hbm(a) for a in arrays], *extra)
    return out[0], out[1], list(out[2:])


def _split_wait(name, send_sems, recv_sems, arrays, after, drain):
    m = len(arrays)

    def body(*refs):
        drain(refs[:m], refs[m], refs[m + 1])

    out = pl.pallas_call(
        body, name=name, out_shape=[_hbm_like(a) for a in arrays],
        in_specs=[HBM_SPEC] * m + [SEM_SPEC, SEM_SPEC, ANY], out_specs=[HBM_SPEC] * m,
        input_output_aliases={k: k for k in range(m)},
        compiler_params=pltpu.CompilerParams(has_side_effects=DATAFLOW),
    )(*arrays, send_sems, recv_sems, after)
    return list(out)


def _wait_both(cp):
    cp.wait_send()
    cp.wait_recv()


class _Flight:
    def __init__(self, name, arrays, n_sems, issue, drain, thru, extra=()):
        self.name, self.drain, self.n = name, drain, len(arrays)
        self.send, self.recv, out = _split_start(name + "_start", [*arrays, thru], n_sems, issue, extra)
        self.arrays, self.thru = out[:-1], out[-1]

    def land(self, after):
        return _split_wait(self.name + "_wait", self.send, self.recv, self.arrays, after, self.drain)


def _gather_flight(tag, ici, d2d, direct, thru):
    kinds = ["ici"] * len(ici) + ["d2d"] * len(d2d) + ["direct"] * len(direct)

    def issue(refs, send_sems, recv_sems):
        x, y, c, chips = _place()
        q = 2 * x + y
        for k, kind in enumerate(kinds):
            for j, chip in enumerate(chips):
                if kind == "ici":
                    src, to = refs[k].at[q, c], (*chip, c)
                elif kind == "d2d":
                    src, to = refs[k].at[2 * chip[0] + chip[1], c], (x, y, 1 - c)
                else:
                    src, to = refs[k].at[q], (*chip, c)
                _remote(src, src, send_sems.at[3 * k + j], recv_sems.at[3 * k + j], to).start()

    def drain(refs, send_sems, recv_sems):
        x, y, c, chips = _place()
        for k, kind in enumerate(kinds):
            for j, chip in enumerate(chips):
                p = 2 * chip[0] + chip[1]
                got = refs[k].at[p] if kind == "direct" else refs[k].at[p, c if kind == "ici" else 1 - c]
                _wait_both(_remote(got, got, send_sems.at[3 * k + j], recv_sems.at[3 * k + j], (x, y, c)))

    return _Flight(f"gather{tag}", [*ici, *d2d, *direct], 3 * len(kinds), issue, drain, thru)


def _swap_flight(tag, g4s, thru):
    n = len(g4s)
    zones = [lax.empty((N_CHIPS,) + g.shape[2:], g.dtype) for g in g4s]

    def issue(refs, send_sems, recv_sems):
        x, y, c, _ = _place()
        for k in range(n):
            for p in range(N_CHIPS):
                _remote(refs[k].at[p, 1 - c], refs[n + k].at[p], send_sems.at[N_CHIPS * k + p], recv_sems.at[N_CHIPS * k + p], (x, y, 1 - c)).start()

    def drain(refs, send_sems, recv_sems):
        x, y, c, _ = _place()
        for k in range(n):
            for p in range(N_CHIPS):
                got = refs[n + k].at[p]
                _wait_both(_remote(got, got, send_sems.at[N_CHIPS * k + p], recv_sems.at[N_CHIPS * k + p], (x, y, c)))

    return _Flight(f"swap{tag}", [*g4s, *zones], N_CHIPS * n, issue, drain, thru)


def _scatter_flight(tag, hs, lands, thru):
    n = len(hs)

    def issue(refs, send_sems, recv_sems):
        x, y, c, chips = _place()
        q = 2 * x + y
        for k in range(n):
            for j, chip in enumerate(chips):
                _remote(refs[k].at[2 * chip[0] + chip[1]], refs[n + k].at[q], send_sems.at[3 * k + j], recv_sems.at[3 * k + j], (*chip, c)).start()

    def drain(refs, send_sems, recv_sems):
        x, y, c, chips = _place()
        for k in range(n):
            for j, chip in enumerate(chips):
                got = refs[n + k].at[2 * chip[0] + chip[1]]
                _wait_both(_remote(got, got, send_sems.at[3 * k + j], recv_sems.at[3 * k + j], (x, y, c)))

    return _Flight(f"scatter{tag}", [*hs, *lands], 3 * n, issue, drain, thru)


def _exchange_flight(tag, buf, thru):
    flips = [(fx, fy, fc) for fx in (0, 1) for fy in (0, 1) for fc in (0, 1)][1:]

    def peers():
        x, y, c, _ = _place()
        return (x, y, c), [((1 - x) if fx else x, (1 - y) if fy else y, (1 - c) if fc else c) for fx, fy, fc in flips]

    def slot(ref, dev):
        return ref.at[4 * dev[0] + 2 * dev[1] + dev[2]]

    def issue(refs, send_sems, recv_sems):
        me, others = peers()
        for j, to in enumerate(others):
            _remote(slot(refs[0], me), slot(refs[0], me), send_sems.at[j], recv_sems.at[j], to).start()

    def drain(refs, send_sems, recv_sems):
        me, others = peers()
        for j, frm in enumerate(others):
            got = slot(refs[0], frm)
            _wait_both(_remote(got, got, send_sems.at[j], recv_sems.at[j], me))

    return _Flight(f"exchange{tag}", [buf], len(flips), issue, drain, thru)


def _share_flight(tag, fins, thru):
    n = len(fins)

    def issue(refs, send_sems, recv_sems):
        x, y, c, _ = _place()
        for k in range(n):
            _remote(refs[k].at[c], refs[k].at[c], send_sems.at[k], recv_sems.at[k], (x, y, 1 - c)).start()

    def drain(refs, send_sems, recv_sems):
        x, y, c, _ = _place()
        for k in range(n):
            got = refs[k].at[1 - c]
            _wait_both(_remote(got, got, send_sems.at[k], recv_sems.at[k], (x, y, c)))

    return _Flight(f"share{tag}", fins, n, issue, drain, thru)


def _pair_blocks(w):
    h, d, _ = w.shape
    z = jnp.zeros((h // 2, d, d), w.dtype)
    return jnp.concatenate([jnp.concatenate([w[0::2], z], axis=2), jnp.concatenate([z, w[1::2]], axis=2)], axis=1)


def _unpair_blocks(b):
    n, dd, _ = b.shape
    d = dd // 2
    return jnp.stack([b[:, :d, :d], b[:, d:, d:]], axis=1).reshape(2 * n, d, d)


def _pad_rows(a, rows):
    return jnp.pad(a, ((0, rows - a.shape[0]), (0, 0)))


class _Packer:
    def __init__(self, shapes, width=1024, row_multiple=64):
        self.shapes = shapes
        self.sizes = [math.prod(s) for s in shapes]
        total = sum(self.sizes)
        self.width = width
        self.rows = -(-total // (width * row_multiple)) * row_multiple
        self.pad = self.rows * width - total

    def pack(self, arrays):
        flat = jnp.concatenate([a.reshape(-1).astype(F32) for a in arrays] + [jnp.zeros((self.pad,), F32)])
        return flat.reshape(self.rows, self.width)

    def unpack(self, packed):
        flat = packed.reshape(-1)
        out, off = [], 0
        for s, n in zip(self.shapes, self.sizes):
            out.append(flat[off:off + n].reshape(s))
            off += n
        return out


SMALL = ["b_ada", "ffn1_norm", "mix_norm", "conv_w", "conv_b", "gate_a_w", "gate_a_b", "gate_x_w", "gate_x_b", "lru_lambda",
         "v_norm", "spatial_w", "spatial_b", "lru_out_norm", "gmlp_out_norm", "ffn2_norm", "final_norm"]
BIG = ["ffn1_w_gu", "ffn1_w_down", "w_in", "w_out", "ffn2_w_gu", "ffn2_w_down"]
GROUPS = (("ffn1_w_gu", "ffn1_w_down"), ("w_in", "w_out"), ("ffn2_w_gu", "ffn2_w_down"))
FWD_GROUPS = (("ffn1_w_gu",), ("ffn1_w_down",), ("w_in", "w_out"), ("ffn2_w_gu",), ("ffn2_w_down",))
MIN_AGE = {"swap": 1, "scatter": 1, "share": 1}
WEIGHTS = ["w_ada", "b_ada", "ffn1_norm", "ffn1_w_gu", "ffn1_w_down", "mix_norm", "w_in", "conv_w", "conv_b", "gate_a_w", "gate_a_b",
           "gate_x_w", "gate_x_b", "lru_lambda", "v_norm", "spatial_w", "spatial_b", "lru_out_norm", "gmlp_out_norm", "w_out",
           "ffn2_norm", "ffn2_w_gu", "ffn2_w_down", "final_norm"]


def kernel(x, c, w_ada, b_ada, ffn1_norm, ffn1_w_gu, ffn1_w_down, mix_norm, w_in, conv_w, conv_b, gate_a_w, gate_a_b, gate_x_w, gate_x_b, lru_lambda, v_norm, spatial_w, spatial_b, lru_out_norm, gmlp_out_norm, w_out, ffn2_norm, ffn2_w_gu, ffn2_w_down, final_norm, loss_target, m_w_ada, m_b_ada, m_ffn1_norm, m_ffn1_w_gu, m_ffn1_w_down, m_mix_norm, m_w_in, m_conv_w, m_conv_b, m_gate_a_w, m_gate_a_b, m_gate_x_w, m_gate_x_b, m_lru_lambda, m_v_norm, m_spatial_w, m_spatial_b, m_lru_out_norm, m_gmlp_out_norm, m_w_out, m_ffn2_norm, m_ffn2_w_gu, m_ffn2_w_down, m_final_norm, v_w_ada, v_b_ada, v_ffn1_norm, v_ffn1_w_gu, v_ffn1_w_down, v_mix_norm, v_w_in, v_conv_w, v_conv_b, v_gate_a_w, v_gate_a_b, v_gate_x_w, v_gate_x_b, v_lru_lambda, v_v_norm, v_spatial_w, v_spatial_b, v_lru_out_norm, v_gmlp_out_norm, v_w_out, v_ffn2_norm, v_ffn2_w_gu, v_ffn2_w_down, v_final_norm):
    given = dict(locals())
    W = {n: given[n] for n in WEIGHTS}
    L = w_ada.shape[0]
    S, D = x.shape[1], x.shape[2]
    LW = conv_b.shape[1]
    hd = LW // HEADS
    xi, yi, ci = lax.axis_index("x"), lax.axis_index("y"), lax.axis_index("c")
    chip = 2 * xi + yi
    dev = 2 * chip + ci
    place = jnp.stack([ci, chip]).astype(jnp.int32)
    xs = x.reshape(S, D)
    tgt = loss_target.reshape(S, D)

    c_all = _all_gather8(_pad_rows(c, SUBLANES))[:, 0, :]
    n_ada = w_ada.shape[2]
    b_shard = lax.dynamic_slice_in_dim(b_ada, chip * n_ada, n_ada, axis=1)
    mod_shard = _ada_fwd(_pad_rows(c_all, 2 * SUBLANES), w_ada, b_shard[:, None, :])

    def in_slot(block):
        return lax.dynamic_update_index_in_dim(jnp.zeros((N_CHIPS,) + block.shape, block.dtype), block, chip, 0)

    cws = LW // N_CHIPS
    small = [in_slot(mod_shard.reshape(L * 2 * SUBLANES, n_ada)), in_slot(conv_w.reshape(L * CONV_WIDTH, cws))]

    def half_view(s):
        return s.reshape(N_CHIPS, 2, s.shape[1] // 2, s.shape[2])

    stages = [(l, names) for l in range(L) for names in FWD_GROUPS]
    seq = [[half_view(_cast_into_slot(W[n], l, place, place)) for n in names] for l, names in stages[:1]]
    flights = {}

    def launch(t, thru, direct=()):
        ici = seq[t] if t < len(seq) else []
        d2d = seq[t - 1] if 1 <= t <= len(seq) else []
        if ici or d2d or direct:
            flights[t] = _gather_flight(t, ici, d2d, list(direct), thru)
            thru = flights[t].thru
        return thru

    def land(t, after):
        if t not in flights:
            return []
        out = flights.pop(t).land(after)
        ni = len(seq[t]) if t < len(seq) else 0
        nd = len(seq[t - 1]) if 1 <= t <= len(seq) else 0
        if ni:
            seq[t] = out[:ni]
        if nd:
            seq[t - 1] = out[ni:ni + nd]
        return out[ni + nd:]

    def group_weights(t):
        return [s.reshape(N_CHIPS, -1, s.shape[3]) for s in seq[t]]

    c_all = launch(0, c_all, small)
    seq += [[half_view(_cast_into_slot(W[n], l, place, c_all)) for n in names] for l, names in stages[1:]]
    mod_all, conv_all = land(0, seq[-1][-1])
    mod_all = launch(1, mod_all)
    land(1, mod_all)
    mod_rows = lax.dynamic_index_in_dim(mod_all.reshape(N_CHIPS, L, 2 * SUBLANES, n_ada), dev, axis=2, keepdims=False)
    mod = mod_rows.transpose(1, 0, 2).reshape(L, N_MOD, 1, D)
    conv_full = conv_all.reshape(N_CHIPS, L, CONV_WIDTH, cws).transpose(1, 2, 0, 3).reshape(L, CONV_WIDTH, LW)

    tril = jnp.tril(jnp.ones((CHUNK, CHUNK), F32))
    seg = (jnp.arange(LW)[:, None] // hd == jnp.arange(LW)[None, :] // hd).astype(jnp.bfloat16)

    def mixer_params(l):
        ws = spatial_w[l] * tril
        wsp = jnp.concatenate([ws[0::2], ws[1::2]], axis=2)
        wa, wx = _pair_blocks(gate_a_w[l]), _pair_blocks(gate_x_w[l])
        return dict(
            cw=conv_full[l], cb=conv_b[l][None],
            wa=wa.astype(MXU_DTYPE), wx=wx.astype(MXU_DTYPE), wat=wa.transpose(0, 2, 1).astype(MXU_DTYPE), wxt=wx.transpose(0, 2, 1).astype(MXU_DTYPE),
            ba=gate_a_b[l].reshape(1, LW), bx=gate_x_b[l].reshape(1, LW), lam=lru_lambda[l][None], gv=v_norm[l][None],
            wsp=wsp.astype(MXU_DTYPE), wspt=wsp.transpose(0, 2, 1).astype(MXU_DTYPE),
            bfull=jnp.repeat(spatial_b[l].T, hd, axis=1), g_lru=lru_out_norm[l][None], g_gm=gmlp_out_norm[l][None])

    saved = []
    xcur = xs
    zero_row = jnp.zeros((1, D), F32)
    h = _modnorm(xcur, ffn1_norm[0][None], mod[0][0], mod[0][1])
    for l in range(L):
        mp, md = mixer_params(l), mod[l]
        s = dict(lw={}, mp=mp, md=md)
        lw = s["lw"]
        t = len(FWD_GROUPS) * l
        s["x0"] = xcur
        s["h1"] = launch(t + 2, h)
        lw["gu1"], = group_weights(t)
        s["a1"], s["gu1"] = _ffn_up(s["h1"], lw["gu1"])
        land(t + 2, s["a1"])
        s["a1"] = launch(t + 3, s["a1"])
        lw["d1"] = group_weights(t + 1)[0].reshape(-1, D)
        s["f1"], xcur, h = _mm_res(s["a1"], lw["d1"], xcur, md[2], 0.5, (mix_norm[l][None], md[3], md[4]))
        land(t + 3, xcur)
        s["x1"] = xcur
        s["h2"] = launch(t + 4, h)
        lw["win"], wout = group_weights(t + 2)
        lw["wout"] = wout.reshape(-1, D)
        s["proj"] = _mm_chunks(s["h2"], lw["win"])
        s["ylru"] = _lru_fwd(s["proj"], mp["cw"], mp["cb"], mp["wa"], mp["ba"], mp["wx"], mp["bx"], mp["lam"])
        s["yn"], s["ygm"] = _gmlp_fwd(s["proj"], s["ylru"], mp["gv"], seg, mp["wsp"], mp["bfull"], mp["g_lru"], mp["g_gm"])
        s["f2"], xcur, h = _mm_res(s["yn"], lw["wout"], xcur, md[5], 1.0, (ffn2_norm[l][None], md[6], md[7]))
        land(t + 4, xcur)
        s["x2"] = xcur
        s["h3"] = launch(t + 5, h)
        lw["gu2"], = group_weights(t + 3)
        s["a3"], s["gu3"] = _ffn_up(s["h3"], lw["gu2"])
        land(t + 5, s["a3"])
        s["a3"] = launch(t + 6, s["a3"])
        lw["d2"] = group_weights(t + 4)[0].reshape(-1, D)
        following = (ffn1_norm[l + 1][None], mod[l + 1][0], mod[l + 1][1]) if l + 1 < L else (final_norm[None], zero_row, zero_row)
        s["f3"], xcur, h = _mm_res(s["a3"], lw["d2"], xcur, md[8], 0.5, following)
        land(t + 6, xcur)
        saved.append(s)

    dx, dq, head_acc = _loss_head(xcur, tgt, final_norm[None], saved[-1]["md"][8], 0.5)
    loss = lax.psum(jnp.sum(head_acc[1]), ("x", "y", "c"))
    small_grads = {}
    big_grads = {n: [None] * L for n in BIG}
    dmods = [None] * L
    zero_row = jnp.zeros((1, D), F32)

    def ffn_bwd(names, l, dx, dq, x_in, h, a, gu, f, wgu, wd, gn, sc, next_gate, next_scale):
        big_grads[names[1]][l] = _mm_tn_chunks(a, dq[None], 1408, 1024)[0].reshape(N_CHIPS, -1, D)
        dgu = _ffn_bwd_act(dq, wd, gu)
        C = dgu.shape[3]
        dgu4 = dgu.reshape(N_CHIPS, S, C)
        big_grads[names[0]][l] = _mm_tn_chunks(h, dgu4, 1024, C)
        dgu4 = reduce_group(names, l, big_grads[names[0]][l], dgu4)
        dx, dq, acc = _mm_nt_norm_bwd(dgu4, wgu, x_in, dx, f, gn, sc, 0.5, next_gate, next_scale)
        return dx, move_on(dx, dq), acc

    stepped = {n: None for n in BIG}
    reducing = []

    per_layer = [n for n in SMALL if n not in ("b_ada", "final_norm")]
    packers, exchanges = {}, {}
    clock = [0]
    to_step = []

    def step_reduced(after):
        while to_step:
            name, l, g = to_step.pop(0)
            stepped[name] = _adamw_layer(W[name], g, given["m_" + name], given["v_" + name], l, stepped[name], after)
            after = stepped[name][1]
        return after

    def move_on(after, thru, force=False):
        clock[0] += 1
        for grp in list(reducing):
            if not force and clock[0] - grp["since"] < MIN_AGE[grp["step"]]:
                continue
            grp["since"] = clock[0]
            landed = grp["flight"].land(after)
            n = len(grp["names"])
            if grp["step"] == "swap":
                pairs = [_add_half(g4, r1, place) for g4, r1 in zip(landed[:n], landed[n:])]
                grp.update(step="scatter", flight=_scatter_flight(grp["tag"], [h for h, _ in pairs], [own for _, own in pairs], thru))
            elif grp["step"] == "scatter":
                grp.update(step="share", flight=_share_flight(grp["tag"], [_sum4_into_half(r2, place) for r2 in landed[n:]], thru))
            else:
                to_step.extend((name, grp["l"], fin.reshape(2 * fin.shape[1], fin.shape[2])) for name, fin in zip(grp["names"], landed))
                reducing.remove(grp)
                continue
            thru = grp["flight"].thru
        return thru

    def reduce_group(names, l, after, thru):
        thru = move_on(after, thru)
        g4s = [big_grads[n][l].reshape(N_CHIPS, 2, big_grads[n][l].shape[1] // 2, big_grads[n][l].shape[2]) for n in names]
        tag = f"{l}{GROUPS.index(names)}"
        reducing.append(dict(names=names, l=l, tag=tag, step="swap", since=clock[0], flight=_swap_flight(tag, g4s, thru)))
        return reducing[-1]["flight"].thru
    for l in reversed(range(L)):
        s = saved[l]
        lw, mp, md = s["lw"], s["mp"], s["md"]
        dx, dq, acc3 = ffn_bwd(
            GROUPS[2], l, dx, dq, s["x2"], s["h3"], s["a3"], s["gu3"], s["f3"], lw["gu2"], lw["d2"], ffn2_norm[l][None], md[7], md[5], 1.0)
        big_grads["w_out"][l] = _mm_tn_chunks(s["yn"], dq[None], 1024, 1024)[0].reshape(N_CHIPS, -1, D)
        dyn = _mm_nt_chunks(dq[None], lw["wout"][None])
        dylru, duv, dwsp, dbfull, gacc = _gmlp_bwd(s["proj"], s["ylru"], s["ygm"], dyn, mp["gv"], seg, mp["wsp"], mp["wspt"], mp["bfull"], mp["g_lru"], mp["g_gm"])
        dxg, dwa, dwx, lvec = _lru_bwd(s["proj"], dylru, mp["cw"], mp["cb"], mp["wa"], mp["ba"], mp["wx"], mp["bx"], mp["lam"], mp["wat"], mp["wxt"])
        dproj = jnp.concatenate([dxg, duv], axis=0)
        big_grads["w_in"][l] = _mm_tn_chunks(s["h2"], dproj, 1024, LW)
        dproj = reduce_group(GROUPS[1], l, big_grads["w_in"][l], dproj)
        dx, dq, acc2 = _mm_nt_norm_bwd(dproj, lw["win"], s["x1"], dx, s["f2"], mix_norm[l][None], md[4], 1.0, md[2], 0.5)
        dq = move_on(dx, dq)
        if l > 0:
            ng, ns = saved[l - 1]["md"][8], 0.5
        else:
            ng, ns = zero_row, 0.0
        dx, dq, acc1 = ffn_bwd(
            GROUPS[0], l, dx, dq, s["x0"], s["h1"], s["a1"], s["gu1"], s["f1"], lw["gu1"], lw["d1"], ffn1_norm[l][None], md[1], ng, ns)

        dmods[l] = jnp.concatenate([acc1[0:2], acc1[3:4], acc2[0:2], acc2[3:4], acc3[0:2], acc3[3:4]], axis=0)
        dws = jnp.stack([dwsp[:, :, :CHUNK], dwsp[:, :, CHUNK:]], axis=1).reshape(HEADS, CHUNK, CHUNK) * tril
        lg = {"ffn1_norm": acc1[2], "mix_norm": acc2[2], "ffn2_norm": acc3[2],
              "conv_w": lvec[4:8], "conv_b": lvec[3], "gate_a_w": _unpair_blocks(dwa), "gate_a_b": lvec[0].reshape(HEADS, hd),
              "gate_x_w": _unpair_blocks(dwx), "gate_x_b": lvec[1].reshape(HEADS, hd), "lru_lambda": lvec[2], "v_norm": gacc[2],
              "spatial_w": dws, "spatial_b": dbfull.reshape(CHUNK, HEADS, hd).sum(-1).T, "lru_out_norm": gacc[0], "gmlp_out_norm": gacc[1]}
        part = [lg[n] for n in per_layer] + [dmods[l]] + ([head_acc[0]] if l == L - 1 else [])
        packers[l] = _Packer([p.shape for p in part])
        packed = packers[l].pack(part)
        exchanges[l] = _exchange_flight(l, lax.dynamic_update_index_in_dim(jnp.zeros((N_DEV,) + packed.shape, F32), packed, dev, 0), dq)
        dq = exchanges[l].thru

    grad_x = dx.reshape(x.shape)

    done = step_reduced(dq)
    while reducing:
        dq = move_on(done, dq, force=True)
        done = step_reduced(dq)
    summed, dmod_rows = [], []
    for l in range(L):
        gathered, = exchanges[l].land(done)
        summed.append(packers[l].unpack(_sum_leading(gathered)))
        off = sum(packers[l].sizes[:len(per_layer)])
        dmod_rows.append(gathered.reshape(N_DEV, -1)[:, off:off + N_MOD * D])
    grads = {n: jnp.stack([summed[l][k] for l in range(L)]) for k, n in enumerate(per_layer)}
    grads["final_norm"] = summed[L - 1][len(per_layer) + 1]
    grads["b_ada"] = jnp.stack([summed[l][len(per_layer)].reshape(N_MOD * D) for l in range(L)])
    dmod_shard = lax.dynamic_slice_in_dim(jnp.stack(dmod_rows), chip * n_ada, n_ada, axis=2)
    grads["w_ada"] = _ada_grad(c_all.T, dmod_shard)
    grads["conv_w"] = lax.dynamic_slice_in_dim(grads["conv_w"], chip * cws, cws, axis=2)

    delta, new_m, new_v = {}, {}, {}
    for n in BIG:
        grads[n], delta[n], new_m[n], new_v[n] = stepped[n]
    shp = w_ada.shape
    d_, m_, v_ = _adamw(*[a.reshape(-1, shp[-1]) for a in (w_ada, grads["w_ada"], m_w_ada, v_w_ada)])
    delta["w_ada"], new_m["w_ada"], new_v["w_ada"] = d_.reshape(shp), m_.reshape(shp), v_.reshape(shp)
    spk = _Packer([W[n].shape for n in SMALL])
    d_, m_, v_ = _adamw(spk.pack([W[n] for n in SMALL]), spk.pack([grads[n] for n in SMALL]),
                        spk.pack([given["m_" + n] for n in SMALL]), spk.pack([given["v_" + n] for n in SMALL]))
    for n, a, b, e in zip(SMALL, spk.unpack(d_), spk.unpack(m_), spk.unpack(v_)):
        delta[n], new_m[n], new_v[n] = a, b, e
    grads = {n: grads[n].reshape(W[n].shape) for n in WEIGHTS}
    return (loss, grad_x, *[grads[n] for n in WEIGHTS], *[delta[n] for n in WEIGHTS], *[new_m[n] for n in WEIGHTS], *[new_v[n] for n in WEIGHTS])
```

```python
import math

import jax
import jax.numpy as jnp
from jax import lax
from jax.experimental import pallas as pl
from jax.experimental.pallas import tpu as pltpu

F32 = jnp.float32
MXU_DTYPE = jnp.bfloat16
ACT_DTYPE = jnp.bfloat16
XFER_DTYPE = jnp.bfloat16
EPS = 1e-6
RG_LRU_C = 8.0
N_MOD = 9
CONV_WIDTH = 4
HEADS = 8
CHUNK = 128
LANES = 128
SUBLANES = 8
N_CHIPS = 4
N_DEV = 8
ADAM_LR, ADAM_B1, ADAM_B2, ADAM_EPS, ADAM_WD, ADAM_STEP = 0.001, 0.9, 0.999, 1e-08, 0.01, 10
VMEM_LIMIT_BYTES = 60 * 1024 * 1024
ROW_TILE_BYTES = 2 << 20
GELU_C = math.sqrt(2.0 / math.pi)
GELU_A = 0.044715

ANY = pl.BlockSpec(memory_space=pl.ANY)
MESH = pl.DeviceIdType.MESH
SDS = jax.ShapeDtypeStruct


def _params(*sem):
    return pltpu.CompilerParams(dimension_semantics=sem, vmem_limit_bytes=VMEM_LIMIT_BYTES)


def _dot(a, b):
    return jnp.dot(a.astype(MXU_DTYPE), b.astype(MXU_DTYPE), preferred_element_type=F32)


def _dot_nt(a, b):
    return lax.dot_general(a.astype(MXU_DTYPE), b.astype(MXU_DTYPE), (((1,), (1,)), ((), ())), preferred_element_type=F32)


def _dot_tn(a, b):
    return lax.dot_general(a.astype(MXU_DTYPE), b.astype(MXU_DTYPE), (((0,), (0,)), ((), ())), preferred_element_type=F32)


def _gelu(x):
    return x * (0.5 * (1.0 + jnp.tanh(GELU_C * (x + GELU_A * (x * x * x)))))


def _gelu_grad(x):
    t = jnp.tanh(GELU_C * (x + GELU_A * (x * x * x)))
    return 0.5 * (1.0 + t) + 0.5 * x * (1.0 - t * t) * (GELU_C * (1.0 + 3.0 * GELU_A * x * x))


def _sigmoid(x):
    return jax.nn.sigmoid(x)


def _rsqrt_ms(x):
    return lax.rsqrt(jnp.mean(x * x, axis=-1, keepdims=True) + EPS)


def _rowsum(x):
    return jnp.sum(x, axis=0, keepdims=True)


def _tile(n, want):
    t = min(n, want)
    assert n % t == 0, (n, want)
    return t


def _row_tile(rows, row_bytes):
    step = 2 * SUBLANES
    cap = max(step, ROW_TILE_BYTES // row_bytes)
    best = None
    for t in range(step, min(rows, cap) + 1, step):
        if rows % t == 0:
            best = t
    assert best is not None, (rows, row_bytes)
    return best


def _modnorm(x, gn, sh, sc):
    S, D = x.shape
    tm = _tile(S, 1024)

    def body(x_ref, gn_ref, sh_ref, sc_ref, h_ref):
        xv = x_ref[...]
        h = (xv * _rsqrt_ms(xv) * gn_ref[...]) * (1.0 + sc_ref[...]) + sh_ref[...]
        h_ref[...] = h.astype(ACT_DTYPE)

    row = pl.BlockSpec((1, D), lambda i: (0, 0))
    return pl.pallas_call(
        body, name="modnorm", grid=(S // tm,),
        in_specs=[pl.BlockSpec((tm, D), lambda i: (i, 0)), row, row, row],
        out_specs=pl.BlockSpec((tm, D), lambda i: (i, 0)),
        out_shape=SDS((S, D), ACT_DTYPE), compiler_params=_params("parallel"),
    )(x, gn, sh, sc)


def _mm_nt_norm_bwd(ac, wc, x, dxo, f, gn, sc, res_scale, next_gate, next_scale):
    P, S, K = ac.shape
    D = x.shape[1]
    tm = _tile(S, 512)

    def body(a_ref, w_ref, x_ref, dxo_ref, f_ref, gn_ref, sc_ref, ng_ref, dx_ref, dq_ref, acc_ref):
        @pl.when(pl.program_id(0) == 0)
        def _():
            acc_ref[...] = jnp.zeros_like(acc_ref)

        dh = _dot_nt(a_ref[0], w_ref[0])
        for p in range(1, P):
            dh += _dot_nt(a_ref[p], w_ref[p])
        xv, dxo = x_ref[...], dxo_ref[...]
        r = _rsqrt_ms(xv)
        xhat = xv * r
        gn = gn_ref[...]
        dn = dh * (1.0 + sc_ref[...])
        dxh = dn * gn
        dx = dxo + r * (dxh - xhat * jnp.mean(dxh * xhat, axis=-1, keepdims=True))
        dx_ref[...] = dx
        dq_ref[...] = ((next_scale * ng_ref[...]) * dx).astype(ACT_DTYPE)
        acc_ref[0:1, :] += _rowsum(dh)
        acc_ref[1:2, :] += _rowsum(dh * (xhat * gn))
        acc_ref[2:3, :] += _rowsum(dn * xhat)
        acc_ref[3:4, :] += _rowsum((res_scale * f_ref[...]) * dxo)

    tile = pl.BlockSpec((tm, D), lambda i: (i, 0))
    row = pl.BlockSpec((1, D), lambda i: (0, 0))
    return pl.pallas_call(
        body, name=f"mm_nt_norm_bwd_k{K}", grid=(S // tm,),
        in_specs=[pl.BlockSpec((P, tm, K), lambda i: (0, i, 0)),
                  pl.BlockSpec((P, D, K), lambda i: (0, 0, 0), pipeline_mode=pl.Buffered(1)),
                  tile, tile, tile, row, row, row],
        out_specs=[tile, tile, pl.BlockSpec((SUBLANES, D), lambda i: (0, 0))],
        out_shape=[SDS((S, D), F32), SDS((S, D), ACT_DTYPE), SDS((SUBLANES, D), F32)],
        compiler_params=_params("arbitrary"),
    )(ac, wc, x, dxo, f, gn, sc, next_gate)


def _loss_head(x, target, gn, next_gate, next_scale):
    S, D = x.shape
    tm = _tile(S, 512)

    def body(x_ref, t_ref, gn_ref, ng_ref, dx_ref, dq_ref, acc_ref):
        @pl.when(pl.program_id(0) == 0)
        def _():
            acc_ref[...] = jnp.zeros_like(acc_ref)

        xv = x_ref[...]
        r = _rsqrt_ms(xv)
        xhat = xv * r
        gn = gn_ref[...]
        err = xhat * gn - t_ref[...]
        dy = err * (1.0 / D)
        dxh = dy * gn
        dx = r * (dxh - xhat * jnp.mean(dxh * xhat, axis=-1, keepdims=True))
        dx_ref[...] = dx
        dq_ref[...] = ((next_scale * ng_ref[...]) * dx).astype(ACT_DTYPE)
        acc_ref[0:1, :] += _rowsum(dy * xhat)
        acc_ref[1:2, :] += _rowsum(err * err) * (0.5 / D)

    tile = pl.BlockSpec((tm, D), lambda i: (i, 0))
    row = pl.BlockSpec((1, D), lambda i: (0, 0))
    return pl.pallas_call(
        body, name="loss_head", grid=(S // tm,),
        in_specs=[tile, tile, row, row],
        out_specs=[tile, tile, pl.BlockSpec((SUBLANES, D), lambda i: (0, 0))],
        out_shape=[SDS((S, D), F32), SDS((S, D), ACT_DTYPE), SDS((SUBLANES, D), F32)],
        compiler_params=_params("arbitrary"),
    )(x, target, gn, next_gate)


def _ffn_up(h, wgu):
    S, D = h.shape
    C = wgu.shape[2]
    tm = _tile(S, 512)

    def body(h_ref, wg_ref, wu_ref, a_ref, gu_ref):
        hv = h_ref[...]
        g = _dot(hv, wg_ref[...])
        u = _dot(hv, wu_ref[...])
        a_ref[...] = (g * _sigmoid(g) * u).astype(ACT_DTYPE)
        gu_ref[0] = g.astype(ACT_DTYPE)
        gu_ref[1] = u.astype(ACT_DTYPE)

    return pl.pallas_call(
        body, name="ffn_up", grid=(2, S // tm),
        in_specs=[
            pl.BlockSpec((tm, D), lambda j, i: (i, 0)),
            pl.BlockSpec((None, D, C), lambda j, i: (j, 0, 0)),
            pl.BlockSpec((None, D, C), lambda j, i: (2 + j, 0, 0)),
        ],
        out_specs=[
            pl.BlockSpec((tm, C), lambda j, i: (i, j)),
            pl.BlockSpec((2, None, tm, C), lambda j, i: (0, j, i, 0)),
        ],
        out_shape=[SDS((S, 2 * C), ACT_DTYPE), SDS((2, 2, S, C), ACT_DTYPE)],
        compiler_params=_params("parallel", "parallel"),
    )(h, wgu, wgu)


def _ffn_bwd_act(dq, wd, gu):
    S, D = dq.shape
    C = gu.shape[3]
    tm = _tile(S, 512)

    def body(dq_ref, wd_ref, gu_ref, dgu_ref):
        da = _dot_nt(dq_ref[...], wd_ref[...])
        g = gu_ref[0].astype(F32)
        u = gu_ref[1].astype(F32)
        s = _sigmoid(g)
        dgu_ref[0] = (da * u * (s * (1.0 + g * (1.0 - s)))).astype(ACT_DTYPE)
        dgu_ref[1] = (da * (g * s)).astype(ACT_DTYPE)

    gu_spec = pl.BlockSpec((2, None, tm, C), lambda j, i: (0, j, i, 0))
    return pl.pallas_call(
        body, name="ffn_bwd_act", grid=(2, S // tm),
        in_specs=[pl.BlockSpec((tm, D), lambda j, i: (i, 0)), pl.BlockSpec((C, D), lambda j, i: (j, 0)), gu_spec],
        out_specs=gu_spec,
        out_shape=SDS(gu.shape, ACT_DTYPE),
        compiler_params=_params("parallel", "parallel"),
    )(dq, wd, gu)


def _mm_res(a, w, x, gate, scale, following):
    S, K = a.shape
    D = w.shape[1]
    tm = _tile(S, 512)

    def body(a_ref, w_ref, x_ref, g_ref, gn_ref, sh_ref, sc_ref, f_ref, xo_ref, h_ref):
        f = _dot(a_ref[...], w_ref[...])
        f_ref[...] = f
        xo = x_ref[...] + (scale * g_ref[...]) * f
        xo_ref[...] = xo
        h_ref[...] = ((xo * _rsqrt_ms(xo) * gn_ref[...]) * (1.0 + sc_ref[...]) + sh_ref[...]).astype(ACT_DTYPE)

    tile = pl.BlockSpec((tm, D), lambda i: (i, 0))
    row = pl.BlockSpec((1, D), lambda i: (0, 0))
    return pl.pallas_call(
        body, name=f"mm_res_k{K}", grid=(S // tm,),
        in_specs=[pl.BlockSpec((tm, K), lambda i: (i, 0)), pl.BlockSpec((K, D), lambda i: (0, 0)), tile, row, row, row, row],
        out_specs=[tile, tile, tile],
        out_shape=[SDS((S, D), F32), SDS((S, D), F32), SDS((S, D), ACT_DTYPE)],
        compiler_params=_params("parallel"),
    )(a, w, x, gate, *following)


def _mm_chunks(h, wc):
    S, K = h.shape
    P, _, N = wc.shape
    tm = _tile(S, 512)

    def body(h_ref, w_ref, o_ref):
        hv = h_ref[...]
        for p in range(P):
            o_ref[:, p * N:(p + 1) * N] = _dot(hv, w_ref[p])

    return pl.pallas_call(
        body, name="mm_chunks", grid=(S // tm,),
        in_specs=[pl.BlockSpec((tm, K), lambda i: (i, 0)), pl.BlockSpec((P, K, N), lambda i: (0, 0, 0))],
        out_specs=pl.BlockSpec((tm, P * N), lambda i: (i, 0)),
        out_shape=SDS((S, P * N), F32),
        compiler_params=_params("parallel"),
    )(h, wc)


def _mm_nt_chunks(ac, wc):
    P, S, K = ac.shape
    N = wc.shape[1]
    tm, tn = _tile(S, 512), _tile(N, 1024)

    def body(a_ref, w_ref, o_ref):
        acc = _dot_nt(a_ref[0], w_ref[0])
        for p in range(1, P):
            acc += _dot_nt(a_ref[p], w_ref[p])
        o_ref[...] = acc

    return pl.pallas_call(
        body, name=f"mm_nt_p{P}k{K}", grid=(S // tm, N // tn),
        in_specs=[pl.BlockSpec((P, tm, K), lambda i, j: (0, i, 0)), pl.BlockSpec((P, tn, K), lambda i, j: (0, j, 0))],
        out_specs=pl.BlockSpec((tm, tn), lambda i, j: (i, j)),
        out_shape=SDS((S, N), F32),
        compiler_params=_params("parallel", "parallel"),
    )(ac, wc)


def _mm_tn_chunks(a, bc, tile_m, tile_n):
    S, M = a.shape
    P, _, N = bc.shape
    ts, tm, tn = _tile(S, 2048), _tile(M, tile_m), _tile(N, tile_n)

    def body(a_ref, b_ref, o_ref):
        @pl.when(pl.program_id(3) == 0)
        def _():
            o_ref[...] = jnp.zeros_like(o_ref)

        o_ref[...] += _dot_tn(a_ref[...], b_ref[...])

    return pl.pallas_call(
        body, name=f"mm_tn_m{M}n{N}", grid=(P, M // tm, N // tn, S // ts),
        in_specs=[pl.BlockSpec((ts, tm), lambda p, m, n, k: (k, m)), pl.BlockSpec((None, ts, tn), lambda p, m, n, k: (p, k, n))],
        out_specs=pl.BlockSpec((None, tm, tn), lambda p, m, n, k: (p, m, n)),
        out_shape=SDS((P, M, N), F32),
        compiler_params=_params("parallel", "parallel", "parallel", "arbitrary"),
    )(a, bc)


def _shift_down(x, s, row, fill):
    return jnp.where(row >= s, pltpu.roll(x, s, 0), fill)


def _shift_up(x, s, row, fill):
    n = x.shape[0]
    return jnp.where(row < n - s, pltpu.roll(x, n - s, 0), fill)


def _scan(a, b, row, scratch, up):
    scr_a, scr_b, scr_c = scratch
    n = a.shape[0]
    g = n // SUBLANES
    in_group = row & (SUBLANES - 1)

    def steps(a, b, pos, size):
        s = 1
        while s < size:
            m = (pos + s < size) if up else (pos >= s)
            b = jnp.where(m, a, 0.0) * pltpu.roll(b, a.shape[0] - s if up else s, 0) + b
            a = jnp.where(m, a * pltpu.roll(a, a.shape[0] - s if up else s, 0), a)
            s *= 2
        return a, b

    a, b = steps(a, b, in_group, SUBLANES)
    scr_a[...] = a
    scr_b[...] = b
    edge = 0 if up else SUBLANES - 1
    at = scr_a[pl.ds(edge, g, stride=SUBLANES), :]
    bt = scr_b[pl.ds(edge, g, stride=SUBLANES), :]
    group = lax.broadcasted_iota(jnp.int32, at.shape, 0)
    _, state = steps(at, bt, group, g)
    carry = jnp.where((group + 1 < g) if up else (group >= 1), pltpu.roll(state, g - 1 if up else 1, 0), 0.0)
    for k in range(SUBLANES):
        scr_c[pl.ds(k, g, stride=SUBLANES), :] = carry
    return b + a * scr_c[...]


def _conv(xl, cw_ref, cb_ref, row):
    y = cb_ref[...] + _shift_down(xl, 3, row, 0.0) * cw_ref[0:1, :]
    y = y + _shift_down(xl, 2, row, 0.0) * cw_ref[1:2, :]
    y = y + _shift_down(xl, 1, row, 0.0) * cw_ref[2:3, :]
    return y + xl * cw_ref[3:4, :]


def _lru_gates(xc, wa_ref, ba_ref, wx_ref, bx_ref, lam_ref):
    ra = _sigmoid(_dot(xc, wa_ref[...]) + ba_ref[...])
    ri = _sigmoid(_dot(xc, wx_ref[...]) + bx_ref[...])
    ls = jax.nn.log_sigmoid(lam_ref[...])
    a = jnp.exp((RG_LRU_C * ra) * ls)
    mult = jnp.sqrt(1.0 - a * a)
    return ra, ri, ls, a, mult


def _lru_specs(S):
    col = lambda off: pl.BlockSpec((S, LANES), lambda j: (0, off + j))
    vec = pl.BlockSpec((1, LANES), lambda j: (0, j))
    blk = pl.BlockSpec((None, LANES, LANES), lambda j: (j, 0, 0))
    cw = pl.BlockSpec((CONV_WIDTH, LANES), lambda j: (0, j))
    return col, vec, blk, cw


def _lru_fwd(proj, cw, cb, wa, ba, wx, bx, lam):
    S = proj.shape[0]
    W = cb.shape[1]
    nb = W // LANES

    def body(xl_ref, gl_ref, cw_ref, cb_ref, wa_ref, ba_ref, wx_ref, bx_ref, lam_ref, y_ref, *scratch):
        row = lax.broadcasted_iota(jnp.int32, (S, LANES), 0)
        xc = _conv(xl_ref[...], cw_ref, cb_ref, row)
        _, ri, _, a, mult = _lru_gates(xc, wa_ref, ba_ref, wx_ref, bx_ref, lam_ref)
        h = _scan(a, mult * (ri * xc), row, scratch, up=False)
        y_ref[...] = h * _gelu(gl_ref[...])

    col, vec, blk, cws = _lru_specs(S)
    return pl.pallas_call(
        body, name="lru_fwd", grid=(nb,),
        in_specs=[col(0), col(nb), cws, vec, blk, vec, blk, vec, vec],
        out_specs=pl.BlockSpec((S, LANES), lambda j: (0, j)),
        out_shape=SDS((S, W), F32), scratch_shapes=[pltpu.VMEM((S, LANES), F32)] * 3, compiler_params=_params("parallel"),
    )(proj, proj, cw, cb, wa, ba, wx, bx, lam)


def _lru_bwd(proj, dy, cw, cb, wa, ba, wx, bx, lam, wat, wxt):
    S = proj.shape[0]
    W = cb.shape[1]
    nb = W // LANES

    def body(xl_ref, gl_ref, dy_ref, cw_ref, cb_ref, wa_ref, ba_ref, wx_ref, bx_ref, lam_ref, wat_ref, wxt_ref,
             dp_ref, dwa_ref, dwx_ref, vec_ref, *scratch):
        row = lax.broadcasted_iota(jnp.int32, (S, LANES), 0)
        xl = xl_ref[...]
        xc = _conv(xl, cw_ref, cb_ref, row)
        ra, ri, ls, a, mult = _lru_gates(xc, wa_ref, ba_ref, wx_ref, bx_ref, lam_ref)
        h = _scan(a, mult * (ri * xc), row, scratch, up=False)
        gl = gl_ref[...]
        dyv = dy_ref[...]
        dp_ref[1] = (dyv * h * _gelu_grad(gl)).astype(ACT_DTYPE)
        adj = _scan(_shift_up(a, 1, row, 0.0), dyv * _gelu(gl), row, scratch, up=True)
        da = adj * _shift_down(h, 1, row, 0.0)
        dmult = adj * (ri * xc)
        dlog_a = da * a - dmult * (a * a) / mult
        dra = dlog_a * (RG_LRU_C * ls)
        dpa = dra * ra * (1.0 - ra)
        dpi = (adj * mult * xc) * ri * (1.0 - ri)
        dxc = adj * mult * ri + _dot(dpa, wat_ref[...]) + _dot(dpi, wxt_ref[...])
        dwa_ref[...] = _dot_tn(xc, dpa)
        dwx_ref[...] = _dot_tn(xc, dpi)
        dxl = dxc * cw_ref[3:4, :]
        dxl = dxl + _shift_up(dxc, 1, row, 0.0) * cw_ref[2:3, :]
        dxl = dxl + _shift_up(dxc, 2, row, 0.0) * cw_ref[1:2, :]
        dxl = dxl + _shift_up(dxc, 3, row, 0.0) * cw_ref[0:1, :]
        dp_ref[0] = dxl.astype(ACT_DTYPE)
        vec_ref[...] = jnp.zeros_like(vec_ref)
        vec_ref[0:1, :] = _rowsum(dpa)
        vec_ref[1:2, :] = _rowsum(dpi)
        vec_ref[2:3, :] = _rowsum(dlog_a * (RG_LRU_C * ra)) * _sigmoid(-lam_ref[...])
        vec_ref[3:4, :] = _rowsum(dxc)
        vec_ref[4:5, :] = _rowsum(dxc * _shift_down(xl, 3, row, 0.0))
        vec_ref[5:6, :] = _rowsum(dxc * _shift_down(xl, 2, row, 0.0))
        vec_ref[6:7, :] = _rowsum(dxc * _shift_down(xl, 1, row, 0.0))
        vec_ref[7:8, :] = _rowsum(dxc * xl)

    col, vec, blk, cws = _lru_specs(S)
    return pl.pallas_call(
        body, name="lru_bwd", grid=(nb,),
        in_specs=[col(0), col(nb), col(0), cws, vec, blk, vec, blk, vec, vec, blk, blk],
        out_specs=[pl.BlockSpec((2, S, LANES), lambda j: (0, 0, j)), blk, blk, pl.BlockSpec((2 * SUBLANES, LANES), lambda j: (0, j))],
        out_shape=[SDS((2, S, W), ACT_DTYPE), SDS((nb, LANES, LANES), F32), SDS((nb, LANES, LANES), F32), SDS((2 * SUBLANES, W), F32)],
        scratch_shapes=[pltpu.VMEM((S, LANES), F32)] * 3, compiler_params=_params("parallel"),
    )(proj, proj, dy, cw, cb, wa, ba, wx, bx, lam, wat, wxt)


def _seg_mean(x, seg_ref, width):
    hi = x.astype(jnp.bfloat16)
    lo = (x - hi.astype(F32)).astype(jnp.bfloat16)
    ones = seg_ref[...]
    s = jnp.dot(hi, ones, preferred_element_type=F32) + jnp.dot(lo, ones, preferred_element_type=F32)
    return s * (1.0 / width)


def _gmlp_core(u_ref, v_ref, gv_ref, seg_ref, ws_ref, bfull_ref, z_scr, hd):
    tm, W = u_ref.shape
    lane = lax.broadcasted_iota(jnp.int32, (CHUNK, LANES), 1)
    ug = _gelu(u_ref[...])
    vg = _gelu(v_ref[...])
    cen = vg - _seg_mean(vg, seg_ref, hd)
    rstd = lax.rsqrt(_seg_mean(cen * cen, seg_ref, hd) + EPS)
    vhat = cen * rstd
    vh = vhat * gv_ref[...]
    vcats = {}
    for ci in range(tm // CHUNK):
        for p in range(W // LANES):
            blk = vh[ci * CHUNK:(ci + 1) * CHUNK, p * LANES:(p + 1) * LANES]
            vcat = jnp.concatenate([jnp.where(lane < hd, blk, 0.0), jnp.where(lane >= hd, blk, 0.0)], axis=0).astype(MXU_DTYPE)
            vcats[ci, p] = vcat
            z_scr[ci * CHUNK:(ci + 1) * CHUNK, p * LANES:(p + 1) * LANES] = (
                jnp.dot(ws_ref[p], vcat, preferred_element_type=F32) + bfull_ref[:, p * LANES:(p + 1) * LANES])
    return ug, vhat, rstd, vcats


def _gmlp_specs(tm, W, nb):
    rows = lambda off: pl.BlockSpec((tm, W), lambda i: (i, off))
    vec = pl.BlockSpec((1, W), lambda i: (0, 0))
    seg = pl.BlockSpec((W, W), lambda i: (0, 0))
    wsp = pl.BlockSpec((nb, CHUNK, 2 * CHUNK), lambda i: (0, 0, 0))
    bfull = pl.BlockSpec((CHUNK, W), lambda i: (0, 0))
    return rows, vec, seg, wsp, bfull


def _gmlp_fwd(proj, ylru, gv, seg, wsp, bfull, g_lru, g_gm):
    S, W = ylru.shape
    nb = W // LANES
    hd = W // HEADS
    tm = _tile(S, 512)

    def body(u_ref, v_ref, yl_ref, gv_ref, seg_ref, ws_ref, bfull_ref, gl_ref, gg_ref, yn_ref, ygm_ref, z_scr):
        ug, _, _, _ = _gmlp_core(u_ref, v_ref, gv_ref, seg_ref, ws_ref, bfull_ref, z_scr, hd)
        ygm = ug * z_scr[...]
        ygm_ref[...] = ygm
        yl = yl_ref[...]
        yn_ref[:, 0:W] = (yl * _rsqrt_ms(yl) * gl_ref[...]).astype(ACT_DTYPE)
        yn_ref[:, W:2 * W] = (ygm * _rsqrt_ms(ygm) * gg_ref[...]).astype(ACT_DTYPE)

    rows, vec, segs, wsps, bfulls = _gmlp_specs(tm, W, nb)
    return pl.pallas_call(
        body, name="gmlp_fwd", grid=(S // tm,),
        in_specs=[rows(2), rows(3), rows(0), vec, segs, wsps, bfulls, vec, vec],
        out_specs=[pl.BlockSpec((tm, 2 * W), lambda i: (i, 0)), rows(0)],
        out_shape=[SDS((S, 2 * W), ACT_DTYPE), SDS((S, W), F32)],
        scratch_shapes=[pltpu.VMEM((tm, W), F32)],
        compiler_params=_params("parallel"),
    )(proj, proj, ylru, gv, seg, wsp, bfull, g_lru, g_gm)


def _rms_bwd(y, g, dyn):
    r = _rsqrt_ms(y)
    yhat = y * r
    dyh = dyn * g
    return r * (dyh - yhat * jnp.mean(dyh * yhat, axis=-1, keepdims=True)), _rowsum(dyn * yhat)


def _gmlp_bwd(proj, ylru, ygm, dyn, gv, seg, wsp, wspt, bfull, g_lru, g_gm):
    S, W = ylru.shape
    nb = W // LANES
    hd = W // HEADS
    tm = _tile(S, 256)

    def body(u_ref, v_ref, yl_ref, ygm_ref, dl_ref, dg_ref, gv_ref, seg_ref, ws_ref, wst_ref, bfull_ref, gl_ref, gg_ref,
             dyl_ref, duv_ref, dws_ref, dbf_ref, acc_ref, z_scr, dvh_scr):
        @pl.when(pl.program_id(0) == 0)
        def _():
            dws_ref[...] = jnp.zeros_like(dws_ref)
            dbf_ref[...] = jnp.zeros_like(dbf_ref)
            acc_ref[...] = jnp.zeros_like(acc_ref)

        dyl, dgl = _rms_bwd(yl_ref[...], gl_ref[...], dl_ref[...])
        dyl_ref[...] = dyl
        dygm, dgg = _rms_bwd(ygm_ref[...], gg_ref[...], dg_ref[...])
        ug, vhat, rstd, vcats = _gmlp_core(u_ref, v_ref, gv_ref, seg_ref, ws_ref, bfull_ref, z_scr, hd)
        duv_ref[0] = (dygm * z_scr[...] * _gelu_grad(u_ref[...])).astype(ACT_DTYPE)
        dz = dygm * ug
        lane = lax.broadcasted_iota(jnp.int32, (CHUNK, LANES), 1)
        dbf = dz[0:CHUNK, :]
        for ci in range(1, tm // CHUNK):
            dbf += dz[ci * CHUNK:(ci + 1) * CHUNK, :]
        dbf_ref[...] += dbf
        for ci in range(tm // CHUNK):
            for p in range(nb):
                dzb = dz[ci * CHUNK:(ci + 1) * CHUNK, p * LANES:(p + 1) * LANES].astype(MXU_DTYPE)
                dws_ref[p] += _dot_nt(dzb, vcats[ci, p])
                dvc = jnp.dot(wst_ref[p], dzb, preferred_element_type=F32)
                dvh_scr[ci * CHUNK:(ci + 1) * CHUNK, p * LANES:(p + 1) * LANES] = jnp.where(lane < hd, dvc[0:CHUNK], dvc[CHUNK:2 * CHUNK])
        dvh = dvh_scr[...]
        dvn = dvh * gv_ref[...]
        dvg = rstd * (dvn - _seg_mean(dvn, seg_ref, hd) - vhat * _seg_mean(dvn * vhat, seg_ref, hd))
        duv_ref[1] = (dvg * _gelu_grad(v_ref[...])).astype(ACT_DTYPE)
        acc_ref[0:1, :] += dgl
        acc_ref[1:2, :] += dgg
        acc_ref[2:3, :] += _rowsum(dvh * vhat)

    rows, vec, segs, wsps, bfulls = _gmlp_specs(tm, W, nb)
    wspt_spec = pl.BlockSpec((nb, 2 * CHUNK, CHUNK), lambda i: (0, 0, 0))
    return pl.pallas_call(
        body, name="gmlp_bwd", grid=(S // tm,),
        in_specs=[rows(2), rows(3), rows(0), rows(0), rows(0), rows(1), vec, segs, wsps, wspt_spec, bfulls, vec, vec],
        out_specs=[rows(0), pl.BlockSpec((2, tm, W), lambda i: (0, i, 0)), wsps, bfulls, pl.BlockSpec((SUBLANES, W), lambda i: (0, 0))],
        out_shape=[SDS((S, W), F32), SDS((2, S, W), ACT_DTYPE), SDS((nb, CHUNK, 2 * CHUNK), F32), SDS((CHUNK, W), F32), SDS((SUBLANES, W), F32)],
        scratch_shapes=[pltpu.VMEM((tm, W), F32), pltpu.VMEM((tm, W), F32)],
        compiler_params=_params("arbitrary"),
    )(proj, proj, ylru, ygm, dyn, dyn, gv, seg, wsp, wspt, bfull, g_lru, g_gm)


def _ada_fwd(c_all, w_ada, b_shard):
    L, D, N = w_ada.shape
    R = c_all.shape[0]
    tn = N // 2

    def body(c_ref, w_ref, b_ref, o_ref):
        cv = c_ref[...]
        o_ref[...] = _dot(cv * _sigmoid(cv), w_ref[...]) + b_ref[...]

    return pl.pallas_call(
        body, name="ada_fwd", grid=(L, N // tn),
        in_specs=[pl.BlockSpec((R, D), lambda l, j: (0, 0)), pl.BlockSpec((None, D, tn), lambda l, j: (l, 0, j)),
                  pl.BlockSpec((None, 1, tn), lambda l, j: (l, 0, j))],
        out_specs=pl.BlockSpec((None, R, tn), lambda l, j: (l, 0, j)),
        out_shape=SDS((L, R, N), F32), compiler_params=_params("parallel", "parallel"),
    )(c_all, w_ada, b_shard)


def _ada_grad(c_all_t, dmod):
    D, B = c_all_t.shape
    L, _, N = dmod.shape
    tn = N // 2

    def body(c_ref, d_ref, o_ref):
        cv = c_ref[...]
        sc = cv * _sigmoid(cv)
        acc = sc[:, 0:1] * d_ref[0:1, :]
        for b in range(1, B):
            acc += sc[:, b:b + 1] * d_ref[b:b + 1, :]
        o_ref[...] = acc

    return pl.pallas_call(
        body, name="ada_grad", grid=(L, N // tn),
        in_specs=[pl.BlockSpec((D, B), lambda l, j: (0, 0)), pl.BlockSpec((None, B, tn), lambda l, j: (l, 0, j))],
        out_specs=pl.BlockSpec((None, D, tn), lambda l, j: (l, 0, j)),
        out_shape=SDS((L, D, N), F32), compiler_params=_params("parallel", "parallel"),
    )(c_all_t, dmod)


def _adamw(w, g, m, v):
    R, C = w.shape
    tr = _row_tile(R, C * 4)

    def body(w_ref, g_ref, m_ref, v_ref, d_ref, mo_ref, vo_ref):
        d_ref[...], mo_ref[...], vo_ref[...] = _adam_math(w_ref[...], g_ref[...], m_ref[...], v_ref[...])

    tile = pl.BlockSpec((tr, C), lambda i: (i, 0))
    return pl.pallas_call(
        body, name=f"adamw_r{R}c{C}", grid=(R // tr,), in_specs=[tile] * 4, out_specs=[tile] * 3,
        out_shape=[SDS((R, C), F32)] * 3, compiler_params=_params("parallel"),
    )(w, g, m, v)


def _adam_math(w, g, m, v):
    mn = ADAM_B1 * m + (1.0 - ADAM_B1) * g
    vn = ADAM_B2 * v + (1.0 - ADAM_B2) * (g * g)
    m_hat = mn / (1.0 - ADAM_B1 ** ADAM_STEP)
    v_hat = vn / (1.0 - ADAM_B2 ** ADAM_STEP)
    return -ADAM_LR * (m_hat / (jnp.sqrt(v_hat) + ADAM_EPS) + ADAM_WD * w), mn, vn


def _adamw_layer(w, g, m, v, l, prev, after):
    L, R, C = w.shape
    tr = _row_tile(R, C * 4)
    prev = (after,) + (() if prev is None else tuple(prev))

    def body(w_ref, g_ref, m_ref, v_ref, *rest):
        go_ref, d_ref, mo_ref, vo_ref = rest[len(prev):]
        gv = g_ref[...]
        go_ref[...] = gv
        d_ref[...], mo_ref[...], vo_ref[...] = _adam_math(w_ref[...], gv, m_ref[...], v_ref[...])

    lay = pl.BlockSpec((None, tr, C), lambda i: (l, i, 0))
    return pl.pallas_call(
        body, name=f"adamw_layer_r{R}c{C}", grid=(R // tr,),
        in_specs=[lay, pl.BlockSpec((tr, C), lambda i: (i, 0)), lay, lay] + [ANY] * len(prev), out_specs=[lay] * 4,
        out_shape=[SDS((L, R, C), F32)] * 4, input_output_aliases={5 + k: k for k in range(len(prev) - 1)},
        compiler_params=_params("parallel"),
    )(w, g, m, v, *prev)


def _adamw_small(ws, gs, ms, vs):
    n = len(ws)

    def body(*refs):
        outs = refs[4 * n:]
        for k in range(n):
            outs[k][...], outs[n + k][...], outs[2 * n + k][...] = _adam_math(
                refs[k][...], refs[n + k][...], refs[2 * n + k][...], refs[3 * n + k][...])

    whole = pl.BlockSpec(memory_space=pltpu.VMEM)
    return pl.pallas_call(
        body, name="adamw_small", in_specs=[whole] * (4 * n), out_specs=[whole] * (3 * n),
        out_shape=[SDS(w.shape, F32) for w in ws] * 3, compiler_params=pltpu.CompilerParams(vmem_limit_bytes=VMEM_LIMIT_BYTES),
    )(*ws, *gs, *ms, *vs)


def _sum_leading(a):
    P, R, C = a.shape
    tr = _row_tile(R, P * C * 4)

    def body(a_ref, o_ref):
        acc = a_ref[0]
        for p in range(1, P):
            acc = acc + a_ref[p]
        o_ref[...] = acc

    return pl.pallas_call(
        body, name=f"sum{P}_r{R}c{C}", grid=(R // tr,),
        in_specs=[pl.BlockSpec((P, tr, C), lambda i: (0, i, 0))],
        out_specs=pl.BlockSpec((tr, C), lambda i: (i, 0)),
        out_shape=SDS((R, C), F32), compiler_params=_params("parallel"),
    )(a)


def _add_half(g4, r1, place):
    _, _, R, C = g4.shape
    tr = _row_tile(R, C * 4)

    def body(place_ref, g_ref, r_ref, h_ref, own_ref):
        s = (g_ref[...] + r_ref[...]).astype(XFER_DTYPE)
        h_ref[...] = s

        @pl.when(pl.program_id(1) == place_ref[1])
        def _():
            own_ref[...] = s

    return pl.pallas_call(
        body, name=f"add_half_r{R}c{C}",
        grid_spec=pltpu.PrefetchScalarGridSpec(
            num_scalar_prefetch=1, grid=(R // tr, N_CHIPS),
            in_specs=[pl.BlockSpec((None, None, tr, C), lambda i, p, place_ref: (p, place_ref[0], i, 0)),
                      pl.BlockSpec((None, tr, C), lambda i, p, place_ref: (p, i, 0))],
            out_specs=[pl.BlockSpec((None, tr, C), lambda i, p, place_ref: (p, i, 0)),
                       pl.BlockSpec((None, tr, C), lambda i, p, place_ref: (place_ref[1], i, 0))],
        ),
        out_shape=[SDS((N_CHIPS, R, C), XFER_DTYPE)] * 2, compiler_params=_params("parallel", "arbitrary"),
    )(place, g4, r1)


def _sum4_into_half(r2, place):
    P, R, C = r2.shape
    tr = _row_tile(R, P * C * 4)

    def body(place_ref, a_ref, o_ref):
        acc = a_ref[0].astype(F32)
        for p in range(1, P):
            acc = acc + a_ref[p].astype(F32)
        o_ref[...] = acc

    return pl.pallas_call(
        body, name=f"sum4_r{R}c{C}",
        grid_spec=pltpu.PrefetchScalarGridSpec(
            num_scalar_prefetch=1, grid=(R // tr,),
            in_specs=[pl.BlockSpec((P, tr, C), lambda i, place_ref: (0, i, 0))],
            out_specs=pl.BlockSpec((None, tr, C), lambda i, place_ref: (place_ref[0], i, 0)),
        ),
        out_shape=SDS((2, R, C), F32), compiler_params=_params("parallel"),
    )(place, r2)


def _cast_into_slot(w, l, place, after):
    _, R, C = w.shape
    tr = _row_tile(R, C * 4)

    def body(place_ref, w_ref, after_ref, o_ref):
        o_ref[...] = w_ref[...].astype(MXU_DTYPE)

    return pl.pallas_call(
        body, name=f"cast_r{R}c{C}",
        grid_spec=pltpu.PrefetchScalarGridSpec(
            num_scalar_prefetch=1, grid=(R // tr,),
            in_specs=[pl.BlockSpec((None, tr, C), lambda i, place_ref: (l, i, 0)), ANY],
            out_specs=pl.BlockSpec((None, tr, C), lambda i, place_ref: (place_ref[1], i, 0)),
        ),
        out_shape=SDS((N_CHIPS, R, C), MXU_DTYPE), compiler_params=_params("parallel"),
    )(place, w, after)


def _place():
    x, y, c = lax.axis_index("x"), lax.axis_index("y"), lax.axis_index("c")
    chips = [(1 - x, y), (x, 1 - y), (1 - x, 1 - y)]
    return x, y, c, chips


def _remote(src, dst, send_sem, recv_sem, to):
    return pltpu.make_async_remote_copy(src_ref=src, dst_ref=dst, send_sem=send_sem, recv_sem=recv_sem, device_id=to, device_id_type=MESH)


def _all_gather8(v):
    R, N = v.shape

    def body(v_ref, out_ref, send_sems, recv_sems, local_sem):
        x, y, c, chips = _place()
        me, sibling = (x, y, c), (x, y, 1 - c)

        def slot(px, py, pc):
            return out_ref.at[4 * px + 2 * py + pc]

        def copy(k, block, to, src=None):
            return _remote(slot(*block) if src is None else src, slot(*block), send_sems.at[k], recv_sems.at[k], to)

        mine = pltpu.make_async_copy(v_ref, slot(*me), local_sem)
        mine.start()
        first = [copy(0, me, sibling, src=v_ref)] + [copy(1 + j, me, (*chip, c), src=v_ref) for j, chip in enumerate(chips)]
        for cp in first:
            cp.start()
        passed = [copy(4 + j, (*chip, c), sibling) for j, chip in enumerate(chips)]
        for j, chip in enumerate(chips):
            copy(1 + j, (*chip, c), me).wait_recv()
            passed[j].start()
        copy(0, sibling, me).wait_recv()
        for j, chip in enumerate(chips):
            copy(4 + j, (*chip, 1 - c), me).wait_recv()
        for cp in first + passed:
            cp.wait_send()
        mine.wait()

    return pl.pallas_call(
        body, name=f"all_gather8_r{R}n{N}", out_shape=SDS((N_DEV, R, N), v.dtype), in_specs=[ANY], out_specs=ANY,
        scratch_shapes=[pltpu.SemaphoreType.DMA((7,)), pltpu.SemaphoreType.DMA((7,)), pltpu.SemaphoreType.DMA],
    )(v)


HBM_SPEC = pl.BlockSpec(memory_space=pltpu.HBM)
SEM_SPEC = pl.BlockSpec(memory_space=pltpu.SEMAPHORE)
DATAFLOW = pltpu.SideEffectType.DATAFLOW_SIDE_EFFECTING


def _in_hbm(a):
    return pltpu.with_memory_space_constraint(a, pltpu.HBM)


def _hbm_like(a):
    return pltpu.HBM(a.shape, a.dtype)


def _split_start(name, arrays, n_sems, issue, extra=()):
    m = len(arrays)

    def body(*refs):
        issue(refs[:m], refs[m + len(extra)], refs[m + len(extra) + 1])

    out = pl.pallas_call(
        body, name=name,
        out_shape=(pltpu.SemaphoreType.DMA((n_sems,)), pltpu.SemaphoreType.DMA((n_sems,)), *[_hbm_like(a) for a in arrays]),
        in_specs=[HBM_SPEC] * m + [ANY] * len(extra), out_specs=(SEM_SPEC, SEM_SPEC, *[HBM_SPEC] * m),
        input_output_aliases={k: 2 + k for k in range(m)},
        compiler_params=pltpu.CompilerParams(has_side_effects=DATAFLOW),
    )(*[_in_hbm(a) for a in arrays], *extra)
    return out[0], out[1], list(out[2:])


def _split_wait(name, send_sems, recv_sems, arrays, after, drain):
    m = len(arrays)

    def body(*refs):
        drain(refs[:m], refs[m], refs[m + 1])

    out = pl.pallas_call(
        body, name=name, out_shape=[_hbm_like(a) for a in arrays],
        in_specs=[HBM_SPEC] * m + [SEM_SPEC, SEM_SPEC, ANY], out_specs=[HBM_SPEC] * m,
        input_output_aliases={k: k for k in range(m)},
        compiler_params=pltpu.CompilerParams(has_side_effects=DATAFLOW),
    )(*arrays, send_sems, recv_sems, after)
    return list(out)


def _wait_both(cp):
    cp.wait_send()
    cp.wait_recv()


class _Flight:
    def __init__(self, name, arrays, n_sems, issue, drain, thru, extra=()):
        self.name, self.drain, self.n = name, drain, len(arrays)
        self.send, self.recv, out = _split_start(name + "_start", [*arrays, thru], n_sems, issue, extra)
        self.arrays, self.thru = out[:-1], out[-1]

    def land(self, after):
        return _split_wait(self.name + "_wait", self.send, self.recv, self.arrays, after, self.drain)


def _gather_flight(tag, ici, d2d, direct, thru):
    kinds = ["ici"] * len(ici) + ["d2d"] * len(d2d) + ["direct"] * len(direct)

    def issue(refs, send_sems, recv_sems):
        x, y, c, chips = _place()
        q = 2 * x + y
        for k, kind in enumerate(kinds):
            for j, chip in enumerate(chips):
                if kind == "ici":
                    src, to = refs[k].at[q, c], (*chip, c)
                elif kind == "d2d":
                    src, to = refs[k].at[2 * chip[0] + chip[1], c], (x, y, 1 - c)
                else:
                    src, to = refs[k].at[q], (*chip, c)
                _remote(src, src, send_sems.at[3 * k + j], recv_sems.at[3 * k + j], to).start()

    def drain(refs, send_sems, recv_sems):
        x, y, c, chips = _place()
        for k, kind in enumerate(kinds):
            for j, chip in enumerate(chips):
                p = 2 * chip[0] + chip[1]
                got = refs[k].at[p] if kind == "direct" else refs[k].at[p, c if kind == "ici" else 1 - c]
                _wait_both(_remote(got, got, send_sems.at[3 * k + j], recv_sems.at[3 * k + j], (x, y, c)))

    return _Flight(f"gather{tag}", [*ici, *d2d, *direct], 3 * len(kinds), issue, drain, thru)


def _swap_flight(tag, g4s, thru):
    n = len(g4s)
    zones = [lax.empty((N_CHIPS,) + g.shape[2:], g.dtype) for g in g4s]

    def issue(refs, send_sems, recv_sems):
        x, y, c, _ = _place()
        for k in range(n):
            for p in range(N_CHIPS):
                _remote(refs[k].at[p, 1 - c], refs[n + k].at[p], send_sems.at[N_CHIPS * k + p], recv_sems.at[N_CHIPS * k + p], (x, y, 1 - c)).start()

    def drain(refs, send_sems, recv_sems):
        x, y, c, _ = _place()
        for k in range(n):
            for p in range(N_CHIPS):
                got = refs[n + k].at[p]
                _wait_both(_remote(got, got, send_sems.at[N_CHIPS * k + p], recv_sems.at[N_CHIPS * k + p], (x, y, c)))

    return _Flight(f"swap{tag}", [*g4s, *zones], N_CHIPS * n, issue, drain, thru)


def _scatter_flight(tag, hs, lands, thru):
    n = len(hs)

    def issue(refs, send_sems, recv_sems):
        x, y, c, chips = _place()
        q = 2 * x + y
        for k in range(n):
            for j, chip in enumerate(chips):
                _remote(refs[k].at[2 * chip[0] + chip[1]], refs[n + k].at[q], send_sems.at[3 * k + j], recv_sems.at[3 * k + j], (*chip, c)).start()

    def drain(refs, send_sems, recv_sems):
        x, y, c, chips = _place()
        for k in range(n):
            for j, chip in enumerate(chips):
                got = refs[n + k].at[2 * chip[0] + chip[1]]
                _wait_both(_remote(got, got, send_sems.at[3 * k + j], recv_sems.at[3 * k + j], (x, y, c)))

    return _Flight(f"scatter{tag}", [*hs, *lands], 3 * n, issue, drain, thru)


def _exchange_flight(tag, buf, thru):
    flips = [(fx, fy, fc) for fx in (0, 1) for fy in (0, 1) for fc in (0, 1)][1:]

    def peers():
        x, y, c, _ = _place()
        return (x, y, c), [((1 - x) if fx else x, (1 - y) if fy else y, (1 - c) if fc else c) for fx, fy, fc in flips]

    def slot(ref, dev):
        return ref.at[4 * dev[0] + 2 * dev[1] + dev[2]]

    def issue(refs, send_sems, recv_sems):
        me, others = peers()
        for j, to in enumerate(others):
            _remote(slot(refs[0], me), slot(refs[0], me), send_sems.at[j], recv_sems.at[j], to).start()

    def drain(refs, send_sems, recv_sems):
        me, others = peers()
        for j, frm in enumerate(others):
            got = slot(refs[0], frm)
            _wait_both(_remote(got, got, send_sems.at[j], recv_sems.at[j], me))

    return _Flight(f"exchange{tag}", [buf], len(flips), issue, drain, thru)


def _share_flight(tag, fins, thru):
    n = len(fins)

    def issue(refs, send_sems, recv_sems):
        x, y, c, _ = _place()
        for k in range(n):
            _remote(refs[k].at[c], refs[k].at[c], send_sems.at[k], recv_sems.at[k], (x, y, 1 - c)).start()

    def drain(refs, send_sems, recv_sems):
        x, y, c, _ = _place()
        for k in range(n):
            got = refs[k].at[1 - c]
            _wait_both(_remote(got, got, send_sems.at[k], recv_sems.at[k], (x, y, c)))

    return _Flight(f"share{tag}", fins, n, issue, drain, thru)


def _pair_blocks(w):
    h, d, _ = w.shape
    z = jnp.zeros((h // 2, d, d), w.dtype)
    return jnp.concatenate([jnp.concatenate([w[0::2], z], axis=2), jnp.concatenate([z, w[1::2]], axis=2)], axis=1)


def _unpair_blocks(b):
    n, dd, _ = b.shape
    d = dd // 2
    return jnp.stack([b[:, :d, :d], b[:, d:, d:]], axis=1).reshape(2 * n, d, d)


def _pad_rows(a, rows):
    return jnp.pad(a, ((0, rows - a.shape[0]), (0, 0)))


class _Packer:
    def __init__(self, shapes, width=1024, row_multiple=64):
        self.shapes = shapes
        self.sizes = [math.prod(s) for s in shapes]
        total = sum(self.sizes)
        self.width = width
        self.rows = -(-total // (width * row_multiple)) * row_multiple
        self.pad = self.rows * width - total

    def pack(self, arrays):
        flat = jnp.concatenate([a.reshape(-1).astype(F32) for a in arrays] + [jnp.zeros((self.pad,), F32)])
        return flat.reshape(self.rows, self.width)

    def unpack(self, packed):
        flat = packed.reshape(-1)
        out, off = [], 0
        for s, n in zip(self.shapes, self.sizes):
            out.append(flat[off:off + n].reshape(s))
            off += n
        return out


SMALL = ["b_ada", "ffn1_norm", "mix_norm", "conv_w", "conv_b", "gate_a_w", "gate_a_b", "gate_x_w", "gate_x_b", "lru_lambda",
         "v_norm", "spatial_w", "spatial_b", "lru_out_norm", "gmlp_out_norm", "ffn2_norm", "final_norm"]
BIG = ["ffn1_w_gu", "ffn1_w_down", "w_in", "w_out", "ffn2_w_gu", "ffn2_w_down"]
GROUPS = (("ffn1_w_gu", "ffn1_w_down"), ("w_in", "w_out"), ("ffn2_w_gu", "ffn2_w_down"))
FWD_GROUPS = (("ffn1_w_gu",), ("ffn1_w_down",), ("w_in", "w_out"), ("ffn2_w_gu",), ("ffn2_w_down",))
MIN_AGE = {"swap": 1, "scatter": 1, "share": 1}
WEIGHTS = ["w_ada", "b_ada", "ffn1_norm", "ffn1_w_gu", "ffn1_w_down", "mix_norm", "w_in", "conv_w", "conv_b", "gate_a_w", "gate_a_b",
           "gate_x_w", "gate_x_b", "lru_lambda", "v_norm", "spatial_w", "spatial_b", "lru_out_norm", "gmlp_out_norm", "w_out",
           "ffn2_norm", "ffn2_w_gu", "ffn2_w_down", "final_norm"]


def kernel(x, c, w_ada, b_ada, ffn1_norm, ffn1_w_gu, ffn1_w_down, mix_norm, w_in, conv_w, conv_b, gate_a_w, gate_a_b, gate_x_w, gate_x_b, lru_lambda, v_norm, spatial_w, spatial_b, lru_out_norm, gmlp_out_norm, w_out, ffn2_norm, ffn2_w_gu, ffn2_w_down, final_norm, loss_target, m_w_ada, m_b_ada, m_ffn1_norm, m_ffn1_w_gu, m_ffn1_w_down, m_mix_norm, m_w_in, m_conv_w, m_conv_b, m_gate_a_w, m_gate_a_b, m_gate_x_w, m_gate_x_b, m_lru_lambda, m_v_norm, m_spatial_w, m_spatial_b, m_lru_out_norm, m_gmlp_out_norm, m_w_out, m_ffn2_norm, m_ffn2_w_gu, m_ffn2_w_down, m_final_norm, v_w_ada, v_b_ada, v_ffn1_norm, v_ffn1_w_gu, v_ffn1_w_down, v_mix_norm, v_w_in, v_conv_w, v_conv_b, v_gate_a_w, v_gate_a_b, v_gate_x_w, v_gate_x_b, v_lru_lambda, v_v_norm, v_spatial_w, v_spatial_b, v_lru_out_norm, v_gmlp_out_norm, v_w_out, v_ffn2_norm, v_ffn2_w_gu, v_ffn2_w_down, v_final_norm):
    given = dict(locals())
    W = {n: given[n] for n in WEIGHTS}
    L = w_ada.shape[0]
    S, D = x.shape[1], x.shape[2]
    LW = conv_b.shape[1]
    hd = LW // HEADS
    xi, yi, ci = lax.axis_index("x"), lax.axis_index("y"), lax.axis_index("c")
    chip = 2 * xi + yi
    dev = 2 * chip + ci
    place = jnp.stack([ci, chip]).astype(jnp.int32)
    xs = x.reshape(S, D)
    tgt = loss_target.reshape(S, D)

    c_all = _all_gather8(_pad_rows(c, SUBLANES))[:, 0, :]
    n_ada = w_ada.shape[2]
    b_shard = lax.dynamic_slice_in_dim(b_ada, chip * n_ada, n_ada, axis=1)
    mod_shard = _ada_fwd(_pad_rows(c_all, 2 * SUBLANES), w_ada, b_shard[:, None, :])

    def in_slot(block):
        return lax.dynamic_update_index_in_dim(jnp.zeros((N_CHIPS,) + block.shape, block.dtype), block, chip, 0)

    cws = LW // N_CHIPS
    small = [in_slot(mod_shard.reshape(L * 2 * SUBLANES, n_ada)), in_slot(conv_w.reshape(L * CONV_WIDTH, cws))]

    def half_view(s):
        return s.reshape(N_CHIPS, 2, s.shape[1] // 2, s.shape[2])

    stages = [(l, names) for l in range(L) for names in FWD_GROUPS]
    seq = [[half_view(_cast_into_slot(W[n], l, place, place)) for n in names] for l, names in stages[:1]]
    flights = {}

    def launch(t, thru, direct=()):
        ici = seq[t] if t < len(seq) else []
        d2d = seq[t - 1] if 1 <= t <= len(seq) else []
        if ici or d2d or direct:
            flights[t] = _gather_flight(t, ici, d2d, list(direct), thru)
            thru = flights[t].thru
        return thru

    def land(t, after):
        if t not in flights:
            return []
        out = flights.pop(t).land(after)
        ni = len(seq[t]) if t < len(seq) else 0
        nd = len(seq[t - 1]) if 1 <= t <= len(seq) else 0
        if ni:
            seq[t] = out[:ni]
        if nd:
            seq[t - 1] = out[ni:ni + nd]
        return out[ni + nd:]

    def group_weights(t):
        return [s.reshape(N_CHIPS, -1, s.shape[3]) for s in seq[t]]

    c_all = launch(0, c_all, small)
    seq += [[half_view(_cast_into_slot(W[n], l, place, c_all)) for n in names] for l, names in stages[1:]]
    mod_all, conv_all = land(0, seq[-1][-1])
    mod_all = launch(1, mod_all)
    land(1, mod_all)
    mod_rows = lax.dynamic_index_in_dim(mod_all.reshape(N_CHIPS, L, 2 * SUBLANES, n_ada), dev, axis=2, keepdims=False)
    mod = mod_rows.transpose(1, 0, 2).reshape(L, N_MOD, 1, D)
    conv_full = conv_all.reshape(N_CHIPS, L, CONV_WIDTH, cws).transpose(1, 2, 0, 3).reshape(L, CONV_WIDTH, LW)

    tril = jnp.tril(jnp.ones((CHUNK, CHUNK), F32))
    seg = (jnp.arange(LW)[:, None] // hd == jnp.arange(LW)[None, :] // hd).astype(jnp.bfloat16)

    def mixer_params(l):
        ws = spatial_w[l] * tril
        wsp = jnp.concatenate([ws[0::2], ws[1::2]], axis=2)
        wa, wx = _pair_blocks(gate_a_w[l]), _pair_blocks(gate_x_w[l])
        return dict(
            cw=conv_full[l], cb=conv_b[l][None],
            wa=wa.astype(MXU_DTYPE), wx=wx.astype(MXU_DTYPE), wat=wa.transpose(0, 2, 1).astype(MXU_DTYPE), wxt=wx.transpose(0, 2, 1).astype(MXU_DTYPE),
            ba=gate_a_b[l].reshape(1, LW), bx=gate_x_b[l].reshape(1, LW), lam=lru_lambda[l][None], gv=v_norm[l][None],
            wsp=wsp.astype(MXU_DTYPE), wspt=wsp.transpose(0, 2, 1).astype(MXU_DTYPE),
            bfull=jnp.repeat(spatial_b[l].T, hd, axis=1), g_lru=lru_out_norm[l][None], g_gm=gmlp_out_norm[l][None])

    saved = []
    xcur = xs
    zero_row = jnp.zeros((1, D), F32)
    h = _modnorm(xcur, ffn1_norm[0][None], mod[0][0], mod[0][1])
    for l in range(L):
        mp, md = mixer_params(l), mod[l]
        s = dict(lw={}, mp=mp, md=md)
        lw = s["lw"]
        t = len(FWD_GROUPS) * l
        s["x0"] = xcur
        s["h1"] = launch(t + 2, h)
        lw["gu1"], = group_weights(t)
        s["a1"], s["gu1"] = _ffn_up(s["h1"], lw["gu1"])
        land(t + 2, s["a1"])
        s["a1"] = launch(t + 3, s["a1"])
        lw["d1"] = group_weights(t + 1)[0].reshape(-1, D)
        s["f1"], xcur, h = _mm_res(s["a1"], lw["d1"], xcur, md[2], 0.5, (mix_norm[l][None], md[3], md[4]))
        land(t + 3, xcur)
        s["x1"] = xcur
        s["h2"] = launch(t + 4, h)
        lw["win"], wout = group_weights(t + 2)
        lw["wout"] = wout.reshape(-1, D)
        s["proj"] = _mm_chunks(s["h2"], lw["win"])
        s["ylru"] = _lru_fwd(s["proj"], mp["cw"], mp["cb"], mp["wa"], mp["ba"], mp["wx"], mp["bx"], mp["lam"])
        s["yn"], s["ygm"] = _gmlp_fwd(s["proj"], s["ylru"], mp["gv"], seg, mp["wsp"], mp["bfull"], mp["g_lru"], mp["g_gm"])
        s["f2"], xcur, h = _mm_res(s["yn"], lw["wout"], xcur, md[5], 1.0, (ffn2_norm[l][None], md[6], md[7]))
        land(t + 4, xcur)
        s["x2"] = xcur
        s["h3"] = launch(t + 5, h)
        lw["gu2"], = group_weights(t + 3)
        s["a3"], s["gu3"] = _ffn_up(s["h3"], lw["gu2"])
        land(t + 5, s["a3"])
        s["a3"] = launch(t + 6, s["a3"])
        lw["d2"] = group_weights(t + 4)[0].reshape(-1, D)
        following = (ffn1_norm[l + 1][None], mod[l + 1][0], mod[l + 1][1]) if l + 1 < L else (final_norm[None], zero_row, zero_row)
        s["f3"], xcur, h = _mm_res(s["a3"], lw["d2"], xcur, md[8], 0.5, following)
        land(t + 6, xcur)
        saved.append(s)

    dx, dq, head_acc = _loss_head(xcur, tgt, final_norm[None], saved[-1]["md"][8], 0.5)
    loss = lax.psum(jnp.sum(head_acc[1]), ("x", "y", "c"))
    big_grads = {n: [None] * L for n in BIG}
    dmods = [None] * L

    def ffn_bwd(names, l, dx, dq, x_in, h, a, gu, f, wgu, wd, gn, sc, next_gate, next_scale):
        big_grads[names[1]][l] = _mm_tn_chunks(a, dq[None], 1408, 1024)[0].reshape(N_CHIPS, -1, D)
        dgu = _ffn_bwd_act(dq, wd, gu)
        C = dgu.shape[3]
        dgu4 = dgu.reshape(N_CHIPS, S, C)
        big_grads[names[0]][l] = _mm_tn_chunks(h, dgu4, 1024, C)
        dgu4 = reduce_group(names, l, big_grads[names[0]][l], dgu4)
        dx, dq, acc = _mm_nt_norm_bwd(dgu4, wgu, x_in, dx, f, gn, sc, 0.5, next_gate, next_scale)
        return dx, move_on(dx, dq), acc

    stepped = {n: None for n in BIG}
    reducing = []

    per_layer = [n for n in SMALL if n not in ("b_ada", "final_norm")]
    packers, exchanges = {}, {}
    clock = [0]
    to_step = []

    def step_reduced(after):
        while to_step:
            name, l, g = to_step.pop(0)
            stepped[name] = _adamw_layer(W[name], g, given["m_" + name], given["v_" + name], l, stepped[name], after)
            after = stepped[name][1]
        return after

    def move_on(after, thru, force=False):
        clock[0] += 1
        for grp in list(reducing):
            if not force and clock[0] - grp["since"] < MIN_AGE[grp["step"]]:
                continue
            grp["since"] = clock[0]
            landed = grp["flight"].land(after)
            n = len(grp["names"])
            if grp["step"] == "swap":
                pairs = [_add_half(g4, r1, place) for g4, r1 in zip(landed[:n], landed[n:])]
                grp.update(step="scatter", flight=_scatter_flight(grp["tag"], [h for h, _ in pairs], [own for _, own in pairs], thru))
            elif grp["step"] == "scatter":
                grp.update(step="share", flight=_share_flight(grp["tag"], [_sum4_into_half(r2, place) for r2 in landed[n:]], thru))
            else:
                to_step.extend((name, grp["l"], fin.reshape(2 * fin.shape[1], fin.shape[2])) for name, fin in zip(grp["names"], landed))
                reducing.remove(grp)
                continue
            thru = grp["flight"].thru
        return thru

    def reduce_group(names, l, after, thru):
        thru = move_on(after, thru)
        g4s = [big_grads[n][l].reshape(N_CHIPS, 2, big_grads[n][l].shape[1] // 2, big_grads[n][l].shape[2]) for n in names]
        tag = f"{l}{GROUPS.index(names)}"
        reducing.append(dict(names=names, l=l, tag=tag, step="swap", since=clock[0], flight=_swap_flight(tag, g4s, thru)))
        return reducing[-1]["flight"].thru
    for l in reversed(range(L)):
        s = saved[l]
        lw, mp, md = s["lw"], s["mp"], s["md"]
        dx, dq, acc3 = ffn_bwd(
            GROUPS[2], l, dx, dq, s["x2"], s["h3"], s["a3"], s["gu3"], s["f3"], lw["gu2"], lw["d2"], ffn2_norm[l][None], md[7], md[5], 1.0)
        big_grads["w_out"][l] = _mm_tn_chunks(s["yn"], dq[None], 1024, 1024)[0].reshape(N_CHIPS, -1, D)
        dyn = _mm_nt_chunks(dq[None], lw["wout"][None])
        dylru, duv, dwsp, dbfull, gacc = _gmlp_bwd(s["proj"], s["ylru"], s["ygm"], dyn, mp["gv"], seg, mp["wsp"], mp["wspt"], mp["bfull"], mp["g_lru"], mp["g_gm"])
        dxg, dwa, dwx, lvec = _lru_bwd(s["proj"], dylru, mp["cw"], mp["cb"], mp["wa"], mp["ba"], mp["wx"], mp["bx"], mp["lam"], mp["wat"], mp["wxt"])
        dproj = jnp.concatenate([dxg, duv], axis=0)
        big_grads["w_in"][l] = _mm_tn_chunks(s["h2"], dproj, 1024, LW)
        dproj = reduce_group(GROUPS[1], l, big_grads["w_in"][l], dproj)
        dx, dq, acc2 = _mm_nt_norm_bwd(dproj, lw["win"], s["x1"], dx, s["f2"], mix_norm[l][None], md[4], 1.0, md[2], 0.5)
        dq = move_on(dx, dq)
        if l > 0:
            ng, ns = saved[l - 1]["md"][8], 0.5
        else:
            ng, ns = zero_row, 0.0
        dx, dq, acc1 = ffn_bwd(
            GROUPS[0], l, dx, dq, s["x0"], s["h1"], s["a1"], s["gu1"], s["f1"], lw["gu1"], lw["d1"], ffn1_norm[l][None], md[1], ng, ns)

        dmods[l] = jnp.concatenate([acc1[0:2], acc1[3:4], acc2[0:2], acc2[3:4], acc3[0:2], acc3[3:4]], axis=0)
        dws = jnp.stack([dwsp[:, :, :CHUNK], dwsp[:, :, CHUNK:]], axis=1).reshape(HEADS, CHUNK, CHUNK) * tril
        lg = {"ffn1_norm": acc1[2], "mix_norm": acc2[2], "ffn2_norm": acc3[2],
              "conv_w": lvec[4:8], "conv_b": lvec[3], "gate_a_w": _unpair_blocks(dwa), "gate_a_b": lvec[0].reshape(HEADS, hd),
              "gate_x_w": _unpair_blocks(dwx), "gate_x_b": lvec[1].reshape(HEADS, hd), "lru_lambda": lvec[2], "v_norm": gacc[2],
              "spatial_w": dws, "spatial_b": dbfull.reshape(CHUNK, HEADS, hd).sum(-1).T, "lru_out_norm": gacc[0], "gmlp_out_norm": gacc[1]}
        part = [lg[n] for n in per_layer] + [dmods[l]] + ([head_acc[0]] if l == L - 1 else [])
        packers[l] = _Packer([p.shape for p in part])
        packed = packers[l].pack(part)
        exchanges[l] = _exchange_flight(l, lax.dynamic_update_index_in_dim(jnp.zeros((N_DEV,) + packed.shape, F32), packed, dev, 0), dq)
        dq = exchanges[l].thru

    grad_x = dx.reshape(x.shape)

    done = step_reduced(dq)
    while reducing:
        dq = move_on(done, dq, force=True)
        done = step_reduced(dq)
    summed, dmod_rows = [], []
    for l in range(L):
        gathered, = exchanges[l].land(done)
        summed.append(packers[l].unpack(_sum_leading(gathered)))
        off = sum(packers[l].sizes[:len(per_layer)])
        dmod_rows.append(gathered.reshape(N_DEV, -1)[:, off:off + N_MOD * D])
    grads = {n: jnp.stack([summed[l][k] for l in range(L)]) for k, n in enumerate(per_layer)}
    grads["final_norm"] = summed[L - 1][len(per_layer) + 1]
    grads["b_ada"] = jnp.stack([summed[l][len(per_layer)].reshape(N_MOD * D) for l in range(L)])
    dmod_shard = lax.dynamic_slice_in_dim(jnp.stack(dmod_rows), chip * n_ada, n_ada, axis=2)
    grads["w_ada"] = _ada_grad(c_all.T, dmod_shard)
    grads["conv_w"] = lax.dynamic_slice_in_dim(grads["conv_w"], chip * cws, cws, axis=2)

    delta, new_m, new_v = {}, {}, {}
    for n in BIG:
        grads[n], delta[n], new_m[n], new_v[n] = stepped[n]
    shp = w_ada.shape
    d_, m_, v_ = _adamw(*[a.reshape(-1, shp[-1]) for a in (w_ada, grads["w_ada"], m_w_ada, v_w_ada)])
    delta["w_ada"], new_m["w_ada"], new_v["w_ada"] = d_.reshape(shp), m_.reshape(shp), v_.reshape(shp)
    def rows_of(a):
        return a.reshape(-1, a.shape[-1])

    stepped_small = _adamw_small(*[[rows_of(src[n].reshape(W[n].shape)) for n in SMALL]
                                   for src in (W, grads, {n: given["m_" + n] for n in SMALL}, {n: given["v_" + n] for n in SMALL})])
    for k, n in enumerate(SMALL):
        delta[n], new_m[n], new_v[n] = (stepped_small[i * len(SMALL) + k].reshape(W[n].shape) for i in range(3))
    grads = {n: grads[n].reshape(W[n].shape) for n in WEIGHTS}
    return (loss, grad_x, *[grads[n] for n in WEIGHTS], *[delta[n] for n in WEIGHTS], *[new_m[n] for n in WEIGHTS], *[new_v[n] for n in WEIGHTS])
```

```python
import math

import jax
import jax.numpy as jnp
from jax import lax
from jax.experimental import pallas as pl
from jax.experimental.pallas import tpu as pltpu

F32 = jnp.float32
MXU_DTYPE = jnp.bfloat16
ACT_DTYPE = jnp.bfloat16
XFER_DTYPE = jnp.bfloat16
EPS = 1e-6
RG_LRU_C = 8.0
N_MOD = 9
CONV_WIDTH = 4
HEADS = 8
CHUNK = 128
LANES = 128
SUBLANES = 8
N_CHIPS = 4
N_DEV = 8
ADAM_LR, ADAM_B1, ADAM_B2, ADAM_EPS, ADAM_WD, ADAM_STEP = 0.001, 0.9, 0.999, 1e-08, 0.01, 10
VMEM_LIMIT_BYTES = 60 * 1024 * 1024
ROW_TILE_BYTES = 2 << 20
GELU_C = math.sqrt(2.0 / math.pi)
GELU_A = 0.044715

ANY = pl.BlockSpec(memory_space=pl.ANY)
MESH = pl.DeviceIdType.MESH
SDS = jax.ShapeDtypeStruct


def _params(*sem):
    return pltpu.CompilerParams(dimension_semantics=sem, vmem_limit_bytes=VMEM_LIMIT_BYTES)


def _dot(a, b):
    return jnp.dot(a.astype(MXU_DTYPE), b.astype(MXU_DTYPE), preferred_element_type=F32)


def _dot_nt(a, b):
    return lax.dot_general(a.astype(MXU_DTYPE), b.astype(MXU_DTYPE), (((1,), (1,)), ((), ())), preferred_element_type=F32)


def _dot_tn(a, b):
    return lax.dot_general(a.astype(MXU_DTYPE), b.astype(MXU_DTYPE), (((0,), (0,)), ((), ())), preferred_element_type=F32)


def _gelu(x):
    return x * (0.5 * (1.0 + jnp.tanh(GELU_C * (x + GELU_A * (x * x * x)))))


def _gelu_grad(x):
    t = jnp.tanh(GELU_C * (x + GELU_A * (x * x * x)))
    return 0.5 * (1.0 + t) + 0.5 * x * (1.0 - t * t) * (GELU_C * (1.0 + 3.0 * GELU_A * x * x))


def _sigmoid(x):
    return jax.nn.sigmoid(x)


def _sigmoid_by_tanh(x):
    return 0.5 * jnp.tanh(0.5 * x) + 0.5


def _rsqrt_ms(x):
    return lax.rsqrt(jnp.mean(x * x, axis=-1, keepdims=True) + EPS)


def _rowsum(x):
    return jnp.sum(x, axis=0, keepdims=True)


def _tile(n, want):
    t = min(n, want)
    assert n % t == 0, (n, want)
    return t


def _row_tile(rows, row_bytes):
    step = 2 * SUBLANES
    cap = max(step, ROW_TILE_BYTES // row_bytes)
    best = None
    for t in range(step, min(rows, cap) + 1, step):
        if rows % t == 0:
            best = t
    assert best is not None, (rows, row_bytes)
    return best


def _modnorm(x, gn, sh, sc):
    S, D = x.shape
    tm = _tile(S, 1024)

    def body(x_ref, gn_ref, sh_ref, sc_ref, h_ref):
        xv = x_ref[...]
        h = (xv * _rsqrt_ms(xv) * gn_ref[...]) * (1.0 + sc_ref[...]) + sh_ref[...]
        h_ref[...] = h.astype(ACT_DTYPE)

    row = pl.BlockSpec((1, D), lambda i: (0, 0))
    return pl.pallas_call(
        body, name="modnorm", grid=(S // tm,),
        in_specs=[pl.BlockSpec((tm, D), lambda i: (i, 0)), row, row, row],
        out_specs=pl.BlockSpec((tm, D), lambda i: (i, 0)),
        out_shape=SDS((S, D), ACT_DTYPE), compiler_params=_params("parallel"),
    )(x, gn, sh, sc)


def _mm_nt_norm_bwd(ac, wc, x, dxo, f, gn, sc, res_scale, next_gate, next_scale):
    P, S, K = ac.shape
    D = x.shape[1]
    tm = _tile(S, 512)

    def body(a_ref, w_ref, x_ref, dxo_ref, f_ref, gn_ref, sc_ref, ng_ref, dx_ref, dq_ref, acc_ref):
        @pl.when(pl.program_id(0) == 0)
        def _():
            acc_ref[...] = jnp.zeros_like(acc_ref)

        dh = _dot_nt(a_ref[0], w_ref[0])
        for p in range(1, P):
            dh += _dot_nt(a_ref[p], w_ref[p])
        xv, dxo = x_ref[...], dxo_ref[...]
        r = _rsqrt_ms(xv)
        xhat = xv * r
        gn = gn_ref[...]
        dn = dh * (1.0 + sc_ref[...])
        dxh = dn * gn
        dx = dxo + r * (dxh - xhat * jnp.mean(dxh * xhat, axis=-1, keepdims=True))
        dx_ref[...] = dx
        dq_ref[...] = ((next_scale * ng_ref[...]) * dx).astype(ACT_DTYPE)
        acc_ref[0:1, :] += _rowsum(dh)
        acc_ref[1:2, :] += _rowsum(dh * (xhat * gn))
        acc_ref[2:3, :] += _rowsum(dn * xhat)
        acc_ref[3:4, :] += _rowsum((res_scale * f_ref[...]) * dxo)

    tile = pl.BlockSpec((tm, D), lambda i: (i, 0))
    row = pl.BlockSpec((1, D), lambda i: (0, 0))
    return pl.pallas_call(
        body, name=f"mm_nt_norm_bwd_k{K}", grid=(S // tm,),
        in_specs=[pl.BlockSpec((P, tm, K), lambda i: (0, i, 0)),
                  pl.BlockSpec((P, D, K), lambda i: (0, 0, 0), pipeline_mode=pl.Buffered(1)),
                  tile, tile, tile, row, row, row],
        out_specs=[tile, tile, pl.BlockSpec((SUBLANES, D), lambda i: (0, 0))],
        out_shape=[SDS((S, D), F32), SDS((S, D), ACT_DTYPE), SDS((SUBLANES, D), F32)],
        compiler_params=_params("arbitrary"),
    )(ac, wc, x, dxo, f, gn, sc, next_gate)


def _loss_head(x, target, gn, next_gate, next_scale):
    S, D = x.shape
    tm = _tile(S, 512)

    def body(x_ref, t_ref, gn_ref, ng_ref, dx_ref, dq_ref, acc_ref):
        @pl.when(pl.program_id(0) == 0)
        def _():
            acc_ref[...] = jnp.zeros_like(acc_ref)

        xv = x_ref[...]
        r = _rsqrt_ms(xv)
        xhat = xv * r
        gn = gn_ref[...]
        err = xhat * gn - t_ref[...]
        dy = err * (1.0 / D)
        dxh = dy * gn
        dx = r * (dxh - xhat * jnp.mean(dxh * xhat, axis=-1, keepdims=True))
        dx_ref[...] = dx
        dq_ref[...] = ((next_scale * ng_ref[...]) * dx).astype(ACT_DTYPE)
        acc_ref[0:1, :] += _rowsum(dy * xhat)
        acc_ref[1:2, :] += _rowsum(err * err) * (0.5 / D)

    tile = pl.BlockSpec((tm, D), lambda i: (i, 0))
    row = pl.BlockSpec((1, D), lambda i: (0, 0))
    return pl.pallas_call(
        body, name="loss_head", grid=(S // tm,),
        in_specs=[tile, tile, row, row],
        out_specs=[tile, tile, pl.BlockSpec((SUBLANES, D), lambda i: (0, 0))],
        out_shape=[SDS((S, D), F32), SDS((S, D), ACT_DTYPE), SDS((SUBLANES, D), F32)],
        compiler_params=_params("arbitrary"),
    )(x, target, gn, next_gate)


def _ffn_up(h, wgu):
    S, D = h.shape
    C = wgu.shape[2]
    tm = _tile(S, 512)

    def body(h_ref, wg_ref, wu_ref, a_ref, gu_ref):
        hv = h_ref[...]
        g = _dot(hv, wg_ref[...])
        u = _dot(hv, wu_ref[...])
        a_ref[...] = (g * _sigmoid_by_tanh(g) * u).astype(ACT_DTYPE)
        gu_ref[0] = g.astype(ACT_DTYPE)
        gu_ref[1] = u.astype(ACT_DTYPE)

    return pl.pallas_call(
        body, name="ffn_up", grid=(2, S // tm),
        in_specs=[
            pl.BlockSpec((tm, D), lambda j, i: (i, 0)),
            pl.BlockSpec((None, D, C), lambda j, i: (j, 0, 0)),
            pl.BlockSpec((None, D, C), lambda j, i: (2 + j, 0, 0)),
        ],
        out_specs=[
            pl.BlockSpec((tm, C), lambda j, i: (i, j)),
            pl.BlockSpec((2, None, tm, C), lambda j, i: (0, j, i, 0)),
        ],
        out_shape=[SDS((S, 2 * C), ACT_DTYPE), SDS((2, 2, S, C), ACT_DTYPE)],
        compiler_params=_params("parallel", "parallel"),
    )(h, wgu, wgu)


def _ffn_bwd_act(dq, wd, gu):
    S, D = dq.shape
    C = gu.shape[3]
    tm = _tile(S, 512)

    def body(dq_ref, wd_ref, gu_ref, dgu_ref):
        da = _dot_nt(dq_ref[...], wd_ref[...])
        g = gu_ref[0].astype(F32)
        u = gu_ref[1].astype(F32)
        s = _sigmoid_by_tanh(g)
        gs = g * s
        dgu_ref[0] = ((da * u) * (s + gs - gs * s)).astype(ACT_DTYPE)
        dgu_ref[1] = (da * gs).astype(ACT_DTYPE)

    gu_spec = pl.BlockSpec((2, None, tm, C), lambda j, i: (0, j, i, 0))
    return pl.pallas_call(
        body, name="ffn_bwd_act", grid=(2, S // tm),
        in_specs=[pl.BlockSpec((tm, D), lambda j, i: (i, 0)), pl.BlockSpec((C, D), lambda j, i: (j, 0)), gu_spec],
        out_specs=gu_spec,
        out_shape=SDS(gu.shape, ACT_DTYPE),
        compiler_params=_params("parallel", "parallel"),
    )(dq, wd, gu)


def _mm_res(a, w, x, gate, scale, following):
    S, K = a.shape
    D = w.shape[1]
    tm = _tile(S, 512)

    def body(a_ref, w_ref, x_ref, g_ref, gn_ref, sh_ref, sc_ref, f_ref, xo_ref, h_ref):
        f = _dot(a_ref[...], w_ref[...])
        f_ref[...] = f
        xo = x_ref[...] + (scale * g_ref[...]) * f
        xo_ref[...] = xo
        h_ref[...] = ((xo * _rsqrt_ms(xo) * gn_ref[...]) * (1.0 + sc_ref[...]) + sh_ref[...]).astype(ACT_DTYPE)

    tile = pl.BlockSpec((tm, D), lambda i: (i, 0))
    row = pl.BlockSpec((1, D), lambda i: (0, 0))
    return pl.pallas_call(
        body, name=f"mm_res_k{K}", grid=(S // tm,),
        in_specs=[pl.BlockSpec((tm, K), lambda i: (i, 0)), pl.BlockSpec((K, D), lambda i: (0, 0)), tile, row, row, row, row],
        out_specs=[tile, tile, tile],
        out_shape=[SDS((S, D), F32), SDS((S, D), F32), SDS((S, D), ACT_DTYPE)],
        compiler_params=_params("parallel"),
    )(a, w, x, gate, *following)


def _mm_chunks(h, wc):
    S, K = h.shape
    P, _, N = wc.shape
    tm = _tile(S, 512)

    def body(h_ref, w_ref, o_ref):
        hv = h_ref[...]
        for p in range(P):
            o_ref[:, p * N:(p + 1) * N] = _dot(hv, w_ref[p])

    return pl.pallas_call(
        body, name="mm_chunks", grid=(S // tm,),
        in_specs=[pl.BlockSpec((tm, K), lambda i: (i, 0)), pl.BlockSpec((P, K, N), lambda i: (0, 0, 0))],
        out_specs=pl.BlockSpec((tm, P * N), lambda i: (i, 0)),
        out_shape=SDS((S, P * N), F32),
        compiler_params=_params("parallel"),
    )(h, wc)


def _mm_nt_chunks(ac, wc):
    P, S, K = ac.shape
    N = wc.shape[1]
    tm, tn = _tile(S, 512), _tile(N, 1024)

    def body(a_ref, w_ref, o_ref):
        acc = _dot_nt(a_ref[0], w_ref[0])
        for p in range(1, P):
            acc += _dot_nt(a_ref[p], w_ref[p])
        o_ref[...] = acc

    return pl.pallas_call(
        body, name=f"mm_nt_p{P}k{K}", grid=(S // tm, N // tn),
        in_specs=[pl.BlockSpec((P, tm, K), lambda i, j: (0, i, 0)), pl.BlockSpec((P, tn, K), lambda i, j: (0, j, 0))],
        out_specs=pl.BlockSpec((tm, tn), lambda i, j: (i, j)),
        out_shape=SDS((S, N), F32),
        compiler_params=_params("parallel", "parallel"),
    )(ac, wc)


def _mm_tn_chunks(a, bc, tile_m, tile_n):
    S, M = a.shape
    P, _, N = bc.shape
    ts, tm, tn = _tile(S, 2048), _tile(M, tile_m), _tile(N, tile_n)

    def body(a_ref, b_ref, o_ref):
        @pl.when(pl.program_id(3) == 0)
        def _():
            o_ref[...] = jnp.zeros_like(o_ref)

        o_ref[...] += _dot_tn(a_ref[...], b_ref[...])

    return pl.pallas_call(
        body, name=f"mm_tn_m{M}n{N}", grid=(P, M // tm, N // tn, S // ts),
        in_specs=[pl.BlockSpec((ts, tm), lambda p, m, n, k: (k, m)), pl.BlockSpec((None, ts, tn), lambda p, m, n, k: (p, k, n))],
        out_specs=pl.BlockSpec((None, tm, tn), lambda p, m, n, k: (p, m, n)),
        out_shape=SDS((P, M, N), F32),
        compiler_params=_params("parallel", "parallel", "parallel", "arbitrary"),
    )(a, bc)


def _shift_down(x, s, row, fill):
    return jnp.where(row >= s, pltpu.roll(x, s, 0), fill)


def _shift_up(x, s, row, fill):
    n = x.shape[0]
    return jnp.where(row < n - s, pltpu.roll(x, n - s, 0), fill)


def _scan(a, b, row, scratch, up):
    scr_a, scr_b, scr_c = scratch
    n = a.shape[0]
    g = n // SUBLANES
    in_group = row & (SUBLANES - 1)

    def steps(a, b, pos, size):
        s = 1
        while s < size:
            m = (pos + s < size) if up else (pos >= s)
            b = jnp.where(m, a, 0.0) * pltpu.roll(b, a.shape[0] - s if up else s, 0) + b
            a = jnp.where(m, a * pltpu.roll(a, a.shape[0] - s if up else s, 0), a)
            s *= 2
        return a, b

    a, b = steps(a, b, in_group, SUBLANES)
    scr_a[...] = a
    scr_b[...] = b
    edge = 0 if up else SUBLANES - 1
    at = scr_a[pl.ds(edge, g, stride=SUBLANES), :]
    bt = scr_b[pl.ds(edge, g, stride=SUBLANES), :]
    group = lax.broadcasted_iota(jnp.int32, at.shape, 0)
    _, state = steps(at, bt, group, g)
    carry = jnp.where((group + 1 < g) if up else (group >= 1), pltpu.roll(state, g - 1 if up else 1, 0), 0.0)
    for k in range(SUBLANES):
        scr_c[pl.ds(k, g, stride=SUBLANES), :] = carry
    return b + a * scr_c[...]


def _conv(xl, cw_ref, cb_ref, row):
    y = cb_ref[...] + _shift_down(xl, 3, row, 0.0) * cw_ref[0:1, :]
    y = y + _shift_down(xl, 2, row, 0.0) * cw_ref[1:2, :]
    y = y + _shift_down(xl, 1, row, 0.0) * cw_ref[2:3, :]
    return y + xl * cw_ref[3:4, :]


def _lru_gates(xc, wa_ref, ba_ref, wx_ref, bx_ref, lam_ref):
    ra = _sigmoid(_dot(xc, wa_ref[...]) + ba_ref[...])
    ri = _sigmoid(_dot(xc, wx_ref[...]) + bx_ref[...])
    ls = jax.nn.log_sigmoid(lam_ref[...])
    a = jnp.exp((RG_LRU_C * ra) * ls)
    mult = jnp.sqrt(1.0 - a * a)
    return ra, ri, ls, a, mult


def _lru_specs(S):
    col = lambda off: pl.BlockSpec((S, LANES), lambda j: (0, off + j))
    vec = pl.BlockSpec((1, LANES), lambda j: (0, j))
    blk = pl.BlockSpec((None, LANES, LANES), lambda j: (j, 0, 0))
    cw = pl.BlockSpec((CONV_WIDTH, LANES), lambda j: (0, j))
    return col, vec, blk, cw


def _lru_fwd(proj, cw, cb, wa, ba, wx, bx, lam):
    S = proj.shape[0]
    W = cb.shape[1]
    nb = W // LANES

    def body(xl_ref, gl_ref, cw_ref, cb_ref, wa_ref, ba_ref, wx_ref, bx_ref, lam_ref, y_ref, *scratch):
        row = lax.broadcasted_iota(jnp.int32, (S, LANES), 0)
        xc = _conv(xl_ref[...], cw_ref, cb_ref, row)
        _, ri, _, a, mult = _lru_gates(xc, wa_ref, ba_ref, wx_ref, bx_ref, lam_ref)
        h = _scan(a, mult * (ri * xc), row, scratch, up=False)
        y_ref[...] = h * _gelu(gl_ref[...])

    col, vec, blk, cws = _lru_specs(S)
    return pl.pallas_call(
        body, name="lru_fwd", grid=(nb,),
        in_specs=[col(0), col(nb), cws, vec, blk, vec, blk, vec, vec],
        out_specs=pl.BlockSpec((S, LANES), lambda j: (0, j)),
        out_shape=SDS((S, W), F32), scratch_shapes=[pltpu.VMEM((S, LANES), F32)] * 3, compiler_params=_params("parallel"),
    )(proj, proj, cw, cb, wa, ba, wx, bx, lam)


def _lru_bwd(proj, dy, cw, cb, wa, ba, wx, bx, lam, wat, wxt):
    S = proj.shape[0]
    W = cb.shape[1]
    nb = W // LANES

    def body(xl_ref, gl_ref, dy_ref, cw_ref, cb_ref, wa_ref, ba_ref, wx_ref, bx_ref, lam_ref, wat_ref, wxt_ref,
             dp_ref, dwa_ref, dwx_ref, vec_ref, *scratch):
        row = lax.broadcasted_iota(jnp.int32, (S, LANES), 0)
        xl = xl_ref[...]
        xc = _conv(xl, cw_ref, cb_ref, row)
        ra, ri, ls, a, mult = _lru_gates(xc, wa_ref, ba_ref, wx_ref, bx_ref, lam_ref)
        h = _scan(a, mult * (ri * xc), row, scratch, up=False)
        gl = gl_ref[...]
        dyv = dy_ref[...]
        dp_ref[1] = (dyv * h * _gelu_grad(gl)).astype(ACT_DTYPE)
        adj = _scan(_shift_up(a, 1, row, 0.0), dyv * _gelu(gl), row, scratch, up=True)
        da = adj * _shift_down(h, 1, row, 0.0)
        dmult = adj * (ri * xc)
        dlog_a = da * a - dmult * (a * a) / mult
        dra = dlog_a * (RG_LRU_C * ls)
        dpa = dra * ra * (1.0 - ra)
        dpi = (adj * mult * xc) * ri * (1.0 - ri)
        dxc = adj * mult * ri + _dot(dpa, wat_ref[...]) + _dot(dpi, wxt_ref[...])
        dwa_ref[...] = _dot_tn(xc, dpa)
        dwx_ref[...] = _dot_tn(xc, dpi)
        dxl = dxc * cw_ref[3:4, :]
        dxl = dxl + _shift_up(dxc, 1, row, 0.0) * cw_ref[2:3, :]
        dxl = dxl + _shift_up(dxc, 2, row, 0.0) * cw_ref[1:2, :]
        dxl = dxl + _shift_up(dxc, 3, row, 0.0) * cw_ref[0:1, :]
        dp_ref[0] = dxl.astype(ACT_DTYPE)
        vec_ref[...] = jnp.zeros_like(vec_ref)
        vec_ref[0:1, :] = _rowsum(dpa)
        vec_ref[1:2, :] = _rowsum(dpi)
        vec_ref[2:3, :] = _rowsum(dlog_a * (RG_LRU_C * ra)) * _sigmoid(-lam_ref[...])
        vec_ref[3:4, :] = _rowsum(dxc)
        vec_ref[4:5, :] = _rowsum(dxc * _shift_down(xl, 3, row, 0.0))
        vec_ref[5:6, :] = _rowsum(dxc * _shift_down(xl, 2, row, 0.0))
        vec_ref[6:7, :] = _rowsum(dxc * _shift_down(xl, 1, row, 0.0))
        vec_ref[7:8, :] = _rowsum(dxc * xl)

    col, vec, blk, cws = _lru_specs(S)
    return pl.pallas_call(
        body, name="lru_bwd", grid=(nb,),
        in_specs=[col(0), col(nb), col(0), cws, vec, blk, vec, blk, vec, vec, blk, blk],
        out_specs=[pl.BlockSpec((2, S, LANES), lambda j: (0, 0, j)), blk, blk, pl.BlockSpec((2 * SUBLANES, LANES), lambda j: (0, j))],
        out_shape=[SDS((2, S, W), ACT_DTYPE), SDS((nb, LANES, LANES), F32), SDS((nb, LANES, LANES), F32), SDS((2 * SUBLANES, W), F32)],
        scratch_shapes=[pltpu.VMEM((S, LANES), F32)] * 3, compiler_params=_params("parallel"),
    )(proj, proj, dy, cw, cb, wa, ba, wx, bx, lam, wat, wxt)


def _seg_mean(x, seg_ref, width):
    hi = x.astype(jnp.bfloat16)
    lo = (x - hi.astype(F32)).astype(jnp.bfloat16)
    ones = seg_ref[...]
    s = jnp.dot(hi, ones, preferred_element_type=F32) + jnp.dot(lo, ones, preferred_element_type=F32)
    return s * (1.0 / width)


def _gmlp_core(u_ref, v_ref, gv_ref, seg_ref, ws_ref, bfull_ref, z_scr, hd):
    tm, W = u_ref.shape
    lane = lax.broadcasted_iota(jnp.int32, (CHUNK, LANES), 1)
    ug = _gelu(u_ref[...])
    vg = _gelu(v_ref[...])
    cen = vg - _seg_mean(vg, seg_ref, hd)
    rstd = lax.rsqrt(_seg_mean(cen * cen, seg_ref, hd) + EPS)
    vhat = cen * rstd
    vh = vhat * gv_ref[...]
    vcats = {}
    for ci in range(tm // CHUNK):
        for p in range(W // LANES):
            blk = vh[ci * CHUNK:(ci + 1) * CHUNK, p * LANES:(p + 1) * LANES]
            vcat = jnp.concatenate([jnp.where(lane < hd, blk, 0.0), jnp.where(lane >= hd, blk, 0.0)], axis=0).astype(MXU_DTYPE)
            vcats[ci, p] = vcat
            z_scr[ci * CHUNK:(ci + 1) * CHUNK, p * LANES:(p + 1) * LANES] = (
                jnp.dot(ws_ref[p], vcat, preferred_element_type=F32) + bfull_ref[:, p * LANES:(p + 1) * LANES])
    return ug, vhat, rstd, vcats


def _gmlp_specs(tm, W, nb):
    rows = lambda off: pl.BlockSpec((tm, W), lambda i: (i, off))
    vec = pl.BlockSpec((1, W), lambda i: (0, 0))
    seg = pl.BlockSpec((W, W), lambda i: (0, 0))
    wsp = pl.BlockSpec((nb, CHUNK, 2 * CHUNK), lambda i: (0, 0, 0))
    bfull = pl.BlockSpec((CHUNK, W), lambda i: (0, 0))
    return rows, vec, seg, wsp, bfull


def _gmlp_fwd(proj, ylru, gv, seg, wsp, bfull, g_lru, g_gm):
    S, W = ylru.shape
    nb = W // LANES
    hd = W // HEADS
    tm = _tile(S, 512)

    def body(u_ref, v_ref, yl_ref, gv_ref, seg_ref, ws_ref, bfull_ref, gl_ref, gg_ref, yn_ref, ygm_ref, z_scr):
        ug, _, _, _ = _gmlp_core(u_ref, v_ref, gv_ref, seg_ref, ws_ref, bfull_ref, z_scr, hd)
        ygm = ug * z_scr[...]
        ygm_ref[...] = ygm
        yl = yl_ref[...]
        yn_ref[:, 0:W] = (yl * _rsqrt_ms(yl) * gl_ref[...]).astype(ACT_DTYPE)
        yn_ref[:, W:2 * W] = (ygm * _rsqrt_ms(ygm) * gg_ref[...]).astype(ACT_DTYPE)

    rows, vec, segs, wsps, bfulls = _gmlp_specs(tm, W, nb)
    return pl.pallas_call(
        body, name="gmlp_fwd", grid=(S // tm,),
        in_specs=[rows(2), rows(3), rows(0), vec, segs, wsps, bfulls, vec, vec],
        out_specs=[pl.BlockSpec((tm, 2 * W), lambda i: (i, 0)), rows(0)],
        out_shape=[SDS((S, 2 * W), ACT_DTYPE), SDS((S, W), F32)],
        scratch_shapes=[pltpu.VMEM((tm, W), F32)],
        compiler_params=_params("parallel"),
    )(proj, proj, ylru, gv, seg, wsp, bfull, g_lru, g_gm)


def _rms_bwd(y, g, dyn):
    r = _rsqrt_ms(y)
    yhat = y * r
    dyh = dyn * g
    return r * (dyh - yhat * jnp.mean(dyh * yhat, axis=-1, keepdims=True)), _rowsum(dyn * yhat)


def _gmlp_bwd(proj, ylru, ygm, dyn, gv, seg, wsp, wspt, bfull, g_lru, g_gm):
    S, W = ylru.shape
    nb = W // LANES
    hd = W // HEADS
    tm = _tile(S, 256)

    def body(u_ref, v_ref, yl_ref, ygm_ref, dl_ref, dg_ref, gv_ref, seg_ref, ws_ref, wst_ref, bfull_ref, gl_ref, gg_ref,
             dyl_ref, duv_ref, dws_ref, dbf_ref, acc_ref, z_scr, dvh_scr):
        @pl.when(pl.program_id(0) == 0)
        def _():
            dws_ref[...] = jnp.zeros_like(dws_ref)
            dbf_ref[...] = jnp.zeros_like(dbf_ref)
            acc_ref[...] = jnp.zeros_like(acc_ref)

        dyl, dgl = _rms_bwd(yl_ref[...], gl_ref[...], dl_ref[...])
        dyl_ref[...] = dyl
        dygm, dgg = _rms_bwd(ygm_ref[...], gg_ref[...], dg_ref[...])
        ug, vhat, rstd, vcats = _gmlp_core(u_ref, v_ref, gv_ref, seg_ref, ws_ref, bfull_ref, z_scr, hd)
        duv_ref[0] = (dygm * z_scr[...] * _gelu_grad(u_ref[...])).astype(ACT_DTYPE)
        dz = dygm * ug
        lane = lax.broadcasted_iota(jnp.int32, (CHUNK, LANES), 1)
        dbf = dz[0:CHUNK, :]
        for ci in range(1, tm // CHUNK):
            dbf += dz[ci * CHUNK:(ci + 1) * CHUNK, :]
        dbf_ref[...] += dbf
        for ci in range(tm // CHUNK):
            for p in range(nb):
                dzb = dz[ci * CHUNK:(ci + 1) * CHUNK, p * LANES:(p + 1) * LANES].astype(MXU_DTYPE)
                dws_ref[p] += _dot_nt(dzb, vcats[ci, p])
                dvc = jnp.dot(wst_ref[p], dzb, preferred_element_type=F32)
                dvh_scr[ci * CHUNK:(ci + 1) * CHUNK, p * LANES:(p + 1) * LANES] = jnp.where(lane < hd, dvc[0:CHUNK], dvc[CHUNK:2 * CHUNK])
        dvh = dvh_scr[...]
        dvn = dvh * gv_ref[...]
        dvg = rstd * (dvn - _seg_mean(dvn, seg_ref, hd) - vhat * _seg_mean(dvn * vhat, seg_ref, hd))
        duv_ref[1] = (dvg * _gelu_grad(v_ref[...])).astype(ACT_DTYPE)
        acc_ref[0:1, :] += dgl
        acc_ref[1:2, :] += dgg
        acc_ref[2:3, :] += _rowsum(dvh * vhat)

    rows, vec, segs, wsps, bfulls = _gmlp_specs(tm, W, nb)
    wspt_spec = pl.BlockSpec((nb, 2 * CHUNK, CHUNK), lambda i: (0, 0, 0))
    return pl.pallas_call(
        body, name="gmlp_bwd", grid=(S // tm,),
        in_specs=[rows(2), rows(3), rows(0), rows(0), rows(0), rows(1), vec, segs, wsps, wspt_spec, bfulls, vec, vec],
        out_specs=[rows(0), pl.BlockSpec((2, tm, W), lambda i: (0, i, 0)), wsps, bfulls, pl.BlockSpec((SUBLANES, W), lambda i: (0, 0))],
        out_shape=[SDS((S, W), F32), SDS((2, S, W), ACT_DTYPE), SDS((nb, CHUNK, 2 * CHUNK), F32), SDS((CHUNK, W), F32), SDS((SUBLANES, W), F32)],
        scratch_shapes=[pltpu.VMEM((tm, W), F32), pltpu.VMEM((tm, W), F32)],
        compiler_params=_params("arbitrary"),
    )(proj, proj, ylru, ygm, dyn, dyn, gv, seg, wsp, wspt, bfull, g_lru, g_gm)


def _ada_fwd(c_all, w_ada, b_shard):
    L, D, N = w_ada.shape
    R = c_all.shape[0]
    tn = N // 2

    def body(c_ref, w_ref, b_ref, o_ref):
        cv = c_ref[...]
        o_ref[...] = _dot(cv * _sigmoid(cv), w_ref[...]) + b_ref[...]

    return pl.pallas_call(
        body, name="ada_fwd", grid=(L, N // tn),
        in_specs=[pl.BlockSpec((R, D), lambda l, j: (0, 0)), pl.BlockSpec((None, D, tn), lambda l, j: (l, 0, j)),
                  pl.BlockSpec((None, 1, tn), lambda l, j: (l, 0, j))],
        out_specs=pl.BlockSpec((None, R, tn), lambda l, j: (l, 0, j)),
        out_shape=SDS((L, R, N), F32), compiler_params=_params("parallel", "parallel"),
    )(c_all, w_ada, b_shard)


def _ada_step(c_all_t, dmod, w, m, v):
    D, B = c_all_t.shape
    L, _, N = dmod.shape
    tn = 3 * LANES
    assert N % tn == 0, N

    def body(c_ref, d_ref, w_ref, m_ref, v_ref, g_ref, do_ref, mo_ref, vo_ref):
        cv = c_ref[...]
        sc = cv * _sigmoid(cv)
        g = sc[:, 0:1] * d_ref[0:1, :]
        for b in range(1, B):
            g += sc[:, b:b + 1] * d_ref[b:b + 1, :]
        g_ref[...] = g
        do_ref[...], mo_ref[...], vo_ref[...] = _adam_math(w_ref[...], g, m_ref[...], v_ref[...])

    tile = pl.BlockSpec((None, D, tn), lambda l, j: (l, 0, j))
    return pl.pallas_call(
        body, name="ada_step", grid=(L, N // tn),
        in_specs=[pl.BlockSpec((D, B), lambda l, j: (0, 0)), pl.BlockSpec((None, B, tn), lambda l, j: (l, 0, j)), tile, tile, tile],
        out_specs=[tile] * 4, out_shape=[SDS((L, D, N), F32)] * 4, compiler_params=_params("parallel", "parallel"),
    )(c_all_t, dmod, w, m, v)


def _adam_math(w, g, m, v):
    mn = ADAM_B1 * m + (1.0 - ADAM_B1) * g
    vn = ADAM_B2 * v + (1.0 - ADAM_B2) * (g * g)
    m_hat = mn / (1.0 - ADAM_B1 ** ADAM_STEP)
    v_hat = vn / (1.0 - ADAM_B2 ** ADAM_STEP)
    return -ADAM_LR * (m_hat / (jnp.sqrt(v_hat) + ADAM_EPS) + ADAM_WD * w), mn, vn


def _adamw_layer(w, g, m, v, l, prev, after):
    L, R, C = w.shape
    tr = _row_tile(R, C * 4)
    prev = (after,) + (() if prev is None else tuple(prev))

    def body(w_ref, g_ref, m_ref, v_ref, *rest):
        go_ref, d_ref, mo_ref, vo_ref = rest[len(prev):]
        gv = g_ref[...]
        go_ref[...] = gv
        d_ref[...], mo_ref[...], vo_ref[...] = _adam_math(w_ref[...], gv, m_ref[...], v_ref[...])

    lay = pl.BlockSpec((None, tr, C), lambda i: (l, i, 0))
    return pl.pallas_call(
        body, name=f"adamw_layer_r{R}c{C}", grid=(R // tr,),
        in_specs=[lay, pl.BlockSpec((tr, C), lambda i: (i, 0)), lay, lay] + [ANY] * len(prev), out_specs=[lay] * 4,
        out_shape=[SDS((L, R, C), F32)] * 4, input_output_aliases={5 + k: k for k in range(len(prev) - 1)},
        compiler_params=_params("parallel"),
    )(w, g, m, v, *prev)


def _adamw_small(ws, gs, ms, vs):
    n = len(ws)

    def body(*refs):
        outs = refs[4 * n:]
        for k in range(n):
            outs[k][...], outs[n + k][...], outs[2 * n + k][...] = _adam_math(
                refs[k][...], refs[n + k][...], refs[2 * n + k][...], refs[3 * n + k][...])

    whole = pl.BlockSpec(memory_space=pltpu.VMEM)
    return pl.pallas_call(
        body, name="adamw_small", in_specs=[whole] * (4 * n), out_specs=[whole] * (3 * n),
        out_shape=[SDS(w.shape, F32) for w in ws] * 3, compiler_params=pltpu.CompilerParams(vmem_limit_bytes=VMEM_LIMIT_BYTES),
    )(*ws, *gs, *ms, *vs)


def _sum_leading(a):
    P, R, C = a.shape
    tr = _row_tile(R, P * C * 4)

    def body(a_ref, o_ref):
        acc = a_ref[0]
        for p in range(1, P):
            acc = acc + a_ref[p]
        o_ref[...] = acc

    return pl.pallas_call(
        body, name=f"sum{P}_r{R}c{C}", grid=(R // tr,),
        in_specs=[pl.BlockSpec((P, tr, C), lambda i: (0, i, 0))],
        out_specs=pl.BlockSpec((tr, C), lambda i: (i, 0)),
        out_shape=SDS((R, C), F32), compiler_params=_params("parallel"),
    )(a)


def _add_half(g4, r1, place):
    _, _, R, C = g4.shape
    tr = _row_tile(R, C * 4)

    def body(place_ref, g_ref, r_ref, h_ref, own_ref):
        s = (g_ref[...] + r_ref[...]).astype(XFER_DTYPE)
        h_ref[...] = s

        @pl.when(pl.program_id(1) == place_ref[1])
        def _():
            own_ref[...] = s

    return pl.pallas_call(
        body, name=f"add_half_r{R}c{C}",
        grid_spec=pltpu.PrefetchScalarGridSpec(
            num_scalar_prefetch=1, grid=(R // tr, N_CHIPS),
            in_specs=[pl.BlockSpec((None, None, tr, C), lambda i, p, place_ref: (p, place_ref[0], i, 0)),
                      pl.BlockSpec((None, tr, C), lambda i, p, place_ref: (p, i, 0))],
            out_specs=[pl.BlockSpec((None, tr, C), lambda i, p, place_ref: (p, i, 0)),
                       pl.BlockSpec((None, tr, C), lambda i, p, place_ref: (place_ref[1], i, 0))],
        ),
        out_shape=[SDS((N_CHIPS, R, C), XFER_DTYPE)] * 2, compiler_params=_params("parallel", "arbitrary"),
    )(place, g4, r1)


def _sum4_into_half(r2, place):
    P, R, C = r2.shape
    tr = _row_tile(R, P * C * 4)

    def body(place_ref, a_ref, o_ref):
        acc = a_ref[0].astype(F32)
        for p in range(1, P):
            acc = acc + a_ref[p].astype(F32)
        o_ref[...] = acc

    return pl.pallas_call(
        body, name=f"sum4_r{R}c{C}",
        grid_spec=pltpu.PrefetchScalarGridSpec(
            num_scalar_prefetch=1, grid=(R // tr,),
            in_specs=[pl.BlockSpec((P, tr, C), lambda i, place_ref: (0, i, 0))],
            out_specs=pl.BlockSpec((None, tr, C), lambda i, place_ref: (place_ref[0], i, 0)),
        ),
        out_shape=SDS((2, R, C), F32), compiler_params=_params("parallel"),
    )(place, r2)


def _cast_into_slot(w, l, place, after):
    _, R, C = w.shape
    tr = _row_tile(R, C * 4)

    def body(place_ref, w_ref, after_ref, o_ref):
        o_ref[...] = w_ref[...].astype(MXU_DTYPE)

    return pl.pallas_call(
        body, name=f"cast_r{R}c{C}",
        grid_spec=pltpu.PrefetchScalarGridSpec(
            num_scalar_prefetch=1, grid=(R // tr,),
            in_specs=[pl.BlockSpec((None, tr, C), lambda i, place_ref: (l, i, 0)), ANY],
            out_specs=pl.BlockSpec((None, tr, C), lambda i, place_ref: (place_ref[1], i, 0)),
        ),
        out_shape=SDS((N_CHIPS, R, C), MXU_DTYPE), compiler_params=_params("parallel"),
    )(place, w, after)


def _place():
    x, y, c = lax.axis_index("x"), lax.axis_index("y"), lax.axis_index("c")
    chips = [(1 - x, y), (x, 1 - y), (1 - x, 1 - y)]
    return x, y, c, chips


def _remote(src, dst, send_sem, recv_sem, to):
    return pltpu.make_async_remote_copy(src_ref=src, dst_ref=dst, send_sem=send_sem, recv_sem=recv_sem, device_id=to, device_id_type=MESH)


def _all_gather8(v):
    R, N = v.shape

    def body(v_ref, out_ref, send_sems, recv_sems, local_sem):
        x, y, c, chips = _place()
        me, sibling = (x, y, c), (x, y, 1 - c)

        def slot(px, py, pc):
            return out_ref.at[4 * px + 2 * py + pc]

        def copy(k, block, to, src=None):
            return _remote(slot(*block) if src is None else src, slot(*block), send_sems.at[k], recv_sems.at[k], to)

        mine = pltpu.make_async_copy(v_ref, slot(*me), local_sem)
        mine.start()
        first = [copy(0, me, sibling, src=v_ref)] + [copy(1 + j, me, (*chip, c), src=v_ref) for j, chip in enumerate(chips)]
        for cp in first:
            cp.start()
        passed = [copy(4 + j, (*chip, c), sibling) for j, chip in enumerate(chips)]
        for j, chip in enumerate(chips):
            copy(1 + j, (*chip, c), me).wait_recv()
            passed[j].start()
        copy(0, sibling, me).wait_recv()
        for j, chip in enumerate(chips):
            copy(4 + j, (*chip, 1 - c), me).wait_recv()
        for cp in first + passed:
            cp.wait_send()
        mine.wait()

    return pl.pallas_call(
        body, name=f"all_gather8_r{R}n{N}", out_shape=SDS((N_DEV, R, N), v.dtype), in_specs=[ANY], out_specs=ANY,
        scratch_shapes=[pltpu.SemaphoreType.DMA((7,)), pltpu.SemaphoreType.DMA((7,)), pltpu.SemaphoreType.DMA],
    )(v)


HBM_SPEC = pl.BlockSpec(memory_space=pltpu.HBM)
SEM_SPEC = pl.BlockSpec(memory_space=pltpu.SEMAPHORE)
DATAFLOW = pltpu.SideEffectType.DATAFLOW_SIDE_EFFECTING


def _in_hbm(a):
    return pltpu.with_memory_space_constraint(a, pltpu.HBM)


def _hbm_like(a):
    return pltpu.HBM(a.shape, a.dtype)


def _split_start(name, arrays, n_sems, issue, extra=()):
    m = len(arrays)

    def body(*refs):
        issue(refs[:m], refs[m + len(extra)], refs[m + len(extra) + 1])

    out = pl.pallas_call(
        body, name=name,
        out_shape=(pltpu.SemaphoreType.DMA((n_sems,)), pltpu.SemaphoreType.DMA((n_sems,)), *[_hbm_like(a) for a in arrays]),
        in_specs=[HBM_SPEC] * m + [ANY] * len(extra), out_specs=(SEM_SPEC, SEM_SPEC, *[HBM_SPEC] * m),
        input_output_aliases={k: 2 + k for k in range(m)},
        compiler_params=pltpu.CompilerParams(has_side_effects=DATAFLOW),
    )(*[_in_hbm(a) for a in arrays], *extra)
    return out[0], out[1], list(out[2:])


def _split_wait(name, send_sems, recv_sems, arrays, after, drain):
    m = len(arrays)

    def body(*refs):
        drain(refs[:m], refs[m], refs[m + 1])

    out = pl.pallas_call(
        body, name=name, out_shape=[_hbm_like(a) for a in arrays],
        in_specs=[HBM_SPEC] * m + [SEM_SPEC, SEM_SPEC, ANY], out_specs=[HBM_SPEC] * m,
        input_output_aliases={k: k for k in range(m)},
        compiler_params=pltpu.CompilerParams(has_side_effects=DATAFLOW),
    )(*arrays, send_sems, recv_sems, after)
    return list(out)


def _wait_both(cp):
    cp.wait_send()
    cp.wait_recv()


class _Flight:
    def __init__(self, name, arrays, n_sems, issue, drain, thru, extra=()):
        self.name, self.drain, self.n = name, drain, len(arrays)
        self.send, self.recv, out = _split_start(name + "_start", [*arrays, thru], n_sems, issue, extra)
        self.arrays, self.thru = out[:-1], out[-1]

    def land(self, after):
        return _split_wait(self.name + "_wait", self.send, self.recv, self.arrays, after, self.drain)


def _gather_flight(tag, ici, d2d, direct, thru):
    kinds = ["ici"] * len(ici) + ["d2d"] * len(d2d) + ["direct"] * len(direct)

    def issue(refs, send_sems, recv_sems):
        x, y, c, chips = _place()
        q = 2 * x + y
        for k, kind in enumerate(kinds):
            for j, chip in enumerate(chips):
                if kind == "ici":
                    src, to = refs[k].at[q, c], (*chip, c)
                elif kind == "d2d":
                    src, to = refs[k].at[2 * chip[0] + chip[1], c], (x, y, 1 - c)
                else:
                    src, to = refs[k].at[q], (*chip, c)
                _remote(src, src, send_sems.at[3 * k + j], recv_sems.at[3 * k + j], to).start()

    def drain(refs, send_sems, recv_sems):
        x, y, c, chips = _place()
        for k, kind in enumerate(kinds):
            for j, chip in enumerate(chips):
                p = 2 * chip[0] + chip[1]
                got = refs[k].at[p] if kind == "direct" else refs[k].at[p, c if kind == "ici" else 1 - c]
                _wait_both(_remote(got, got, send_sems.at[3 * k + j], recv_sems.at[3 * k + j], (x, y, c)))

    return _Flight(f"gather{tag}", [*ici, *d2d, *direct], 3 * len(kinds), issue, drain, thru)


def _swap_flight(tag, g4s, thru):
    n = len(g4s)
    zones = [lax.empty((N_CHIPS,) + g.shape[2:], g.dtype) for g in g4s]

    def issue(refs, send_sems, recv_sems):
        x, y, c, _ = _place()
        for k in range(n):
            for p in range(N_CHIPS):
                _remote(refs[k].at[p, 1 - c], refs[n + k].at[p], send_sems.at[N_CHIPS * k + p], recv_sems.at[N_CHIPS * k + p], (x, y, 1 - c)).start()

    def drain(refs, send_sems, recv_sems):
        x, y, c, _ = _place()
        for k in range(n):
            for p in range(N_CHIPS):
                got = refs[n + k].at[p]
                _wait_both(_remote(got, got, send_sems.at[N_CHIPS * k + p], recv_sems.at[N_CHIPS * k + p], (x, y, c)))

    return _Flight(f"swap{tag}", [*g4s, *zones], N_CHIPS * n, issue, drain, thru)


def _scatter_flight(tag, hs, lands, thru):
    n = len(hs)

    def issue(refs, send_sems, recv_sems):
        x, y, c, chips = _place()
        q = 2 * x + y
        for k in range(n):
            for j, chip in enumerate(chips):
                _remote(refs[k].at[2 * chip[0] + chip[1]], refs[n + k].at[q], send_sems.at[3 * k + j], recv_sems.at[3 * k + j], (*chip, c)).start()

    def drain(refs, send_sems, recv_sems):
        x, y, c, chips = _place()
        for k in range(n):
            for j, chip in enumerate(chips):
                got = refs[n + k].at[2 * chip[0] + chip[1]]
                _wait_both(_remote(got, got, send_sems.at[3 * k + j], recv_sems.at[3 * k + j], (x, y, c)))

    return _Flight(f"scatter{tag}", [*hs, *lands], 3 * n, issue, drain, thru)


def _exchange_flight(tag, buf, thru):
    flips = [(fx, fy, fc) for fx in (0, 1) for fy in (0, 1) for fc in (0, 1)][1:]

    def peers():
        x, y, c, _ = _place()
        return (x, y, c), [((1 - x) if fx else x, (1 - y) if fy else y, (1 - c) if fc else c) for fx, fy, fc in flips]

    def slot(ref, dev):
        return ref.at[4 * dev[0] + 2 * dev[1] + dev[2]]

    def issue(refs, send_sems, recv_sems):
        me, others = peers()
        for j, to in enumerate(others):
            _remote(slot(refs[0], me), slot(refs[0], me), send_sems.at[j], recv_sems.at[j], to).start()

    def drain(refs, send_sems, recv_sems):
        me, others = peers()
        for j, frm in enumerate(others):
            got = slot(refs[0], frm)
            _wait_both(_remote(got, got, send_sems.at[j], recv_sems.at[j], me))

    return _Flight(f"exchange{tag}", [buf], len(flips), issue, drain, thru)


def _share_flight(tag, fins, thru):
    n = len(fins)

    def issue(refs, send_sems, recv_sems):
        x, y, c, _ = _place()
        for k in range(n):
            _remote(refs[k].at[c], refs[k].at[c], send_sems.at[k], recv_sems.at[k], (x, y, 1 - c)).start()

    def drain(refs, send_sems, recv_sems):
        x, y, c, _ = _place()
        for k in range(n):
            got = refs[k].at[1 - c]
            _wait_both(_remote(got, got, send_sems.at[k], recv_sems.at[k], (x, y, c)))

    return _Flight(f"share{tag}", fins, n, issue, drain, thru)


def _pair_blocks(w):
    h, d, _ = w.shape
    z = jnp.zeros((h // 2, d, d), w.dtype)
    return jnp.concatenate([jnp.concatenate([w[0::2], z], axis=2), jnp.concatenate([z, w[1::2]], axis=2)], axis=1)


def _unpair_blocks(b):
    n, dd, _ = b.shape
    d = dd // 2
    return jnp.stack([b[:, :d, :d], b[:, d:, d:]], axis=1).reshape(2 * n, d, d)


def _pad_rows(a, rows):
    return jnp.pad(a, ((0, rows - a.shape[0]), (0, 0)))


class _Packer:
    def __init__(self, shapes, width=1024, row_multiple=64):
        self.shapes = shapes
        self.sizes = [math.prod(s) for s in shapes]
        total = sum(self.sizes)
        self.width = width
        self.rows = -(-total // (width * row_multiple)) * row_multiple
        self.pad = self.rows * width - total

    def pack(self, arrays):
        flat = jnp.concatenate([a.reshape(-1).astype(F32) for a in arrays] + [jnp.zeros((self.pad,), F32)])
        return flat.reshape(self.rows, self.width)

    def unpack(self, packed):
        flat = packed.reshape(-1)
        out, off = [], 0
        for s, n in zip(self.shapes, self.sizes):
            out.append(flat[off:off + n].reshape(s))
            off += n
        return out


SMALL = ["b_ada", "ffn1_norm", "mix_norm", "conv_w", "conv_b", "gate_a_w", "gate_a_b", "gate_x_w", "gate_x_b", "lru_lambda",
         "v_norm", "spatial_w", "spatial_b", "lru_out_norm", "gmlp_out_norm", "ffn2_norm", "final_norm"]
BIG = ["ffn1_w_gu", "ffn1_w_down", "w_in", "w_out", "ffn2_w_gu", "ffn2_w_down"]
GROUPS = (("ffn1_w_gu", "ffn1_w_down"), ("w_in", "w_out"), ("ffn2_w_gu", "ffn2_w_down"))
FWD_GROUPS = (("ffn1_w_gu",), ("ffn1_w_down",), ("w_in", "w_out"), ("ffn2_w_gu",), ("ffn2_w_down",))
MIN_AGE = {"swap": 1, "scatter": 1, "share": 1}
WEIGHTS = ["w_ada", "b_ada", "ffn1_norm", "ffn1_w_gu", "ffn1_w_down", "mix_norm", "w_in", "conv_w", "conv_b", "gate_a_w", "gate_a_b",
           "gate_x_w", "gate_x_b", "lru_lambda", "v_norm", "spatial_w", "spatial_b", "lru_out_norm", "gmlp_out_norm", "w_out",
           "ffn2_norm", "ffn2_w_gu", "ffn2_w_down", "final_norm"]


def kernel(x, c, w_ada, b_ada, ffn1_norm, ffn1_w_gu, ffn1_w_down, mix_norm, w_in, conv_w, conv_b, gate_a_w, gate_a_b, gate_x_w, gate_x_b, lru_lambda, v_norm, spatial_w, spatial_b, lru_out_norm, gmlp_out_norm, w_out, ffn2_norm, ffn2_w_gu, ffn2_w_down, final_norm, loss_target, m_w_ada, m_b_ada, m_ffn1_norm, m_ffn1_w_gu, m_ffn1_w_down, m_mix_norm, m_w_in, m_conv_w, m_conv_b, m_gate_a_w, m_gate_a_b, m_gate_x_w, m_gate_x_b, m_lru_lambda, m_v_norm, m_spatial_w, m_spatial_b, m_lru_out_norm, m_gmlp_out_norm, m_w_out, m_ffn2_norm, m_ffn2_w_gu, m_ffn2_w_down, m_final_norm, v_w_ada, v_b_ada, v_ffn1_norm, v_ffn1_w_gu, v_ffn1_w_down, v_mix_norm, v_w_in, v_conv_w, v_conv_b, v_gate_a_w, v_gate_a_b, v_gate_x_w, v_gate_x_b, v_lru_lambda, v_v_norm, v_spatial_w, v_spatial_b, v_lru_out_norm, v_gmlp_out_norm, v_w_out, v_ffn2_norm, v_ffn2_w_gu, v_ffn2_w_down, v_final_norm):
    given = dict(locals())
    W = {n: given[n] for n in WEIGHTS}
    L = w_ada.shape[0]
    S, D = x.shape[1], x.shape[2]
    LW = conv_b.shape[1]
    hd = LW // HEADS
    xi, yi, ci = lax.axis_index("x"), lax.axis_index("y"), lax.axis_index("c")
    chip = 2 * xi + yi
    dev = 2 * chip + ci
    place = jnp.stack([ci, chip]).astype(jnp.int32)
    xs = x.reshape(S, D)
    tgt = loss_target.reshape(S, D)

    c_all = _all_gather8(_pad_rows(c, SUBLANES))[:, 0, :]
    n_ada = w_ada.shape[2]
    b_shard = lax.dynamic_slice_in_dim(b_ada, chip * n_ada, n_ada, axis=1)
    mod_shard = _ada_fwd(_pad_rows(c_all, 2 * SUBLANES), w_ada, b_shard[:, None, :])

    def in_slot(block):
        return lax.dynamic_update_index_in_dim(jnp.zeros((N_CHIPS,) + block.shape, block.dtype), block, chip, 0)

    cws = LW // N_CHIPS
    small = [in_slot(mod_shard.reshape(L * 2 * SUBLANES, n_ada)), in_slot(conv_w.reshape(L * CONV_WIDTH, cws))]

    def half_view(s):
        return s.reshape(N_CHIPS, 2, s.shape[1] // 2, s.shape[2])

    stages = [(l, names) for l in range(L) for names in FWD_GROUPS]
    seq = [[half_view(_cast_into_slot(W[n], l, place, place)) for n in names] for l, names in stages[:1]]
    flights = {}

    def launch(t, thru, direct=()):
        ici = seq[t] if t < len(seq) else []
        d2d = seq[t - 1] if 1 <= t <= len(seq) else []
        if ici or d2d or direct:
            flights[t] = _gather_flight(t, ici, d2d, list(direct), thru)
            thru = flights[t].thru
        return thru

    def land(t, after):
        if t not in flights:
            return []
        out = flights.pop(t).land(after)
        ni = len(seq[t]) if t < len(seq) else 0
        nd = len(seq[t - 1]) if 1 <= t <= len(seq) else 0
        if ni:
            seq[t] = out[:ni]
        if nd:
            seq[t - 1] = out[ni:ni + nd]
        return out[ni + nd:]

    def group_weights(t):
        return [s.reshape(N_CHIPS, -1, s.shape[3]) for s in seq[t]]

    c_all = launch(0, c_all, small)
    seq += [[half_view(_cast_into_slot(W[n], l, place, c_all)) for n in names] for l, names in stages[1:]]
    mod_all, conv_all = land(0, seq[-1][-1])
    mod_all = launch(1, mod_all)
    land(1, mod_all)
    mod_rows = lax.dynamic_index_in_dim(mod_all.reshape(N_CHIPS, L, 2 * SUBLANES, n_ada), dev, axis=2, keepdims=False)
    mod = mod_rows.transpose(1, 0, 2).reshape(L, N_MOD, 1, D)
    conv_full = conv_all.reshape(N_CHIPS, L, CONV_WIDTH, cws).transpose(1, 2, 0, 3).reshape(L, CONV_WIDTH, LW)

    tril = jnp.tril(jnp.ones((CHUNK, CHUNK), F32))
    seg = (jnp.arange(LW)[:, None] // hd == jnp.arange(LW)[None, :] // hd).astype(jnp.bfloat16)

    def mixer_params(l):
        ws = spatial_w[l] * tril
        wsp = jnp.concatenate([ws[0::2], ws[1::2]], axis=2)
        wa, wx = _pair_blocks(gate_a_w[l]), _pair_blocks(gate_x_w[l])
        return dict(
            cw=conv_full[l], cb=conv_b[l][None],
            wa=wa.astype(MXU_DTYPE), wx=wx.astype(MXU_DTYPE), wat=wa.transpose(0, 2, 1).astype(MXU_DTYPE), wxt=wx.transpose(0, 2, 1).astype(MXU_DTYPE),
            ba=gate_a_b[l].reshape(1, LW), bx=gate_x_b[l].reshape(1, LW), lam=lru_lambda[l][None], gv=v_norm[l][None],
            wsp=wsp.astype(MXU_DTYPE), wspt=wsp.transpose(0, 2, 1).astype(MXU_DTYPE),
            bfull=jnp.repeat(spatial_b[l].T, hd, axis=1), g_lru=lru_out_norm[l][None], g_gm=gmlp_out_norm[l][None])

    saved = []
    xcur = xs
    zero_row = jnp.zeros((1, D), F32)
    h = _modnorm(xcur, ffn1_norm[0][None], mod[0][0], mod[0][1])
    for l in range(L):
        mp, md = mixer_params(l), mod[l]
        s = dict(lw={}, mp=mp, md=md)
        lw = s["lw"]
        t = len(FWD_GROUPS) * l
        s["x0"] = xcur
        s["h1"] = launch(t + 2, h)
        lw["gu1"], = group_weights(t)
        s["a1"], s["gu1"] = _ffn_up(s["h1"], lw["gu1"])
        land(t + 2, s["a1"])
        s["a1"] = launch(t + 3, s["a1"])
        lw["d1"] = group_weights(t + 1)[0].reshape(-1, D)
        s["f1"], xcur, h = _mm_res(s["a1"], lw["d1"], xcur, md[2], 0.5, (mix_norm[l][None], md[3], md[4]))
        land(t + 3, xcur)
        s["x1"] = xcur
        s["h2"] = launch(t + 4, h)
        lw["win"], wout = group_weights(t + 2)
        lw["wout"] = wout.reshape(-1, D)
        s["proj"] = _mm_chunks(s["h2"], lw["win"])
        s["ylru"] = _lru_fwd(s["proj"], mp["cw"], mp["cb"], mp["wa"], mp["ba"], mp["wx"], mp["bx"], mp["lam"])
        s["yn"], s["ygm"] = _gmlp_fwd(s["proj"], s["ylru"], mp["gv"], seg, mp["wsp"], mp["bfull"], mp["g_lru"], mp["g_gm"])
        s["f2"], xcur, h = _mm_res(s["yn"], lw["wout"], xcur, md[5], 1.0, (ffn2_norm[l][None], md[6], md[7]))
        land(t + 4, xcur)
        s["x2"] = xcur
        s["h3"] = launch(t + 5, h)
        lw["gu2"], = group_weights(t + 3)
        s["a3"], s["gu3"] = _ffn_up(s["h3"], lw["gu2"])
        land(t + 5, s["a3"])
        s["a3"] = launch(t + 6, s["a3"])
        lw["d2"] = group_weights(t + 4)[0].reshape(-1, D)
        following = (ffn1_norm[l + 1][None], mod[l + 1][0], mod[l + 1][1]) if l + 1 < L else (final_norm[None], zero_row, zero_row)
        s["f3"], xcur, h = _mm_res(s["a3"], lw["d2"], xcur, md[8], 0.5, following)
        land(t + 6, xcur)
        saved.append(s)

    dx, dq, head_acc = _loss_head(xcur, tgt, final_norm[None], saved[-1]["md"][8], 0.5)
    loss = lax.psum(jnp.sum(head_acc[1]), ("x", "y", "c"))
    big_grads = {n: [None] * L for n in BIG}
    dmods = [None] * L

    def ffn_bwd(names, l, dx, dq, x_in, h, a, gu, f, wgu, wd, gn, sc, next_gate, next_scale):
        big_grads[names[1]][l] = _mm_tn_chunks(a, dq[None], 1408, 1024)[0].reshape(N_CHIPS, -1, D)
        dgu = _ffn_bwd_act(dq, wd, gu)
        C = dgu.shape[3]
        dgu4 = dgu.reshape(N_CHIPS, S, C)
        big_grads[names[0]][l] = _mm_tn_chunks(h, dgu4, 1024, C)
        dgu4 = reduce_group(names, l, big_grads[names[0]][l], dgu4)
        dx, dq, acc = _mm_nt_norm_bwd(dgu4, wgu, x_in, dx, f, gn, sc, 0.5, next_gate, next_scale)
        return dx, move_on(dx, dq), acc

    stepped = {n: None for n in BIG}
    reducing = []

    per_layer = [n for n in SMALL if n not in ("b_ada", "final_norm")]
    packers, exchanges = {}, {}
    clock = [0]
    to_step = []

    def step_reduced(after):
        while to_step:
            name, l, g = to_step.pop(0)
            stepped[name] = _adamw_layer(W[name], g, given["m_" + name], given["v_" + name], l, stepped[name], after)
            after = stepped[name][1]
        return after

    def move_on(after, thru, force=False):
        clock[0] += 1
        for grp in list(reducing):
            if not force and clock[0] - grp["since"] < MIN_AGE[grp["step"]]:
                continue
            grp["since"] = clock[0]
            landed = grp["flight"].land(after)
            n = len(grp["names"])
            if grp["step"] == "swap":
                pairs = [_add_half(g4, r1, place) for g4, r1 in zip(landed[:n], landed[n:])]
                grp.update(step="scatter", flight=_scatter_flight(grp["tag"], [h for h, _ in pairs], [own for _, own in pairs], thru))
            elif grp["step"] == "scatter":
                grp.update(step="share", flight=_share_flight(grp["tag"], [_sum4_into_half(r2, place) for r2 in landed[n:]], thru))
            else:
                to_step.extend((name, grp["l"], fin.reshape(2 * fin.shape[1], fin.shape[2])) for name, fin in zip(grp["names"], landed))
                reducing.remove(grp)
                continue
            thru = grp["flight"].thru
        return thru

    def reduce_group(names, l, after, thru):
        thru = move_on(after, thru)
        g4s = [big_grads[n][l].reshape(N_CHIPS, 2, big_grads[n][l].shape[1] // 2, big_grads[n][l].shape[2]) for n in names]
        tag = f"{l}{GROUPS.index(names)}"
        reducing.append(dict(names=names, l=l, tag=tag, step="swap", since=clock[0], flight=_swap_flight(tag, g4s, thru)))
        return reducing[-1]["flight"].thru
    for l in reversed(range(L)):
        s = saved[l]
        lw, mp, md = s["lw"], s["mp"], s["md"]
        dx, dq, acc3 = ffn_bwd(
            GROUPS[2], l, dx, dq, s["x2"], s["h3"], s["a3"], s["gu3"], s["f3"], lw["gu2"], lw["d2"], ffn2_norm[l][None], md[7], md[5], 1.0)
        big_grads["w_out"][l] = _mm_tn_chunks(s["yn"], dq[None], 1024, 1024)[0].reshape(N_CHIPS, -1, D)
        dyn = _mm_nt_chunks(dq[None], lw["wout"][None])
        dylru, duv, dwsp, dbfull, gacc = _gmlp_bwd(s["proj"], s["ylru"], s["ygm"], dyn, mp["gv"], seg, mp["wsp"], mp["wspt"], mp["bfull"], mp["g_lru"], mp["g_gm"])
        dxg, dwa, dwx, lvec = _lru_bwd(s["proj"], dylru, mp["cw"], mp["cb"], mp["wa"], mp["ba"], mp["wx"], mp["bx"], mp["lam"], mp["wat"], mp["wxt"])
        dproj = jnp.concatenate([dxg, duv], axis=0)
        big_grads["w_in"][l] = _mm_tn_chunks(s["h2"], dproj, 1024, LW)
        dproj = reduce_group(GROUPS[1], l, big_grads["w_in"][l], dproj)
        dx, dq, acc2 = _mm_nt_norm_bwd(dproj, lw["win"], s["x1"], dx, s["f2"], mix_norm[l][None], md[4], 1.0, md[2], 0.5)
        dq = move_on(dx, dq)
        if l > 0:
            ng, ns = saved[l - 1]["md"][8], 0.5
        else:
            ng, ns = zero_row, 0.0
        dx, dq, acc1 = ffn_bwd(
            GROUPS[0], l, dx, dq, s["x0"], s["h1"], s["a1"], s["gu1"], s["f1"], lw["gu1"], lw["d1"], ffn1_norm[l][None], md[1], ng, ns)

        dmods[l] = jnp.concatenate([acc1[0:2], acc1[3:4], acc2[0:2], acc2[3:4], acc3[0:2], acc3[3:4]], axis=0)
        dws = jnp.stack([dwsp[:, :, :CHUNK], dwsp[:, :, CHUNK:]], axis=1).reshape(HEADS, CHUNK, CHUNK) * tril
        lg = {"ffn1_norm": acc1[2], "mix_norm": acc2[2], "ffn2_norm": acc3[2],
              "conv_w": lvec[4:8], "conv_b": lvec[3], "gate_a_w": _unpair_blocks(dwa), "gate_a_b": lvec[0].reshape(HEADS, hd),
              "gate_x_w": _unpair_blocks(dwx), "gate_x_b": lvec[1].reshape(HEADS, hd), "lru_lambda": lvec[2], "v_norm": gacc[2],
              "spatial_w": dws, "spatial_b": dbfull.reshape(CHUNK, HEADS, hd).sum(-1).T, "lru_out_norm": gacc[0], "gmlp_out_norm": gacc[1]}
        part = [lg[n] for n in per_layer] + [dmods[l]] + ([head_acc[0]] if l == L - 1 else [])
        packers[l] = _Packer([p.shape for p in part])
        packed = packers[l].pack(part)
        exchanges[l] = _exchange_flight(l, lax.dynamic_update_index_in_dim(jnp.zeros((N_DEV,) + packed.shape, F32), packed, dev, 0), dq)
        dq = exchanges[l].thru

    grad_x = dx.reshape(x.shape)

    done = step_reduced(dq)
    while reducing:
        dq = move_on(done, dq, force=True)
        done = step_reduced(dq)
    summed, dmod_rows = [], []
    for l in range(L):
        gathered, = exchanges[l].land(done)
        summed.append(packers[l].unpack(_sum_leading(gathered)))
        off = sum(packers[l].sizes[:len(per_layer)])
        dmod_rows.append(gathered.reshape(N_DEV, -1)[:, off:off + N_MOD * D])
    grads = {n: jnp.stack([summed[l][k] for l in range(L)]) for k, n in enumerate(per_layer)}
    grads["final_norm"] = summed[L - 1][len(per_layer) + 1]
    grads["b_ada"] = jnp.stack([summed[l][len(per_layer)].reshape(N_MOD * D) for l in range(L)])
    dmod_shard = lax.dynamic_slice_in_dim(jnp.stack(dmod_rows), chip * n_ada, n_ada, axis=2)
    stepped_ada = _ada_step(c_all.T, dmod_shard, w_ada, m_w_ada, v_w_ada)
    grads["w_ada"] = stepped_ada[0]
    grads["conv_w"] = lax.dynamic_slice_in_dim(grads["conv_w"], chip * cws, cws, axis=2)

    delta, new_m, new_v = {}, {}, {}
    for n in BIG:
        grads[n], delta[n], new_m[n], new_v[n] = stepped[n]
    delta["w_ada"], new_m["w_ada"], new_v["w_ada"] = stepped_ada[1:]
    def rows_of(a):
        return a.reshape(-1, a.shape[-1])

    stepped_small = _adamw_small(*[[rows_of(src[n].reshape(W[n].shape)) for n in SMALL]
                                   for src in (W, grads, {n: given["m_" + n] for n in SMALL}, {n: given["v_" + n] for n in SMALL})])
    for k, n in enumerate(SMALL):
        delta[n], new_m[n], new_v[n] = (stepped_small[i * len(SMALL) + k].reshape(W[n].shape) for i in range(3))
    grads = {n: grads[n].reshape(W[n].shape) for n in WEIGHTS}
    return (loss, grad_x, *[grads[n] for n in WEIGHTS], *[delta[n] for n in WEIGHTS], *[new_m[n] for n in WEIGHTS], *[new_v[n] for n in WEIGHTS])
```

```python
import math

import jax
import jax.numpy as jnp
from jax import lax
from jax.experimental import pallas as pl
from jax.experimental.pallas import tpu as pltpu

F32 = jnp.float32
MXU_DTYPE = jnp.bfloat16
ACT_DTYPE = jnp.bfloat16
XFER_DTYPE = jnp.bfloat16
EPS = 1e-6
RG_LRU_C = 8.0
N_MOD = 9
CONV_WIDTH = 4
HEADS = 8
CHUNK = 128
LANES = 128
SUBLANES = 8
N_CHIPS = 4
N_DEV = 8
ADAM_LR, ADAM_B1, ADAM_B2, ADAM_EPS, ADAM_WD, ADAM_STEP = 0.001, 0.9, 0.999, 1e-08, 0.01, 10
VMEM_LIMIT_BYTES = 60 * 1024 * 1024
ROW_TILE_BYTES = 2 << 20
GELU_C = math.sqrt(2.0 / math.pi)
GELU_A = 0.044715

ANY = pl.BlockSpec(memory_space=pl.ANY)
MESH = pl.DeviceIdType.MESH
SDS = jax.ShapeDtypeStruct


def _params(*sem):
    return pltpu.CompilerParams(dimension_semantics=sem, vmem_limit_bytes=VMEM_LIMIT_BYTES)


def _dot(a, b):
    return jnp.dot(a.astype(MXU_DTYPE), b.astype(MXU_DTYPE), preferred_element_type=F32)


def _dot_nt(a, b):
    return lax.dot_general(a.astype(MXU_DTYPE), b.astype(MXU_DTYPE), (((1,), (1,)), ((), ())), preferred_element_type=F32)


def _dot_tn(a, b):
    return lax.dot_general(a.astype(MXU_DTYPE), b.astype(MXU_DTYPE), (((0,), (0,)), ((), ())), preferred_element_type=F32)


def _gelu(x):
    return x * (0.5 * (1.0 + jnp.tanh(GELU_C * (x + GELU_A * (x * x * x)))))


def _gelu_grad(x):
    t = jnp.tanh(GELU_C * (x + GELU_A * (x * x * x)))
    return 0.5 * (1.0 + t) + 0.5 * x * (1.0 - t * t) * (GELU_C * (1.0 + 3.0 * GELU_A * x * x))


def _sigmoid(x):
    return jax.nn.sigmoid(x)


def _sigmoid_by_tanh(x):
    return 0.5 * jnp.tanh(0.5 * x) + 0.5


def _rsqrt_ms(x):
    return lax.rsqrt(jnp.mean(x * x, axis=-1, keepdims=True) + EPS)


def _rowsum(x):
    return jnp.sum(x, axis=0, keepdims=True)


def _tile(n, want):
    t = min(n, want)
    assert n % t == 0, (n, want)
    return t


def _row_tile(rows, row_bytes):
    step = 2 * SUBLANES
    cap = max(step, ROW_TILE_BYTES // row_bytes)
    best = None
    for t in range(step, min(rows, cap) + 1, step):
        if rows % t == 0:
            best = t
    assert best is not None, (rows, row_bytes)
    return best


def _modnorm(x, gn, sh, sc):
    S, D = x.shape
    tm = _tile(S, 1024)

    def body(x_ref, gn_ref, sh_ref, sc_ref, h_ref):
        xv = x_ref[...]
        h = (xv * _rsqrt_ms(xv) * gn_ref[...]) * (1.0 + sc_ref[...]) + sh_ref[...]
        h_ref[...] = h.astype(ACT_DTYPE)

    row = pl.BlockSpec((1, D), lambda i: (0, 0))
    return pl.pallas_call(
        body, name="modnorm", grid=(S // tm,),
        in_specs=[pl.BlockSpec((tm, D), lambda i: (i, 0)), row, row, row],
        out_specs=pl.BlockSpec((tm, D), lambda i: (i, 0)),
        out_shape=SDS((S, D), ACT_DTYPE), compiler_params=_params("parallel"),
    )(x, gn, sh, sc)


def _mm_nt_norm_bwd(ac, wc, x, dxo, f, gn, sc, res_scale, next_gate, next_scale):
    P, S, K = ac.shape
    D = x.shape[1]
    tm = _tile(S, 512)

    def body(a_ref, w_ref, x_ref, dxo_ref, f_ref, gn_ref, sc_ref, ng_ref, dx_ref, dq_ref, acc_ref):
        @pl.when(pl.program_id(0) == 0)
        def _():
            acc_ref[...] = jnp.zeros_like(acc_ref)

        dh = _dot_nt(a_ref[0], w_ref[0])
        for p in range(1, P):
            dh += _dot_nt(a_ref[p], w_ref[p])
        xv, dxo = x_ref[...], dxo_ref[...]
        r = _rsqrt_ms(xv)
        xhat = xv * r
        gn = gn_ref[...]
        dn = dh * (1.0 + sc_ref[...])
        dxh = dn * gn
        dx = dxo + r * (dxh - xhat * jnp.mean(dxh * xhat, axis=-1, keepdims=True))
        dx_ref[...] = dx
        dq_ref[...] = ((next_scale * ng_ref[...]) * dx).astype(ACT_DTYPE)
        acc_ref[0:1, :] += _rowsum(dh)
        acc_ref[1:2, :] += _rowsum(dh * (xhat * gn))
        acc_ref[2:3, :] += _rowsum(dn * xhat)
        acc_ref[3:4, :] += _rowsum((res_scale * f_ref[...]) * dxo)

    tile = pl.BlockSpec((tm, D), lambda i: (i, 0))
    row = pl.BlockSpec((1, D), lambda i: (0, 0))
    return pl.pallas_call(
        body, name=f"mm_nt_norm_bwd_k{K}", grid=(S // tm,),
        in_specs=[pl.BlockSpec((P, tm, K), lambda i: (0, i, 0)),
                  pl.BlockSpec((P, D, K), lambda i: (0, 0, 0), pipeline_mode=pl.Buffered(1)),
                  tile, tile, tile, row, row, row],
        out_specs=[tile, tile, pl.BlockSpec((SUBLANES, D), lambda i: (0, 0))],
        out_shape=[SDS((S, D), F32), SDS((S, D), ACT_DTYPE), SDS((SUBLANES, D), F32)],
        compiler_params=_params("arbitrary"),
    )(ac, wc, x, dxo, f, gn, sc, next_gate)


def _loss_head(x, target, gn, next_gate, next_scale):
    S, D = x.shape
    tm = _tile(S, 512)

    def body(x_ref, t_ref, gn_ref, ng_ref, dx_ref, dq_ref, acc_ref):
        @pl.when(pl.program_id(0) == 0)
        def _():
            acc_ref[...] = jnp.zeros_like(acc_ref)

        xv = x_ref[...]
        r = _rsqrt_ms(xv)
        xhat = xv * r
        gn = gn_ref[...]
        err = xhat * gn - t_ref[...]
        dy = err * (1.0 / D)
        dxh = dy * gn
        dx = r * (dxh - xhat * jnp.mean(dxh * xhat, axis=-1, keepdims=True))
        dx_ref[...] = dx
        dq_ref[...] = ((next_scale * ng_ref[...]) * dx).astype(ACT_DTYPE)
        acc_ref[0:1, :] += _rowsum(dy * xhat)
        acc_ref[1:2, :] += _rowsum(err * err) * (0.5 / D)

    tile = pl.BlockSpec((tm, D), lambda i: (i, 0))
    row = pl.BlockSpec((1, D), lambda i: (0, 0))
    return pl.pallas_call(
        body, name="loss_head", grid=(S // tm,),
        in_specs=[tile, tile, row, row],
        out_specs=[tile, tile, pl.BlockSpec((SUBLANES, D), lambda i: (0, 0))],
        out_shape=[SDS((S, D), F32), SDS((S, D), ACT_DTYPE), SDS((SUBLANES, D), F32)],
        compiler_params=_params("arbitrary"),
    )(x, target, gn, next_gate)


def _ffn_up(h, wgu):
    S, D = h.shape
    C = wgu.shape[2]
    tm = _tile(S, 512)

    def body(h_ref, wg_ref, wu_ref, a_ref, gu_ref):
        hv = h_ref[...]
        g = _dot(hv, wg_ref[...])
        u = _dot(hv, wu_ref[...])
        a_ref[...] = (g * _sigmoid_by_tanh(g) * u).astype(ACT_DTYPE)
        gu_ref[0] = g.astype(ACT_DTYPE)
        gu_ref[1] = u.astype(ACT_DTYPE)

    return pl.pallas_call(
        body, name="ffn_up", grid=(2, S // tm),
        in_specs=[
            pl.BlockSpec((tm, D), lambda j, i: (i, 0)),
            pl.BlockSpec((None, D, C), lambda j, i: (j, 0, 0)),
            pl.BlockSpec((None, D, C), lambda j, i: (2 + j, 0, 0)),
        ],
        out_specs=[
            pl.BlockSpec((tm, C), lambda j, i: (i, j)),
            pl.BlockSpec((2, None, tm, C), lambda j, i: (0, j, i, 0)),
        ],
        out_shape=[SDS((S, 2 * C), ACT_DTYPE), SDS((2, 2, S, C), ACT_DTYPE)],
        compiler_params=_params("parallel", "parallel"),
    )(h, wgu, wgu)


def _ffn_bwd_act(dq, wd, gu):
    S, D = dq.shape
    C = gu.shape[3]
    tm = _tile(S, 512)

    def body(dq_ref, wd_ref, gu_ref, dgu_ref):
        da = _dot_nt(dq_ref[...], wd_ref[...])
        g = gu_ref[0].astype(F32)
        u = gu_ref[1].astype(F32)
        s = _sigmoid_by_tanh(g)
        gs = g * s
        dgu_ref[0] = ((da * u) * (s + gs - gs * s)).astype(ACT_DTYPE)
        dgu_ref[1] = (da * gs).astype(ACT_DTYPE)

    gu_spec = pl.BlockSpec((2, None, tm, C), lambda j, i: (0, j, i, 0))
    return pl.pallas_call(
        body, name="ffn_bwd_act", grid=(2, S // tm),
        in_specs=[pl.BlockSpec((tm, D), lambda j, i: (i, 0)), pl.BlockSpec((C, D), lambda j, i: (j, 0)), gu_spec],
        out_specs=gu_spec,
        out_shape=SDS(gu.shape, ACT_DTYPE),
        compiler_params=_params("parallel", "parallel"),
    )(dq, wd, gu)


def _mm_res(a, w, x, gate, scale, following):
    S, K = a.shape
    D = w.shape[1]
    tm = _tile(S, 512)

    def body(a_ref, w_ref, x_ref, g_ref, gn_ref, sh_ref, sc_ref, f_ref, xo_ref, h_ref):
        f = _dot(a_ref[...], w_ref[...])
        f_ref[...] = f
        xo = x_ref[...] + (scale * g_ref[...]) * f
        xo_ref[...] = xo
        h_ref[...] = ((xo * _rsqrt_ms(xo) * gn_ref[...]) * (1.0 + sc_ref[...]) + sh_ref[...]).astype(ACT_DTYPE)

    tile = pl.BlockSpec((tm, D), lambda i: (i, 0))
    row = pl.BlockSpec((1, D), lambda i: (0, 0))
    return pl.pallas_call(
        body, name=f"mm_res_k{K}", grid=(S // tm,),
        in_specs=[pl.BlockSpec((tm, K), lambda i: (i, 0)), pl.BlockSpec((K, D), lambda i: (0, 0)), tile, row, row, row, row],
        out_specs=[tile, tile, tile],
        out_shape=[SDS((S, D), F32), SDS((S, D), F32), SDS((S, D), ACT_DTYPE)],
        compiler_params=_params("parallel"),
    )(a, w, x, gate, *following)


def _mm_chunks(h, wc):
    S, K = h.shape
    P, _, N = wc.shape
    tm = _tile(S, 512)

    def body(h_ref, w_ref, o_ref):
        hv = h_ref[...]
        for p in range(P):
            o_ref[:, p * N:(p + 1) * N] = _dot(hv, w_ref[p])

    return pl.pallas_call(
        body, name="mm_chunks", grid=(S // tm,),
        in_specs=[pl.BlockSpec((tm, K), lambda i: (i, 0)), pl.BlockSpec((P, K, N), lambda i: (0, 0, 0))],
        out_specs=pl.BlockSpec((tm, P * N), lambda i: (i, 0)),
        out_shape=SDS((S, P * N), F32),
        compiler_params=_params("parallel"),
    )(h, wc)


def _mm_nt_chunks(ac, wc):
    P, S, K = ac.shape
    N = wc.shape[1]
    tm, tn = _tile(S, 512), _tile(N, 1024)

    def body(a_ref, w_ref, o_ref):
        acc = _dot_nt(a_ref[0], w_ref[0])
        for p in range(1, P):
            acc += _dot_nt(a_ref[p], w_ref[p])
        o_ref[...] = acc

    return pl.pallas_call(
        body, name=f"mm_nt_p{P}k{K}", grid=(S // tm, N // tn),
        in_specs=[pl.BlockSpec((P, tm, K), lambda i, j: (0, i, 0)), pl.BlockSpec((P, tn, K), lambda i, j: (0, j, 0))],
        out_specs=pl.BlockSpec((tm, tn), lambda i, j: (i, j)),
        out_shape=SDS((S, N), F32),
        compiler_params=_params("parallel", "parallel"),
    )(ac, wc)


def _mm_tn_chunks(a, bc, tile_m, tile_n):
    S, M = a.shape
    P, _, N = bc.shape
    ts, tm, tn = _tile(S, 2048), _tile(M, tile_m), _tile(N, tile_n)

    def body(a_ref, b_ref, o_ref):
        @pl.when(pl.program_id(3) == 0)
        def _():
            o_ref[...] = jnp.zeros_like(o_ref)

        o_ref[...] += _dot_tn(a_ref[...], b_ref[...])

    return pl.pallas_call(
        body, name=f"mm_tn_m{M}n{N}", grid=(P, M // tm, N // tn, S // ts),
        in_specs=[pl.BlockSpec((ts, tm), lambda p, m, n, k: (k, m)), pl.BlockSpec((None, ts, tn), lambda p, m, n, k: (p, k, n))],
        out_specs=pl.BlockSpec((None, tm, tn), lambda p, m, n, k: (p, m, n)),
        out_shape=SDS((P, M, N), F32),
        compiler_params=_params("parallel", "parallel", "parallel", "arbitrary"),
    )(a, bc)


def _shift_down(x, s, row, fill):
    return jnp.where(row >= s, pltpu.roll(x, s, 0), fill)


def _shift_up(x, s, row, fill):
    n = x.shape[0]
    return jnp.where(row < n - s, pltpu.roll(x, n - s, 0), fill)


def _scan(a, b, row, scratch, up):
    scr_a, scr_b, scr_c = scratch
    n = a.shape[0]
    g = n // SUBLANES
    in_group = row & (SUBLANES - 1)

    def steps(a, b, pos, size):
        s = 1
        while s < size:
            m = (pos + s < size) if up else (pos >= s)
            b = jnp.where(m, a, 0.0) * pltpu.roll(b, a.shape[0] - s if up else s, 0) + b
            a = jnp.where(m, a * pltpu.roll(a, a.shape[0] - s if up else s, 0), a)
            s *= 2
        return a, b

    a, b = steps(a, b, in_group, SUBLANES)
    scr_a[...] = a
    scr_b[...] = b
    edge = 0 if up else SUBLANES - 1
    at = scr_a[pl.ds(edge, g, stride=SUBLANES), :]
    bt = scr_b[pl.ds(edge, g, stride=SUBLANES), :]
    group = lax.broadcasted_iota(jnp.int32, at.shape, 0)
    _, state = steps(at, bt, group, g)
    carry = jnp.where((group + 1 < g) if up else (group >= 1), pltpu.roll(state, g - 1 if up else 1, 0), 0.0)
    for k in range(SUBLANES):
        scr_c[pl.ds(k, g, stride=SUBLANES), :] = carry
    return b + a * scr_c[...]


def _conv(xl, cw_ref, cb_ref, row):
    y = cb_ref[...] + _shift_down(xl, 3, row, 0.0) * cw_ref[0:1, :]
    y = y + _shift_down(xl, 2, row, 0.0) * cw_ref[1:2, :]
    y = y + _shift_down(xl, 1, row, 0.0) * cw_ref[2:3, :]
    return y + xl * cw_ref[3:4, :]


def _lru_gates(xc, wa_ref, ba_ref, wx_ref, bx_ref, lam_ref):
    ra = _sigmoid(_dot(xc, wa_ref[...]) + ba_ref[...])
    ri = _sigmoid(_dot(xc, wx_ref[...]) + bx_ref[...])
    ls = jax.nn.log_sigmoid(lam_ref[...])
    a = jnp.exp((RG_LRU_C * ra) * ls)
    mult = jnp.sqrt(1.0 - a * a)
    return ra, ri, ls, a, mult


def _lru_specs(S):
    col = lambda off: pl.BlockSpec((S, LANES), lambda j: (0, off + j))
    vec = pl.BlockSpec((1, LANES), lambda j: (0, j))
    blk = pl.BlockSpec((None, LANES, LANES), lambda j: (j, 0, 0))
    cw = pl.BlockSpec((CONV_WIDTH, LANES), lambda j: (0, j))
    return col, vec, blk, cw


def _lru_fwd(proj, cw, cb, wa, ba, wx, bx, lam):
    S = proj.shape[0]
    W = cb.shape[1]
    nb = W // LANES

    def body(xl_ref, gl_ref, cw_ref, cb_ref, wa_ref, ba_ref, wx_ref, bx_ref, lam_ref, y_ref, *scratch):
        row = lax.broadcasted_iota(jnp.int32, (S, LANES), 0)
        xc = _conv(xl_ref[...], cw_ref, cb_ref, row)
        _, ri, _, a, mult = _lru_gates(xc, wa_ref, ba_ref, wx_ref, bx_ref, lam_ref)
        h = _scan(a, mult * (ri * xc), row, scratch, up=False)
        y_ref[...] = h * _gelu(gl_ref[...])

    col, vec, blk, cws = _lru_specs(S)
    return pl.pallas_call(
        body, name="lru_fwd", grid=(nb,),
        in_specs=[col(0), col(nb), cws, vec, blk, vec, blk, vec, vec],
        out_specs=pl.BlockSpec((S, LANES), lambda j: (0, j)),
        out_shape=SDS((S, W), F32), scratch_shapes=[pltpu.VMEM((S, LANES), F32)] * 3, compiler_params=_params("parallel"),
    )(proj, proj, cw, cb, wa, ba, wx, bx, lam)


def _lru_bwd(proj, dy, cw, cb, wa, ba, wx, bx, lam, wat, wxt):
    S = proj.shape[0]
    W = cb.shape[1]
    nb = W // LANES

    def body(xl_ref, gl_ref, dy_ref, cw_ref, cb_ref, wa_ref, ba_ref, wx_ref, bx_ref, lam_ref, wat_ref, wxt_ref,
             dp_ref, dwa_ref, dwx_ref, vec_ref, *scratch):
        row = lax.broadcasted_iota(jnp.int32, (S, LANES), 0)
        xl = xl_ref[...]
        xc = _conv(xl, cw_ref, cb_ref, row)
        ra, ri, ls, a, mult = _lru_gates(xc, wa_ref, ba_ref, wx_ref, bx_ref, lam_ref)
        h = _scan(a, mult * (ri * xc), row, scratch, up=False)
        gl = gl_ref[...]
        dyv = dy_ref[...]
        dp_ref[1] = (dyv * h * _gelu_grad(gl)).astype(ACT_DTYPE)
        adj = _scan(_shift_up(a, 1, row, 0.0), dyv * _gelu(gl), row, scratch, up=True)
        da = adj * _shift_down(h, 1, row, 0.0)
        dmult = adj * (ri * xc)
        dlog_a = da * a - dmult * (a * a) / mult
        dra = dlog_a * (RG_LRU_C * ls)
        dpa = dra * ra * (1.0 - ra)
        dpi = (adj * mult * xc) * ri * (1.0 - ri)
        dxc = adj * mult * ri + _dot(dpa, wat_ref[...]) + _dot(dpi, wxt_ref[...])
        dwa_ref[...] = _dot_tn(xc, dpa)
        dwx_ref[...] = _dot_tn(xc, dpi)
        dxl = dxc * cw_ref[3:4, :]
        dxl = dxl + _shift_up(dxc, 1, row, 0.0) * cw_ref[2:3, :]
        dxl = dxl + _shift_up(dxc, 2, row, 0.0) * cw_ref[1:2, :]
        dxl = dxl + _shift_up(dxc, 3, row, 0.0) * cw_ref[0:1, :]
        dp_ref[0] = dxl.astype(ACT_DTYPE)
        vec_ref[...] = jnp.zeros_like(vec_ref)
        vec_ref[0:1, :] = _rowsum(dpa)
        vec_ref[1:2, :] = _rowsum(dpi)
        vec_ref[2:3, :] = _rowsum(dlog_a * (RG_LRU_C * ra)) * _sigmoid(-lam_ref[...])
        vec_ref[3:4, :] = _rowsum(dxc)
        vec_ref[4:5, :] = _rowsum(dxc * _shift_down(xl, 3, row, 0.0))
        vec_ref[5:6, :] = _rowsum(dxc * _shift_down(xl, 2, row, 0.0))
        vec_ref[6:7, :] = _rowsum(dxc * _shift_down(xl, 1, row, 0.0))
        vec_ref[7:8, :] = _rowsum(dxc * xl)

    col, vec, blk, cws = _lru_specs(S)
    return pl.pallas_call(
        body, name="lru_bwd", grid=(nb,),
        in_specs=[col(0), col(nb), col(0), cws, vec, blk, vec, blk, vec, vec, blk, blk],
        out_specs=[pl.BlockSpec((2, S, LANES), lambda j: (0, 0, j)), blk, blk, pl.BlockSpec((2 * SUBLANES, LANES), lambda j: (0, j))],
        out_shape=[SDS((2, S, W), ACT_DTYPE), SDS((nb, LANES, LANES), F32), SDS((nb, LANES, LANES), F32), SDS((2 * SUBLANES, W), F32)],
        scratch_shapes=[pltpu.VMEM((S, LANES), F32)] * 3, compiler_params=_params("parallel"),
    )(proj, proj, dy, cw, cb, wa, ba, wx, bx, lam, wat, wxt)


def _seg_mean(x, seg_ref, width):
    hi = x.astype(jnp.bfloat16)
    lo = (x - hi.astype(F32)).astype(jnp.bfloat16)
    ones = seg_ref[...]
    s = jnp.dot(hi, ones, preferred_element_type=F32) + jnp.dot(lo, ones, preferred_element_type=F32)
    return s * (1.0 / width)


def _gmlp_core(u_ref, v_ref, gv_ref, seg_ref, ws_ref, bfull_ref, z_scr, hd):
    tm, W = u_ref.shape
    lane = lax.broadcasted_iota(jnp.int32, (CHUNK, LANES), 1)
    ug = _gelu(u_ref[...])
    vg = _gelu(v_ref[...])
    cen = vg - _seg_mean(vg, seg_ref, hd)
    rstd = lax.rsqrt(_seg_mean(cen * cen, seg_ref, hd) + EPS)
    vhat = cen * rstd
    vh = vhat * gv_ref[...]
    vcats = {}
    for ci in range(tm // CHUNK):
        for p in range(W // LANES):
            blk = vh[ci * CHUNK:(ci + 1) * CHUNK, p * LANES:(p + 1) * LANES]
            vcat = jnp.concatenate([jnp.where(lane < hd, blk, 0.0), jnp.where(lane >= hd, blk, 0.0)], axis=0).astype(MXU_DTYPE)
            vcats[ci, p] = vcat
            z_scr[ci * CHUNK:(ci + 1) * CHUNK, p * LANES:(p + 1) * LANES] = (
                jnp.dot(ws_ref[p], vcat, preferred_element_type=F32) + bfull_ref[:, p * LANES:(p + 1) * LANES])
    return ug, vhat, rstd, vcats


def _gmlp_specs(tm, W, nb):
    rows = lambda off: pl.BlockSpec((tm, W), lambda i: (i, off))
    vec = pl.BlockSpec((1, W), lambda i: (0, 0))
    seg = pl.BlockSpec((W, W), lambda i: (0, 0))
    wsp = pl.BlockSpec((nb, CHUNK, 2 * CHUNK), lambda i: (0, 0, 0))
    bfull = pl.BlockSpec((CHUNK, W), lambda i: (0, 0))
    return rows, vec, seg, wsp, bfull


def _gmlp_fwd(proj, ylru, gv, seg, wsp, bfull, g_lru, g_gm):
    S, W = ylru.shape
    nb = W // LANES
    hd = W // HEADS
    tm = _tile(S, 512)

    def body(u_ref, v_ref, yl_ref, gv_ref, seg_ref, ws_ref, bfull_ref, gl_ref, gg_ref, yn_ref, ygm_ref, z_scr):
        ug, _, _, _ = _gmlp_core(u_ref, v_ref, gv_ref, seg_ref, ws_ref, bfull_ref, z_scr, hd)
        ygm = ug * z_scr[...]
        ygm_ref[...] = ygm
        yl = yl_ref[...]
        yn_ref[:, 0:W] = (yl * _rsqrt_ms(yl) * gl_ref[...]).astype(ACT_DTYPE)
        yn_ref[:, W:2 * W] = (ygm * _rsqrt_ms(ygm) * gg_ref[...]).astype(ACT_DTYPE)

    rows, vec, segs, wsps, bfulls = _gmlp_specs(tm, W, nb)
    return pl.pallas_call(
        body, name="gmlp_fwd", grid=(S // tm,),
        in_specs=[rows(2), rows(3), rows(0), vec, segs, wsps, bfulls, vec, vec],
        out_specs=[pl.BlockSpec((tm, 2 * W), lambda i: (i, 0)), rows(0)],
        out_shape=[SDS((S, 2 * W), ACT_DTYPE), SDS((S, W), F32)],
        scratch_shapes=[pltpu.VMEM((tm, W), F32)],
        compiler_params=_params("parallel"),
    )(proj, proj, ylru, gv, seg, wsp, bfull, g_lru, g_gm)


def _rms_bwd(y, g, dyn):
    r = _rsqrt_ms(y)
    yhat = y * r
    dyh = dyn * g
    return r * (dyh - yhat * jnp.mean(dyh * yhat, axis=-1, keepdims=True)), _rowsum(dyn * yhat)


def _gmlp_bwd(proj, ylru, ygm, dyn, gv, seg, wsp, wspt, bfull, g_lru, g_gm):
    S, W = ylru.shape
    nb = W // LANES
    hd = W // HEADS
    tm = _tile(S, 256)

    def body(u_ref, v_ref, yl_ref, ygm_ref, dl_ref, dg_ref, gv_ref, seg_ref, ws_ref, wst_ref, bfull_ref, gl_ref, gg_ref,
             dyl_ref, duv_ref, dws_ref, dbf_ref, acc_ref, z_scr, dvh_scr):
        @pl.when(pl.program_id(0) == 0)
        def _():
            dws_ref[...] = jnp.zeros_like(dws_ref)
            dbf_ref[...] = jnp.zeros_like(dbf_ref)
            acc_ref[...] = jnp.zeros_like(acc_ref)

        dyl, dgl = _rms_bwd(yl_ref[...], gl_ref[...], dl_ref[...])
        dyl_ref[...] = dyl
        dygm, dgg = _rms_bwd(ygm_ref[...], gg_ref[...], dg_ref[...])
        ug, vhat, rstd, vcats = _gmlp_core(u_ref, v_ref, gv_ref, seg_ref, ws_ref, bfull_ref, z_scr, hd)
        duv_ref[0] = (dygm * z_scr[...] * _gelu_grad(u_ref[...])).astype(ACT_DTYPE)
        dz = dygm * ug
        lane = lax.broadcasted_iota(jnp.int32, (CHUNK, LANES), 1)
        dbf = dz[0:CHUNK, :]
        for ci in range(1, tm // CHUNK):
            dbf += dz[ci * CHUNK:(ci + 1) * CHUNK, :]
        dbf_ref[...] += dbf
        for ci in range(tm // CHUNK):
            for p in range(nb):
                dzb = dz[ci * CHUNK:(ci + 1) * CHUNK, p * LANES:(p + 1) * LANES].astype(MXU_DTYPE)
                dws_ref[p] += _dot_nt(dzb, vcats[ci, p])
                dvc = jnp.dot(wst_ref[p], dzb, preferred_element_type=F32)
                dvh_scr[ci * CHUNK:(ci + 1) * CHUNK, p * LANES:(p + 1) * LANES] = jnp.where(lane < hd, dvc[0:CHUNK], dvc[CHUNK:2 * CHUNK])
        dvh = dvh_scr[...]
        dvn = dvh * gv_ref[...]
        dvg = rstd * (dvn - _seg_mean(dvn, seg_ref, hd) - vhat * _seg_mean(dvn * vhat, seg_ref, hd))
        duv_ref[1] = (dvg * _gelu_grad(v_ref[...])).astype(ACT_DTYPE)
        acc_ref[0:1, :] += dgl
        acc_ref[1:2, :] += dgg
        acc_ref[2:3, :] += _rowsum(dvh * vhat)

    rows, vec, segs, wsps, bfulls = _gmlp_specs(tm, W, nb)
    wspt_spec = pl.BlockSpec((nb, 2 * CHUNK, CHUNK), lambda i: (0, 0, 0))
    return pl.pallas_call(
        body, name="gmlp_bwd", grid=(S // tm,),
        in_specs=[rows(2), rows(3), rows(0), rows(0), rows(0), rows(1), vec, segs, wsps, wspt_spec, bfulls, vec, vec],
        out_specs=[rows(0), pl.BlockSpec((2, tm, W), lambda i: (0, i, 0)), wsps, bfulls, pl.BlockSpec((SUBLANES, W), lambda i: (0, 0))],
        out_shape=[SDS((S, W), F32), SDS((2, S, W), ACT_DTYPE), SDS((nb, CHUNK, 2 * CHUNK), F32), SDS((CHUNK, W), F32), SDS((SUBLANES, W), F32)],
        scratch_shapes=[pltpu.VMEM((tm, W), F32), pltpu.VMEM((tm, W), F32)],
        compiler_params=_params("arbitrary"),
    )(proj, proj, ylru, ygm, dyn, dyn, gv, seg, wsp, wspt, bfull, g_lru, g_gm)


def _ada_fwd(c_all, w_ada, b_shard):
    L, D, N = w_ada.shape
    R = c_all.shape[0]
    tn = N // 2

    def body(c_ref, w_ref, b_ref, o_ref):
        cv = c_ref[...]
        o_ref[...] = _dot(cv * _sigmoid(cv), w_ref[...]) + b_ref[...]

    return pl.pallas_call(
        body, name="ada_fwd", grid=(L, N // tn),
        in_specs=[pl.BlockSpec((R, D), lambda l, j: (0, 0)), pl.BlockSpec((None, D, tn), lambda l, j: (l, 0, j)),
                  pl.BlockSpec((None, 1, tn), lambda l, j: (l, 0, j))],
        out_specs=pl.BlockSpec((None, R, tn), lambda l, j: (l, 0, j)),
        out_shape=SDS((L, R, N), F32), compiler_params=_params("parallel", "parallel"),
    )(c_all, w_ada, b_shard)


def _ada_step(c_all_t, dmod, w, m, v):
    D, B = c_all_t.shape
    L, _, N = dmod.shape
    tn = 3 * LANES
    assert N % tn == 0, N

    def body(c_ref, d_ref, w_ref, m_ref, v_ref, g_ref, do_ref, mo_ref, vo_ref):
        cv = c_ref[...]
        sc = cv * _sigmoid(cv)
        g = sc[:, 0:1] * d_ref[0:1, :]
        for b in range(1, B):
            g += sc[:, b:b + 1] * d_ref[b:b + 1, :]
        g_ref[...] = g
        do_ref[...], mo_ref[...], vo_ref[...] = _adam_math(w_ref[...], g, m_ref[...], v_ref[...])

    tile = pl.BlockSpec((None, D, tn), lambda l, j: (l, 0, j))
    return pl.pallas_call(
        body, name="ada_step", grid=(L, N // tn),
        in_specs=[pl.BlockSpec((D, B), lambda l, j: (0, 0)), pl.BlockSpec((None, B, tn), lambda l, j: (l, 0, j)), tile, tile, tile],
        out_specs=[tile] * 4, out_shape=[SDS((L, D, N), F32)] * 4, compiler_params=_params("parallel", "parallel"),
    )(c_all_t, dmod, w, m, v)


def _adam_math(w, g, m, v):
    mn = ADAM_B1 * m + (1.0 - ADAM_B1) * g
    vn = ADAM_B2 * v + (1.0 - ADAM_B2) * (g * g)
    m_hat = mn / (1.0 - ADAM_B1 ** ADAM_STEP)
    v_hat = vn / (1.0 - ADAM_B2 ** ADAM_STEP)
    return -ADAM_LR * (m_hat / (jnp.sqrt(v_hat) + ADAM_EPS) + ADAM_WD * w), mn, vn


def _adamw_layer(w, g, m, v, l, prev, after):
    L, R, C = w.shape
    tr = _row_tile(R, C * 4)
    prev = (after,) + (() if prev is None else tuple(prev))

    def body(w_ref, g_ref, m_ref, v_ref, *rest):
        go_ref, d_ref, mo_ref, vo_ref = rest[len(prev):]
        gv = g_ref[...]
        go_ref[...] = gv
        d_ref[...], mo_ref[...], vo_ref[...] = _adam_math(w_ref[...], gv, m_ref[...], v_ref[...])

    lay = pl.BlockSpec((None, tr, C), lambda i: (l, i, 0))
    return pl.pallas_call(
        body, name=f"adamw_layer_r{R}c{C}", grid=(R // tr,),
        in_specs=[lay, pl.BlockSpec((tr, C), lambda i: (i, 0)), lay, lay] + [ANY] * len(prev), out_specs=[lay] * 4,
        out_shape=[SDS((L, R, C), F32)] * 4, input_output_aliases={5 + k: k for k in range(len(prev) - 1)},
        compiler_params=_params("parallel"),
    )(w, g, m, v, *prev)


def _adamw_small(ws, gs, ms, vs):
    n = len(ws)

    def body(*refs):
        outs = refs[4 * n:]
        for k in range(n):
            outs[k][...], outs[n + k][...], outs[2 * n + k][...] = _adam_math(
                refs[k][...], refs[n + k][...], refs[2 * n + k][...], refs[3 * n + k][...])

    whole = pl.BlockSpec(memory_space=pltpu.VMEM)
    return pl.pallas_call(
        body, name="adamw_small", in_specs=[whole] * (4 * n), out_specs=[whole] * (3 * n),
        out_shape=[SDS(w.shape, F32) for w in ws] * 3, compiler_params=pltpu.CompilerParams(vmem_limit_bytes=VMEM_LIMIT_BYTES),
    )(*ws, *gs, *ms, *vs)


def _sum_leading(a):
    P, R, C = a.shape
    tr = _row_tile(R, P * C * 4)

    def body(a_ref, o_ref):
        acc = a_ref[0]
        for p in range(1, P):
            acc = acc + a_ref[p]
        o_ref[...] = acc

    return pl.pallas_call(
        body, name=f"sum{P}_r{R}c{C}", grid=(R // tr,),
        in_specs=[pl.BlockSpec((P, tr, C), lambda i: (0, i, 0))],
        out_specs=pl.BlockSpec((tr, C), lambda i: (i, 0)),
        out_shape=SDS((R, C), F32), compiler_params=_params("parallel"),
    )(a)


def _add_half(g4, r1, place):
    _, _, R, C = g4.shape
    tr = _row_tile(R, C * 4)

    def body(place_ref, g_ref, r_ref, h_ref, own_ref):
        s = (g_ref[...] + r_ref[...]).astype(XFER_DTYPE)
        h_ref[...] = s

        @pl.when(pl.program_id(1) == place_ref[1])
        def _():
            own_ref[...] = s

    return pl.pallas_call(
        body, name=f"add_half_r{R}c{C}",
        grid_spec=pltpu.PrefetchScalarGridSpec(
            num_scalar_prefetch=1, grid=(R // tr, N_CHIPS),
            in_specs=[pl.BlockSpec((None, None, tr, C), lambda i, p, place_ref: (p, place_ref[0], i, 0)),
                      pl.BlockSpec((None, tr, C), lambda i, p, place_ref: (p, i, 0))],
            out_specs=[pl.BlockSpec((None, tr, C), lambda i, p, place_ref: (p, i, 0)),
                       pl.BlockSpec((None, tr, C), lambda i, p, place_ref: (place_ref[1], i, 0))],
        ),
        out_shape=[SDS((N_CHIPS, R, C), XFER_DTYPE)] * 2, compiler_params=_params("parallel", "arbitrary"),
    )(place, g4, r1)


def _sum4_into_half(r2, place):
    P, R, C = r2.shape
    tr = _row_tile(R, P * C * 4)

    def body(place_ref, a_ref, o_ref):
        acc = a_ref[0].astype(F32)
        for p in range(1, P):
            acc = acc + a_ref[p].astype(F32)
        o_ref[...] = acc

    return pl.pallas_call(
        body, name=f"sum4_r{R}c{C}",
        grid_spec=pltpu.PrefetchScalarGridSpec(
            num_scalar_prefetch=1, grid=(R // tr,),
            in_specs=[pl.BlockSpec((P, tr, C), lambda i, place_ref: (0, i, 0))],
            out_specs=pl.BlockSpec((None, tr, C), lambda i, place_ref: (place_ref[0], i, 0)),
        ),
        out_shape=SDS((2, R, C), F32), compiler_params=_params("parallel"),
    )(place, r2)


def _cast_into_slot(w, l, place, after):
    _, R, C = w.shape
    tr = _row_tile(R, C * 4)

    def body(place_ref, w_ref, after_ref, o_ref):
        o_ref[...] = w_ref[...].astype(MXU_DTYPE)

    return pl.pallas_call(
        body, name=f"cast_r{R}c{C}",
        grid_spec=pltpu.PrefetchScalarGridSpec(
            num_scalar_prefetch=1, grid=(R // tr,),
            in_specs=[pl.BlockSpec((None, tr, C), lambda i, place_ref: (l, i, 0)), ANY],
            out_specs=pl.BlockSpec((None, tr, C), lambda i, place_ref: (place_ref[1], i, 0)),
        ),
        out_shape=SDS((N_CHIPS, R, C), MXU_DTYPE), compiler_params=_params("parallel"),
    )(place, w, after)


def _place():
    x, y, c = lax.axis_index("x"), lax.axis_index("y"), lax.axis_index("c")
    chips = [(1 - x, y), (x, 1 - y), (1 - x, 1 - y)]
    return x, y, c, chips


def _remote(src, dst, send_sem, recv_sem, to):
    return pltpu.make_async_remote_copy(src_ref=src, dst_ref=dst, send_sem=send_sem, recv_sem=recv_sem, device_id=to, device_id_type=MESH)


def _all_gather8(v):
    R, N = v.shape

    def body(v_ref, out_ref, send_sems, recv_sems, local_sem):
        x, y, c, chips = _place()
        me, sibling = (x, y, c), (x, y, 1 - c)

        def slot(px, py, pc):
            return out_ref.at[4 * px + 2 * py + pc]

        def copy(k, block, to, src=None):
            return _remote(slot(*block) if src is None else src, slot(*block), send_sems.at[k], recv_sems.at[k], to)

        mine = pltpu.make_async_copy(v_ref, slot(*me), local_sem)
        mine.start()
        first = [copy(0, me, sibling, src=v_ref)] + [copy(1 + j, me, (*chip, c), src=v_ref) for j, chip in enumerate(chips)]
        for cp in first:
            cp.start()
        passed = [copy(4 + j, (*chip, c), sibling) for j, chip in enumerate(chips)]
        for j, chip in enumerate(chips):
            copy(1 + j, (*chip, c), me).wait_recv()
            passed[j].start()
        copy(0, sibling, me).wait_recv()
        for j, chip in enumerate(chips):
            copy(4 + j, (*chip, 1 - c), me).wait_recv()
        for cp in first + passed:
            cp.wait_send()
        mine.wait()

    return pl.pallas_call(
        body, name=f"all_gather8_r{R}n{N}", out_shape=SDS((N_DEV, R, N), v.dtype), in_specs=[ANY], out_specs=ANY,
        scratch_shapes=[pltpu.SemaphoreType.DMA((7,)), pltpu.SemaphoreType.DMA((7,)), pltpu.SemaphoreType.DMA],
    )(v)


HBM_SPEC = pl.BlockSpec(memory_space=pltpu.HBM)
SEM_SPEC = pl.BlockSpec(memory_space=pltpu.SEMAPHORE)
DATAFLOW = pltpu.SideEffectType.DATAFLOW_SIDE_EFFECTING


def _in_hbm(a):
    return pltpu.with_memory_space_constraint(a, pltpu.HBM)


def _hbm_like(a):
    return pltpu.HBM(a.shape, a.dtype)


def _split_start(name, arrays, n_sems, issue, extra=()):
    m = len(arrays)

    def body(*refs):
        issue(refs[:m], refs[m + len(extra)], refs[m + len(extra) + 1])

    out = pl.pallas_call(
        body, name=name,
        out_shape=(pltpu.SemaphoreType.DMA((n_sems,)), pltpu.SemaphoreType.DMA((n_sems,)), *[_hbm_like(a) for a in arrays]),
        in_specs=[HBM_SPEC] * m + [ANY] * len(extra), out_specs=(SEM_SPEC, SEM_SPEC, *[HBM_SPEC] * m),
        input_output_aliases={k: 2 + k for k in range(m)},
        compiler_params=pltpu.CompilerParams(has_side_effects=DATAFLOW),
    )(*[_in_hbm(a) for a in arrays], *extra)
    return out[0], out[1], list(out[2:])


def _split_wait(name, send_sems, recv_sems, arrays, after, drain):
    m = len(arrays)

    def body(*refs):
        drain(refs[:m], refs[m], refs[m + 1])

    out = pl.pallas_call(
        body, name=name, out_shape=[_hbm_like(a) for a in arrays],
        in_specs=[HBM_SPEC] * m + [SEM_SPEC, SEM_SPEC, ANY], out_specs=[HBM_SPEC] * m,
        input_output_aliases={k: k for k in range(m)},
        compiler_params=pltpu.CompilerParams(has_side_effects=DATAFLOW),
    )(*arrays, send_sems, recv_sems, after)
    return list(out)


def _wait_both(cp):
    cp.wait_send()
    cp.wait_recv()


class _Flight:
    def __init__(self, name, arrays, n_sems, issue, drain, thru, extra=()):
        self.name, self.drain, self.n = name, drain, len(arrays)
        self.send, self.recv, out = _split_start(name + "_start", [*arrays, thru], n_sems, issue, extra)
        self.arrays, self.thru = out[:-1], out[-1]

    def land(self, after):
        return _split_wait(self.name + "_wait", self.send, self.recv, self.arrays, after, self.drain)


def _gather_flight(tag, ici, d2d, direct, thru):
    kinds = ["ici"] * len(ici) + ["d2d"] * len(d2d) + ["direct"] * len(direct)
    rows = [pl.ds(part * (a.shape[2] // parts), a.shape[2] // parts) for a, part, parts in ici]
    ici = [a for a, _, _ in ici]

    def issue(refs, send_sems, recv_sems):
        x, y, c, chips = _place()
        q = 2 * x + y
        for k, kind in enumerate(kinds):
            for j, chip in enumerate(chips):
                if kind == "ici":
                    src, to = refs[k].at[q, c, rows[k]], (*chip, c)
                elif kind == "d2d":
                    src, to = refs[k].at[2 * chip[0] + chip[1], c], (x, y, 1 - c)
                else:
                    src, to = refs[k].at[q], (*chip, c)
                _remote(src, src, send_sems.at[3 * k + j], recv_sems.at[3 * k + j], to).start()

    def drain(refs, send_sems, recv_sems):
        x, y, c, chips = _place()
        for k, kind in enumerate(kinds):
            for j, chip in enumerate(chips):
                p = 2 * chip[0] + chip[1]
                got = refs[k].at[p] if kind == "direct" else refs[k].at[p, c, rows[k]] if kind == "ici" else refs[k].at[p, 1 - c]
                _wait_both(_remote(got, got, send_sems.at[3 * k + j], recv_sems.at[3 * k + j], (x, y, c)))

    return _Flight(f"gather{tag}", [*ici, *d2d, *direct], 3 * len(kinds), issue, drain, thru)


def _swap_flight(tag, g4s, thru):
    n = len(g4s)
    zones = [lax.empty((N_CHIPS,) + g.shape[2:], g.dtype) for g in g4s]

    def issue(refs, send_sems, recv_sems):
        x, y, c, _ = _place()
        for k in range(n):
            for p in range(N_CHIPS):
                _remote(refs[k].at[p, 1 - c], refs[n + k].at[p], send_sems.at[N_CHIPS * k + p], recv_sems.at[N_CHIPS * k + p], (x, y, 1 - c)).start()

    def drain(refs, send_sems, recv_sems):
        x, y, c, _ = _place()
        for k in range(n):
            for p in range(N_CHIPS):
                got = refs[n + k].at[p]
                _wait_both(_remote(got, got, send_sems.at[N_CHIPS * k + p], recv_sems.at[N_CHIPS * k + p], (x, y, c)))

    return _Flight(f"swap{tag}", [*g4s, *zones], N_CHIPS * n, issue, drain, thru)


def _scatter_flight(tag, hs, lands, thru):
    n = len(hs)

    def issue(refs, send_sems, recv_sems):
        x, y, c, chips = _place()
        q = 2 * x + y
        for k in range(n):
            for j, chip in enumerate(chips):
                _remote(refs[k].at[2 * chip[0] + chip[1]], refs[n + k].at[q], send_sems.at[3 * k + j], recv_sems.at[3 * k + j], (*chip, c)).start()

    def drain(refs, send_sems, recv_sems):
        x, y, c, chips = _place()
        for k in range(n):
            for j, chip in enumerate(chips):
                got = refs[n + k].at[2 * chip[0] + chip[1]]
                _wait_both(_remote(got, got, send_sems.at[3 * k + j], recv_sems.at[3 * k + j], (x, y, c)))

    return _Flight(f"scatter{tag}", [*hs, *lands], 3 * n, issue, drain, thru)


def _exchange_flight(tag, buf, thru):
    flips = [(fx, fy, fc) for fx in (0, 1) for fy in (0, 1) for fc in (0, 1)][1:]

    def peers():
        x, y, c, _ = _place()
        return (x, y, c), [((1 - x) if fx else x, (1 - y) if fy else y, (1 - c) if fc else c) for fx, fy, fc in flips]

    def slot(ref, dev):
        return ref.at[4 * dev[0] + 2 * dev[1] + dev[2]]

    def issue(refs, send_sems, recv_sems):
        me, others = peers()
        for j, to in enumerate(others):
            _remote(slot(refs[0], me), slot(refs[0], me), send_sems.at[j], recv_sems.at[j], to).start()

    def drain(refs, send_sems, recv_sems):
        me, others = peers()
        for j, frm in enumerate(others):
            got = slot(refs[0], frm)
            _wait_both(_remote(got, got, send_sems.at[j], recv_sems.at[j], me))

    return _Flight(f"exchange{tag}", [buf], len(flips), issue, drain, thru)


def _share_flight(tag, fins, thru):
    n = len(fins)

    def issue(refs, send_sems, recv_sems):
        x, y, c, _ = _place()
        for k in range(n):
            _remote(refs[k].at[c], refs[k].at[c], send_sems.at[k], recv_sems.at[k], (x, y, 1 - c)).start()

    def drain(refs, send_sems, recv_sems):
        x, y, c, _ = _place()
        for k in range(n):
            got = refs[k].at[1 - c]
            _wait_both(_remote(got, got, send_sems.at[k], recv_sems.at[k], (x, y, c)))

    return _Flight(f"share{tag}", fins, n, issue, drain, thru)


def _pair_blocks(w):
    h, d, _ = w.shape
    z = jnp.zeros((h // 2, d, d), w.dtype)
    return jnp.concatenate([jnp.concatenate([w[0::2], z], axis=2), jnp.concatenate([z, w[1::2]], axis=2)], axis=1)


def _unpair_blocks(b):
    n, dd, _ = b.shape
    d = dd // 2
    return jnp.stack([b[:, :d, :d], b[:, d:, d:]], axis=1).reshape(2 * n, d, d)


def _pad_rows(a, rows):
    return jnp.pad(a, ((0, rows - a.shape[0]), (0, 0)))


class _Packer:
    def __init__(self, shapes, width=1024, row_multiple=64):
        self.shapes = shapes
        self.sizes = [math.prod(s) for s in shapes]
        total = sum(self.sizes)
        self.width = width
        self.rows = -(-total // (width * row_multiple)) * row_multiple
        self.pad = self.rows * width - total

    def pack(self, arrays):
        flat = jnp.concatenate([a.reshape(-1).astype(F32) for a in arrays] + [jnp.zeros((self.pad,), F32)])
        return flat.reshape(self.rows, self.width)

    def unpack(self, packed):
        flat = packed.reshape(-1)
        out, off = [], 0
        for s, n in zip(self.shapes, self.sizes):
            out.append(flat[off:off + n].reshape(s))
            off += n
        return out


SMALL = ["b_ada", "ffn1_norm", "mix_norm", "conv_w", "conv_b", "gate_a_w", "gate_a_b", "gate_x_w", "gate_x_b", "lru_lambda",
         "v_norm", "spatial_w", "spatial_b", "lru_out_norm", "gmlp_out_norm", "ffn2_norm", "final_norm"]
BIG = ["ffn1_w_gu", "ffn1_w_down", "w_in", "w_out", "ffn2_w_gu", "ffn2_w_down"]
GROUPS = (("ffn1_w_gu", "ffn1_w_down"), ("w_in", "w_out"), ("ffn2_w_gu", "ffn2_w_down"))
FWD_GROUPS = (("ffn1_w_gu",), ("ffn1_w_down",), ("w_in", "w_out"), ("ffn2_w_gu",), ("ffn2_w_down",))
MIN_AGE = {"swap": 1, "scatter": 1, "share": 1}
WEIGHTS = ["w_ada", "b_ada", "ffn1_norm", "ffn1_w_gu", "ffn1_w_down", "mix_norm", "w_in", "conv_w", "conv_b", "gate_a_w", "gate_a_b",
           "gate_x_w", "gate_x_b", "lru_lambda", "v_norm", "spatial_w", "spatial_b", "lru_out_norm", "gmlp_out_norm", "w_out",
           "ffn2_norm", "ffn2_w_gu", "ffn2_w_down", "final_norm"]


def kernel(x, c, w_ada, b_ada, ffn1_norm, ffn1_w_gu, ffn1_w_down, mix_norm, w_in, conv_w, conv_b, gate_a_w, gate_a_b, gate_x_w, gate_x_b, lru_lambda, v_norm, spatial_w, spatial_b, lru_out_norm, gmlp_out_norm, w_out, ffn2_norm, ffn2_w_gu, ffn2_w_down, final_norm, loss_target, m_w_ada, m_b_ada, m_ffn1_norm, m_ffn1_w_gu, m_ffn1_w_down, m_mix_norm, m_w_in, m_conv_w, m_conv_b, m_gate_a_w, m_gate_a_b, m_gate_x_w, m_gate_x_b, m_lru_lambda, m_v_norm, m_spatial_w, m_spatial_b, m_lru_out_norm, m_gmlp_out_norm, m_w_out, m_ffn2_norm, m_ffn2_w_gu, m_ffn2_w_down, m_final_norm, v_w_ada, v_b_ada, v_ffn1_norm, v_ffn1_w_gu, v_ffn1_w_down, v_mix_norm, v_w_in, v_conv_w, v_conv_b, v_gate_a_w, v_gate_a_b, v_gate_x_w, v_gate_x_b, v_lru_lambda, v_v_norm, v_spatial_w, v_spatial_b, v_lru_out_norm, v_gmlp_out_norm, v_w_out, v_ffn2_norm, v_ffn2_w_gu, v_ffn2_w_down, v_final_norm):
    given = dict(locals())
    W = {n: given[n] for n in WEIGHTS}
    L = w_ada.shape[0]
    S, D = x.shape[1], x.shape[2]
    LW = conv_b.shape[1]
    hd = LW // HEADS
    xi, yi, ci = lax.axis_index("x"), lax.axis_index("y"), lax.axis_index("c")
    chip = 2 * xi + yi
    dev = 2 * chip + ci
    place = jnp.stack([ci, chip]).astype(jnp.int32)
    xs = x.reshape(S, D)
    tgt = loss_target.reshape(S, D)

    c_all = _all_gather8(_pad_rows(c, SUBLANES))[:, 0, :]
    n_ada = w_ada.shape[2]
    b_shard = lax.dynamic_slice_in_dim(b_ada, chip * n_ada, n_ada, axis=1)
    mod_shard = _ada_fwd(_pad_rows(c_all, 2 * SUBLANES), w_ada, b_shard[:, None, :])

    def in_slot(block):
        return lax.dynamic_update_index_in_dim(jnp.zeros((N_CHIPS,) + block.shape, block.dtype), block, chip, 0)

    cws = LW // N_CHIPS
    small = [in_slot(mod_shard.reshape(L * 2 * SUBLANES, n_ada)), in_slot(conv_w.reshape(L * CONV_WIDTH, cws))]

    def half_view(s):
        return s.reshape(N_CHIPS, 2, s.shape[1] // 2, s.shape[2])

    stages = [(l, names) for l in range(L) for names in FWD_GROUPS]
    seq = [[half_view(_cast_into_slot(W[n], l, place, place)) for n in names] for l, names in stages[:1]]
    flights = {}

    def ici_plan(t):
        if t >= len(stages):
            return []
        stage = t % len(FWD_GROUPS)
        return {2: [(t, 0, 1), (t + 1, 0, 2)], 3: [(t, 1, 2)]}.get(stage, [(t, 0, 1)])

    def launch(t, thru, direct=()):
        ici = [(a, part, parts) for g, part, parts in ici_plan(t) for a in seq[g]]
        d2d = seq[t - 1] if 1 <= t <= len(seq) else []
        if ici or d2d or direct:
            flights[t] = _gather_flight(t, ici, d2d, list(direct), thru)
            thru = flights[t].thru
        return thru

    def land(t, after):
        if t not in flights:
            return []
        out = flights.pop(t).land(after)
        for g, _, _ in ici_plan(t):
            seq[g], out = out[:len(seq[g])], out[len(seq[g]):]
        if 1 <= t <= len(seq):
            seq[t - 1], out = out[:len(seq[t - 1])], out[len(seq[t - 1]):]
        return out

    def group_weights(t):
        return [s.reshape(N_CHIPS, -1, s.shape[3]) for s in seq[t]]

    c_all = launch(0, c_all, small)
    seq += [[half_view(_cast_into_slot(W[n], l, place, c_all)) for n in names] for l, names in stages[1:]]
    mod_all, conv_all = land(0, seq[-1][-1])
    mod_all = launch(1, mod_all)
    land(1, mod_all)
    mod_rows = lax.dynamic_index_in_dim(mod_all.reshape(N_CHIPS, L, 2 * SUBLANES, n_ada), dev, axis=2, keepdims=False)
    mod = mod_rows.transpose(1, 0, 2).reshape(L, N_MOD, 1, D)
    conv_full = conv_all.reshape(N_CHIPS, L, CONV_WIDTH, cws).transpose(1, 2, 0, 3).reshape(L, CONV_WIDTH, LW)

    tril = jnp.tril(jnp.ones((CHUNK, CHUNK), F32))
    seg = (jnp.arange(LW)[:, None] // hd == jnp.arange(LW)[None, :] // hd).astype(jnp.bfloat16)

    def mixer_params(l):
        ws = spatial_w[l] * tril
        wsp = jnp.concatenate([ws[0::2], ws[1::2]], axis=2)
        wa, wx = _pair_blocks(gate_a_w[l]), _pair_blocks(gate_x_w[l])
        return dict(
            cw=conv_full[l], cb=conv_b[l][None],
            wa=wa.astype(MXU_DTYPE), wx=wx.astype(MXU_DTYPE), wat=wa.transpose(0, 2, 1).astype(MXU_DTYPE), wxt=wx.transpose(0, 2, 1).astype(MXU_DTYPE),
            ba=gate_a_b[l].reshape(1, LW), bx=gate_x_b[l].reshape(1, LW), lam=lru_lambda[l][None], gv=v_norm[l][None],
            wsp=wsp.astype(MXU_DTYPE), wspt=wsp.transpose(0, 2, 1).astype(MXU_DTYPE),
            bfull=jnp.repeat(spatial_b[l].T, hd, axis=1), g_lru=lru_out_norm[l][None], g_gm=gmlp_out_norm[l][None])

    saved = []
    xcur = xs
    zero_row = jnp.zeros((1, D), F32)
    h = _modnorm(xcur, ffn1_norm[0][None], mod[0][0], mod[0][1])
    for l in range(L):
        mp, md = mixer_params(l), mod[l]
        s = dict(lw={}, mp=mp, md=md)
        lw = s["lw"]
        t = len(FWD_GROUPS) * l
        s["x0"] = xcur
        s["h1"] = launch(t + 2, h)
        lw["gu1"], = group_weights(t)
        s["a1"], s["gu1"] = _ffn_up(s["h1"], lw["gu1"])
        land(t + 2, s["a1"])
        s["a1"] = launch(t + 3, s["a1"])
        lw["d1"] = group_weights(t + 1)[0].reshape(-1, D)
        s["f1"], xcur, h = _mm_res(s["a1"], lw["d1"], xcur, md[2], 0.5, (mix_norm[l][None], md[3], md[4]))
        land(t + 3, xcur)
        s["x1"] = xcur
        s["h2"] = launch(t + 4, h)
        lw["win"], wout = group_weights(t + 2)
        lw["wout"] = wout.reshape(-1, D)
        s["proj"] = _mm_chunks(s["h2"], lw["win"])
        s["ylru"] = _lru_fwd(s["proj"], mp["cw"], mp["cb"], mp["wa"], mp["ba"], mp["wx"], mp["bx"], mp["lam"])
        s["yn"], s["ygm"] = _gmlp_fwd(s["proj"], s["ylru"], mp["gv"], seg, mp["wsp"], mp["bfull"], mp["g_lru"], mp["g_gm"])
        s["f2"], xcur, h = _mm_res(s["yn"], lw["wout"], xcur, md[5], 1.0, (ffn2_norm[l][None], md[6], md[7]))
        land(t + 4, xcur)
        s["x2"] = xcur
        s["h3"] = launch(t + 5, h)
        lw["gu2"], = group_weights(t + 3)
        s["a3"], s["gu3"] = _ffn_up(s["h3"], lw["gu2"])
        land(t + 5, s["a3"])
        s["a3"] = launch(t + 6, s["a3"])
        lw["d2"] = group_weights(t + 4)[0].reshape(-1, D)
        following = (ffn1_norm[l + 1][None], mod[l + 1][0], mod[l + 1][1]) if l + 1 < L else (final_norm[None], zero_row, zero_row)
        s["f3"], xcur, h = _mm_res(s["a3"], lw["d2"], xcur, md[8], 0.5, following)
        land(t + 6, xcur)
        saved.append(s)

    dx, dq, head_acc = _loss_head(xcur, tgt, final_norm[None], saved[-1]["md"][8], 0.5)
    loss = lax.psum(jnp.sum(head_acc[1]), ("x", "y", "c"))
    big_grads = {n: [None] * L for n in BIG}
    dmods = [None] * L

    def ffn_bwd(names, l, dx, dq, x_in, h, a, gu, f, wgu, wd, gn, sc, next_gate, next_scale):
        big_grads[names[1]][l] = _mm_tn_chunks(a, dq[None], 1408, 1024)[0].reshape(N_CHIPS, -1, D)
        dgu = _ffn_bwd_act(dq, wd, gu)
        C = dgu.shape[3]
        dgu4 = dgu.reshape(N_CHIPS, S, C)
        big_grads[names[0]][l] = _mm_tn_chunks(h, dgu4, 1024, C)
        dgu4 = reduce_group(names, l, big_grads[names[0]][l], dgu4)
        dx, dq, acc = _mm_nt_norm_bwd(dgu4, wgu, x_in, dx, f, gn, sc, 0.5, next_gate, next_scale)
        return dx, move_on(dx, dq), acc

    stepped = {n: None for n in BIG}
    reducing = []

    per_layer = [n for n in SMALL if n not in ("b_ada", "final_norm")]
    packers, exchanges = {}, {}
    clock = [0]
    to_step = []

    def step_reduced(after):
        while to_step:
            name, l, g = to_step.pop(0)
            stepped[name] = _adamw_layer(W[name], g, given["m_" + name], given["v_" + name], l, stepped[name], after)
            after = stepped[name][1]
        return after

    def move_on(after, thru, force=False):
        clock[0] += 1
        for grp in list(reducing):
            if not force and clock[0] - grp["since"] < MIN_AGE[grp["step"]]:
                continue
            grp["since"] = clock[0]
            landed = grp["flight"].land(after)
            n = len(grp["names"])
            if grp["step"] == "swap":
                pairs = [_add_half(g4, r1, place) for g4, r1 in zip(landed[:n], landed[n:])]
                grp.update(step="scatter", flight=_scatter_flight(grp["tag"], [h for h, _ in pairs], [own for _, own in pairs], thru))
            elif grp["step"] == "scatter":
                grp.update(step="share", flight=_share_flight(grp["tag"], [_sum4_into_half(r2, place) for r2 in landed[n:]], thru))
            else:
                to_step.extend((name, grp["l"], fin.reshape(2 * fin.shape[1], fin.shape[2])) for name, fin in zip(grp["names"], landed))
                reducing.remove(grp)
                continue
            thru = grp["flight"].thru
        return thru

    def reduce_group(names, l, after, thru):
        thru = move_on(after, thru)
        g4s = [big_grads[n][l].reshape(N_CHIPS, 2, big_grads[n][l].shape[1] // 2, big_grads[n][l].shape[2]) for n in names]
        tag = f"{l}{GROUPS.index(names)}"
        reducing.append(dict(names=names, l=l, tag=tag, step="swap", since=clock[0], flight=_swap_flight(tag, g4s, thru)))
        return reducing[-1]["flight"].thru
    for l in reversed(range(L)):
        s = saved[l]
        lw, mp, md = s["lw"], s["mp"], s["md"]
        dx, dq, acc3 = ffn_bwd(
            GROUPS[2], l, dx, dq, s["x2"], s["h3"], s["a3"], s["gu3"], s["f3"], lw["gu2"], lw["d2"], ffn2_norm[l][None], md[7], md[5], 1.0)
        big_grads["w_out"][l] = _mm_tn_chunks(s["yn"], dq[None], 1024, 1024)[0].reshape(N_CHIPS, -1, D)
        dyn = _mm_nt_chunks(dq[None], lw["wout"][None])
        dylru, duv, dwsp, dbfull, gacc = _gmlp_bwd(s["proj"], s["ylru"], s["ygm"], dyn, mp["gv"], seg, mp["wsp"], mp["wspt"], mp["bfull"], mp["g_lru"], mp["g_gm"])
        dxg, dwa, dwx, lvec = _lru_bwd(s["proj"], dylru, mp["cw"], mp["cb"], mp["wa"], mp["ba"], mp["wx"], mp["bx"], mp["lam"], mp["wat"], mp["wxt"])
        dproj = jnp.concatenate([dxg, duv], axis=0)
        big_grads["w_in"][l] = _mm_tn_chunks(s["h2"], dproj, 1024, LW)
        dproj = reduce_group(GROUPS[1], l, big_grads["w_in"][l], dproj)
        dx, dq, acc2 = _mm_nt_norm_bwd(dproj, lw["win"], s["x1"], dx, s["f2"], mix_norm[l][None], md[4], 1.0, md[2], 0.5)
        dq = move_on(dx, dq)
        if l > 0:
            ng, ns = saved[l - 1]["md"][8], 0.5
        else:
            ng, ns = zero_row, 0.0
        dx, dq, acc1 = ffn_bwd(
            GROUPS[0], l, dx, dq, s["x0"], s["h1"], s["a1"], s["gu1"], s["f1"], lw["gu1"], lw["d1"], ffn1_norm[l][None], md[1], ng, ns)

        dmods[l] = jnp.concatenate([acc1[0:2], acc1[3:4], acc2[0:2], acc2[3:4], acc3[0:2], acc3[3:4]], axis=0)
        dws = jnp.stack([dwsp[:, :, :CHUNK], dwsp[:, :, CHUNK:]], axis=1).reshape(HEADS, CHUNK, CHUNK) * tril
        lg = {"ffn1_norm": acc1[2], "mix_norm": acc2[2], "ffn2_norm": acc3[2],
              "conv_w": lvec[4:8], "conv_b": lvec[3], "gate_a_w": _unpair_blocks(dwa), "gate_a_b": lvec[0].reshape(HEADS, hd),
              "gate_x_w": _unpair_blocks(dwx), "gate_x_b": lvec[1].reshape(HEADS, hd), "lru_lambda": lvec[2], "v_norm": gacc[2],
              "spatial_w": dws, "spatial_b": dbfull.reshape(CHUNK, HEADS, hd).sum(-1).T, "lru_out_norm": gacc[0], "gmlp_out_norm": gacc[1]}
        part = [lg[n] for n in per_layer] + [dmods[l]] + ([head_acc[0]] if l == L - 1 else [])
        packers[l] = _Packer([p.shape for p in part])
        packed = packers[l].pack(part)
        exchanges[l] = _exchange_flight(l, lax.dynamic_update_index_in_dim(jnp.zeros((N_DEV,) + packed.shape, F32), packed, dev, 0), dq)
        dq = exchanges[l].thru

    grad_x = dx.reshape(x.shape)

    done = step_reduced(dq)
    while reducing:
        dq = move_on(done, dq, force=True)
        done = step_reduced(dq)
    summed, dmod_rows = [], []
    for l in range(L):
        gathered, = exchanges[l].land(done)
        summed.append(packers[l].unpack(_sum_leading(gathered)))
        off = sum(packers[l].sizes[:len(per_layer)])
        dmod_rows.append(gathered.reshape(N_DEV, -1)[:, off:off + N_MOD * D])
    grads = {n: jnp.stack([summed[l][k] for l in range(L)]) for k, n in enumerate(per_layer)}
    grads["final_norm"] = summed[L - 1][len(per_layer) + 1]
    grads["b_ada"] = jnp.stack([summed[l][len(per_layer)].reshape(N_MOD * D) for l in range(L)])
    dmod_shard = lax.dynamic_slice_in_dim(jnp.stack(dmod_rows), chip * n_ada, n_ada, axis=2)
    stepped_ada = _ada_step(c_all.T, dmod_shard, w_ada, m_w_ada, v_w_ada)
    grads["w_ada"] = stepped_ada[0]
    grads["conv_w"] = lax.dynamic_slice_in_dim(grads["conv_w"], chip * cws, cws, axis=2)

    delta, new_m, new_v = {}, {}, {}
    for n in BIG:
        grads[n], delta[n], new_m[n], new_v[n] = stepped[n]
    delta["w_ada"], new_m["w_ada"], new_v["w_ada"] = stepped_ada[1:]
    def rows_of(a):
        return a.reshape(-1, a.shape[-1])

    stepped_small = _adamw_small(*[[rows_of(src[n].reshape(W[n].shape)) for n in SMALL]
                                   for src in (W, grads, {n: given["m_" + n] for n in SMALL}, {n: given["v_" + n] for n in SMALL})])
    for k, n in enumerate(SMALL):
        delta[n], new_m[n], new_v[n] = (stepped_small[i * len(SMALL) + k].reshape(W[n].shape) for i in range(3))
    grads = {n: grads[n].reshape(W[n].shape) for n in WEIGHTS}
    return (loss, grad_x, *[grads[n] for n in WEIGHTS], *[delta[n] for n in WEIGHTS], *[new_m[n] for n in WEIGHTS], *[new_v[n] for n in WEIGHTS])
```

```python
import math

import jax
import jax.numpy as jnp
from jax import lax
from jax.experimental import pallas as pl
from jax.experimental.pallas import tpu as pltpu

F32 = jnp.float32
MXU_DTYPE = jnp.bfloat16
ACT_DTYPE = jnp.bfloat16
XFER_DTYPE = jnp.bfloat16
EPS = 1e-6
RG_LRU_C = 8.0
N_MOD = 9
CONV_WIDTH = 4
HEADS = 8
CHUNK = 128
LANES = 128
SUBLANES = 8
N_CHIPS = 4
N_DEV = 8
ADAM_LR, ADAM_B1, ADAM_B2, ADAM_EPS, ADAM_WD, ADAM_STEP = 0.001, 0.9, 0.999, 1e-08, 0.01, 10
VMEM_LIMIT_BYTES = 60 * 1024 * 1024
ROW_TILE_BYTES = 2 << 20
GELU_C = math.sqrt(2.0 / math.pi)
GELU_A = 0.044715

ANY = pl.BlockSpec(memory_space=pl.ANY)
MESH = pl.DeviceIdType.MESH
SDS = jax.ShapeDtypeStruct


def _params(*sem):
    return pltpu.CompilerParams(dimension_semantics=sem, vmem_limit_bytes=VMEM_LIMIT_BYTES)


def _dot(a, b):
    return jnp.dot(a.astype(MXU_DTYPE), b.astype(MXU_DTYPE), preferred_element_type=F32)


def _dot_nt(a, b):
    return lax.dot_general(a.astype(MXU_DTYPE), b.astype(MXU_DTYPE), (((1,), (1,)), ((), ())), preferred_element_type=F32)


def _dot_tn(a, b):
    return lax.dot_general(a.astype(MXU_DTYPE), b.astype(MXU_DTYPE), (((0,), (0,)), ((), ())), preferred_element_type=F32)


def _gelu(x):
    return x * (0.5 * (1.0 + jnp.tanh(GELU_C * (x + GELU_A * (x * x * x)))))


def _gelu_grad(x):
    t = jnp.tanh(GELU_C * (x + GELU_A * (x * x * x)))
    return 0.5 * (1.0 + t) + 0.5 * x * (1.0 - t * t) * (GELU_C * (1.0 + 3.0 * GELU_A * x * x))


def _sigmoid(x):
    return jax.nn.sigmoid(x)


def _sigmoid_by_tanh(x):
    return 0.5 * jnp.tanh(0.5 * x) + 0.5


def _rsqrt_ms(x):
    return lax.rsqrt(jnp.mean(x * x, axis=-1, keepdims=True) + EPS)


def _rowsum(x):
    return jnp.sum(x, axis=0, keepdims=True)


def _tile(n, want):
    t = min(n, want)
    assert n % t == 0, (n, want)
    return t


def _row_tile(rows, row_bytes):
    step = 2 * SUBLANES
    cap = max(step, ROW_TILE_BYTES // row_bytes)
    best = None
    for t in range(step, min(rows, cap) + 1, step):
        if rows % t == 0:
            best = t
    assert best is not None, (rows, row_bytes)
    return best


def _modnorm(x, gn, sh, sc):
    S, D = x.shape
    tm = _tile(S, 1024)

    def body(x_ref, gn_ref, sh_ref, sc_ref, h_ref):
        xv = x_ref[...]
        h = (xv * _rsqrt_ms(xv) * gn_ref[...]) * (1.0 + sc_ref[...]) + sh_ref[...]
        h_ref[...] = h.astype(ACT_DTYPE)

    row = pl.BlockSpec((1, D), lambda i: (0, 0))
    return pl.pallas_call(
        body, name="modnorm", grid=(S // tm,),
        in_specs=[pl.BlockSpec((tm, D), lambda i: (i, 0)), row, row, row],
        out_specs=pl.BlockSpec((tm, D), lambda i: (i, 0)),
        out_shape=SDS((S, D), ACT_DTYPE), compiler_params=_params("parallel"),
    )(x, gn, sh, sc)


def _mm_nt_norm_bwd(ac, wc, x, dxo, f, gn, sc, res_scale, next_gate, next_scale):
    P, S, K = ac.shape
    D = x.shape[1]
    tm = _tile(S, 512)

    def body(a_ref, w_ref, x_ref, dxo_ref, f_ref, gn_ref, sc_ref, ng_ref, dx_ref, dq_ref, acc_ref):
        @pl.when(pl.program_id(0) == 0)
        def _():
            acc_ref[...] = jnp.zeros_like(acc_ref)

        dh = _dot_nt(a_ref[0], w_ref[0])
        for p in range(1, P):
            dh += _dot_nt(a_ref[p], w_ref[p])
        xv, dxo = x_ref[...], dxo_ref[...]
        r = _rsqrt_ms(xv)
        xhat = xv * r
        gn = gn_ref[...]
        dn = dh * (1.0 + sc_ref[...])
        dxh = dn * gn
        dx = dxo + r * (dxh - xhat * jnp.mean(dxh * xhat, axis=-1, keepdims=True))
        dx_ref[...] = dx
        dq_ref[...] = ((next_scale * ng_ref[...]) * dx).astype(ACT_DTYPE)
        acc_ref[0:1, :] += _rowsum(dh)
        acc_ref[1:2, :] += _rowsum(dh * (xhat * gn))
        acc_ref[2:3, :] += _rowsum(dn * xhat)
        acc_ref[3:4, :] += _rowsum((res_scale * f_ref[...]) * dxo)

    tile = pl.BlockSpec((tm, D), lambda i: (i, 0))
    row = pl.BlockSpec((1, D), lambda i: (0, 0))
    return pl.pallas_call(
        body, name=f"mm_nt_norm_bwd_k{K}", grid=(S // tm,),
        in_specs=[pl.BlockSpec((P, tm, K), lambda i: (0, i, 0)),
                  pl.BlockSpec((P, D, K), lambda i: (0, 0, 0), pipeline_mode=pl.Buffered(1)),
                  tile, tile, tile, row, row, row],
        out_specs=[tile, tile, pl.BlockSpec((SUBLANES, D), lambda i: (0, 0))],
        out_shape=[SDS((S, D), F32), SDS((S, D), ACT_DTYPE), SDS((SUBLANES, D), F32)],
        compiler_params=_params("arbitrary"),
    )(ac, wc, x, dxo, f, gn, sc, next_gate)


def _loss_head(x, target, gn, next_gate, next_scale):
    S, D = x.shape
    tm = _tile(S, 512)

    def body(x_ref, t_ref, gn_ref, ng_ref, dx_ref, dq_ref, acc_ref):
        @pl.when(pl.program_id(0) == 0)
        def _():
            acc_ref[...] = jnp.zeros_like(acc_ref)

        xv = x_ref[...]
        r = _rsqrt_ms(xv)
        xhat = xv * r
        gn = gn_ref[...]
        err = xhat * gn - t_ref[...]
        dy = err * (1.0 / D)
        dxh = dy * gn
        dx = r * (dxh - xhat * jnp.mean(dxh * xhat, axis=-1, keepdims=True))
        dx_ref[...] = dx
        dq_ref[...] = ((next_scale * ng_ref[...]) * dx).astype(ACT_DTYPE)
        acc_ref[0:1, :] += _rowsum(dy * xhat)
        acc_ref[1:2, :] += _rowsum(err * err) * (0.5 / D)

    tile = pl.BlockSpec((tm, D), lambda i: (i, 0))
    row = pl.BlockSpec((1, D), lambda i: (0, 0))
    return pl.pallas_call(
        body, name="loss_head", grid=(S // tm,),
        in_specs=[tile, tile, row, row],
        out_specs=[tile, tile, pl.BlockSpec((SUBLANES, D), lambda i: (0, 0))],
        out_shape=[SDS((S, D), F32), SDS((S, D), ACT_DTYPE), SDS((SUBLANES, D), F32)],
        compiler_params=_params("arbitrary"),
    )(x, target, gn, next_gate)


def _ffn_up(h, wgu):
    S, D = h.shape
    C = wgu.shape[2]
    tm = _tile(S, 512)

    def body(h_ref, wg_ref, wu_ref, a_ref, gu_ref):
        hv = h_ref[...]
        g = _dot(hv, wg_ref[...])
        u = _dot(hv, wu_ref[...])
        a_ref[...] = (g * _sigmoid_by_tanh(g) * u).astype(ACT_DTYPE)
        gu_ref[0] = g.astype(ACT_DTYPE)
        gu_ref[1] = u.astype(ACT_DTYPE)

    return pl.pallas_call(
        body, name="ffn_up", grid=(2, S // tm),
        in_specs=[
            pl.BlockSpec((tm, D), lambda j, i: (i, 0)),
            pl.BlockSpec((None, D, C), lambda j, i: (j, 0, 0)),
            pl.BlockSpec((None, D, C), lambda j, i: (2 + j, 0, 0)),
        ],
        out_specs=[
            pl.BlockSpec((tm, C), lambda j, i: (i, j)),
            pl.BlockSpec((2, None, tm, C), lambda j, i: (0, j, i, 0)),
        ],
        out_shape=[SDS((S, 2 * C), ACT_DTYPE), SDS((2, 2, S, C), ACT_DTYPE)],
        compiler_params=_params("parallel", "parallel"),
    )(h, wgu, wgu)


def _ffn_bwd_act(dq, wd, gu):
    S, D = dq.shape
    C = gu.shape[3]
    tm = _tile(S, 512)

    def body(dq_ref, wd_ref, gu_ref, dgu_ref):
        da = _dot_nt(dq_ref[...], wd_ref[...])
        g = gu_ref[0].astype(F32)
        u = gu_ref[1].astype(F32)
        s = _sigmoid_by_tanh(g)
        gs = g * s
        dgu_ref[0] = ((da * u) * (s + gs - gs * s)).astype(ACT_DTYPE)
        dgu_ref[1] = (da * gs).astype(ACT_DTYPE)

    gu_spec = pl.BlockSpec((2, None, tm, C), lambda j, i: (0, j, i, 0))
    return pl.pallas_call(
        body, name="ffn_bwd_act", grid=(2, S // tm),
        in_specs=[pl.BlockSpec((tm, D), lambda j, i: (i, 0)), pl.BlockSpec((C, D), lambda j, i: (j, 0)), gu_spec],
        out_specs=gu_spec,
        out_shape=SDS(gu.shape, ACT_DTYPE),
        compiler_params=_params("parallel", "parallel"),
    )(dq, wd, gu)


def _mm_res(a, w, x, gate, scale, following):
    S, K = a.shape
    D = w.shape[1]
    tm = _tile(S, 512)

    def body(a_ref, w_ref, x_ref, g_ref, gn_ref, sh_ref, sc_ref, f_ref, xo_ref, h_ref):
        f = _dot(a_ref[...], w_ref[...])
        f_ref[...] = f
        xo = x_ref[...] + (scale * g_ref[...]) * f
        xo_ref[...] = xo
        h_ref[...] = ((xo * _rsqrt_ms(xo) * gn_ref[...]) * (1.0 + sc_ref[...]) + sh_ref[...]).astype(ACT_DTYPE)

    tile = pl.BlockSpec((tm, D), lambda i: (i, 0))
    row = pl.BlockSpec((1, D), lambda i: (0, 0))
    return pl.pallas_call(
        body, name=f"mm_res_k{K}", grid=(S // tm,),
        in_specs=[pl.BlockSpec((tm, K), lambda i: (i, 0)), pl.BlockSpec((K, D), lambda i: (0, 0)), tile, row, row, row, row],
        out_specs=[tile, tile, tile],
        out_shape=[SDS((S, D), F32), SDS((S, D), F32), SDS((S, D), ACT_DTYPE)],
        compiler_params=_params("parallel"),
    )(a, w, x, gate, *following)


def _mm_chunks(h, wc):
    S, K = h.shape
    P, _, N = wc.shape
    tm = _tile(S, 512)

    def body(h_ref, w_ref, o_ref):
        hv = h_ref[...]
        for p in range(P):
            o_ref[:, p * N:(p + 1) * N] = _dot(hv, w_ref[p])

    return pl.pallas_call(
        body, name="mm_chunks", grid=(S // tm,),
        in_specs=[pl.BlockSpec((tm, K), lambda i: (i, 0)), pl.BlockSpec((P, K, N), lambda i: (0, 0, 0))],
        out_specs=pl.BlockSpec((tm, P * N), lambda i: (i, 0)),
        out_shape=SDS((S, P * N), F32),
        compiler_params=_params("parallel"),
    )(h, wc)


def _mm_nt_chunks(ac, wc):
    P, S, K = ac.shape
    N = wc.shape[1]
    tm, tn = _tile(S, 512), _tile(N, 1024)

    def body(a_ref, w_ref, o_ref):
        acc = _dot_nt(a_ref[0], w_ref[0])
        for p in range(1, P):
            acc += _dot_nt(a_ref[p], w_ref[p])
        o_ref[...] = acc

    return pl.pallas_call(
        body, name=f"mm_nt_p{P}k{K}", grid=(S // tm, N // tn),
        in_specs=[pl.BlockSpec((P, tm, K), lambda i, j: (0, i, 0)), pl.BlockSpec((P, tn, K), lambda i, j: (0, j, 0))],
        out_specs=pl.BlockSpec((tm, tn), lambda i, j: (i, j)),
        out_shape=SDS((S, N), F32),
        compiler_params=_params("parallel", "parallel"),
    )(ac, wc)


def _mm_tn_chunks(a, bc, tile_m, tile_n):
    S, M = a.shape
    P, _, N = bc.shape
    ts, tm, tn = _tile(S, 2048), _tile(M, tile_m), _tile(N, tile_n)

    def body(a_ref, b_ref, o_ref):
        @pl.when(pl.program_id(3) == 0)
        def _():
            o_ref[...] = jnp.zeros_like(o_ref)

        o_ref[...] += _dot_tn(a_ref[...], b_ref[...])

    return pl.pallas_call(
        body, name=f"mm_tn_m{M}n{N}", grid=(P, M // tm, N // tn, S // ts),
        in_specs=[pl.BlockSpec((ts, tm), lambda p, m, n, k: (k, m)), pl.BlockSpec((None, ts, tn), lambda p, m, n, k: (p, k, n))],
        out_specs=pl.BlockSpec((None, tm, tn), lambda p, m, n, k: (p, m, n)),
        out_shape=SDS((P, M, N), F32),
        compiler_params=_params("parallel", "parallel", "parallel", "arbitrary"),
    )(a, bc)


def _shift_down(x, s, row, fill):
    return jnp.where(row >= s, pltpu.roll(x, s, 0), fill)


def _shift_up(x, s, row, fill):
    n = x.shape[0]
    return jnp.where(row < n - s, pltpu.roll(x, n - s, 0), fill)


def _scan(a, b, row, scratch, up):
    scr_a, scr_b, scr_c = scratch
    n = a.shape[0]
    g = n // SUBLANES
    in_group = row & (SUBLANES - 1)

    def steps(a, b, pos, size):
        s = 1
        while s < size:
            m = (pos + s < size) if up else (pos >= s)
            b = jnp.where(m, a, 0.0) * pltpu.roll(b, a.shape[0] - s if up else s, 0) + b
            a = jnp.where(m, a * pltpu.roll(a, a.shape[0] - s if up else s, 0), a)
            s *= 2
        return a, b

    a, b = steps(a, b, in_group, SUBLANES)
    scr_a[...] = a
    scr_b[...] = b
    edge = 0 if up else SUBLANES - 1
    at = scr_a[pl.ds(edge, g, stride=SUBLANES), :]
    bt = scr_b[pl.ds(edge, g, stride=SUBLANES), :]
    group = lax.broadcasted_iota(jnp.int32, at.shape, 0)
    _, state = steps(at, bt, group, g)
    carry = jnp.where((group + 1 < g) if up else (group >= 1), pltpu.roll(state, g - 1 if up else 1, 0), 0.0)
    for k in range(SUBLANES):
        scr_c[pl.ds(k, g, stride=SUBLANES), :] = carry
    return b + a * scr_c[...]


def _conv(xl, cw_ref, cb_ref, row):
    y = cb_ref[...] + _shift_down(xl, 3, row, 0.0) * cw_ref[0:1, :]
    y = y + _shift_down(xl, 2, row, 0.0) * cw_ref[1:2, :]
    y = y + _shift_down(xl, 1, row, 0.0) * cw_ref[2:3, :]
    return y + xl * cw_ref[3:4, :]


def _lru_gates(xc, wa_ref, ba_ref, wx_ref, bx_ref, lam_ref):
    ra = _sigmoid(_dot(xc, wa_ref[...]) + ba_ref[...])
    ri = _sigmoid(_dot(xc, wx_ref[...]) + bx_ref[...])
    ls = jax.nn.log_sigmoid(lam_ref[...])
    a = jnp.exp((RG_LRU_C * ra) * ls)
    mult = jnp.sqrt(1.0 - a * a)
    return ra, ri, ls, a, mult


def _lru_specs(S):
    col = lambda off: pl.BlockSpec((S, LANES), lambda j: (0, off + j))
    vec = pl.BlockSpec((1, LANES), lambda j: (0, j))
    blk = pl.BlockSpec((None, LANES, LANES), lambda j: (j, 0, 0))
    cw = pl.BlockSpec((CONV_WIDTH, LANES), lambda j: (0, j))
    return col, vec, blk, cw


def _lru_fwd(proj, cw, cb, wa, ba, wx, bx, lam):
    S = proj.shape[0]
    W = cb.shape[1]
    nb = W // LANES

    def body(xl_ref, gl_ref, cw_ref, cb_ref, wa_ref, ba_ref, wx_ref, bx_ref, lam_ref, y_ref, *scratch):
        row = lax.broadcasted_iota(jnp.int32, (S, LANES), 0)
        xc = _conv(xl_ref[...], cw_ref, cb_ref, row)
        _, ri, _, a, mult = _lru_gates(xc, wa_ref, ba_ref, wx_ref, bx_ref, lam_ref)
        h = _scan(a, mult * (ri * xc), row, scratch, up=False)
        y_ref[...] = h * _gelu(gl_ref[...])

    col, vec, blk, cws = _lru_specs(S)
    return pl.pallas_call(
        body, name="lru_fwd", grid=(nb,),
        in_specs=[col(0), col(nb), cws, vec, blk, vec, blk, vec, vec],
        out_specs=pl.BlockSpec((S, LANES), lambda j: (0, j)),
        out_shape=SDS((S, W), F32), scratch_shapes=[pltpu.VMEM((S, LANES), F32)] * 3, compiler_params=_params("parallel"),
    )(proj, proj, cw, cb, wa, ba, wx, bx, lam)


def _lru_bwd(proj, dy, cw, cb, wa, ba, wx, bx, lam, wat, wxt):
    S = proj.shape[0]
    W = cb.shape[1]
    nb = W // LANES

    def body(xl_ref, gl_ref, dy_ref, cw_ref, cb_ref, wa_ref, ba_ref, wx_ref, bx_ref, lam_ref, wat_ref, wxt_ref,
             dp_ref, dwa_ref, dwx_ref, vec_ref, *scratch):
        row = lax.broadcasted_iota(jnp.int32, (S, LANES), 0)
        xl = xl_ref[...]
        xc = _conv(xl, cw_ref, cb_ref, row)
        ra, ri, ls, a, mult = _lru_gates(xc, wa_ref, ba_ref, wx_ref, bx_ref, lam_ref)
        h = _scan(a, mult * (ri * xc), row, scratch, up=False)
        gl = gl_ref[...]
        dyv = dy_ref[...]
        dp_ref[1] = (dyv * h * _gelu_grad(gl)).astype(ACT_DTYPE)
        adj = _scan(_shift_up(a, 1, row, 0.0), dyv * _gelu(gl), row, scratch, up=True)
        da = adj * _shift_down(h, 1, row, 0.0)
        dmult = adj * (ri * xc)
        dlog_a = da * a - dmult * (a * a) / mult
        dra = dlog_a * (RG_LRU_C * ls)
        dpa = dra * ra * (1.0 - ra)
        dpi = (adj * mult * xc) * ri * (1.0 - ri)
        dxc = adj * mult * ri + _dot(dpa, wat_ref[...]) + _dot(dpi, wxt_ref[...])
        dwa_ref[...] = _dot_tn(xc, dpa)
        dwx_ref[...] = _dot_tn(xc, dpi)
        dxl = dxc * cw_ref[3:4, :]
        dxl = dxl + _shift_up(dxc, 1, row, 0.0) * cw_ref[2:3, :]
        dxl = dxl + _shift_up(dxc, 2, row, 0.0) * cw_ref[1:2, :]
        dxl = dxl + _shift_up(dxc, 3, row, 0.0) * cw_ref[0:1, :]
        dp_ref[0] = dxl.astype(ACT_DTYPE)
        vec_ref[...] = jnp.zeros_like(vec_ref)
        vec_ref[0:1, :] = _rowsum(dpa)
        vec_ref[1:2, :] = _rowsum(dpi)
        vec_ref[2:3, :] = _rowsum(dlog_a * (RG_LRU_C * ra)) * _sigmoid(-lam_ref[...])
        vec_ref[3:4, :] = _rowsum(dxc)
        vec_ref[4:5, :] = _rowsum(dxc * _shift_down(xl, 3, row, 0.0))
        vec_ref[5:6, :] = _rowsum(dxc * _shift_down(xl, 2, row, 0.0))
        vec_ref[6:7, :] = _rowsum(dxc * _shift_down(xl, 1, row, 0.0))
        vec_ref[7:8, :] = _rowsum(dxc * xl)

    col, vec, blk, cws = _lru_specs(S)
    return pl.pallas_call(
        body, name="lru_bwd", grid=(nb,),
        in_specs=[col(0), col(nb), col(0), cws, vec, blk, vec, blk, vec, vec, blk, blk],
        out_specs=[pl.BlockSpec((2, S, LANES), lambda j: (0, 0, j)), blk, blk, pl.BlockSpec((2 * SUBLANES, LANES), lambda j: (0, j))],
        out_shape=[SDS((2, S, W), ACT_DTYPE), SDS((nb, LANES, LANES), F32), SDS((nb, LANES, LANES), F32), SDS((2 * SUBLANES, W), F32)],
        scratch_shapes=[pltpu.VMEM((S, LANES), F32)] * 3, compiler_params=_params("parallel"),
    )(proj, proj, dy, cw, cb, wa, ba, wx, bx, lam, wat, wxt)


def _seg_mean(x, seg_ref, width):
    hi = x.astype(jnp.bfloat16)
    lo = (x - hi.astype(F32)).astype(jnp.bfloat16)
    ones = seg_ref[...]
    s = jnp.dot(hi, ones, preferred_element_type=F32) + jnp.dot(lo, ones, preferred_element_type=F32)
    return s * (1.0 / width)


def _gmlp_core(u_ref, v_ref, gv_ref, seg_ref, ws_ref, bfull_ref, z_scr, hd):
    tm, W = u_ref.shape
    lane = lax.broadcasted_iota(jnp.int32, (CHUNK, LANES), 1)
    ug = _gelu(u_ref[...])
    vg = _gelu(v_ref[...])
    cen = vg - _seg_mean(vg, seg_ref, hd)
    rstd = lax.rsqrt(_seg_mean(cen * cen, seg_ref, hd) + EPS)
    vhat = cen * rstd
    vh = vhat * gv_ref[...]
    vcats = {}
    for ci in range(tm // CHUNK):
        for p in range(W // LANES):
            blk = vh[ci * CHUNK:(ci + 1) * CHUNK, p * LANES:(p + 1) * LANES]
            vcat = jnp.concatenate([jnp.where(lane < hd, blk, 0.0), jnp.where(lane >= hd, blk, 0.0)], axis=0).astype(MXU_DTYPE)
            vcats[ci, p] = vcat
            z_scr[ci * CHUNK:(ci + 1) * CHUNK, p * LANES:(p + 1) * LANES] = (
                jnp.dot(ws_ref[p], vcat, preferred_element_type=F32) + bfull_ref[:, p * LANES:(p + 1) * LANES])
    return ug, vhat, rstd, vcats


def _gmlp_specs(tm, W, nb):
    rows = lambda off: pl.BlockSpec((tm, W), lambda i: (i, off))
    vec = pl.BlockSpec((1, W), lambda i: (0, 0))
    seg = pl.BlockSpec((W, W), lambda i: (0, 0))
    wsp = pl.BlockSpec((nb, CHUNK, 2 * CHUNK), lambda i: (0, 0, 0))
    bfull = pl.BlockSpec((CHUNK, W), lambda i: (0, 0))
    return rows, vec, seg, wsp, bfull


def _gmlp_fwd(proj, ylru, gv, seg, wsp, bfull, g_lru, g_gm):
    S, W = ylru.shape
    nb = W // LANES
    hd = W // HEADS
    tm = _tile(S, 512)

    def body(u_ref, v_ref, yl_ref, gv_ref, seg_ref, ws_ref, bfull_ref, gl_ref, gg_ref, yn_ref, ygm_ref, z_scr):
        ug, _, _, _ = _gmlp_core(u_ref, v_ref, gv_ref, seg_ref, ws_ref, bfull_ref, z_scr, hd)
        ygm = ug * z_scr[...]
        ygm_ref[...] = ygm
        yl = yl_ref[...]
        yn_ref[:, 0:W] = (yl * _rsqrt_ms(yl) * gl_ref[...]).astype(ACT_DTYPE)
        yn_ref[:, W:2 * W] = (ygm * _rsqrt_ms(ygm) * gg_ref[...]).astype(ACT_DTYPE)

    rows, vec, segs, wsps, bfulls = _gmlp_specs(tm, W, nb)
    return pl.pallas_call(
        body, name="gmlp_fwd", grid=(S // tm,),
        in_specs=[rows(2), rows(3), rows(0), vec, segs, wsps, bfulls, vec, vec],
        out_specs=[pl.BlockSpec((tm, 2 * W), lambda i: (i, 0)), rows(0)],
        out_shape=[SDS((S, 2 * W), ACT_DTYPE), SDS((S, W), F32)],
        scratch_shapes=[pltpu.VMEM((tm, W), F32)],
        compiler_params=_params("parallel"),
    )(proj, proj, ylru, gv, seg, wsp, bfull, g_lru, g_gm)


def _rms_bwd(y, g, dyn):
    r = _rsqrt_ms(y)
    yhat = y * r
    dyh = dyn * g
    return r * (dyh - yhat * jnp.mean(dyh * yhat, axis=-1, keepdims=True)), _rowsum(dyn * yhat)


def _gmlp_bwd(proj, ylru, ygm, dyn, gv, seg, wsp, wspt, bfull, g_lru, g_gm):
    S, W = ylru.shape
    nb = W // LANES
    hd = W // HEADS
    tm = _tile(S, 256)

    def body(u_ref, v_ref, yl_ref, ygm_ref, dl_ref, dg_ref, gv_ref, seg_ref, ws_ref, wst_ref, bfull_ref, gl_ref, gg_ref,
             dyl_ref, duv_ref, dws_ref, dbf_ref, acc_ref, z_scr, dvh_scr):
        @pl.when(pl.program_id(0) == 0)
        def _():
            dws_ref[...] = jnp.zeros_like(dws_ref)
            dbf_ref[...] = jnp.zeros_like(dbf_ref)
            acc_ref[...] = jnp.zeros_like(acc_ref)

        dyl, dgl = _rms_bwd(yl_ref[...], gl_ref[...], dl_ref[...])
        dyl_ref[...] = dyl
        dygm, dgg = _rms_bwd(ygm_ref[...], gg_ref[...], dg_ref[...])
        ug, vhat, rstd, vcats = _gmlp_core(u_ref, v_ref, gv_ref, seg_ref, ws_ref, bfull_ref, z_scr, hd)
        duv_ref[0] = (dygm * z_scr[...] * _gelu_grad(u_ref[...])).astype(ACT_DTYPE)
        dz = dygm * ug
        lane = lax.broadcasted_iota(jnp.int32, (CHUNK, LANES), 1)
        dbf = dz[0:CHUNK, :]
        for ci in range(1, tm // CHUNK):
            dbf += dz[ci * CHUNK:(ci + 1) * CHUNK, :]
        dbf_ref[...] += dbf
        for ci in range(tm // CHUNK):
            for p in range(nb):
                dzb = dz[ci * CHUNK:(ci + 1) * CHUNK, p * LANES:(p + 1) * LANES].astype(MXU_DTYPE)
                dws_ref[p] += _dot_nt(dzb, vcats[ci, p])
                dvc = jnp.dot(wst_ref[p], dzb, preferred_element_type=F32)
                dvh_scr[ci * CHUNK:(ci + 1) * CHUNK, p * LANES:(p + 1) * LANES] = jnp.where(lane < hd, dvc[0:CHUNK], dvc[CHUNK:2 * CHUNK])
        dvh = dvh_scr[...]
        dvn = dvh * gv_ref[...]
        dvg = rstd * (dvn - _seg_mean(dvn, seg_ref, hd) - vhat * _seg_mean(dvn * vhat, seg_ref, hd))
        duv_ref[1] = (dvg * _gelu_grad(v_ref[...])).astype(ACT_DTYPE)
        acc_ref[0:1, :] += dgl
        acc_ref[1:2, :] += dgg
        acc_ref[2:3, :] += _rowsum(dvh * vhat)

    rows, vec, segs, wsps, bfulls = _gmlp_specs(tm, W, nb)
    wspt_spec = pl.BlockSpec((nb, 2 * CHUNK, CHUNK), lambda i: (0, 0, 0))
    return pl.pallas_call(
        body, name="gmlp_bwd", grid=(S // tm,),
        in_specs=[rows(2), rows(3), rows(0), rows(0), rows(0), rows(1), vec, segs, wsps, wspt_spec, bfulls, vec, vec],
        out_specs=[rows(0), pl.BlockSpec((2, tm, W), lambda i: (0, i, 0)), wsps, bfulls, pl.BlockSpec((SUBLANES, W), lambda i: (0, 0))],
        out_shape=[SDS((S, W), F32), SDS((2, S, W), ACT_DTYPE), SDS((nb, CHUNK, 2 * CHUNK), F32), SDS((CHUNK, W), F32), SDS((SUBLANES, W), F32)],
        scratch_shapes=[pltpu.VMEM((tm, W), F32), pltpu.VMEM((tm, W), F32)],
        compiler_params=_params("arbitrary"),
    )(proj, proj, ylru, ygm, dyn, dyn, gv, seg, wsp, wspt, bfull, g_lru, g_gm)


def _ada_fwd(c_all, w_ada, b_shard):
    L, D, N = w_ada.shape
    R = c_all.shape[0]
    tn = N // 2

    def body(c_ref, w_ref, b_ref, o_ref):
        cv = c_ref[...]
        o_ref[...] = _dot(cv * _sigmoid(cv), w_ref[...]) + b_ref[...]

    return pl.pallas_call(
        body, name="ada_fwd", grid=(L, N // tn),
        in_specs=[pl.BlockSpec((R, D), lambda l, j: (0, 0)), pl.BlockSpec((None, D, tn), lambda l, j: (l, 0, j)),
                  pl.BlockSpec((None, 1, tn), lambda l, j: (l, 0, j))],
        out_specs=pl.BlockSpec((None, R, tn), lambda l, j: (l, 0, j)),
        out_shape=SDS((L, R, N), F32), compiler_params=_params("parallel", "parallel"),
    )(c_all, w_ada, b_shard)


def _ada_step(c_all_t, dmod, w, m, v):
    D, B = c_all_t.shape
    L, _, N = dmod.shape
    tn = 3 * LANES
    assert N % tn == 0, N

    def body(c_ref, d_ref, w_ref, m_ref, v_ref, g_ref, do_ref, mo_ref, vo_ref):
        cv = c_ref[...]
        sc = cv * _sigmoid(cv)
        g = sc[:, 0:1] * d_ref[0:1, :]
        for b in range(1, B):
            g += sc[:, b:b + 1] * d_ref[b:b + 1, :]
        g_ref[...] = g
        do_ref[...], mo_ref[...], vo_ref[...] = _adam_math(w_ref[...], g, m_ref[...], v_ref[...])

    tile = pl.BlockSpec((None, D, tn), lambda l, j: (l, 0, j))
    return pl.pallas_call(
        body, name="ada_step", grid=(L, N // tn),
        in_specs=[pl.BlockSpec((D, B), lambda l, j: (0, 0)), pl.BlockSpec((None, B, tn), lambda l, j: (l, 0, j)), tile, tile, tile],
        out_specs=[tile] * 4, out_shape=[SDS((L, D, N), F32)] * 4, compiler_params=_params("parallel", "parallel"),
    )(c_all_t, dmod, w, m, v)


def _adam_math(w, g, m, v):
    mn = ADAM_B1 * m + (1.0 - ADAM_B1) * g
    vn = ADAM_B2 * v + (1.0 - ADAM_B2) * (g * g)
    m_hat = mn / (1.0 - ADAM_B1 ** ADAM_STEP)
    v_hat = vn / (1.0 - ADAM_B2 ** ADAM_STEP)
    return -ADAM_LR * (m_hat / (jnp.sqrt(v_hat) + ADAM_EPS) + ADAM_WD * w), mn, vn


def _adamw_layer(w, g, m, v, l, prev, after):
    L, R, C = w.shape
    tr = _row_tile(R, C * 4)
    prev = (after,) + (() if prev is None else tuple(prev))

    def body(w_ref, g_ref, m_ref, v_ref, *rest):
        go_ref, d_ref, mo_ref, vo_ref = rest[len(prev):]
        gv = g_ref[...]
        go_ref[...] = gv
        d_ref[...], mo_ref[...], vo_ref[...] = _adam_math(w_ref[...], gv, m_ref[...], v_ref[...])

    lay = pl.BlockSpec((None, tr, C), lambda i: (l, i, 0))
    return pl.pallas_call(
        body, name=f"adamw_layer_r{R}c{C}", grid=(R // tr,),
        in_specs=[lay, pl.BlockSpec((tr, C), lambda i: (i, 0)), lay, lay] + [ANY] * len(prev), out_specs=[lay] * 4,
        out_shape=[SDS((L, R, C), F32)] * 4, input_output_aliases={5 + k: k for k in range(len(prev) - 1)},
        compiler_params=_params("parallel"),
    )(w, g, m, v, *prev)


def _adamw_small(ws, gs, ms, vs):
    n = len(ws)

    def body(*refs):
        outs = refs[4 * n:]
        for k in range(n):
            outs[k][...], outs[n + k][...], outs[2 * n + k][...] = _adam_math(
                refs[k][...], refs[n + k][...], refs[2 * n + k][...], refs[3 * n + k][...])

    whole = pl.BlockSpec(memory_space=pltpu.VMEM)
    return pl.pallas_call(
        body, name="adamw_small", in_specs=[whole] * (4 * n), out_specs=[whole] * (3 * n),
        out_shape=[SDS(w.shape, F32) for w in ws] * 3, compiler_params=pltpu.CompilerParams(vmem_limit_bytes=VMEM_LIMIT_BYTES),
    )(*ws, *gs, *ms, *vs)


def _sum_leading(a):
    P, R, C = a.shape
    tr = _row_tile(R, P * C * 4)

    def body(a_ref, o_ref):
        acc = a_ref[0]
        for p in range(1, P):
            acc = acc + a_ref[p]
        o_ref[...] = acc

    return pl.pallas_call(
        body, name=f"sum{P}_r{R}c{C}", grid=(R // tr,),
        in_specs=[pl.BlockSpec((P, tr, C), lambda i: (0, i, 0))],
        out_specs=pl.BlockSpec((tr, C), lambda i: (i, 0)),
        out_shape=SDS((R, C), F32), compiler_params=_params("parallel"),
    )(a)


def _add_half(g4, r1, place):
    _, _, R, C = g4.shape
    tr = _row_tile(R, C * 4)

    def body(place_ref, g_ref, r_ref, h_ref, own_ref):
        s = (g_ref[...] + r_ref[...]).astype(XFER_DTYPE)
        h_ref[...] = s

        @pl.when(pl.program_id(1) == place_ref[1])
        def _():
            own_ref[...] = s

    return pl.pallas_call(
        body, name=f"add_half_r{R}c{C}",
        grid_spec=pltpu.PrefetchScalarGridSpec(
            num_scalar_prefetch=1, grid=(R // tr, N_CHIPS),
            in_specs=[pl.BlockSpec((None, None, tr, C), lambda i, p, place_ref: (p, place_ref[0], i, 0)),
                      pl.BlockSpec((None, tr, C), lambda i, p, place_ref: (p, i, 0))],
            out_specs=[pl.BlockSpec((None, tr, C), lambda i, p, place_ref: (p, i, 0)),
                       pl.BlockSpec((None, tr, C), lambda i, p, place_ref: (place_ref[1], i, 0))],
        ),
        out_shape=[SDS((N_CHIPS, R, C), XFER_DTYPE)] * 2, compiler_params=_params("parallel", "arbitrary"),
    )(place, g4, r1)


def _sum4_into_half(r2, place):
    P, R, C = r2.shape
    tr = _row_tile(R, P * C * 4)

    def body(place_ref, a_ref, o_ref):
        acc = a_ref[0].astype(F32)
        for p in range(1, P):
            acc = acc + a_ref[p].astype(F32)
        o_ref[...] = acc

    return pl.pallas_call(
        body, name=f"sum4_r{R}c{C}",
        grid_spec=pltpu.PrefetchScalarGridSpec(
            num_scalar_prefetch=1, grid=(R // tr,),
            in_specs=[pl.BlockSpec((P, tr, C), lambda i, place_ref: (0, i, 0))],
            out_specs=pl.BlockSpec((None, tr, C), lambda i, place_ref: (place_ref[0], i, 0)),
        ),
        out_shape=SDS((2, R, C), F32), compiler_params=_params("parallel"),
    )(place, r2)


def _cast_into_slot(w, l, place, after):
    _, R, C = w.shape
    tr = _row_tile(R, C * 4)

    def body(place_ref, w_ref, after_ref, o_ref):
        o_ref[...] = w_ref[...].astype(MXU_DTYPE)

    return pl.pallas_call(
        body, name=f"cast_r{R}c{C}",
        grid_spec=pltpu.PrefetchScalarGridSpec(
            num_scalar_prefetch=1, grid=(R // tr,),
            in_specs=[pl.BlockSpec((None, tr, C), lambda i, place_ref: (l, i, 0)), ANY],
            out_specs=pl.BlockSpec((None, tr, C), lambda i, place_ref: (place_ref[1], i, 0)),
        ),
        out_shape=SDS((N_CHIPS, R, C), MXU_DTYPE), compiler_params=_params("parallel"),
    )(place, w, after)


def _place():
    x, y, c = lax.axis_index("x"), lax.axis_index("y"), lax.axis_index("c")
    chips = [(1 - x, y), (x, 1 - y), (1 - x, 1 - y)]
    return x, y, c, chips


def _remote(src, dst, send_sem, recv_sem, to):
    return pltpu.make_async_remote_copy(src_ref=src, dst_ref=dst, send_sem=send_sem, recv_sem=recv_sem, device_id=to, device_id_type=MESH)


HBM_SPEC = pl.BlockSpec(memory_space=pltpu.HBM)
SEM_SPEC = pl.BlockSpec(memory_space=pltpu.SEMAPHORE)
DATAFLOW = pltpu.SideEffectType.DATAFLOW_SIDE_EFFECTING


def _in_hbm(a):
    return pltpu.with_memory_space_constraint(a, pltpu.HBM)


def _hbm_like(a):
    return pltpu.HBM(a.shape, a.dtype)


def _split_start(name, arrays, n_sems, issue, extra=()):
    m = len(arrays)

    def body(*refs):
        issue(refs[:m], refs[m + len(extra)], refs[m + len(extra) + 1])

    out = pl.pallas_call(
        body, name=name,
        out_shape=(pltpu.SemaphoreType.DMA((n_sems,)), pltpu.SemaphoreType.DMA((n_sems,)), *[_hbm_like(a) for a in arrays]),
        in_specs=[HBM_SPEC] * m + [ANY] * len(extra), out_specs=(SEM_SPEC, SEM_SPEC, *[HBM_SPEC] * m),
        input_output_aliases={k: 2 + k for k in range(m)},
        compiler_params=pltpu.CompilerParams(has_side_effects=DATAFLOW),
    )(*[_in_hbm(a) for a in arrays], *extra)
    return out[0], out[1], list(out[2:])


def _split_wait(name, send_sems, recv_sems, arrays, after, drain):
    m = len(arrays)

    def body(*refs):
        drain(refs[:m], refs[m], refs[m + 1])

    out = pl.pallas_call(
        body, name=name, out_shape=[_hbm_like(a) for a in arrays],
        in_specs=[HBM_SPEC] * m + [SEM_SPEC, SEM_SPEC, ANY], out_specs=[HBM_SPEC] * m,
        input_output_aliases={k: k for k in range(m)},
        compiler_params=pltpu.CompilerParams(has_side_effects=DATAFLOW),
    )(*arrays, send_sems, recv_sems, after)
    return list(out)


def _wait_both(cp):
    cp.wait_send()
    cp.wait_recv()


class _Flight:
    def __init__(self, name, arrays, n_sems, issue, drain, thru, extra=()):
        self.name, self.drain, self.n = name, drain, len(arrays)
        self.send, self.recv, out = _split_start(name + "_start", [*arrays, thru], n_sems, issue, extra)
        self.arrays, self.thru = out[:-1], out[-1]

    def land(self, after):
        return _split_wait(self.name + "_wait", self.send, self.recv, self.arrays, after, self.drain)


def _gather_flight(tag, ici, d2d, direct, thru):
    kinds = ["ici"] * len(ici) + ["d2d"] * len(d2d) + ["direct"] * len(direct)
    rows = [pl.ds(part * (a.shape[2] // parts), a.shape[2] // parts) for a, part, parts in ici]
    ici = [a for a, _, _ in ici]

    def issue(refs, send_sems, recv_sems):
        x, y, c, chips = _place()
        q = 2 * x + y
        for k, kind in enumerate(kinds):
            for j, chip in enumerate(chips):
                if kind == "ici":
                    src, to = refs[k].at[q, c, rows[k]], (*chip, c)
                elif kind == "d2d":
                    src, to = refs[k].at[2 * chip[0] + chip[1], c], (x, y, 1 - c)
                else:
                    src, to = refs[k].at[q], (*chip, c)
                _remote(src, src, send_sems.at[3 * k + j], recv_sems.at[3 * k + j], to).start()

    def drain(refs, send_sems, recv_sems):
        x, y, c, chips = _place()
        for k, kind in enumerate(kinds):
            for j, chip in enumerate(chips):
                p = 2 * chip[0] + chip[1]
                got = refs[k].at[p] if kind == "direct" else refs[k].at[p, c, rows[k]] if kind == "ici" else refs[k].at[p, 1 - c]
                _wait_both(_remote(got, got, send_sems.at[3 * k + j], recv_sems.at[3 * k + j], (x, y, c)))

    return _Flight(f"gather{tag}", [*ici, *d2d, *direct], 3 * len(kinds), issue, drain, thru)


def _swap_flight(tag, g4s, thru):
    n = len(g4s)
    zones = [lax.empty((N_CHIPS,) + g.shape[2:], g.dtype) for g in g4s]

    def issue(refs, send_sems, recv_sems):
        x, y, c, _ = _place()
        for k in range(n):
            for p in range(N_CHIPS):
                _remote(refs[k].at[p, 1 - c], refs[n + k].at[p], send_sems.at[N_CHIPS * k + p], recv_sems.at[N_CHIPS * k + p], (x, y, 1 - c)).start()

    def drain(refs, send_sems, recv_sems):
        x, y, c, _ = _place()
        for k in range(n):
            for p in range(N_CHIPS):
                got = refs[n + k].at[p]
                _wait_both(_remote(got, got, send_sems.at[N_CHIPS * k + p], recv_sems.at[N_CHIPS * k + p], (x, y, c)))

    return _Flight(f"swap{tag}", [*g4s, *zones], N_CHIPS * n, issue, drain, thru)


def _scatter_flight(tag, hs, lands, thru):
    n = len(hs)

    def issue(refs, send_sems, recv_sems):
        x, y, c, chips = _place()
        q = 2 * x + y
        for k in range(n):
            for j, chip in enumerate(chips):
                _remote(refs[k].at[2 * chip[0] + chip[1]], refs[n + k].at[q], send_sems.at[3 * k + j], recv_sems.at[3 * k + j], (*chip, c)).start()

    def drain(refs, send_sems, recv_sems):
        x, y, c, chips = _place()
        for k in range(n):
            for j, chip in enumerate(chips):
                got = refs[n + k].at[2 * chip[0] + chip[1]]
                _wait_both(_remote(got, got, send_sems.at[3 * k + j], recv_sems.at[3 * k + j], (x, y, c)))

    return _Flight(f"scatter{tag}", [*hs, *lands], 3 * n, issue, drain, thru)


def _exchange_flight(tag, buf, thru):
    flips = [(fx, fy, fc) for fx in (0, 1) for fy in (0, 1) for fc in (0, 1)][1:]

    def peers():
        x, y, c, _ = _place()
        return (x, y, c), [((1 - x) if fx else x, (1 - y) if fy else y, (1 - c) if fc else c) for fx, fy, fc in flips]

    def slot(ref, dev):
        return ref.at[4 * dev[0] + 2 * dev[1] + dev[2]]

    def issue(refs, send_sems, recv_sems):
        me, others = peers()
        for j, to in enumerate(others):
            _remote(slot(refs[0], me), slot(refs[0], me), send_sems.at[j], recv_sems.at[j], to).start()

    def drain(refs, send_sems, recv_sems):
        me, others = peers()
        for j, frm in enumerate(others):
            got = slot(refs[0], frm)
            _wait_both(_remote(got, got, send_sems.at[j], recv_sems.at[j], me))

    return _Flight(f"exchange{tag}", [buf], len(flips), issue, drain, thru)


def _share_flight(tag, fins, thru):
    n = len(fins)

    def issue(refs, send_sems, recv_sems):
        x, y, c, _ = _place()
        for k in range(n):
            _remote(refs[k].at[c], refs[k].at[c], send_sems.at[k], recv_sems.at[k], (x, y, 1 - c)).start()

    def drain(refs, send_sems, recv_sems):
        x, y, c, _ = _place()
        for k in range(n):
            got = refs[k].at[1 - c]
            _wait_both(_remote(got, got, send_sems.at[k], recv_sems.at[k], (x, y, c)))

    return _Flight(f"share{tag}", fins, n, issue, drain, thru)


def _pair_blocks(w):
    h, d, _ = w.shape
    z = jnp.zeros((h // 2, d, d), w.dtype)
    return jnp.concatenate([jnp.concatenate([w[0::2], z], axis=2), jnp.concatenate([z, w[1::2]], axis=2)], axis=1)


def _unpair_blocks(b):
    n, dd, _ = b.shape
    d = dd // 2
    return jnp.stack([b[:, :d, :d], b[:, d:, d:]], axis=1).reshape(2 * n, d, d)


def _pad_rows(a, rows):
    return jnp.pad(a, ((0, rows - a.shape[0]), (0, 0)))


class _Packer:
    def __init__(self, shapes, width=1024, row_multiple=64):
        self.shapes = shapes
        self.sizes = [math.prod(s) for s in shapes]
        total = sum(self.sizes)
        self.width = width
        self.rows = -(-total // (width * row_multiple)) * row_multiple
        self.pad = self.rows * width - total

    def pack(self, arrays):
        flat = jnp.concatenate([a.reshape(-1).astype(F32) for a in arrays] + [jnp.zeros((self.pad,), F32)])
        return flat.reshape(self.rows, self.width)

    def unpack(self, packed):
        flat = packed.reshape(-1)
        out, off = [], 0
        for s, n in zip(self.shapes, self.sizes):
            out.append(flat[off:off + n].reshape(s))
            off += n
        return out


SMALL = ["b_ada", "ffn1_norm", "mix_norm", "conv_w", "conv_b", "gate_a_w", "gate_a_b", "gate_x_w", "gate_x_b", "lru_lambda",
         "v_norm", "spatial_w", "spatial_b", "lru_out_norm", "gmlp_out_norm", "ffn2_norm", "final_norm"]
BIG = ["ffn1_w_gu", "ffn1_w_down", "w_in", "w_out", "ffn2_w_gu", "ffn2_w_down"]
GROUPS = (("ffn1_w_gu", "ffn1_w_down"), ("w_in", "w_out"), ("ffn2_w_gu", "ffn2_w_down"))
FWD_GROUPS = (("ffn1_w_gu",), ("ffn1_w_down",), ("w_in", "w_out"), ("ffn2_w_gu",), ("ffn2_w_down",))
MIN_AGE = {"swap": 1, "scatter": 1, "share": 1}
WEIGHTS = ["w_ada", "b_ada", "ffn1_norm", "ffn1_w_gu", "ffn1_w_down", "mix_norm", "w_in", "conv_w", "conv_b", "gate_a_w", "gate_a_b",
           "gate_x_w", "gate_x_b", "lru_lambda", "v_norm", "spatial_w", "spatial_b", "lru_out_norm", "gmlp_out_norm", "w_out",
           "ffn2_norm", "ffn2_w_gu", "ffn2_w_down", "final_norm"]


def kernel(x, c, w_ada, b_ada, ffn1_norm, ffn1_w_gu, ffn1_w_down, mix_norm, w_in, conv_w, conv_b, gate_a_w, gate_a_b, gate_x_w, gate_x_b, lru_lambda, v_norm, spatial_w, spatial_b, lru_out_norm, gmlp_out_norm, w_out, ffn2_norm, ffn2_w_gu, ffn2_w_down, final_norm, loss_target, m_w_ada, m_b_ada, m_ffn1_norm, m_ffn1_w_gu, m_ffn1_w_down, m_mix_norm, m_w_in, m_conv_w, m_conv_b, m_gate_a_w, m_gate_a_b, m_gate_x_w, m_gate_x_b, m_lru_lambda, m_v_norm, m_spatial_w, m_spatial_b, m_lru_out_norm, m_gmlp_out_norm, m_w_out, m_ffn2_norm, m_ffn2_w_gu, m_ffn2_w_down, m_final_norm, v_w_ada, v_b_ada, v_ffn1_norm, v_ffn1_w_gu, v_ffn1_w_down, v_mix_norm, v_w_in, v_conv_w, v_conv_b, v_gate_a_w, v_gate_a_b, v_gate_x_w, v_gate_x_b, v_lru_lambda, v_v_norm, v_spatial_w, v_spatial_b, v_lru_out_norm, v_gmlp_out_norm, v_w_out, v_ffn2_norm, v_ffn2_w_gu, v_ffn2_w_down, v_final_norm):
    given = dict(locals())
    W = {n: given[n] for n in WEIGHTS}
    L = w_ada.shape[0]
    S, D = x.shape[1], x.shape[2]
    LW = conv_b.shape[1]
    hd = LW // HEADS
    xi, yi, ci = lax.axis_index("x"), lax.axis_index("y"), lax.axis_index("c")
    chip = 2 * xi + yi
    dev = 2 * chip + ci
    place = jnp.stack([ci, chip]).astype(jnp.int32)
    xs = x.reshape(S, D)
    tgt = loss_target.reshape(S, D)

    n_ada = w_ada.shape[2]
    cws = LW // N_CHIPS

    def half_view(s):
        return s.reshape(N_CHIPS, 2, s.shape[1] // 2, s.shape[2])

    stages = [(l, names) for l in range(L) for names in FWD_GROUPS]
    seq = [[half_view(_cast_into_slot(W[n], l, place, place)) for n in names] for l, names in stages[:1]]
    flights = {}

    def ici_plan(t):
        if t >= len(stages):
            return []
        stage = t % len(FWD_GROUPS)
        return {2: [(t, 0, 1), (t + 1, 0, 2)], 3: [(t, 1, 2)]}.get(stage, [(t, 0, 1)])

    def launch(t, thru, direct=()):
        ici = [(a, part, parts) for g, part, parts in ici_plan(t) for a in seq[g]]
        d2d = seq[t - 1] if 1 <= t <= len(seq) else []
        if ici or d2d or direct:
            flights[t] = _gather_flight(t, ici, d2d, list(direct), thru)
            thru = flights[t].thru
        return thru

    def land(t, after):
        if t not in flights:
            return []
        out = flights.pop(t).land(after)
        for g, _, _ in ici_plan(t):
            seq[g], out = out[:len(seq[g])], out[len(seq[g]):]
        if 1 <= t <= len(seq):
            seq[t - 1], out = out[:len(seq[t - 1])], out[len(seq[t - 1]):]
        return out

    def group_weights(t):
        return [s.reshape(N_CHIPS, -1, s.shape[3]) for s in seq[t]]

    c_rows = _pad_rows(c, SUBLANES)
    c_flight = _exchange_flight("c", lax.dynamic_update_index_in_dim(jnp.zeros((N_DEV,) + c_rows.shape, F32), c_rows, dev, 0), c_rows)
    started = launch(0, c_flight.thru)
    c_all = c_flight.land(started)[0][:, 0, :]
    b_shard = lax.dynamic_slice_in_dim(b_ada, chip * n_ada, n_ada, axis=1)
    mod_shard = _ada_fwd(_pad_rows(c_all, 2 * SUBLANES), w_ada, b_shard[:, None, :])
    seq += [[half_view(_cast_into_slot(W[n], l, place, started)) for n in names] for l, names in stages[1:]]

    def in_slot(block):
        return lax.dynamic_update_index_in_dim(jnp.zeros((N_CHIPS,) + block.shape, block.dtype), block, chip, 0)

    land(0, seq[-1][-1])
    small = [in_slot(mod_shard.reshape(L * 2 * SUBLANES, n_ada)), in_slot(conv_w.reshape(L * CONV_WIDTH, cws))]
    mod_all, conv_all = land(1, launch(1, mod_shard, small))
    mod_rows = lax.dynamic_index_in_dim(mod_all.reshape(N_CHIPS, L, 2 * SUBLANES, n_ada), dev, axis=2, keepdims=False)
    mod = mod_rows.transpose(1, 0, 2).reshape(L, N_MOD, 1, D)
    conv_full = conv_all.reshape(N_CHIPS, L, CONV_WIDTH, cws).transpose(1, 2, 0, 3).reshape(L, CONV_WIDTH, LW)

    tril = jnp.tril(jnp.ones((CHUNK, CHUNK), F32))
    seg = (jnp.arange(LW)[:, None] // hd == jnp.arange(LW)[None, :] // hd).astype(jnp.bfloat16)

    def mixer_params(l):
        ws = spatial_w[l] * tril
        wsp = jnp.concatenate([ws[0::2], ws[1::2]], axis=2)
        wa, wx = _pair_blocks(gate_a_w[l]), _pair_blocks(gate_x_w[l])
        return dict(
            cw=conv_full[l], cb=conv_b[l][None],
            wa=wa.astype(MXU_DTYPE), wx=wx.astype(MXU_DTYPE), wat=wa.transpose(0, 2, 1).astype(MXU_DTYPE), wxt=wx.transpose(0, 2, 1).astype(MXU_DTYPE),
            ba=gate_a_b[l].reshape(1, LW), bx=gate_x_b[l].reshape(1, LW), lam=lru_lambda[l][None], gv=v_norm[l][None],
            wsp=wsp.astype(MXU_DTYPE), wspt=wsp.transpose(0, 2, 1).astype(MXU_DTYPE),
            bfull=jnp.repeat(spatial_b[l].T, hd, axis=1), g_lru=lru_out_norm[l][None], g_gm=gmlp_out_norm[l][None])

    saved = []
    xcur = xs
    zero_row = jnp.zeros((1, D), F32)
    h = _modnorm(xcur, ffn1_norm[0][None], mod[0][0], mod[0][1])
    for l in range(L):
        mp, md = mixer_params(l), mod[l]
        s = dict(lw={}, mp=mp, md=md)
        lw = s["lw"]
        t = len(FWD_GROUPS) * l
        s["x0"] = xcur
        s["h1"] = launch(t + 2, h)
        lw["gu1"], = group_weights(t)
        s["a1"], s["gu1"] = _ffn_up(s["h1"], lw["gu1"])
        land(t + 2, s["a1"])
        s["a1"] = launch(t + 3, s["a1"])
        lw["d1"] = group_weights(t + 1)[0].reshape(-1, D)
        s["f1"], xcur, h = _mm_res(s["a1"], lw["d1"], xcur, md[2], 0.5, (mix_norm[l][None], md[3], md[4]))
        land(t + 3, xcur)
        s["x1"] = xcur
        s["h2"] = launch(t + 4, h)
        lw["win"], wout = group_weights(t + 2)
        lw["wout"] = wout.reshape(-1, D)
        s["proj"] = _mm_chunks(s["h2"], lw["win"])
        s["ylru"] = _lru_fwd(s["proj"], mp["cw"], mp["cb"], mp["wa"], mp["ba"], mp["wx"], mp["bx"], mp["lam"])
        s["yn"], s["ygm"] = _gmlp_fwd(s["proj"], s["ylru"], mp["gv"], seg, mp["wsp"], mp["bfull"], mp["g_lru"], mp["g_gm"])
        s["f2"], xcur, h = _mm_res(s["yn"], lw["wout"], xcur, md[5], 1.0, (ffn2_norm[l][None], md[6], md[7]))
        land(t + 4, xcur)
        s["x2"] = xcur
        s["h3"] = launch(t + 5, h)
        lw["gu2"], = group_weights(t + 3)
        s["a3"], s["gu3"] = _ffn_up(s["h3"], lw["gu2"])
        land(t + 5, s["a3"])
        s["a3"] = launch(t + 6, s["a3"])
        lw["d2"] = group_weights(t + 4)[0].reshape(-1, D)
        following = (ffn1_norm[l + 1][None], mod[l + 1][0], mod[l + 1][1]) if l + 1 < L else (final_norm[None], zero_row, zero_row)
        s["f3"], xcur, h = _mm_res(s["a3"], lw["d2"], xcur, md[8], 0.5, following)
        land(t + 6, xcur)
        saved.append(s)

    dx, dq, head_acc = _loss_head(xcur, tgt, final_norm[None], saved[-1]["md"][8], 0.5)
    loss = lax.psum(jnp.sum(head_acc[1]), ("x", "y", "c"))
    big_grads = {n: [None] * L for n in BIG}
    dmods = [None] * L

    def ffn_bwd(names, l, dx, dq, x_in, h, a, gu, f, wgu, wd, gn, sc, next_gate, next_scale):
        big_grads[names[1]][l] = _mm_tn_chunks(a, dq[None], 1408, 1024)[0].reshape(N_CHIPS, -1, D)
        dgu = _ffn_bwd_act(dq, wd, gu)
        C = dgu.shape[3]
        dgu4 = dgu.reshape(N_CHIPS, S, C)
        big_grads[names[0]][l] = _mm_tn_chunks(h, dgu4, 1024, C)
        dgu4 = reduce_group(names, l, big_grads[names[0]][l], dgu4)
        dx, dq, acc = _mm_nt_norm_bwd(dgu4, wgu, x_in, dx, f, gn, sc, 0.5, next_gate, next_scale)
        return dx, move_on(dx, dq), acc

    stepped = {n: None for n in BIG}
    reducing = []

    per_layer = [n for n in SMALL if n not in ("b_ada", "final_norm")]
    packers, exchanges = {}, {}
    clock = [0]
    to_step = []

    def step_reduced(after):
        while to_step:
            name, l, g = to_step.pop(0)
            stepped[name] = _adamw_layer(W[name], g, given["m_" + name], given["v_" + name], l, stepped[name], after)
            after = stepped[name][1]
        return after

    def move_on(after, thru, force=False):
        clock[0] += 1
        for grp in list(reducing):
            if not force and clock[0] - grp["since"] < MIN_AGE[grp["step"]]:
                continue
            grp["since"] = clock[0]
            landed = grp["flight"].land(after)
            n = len(grp["names"])
            if grp["step"] == "swap":
                pairs = [_add_half(g4, r1, place) for g4, r1 in zip(landed[:n], landed[n:])]
                grp.update(step="scatter", flight=_scatter_flight(grp["tag"], [h for h, _ in pairs], [own for _, own in pairs], thru))
            elif grp["step"] == "scatter":
                grp.update(step="share", flight=_share_flight(grp["tag"], [_sum4_into_half(r2, place) for r2 in landed[n:]], thru))
            else:
                to_step.extend((name, grp["l"], fin.reshape(2 * fin.shape[1], fin.shape[2])) for name, fin in zip(grp["names"], landed))
                reducing.remove(grp)
                continue
            thru = grp["flight"].thru
        return thru

    def reduce_group(names, l, after, thru):
        thru = move_on(after, thru)
        g4s = [big_grads[n][l].reshape(N_CHIPS, 2, big_grads[n][l].shape[1] // 2, big_grads[n][l].shape[2]) for n in names]
        tag = f"{l}{GROUPS.index(names)}"
        reducing.append(dict(names=names, l=l, tag=tag, step="swap", since=clock[0], flight=_swap_flight(tag, g4s, thru)))
        return reducing[-1]["flight"].thru
    for l in reversed(range(L)):
        s = saved[l]
        lw, mp, md = s["lw"], s["mp"], s["md"]
        dx, dq, acc3 = ffn_bwd(
            GROUPS[2], l, dx, dq, s["x2"], s["h3"], s["a3"], s["gu3"], s["f3"], lw["gu2"], lw["d2"], ffn2_norm[l][None], md[7], md[5], 1.0)
        big_grads["w_out"][l] = _mm_tn_chunks(s["yn"], dq[None], 1024, 1024)[0].reshape(N_CHIPS, -1, D)
        dyn = _mm_nt_chunks(dq[None], lw["wout"][None])
        dylru, duv, dwsp, dbfull, gacc = _gmlp_bwd(s["proj"], s["ylru"], s["ygm"], dyn, mp["gv"], seg, mp["wsp"], mp["wspt"], mp["bfull"], mp["g_lru"], mp["g_gm"])
        dxg, dwa, dwx, lvec = _lru_bwd(s["proj"], dylru, mp["cw"], mp["cb"], mp["wa"], mp["ba"], mp["wx"], mp["bx"], mp["lam"], mp["wat"], mp["wxt"])
        dproj = jnp.concatenate([dxg, duv], axis=0)
        big_grads["w_in"][l] = _mm_tn_chunks(s["h2"], dproj, 1024, LW)
        dproj = reduce_group(GROUPS[1], l, big_grads["w_in"][l], dproj)
        dx, dq, acc2 = _mm_nt_norm_bwd(dproj, lw["win"], s["x1"], dx, s["f2"], mix_norm[l][None], md[4], 1.0, md[2], 0.5)
        dq = move_on(dx, dq)
        if l > 0:
            ng, ns = saved[l - 1]["md"][8], 0.5
        else:
            ng, ns = zero_row, 0.0
        dx, dq, acc1 = ffn_bwd(
            GROUPS[0], l, dx, dq, s["x0"], s["h1"], s["a1"], s["gu1"], s["f1"], lw["gu1"], lw["d1"], ffn1_norm[l][None], md[1], ng, ns)

        dmods[l] = jnp.concatenate([acc1[0:2], acc1[3:4], acc2[0:2], acc2[3:4], acc3[0:2], acc3[3:4]], axis=0)
        dws = jnp.stack([dwsp[:, :, :CHUNK], dwsp[:, :, CHUNK:]], axis=1).reshape(HEADS, CHUNK, CHUNK) * tril
        lg = {"ffn1_norm": acc1[2], "mix_norm": acc2[2], "ffn2_norm": acc3[2],
              "conv_w": lvec[4:8], "conv_b": lvec[3], "gate_a_w": _unpair_blocks(dwa), "gate_a_b": lvec[0].reshape(HEADS, hd),
              "gate_x_w": _unpair_blocks(dwx), "gate_x_b": lvec[1].reshape(HEADS, hd), "lru_lambda": lvec[2], "v_norm": gacc[2],
              "spatial_w": dws, "spatial_b": dbfull.reshape(CHUNK, HEADS, hd).sum(-1).T, "lru_out_norm": gacc[0], "gmlp_out_norm": gacc[1]}
        part = [lg[n] for n in per_layer] + [dmods[l]] + ([head_acc[0]] if l == L - 1 else [])
        packers[l] = _Packer([p.shape for p in part])
        packed = packers[l].pack(part)
        exchanges[l] = _exchange_flight(l, lax.dynamic_update_index_in_dim(jnp.zeros((N_DEV,) + packed.shape, F32), packed, dev, 0), dq)
        dq = exchanges[l].thru

    grad_x = dx.reshape(x.shape)

    done = step_reduced(dq)
    while reducing:
        dq = move_on(done, dq, force=True)
        done = step_reduced(dq)
    summed, dmod_rows = [], []
    for l in range(L):
        gathered, = exchanges[l].land(done)
        summed.append(packers[l].unpack(_sum_leading(gathered)))
        off = sum(packers[l].sizes[:len(per_layer)])
        dmod_rows.append(gathered.reshape(N_DEV, -1)[:, off:off + N_MOD * D])
    grads = {n: jnp.stack([summed[l][k] for l in range(L)]) for k, n in enumerate(per_layer)}
    grads["final_norm"] = summed[L - 1][len(per_layer) + 1]
    grads["b_ada"] = jnp.stack([summed[l][len(per_layer)].reshape(N_MOD * D) for l in range(L)])
    dmod_shard = lax.dynamic_slice_in_dim(jnp.stack(dmod_rows), chip * n_ada, n_ada, axis=2)
    stepped_ada = _ada_step(c_all.T, dmod_shard, w_ada, m_w_ada, v_w_ada)
    grads["w_ada"] = stepped_ada[0]
    grads["conv_w"] = lax.dynamic_slice_in_dim(grads["conv_w"], chip * cws, cws, axis=2)

    delta, new_m, new_v = {}, {}, {}
    for n in BIG:
        grads[n], delta[n], new_m[n], new_v[n] = stepped[n]
    delta["w_ada"], new_m["w_ada"], new_v["w_ada"] = stepped_ada[1:]
    def rows_of(a):
        return a.reshape(-1, a.shape[-1])

    stepped_small = _adamw_small(*[[rows_of(src[n].reshape(W[n].shape)) for n in SMALL]
                                   for src in (W, grads, {n: given["m_" + n] for n in SMALL}, {n: given["v_" + n] for n in SMALL})])
    for k, n in enumerate(SMALL):
        delta[n], new_m[n], new_v[n] = (stepped_small[i * len(SMALL) + k].reshape(W[n].shape) for i in range(3))
    grads = {n: grads[n].reshape(W[n].shape) for n in WEIGHTS}
    return (loss, grad_x, *[grads[n] for n in WEIGHTS], *[delta[n] for n in WEIGHTS], *[new_m[n] for n in WEIGHTS], *[new_v[n] for n in WEIGHTS])
```

```python
import math

import jax
import jax.numpy as jnp
from jax import lax
from jax.experimental import pallas as pl
from jax.experimental.pallas import tpu as pltpu

F32 = jnp.float32
MXU_DTYPE = jnp.bfloat16
ACT_DTYPE = jnp.bfloat16
XFER_DTYPE = jnp.bfloat16
EPS = 1e-6
RG_LRU_C = 8.0
N_MOD = 9
CONV_WIDTH = 4
HEADS = 8
CHUNK = 128
LANES = 128
SUBLANES = 8
N_CHIPS = 4
N_DEV = 8
ADAM_LR, ADAM_B1, ADAM_B2, ADAM_EPS, ADAM_WD, ADAM_STEP = 0.001, 0.9, 0.999, 1e-08, 0.01, 10
VMEM_LIMIT_BYTES = 60 * 1024 * 1024
ROW_TILE_BYTES = 2 << 20
GELU_C = math.sqrt(2.0 / math.pi)
GELU_A = 0.044715

ANY = pl.BlockSpec(memory_space=pl.ANY)
MESH = pl.DeviceIdType.MESH
SDS = jax.ShapeDtypeStruct


def _params(*sem):
    return pltpu.CompilerParams(dimension_semantics=sem, vmem_limit_bytes=VMEM_LIMIT_BYTES)


def _dot(a, b):
    return jnp.dot(a.astype(MXU_DTYPE), b.astype(MXU_DTYPE), preferred_element_type=F32)


def _dot_nt(a, b):
    return lax.dot_general(a.astype(MXU_DTYPE), b.astype(MXU_DTYPE), (((1,), (1,)), ((), ())), preferred_element_type=F32)


def _dot_tn(a, b):
    return lax.dot_general(a.astype(MXU_DTYPE), b.astype(MXU_DTYPE), (((0,), (0,)), ((), ())), preferred_element_type=F32)


def _gelu(x):
    return x * (0.5 * (1.0 + jnp.tanh(GELU_C * (x + GELU_A * (x * x * x)))))


def _gelu_grad(x):
    t = jnp.tanh(GELU_C * (x + GELU_A * (x * x * x)))
    return 0.5 * (1.0 + t) + 0.5 * x * (1.0 - t * t) * (GELU_C * (1.0 + 3.0 * GELU_A * x * x))


def _sigmoid(x):
    return jax.nn.sigmoid(x)


def _sigmoid_by_tanh(x):
    return 0.5 * jnp.tanh(0.5 * x) + 0.5


def _rsqrt_ms(x):
    return lax.rsqrt(jnp.mean(x * x, axis=-1, keepdims=True) + EPS)


def _rowsum(x):
    return jnp.sum(x, axis=0, keepdims=True)


def _tile(n, want):
    t = min(n, want)
    assert n % t == 0, (n, want)
    return t


def _row_tile(rows, row_bytes):
    step = 2 * SUBLANES
    cap = max(step, ROW_TILE_BYTES // row_bytes)
    best = None
    for t in range(step, min(rows, cap) + 1, step):
        if rows % t == 0:
            best = t
    assert best is not None, (rows, row_bytes)
    return best


def _modnorm(x, gn, sh, sc):
    S, D = x.shape
    tm = _tile(S, 1024)

    def body(x_ref, gn_ref, sh_ref, sc_ref, h_ref):
        xv = x_ref[...]
        h = (xv * _rsqrt_ms(xv) * gn_ref[...]) * (1.0 + sc_ref[...]) + sh_ref[...]
        h_ref[...] = h.astype(ACT_DTYPE)

    row = pl.BlockSpec((1, D), lambda i: (0, 0))
    return pl.pallas_call(
        body, name="modnorm", grid=(S // tm,),
        in_specs=[pl.BlockSpec((tm, D), lambda i: (i, 0)), row, row, row],
        out_specs=pl.BlockSpec((tm, D), lambda i: (i, 0)),
        out_shape=SDS((S, D), ACT_DTYPE), compiler_params=_params("parallel"),
    )(x, gn, sh, sc)


def _mm_nt_norm_bwd(ac, wc, x, dxo, f, gn, sc, res_scale, next_gate, next_scale):
    P, S, K = ac.shape
    D = x.shape[1]
    tm = _tile(S, 512)

    def body(a_ref, w_ref, x_ref, dxo_ref, f_ref, gn_ref, sc_ref, ng_ref, dx_ref, dq_ref, acc_ref):
        @pl.when(pl.program_id(0) == 0)
        def _():
            acc_ref[...] = jnp.zeros_like(acc_ref)

        dh = _dot_nt(a_ref[0], w_ref[0])
        for p in range(1, P):
            dh += _dot_nt(a_ref[p], w_ref[p])
        xv, dxo = x_ref[...], dxo_ref[...]
        r = _rsqrt_ms(xv)
        xhat = xv * r
        gn = gn_ref[...]
        dn = dh * (1.0 + sc_ref[...])
        dxh = dn * gn
        dx = dxo + r * (dxh - xhat * jnp.mean(dxh * xhat, axis=-1, keepdims=True))
        dx_ref[...] = dx
        dq_ref[...] = ((next_scale * ng_ref[...]) * dx).astype(ACT_DTYPE)
        acc_ref[0:1, :] += _rowsum(dh)
        acc_ref[1:2, :] += _rowsum(dh * (xhat * gn))
        acc_ref[2:3, :] += _rowsum(dn * xhat)
        acc_ref[3:4, :] += _rowsum((res_scale * f_ref[...]) * dxo)

    tile = pl.BlockSpec((tm, D), lambda i: (i, 0))
    row = pl.BlockSpec((1, D), lambda i: (0, 0))
    return pl.pallas_call(
        body, name=f"mm_nt_norm_bwd_k{K}", grid=(S // tm,),
        in_specs=[pl.BlockSpec((P, tm, K), lambda i: (0, i, 0)),
                  pl.BlockSpec((P, D, K), lambda i: (0, 0, 0), pipeline_mode=pl.Buffered(1)),
                  tile, tile, tile, row, row, row],
        out_specs=[tile, tile, pl.BlockSpec((SUBLANES, D), lambda i: (0, 0))],
        out_shape=[SDS((S, D), F32), SDS((S, D), ACT_DTYPE), SDS((SUBLANES, D), F32)],
        compiler_params=_params("arbitrary"),
    )(ac, wc, x, dxo, f, gn, sc, next_gate)


def _loss_head(x, target, gn, next_gate, next_scale):
    S, D = x.shape
    tm = _tile(S, 512)

    def body(x_ref, t_ref, gn_ref, ng_ref, dx_ref, dq_ref, acc_ref):
        @pl.when(pl.program_id(0) == 0)
        def _():
            acc_ref[...] = jnp.zeros_like(acc_ref)

        xv = x_ref[...]
        r = _rsqrt_ms(xv)
        xhat = xv * r
        gn = gn_ref[...]
        err = xhat * gn - t_ref[...]
        dy = err * (1.0 / D)
        dxh = dy * gn
        dx = r * (dxh - xhat * jnp.mean(dxh * xhat, axis=-1, keepdims=True))
        dx_ref[...] = dx
        dq_ref[...] = ((next_scale * ng_ref[...]) * dx).astype(ACT_DTYPE)
        acc_ref[0:1, :] += _rowsum(dy * xhat)
        acc_ref[1:2, :] += _rowsum(err * err) * (0.5 / D)

    tile = pl.BlockSpec((tm, D), lambda i: (i, 0))
    row = pl.BlockSpec((1, D), lambda i: (0, 0))
    return pl.pallas_call(
        body, name="loss_head", grid=(S // tm,),
        in_specs=[tile, tile, row, row],
        out_specs=[tile, tile, pl.BlockSpec((SUBLANES, D), lambda i: (0, 0))],
        out_shape=[SDS((S, D), F32), SDS((S, D), ACT_DTYPE), SDS((SUBLANES, D), F32)],
        compiler_params=_params("arbitrary"),
    )(x, target, gn, next_gate)


def _ffn_up(h, wgu):
    S, D = h.shape
    C = wgu.shape[2]
    tm = _tile(S, 512)

    def body(h_ref, wg_ref, wu_ref, a_ref, gu_ref):
        hv = h_ref[...]
        g = _dot(hv, wg_ref[...])
        u = _dot(hv, wu_ref[...])
        a_ref[...] = (g * _sigmoid_by_tanh(g) * u).astype(ACT_DTYPE)
        gu_ref[0] = g.astype(ACT_DTYPE)
        gu_ref[1] = u.astype(ACT_DTYPE)

    return pl.pallas_call(
        body, name="ffn_up", grid=(2, S // tm),
        in_specs=[
            pl.BlockSpec((tm, D), lambda j, i: (i, 0)),
            pl.BlockSpec((None, D, C), lambda j, i: (j, 0, 0)),
            pl.BlockSpec((None, D, C), lambda j, i: (2 + j, 0, 0)),
        ],
        out_specs=[
            pl.BlockSpec((tm, C), lambda j, i: (i, j)),
            pl.BlockSpec((2, None, tm, C), lambda j, i: (0, j, i, 0)),
        ],
        out_shape=[SDS((S, 2 * C), ACT_DTYPE), SDS((2, 2, S, C), ACT_DTYPE)],
        compiler_params=_params("parallel", "parallel"),
    )(h, wgu, wgu)


def _ffn_bwd_act(dq, wd, gu):
    S, D = dq.shape
    C = gu.shape[3]
    tm = _tile(S, 512)

    def body(dq_ref, wd_ref, gu_ref, dgu_ref):
        da = _dot_nt(dq_ref[...], wd_ref[...])
        g = gu_ref[0].astype(F32)
        u = gu_ref[1].astype(F32)
        s = _sigmoid_by_tanh(g)
        gs = g * s
        dgu_ref[0] = ((da * u) * (s + gs - gs * s)).astype(ACT_DTYPE)
        dgu_ref[1] = (da * gs).astype(ACT_DTYPE)

    gu_spec = pl.BlockSpec((2, None, tm, C), lambda j, i: (0, j, i, 0))
    return pl.pallas_call(
        body, name="ffn_bwd_act", grid=(2, S // tm),
        in_specs=[pl.BlockSpec((tm, D), lambda j, i: (i, 0)), pl.BlockSpec((C, D), lambda j, i: (j, 0)), gu_spec],
        out_specs=gu_spec,
        out_shape=SDS(gu.shape, ACT_DTYPE),
        compiler_params=_params("parallel", "parallel"),
    )(dq, wd, gu)


def _mm_res(a, w, x, gate, scale, following):
    S, K = a.shape
    D = w.shape[1]
    tm = _tile(S, 512)

    def body(a_ref, w_ref, x_ref, g_ref, gn_ref, sh_ref, sc_ref, f_ref, xo_ref, h_ref):
        f = _dot(a_ref[...], w_ref[...])
        f_ref[...] = f
        xo = x_ref[...] + (scale * g_ref[...]) * f
        xo_ref[...] = xo
        h_ref[...] = ((xo * _rsqrt_ms(xo) * gn_ref[...]) * (1.0 + sc_ref[...]) + sh_ref[...]).astype(ACT_DTYPE)

    tile = pl.BlockSpec((tm, D), lambda i: (i, 0))
    row = pl.BlockSpec((1, D), lambda i: (0, 0))
    return pl.pallas_call(
        body, name=f"mm_res_k{K}", grid=(S // tm,),
        in_specs=[pl.BlockSpec((tm, K), lambda i: (i, 0)), pl.BlockSpec((K, D), lambda i: (0, 0)), tile, row, row, row, row],
        out_specs=[tile, tile, tile],
        out_shape=[SDS((S, D), F32), SDS((S, D), F32), SDS((S, D), ACT_DTYPE)],
        compiler_params=_params("parallel"),
    )(a, w, x, gate, *following)


def _mm_chunks(h, wc):
    S, K = h.shape
    P, _, N = wc.shape
    tm = _tile(S, 512)

    def body(h_ref, w_ref, o_ref):
        hv = h_ref[...]
        for p in range(P):
            o_ref[:, p * N:(p + 1) * N] = _dot(hv, w_ref[p])

    return pl.pallas_call(
        body, name="mm_chunks", grid=(S // tm,),
        in_specs=[pl.BlockSpec((tm, K), lambda i: (i, 0)), pl.BlockSpec((P, K, N), lambda i: (0, 0, 0))],
        out_specs=pl.BlockSpec((tm, P * N), lambda i: (i, 0)),
        out_shape=SDS((S, P * N), F32),
        compiler_params=_params("parallel"),
    )(h, wc)


def _mm_nt_chunks(ac, wc):
    P, S, K = ac.shape
    N = wc.shape[1]
    tm, tn = _tile(S, 512), _tile(N, 1024)

    def body(a_ref, w_ref, o_ref):
        acc = _dot_nt(a_ref[0], w_ref[0])
        for p in range(1, P):
            acc += _dot_nt(a_ref[p], w_ref[p])
        o_ref[...] = acc

    return pl.pallas_call(
        body, name=f"mm_nt_p{P}k{K}", grid=(S // tm, N // tn),
        in_specs=[pl.BlockSpec((P, tm, K), lambda i, j: (0, i, 0)), pl.BlockSpec((P, tn, K), lambda i, j: (0, j, 0))],
        out_specs=pl.BlockSpec((tm, tn), lambda i, j: (i, j)),
        out_shape=SDS((S, N), F32),
        compiler_params=_params("parallel", "parallel"),
    )(ac, wc)


def _mm_tn_chunks(a, bc, tile_m, tile_n):
    S, M = a.shape
    P, _, N = bc.shape
    ts, tm, tn = _tile(S, 2048), _tile(M, tile_m), _tile(N, tile_n)

    def body(a_ref, b_ref, o_ref):
        @pl.when(pl.program_id(3) == 0)
        def _():
            o_ref[...] = jnp.zeros_like(o_ref)

        o_ref[...] += _dot_tn(a_ref[...], b_ref[...])

    return pl.pallas_call(
        body, name=f"mm_tn_m{M}n{N}", grid=(P, M // tm, N // tn, S // ts),
        in_specs=[pl.BlockSpec((ts, tm), lambda p, m, n, k: (k, m)), pl.BlockSpec((None, ts, tn), lambda p, m, n, k: (p, k, n))],
        out_specs=pl.BlockSpec((None, tm, tn), lambda p, m, n, k: (p, m, n)),
        out_shape=SDS((P, M, N), F32),
        compiler_params=_params("parallel", "parallel", "parallel", "arbitrary"),
    )(a, bc)


def _shift_down(x, s, row, fill):
    return jnp.where(row >= s, pltpu.roll(x, s, 0), fill)


def _shift_up(x, s, row, fill):
    n = x.shape[0]
    return jnp.where(row < n - s, pltpu.roll(x, n - s, 0), fill)


def _scan(a, b, row, scratch, up):
    scr_a, scr_b, scr_c = scratch
    n = a.shape[0]
    g = n // SUBLANES
    in_group = row & (SUBLANES - 1)

    def steps(a, b, pos, size):
        s = 1
        while s < size:
            m = (pos + s < size) if up else (pos >= s)
            b = jnp.where(m, a, 0.0) * pltpu.roll(b, a.shape[0] - s if up else s, 0) + b
            a = jnp.where(m, a * pltpu.roll(a, a.shape[0] - s if up else s, 0), a)
            s *= 2
        return a, b

    a, b = steps(a, b, in_group, SUBLANES)
    scr_a[...] = a
    scr_b[...] = b
    edge = 0 if up else SUBLANES - 1
    at = scr_a[pl.ds(edge, g, stride=SUBLANES), :]
    bt = scr_b[pl.ds(edge, g, stride=SUBLANES), :]
    group = lax.broadcasted_iota(jnp.int32, at.shape, 0)
    _, state = steps(at, bt, group, g)
    carry = jnp.where((group + 1 < g) if up else (group >= 1), pltpu.roll(state, g - 1 if up else 1, 0), 0.0)
    for k in range(SUBLANES):
        scr_c[pl.ds(k, g, stride=SUBLANES), :] = carry
    return b + a * scr_c[...]


def _conv(xl, cw_ref, cb_ref, row):
    y = cb_ref[...] + _shift_down(xl, 3, row, 0.0) * cw_ref[0:1, :]
    y = y + _shift_down(xl, 2, row, 0.0) * cw_ref[1:2, :]
    y = y + _shift_down(xl, 1, row, 0.0) * cw_ref[2:3, :]
    return y + xl * cw_ref[3:4, :]


def _lru_gates(xc, wa_ref, ba_ref, wx_ref, bx_ref, lam_ref):
    ra = _sigmoid(_dot(xc, wa_ref[...]) + ba_ref[...])
    ri = _sigmoid(_dot(xc, wx_ref[...]) + bx_ref[...])
    ls = jax.nn.log_sigmoid(lam_ref[...])
    a = jnp.exp((RG_LRU_C * ra) * ls)
    mult = jnp.sqrt(1.0 - a * a)
    return ra, ri, ls, a, mult


def _lru_specs(S):
    col = lambda off: pl.BlockSpec((S, LANES), lambda j: (0, off + j))
    vec = pl.BlockSpec((1, LANES), lambda j: (0, j))
    blk = pl.BlockSpec((None, LANES, LANES), lambda j: (j, 0, 0))
    cw = pl.BlockSpec((CONV_WIDTH, LANES), lambda j: (0, j))
    return col, vec, blk, cw


def _lru_fwd(proj, cw, cb, wa, ba, wx, bx, lam):
    S = proj.shape[0]
    W = cb.shape[1]
    nb = W // LANES

    def body(xl_ref, gl_ref, cw_ref, cb_ref, wa_ref, ba_ref, wx_ref, bx_ref, lam_ref, y_ref, *scratch):
        row = lax.broadcasted_iota(jnp.int32, (S, LANES), 0)
        xc = _conv(xl_ref[...], cw_ref, cb_ref, row)
        _, ri, _, a, mult = _lru_gates(xc, wa_ref, ba_ref, wx_ref, bx_ref, lam_ref)
        h = _scan(a, mult * (ri * xc), row, scratch, up=False)
        y_ref[...] = h * _gelu(gl_ref[...])

    col, vec, blk, cws = _lru_specs(S)
    return pl.pallas_call(
        body, name="lru_fwd", grid=(nb,),
        in_specs=[col(0), col(nb), cws, vec, blk, vec, blk, vec, vec],
        out_specs=pl.BlockSpec((S, LANES), lambda j: (0, j)),
        out_shape=SDS((S, W), F32), scratch_shapes=[pltpu.VMEM((S, LANES), F32)] * 3, compiler_params=_params("parallel"),
    )(proj, proj, cw, cb, wa, ba, wx, bx, lam)


def _lru_bwd(proj, dy, cw, cb, wa, ba, wx, bx, lam, wat, wxt):
    S = proj.shape[0]
    W = cb.shape[1]
    nb = W // LANES

    def body(xl_ref, gl_ref, dy_ref, cw_ref, cb_ref, wa_ref, ba_ref, wx_ref, bx_ref, lam_ref, wat_ref, wxt_ref,
             dp_ref, dwa_ref, dwx_ref, vec_ref, *scratch):
        row = lax.broadcasted_iota(jnp.int32, (S, LANES), 0)
        xl = xl_ref[...]
        xc = _conv(xl, cw_ref, cb_ref, row)
        ra, ri, ls, a, mult = _lru_gates(xc, wa_ref, ba_ref, wx_ref, bx_ref, lam_ref)
        h = _scan(a, mult * (ri * xc), row, scratch, up=False)
        gl = gl_ref[...]
        dyv = dy_ref[...]
        dp_ref[1] = (dyv * h * _gelu_grad(gl)).astype(ACT_DTYPE)
        adj = _scan(_shift_up(a, 1, row, 0.0), dyv * _gelu(gl), row, scratch, up=True)
        da = adj * _shift_down(h, 1, row, 0.0)
        dmult = adj * (ri * xc)
        dlog_a = da * a - dmult * (a * a) / mult
        dra = dlog_a * (RG_LRU_C * ls)
        dpa = dra * ra * (1.0 - ra)
        dpi = (adj * mult * xc) * ri * (1.0 - ri)
        dxc = adj * mult * ri + _dot(dpa, wat_ref[...]) + _dot(dpi, wxt_ref[...])
        dwa_ref[...] = _dot_tn(xc, dpa)
        dwx_ref[...] = _dot_tn(xc, dpi)
        dxl = dxc * cw_ref[3:4, :]
        dxl = dxl + _shift_up(dxc, 1, row, 0.0) * cw_ref[2:3, :]
        dxl = dxl + _shift_up(dxc, 2, row, 0.0) * cw_ref[1:2, :]
        dxl = dxl + _shift_up(dxc, 3, row, 0.0) * cw_ref[0:1, :]
        dp_ref[0] = dxl.astype(ACT_DTYPE)
        vec_ref[...] = jnp.zeros_like(vec_ref)
        vec_ref[0:1, :] = _rowsum(dpa)
        vec_ref[1:2, :] = _rowsum(dpi)
        vec_ref[2:3, :] = _rowsum(dlog_a * (RG_LRU_C * ra)) * _sigmoid(-lam_ref[...])
        vec_ref[3:4, :] = _rowsum(dxc)
        vec_ref[4:5, :] = _rowsum(dxc * _shift_down(xl, 3, row, 0.0))
        vec_ref[5:6, :] = _rowsum(dxc * _shift_down(xl, 2, row, 0.0))
        vec_ref[6:7, :] = _rowsum(dxc * _shift_down(xl, 1, row, 0.0))
        vec_ref[7:8, :] = _rowsum(dxc * xl)

    col, vec, blk, cws = _lru_specs(S)
    return pl.pallas_call(
        body, name="lru_bwd", grid=(nb,),
        in_specs=[col(0), col(nb), col(0), cws, vec, blk, vec, blk, vec, vec, blk, blk],
        out_specs=[pl.BlockSpec((2, S, LANES), lambda j: (0, 0, j)), blk, blk, pl.BlockSpec((2 * SUBLANES, LANES), lambda j: (0, j))],
        out_shape=[SDS((2, S, W), ACT_DTYPE), SDS((nb, LANES, LANES), F32), SDS((nb, LANES, LANES), F32), SDS((2 * SUBLANES, W), F32)],
        scratch_shapes=[pltpu.VMEM((S, LANES), F32)] * 3, compiler_params=_params("parallel"),
    )(proj, proj, dy, cw, cb, wa, ba, wx, bx, lam, wat, wxt)


def _seg_mean(x, seg_ref, width):
    hi = x.astype(jnp.bfloat16)
    lo = (x - hi.astype(F32)).astype(jnp.bfloat16)
    ones = seg_ref[...]
    s = jnp.dot(hi, ones, preferred_element_type=F32) + jnp.dot(lo, ones, preferred_element_type=F32)
    return s * (1.0 / width)


def _gmlp_core(u_ref, v_ref, gv_ref, seg_ref, ws_ref, bfull_ref, z_scr, hd):
    tm, W = u_ref.shape
    lane = lax.broadcasted_iota(jnp.int32, (CHUNK, LANES), 1)
    ug = _gelu(u_ref[...])
    vg = _gelu(v_ref[...])
    cen = vg - _seg_mean(vg, seg_ref, hd)
    rstd = lax.rsqrt(_seg_mean(cen * cen, seg_ref, hd) + EPS)
    vhat = cen * rstd
    vh = vhat * gv_ref[...]
    vcats = {}
    for ci in range(tm // CHUNK):
        for p in range(W // LANES):
            blk = vh[ci * CHUNK:(ci + 1) * CHUNK, p * LANES:(p + 1) * LANES]
            vcat = jnp.concatenate([jnp.where(lane < hd, blk, 0.0), jnp.where(lane >= hd, blk, 0.0)], axis=0).astype(MXU_DTYPE)
            vcats[ci, p] = vcat
            z_scr[ci * CHUNK:(ci + 1) * CHUNK, p * LANES:(p + 1) * LANES] = (
                jnp.dot(ws_ref[p], vcat, preferred_element_type=F32) + bfull_ref[:, p * LANES:(p + 1) * LANES])
    return ug, vhat, rstd, vcats


def _gmlp_specs(tm, W, nb):
    rows = lambda off: pl.BlockSpec((tm, W), lambda i: (i, off))
    vec = pl.BlockSpec((1, W), lambda i: (0, 0))
    seg = pl.BlockSpec((W, W), lambda i: (0, 0))
    wsp = pl.BlockSpec((nb, CHUNK, 2 * CHUNK), lambda i: (0, 0, 0))
    bfull = pl.BlockSpec((CHUNK, W), lambda i: (0, 0))
    return rows, vec, seg, wsp, bfull


def _gmlp_fwd(proj, ylru, gv, seg, wsp, bfull, g_lru, g_gm):
    S, W = ylru.shape
    nb = W // LANES
    hd = W // HEADS
    tm = _tile(S, 512)

    def body(u_ref, v_ref, yl_ref, gv_ref, seg_ref, ws_ref, bfull_ref, gl_ref, gg_ref, yn_ref, ygm_ref, z_scr):
        ug, _, _, _ = _gmlp_core(u_ref, v_ref, gv_ref, seg_ref, ws_ref, bfull_ref, z_scr, hd)
        ygm = ug * z_scr[...]
        ygm_ref[...] = ygm
        yl = yl_ref[...]
        yn_ref[:, 0:W] = (yl * _rsqrt_ms(yl) * gl_ref[...]).astype(ACT_DTYPE)
        yn_ref[:, W:2 * W] = (ygm * _rsqrt_ms(ygm) * gg_ref[...]).astype(ACT_DTYPE)

    rows, vec, segs, wsps, bfulls = _gmlp_specs(tm, W, nb)
    return pl.pallas_call(
        body, name="gmlp_fwd", grid=(S // tm,),
        in_specs=[rows(2), rows(3), rows(0), vec, segs, wsps, bfulls, vec, vec],
        out_specs=[pl.BlockSpec((tm, 2 * W), lambda i: (i, 0)), rows(0)],
        out_shape=[SDS((S, 2 * W), ACT_DTYPE), SDS((S, W), F32)],
        scratch_shapes=[pltpu.VMEM((tm, W), F32)],
        compiler_params=_params("parallel"),
    )(proj, proj, ylru, gv, seg, wsp, bfull, g_lru, g_gm)


def _rms_bwd(y, g, dyn):
    r = _rsqrt_ms(y)
    yhat = y * r
    dyh = dyn * g
    return r * (dyh - yhat * jnp.mean(dyh * yhat, axis=-1, keepdims=True)), _rowsum(dyn * yhat)


def _gmlp_bwd(proj, ylru, ygm, dyn, gv, seg, wsp, wspt, bfull, g_lru, g_gm):
    S, W = ylru.shape
    nb = W // LANES
    hd = W // HEADS
    tm = _tile(S, 256)

    def body(u_ref, v_ref, yl_ref, ygm_ref, dl_ref, dg_ref, gv_ref, seg_ref, ws_ref, wst_ref, bfull_ref, gl_ref, gg_ref,
             dyl_ref, duv_ref, dws_ref, dbf_ref, acc_ref, z_scr, dvh_scr):
        @pl.when(pl.program_id(0) == 0)
        def _():
            dws_ref[...] = jnp.zeros_like(dws_ref)
            dbf_ref[...] = jnp.zeros_like(dbf_ref)
            acc_ref[...] = jnp.zeros_like(acc_ref)

        dyl, dgl = _rms_bwd(yl_ref[...], gl_ref[...], dl_ref[...])
        dyl_ref[...] = dyl
        dygm, dgg = _rms_bwd(ygm_ref[...], gg_ref[...], dg_ref[...])
        ug, vhat, rstd, vcats = _gmlp_core(u_ref, v_ref, gv_ref, seg_ref, ws_ref, bfull_ref, z_scr, hd)
        duv_ref[0] = (dygm * z_scr[...] * _gelu_grad(u_ref[...])).astype(ACT_DTYPE)
        dz = dygm * ug
        lane = lax.broadcasted_iota(jnp.int32, (CHUNK, LANES), 1)
        dbf = dz[0:CHUNK, :]
        for ci in range(1, tm // CHUNK):
            dbf += dz[ci * CHUNK:(ci + 1) * CHUNK, :]
        dbf_ref[...] += dbf
        for ci in range(tm // CHUNK):
            for p in range(nb):
                dzb = dz[ci * CHUNK:(ci + 1) * CHUNK, p * LANES:(p + 1) * LANES].astype(MXU_DTYPE)
                dws_ref[p] += _dot_nt(dzb, vcats[ci, p])
                dvc = jnp.dot(wst_ref[p], dzb, preferred_element_type=F32)
                dvh_scr[ci * CHUNK:(ci + 1) * CHUNK, p * LANES:(p + 1) * LANES] = jnp.where(lane < hd, dvc[0:CHUNK], dvc[CHUNK:2 * CHUNK])
        dvh = dvh_scr[...]
        dvn = dvh * gv_ref[...]
        dvg = rstd * (dvn - _seg_mean(dvn, seg_ref, hd) - vhat * _seg_mean(dvn * vhat, seg_ref, hd))
        duv_ref[1] = (dvg * _gelu_grad(v_ref[...])).astype(ACT_DTYPE)
        acc_ref[0:1, :] += dgl
        acc_ref[1:2, :] += dgg
        acc_ref[2:3, :] += _rowsum(dvh * vhat)

    rows, vec, segs, wsps, bfulls = _gmlp_specs(tm, W, nb)
    wspt_spec = pl.BlockSpec((nb, 2 * CHUNK, CHUNK), lambda i: (0, 0, 0))
    return pl.pallas_call(
        body, name="gmlp_bwd", grid=(S // tm,),
        in_specs=[rows(2), rows(3), rows(0), rows(0), rows(0), rows(1), vec, segs, wsps, wspt_spec, bfulls, vec, vec],
        out_specs=[rows(0), pl.BlockSpec((2, tm, W), lambda i: (0, i, 0)), wsps, bfulls, pl.BlockSpec((SUBLANES, W), lambda i: (0, 0))],
        out_shape=[SDS((S, W), F32), SDS((2, S, W), ACT_DTYPE), SDS((nb, CHUNK, 2 * CHUNK), F32), SDS((CHUNK, W), F32), SDS((SUBLANES, W), F32)],
        scratch_shapes=[pltpu.VMEM((tm, W), F32), pltpu.VMEM((tm, W), F32)],
        compiler_params=_params("arbitrary"),
    )(proj, proj, ylru, ygm, dyn, dyn, gv, seg, wsp, wspt, bfull, g_lru, g_gm)


def _ada_fwd(c_all, w_ada, b_shard):
    L, D, N = w_ada.shape
    R = c_all.shape[0]
    tn = N // 2

    def body(c_ref, w_ref, b_ref, o_ref):
        cv = c_ref[...]
        o_ref[...] = _dot(cv * _sigmoid(cv), w_ref[...]) + b_ref[...]

    return pl.pallas_call(
        body, name="ada_fwd", grid=(L, N // tn),
        in_specs=[pl.BlockSpec((R, D), lambda l, j: (0, 0)), pl.BlockSpec((None, D, tn), lambda l, j: (l, 0, j)),
                  pl.BlockSpec((None, 1, tn), lambda l, j: (l, 0, j))],
        out_specs=pl.BlockSpec((None, R, tn), lambda l, j: (l, 0, j)),
        out_shape=SDS((L, R, N), F32), compiler_params=_params("parallel", "parallel"),
    )(c_all, w_ada, b_shard)


def _ada_step(c_all_t, dmod, w, m, v):
    D, B = c_all_t.shape
    L, _, N = dmod.shape
    tn = 3 * LANES
    assert N % tn == 0, N

    def body(c_ref, d_ref, w_ref, m_ref, v_ref, g_ref, do_ref, mo_ref, vo_ref):
        cv = c_ref[...]
        sc = cv * _sigmoid(cv)
        g = sc[:, 0:1] * d_ref[0:1, :]
        for b in range(1, B):
            g += sc[:, b:b + 1] * d_ref[b:b + 1, :]
        g_ref[...] = g
        do_ref[...], mo_ref[...], vo_ref[...] = _adam_math(w_ref[...], g, m_ref[...], v_ref[...])

    tile = pl.BlockSpec((None, D, tn), lambda l, j: (l, 0, j))
    return pl.pallas_call(
        body, name="ada_step", grid=(L, N // tn),
        in_specs=[pl.BlockSpec((D, B), lambda l, j: (0, 0)), pl.BlockSpec((None, B, tn), lambda l, j: (l, 0, j)), tile, tile, tile],
        out_specs=[tile] * 4, out_shape=[SDS((L, D, N), F32)] * 4, compiler_params=_params("parallel", "parallel"),
    )(c_all_t, dmod, w, m, v)


def _adam_math(w, g, m, v):
    mn = ADAM_B1 * m + (1.0 - ADAM_B1) * g
    vn = ADAM_B2 * v + (1.0 - ADAM_B2) * (g * g)
    m_hat = mn / (1.0 - ADAM_B1 ** ADAM_STEP)
    v_hat = vn / (1.0 - ADAM_B2 ** ADAM_STEP)
    return -ADAM_LR * (m_hat / (jnp.sqrt(v_hat) + ADAM_EPS) + ADAM_WD * w), mn, vn


def _adamw_layer(w, g, m, v, l, prev, after):
    L, R, C = w.shape
    tr = _row_tile(R, C * 4)
    prev = (after,) + (() if prev is None else tuple(prev))

    def body(w_ref, g_ref, m_ref, v_ref, *rest):
        go_ref, d_ref, mo_ref, vo_ref = rest[len(prev):]
        gv = g_ref[...]
        go_ref[...] = gv
        d_ref[...], mo_ref[...], vo_ref[...] = _adam_math(w_ref[...], gv, m_ref[...], v_ref[...])

    lay = pl.BlockSpec((None, tr, C), lambda i: (l, i, 0))
    return pl.pallas_call(
        body, name=f"adamw_layer_r{R}c{C}", grid=(R // tr,),
        in_specs=[lay, pl.BlockSpec((tr, C), lambda i: (i, 0)), lay, lay] + [ANY] * len(prev), out_specs=[lay] * 4,
        out_shape=[SDS((L, R, C), F32)] * 4, input_output_aliases={5 + k: k for k in range(len(prev) - 1)},
        compiler_params=_params("parallel"),
    )(w, g, m, v, *prev)


def _adamw_small(ws, gs, ms, vs):
    n = len(ws)

    def body(*refs):
        outs = refs[4 * n:]
        for k in range(n):
            outs[k][...], outs[n + k][...], outs[2 * n + k][...] = _adam_math(
                refs[k][...], refs[n + k][...], refs[2 * n + k][...], refs[3 * n + k][...])

    whole = pl.BlockSpec(memory_space=pltpu.VMEM)
    return pl.pallas_call(
        body, name="adamw_small", in_specs=[whole] * (4 * n), out_specs=[whole] * (3 * n),
        out_shape=[SDS(w.shape, F32) for w in ws] * 3, compiler_params=pltpu.CompilerParams(vmem_limit_bytes=VMEM_LIMIT_BYTES),
    )(*ws, *gs, *ms, *vs)


def _sum_leading(a):
    P, R, C = a.shape
    tr = _row_tile(R, P * C * 4)

    def body(a_ref, o_ref):
        acc = a_ref[0]
        for p in range(1, P):
            acc = acc + a_ref[p]
        o_ref[...] = acc

    return pl.pallas_call(
        body, name=f"sum{P}_r{R}c{C}", grid=(R // tr,),
        in_specs=[pl.BlockSpec((P, tr, C), lambda i: (0, i, 0))],
        out_specs=pl.BlockSpec((tr, C), lambda i: (i, 0)),
        out_shape=SDS((R, C), F32), compiler_params=_params("parallel"),
    )(a)


def _add_half(g4, r1, place):
    _, _, R, C = g4.shape
    tr = _row_tile(R, C * 4)

    def body(place_ref, g_ref, r_ref, h_ref, own_ref):
        s = (g_ref[...] + r_ref[...]).astype(XFER_DTYPE)
        h_ref[...] = s

        @pl.when(pl.program_id(1) == place_ref[1])
        def _():
            own_ref[...] = s

    return pl.pallas_call(
        body, name=f"add_half_r{R}c{C}",
        grid_spec=pltpu.PrefetchScalarGridSpec(
            num_scalar_prefetch=1, grid=(R // tr, N_CHIPS),
            in_specs=[pl.BlockSpec((None, None, tr, C), lambda i, p, place_ref: (p, place_ref[0], i, 0)),
                      pl.BlockSpec((None, tr, C), lambda i, p, place_ref: (p, i, 0))],
            out_specs=[pl.BlockSpec((None, tr, C), lambda i, p, place_ref: (p, i, 0)),
                       pl.BlockSpec((None, tr, C), lambda i, p, place_ref: (place_ref[1], i, 0))],
        ),
        out_shape=[SDS((N_CHIPS, R, C), XFER_DTYPE)] * 2, compiler_params=_params("parallel", "arbitrary"),
    )(place, g4, r1)


def _sum4_into_half(r2, place):
    P, R, C = r2.shape
    tr = _row_tile(R, P * C * 4)

    def body(place_ref, a_ref, o_ref):
        acc = a_ref[0].astype(F32)
        for p in range(1, P):
            acc = acc + a_ref[p].astype(F32)
        o_ref[...] = acc

    return pl.pallas_call(
        body, name=f"sum4_r{R}c{C}",
        grid_spec=pltpu.PrefetchScalarGridSpec(
            num_scalar_prefetch=1, grid=(R // tr,),
            in_specs=[pl.BlockSpec((P, tr, C), lambda i, place_ref: (0, i, 0))],
            out_specs=pl.BlockSpec((None, tr, C), lambda i, place_ref: (place_ref[0], i, 0)),
        ),
        out_shape=SDS((2, R, C), F32), compiler_params=_params("parallel"),
    )(place, r2)


def _cast_into_slot(w, l, place, after):
    _, R, C = w.shape
    tr = _row_tile(R, C * 4)

    def body(place_ref, w_ref, after_ref, o_ref):
        o_ref[...] = w_ref[...].astype(MXU_DTYPE)

    return pl.pallas_call(
        body, name=f"cast_r{R}c{C}",
        grid_spec=pltpu.PrefetchScalarGridSpec(
            num_scalar_prefetch=1, grid=(R // tr,),
            in_specs=[pl.BlockSpec((None, tr, C), lambda i, place_ref: (l, i, 0)), ANY],
            out_specs=pl.BlockSpec((None, tr, C), lambda i, place_ref: (place_ref[1], i, 0)),
        ),
        out_shape=SDS((N_CHIPS, R, C), MXU_DTYPE), compiler_params=_params("parallel"),
    )(place, w, after)


def _place():
    x, y, c = lax.axis_index("x"), lax.axis_index("y"), lax.axis_index("c")
    chips = [(1 - x, y), (x, 1 - y), (1 - x, 1 - y)]
    return x, y, c, chips


def _remote(src, dst, send_sem, recv_sem, to):
    return pltpu.make_async_remote_copy(src_ref=src, dst_ref=dst, send_sem=send_sem, recv_sem=recv_sem, device_id=to, device_id_type=MESH)


HBM_SPEC = pl.BlockSpec(memory_space=pltpu.HBM)
SEM_SPEC = pl.BlockSpec(memory_space=pltpu.SEMAPHORE)
DATAFLOW = pltpu.SideEffectType.DATAFLOW_SIDE_EFFECTING


def _in_hbm(a):
    return pltpu.with_memory_space_constraint(a, pltpu.HBM)


def _hbm_like(a):
    return pltpu.HBM(a.shape, a.dtype)


def _split_start(name, arrays, n_sems, issue, extra=()):
    m = len(arrays)

    def body(*refs):
        issue(refs[:m], refs[m + len(extra)], refs[m + len(extra) + 1])

    out = pl.pallas_call(
        body, name=name,
        out_shape=(pltpu.SemaphoreType.DMA((n_sems,)), pltpu.SemaphoreType.DMA((n_sems,)), *[_hbm_like(a) for a in arrays]),
        in_specs=[HBM_SPEC] * m + [ANY] * len(extra), out_specs=(SEM_SPEC, SEM_SPEC, *[HBM_SPEC] * m),
        input_output_aliases={k: 2 + k for k in range(m)},
        compiler_params=pltpu.CompilerParams(has_side_effects=DATAFLOW),
    )(*[_in_hbm(a) for a in arrays], *extra)
    return out[0], out[1], list(out[2:])


def _split_wait(name, send_sems, recv_sems, arrays, after, drain):
    m = len(arrays)

    def body(*refs):
        drain(refs[:m], refs[m], refs[m + 1])

    out = pl.pallas_call(
        body, name=name, out_shape=[_hbm_like(a) for a in arrays],
        in_specs=[HBM_SPEC] * m + [SEM_SPEC, SEM_SPEC, ANY], out_specs=[HBM_SPEC] * m,
        input_output_aliases={k: k for k in range(m)},
        compiler_params=pltpu.CompilerParams(has_side_effects=DATAFLOW),
    )(*arrays, send_sems, recv_sems, after)
    return list(out)


def _wait_both(cp):
    cp.wait_send()
    cp.wait_recv()


class _Flight:
    def __init__(self, name, arrays, n_sems, issue, drain, thru, extra=()):
        self.name, self.drain, self.n = name, drain, len(arrays)
        self.send, self.recv, out = _split_start(name + "_start", [*arrays, thru], n_sems, issue, extra)
        self.arrays, self.thru = out[:-1], out[-1]

    def land(self, after):
        return _split_wait(self.name + "_wait", self.send, self.recv, self.arrays, after, self.drain)


def _gather_flight(tag, ici, d2d, direct, thru):
    kinds = ["ici"] * len(ici) + ["d2d"] * len(d2d) + ["direct"] * len(direct)
    rows = [pl.ds(part * (a.shape[2] // parts), a.shape[2] // parts) for a, part, parts in ici]
    ici = [a for a, _, _ in ici]

    def issue(refs, send_sems, recv_sems):
        x, y, c, chips = _place()
        q = 2 * x + y
        for k, kind in enumerate(kinds):
            for j, chip in enumerate(chips):
                if kind == "ici":
                    src, to = refs[k].at[q, c, rows[k]], (*chip, c)
                elif kind == "d2d":
                    src, to = refs[k].at[2 * chip[0] + chip[1], c], (x, y, 1 - c)
                else:
                    src, to = refs[k].at[q], (*chip, c)
                _remote(src, src, send_sems.at[3 * k + j], recv_sems.at[3 * k + j], to).start()

    def drain(refs, send_sems, recv_sems):
        x, y, c, chips = _place()
        for k, kind in enumerate(kinds):
            for j, chip in enumerate(chips):
                p = 2 * chip[0] + chip[1]
                got = refs[k].at[p] if kind == "direct" else refs[k].at[p, c, rows[k]] if kind == "ici" else refs[k].at[p, 1 - c]
                _wait_both(_remote(got, got, send_sems.at[3 * k + j], recv_sems.at[3 * k + j], (x, y, c)))

    return _Flight(f"gather{tag}", [*ici, *d2d, *direct], 3 * len(kinds), issue, drain, thru)


def _swap_flight(tag, g4s, thru):
    n = len(g4s)
    zones = [lax.empty((N_CHIPS,) + g.shape[2:], g.dtype) for g in g4s]

    def issue(refs, send_sems, recv_sems):
        x, y, c, _ = _place()
        for k in range(n):
            for p in range(N_CHIPS):
                _remote(refs[k].at[p, 1 - c], refs[n + k].at[p], send_sems.at[N_CHIPS * k + p], recv_sems.at[N_CHIPS * k + p], (x, y, 1 - c)).start()

    def drain(refs, send_sems, recv_sems):
        x, y, c, _ = _place()
        for k in range(n):
            for p in range(N_CHIPS):
                got = refs[n + k].at[p]
                _wait_both(_remote(got, got, send_sems.at[N_CHIPS * k + p], recv_sems.at[N_CHIPS * k + p], (x, y, c)))

    return _Flight(f"swap{tag}", [*g4s, *zones], N_CHIPS * n, issue, drain, thru)


def _scatter_flight(tag, hs, lands, thru):
    n = len(hs)

    def issue(refs, send_sems, recv_sems):
        x, y, c, chips = _place()
        q = 2 * x + y
        for k in range(n):
            for j, chip in enumerate(chips):
                _remote(refs[k].at[2 * chip[0] + chip[1]], refs[n + k].at[q], send_sems.at[3 * k + j], recv_sems.at[3 * k + j], (*chip, c)).start()

    def drain(refs, send_sems, recv_sems):
        x, y, c, chips = _place()
        for k in range(n):
            for j, chip in enumerate(chips):
                got = refs[n + k].at[2 * chip[0] + chip[1]]
                _wait_both(_remote(got, got, send_sems.at[3 * k + j], recv_sems.at[3 * k + j], (x, y, c)))

    return _Flight(f"scatter{tag}", [*hs, *lands], 3 * n, issue, drain, thru)


def _exchange_flight(tag, buf, thru):
    flips = [(fx, fy, fc) for fx in (0, 1) for fy in (0, 1) for fc in (0, 1)][1:]

    def peers():
        x, y, c, _ = _place()
        return (x, y, c), [((1 - x) if fx else x, (1 - y) if fy else y, (1 - c) if fc else c) for fx, fy, fc in flips]

    def slot(ref, dev):
        return ref.at[4 * dev[0] + 2 * dev[1] + dev[2]]

    def issue(refs, send_sems, recv_sems):
        me, others = peers()
        for j, to in enumerate(others):
            _remote(slot(refs[0], me), slot(refs[0], me), send_sems.at[j], recv_sems.at[j], to).start()

    def drain(refs, send_sems, recv_sems):
        me, others = peers()
        for j, frm in enumerate(others):
            got = slot(refs[0], frm)
            _wait_both(_remote(got, got, send_sems.at[j], recv_sems.at[j], me))

    return _Flight(f"exchange{tag}", [buf], len(flips), issue, drain, thru)


def _share_flight(tag, fins, thru):
    n = len(fins)

    def issue(refs, send_sems, recv_sems):
        x, y, c, _ = _place()
        for k in range(n):
            _remote(refs[k].at[c], refs[k].at[c], send_sems.at[k], recv_sems.at[k], (x, y, 1 - c)).start()

    def drain(refs, send_sems, recv_sems):
        x, y, c, _ = _place()
        for k in range(n):
            got = refs[k].at[1 - c]
            _wait_both(_remote(got, got, send_sems.at[k], recv_sems.at[k], (x, y, c)))

    return _Flight(f"share{tag}", fins, n, issue, drain, thru)


def _pair_blocks(w):
    h, d, _ = w.shape
    z = jnp.zeros((h // 2, d, d), w.dtype)
    return jnp.concatenate([jnp.concatenate([w[0::2], z], axis=2), jnp.concatenate([z, w[1::2]], axis=2)], axis=1)


def _unpair_blocks(b):
    n, dd, _ = b.shape
    d = dd // 2
    return jnp.stack([b[:, :d, :d], b[:, d:, d:]], axis=1).reshape(2 * n, d, d)


def _pad_rows(a, rows):
    return jnp.pad(a, ((0, rows - a.shape[0]), (0, 0)))


class _Packer:
    def __init__(self, shapes, width=1024, row_multiple=64):
        self.shapes = shapes
        self.sizes = [math.prod(s) for s in shapes]
        total = sum(self.sizes)
        self.width = width
        self.rows = -(-total // (width * row_multiple)) * row_multiple
        self.pad = self.rows * width - total

    def pack(self, arrays):
        flat = jnp.concatenate([a.reshape(-1).astype(F32) for a in arrays] + [jnp.zeros((self.pad,), F32)])
        return flat.reshape(self.rows, self.width)

    def unpack(self, packed):
        flat = packed.reshape(-1)
        out, off = [], 0
        for s, n in zip(self.shapes, self.sizes):
            out.append(flat[off:off + n].reshape(s))
            off += n
        return out


SMALL = ["b_ada", "ffn1_norm", "mix_norm", "conv_w", "conv_b", "gate_a_w", "gate_a_b", "gate_x_w", "gate_x_b", "lru_lambda",
         "v_norm", "spatial_w", "spatial_b", "lru_out_norm", "gmlp_out_norm", "ffn2_norm", "final_norm"]
BIG = ["ffn1_w_gu", "ffn1_w_down", "w_in", "w_out", "ffn2_w_gu", "ffn2_w_down"]
GROUPS = (("ffn1_w_gu", "ffn1_w_down"), ("w_in", "w_out"), ("ffn2_w_gu", "ffn2_w_down"))
FWD_GROUPS = (("ffn1_w_gu",), ("ffn1_w_down",), ("w_in", "w_out"), ("ffn2_w_gu",), ("ffn2_w_down",))
MIN_AGE = {"swap": 1, "scatter": 1, "share": 1}
WEIGHTS = ["w_ada", "b_ada", "ffn1_norm", "ffn1_w_gu", "ffn1_w_down", "mix_norm", "w_in", "conv_w", "conv_b", "gate_a_w", "gate_a_b",
           "gate_x_w", "gate_x_b", "lru_lambda", "v_norm", "spatial_w", "spatial_b", "lru_out_norm", "gmlp_out_norm", "w_out",
           "ffn2_norm", "ffn2_w_gu", "ffn2_w_down", "final_norm"]


def kernel(x, c, w_ada, b_ada, ffn1_norm, ffn1_w_gu, ffn1_w_down, mix_norm, w_in, conv_w, conv_b, gate_a_w, gate_a_b, gate_x_w, gate_x_b, lru_lambda, v_norm, spatial_w, spatial_b, lru_out_norm, gmlp_out_norm, w_out, ffn2_norm, ffn2_w_gu, ffn2_w_down, final_norm, loss_target, m_w_ada, m_b_ada, m_ffn1_norm, m_ffn1_w_gu, m_ffn1_w_down, m_mix_norm, m_w_in, m_conv_w, m_conv_b, m_gate_a_w, m_gate_a_b, m_gate_x_w, m_gate_x_b, m_lru_lambda, m_v_norm, m_spatial_w, m_spatial_b, m_lru_out_norm, m_gmlp_out_norm, m_w_out, m_ffn2_norm, m_ffn2_w_gu, m_ffn2_w_down, m_final_norm, v_w_ada, v_b_ada, v_ffn1_norm, v_ffn1_w_gu, v_ffn1_w_down, v_mix_norm, v_w_in, v_conv_w, v_conv_b, v_gate_a_w, v_gate_a_b, v_gate_x_w, v_gate_x_b, v_lru_lambda, v_v_norm, v_spatial_w, v_spatial_b, v_lru_out_norm, v_gmlp_out_norm, v_w_out, v_ffn2_norm, v_ffn2_w_gu, v_ffn2_w_down, v_final_norm):
    given = dict(locals())
    W = {n: given[n] for n in WEIGHTS}
    L = w_ada.shape[0]
    S, D = x.shape[1], x.shape[2]
    LW = conv_b.shape[1]
    hd = LW // HEADS
    xi, yi, ci = lax.axis_index("x"), lax.axis_index("y"), lax.axis_index("c")
    chip = 2 * xi + yi
    dev = 2 * chip + ci
    place = jnp.stack([ci, chip]).astype(jnp.int32)
    xs = x.reshape(S, D)
    tgt = loss_target.reshape(S, D)

    n_ada = w_ada.shape[2]
    cws = LW // N_CHIPS

    def half_view(s):
        return s.reshape(N_CHIPS, 2, s.shape[1] // 2, s.shape[2])

    stages = [(l, names) for l in range(L) for names in FWD_GROUPS]
    seq = [[half_view(_cast_into_slot(W[n], l, place, place)) for n in names] for l, names in stages[:2]]
    flights = {}

    def ici_plan(t):
        if t >= len(stages) or t == 1:
            return []
        if t == 0:
            return [(0, 0, 1), (1, 0, 1)]
        stage = t % len(FWD_GROUPS)
        return {2: [(t, 0, 1), (t + 1, 0, 2)], 3: [(t, 1, 2)]}.get(stage, [(t, 0, 1)])

    def launch(t, thru, direct=()):
        ici = [(a, part, parts) for g, part, parts in ici_plan(t) for a in seq[g]]
        d2d = seq[t - 1] if 1 <= t <= len(seq) else []
        if ici or d2d or direct:
            flights[t] = _gather_flight(t, ici, d2d, list(direct), thru)
            thru = flights[t].thru
        return thru

    def land(t, after):
        if t not in flights:
            return []
        out = flights.pop(t).land(after)
        for g, _, _ in ici_plan(t):
            seq[g], out = out[:len(seq[g])], out[len(seq[g]):]
        if 1 <= t <= len(seq):
            seq[t - 1], out = out[:len(seq[t - 1])], out[len(seq[t - 1]):]
        return out

    def group_weights(t):
        return [s.reshape(N_CHIPS, -1, s.shape[3]) for s in seq[t]]

    c_rows = _pad_rows(c, SUBLANES)
    c_flight = _exchange_flight("c", lax.dynamic_update_index_in_dim(jnp.zeros((N_DEV,) + c_rows.shape, F32), c_rows, dev, 0), c_rows)
    started = launch(0, c_flight.thru)
    c_all = c_flight.land(started)[0][:, 0, :]
    b_shard = lax.dynamic_slice_in_dim(b_ada, chip * n_ada, n_ada, axis=1)
    mod_shard = _ada_fwd(_pad_rows(c_all, 2 * SUBLANES), w_ada, b_shard[:, None, :])
    seq += [[half_view(_cast_into_slot(W[n], l, place, started)) for n in names] for l, names in stages[2:]]

    def in_slot(block):
        return lax.dynamic_update_index_in_dim(jnp.zeros((N_CHIPS,) + block.shape, block.dtype), block, chip, 0)

    land(0, seq[-1][-1])
    small = [in_slot(mod_shard.reshape(L * 2 * SUBLANES, n_ada)), in_slot(conv_w.reshape(L * CONV_WIDTH, cws))]
    mod_all, conv_all = land(1, launch(1, mod_shard, small))
    mod_rows = lax.dynamic_index_in_dim(mod_all.reshape(N_CHIPS, L, 2 * SUBLANES, n_ada), dev, axis=2, keepdims=False)
    mod = mod_rows.transpose(1, 0, 2).reshape(L, N_MOD, 1, D)
    conv_full = conv_all.reshape(N_CHIPS, L, CONV_WIDTH, cws).transpose(1, 2, 0, 3).reshape(L, CONV_WIDTH, LW)

    tril = jnp.tril(jnp.ones((CHUNK, CHUNK), F32))
    seg = (jnp.arange(LW)[:, None] // hd == jnp.arange(LW)[None, :] // hd).astype(jnp.bfloat16)

    def mixer_params(l):
        ws = spatial_w[l] * tril
        wsp = jnp.concatenate([ws[0::2], ws[1::2]], axis=2)
        wa, wx = _pair_blocks(gate_a_w[l]), _pair_blocks(gate_x_w[l])
        return dict(
            cw=conv_full[l], cb=conv_b[l][None],
            wa=wa.astype(MXU_DTYPE), wx=wx.astype(MXU_DTYPE), wat=wa.transpose(0, 2, 1).astype(MXU_DTYPE), wxt=wx.transpose(0, 2, 1).astype(MXU_DTYPE),
            ba=gate_a_b[l].reshape(1, LW), bx=gate_x_b[l].reshape(1, LW), lam=lru_lambda[l][None], gv=v_norm[l][None],
            wsp=wsp.astype(MXU_DTYPE), wspt=wsp.transpose(0, 2, 1).astype(MXU_DTYPE),
            bfull=jnp.repeat(spatial_b[l].T, hd, axis=1), g_lru=lru_out_norm[l][None], g_gm=gmlp_out_norm[l][None])

    saved = []
    xcur = xs
    zero_row = jnp.zeros((1, D), F32)
    h = _modnorm(xcur, ffn1_norm[0][None], mod[0][0], mod[0][1])
    for l in range(L):
        mp, md = mixer_params(l), mod[l]
        s = dict(lw={}, mp=mp, md=md)
        lw = s["lw"]
        t = len(FWD_GROUPS) * l
        s["x0"] = xcur
        s["h1"] = launch(t + 2, h)
        lw["gu1"], = group_weights(t)
        s["a1"], s["gu1"] = _ffn_up(s["h1"], lw["gu1"])
        land(t + 2, s["a1"])
        s["a1"] = launch(t + 3, s["a1"])
        lw["d1"] = group_weights(t + 1)[0].reshape(-1, D)
        s["f1"], xcur, h = _mm_res(s["a1"], lw["d1"], xcur, md[2], 0.5, (mix_norm[l][None], md[3], md[4]))
        land(t + 3, xcur)
        s["x1"] = xcur
        s["h2"] = launch(t + 4, h)
        lw["win"], wout = group_weights(t + 2)
        lw["wout"] = wout.reshape(-1, D)
        s["proj"] = _mm_chunks(s["h2"], lw["win"])
        s["ylru"] = _lru_fwd(s["proj"], mp["cw"], mp["cb"], mp["wa"], mp["ba"], mp["wx"], mp["bx"], mp["lam"])
        s["yn"], s["ygm"] = _gmlp_fwd(s["proj"], s["ylru"], mp["gv"], seg, mp["wsp"], mp["bfull"], mp["g_lru"], mp["g_gm"])
        s["f2"], xcur, h = _mm_res(s["yn"], lw["wout"], xcur, md[5], 1.0, (ffn2_norm[l][None], md[6], md[7]))
        land(t + 4, xcur)
        s["x2"] = xcur
        s["h3"] = launch(t + 5, h)
        lw["gu2"], = group_weights(t + 3)
        s["a3"], s["gu3"] = _ffn_up(s["h3"], lw["gu2"])
        land(t + 5, s["a3"])
        s["a3"] = launch(t + 6, s["a3"])
        lw["d2"] = group_weights(t + 4)[0].reshape(-1, D)
        following = (ffn1_norm[l + 1][None], mod[l + 1][0], mod[l + 1][1]) if l + 1 < L else (final_norm[None], zero_row, zero_row)
        s["f3"], xcur, h = _mm_res(s["a3"], lw["d2"], xcur, md[8], 0.5, following)
        land(t + 6, xcur)
        saved.append(s)

    dx, dq, head_acc = _loss_head(xcur, tgt, final_norm[None], saved[-1]["md"][8], 0.5)
    loss = lax.psum(jnp.sum(head_acc[1]), ("x", "y", "c"))
    big_grads = {n: [None] * L for n in BIG}
    dmods = [None] * L

    def ffn_bwd(names, l, dx, dq, x_in, h, a, gu, f, wgu, wd, gn, sc, next_gate, next_scale):
        big_grads[names[1]][l] = _mm_tn_chunks(a, dq[None], 1408, 1024)[0].reshape(N_CHIPS, -1, D)
        dgu = _ffn_bwd_act(dq, wd, gu)
        C = dgu.shape[3]
        dgu4 = dgu.reshape(N_CHIPS, S, C)
        big_grads[names[0]][l] = _mm_tn_chunks(h, dgu4, 1024, C)
        dgu4 = reduce_group(names, l, big_grads[names[0]][l], dgu4)
        dx, dq, acc = _mm_nt_norm_bwd(dgu4, wgu, x_in, dx, f, gn, sc, 0.5, next_gate, next_scale)
        return dx, move_on(dx, dq), acc

    stepped = {n: None for n in BIG}
    reducing = []

    per_layer = [n for n in SMALL if n not in ("b_ada", "final_norm")]
    packers, exchanges = {}, {}
    clock = [0]
    to_step = []

    def step_reduced(after):
        while to_step:
            name, l, g = to_step.pop(0)
            stepped[name] = _adamw_layer(W[name], g, given["m_" + name], given["v_" + name], l, stepped[name], after)
            after = stepped[name][1]
        return after

    def move_on(after, thru, force=False):
        clock[0] += 1
        for grp in list(reducing):
            if not force and clock[0] - grp["since"] < MIN_AGE[grp["step"]]:
                continue
            grp["since"] = clock[0]
            landed = grp["flight"].land(after)
            n = len(grp["names"])
            if grp["step"] == "swap":
                pairs = [_add_half(g4, r1, place) for g4, r1 in zip(landed[:n], landed[n:])]
                grp.update(step="scatter", flight=_scatter_flight(grp["tag"], [h for h, _ in pairs], [own for _, own in pairs], thru))
            elif grp["step"] == "scatter":
                grp.update(step="share", flight=_share_flight(grp["tag"], [_sum4_into_half(r2, place) for r2 in landed[n:]], thru))
            else:
                to_step.extend((name, grp["l"], fin.reshape(2 * fin.shape[1], fin.shape[2])) for name, fin in zip(grp["names"], landed))
                reducing.remove(grp)
                continue
            thru = grp["flight"].thru
        return thru

    def reduce_group(names, l, after, thru):
        thru = move_on(after, thru)
        g4s = [big_grads[n][l].reshape(N_CHIPS, 2, big_grads[n][l].shape[1] // 2, big_grads[n][l].shape[2]) for n in names]
        tag = f"{l}{GROUPS.index(names)}"
        reducing.append(dict(names=names, l=l, tag=tag, step="swap", since=clock[0], flight=_swap_flight(tag, g4s, thru)))
        return reducing[-1]["flight"].thru
    for l in reversed(range(L)):
        s = saved[l]
        lw, mp, md = s["lw"], s["mp"], s["md"]
        dx, dq, acc3 = ffn_bwd(
            GROUPS[2], l, dx, dq, s["x2"], s["h3"], s["a3"], s["gu3"], s["f3"], lw["gu2"], lw["d2"], ffn2_norm[l][None], md[7], md[5], 1.0)
        big_grads["w_out"][l] = _mm_tn_chunks(s["yn"], dq[None], 1024, 1024)[0].reshape(N_CHIPS, -1, D)
        dyn = _mm_nt_chunks(dq[None], lw["wout"][None])
        dylru, duv, dwsp, dbfull, gacc = _gmlp_bwd(s["proj"], s["ylru"], s["ygm"], dyn, mp["gv"], seg, mp["wsp"], mp["wspt"], mp["bfull"], mp["g_lru"], mp["g_gm"])
        dxg, dwa, dwx, lvec = _lru_bwd(s["proj"], dylru, mp["cw"], mp["cb"], mp["wa"], mp["ba"], mp["wx"], mp["bx"], mp["lam"], mp["wat"], mp["wxt"])
        dproj = jnp.concatenate([dxg, duv], axis=0)
        big_grads["w_in"][l] = _mm_tn_chunks(s["h2"], dproj, 1024, LW)
        dproj = reduce_group(GROUPS[1], l, big_grads["w_in"][l], dproj)
        dx, dq, acc2 = _mm_nt_norm_bwd(dproj, lw["win"], s["x1"], dx, s["f2"], mix_norm[l][None], md[4], 1.0, md[2], 0.5)
        dq = move_on(dx, dq)
        if l > 0:
            ng, ns = saved[l - 1]["md"][8], 0.5
        else:
            ng, ns = zero_row, 0.0
        dx, dq, acc1 = ffn_bwd(
            GROUPS[0], l, dx, dq, s["x0"], s["h1"], s["a1"], s["gu1"], s["f1"], lw["gu1"], lw["d1"], ffn1_norm[l][None], md[1], ng, ns)

        dmods[l] = jnp.concatenate([acc1[0:2], acc1[3:4], acc2[0:2], acc2[3:4], acc3[0:2], acc3[3:4]], axis=0)
        dws = jnp.stack([dwsp[:, :, :CHUNK], dwsp[:, :, CHUNK:]], axis=1).reshape(HEADS, CHUNK, CHUNK) * tril
        lg = {"ffn1_norm": acc1[2], "mix_norm": acc2[2], "ffn2_norm": acc3[2],
              "conv_w": lvec[4:8], "conv_b": lvec[3], "gate_a_w": _unpair_blocks(dwa), "gate_a_b": lvec[0].reshape(HEADS, hd),
              "gate_x_w": _unpair_blocks(dwx), "gate_x_b": lvec[1].reshape(HEADS, hd), "lru_lambda": lvec[2], "v_norm": gacc[2],
              "spatial_w": dws, "spatial_b": dbfull.reshape(CHUNK, HEADS, hd).sum(-1).T, "lru_out_norm": gacc[0], "gmlp_out_norm": gacc[1]}
        part = [lg[n] for n in per_layer] + [dmods[l]] + ([head_acc[0]] if l == L - 1 else [])
        packers[l] = _Packer([p.shape for p in part])
        packed = packers[l].pack(part)
        exchanges[l] = _exchange_flight(l, lax.dynamic_update_index_in_dim(jnp.zeros((N_DEV,) + packed.shape, F32), packed, dev, 0), dq)
        dq = exchanges[l].thru

    grad_x = dx.reshape(x.shape)

    done = step_reduced(dq)
    while reducing:
        dq = move_on(done, dq, force=True)
        done = step_reduced(dq)
    summed, dmod_rows = [], []
    for l in range(L):
        gathered, = exchanges[l].land(done)
        summed.append(packers[l].unpack(_sum_leading(gathered)))
        off = sum(packers[l].sizes[:len(per_layer)])
        dmod_rows.append(gathered.reshape(N_DEV, -1)[:, off:off + N_MOD * D])
    grads = {n: jnp.stack([summed[l][k] for l in range(L)]) for k, n in enumerate(per_layer)}
    grads["final_norm"] = summed[L - 1][len(per_layer) + 1]
    grads["b_ada"] = jnp.stack([summed[l][len(per_layer)].reshape(N_MOD * D) for l in range(L)])
    dmod_shard = lax.dynamic_slice_in_dim(jnp.stack(dmod_rows), chip * n_ada, n_ada, axis=2)
    stepped_ada = _ada_step(c_all.T, dmod_shard, w_ada, m_w_ada, v_w_ada)
    grads["w_ada"] = stepped_ada[0]
    grads["conv_w"] = lax.dynamic_slice_in_dim(grads["conv_w"], chip * cws, cws, axis=2)

    delta, new_m, new_v = {}, {}, {}
    for n in BIG:
        grads[n], delta[n], new_m[n], new_v[n] = stepped[n]
    delta["w_ada"], new_m["w_ada"], new_v["w_ada"] = stepped_ada[1:]
    def rows_of(a):
        return a.reshape(-1, a.shape[-1])

    stepped_small = _adamw_small(*[[rows_of(src[n].reshape(W[n].shape)) for n in SMALL]
                                   for src in (W, grads, {n: given["m_" + n] for n in SMALL}, {n: given["v_" + n] for n in SMALL})])
    for k, n in enumerate(SMALL):
        delta[n], new_m[n], new_v[n] = (stepped_small[i * len(SMALL) + k].reshape(W[n].shape) for i in range(3))
    grads = {n: grads[n].reshape(W[n].shape) for n in WEIGHTS}
    return (loss, grad_x, *[grads[n] for n in WEIGHTS], *[delta[n] for n in WEIGHTS], *[new_m[n] for n in WEIGHTS], *[new_v[n] for n in WEIGHTS])
```

```python
import math

import jax
import jax.numpy as jnp
from jax import lax
from jax.experimental import pallas as pl
from jax.experimental.pallas import tpu as pltpu

F32 = jnp.float32
MXU_DTYPE = jnp.bfloat16
ACT_DTYPE = jnp.bfloat16
XFER_DTYPE = jnp.bfloat16
EPS = 1e-6
RG_LRU_C = 8.0
N_MOD = 9
CONV_WIDTH = 4
HEADS = 8
CHUNK = 128
LANES = 128
SUBLANES = 8
N_CHIPS = 4
N_DEV = 8
ADAM_LR, ADAM_B1, ADAM_B2, ADAM_EPS, ADAM_WD, ADAM_STEP = 0.001, 0.9, 0.999, 1e-08, 0.01, 10
VMEM_LIMIT_BYTES = 60 * 1024 * 1024
ROW_TILE_BYTES = 2 << 20
GELU_C = math.sqrt(2.0 / math.pi)
GELU_A = 0.044715

ANY = pl.BlockSpec(memory_space=pl.ANY)
MESH = pl.DeviceIdType.MESH
SDS = jax.ShapeDtypeStruct


def _params(*sem):
    return pltpu.CompilerParams(dimension_semantics=sem, vmem_limit_bytes=VMEM_LIMIT_BYTES)


def _dot(a, b):
    return jnp.dot(a.astype(MXU_DTYPE), b.astype(MXU_DTYPE), preferred_element_type=F32)


def _dot_nt(a, b):
    return lax.dot_general(a.astype(MXU_DTYPE), b.astype(MXU_DTYPE), (((1,), (1,)), ((), ())), preferred_element_type=F32)


def _dot_tn(a, b):
    return lax.dot_general(a.astype(MXU_DTYPE), b.astype(MXU_DTYPE), (((0,), (0,)), ((), ())), preferred_element_type=F32)


def _gelu(x):
    return x * (0.5 * (1.0 + jnp.tanh(GELU_C * (x + GELU_A * (x * x * x)))))


def _gelu_grad(x):
    t = jnp.tanh(GELU_C * (x + GELU_A * (x * x * x)))
    return 0.5 * (1.0 + t) + 0.5 * x * (1.0 - t * t) * (GELU_C * (1.0 + 3.0 * GELU_A * x * x))


def _sigmoid(x):
    return jax.nn.sigmoid(x)


def _sigmoid_by_tanh(x):
    return 0.5 * jnp.tanh(0.5 * x) + 0.5


def _rsqrt_ms(x):
    return lax.rsqrt(jnp.mean(x * x, axis=-1, keepdims=True) + EPS)


def _rowsum(x):
    return jnp.sum(x, axis=0, keepdims=True)


def _tile(n, want):
    t = min(n, want)
    assert n % t == 0, (n, want)
    return t


def _row_tile(rows, row_bytes):
    step = 2 * SUBLANES
    cap = max(step, ROW_TILE_BYTES // row_bytes)
    best = None
    for t in range(step, min(rows, cap) + 1, step):
        if rows % t == 0:
            best = t
    assert best is not None, (rows, row_bytes)
    return best


def _modnorm(x, gn, sh, sc):
    S, D = x.shape
    tm = _tile(S, 1024)

    def body(x_ref, gn_ref, sh_ref, sc_ref, h_ref):
        xv = x_ref[...]
        h = (xv * _rsqrt_ms(xv) * gn_ref[...]) * (1.0 + sc_ref[...]) + sh_ref[...]
        h_ref[...] = h.astype(ACT_DTYPE)

    row = pl.BlockSpec((1, D), lambda i: (0, 0))
    return pl.pallas_call(
        body, name="modnorm", grid=(S // tm,),
        in_specs=[pl.BlockSpec((tm, D), lambda i: (i, 0)), row, row, row],
        out_specs=pl.BlockSpec((tm, D), lambda i: (i, 0)),
        out_shape=SDS((S, D), ACT_DTYPE), compiler_params=_params("parallel"),
    )(x, gn, sh, sc)


def _mm_nt_norm_bwd(ac, wc, x, dxo, f, gn, sc, res_scale, next_gate, next_scale):
    P, S, K = ac.shape
    D = x.shape[1]
    tm = _tile(S, 512)

    def body(a_ref, w_ref, x_ref, dxo_ref, f_ref, gn_ref, sc_ref, ng_ref, dx_ref, dq_ref, acc_ref):
        @pl.when(pl.program_id(0) == 0)
        def _():
            acc_ref[...] = jnp.zeros_like(acc_ref)

        dh = _dot_nt(a_ref[0], w_ref[0])
        for p in range(1, P):
            dh += _dot_nt(a_ref[p], w_ref[p])
        xv, dxo = x_ref[...], dxo_ref[...]
        r = _rsqrt_ms(xv)
        xhat = xv * r
        gn = gn_ref[...]
        dn = dh * (1.0 + sc_ref[...])
        dxh = dn * gn
        dx = dxo + r * (dxh - xhat * jnp.mean(dxh * xhat, axis=-1, keepdims=True))
        dx_ref[...] = dx
        dq_ref[...] = ((next_scale * ng_ref[...]) * dx).astype(ACT_DTYPE)
        acc_ref[0:1, :] += _rowsum(dh)
        acc_ref[1:2, :] += _rowsum(dh * (xhat * gn))
        acc_ref[2:3, :] += _rowsum(dn * xhat)
        acc_ref[3:4, :] += _rowsum((res_scale * f_ref[...]) * dxo)

    tile = pl.BlockSpec((tm, D), lambda i: (i, 0))
    row = pl.BlockSpec((1, D), lambda i: (0, 0))
    return pl.pallas_call(
        body, name=f"mm_nt_norm_bwd_k{K}", grid=(S // tm,),
        in_specs=[pl.BlockSpec((P, tm, K), lambda i: (0, i, 0)),
                  pl.BlockSpec((P, D, K), lambda i: (0, 0, 0), pipeline_mode=pl.Buffered(1)),
                  tile, tile, tile, row, row, row],
        out_specs=[tile, tile, pl.BlockSpec((SUBLANES, D), lambda i: (0, 0))],
        out_shape=[SDS((S, D), F32), SDS((S, D), ACT_DTYPE), SDS((SUBLANES, D), F32)],
        compiler_params=_params("arbitrary"),
    )(ac, wc, x, dxo, f, gn, sc, next_gate)


def _loss_head(x, target, gn, next_gate, next_scale):
    S, D = x.shape
    tm = _tile(S, 512)

    def body(x_ref, t_ref, gn_ref, ng_ref, dx_ref, dq_ref, acc_ref):
        @pl.when(pl.program_id(0) == 0)
        def _():
            acc_ref[...] = jnp.zeros_like(acc_ref)

        xv = x_ref[...]
        r = _rsqrt_ms(xv)
        xhat = xv * r
        gn = gn_ref[...]
        err = xhat * gn - t_ref[...]
        dy = err * (1.0 / D)
        dxh = dy * gn
        dx = r * (dxh - xhat * jnp.mean(dxh * xhat, axis=-1, keepdims=True))
        dx_ref[...] = dx
        dq_ref[...] = ((next_scale * ng_ref[...]) * dx).astype(ACT_DTYPE)
        acc_ref[0:1, :] += _rowsum(dy * xhat)
        acc_ref[1:2, :] += _rowsum(err * err) * (0.5 / D)

    tile = pl.BlockSpec((tm, D), lambda i: (i, 0))
    row = pl.BlockSpec((1, D), lambda i: (0, 0))
    return pl.pallas_call(
        body, name="loss_head", grid=(S // tm,),
        in_specs=[tile, tile, row, row],
        out_specs=[tile, tile, pl.BlockSpec((SUBLANES, D), lambda i: (0, 0))],
        out_shape=[SDS((S, D), F32), SDS((S, D), ACT_DTYPE), SDS((SUBLANES, D), F32)],
        compiler_params=_params("arbitrary"),
    )(x, target, gn, next_gate)


def _ffn_up(h, wgu):
    S, D = h.shape
    C = wgu.shape[2]
    tm = _tile(S, 512)

    def body(h_ref, wg_ref, wu_ref, a_ref, gu_ref):
        hv = h_ref[...]
        g = _dot(hv, wg_ref[...])
        u = _dot(hv, wu_ref[...])
        a_ref[...] = (g * _sigmoid_by_tanh(g) * u).astype(ACT_DTYPE)
        gu_ref[0] = g.astype(ACT_DTYPE)
        gu_ref[1] = u.astype(ACT_DTYPE)

    return pl.pallas_call(
        body, name="ffn_up", grid=(2, S // tm),
        in_specs=[
            pl.BlockSpec((tm, D), lambda j, i: (i, 0)),
            pl.BlockSpec((None, D, C), lambda j, i: (j, 0, 0)),
            pl.BlockSpec((None, D, C), lambda j, i: (2 + j, 0, 0)),
        ],
        out_specs=[
            pl.BlockSpec((tm, C), lambda j, i: (i, j)),
            pl.BlockSpec((2, None, tm, C), lambda j, i: (0, j, i, 0)),
        ],
        out_shape=[SDS((S, 2 * C), ACT_DTYPE), SDS((2, 2, S, C), ACT_DTYPE)],
        compiler_params=_params("parallel", "parallel"),
    )(h, wgu, wgu)


def _ffn_bwd_act(dq, wd, gu):
    S, D = dq.shape
    C = gu.shape[3]
    tm = _tile(S, 512)

    def body(dq_ref, wd_ref, gu_ref, dgu_ref):
        da = _dot_nt(dq_ref[...], wd_ref[...])
        g = gu_ref[0].astype(F32)
        u = gu_ref[1].astype(F32)
        s = _sigmoid_by_tanh(g)
        gs = g * s
        dgu_ref[0] = ((da * u) * (s + gs - gs * s)).astype(ACT_DTYPE)
        dgu_ref[1] = (da * gs).astype(ACT_DTYPE)

    gu_spec = pl.BlockSpec((2, None, tm, C), lambda j, i: (0, j, i, 0))
    return pl.pallas_call(
        body, name="ffn_bwd_act", grid=(2, S // tm),
        in_specs=[pl.BlockSpec((tm, D), lambda j, i: (i, 0)), pl.BlockSpec((C, D), lambda j, i: (j, 0)), gu_spec],
        out_specs=gu_spec,
        out_shape=SDS(gu.shape, ACT_DTYPE),
        compiler_params=_params("parallel", "parallel"),
    )(dq, wd, gu)


def _mm_res(a, w, x, gate, scale, following):
    S, K = a.shape
    D = w.shape[1]
    tm = _tile(S, 512)

    def body(a_ref, w_ref, x_ref, g_ref, gn_ref, sh_ref, sc_ref, f_ref, xo_ref, h_ref):
        f = _dot(a_ref[...], w_ref[...])
        f_ref[...] = f
        xo = x_ref[...] + (scale * g_ref[...]) * f
        xo_ref[...] = xo
        h_ref[...] = ((xo * _rsqrt_ms(xo) * gn_ref[...]) * (1.0 + sc_ref[...]) + sh_ref[...]).astype(ACT_DTYPE)

    tile = pl.BlockSpec((tm, D), lambda i: (i, 0))
    row = pl.BlockSpec((1, D), lambda i: (0, 0))
    return pl.pallas_call(
        body, name=f"mm_res_k{K}", grid=(S // tm,),
        in_specs=[pl.BlockSpec((tm, K), lambda i: (i, 0)), pl.BlockSpec((K, D), lambda i: (0, 0)), tile, row, row, row, row],
        out_specs=[tile, tile, tile],
        out_shape=[SDS((S, D), F32), SDS((S, D), F32), SDS((S, D), ACT_DTYPE)],
        compiler_params=_params("parallel"),
    )(a, w, x, gate, *following)


def _mm_chunks(h, wc):
    S, K = h.shape
    P, _, N = wc.shape
    tm = _tile(S, 512)

    def body(h_ref, w_ref, o_ref):
        hv = h_ref[...]
        for p in range(P):
            o_ref[:, p * N:(p + 1) * N] = _dot(hv, w_ref[p])

    return pl.pallas_call(
        body, name="mm_chunks", grid=(S // tm,),
        in_specs=[pl.BlockSpec((tm, K), lambda i: (i, 0)), pl.BlockSpec((P, K, N), lambda i: (0, 0, 0))],
        out_specs=pl.BlockSpec((tm, P * N), lambda i: (i, 0)),
        out_shape=SDS((S, P * N), F32),
        compiler_params=_params("parallel"),
    )(h, wc)


def _mm_nt_chunks(ac, wc):
    P, S, K = ac.shape
    N = wc.shape[1]
    tm, tn = _tile(S, 512), _tile(N, 1024)

    def body(a_ref, w_ref, o_ref):
        acc = _dot_nt(a_ref[0], w_ref[0])
        for p in range(1, P):
            acc += _dot_nt(a_ref[p], w_ref[p])
        o_ref[...] = acc

    return pl.pallas_call(
        body, name=f"mm_nt_p{P}k{K}", grid=(S // tm, N // tn),
        in_specs=[pl.BlockSpec((P, tm, K), lambda i, j: (0, i, 0)), pl.BlockSpec((P, tn, K), lambda i, j: (0, j, 0))],
        out_specs=pl.BlockSpec((tm, tn), lambda i, j: (i, j)),
        out_shape=SDS((S, N), F32),
        compiler_params=_params("parallel", "parallel"),
    )(ac, wc)


def _mm_tn_chunks(a, bc, tile_m, tile_n):
    S, M = a.shape
    P, _, N = bc.shape
    ts, tm, tn = _tile(S, 2048), _tile(M, tile_m), _tile(N, tile_n)

    def body(a_ref, b_ref, o_ref):
        @pl.when(pl.program_id(3) == 0)
        def _():
            o_ref[...] = jnp.zeros_like(o_ref)

        o_ref[...] += _dot_tn(a_ref[...], b_ref[...])

    return pl.pallas_call(
        body, name=f"mm_tn_m{M}n{N}", grid=(P, M // tm, N // tn, S // ts),
        in_specs=[pl.BlockSpec((ts, tm), lambda p, m, n, k: (k, m)), pl.BlockSpec((None, ts, tn), lambda p, m, n, k: (p, k, n))],
        out_specs=pl.BlockSpec((None, tm, tn), lambda p, m, n, k: (p, m, n)),
        out_shape=SDS((P, M, N), F32),
        compiler_params=_params("parallel", "parallel", "parallel", "arbitrary"),
    )(a, bc)


def _mm_tn_together(a, bc):
    S, M = a.shape
    P, _, N = bc.shape
    ts = _tile(S, 2048)

    def body(a_ref, b_ref, o_ref):
        @pl.when(pl.program_id(0) == 0)
        def _():
            o_ref[...] = jnp.zeros_like(o_ref)

        at = a_ref[...].T
        for p in range(P):
            o_ref[p] += jnp.dot(at, b_ref[p], preferred_element_type=F32)

    return pl.pallas_call(
        body, name=f"mm_tn_together_m{M}n{N}", grid=(S // ts,),
        in_specs=[pl.BlockSpec((ts, M), lambda k: (k, 0)), pl.BlockSpec((P, ts, N), lambda k: (0, k, 0))],
        out_specs=pl.BlockSpec((P, M, N), lambda k: (0, 0, 0)),
        out_shape=SDS((P, M, N), F32), compiler_params=_params("arbitrary"),
    )(a, bc)


def _shift_down(x, s, row, fill):
    return jnp.where(row >= s, pltpu.roll(x, s, 0), fill)


def _shift_up(x, s, row, fill):
    n = x.shape[0]
    return jnp.where(row < n - s, pltpu.roll(x, n - s, 0), fill)


def _scan(a, b, row, scratch, up):
    scr_a, scr_b, scr_c = scratch
    n = a.shape[0]
    g = n // SUBLANES
    in_group = row & (SUBLANES - 1)

    def steps(a, b, pos, size):
        s = 1
        while s < size:
            m = (pos + s < size) if up else (pos >= s)
            b = jnp.where(m, a, 0.0) * pltpu.roll(b, a.shape[0] - s if up else s, 0) + b
            a = jnp.where(m, a * pltpu.roll(a, a.shape[0] - s if up else s, 0), a)
            s *= 2
        return a, b

    a, b = steps(a, b, in_group, SUBLANES)
    scr_a[...] = a
    scr_b[...] = b
    edge = 0 if up else SUBLANES - 1
    at = scr_a[pl.ds(edge, g, stride=SUBLANES), :]
    bt = scr_b[pl.ds(edge, g, stride=SUBLANES), :]
    group = lax.broadcasted_iota(jnp.int32, at.shape, 0)
    _, state = steps(at, bt, group, g)
    carry = jnp.where((group + 1 < g) if up else (group >= 1), pltpu.roll(state, g - 1 if up else 1, 0), 0.0)
    for k in range(SUBLANES):
        scr_c[pl.ds(k, g, stride=SUBLANES), :] = carry
    return b + a * scr_c[...]


def _conv(xl, cw_ref, cb_ref, row):
    y = cb_ref[...] + _shift_down(xl, 3, row, 0.0) * cw_ref[0:1, :]
    y = y + _shift_down(xl, 2, row, 0.0) * cw_ref[1:2, :]
    y = y + _shift_down(xl, 1, row, 0.0) * cw_ref[2:3, :]
    return y + xl * cw_ref[3:4, :]


def _lru_gates(xc, wa_ref, ba_ref, wx_ref, bx_ref, lam_ref):
    ra = _sigmoid(_dot(xc, wa_ref[...]) + ba_ref[...])
    ri = _sigmoid(_dot(xc, wx_ref[...]) + bx_ref[...])
    ls = jax.nn.log_sigmoid(lam_ref[...])
    a = jnp.exp((RG_LRU_C * ra) * ls)
    mult = jnp.sqrt(1.0 - a * a)
    return ra, ri, ls, a, mult


def _lru_specs(S):
    col = lambda off: pl.BlockSpec((S, LANES), lambda j: (0, off + j))
    vec = pl.BlockSpec((1, LANES), lambda j: (0, j))
    blk = pl.BlockSpec((None, LANES, LANES), lambda j: (j, 0, 0))
    cw = pl.BlockSpec((CONV_WIDTH, LANES), lambda j: (0, j))
    return col, vec, blk, cw


def _lru_fwd(proj, cw, cb, wa, ba, wx, bx, lam):
    S = proj.shape[0]
    W = cb.shape[1]
    nb = W // LANES

    def body(xl_ref, gl_ref, cw_ref, cb_ref, wa_ref, ba_ref, wx_ref, bx_ref, lam_ref, y_ref, *scratch):
        row = lax.broadcasted_iota(jnp.int32, (S, LANES), 0)
        xc = _conv(xl_ref[...], cw_ref, cb_ref, row)
        _, ri, _, a, mult = _lru_gates(xc, wa_ref, ba_ref, wx_ref, bx_ref, lam_ref)
        h = _scan(a, mult * (ri * xc), row, scratch, up=False)
        y_ref[...] = h * _gelu(gl_ref[...])

    col, vec, blk, cws = _lru_specs(S)
    return pl.pallas_call(
        body, name="lru_fwd", grid=(nb,),
        in_specs=[col(0), col(nb), cws, vec, blk, vec, blk, vec, vec],
        out_specs=pl.BlockSpec((S, LANES), lambda j: (0, j)),
        out_shape=SDS((S, W), F32), scratch_shapes=[pltpu.VMEM((S, LANES), F32)] * 3, compiler_params=_params("parallel"),
    )(proj, proj, cw, cb, wa, ba, wx, bx, lam)


def _lru_bwd(proj, dy, cw, cb, wa, ba, wx, bx, lam, wat, wxt):
    S = proj.shape[0]
    W = cb.shape[1]
    nb = W // LANES

    def body(xl_ref, gl_ref, dy_ref, cw_ref, cb_ref, wa_ref, ba_ref, wx_ref, bx_ref, lam_ref, wat_ref, wxt_ref,
             dp_ref, dwa_ref, dwx_ref, vec_ref, *scratch):
        row = lax.broadcasted_iota(jnp.int32, (S, LANES), 0)
        xl = xl_ref[...]
        xc = _conv(xl, cw_ref, cb_ref, row)
        ra, ri, ls, a, mult = _lru_gates(xc, wa_ref, ba_ref, wx_ref, bx_ref, lam_ref)
        h = _scan(a, mult * (ri * xc), row, scratch, up=False)
        gl = gl_ref[...]
        dyv = dy_ref[...]
        dp_ref[1] = (dyv * h * _gelu_grad(gl)).astype(ACT_DTYPE)
        adj = _scan(_shift_up(a, 1, row, 0.0), dyv * _gelu(gl), row, scratch, up=True)
        da = adj * _shift_down(h, 1, row, 0.0)
        dmult = adj * (ri * xc)
        dlog_a = da * a - dmult * (a * a) / mult
        dra = dlog_a * (RG_LRU_C * ls)
        dpa = dra * ra * (1.0 - ra)
        dpi = (adj * mult * xc) * ri * (1.0 - ri)
        dxc = adj * mult * ri + _dot(dpa, wat_ref[...]) + _dot(dpi, wxt_ref[...])
        dwa_ref[...] = _dot_tn(xc, dpa)
        dwx_ref[...] = _dot_tn(xc, dpi)
        dxl = dxc * cw_ref[3:4, :]
        dxl = dxl + _shift_up(dxc, 1, row, 0.0) * cw_ref[2:3, :]
        dxl = dxl + _shift_up(dxc, 2, row, 0.0) * cw_ref[1:2, :]
        dxl = dxl + _shift_up(dxc, 3, row, 0.0) * cw_ref[0:1, :]
        dp_ref[0] = dxl.astype(ACT_DTYPE)
        vec_ref[...] = jnp.zeros_like(vec_ref)
        vec_ref[0:1, :] = _rowsum(dpa)
        vec_ref[1:2, :] = _rowsum(dpi)
        vec_ref[2:3, :] = _rowsum(dlog_a * (RG_LRU_C * ra)) * _sigmoid(-lam_ref[...])
        vec_ref[3:4, :] = _rowsum(dxc)
        vec_ref[4:5, :] = _rowsum(dxc * _shift_down(xl, 3, row, 0.0))
        vec_ref[5:6, :] = _rowsum(dxc * _shift_down(xl, 2, row, 0.0))
        vec_ref[6:7, :] = _rowsum(dxc * _shift_down(xl, 1, row, 0.0))
        vec_ref[7:8, :] = _rowsum(dxc * xl)

    col, vec, blk, cws = _lru_specs(S)
    return pl.pallas_call(
        body, name="lru_bwd", grid=(nb,),
        in_specs=[col(0), col(nb), col(0), cws, vec, blk, vec, blk, vec, vec, blk, blk],
        out_specs=[pl.BlockSpec((2, S, LANES), lambda j: (0, 0, j)), blk, blk, pl.BlockSpec((2 * SUBLANES, LANES), lambda j: (0, j))],
        out_shape=[SDS((2, S, W), ACT_DTYPE), SDS((nb, LANES, LANES), F32), SDS((nb, LANES, LANES), F32), SDS((2 * SUBLANES, W), F32)],
        scratch_shapes=[pltpu.VMEM((S, LANES), F32)] * 3, compiler_params=_params("parallel"),
    )(proj, proj, dy, cw, cb, wa, ba, wx, bx, lam, wat, wxt)


def _seg_mean(x, seg_ref, width):
    hi = x.astype(jnp.bfloat16)
    lo = (x - hi.astype(F32)).astype(jnp.bfloat16)
    ones = seg_ref[...]
    s = jnp.dot(hi, ones, preferred_element_type=F32) + jnp.dot(lo, ones, preferred_element_type=F32)
    return s * (1.0 / width)


def _gmlp_core(u_ref, v_ref, gv_ref, seg_ref, ws_ref, bfull_ref, z_scr, hd):
    tm, W = u_ref.shape
    lane = lax.broadcasted_iota(jnp.int32, (CHUNK, LANES), 1)
    ug = _gelu(u_ref[...])
    vg = _gelu(v_ref[...])
    cen = vg - _seg_mean(vg, seg_ref, hd)
    rstd = lax.rsqrt(_seg_mean(cen * cen, seg_ref, hd) + EPS)
    vhat = cen * rstd
    vh = vhat * gv_ref[...]
    vcats = {}
    for ci in range(tm // CHUNK):
        for p in range(W // LANES):
            blk = vh[ci * CHUNK:(ci + 1) * CHUNK, p * LANES:(p + 1) * LANES]
            vcat = jnp.concatenate([jnp.where(lane < hd, blk, 0.0), jnp.where(lane >= hd, blk, 0.0)], axis=0).astype(MXU_DTYPE)
            vcats[ci, p] = vcat
            z_scr[ci * CHUNK:(ci + 1) * CHUNK, p * LANES:(p + 1) * LANES] = (
                jnp.dot(ws_ref[p], vcat, preferred_element_type=F32) + bfull_ref[:, p * LANES:(p + 1) * LANES])
    return ug, vhat, rstd, vcats


def _gmlp_specs(tm, W, nb):
    rows = lambda off: pl.BlockSpec((tm, W), lambda i: (i, off))
    vec = pl.BlockSpec((1, W), lambda i: (0, 0))
    seg = pl.BlockSpec((W, W), lambda i: (0, 0))
    wsp = pl.BlockSpec((nb, CHUNK, 2 * CHUNK), lambda i: (0, 0, 0))
    bfull = pl.BlockSpec((CHUNK, W), lambda i: (0, 0))
    return rows, vec, seg, wsp, bfull


def _gmlp_fwd(proj, ylru, gv, seg, wsp, bfull, g_lru, g_gm):
    S, W = ylru.shape
    nb = W // LANES
    hd = W // HEADS
    tm = _tile(S, 512)

    def body(u_ref, v_ref, yl_ref, gv_ref, seg_ref, ws_ref, bfull_ref, gl_ref, gg_ref, yn_ref, ygm_ref, z_scr):
        ug, _, _, _ = _gmlp_core(u_ref, v_ref, gv_ref, seg_ref, ws_ref, bfull_ref, z_scr, hd)
        ygm = ug * z_scr[...]
        ygm_ref[...] = ygm
        yl = yl_ref[...]
        yn_ref[:, 0:W] = (yl * _rsqrt_ms(yl) * gl_ref[...]).astype(ACT_DTYPE)
        yn_ref[:, W:2 * W] = (ygm * _rsqrt_ms(ygm) * gg_ref[...]).astype(ACT_DTYPE)

    rows, vec, segs, wsps, bfulls = _gmlp_specs(tm, W, nb)
    return pl.pallas_call(
        body, name="gmlp_fwd", grid=(S // tm,),
        in_specs=[rows(2), rows(3), rows(0), vec, segs, wsps, bfulls, vec, vec],
        out_specs=[pl.BlockSpec((tm, 2 * W), lambda i: (i, 0)), rows(0)],
        out_shape=[SDS((S, 2 * W), ACT_DTYPE), SDS((S, W), F32)],
        scratch_shapes=[pltpu.VMEM((tm, W), F32)],
        compiler_params=_params("parallel"),
    )(proj, proj, ylru, gv, seg, wsp, bfull, g_lru, g_gm)


def _rms_bwd(y, g, dyn):
    r = _rsqrt_ms(y)
    yhat = y * r
    dyh = dyn * g
    return r * (dyh - yhat * jnp.mean(dyh * yhat, axis=-1, keepdims=True)), _rowsum(dyn * yhat)


def _gmlp_bwd(proj, ylru, ygm, dyn, gv, seg, wsp, wspt, bfull, g_lru, g_gm):
    S, W = ylru.shape
    nb = W // LANES
    hd = W // HEADS
    tm = _tile(S, 256)

    def body(u_ref, v_ref, yl_ref, ygm_ref, dl_ref, dg_ref, gv_ref, seg_ref, ws_ref, wst_ref, bfull_ref, gl_ref, gg_ref,
             dyl_ref, duv_ref, dws_ref, dbf_ref, acc_ref, z_scr, dvh_scr):
        @pl.when(pl.program_id(0) == 0)
        def _():
            dws_ref[...] = jnp.zeros_like(dws_ref)
            dbf_ref[...] = jnp.zeros_like(dbf_ref)
            acc_ref[...] = jnp.zeros_like(acc_ref)

        dyl, dgl = _rms_bwd(yl_ref[...], gl_ref[...], dl_ref[...])
        dyl_ref[...] = dyl
        dygm, dgg = _rms_bwd(ygm_ref[...], gg_ref[...], dg_ref[...])
        ug, vhat, rstd, vcats = _gmlp_core(u_ref, v_ref, gv_ref, seg_ref, ws_ref, bfull_ref, z_scr, hd)
        duv_ref[0] = (dygm * z_scr[...] * _gelu_grad(u_ref[...])).astype(ACT_DTYPE)
        dz = dygm * ug
        lane = lax.broadcasted_iota(jnp.int32, (CHUNK, LANES), 1)
        dbf = dz[0:CHUNK, :]
        for ci in range(1, tm // CHUNK):
            dbf += dz[ci * CHUNK:(ci + 1) * CHUNK, :]
        dbf_ref[...] += dbf
        for ci in range(tm // CHUNK):
            for p in range(nb):
                dzb = dz[ci * CHUNK:(ci + 1) * CHUNK, p * LANES:(p + 1) * LANES].astype(MXU_DTYPE)
                dws_ref[p] += _dot_nt(dzb, vcats[ci, p])
                dvc = jnp.dot(wst_ref[p], dzb, preferred_element_type=F32)
                dvh_scr[ci * CHUNK:(ci + 1) * CHUNK, p * LANES:(p + 1) * LANES] = jnp.where(lane < hd, dvc[0:CHUNK], dvc[CHUNK:2 * CHUNK])
        dvh = dvh_scr[...]
        dvn = dvh * gv_ref[...]
        dvg = rstd * (dvn - _seg_mean(dvn, seg_ref, hd) - vhat * _seg_mean(dvn * vhat, seg_ref, hd))
        duv_ref[1] = (dvg * _gelu_grad(v_ref[...])).astype(ACT_DTYPE)
        acc_ref[0:1, :] += dgl
        acc_ref[1:2, :] += dgg
        acc_ref[2:3, :] += _rowsum(dvh * vhat)

    rows, vec, segs, wsps, bfulls = _gmlp_specs(tm, W, nb)
    wspt_spec = pl.BlockSpec((nb, 2 * CHUNK, CHUNK), lambda i: (0, 0, 0))
    return pl.pallas_call(
        body, name="gmlp_bwd", grid=(S // tm,),
        in_specs=[rows(2), rows(3), rows(0), rows(0), rows(0), rows(1), vec, segs, wsps, wspt_spec, bfulls, vec, vec],
        out_specs=[rows(0), pl.BlockSpec((2, tm, W), lambda i: (0, i, 0)), wsps, bfulls, pl.BlockSpec((SUBLANES, W), lambda i: (0, 0))],
        out_shape=[SDS((S, W), F32), SDS((2, S, W), ACT_DTYPE), SDS((nb, CHUNK, 2 * CHUNK), F32), SDS((CHUNK, W), F32), SDS((SUBLANES, W), F32)],
        scratch_shapes=[pltpu.VMEM((tm, W), F32), pltpu.VMEM((tm, W), F32)],
        compiler_params=_params("arbitrary"),
    )(proj, proj, ylru, ygm, dyn, dyn, gv, seg, wsp, wspt, bfull, g_lru, g_gm)


def _ada_fwd(c_all, w_ada, b_shard):
    L, D, N = w_ada.shape
    R = c_all.shape[0]
    tn = N // 2

    def body(c_ref, w_ref, b_ref, o_ref):
        cv = c_ref[...]
        o_ref[...] = _dot(cv * _sigmoid(cv), w_ref[...]) + b_ref[...]

    return pl.pallas_call(
        body, name="ada_fwd", grid=(L, N // tn),
        in_specs=[pl.BlockSpec((R, D), lambda l, j: (0, 0)), pl.BlockSpec((None, D, tn), lambda l, j: (l, 0, j)),
                  pl.BlockSpec((None, 1, tn), lambda l, j: (l, 0, j))],
        out_specs=pl.BlockSpec((None, R, tn), lambda l, j: (l, 0, j)),
        out_shape=SDS((L, R, N), F32), compiler_params=_params("parallel", "parallel"),
    )(c_all, w_ada, b_shard)


def _ada_step(c_all_t, dmod, w, m, v):
    D, B = c_all_t.shape
    L, _, N = dmod.shape
    tn = 3 * LANES
    assert N % tn == 0, N

    def body(c_ref, d_ref, w_ref, m_ref, v_ref, g_ref, do_ref, mo_ref, vo_ref):
        cv = c_ref[...]
        sc = cv * _sigmoid(cv)
        g = sc[:, 0:1] * d_ref[0:1, :]
        for b in range(1, B):
            g += sc[:, b:b + 1] * d_ref[b:b + 1, :]
        g_ref[...] = g
        do_ref[...], mo_ref[...], vo_ref[...] = _adam_math(w_ref[...], g, m_ref[...], v_ref[...])

    tile = pl.BlockSpec((None, D, tn), lambda l, j: (l, 0, j))
    return pl.pallas_call(
        body, name="ada_step", grid=(L, N // tn),
        in_specs=[pl.BlockSpec((D, B), lambda l, j: (0, 0)), pl.BlockSpec((None, B, tn), lambda l, j: (l, 0, j)), tile, tile, tile],
        out_specs=[tile] * 4, out_shape=[SDS((L, D, N), F32)] * 4, compiler_params=_params("parallel", "parallel"),
    )(c_all_t, dmod, w, m, v)


def _adam_math(w, g, m, v):
    mn = ADAM_B1 * m + (1.0 - ADAM_B1) * g
    vn = ADAM_B2 * v + (1.0 - ADAM_B2) * (g * g)
    m_hat = mn / (1.0 - ADAM_B1 ** ADAM_STEP)
    v_hat = vn / (1.0 - ADAM_B2 ** ADAM_STEP)
    return -ADAM_LR * (m_hat / (jnp.sqrt(v_hat) + ADAM_EPS) + ADAM_WD * w), mn, vn


def _adamw_layer(w, g, m, v, l, prev, after):
    L, R, C = w.shape
    tr = _row_tile(R, C * 4)
    prev = (after,) + (() if prev is None else tuple(prev))

    def body(w_ref, g_ref, m_ref, v_ref, *rest):
        go_ref, d_ref, mo_ref, vo_ref = rest[len(prev):]
        gv = g_ref[...]
        go_ref[...] = gv
        d_ref[...], mo_ref[...], vo_ref[...] = _adam_math(w_ref[...], gv, m_ref[...], v_ref[...])

    lay = pl.BlockSpec((None, tr, C), lambda i: (l, i, 0))
    return pl.pallas_call(
        body, name=f"adamw_layer_r{R}c{C}", grid=(R // tr,),
        in_specs=[lay, pl.BlockSpec((tr, C), lambda i: (i, 0)), lay, lay] + [ANY] * len(prev), out_specs=[lay] * 4,
        out_shape=[SDS((L, R, C), F32)] * 4, input_output_aliases={5 + k: k for k in range(len(prev) - 1)},
        compiler_params=_params("parallel"),
    )(w, g, m, v, *prev)


def _adamw_small(ws, gs, ms, vs):
    n = len(ws)

    def body(*refs):
        outs = refs[4 * n:]
        for k in range(n):
            outs[k][...], outs[n + k][...], outs[2 * n + k][...] = _adam_math(
                refs[k][...], refs[n + k][...], refs[2 * n + k][...], refs[3 * n + k][...])

    whole = pl.BlockSpec(memory_space=pltpu.VMEM)
    return pl.pallas_call(
        body, name="adamw_small", in_specs=[whole] * (4 * n), out_specs=[whole] * (3 * n),
        out_shape=[SDS(w.shape, F32) for w in ws] * 3, compiler_params=pltpu.CompilerParams(vmem_limit_bytes=VMEM_LIMIT_BYTES),
    )(*ws, *gs, *ms, *vs)


def _sum_leading(a):
    P, R, C = a.shape
    tr = _row_tile(R, P * C * 4)

    def body(a_ref, o_ref):
        acc = a_ref[0]
        for p in range(1, P):
            acc = acc + a_ref[p]
        o_ref[...] = acc

    return pl.pallas_call(
        body, name=f"sum{P}_r{R}c{C}", grid=(R // tr,),
        in_specs=[pl.BlockSpec((P, tr, C), lambda i: (0, i, 0))],
        out_specs=pl.BlockSpec((tr, C), lambda i: (i, 0)),
        out_shape=SDS((R, C), F32), compiler_params=_params("parallel"),
    )(a)


def _add_half(g4, r1, place):
    _, _, R, C = g4.shape
    tr = _row_tile(R, C * 4)

    def body(place_ref, g_ref, r_ref, h_ref, own_ref):
        s = (g_ref[...] + r_ref[...]).astype(XFER_DTYPE)
        h_ref[...] = s

        @pl.when(pl.program_id(1) == place_ref[1])
        def _():
            own_ref[...] = s

    return pl.pallas_call(
        body, name=f"add_half_r{R}c{C}",
        grid_spec=pltpu.PrefetchScalarGridSpec(
            num_scalar_prefetch=1, grid=(R // tr, N_CHIPS),
            in_specs=[pl.BlockSpec((None, None, tr, C), lambda i, p, place_ref: (p, place_ref[0], i, 0)),
                      pl.BlockSpec((None, tr, C), lambda i, p, place_ref: (p, i, 0))],
            out_specs=[pl.BlockSpec((None, tr, C), lambda i, p, place_ref: (p, i, 0)),
                       pl.BlockSpec((None, tr, C), lambda i, p, place_ref: (place_ref[1], i, 0))],
        ),
        out_shape=[SDS((N_CHIPS, R, C), XFER_DTYPE)] * 2, compiler_params=_params("parallel", "arbitrary"),
    )(place, g4, r1)


def _sum4_into_half(r2, place):
    P, R, C = r2.shape
    tr = _row_tile(R, P * C * 4)

    def body(place_ref, a_ref, o_ref):
        acc = a_ref[0].astype(F32)
        for p in range(1, P):
            acc = acc + a_ref[p].astype(F32)
        o_ref[...] = acc

    return pl.pallas_call(
        body, name=f"sum4_r{R}c{C}",
        grid_spec=pltpu.PrefetchScalarGridSpec(
            num_scalar_prefetch=1, grid=(R // tr,),
            in_specs=[pl.BlockSpec((P, tr, C), lambda i, place_ref: (0, i, 0))],
            out_specs=pl.BlockSpec((None, tr, C), lambda i, place_ref: (place_ref[0], i, 0)),
        ),
        out_shape=SDS((2, R, C), F32), compiler_params=_params("parallel"),
    )(place, r2)


def _cast_into_slot(w, l, place, after):
    _, R, C = w.shape
    tr = _row_tile(R, C * 4)

    def body(place_ref, w_ref, after_ref, o_ref):
        o_ref[...] = w_ref[...].astype(MXU_DTYPE)

    return pl.pallas_call(
        body, name=f"cast_r{R}c{C}",
        grid_spec=pltpu.PrefetchScalarGridSpec(
            num_scalar_prefetch=1, grid=(R // tr,),
            in_specs=[pl.BlockSpec((None, tr, C), lambda i, place_ref: (l, i, 0)), ANY],
            out_specs=pl.BlockSpec((None, tr, C), lambda i, place_ref: (place_ref[1], i, 0)),
        ),
        out_shape=SDS((N_CHIPS, R, C), MXU_DTYPE), compiler_params=_params("parallel"),
    )(place, w, after)


def _place():
    x, y, c = lax.axis_index("x"), lax.axis_index("y"), lax.axis_index("c")
    chips = [(1 - x, y), (x, 1 - y), (1 - x, 1 - y)]
    return x, y, c, chips


def _remote(src, dst, send_sem, recv_sem, to):
    return pltpu.make_async_remote_copy(src_ref=src, dst_ref=dst, send_sem=send_sem, recv_sem=recv_sem, device_id=to, device_id_type=MESH)


HBM_SPEC = pl.BlockSpec(memory_space=pltpu.HBM)
SEM_SPEC = pl.BlockSpec(memory_space=pltpu.SEMAPHORE)
DATAFLOW = pltpu.SideEffectType.DATAFLOW_SIDE_EFFECTING


def _in_hbm(a):
    return pltpu.with_memory_space_constraint(a, pltpu.HBM)


def _hbm_like(a):
    return pltpu.HBM(a.shape, a.dtype)


def _split_start(name, arrays, n_sems, issue, extra=()):
    m = len(arrays)

    def body(*refs):
        issue(refs[:m], refs[m + len(extra)], refs[m + len(extra) + 1])

    out = pl.pallas_call(
        body, name=name,
        out_shape=(pltpu.SemaphoreType.DMA((n_sems,)), pltpu.SemaphoreType.DMA((n_sems,)), *[_hbm_like(a) for a in arrays]),
        in_specs=[HBM_SPEC] * m + [ANY] * len(extra), out_specs=(SEM_SPEC, SEM_SPEC, *[HBM_SPEC] * m),
        input_output_aliases={k: 2 + k for k in range(m)},
        compiler_params=pltpu.CompilerParams(has_side_effects=DATAFLOW),
    )(*[_in_hbm(a) for a in arrays], *extra)
    return out[0], out[1], list(out[2:])


def _split_wait(name, send_sems, recv_sems, arrays, after, drain):
    m = len(arrays)

    def body(*refs):
        drain(refs[:m], refs[m], refs[m + 1])

    out = pl.pallas_call(
        body, name=name, out_shape=[_hbm_like(a) for a in arrays],
        in_specs=[HBM_SPEC] * m + [SEM_SPEC, SEM_SPEC, ANY], out_specs=[HBM_SPEC] * m,
        input_output_aliases={k: k for k in range(m)},
        compiler_params=pltpu.CompilerParams(has_side_effects=DATAFLOW),
    )(*arrays, send_sems, recv_sems, after)
    return list(out)


def _wait_both(cp):
    cp.wait_send()
    cp.wait_recv()


class _Flight:
    def __init__(self, name, arrays, n_sems, issue, drain, thru, extra=()):
        self.name, self.drain, self.n = name, drain, len(arrays)
        self.send, self.recv, out = _split_start(name + "_start", [*arrays, thru], n_sems, issue, extra)
        self.arrays, self.thru = out[:-1], out[-1]

    def land(self, after):
        return _split_wait(self.name + "_wait", self.send, self.recv, self.arrays, after, self.drain)


def _gather_flight(tag, ici, d2d, direct, thru):
    kinds = ["ici"] * len(ici) + ["d2d"] * len(d2d) + ["direct"] * len(direct)
    rows = [pl.ds(part * (a.shape[2] // parts), a.shape[2] // parts) for a, part, parts in ici]
    ici = [a for a, _, _ in ici]

    def issue(refs, send_sems, recv_sems):
        x, y, c, chips = _place()
        q = 2 * x + y
        for k, kind in enumerate(kinds):
            for j, chip in enumerate(chips):
                if kind == "ici":
                    src, to = refs[k].at[q, c, rows[k]], (*chip, c)
                elif kind == "d2d":
                    src, to = refs[k].at[2 * chip[0] + chip[1], c], (x, y, 1 - c)
                else:
                    src, to = refs[k].at[q], (*chip, c)
                _remote(src, src, send_sems.at[3 * k + j], recv_sems.at[3 * k + j], to).start()

    def drain(refs, send_sems, recv_sems):
        x, y, c, chips = _place()
        for k, kind in enumerate(kinds):
            for j, chip in enumerate(chips):
                p = 2 * chip[0] + chip[1]
                got = refs[k].at[p] if kind == "direct" else refs[k].at[p, c, rows[k]] if kind == "ici" else refs[k].at[p, 1 - c]
                _wait_both(_remote(got, got, send_sems.at[3 * k + j], recv_sems.at[3 * k + j], (x, y, c)))

    return _Flight(f"gather{tag}", [*ici, *d2d, *direct], 3 * len(kinds), issue, drain, thru)


def _swap_flight(tag, g4s, thru):
    n = len(g4s)
    zones = [lax.empty((N_CHIPS,) + g.shape[2:], g.dtype) for g in g4s]

    def issue(refs, send_sems, recv_sems):
        x, y, c, _ = _place()
        for k in range(n):
            for p in range(N_CHIPS):
                _remote(refs[k].at[p, 1 - c], refs[n + k].at[p], send_sems.at[N_CHIPS * k + p], recv_sems.at[N_CHIPS * k + p], (x, y, 1 - c)).start()

    def drain(refs, send_sems, recv_sems):
        x, y, c, _ = _place()
        for k in range(n):
            for p in range(N_CHIPS):
                got = refs[n + k].at[p]
                _wait_both(_remote(got, got, send_sems.at[N_CHIPS * k + p], recv_sems.at[N_CHIPS * k + p], (x, y, c)))

    return _Flight(f"swap{tag}", [*g4s, *zones], N_CHIPS * n, issue, drain, thru)


def _scatter_flight(tag, hs, lands, thru):
    n = len(hs)

    def issue(refs, send_sems, recv_sems):
        x, y, c, chips = _place()
        q = 2 * x + y
        for k in range(n):
            for j, chip in enumerate(chips):
                _remote(refs[k].at[2 * chip[0] + chip[1]], refs[n + k].at[q], send_sems.at[3 * k + j], recv_sems.at[3 * k + j], (*chip, c)).start()

    def drain(refs, send_sems, recv_sems):
        x, y, c, chips = _place()
        for k in range(n):
            for j, chip in enumerate(chips):
                got = refs[n + k].at[2 * chip[0] + chip[1]]
                _wait_both(_remote(got, got, send_sems.at[3 * k + j], recv_sems.at[3 * k + j], (x, y, c)))

    return _Flight(f"scatter{tag}", [*hs, *lands], 3 * n, issue, drain, thru)


def _exchange_flight(tag, buf, thru):
    flips = [(fx, fy, fc) for fx in (0, 1) for fy in (0, 1) for fc in (0, 1)][1:]

    def peers():
        x, y, c, _ = _place()
        return (x, y, c), [((1 - x) if fx else x, (1 - y) if fy else y, (1 - c) if fc else c) for fx, fy, fc in flips]

    def slot(ref, dev):
        return ref.at[4 * dev[0] + 2 * dev[1] + dev[2]]

    def issue(refs, send_sems, recv_sems):
        me, others = peers()
        for j, to in enumerate(others):
            _remote(slot(refs[0], me), slot(refs[0], me), send_sems.at[j], recv_sems.at[j], to).start()

    def drain(refs, send_sems, recv_sems):
        me, others = peers()
        for j, frm in enumerate(others):
            got = slot(refs[0], frm)
            _wait_both(_remote(got, got, send_sems.at[j], recv_sems.at[j], me))

    return _Flight(f"exchange{tag}", [buf], len(flips), issue, drain, thru)


def _share_flight(tag, fins, thru):
    n = len(fins)

    def issue(refs, send_sems, recv_sems):
        x, y, c, _ = _place()
        for k in range(n):
            _remote(refs[k].at[c], refs[k].at[c], send_sems.at[k], recv_sems.at[k], (x, y, 1 - c)).start()

    def drain(refs, send_sems, recv_sems):
        x, y, c, _ = _place()
        for k in range(n):
            got = refs[k].at[1 - c]
            _wait_both(_remote(got, got, send_sems.at[k], recv_sems.at[k], (x, y, c)))

    return _Flight(f"share{tag}", fins, n, issue, drain, thru)


def _pair_blocks(w):
    h, d, _ = w.shape
    z = jnp.zeros((h // 2, d, d), w.dtype)
    return jnp.concatenate([jnp.concatenate([w[0::2], z], axis=2), jnp.concatenate([z, w[1::2]], axis=2)], axis=1)


def _unpair_blocks(b):
    n, dd, _ = b.shape
    d = dd // 2
    return jnp.stack([b[:, :d, :d], b[:, d:, d:]], axis=1).reshape(2 * n, d, d)


def _pad_rows(a, rows):
    return jnp.pad(a, ((0, rows - a.shape[0]), (0, 0)))


class _Packer:
    def __init__(self, shapes, width=1024, row_multiple=64):
        self.shapes = shapes
        self.sizes = [math.prod(s) for s in shapes]
        total = sum(self.sizes)
        self.width = width
        self.rows = -(-total // (width * row_multiple)) * row_multiple
        self.pad = self.rows * width - total

    def pack(self, arrays):
        flat = jnp.concatenate([a.reshape(-1).astype(F32) for a in arrays] + [jnp.zeros((self.pad,), F32)])
        return flat.reshape(self.rows, self.width)

    def unpack(self, packed):
        flat = packed.reshape(-1)
        out, off = [], 0
        for s, n in zip(self.shapes, self.sizes):
            out.append(flat[off:off + n].reshape(s))
            off += n
        return out


SMALL = ["b_ada", "ffn1_norm", "mix_norm", "conv_w", "conv_b", "gate_a_w", "gate_a_b", "gate_x_w", "gate_x_b", "lru_lambda",
         "v_norm", "spatial_w", "spatial_b", "lru_out_norm", "gmlp_out_norm", "ffn2_norm", "final_norm"]
BIG = ["ffn1_w_gu", "ffn1_w_down", "w_in", "w_out", "ffn2_w_gu", "ffn2_w_down"]
GROUPS = (("ffn1_w_gu", "ffn1_w_down"), ("w_in", "w_out"), ("ffn2_w_gu", "ffn2_w_down"))
FWD_GROUPS = (("ffn1_w_gu",), ("ffn1_w_down",), ("w_in", "w_out"), ("ffn2_w_gu",), ("ffn2_w_down",))
MIN_AGE = {"swap": 1, "scatter": 1, "share": 1}
WEIGHTS = ["w_ada", "b_ada", "ffn1_norm", "ffn1_w_gu", "ffn1_w_down", "mix_norm", "w_in", "conv_w", "conv_b", "gate_a_w", "gate_a_b",
           "gate_x_w", "gate_x_b", "lru_lambda", "v_norm", "spatial_w", "spatial_b", "lru_out_norm", "gmlp_out_norm", "w_out",
           "ffn2_norm", "ffn2_w_gu", "ffn2_w_down", "final_norm"]


def kernel(x, c, w_ada, b_ada, ffn1_norm, ffn1_w_gu, ffn1_w_down, mix_norm, w_in, conv_w, conv_b, gate_a_w, gate_a_b, gate_x_w, gate_x_b, lru_lambda, v_norm, spatial_w, spatial_b, lru_out_norm, gmlp_out_norm, w_out, ffn2_norm, ffn2_w_gu, ffn2_w_down, final_norm, loss_target, m_w_ada, m_b_ada, m_ffn1_norm, m_ffn1_w_gu, m_ffn1_w_down, m_mix_norm, m_w_in, m_conv_w, m_conv_b, m_gate_a_w, m_gate_a_b, m_gate_x_w, m_gate_x_b, m_lru_lambda, m_v_norm, m_spatial_w, m_spatial_b, m_lru_out_norm, m_gmlp_out_norm, m_w_out, m_ffn2_norm, m_ffn2_w_gu, m_ffn2_w_down, m_final_norm, v_w_ada, v_b_ada, v_ffn1_norm, v_ffn1_w_gu, v_ffn1_w_down, v_mix_norm, v_w_in, v_conv_w, v_conv_b, v_gate_a_w, v_gate_a_b, v_gate_x_w, v_gate_x_b, v_lru_lambda, v_v_norm, v_spatial_w, v_spatial_b, v_lru_out_norm, v_gmlp_out_norm, v_w_out, v_ffn2_norm, v_ffn2_w_gu, v_ffn2_w_down, v_final_norm):
    given = dict(locals())
    W = {n: given[n] for n in WEIGHTS}
    L = w_ada.shape[0]
    S, D = x.shape[1], x.shape[2]
    LW = conv_b.shape[1]
    hd = LW // HEADS
    xi, yi, ci = lax.axis_index("x"), lax.axis_index("y"), lax.axis_index("c")
    chip = 2 * xi + yi
    dev = 2 * chip + ci
    place = jnp.stack([ci, chip]).astype(jnp.int32)
    xs = x.reshape(S, D)
    tgt = loss_target.reshape(S, D)

    n_ada = w_ada.shape[2]
    cws = LW // N_CHIPS

    def half_view(s):
        return s.reshape(N_CHIPS, 2, s.shape[1] // 2, s.shape[2])

    stages = [(l, names) for l in range(L) for names in FWD_GROUPS]
    seq = [[half_view(_cast_into_slot(W[n], l, place, place)) for n in names] for l, names in stages[:1]]
    flights = {}

    def ici_plan(t):
        if t >= len(stages):
            return []
        stage = t % len(FWD_GROUPS)
        return {2: [(t, 0, 1), (t + 1, 0, 2)], 3: [(t, 1, 2)]}.get(stage, [(t, 0, 1)])

    def launch(t, thru, direct=()):
        ici = [(a, part, parts) for g, part, parts in ici_plan(t) for a in seq[g]]
        d2d = seq[t - 1] if 1 <= t <= len(seq) else []
        if ici or d2d or direct:
            flights[t] = _gather_flight(t, ici, d2d, list(direct), thru)
            thru = flights[t].thru
        return thru

    def land(t, after):
        if t not in flights:
            return []
        out = flights.pop(t).land(after)
        for g, _, _ in ici_plan(t):
            seq[g], out = out[:len(seq[g])], out[len(seq[g]):]
        if 1 <= t <= len(seq):
            seq[t - 1], out = out[:len(seq[t - 1])], out[len(seq[t - 1]):]
        return out

    def group_weights(t):
        return [s.reshape(N_CHIPS, -1, s.shape[3]) for s in seq[t]]

    c_rows = _pad_rows(c, SUBLANES)
    c_flight = _exchange_flight("c", lax.dynamic_update_index_in_dim(jnp.zeros((N_DEV,) + c_rows.shape, F32), c_rows, dev, 0), c_rows)
    started = launch(0, c_flight.thru)
    c_all = c_flight.land(started)[0][:, 0, :]
    b_shard = lax.dynamic_slice_in_dim(b_ada, chip * n_ada, n_ada, axis=1)
    mod_shard = _ada_fwd(_pad_rows(c_all, 2 * SUBLANES), w_ada, b_shard[:, None, :])
    seq += [[half_view(_cast_into_slot(W[n], l, place, started)) for n in names] for l, names in stages[1:]]

    def in_slot(block):
        return lax.dynamic_update_index_in_dim(jnp.zeros((N_CHIPS,) + block.shape, block.dtype), block, chip, 0)

    land(0, seq[-1][-1])
    small = [in_slot(mod_shard.reshape(L * 2 * SUBLANES, n_ada)), in_slot(conv_w.reshape(L * CONV_WIDTH, cws))]
    mod_all, conv_all = land(1, launch(1, mod_shard, small))
    mod_rows = lax.dynamic_index_in_dim(mod_all.reshape(N_CHIPS, L, 2 * SUBLANES, n_ada), dev, axis=2, keepdims=False)
    mod = mod_rows.transpose(1, 0, 2).reshape(L, N_MOD, 1, D)
    conv_full = conv_all.reshape(N_CHIPS, L, CONV_WIDTH, cws).transpose(1, 2, 0, 3).reshape(L, CONV_WIDTH, LW)

    tril = jnp.tril(jnp.ones((CHUNK, CHUNK), F32))
    seg = (jnp.arange(LW)[:, None] // hd == jnp.arange(LW)[None, :] // hd).astype(jnp.bfloat16)

    def mixer_params(l):
        ws = spatial_w[l] * tril
        wsp = jnp.concatenate([ws[0::2], ws[1::2]], axis=2)
        wa, wx = _pair_blocks(gate_a_w[l]), _pair_blocks(gate_x_w[l])
        return dict(
            cw=conv_full[l], cb=conv_b[l][None],
            wa=wa.astype(MXU_DTYPE), wx=wx.astype(MXU_DTYPE), wat=wa.transpose(0, 2, 1).astype(MXU_DTYPE), wxt=wx.transpose(0, 2, 1).astype(MXU_DTYPE),
            ba=gate_a_b[l].reshape(1, LW), bx=gate_x_b[l].reshape(1, LW), lam=lru_lambda[l][None], gv=v_norm[l][None],
            wsp=wsp.astype(MXU_DTYPE), wspt=wsp.transpose(0, 2, 1).astype(MXU_DTYPE),
            bfull=jnp.repeat(spatial_b[l].T, hd, axis=1), g_lru=lru_out_norm[l][None], g_gm=gmlp_out_norm[l][None])

    saved = []
    xcur = xs
    zero_row = jnp.zeros((1, D), F32)
    h = _modnorm(xcur, ffn1_norm[0][None], mod[0][0], mod[0][1])
    for l in range(L):
        mp, md = mixer_params(l), mod[l]
        s = dict(lw={}, mp=mp, md=md)
        lw = s["lw"]
        t = len(FWD_GROUPS) * l
        s["x0"] = xcur
        s["h1"] = launch(t + 2, h)
        lw["gu1"], = group_weights(t)
        s["a1"], s["gu1"] = _ffn_up(s["h1"], lw["gu1"])
        land(t + 2, s["a1"])
        s["a1"] = launch(t + 3, s["a1"])
        lw["d1"] = group_weights(t + 1)[0].reshape(-1, D)
        s["f1"], xcur, h = _mm_res(s["a1"], lw["d1"], xcur, md[2], 0.5, (mix_norm[l][None], md[3], md[4]))
        land(t + 3, xcur)
        s["x1"] = xcur
        s["h2"] = launch(t + 4, h)
        lw["win"], wout = group_weights(t + 2)
        lw["wout"] = wout.reshape(-1, D)
        s["proj"] = _mm_chunks(s["h2"], lw["win"])
        s["ylru"] = _lru_fwd(s["proj"], mp["cw"], mp["cb"], mp["wa"], mp["ba"], mp["wx"], mp["bx"], mp["lam"])
        s["yn"], s["ygm"] = _gmlp_fwd(s["proj"], s["ylru"], mp["gv"], seg, mp["wsp"], mp["bfull"], mp["g_lru"], mp["g_gm"])
        s["f2"], xcur, h = _mm_res(s["yn"], lw["wout"], xcur, md[5], 1.0, (ffn2_norm[l][None], md[6], md[7]))
        land(t + 4, xcur)
        s["x2"] = xcur
        s["h3"] = launch(t + 5, h)
        lw["gu2"], = group_weights(t + 3)
        s["a3"], s["gu3"] = _ffn_up(s["h3"], lw["gu2"])
        land(t + 5, s["a3"])
        s["a3"] = launch(t + 6, s["a3"])
        lw["d2"] = group_weights(t + 4)[0].reshape(-1, D)
        following = (ffn1_norm[l + 1][None], mod[l + 1][0], mod[l + 1][1]) if l + 1 < L else (final_norm[None], zero_row, zero_row)
        s["f3"], xcur, h = _mm_res(s["a3"], lw["d2"], xcur, md[8], 0.5, following)
        land(t + 6, xcur)
        saved.append(s)

    dx, dq, head_acc = _loss_head(xcur, tgt, final_norm[None], saved[-1]["md"][8], 0.5)
    loss = lax.psum(jnp.sum(head_acc[1]), ("x", "y", "c"))
    big_grads = {n: [None] * L for n in BIG}
    dmods = [None] * L

    def ffn_bwd(names, l, dx, dq, x_in, h, a, gu, f, wgu, wd, gn, sc, next_gate, next_scale):
        big_grads[names[1]][l] = _mm_tn_chunks(a, dq[None], 1408, 1024)[0].reshape(N_CHIPS, -1, D)
        dgu = _ffn_bwd_act(dq, wd, gu)
        C = dgu.shape[3]
        dgu4 = dgu.reshape(N_CHIPS, S, C)
        big_grads[names[0]][l] = _mm_tn_chunks(h, dgu4, 1024, C)
        dgu4 = reduce_group(names, l, big_grads[names[0]][l], dgu4)
        dx, dq, acc = _mm_nt_norm_bwd(dgu4, wgu, x_in, dx, f, gn, sc, 0.5, next_gate, next_scale)
        return dx, move_on(dx, dq), acc

    stepped = {n: None for n in BIG}
    reducing = []

    per_layer = [n for n in SMALL if n not in ("b_ada", "final_norm")]
    packers, exchanges = {}, {}
    clock = [0]
    to_step = []

    def step_reduced(after):
        while to_step:
            name, l, g = to_step.pop(0)
            stepped[name] = _adamw_layer(W[name], g, given["m_" + name], given["v_" + name], l, stepped[name], after)
            after = stepped[name][1]
        return after

    def move_on(after, thru, force=False):
        clock[0] += 1
        for grp in list(reducing):
            if not force and clock[0] - grp["since"] < MIN_AGE[grp["step"]]:
                continue
            grp["since"] = clock[0]
            landed = grp["flight"].land(after)
            n = len(grp["names"])
            if grp["step"] == "swap":
                pairs = [_add_half(g4, r1, place) for g4, r1 in zip(landed[:n], landed[n:])]
                grp.update(step="scatter", flight=_scatter_flight(grp["tag"], [h for h, _ in pairs], [own for _, own in pairs], thru))
            elif grp["step"] == "scatter":
                grp.update(step="share", flight=_share_flight(grp["tag"], [_sum4_into_half(r2, place) for r2 in landed[n:]], thru))
            else:
                to_step.extend((name, grp["l"], fin.reshape(2 * fin.shape[1], fin.shape[2])) for name, fin in zip(grp["names"], landed))
                reducing.remove(grp)
                continue
            thru = grp["flight"].thru
        return thru

    def reduce_group(names, l, after, thru):
        thru = move_on(after, thru)
        g4s = [big_grads[n][l].reshape(N_CHIPS, 2, big_grads[n][l].shape[1] // 2, big_grads[n][l].shape[2]) for n in names]
        tag = f"{l}{GROUPS.index(names)}"
        reducing.append(dict(names=names, l=l, tag=tag, step="swap", since=clock[0], flight=_swap_flight(tag, g4s, thru)))
        return reducing[-1]["flight"].thru
    for l in reversed(range(L)):
        s = saved[l]
        lw, mp, md = s["lw"], s["mp"], s["md"]
        dx, dq, acc3 = ffn_bwd(
            GROUPS[2], l, dx, dq, s["x2"], s["h3"], s["a3"], s["gu3"], s["f3"], lw["gu2"], lw["d2"], ffn2_norm[l][None], md[7], md[5], 1.0)
        big_grads["w_out"][l] = _mm_tn_chunks(s["yn"], dq[None], 1024, 1024)[0].reshape(N_CHIPS, -1, D)
        dyn = _mm_nt_chunks(dq[None], lw["wout"][None])
        dylru, duv, dwsp, dbfull, gacc = _gmlp_bwd(s["proj"], s["ylru"], s["ygm"], dyn, mp["gv"], seg, mp["wsp"], mp["wspt"], mp["bfull"], mp["g_lru"], mp["g_gm"])
        dxg, dwa, dwx, lvec = _lru_bwd(s["proj"], dylru, mp["cw"], mp["cb"], mp["wa"], mp["ba"], mp["wx"], mp["bx"], mp["lam"], mp["wat"], mp["wxt"])
        dproj = jnp.concatenate([dxg, duv], axis=0)
        big_grads["w_in"][l] = _mm_tn_together(s["h2"], dproj)
        dproj = reduce_group(GROUPS[1], l, big_grads["w_in"][l], dproj)
        dx, dq, acc2 = _mm_nt_norm_bwd(dproj, lw["win"], s["x1"], dx, s["f2"], mix_norm[l][None], md[4], 1.0, md[2], 0.5)
        dq = move_on(dx, dq)
        if l > 0:
            ng, ns = saved[l - 1]["md"][8], 0.5
        else:
            ng, ns = zero_row, 0.0
        dx, dq, acc1 = ffn_bwd(
            GROUPS[0], l, dx, dq, s["x0"], s["h1"], s["a1"], s["gu1"], s["f1"], lw["gu1"], lw["d1"], ffn1_norm[l][None], md[1], ng, ns)

        dmods[l] = jnp.concatenate([acc1[0:2], acc1[3:4], acc2[0:2], acc2[3:4], acc3[0:2], acc3[3:4]], axis=0)
        dws = jnp.stack([dwsp[:, :, :CHUNK], dwsp[:, :, CHUNK:]], axis=1).reshape(HEADS, CHUNK, CHUNK) * tril
        lg = {"ffn1_norm": acc1[2], "mix_norm": acc2[2], "ffn2_norm": acc3[2],
              "conv_w": lvec[4:8], "conv_b": lvec[3], "gate_a_w": _unpair_blocks(dwa), "gate_a_b": lvec[0].reshape(HEADS, hd),
              "gate_x_w": _unpair_blocks(dwx), "gate_x_b": lvec[1].reshape(HEADS, hd), "lru_lambda": lvec[2], "v_norm": gacc[2],
              "spatial_w": dws, "spatial_b": dbfull.reshape(CHUNK, HEADS, hd).sum(-1).T, "lru_out_norm": gacc[0], "gmlp_out_norm": gacc[1]}
        part = [lg[n] for n in per_layer] + [dmods[l]] + ([head_acc[0]] if l == L - 1 else [])
        packers[l] = _Packer([p.shape for p in part])
        packed = packers[l].pack(part)
        exchanges[l] = _exchange_flight(l, lax.dynamic_update_index_in_dim(jnp.zeros((N_DEV,) + packed.shape, F32), packed, dev, 0), dq)
        dq = exchanges[l].thru

    grad_x = dx.reshape(x.shape)

    done = step_reduced(dq)
    while reducing:
        dq = move_on(done, dq, force=True)
        done = step_reduced(dq)
    summed, dmod_rows = [], []
    for l in range(L):
        gathered, = exchanges[l].land(done)
        summed.append(packers[l].unpack(_sum_leading(gathered)))
        off = sum(packers[l].sizes[:len(per_layer)])
        dmod_rows.append(gathered.reshape(N_DEV, -1)[:, off:off + N_MOD * D])
    grads = {n: jnp.stack([summed[l][k] for l in range(L)]) for k, n in enumerate(per_layer)}
    grads["final_norm"] = summed[L - 1][len(per_layer) + 1]
    grads["b_ada"] = jnp.stack([summed[l][len(per_layer)].reshape(N_MOD * D) for l in range(L)])
    dmod_shard = lax.dynamic_slice_in_dim(jnp.stack(dmod_rows), chip * n_ada, n_ada, axis=2)
    stepped_ada = _ada_step(c_all.T, dmod_shard, w_ada, m_w_ada, v_w_ada)
    grads["w_ada"] = stepped_ada[0]
    grads["conv_w"] = lax.dynamic_slice_in_dim(grads["conv_w"], chip * cws, cws, axis=2)

    delta, new_m, new_v = {}, {}, {}
    for n in BIG:
        grads[n], delta[n], new_m[n], new_v[n] = stepped[n]
    delta["w_ada"], new_m["w_ada"], new_v["w_ada"] = stepped_ada[1:]
    def rows_of(a):
        return a.reshape(-1, a.shape[-1])

    stepped_small = _adamw_small(*[[rows_of(src[n].reshape(W[n].shape)) for n in SMALL]
                                   for src in (W, grads, {n: given["m_" + n] for n in SMALL}, {n: given["v_" + n] for n in SMALL})])
    for k, n in enumerate(SMALL):
        delta[n], new_m[n], new_v[n] = (stepped_small[i * len(SMALL) + k].reshape(W[n].shape) for i in range(3))
    grads = {n: grads[n].reshape(W[n].shape) for n in WEIGHTS}
    return (loss, grad_x, *[grads[n] for n in WEIGHTS], *[delta[n] for n in WEIGHTS], *[new_m[n] for n in WEIGHTS], *[new_v[n] for n in WEIGHTS])
```

```python
import math

import jax
import jax.numpy as jnp
from jax import lax
from jax.experimental import pallas as pl
from jax.experimental.pallas import tpu as pltpu

F32 = jnp.float32
MXU_DTYPE = jnp.bfloat16
ACT_DTYPE = jnp.bfloat16
XFER_DTYPE = jnp.bfloat16
EPS = 1e-6
RG_LRU_C = 8.0
N_MOD = 9
CONV_WIDTH = 4
HEADS = 8
CHUNK = 128
LANES = 128
SUBLANES = 8
N_CHIPS = 4
N_DEV = 8
ADAM_LR, ADAM_B1, ADAM_B2, ADAM_EPS, ADAM_WD, ADAM_STEP = 0.001, 0.9, 0.999, 1e-08, 0.01, 10
VMEM_LIMIT_BYTES = 60 * 1024 * 1024
ROW_TILE_BYTES = 2 << 20
GELU_C = math.sqrt(2.0 / math.pi)
GELU_A = 0.044715

ANY = pl.BlockSpec(memory_space=pl.ANY)
MESH = pl.DeviceIdType.MESH
SDS = jax.ShapeDtypeStruct


def _params(*sem):
    return pltpu.CompilerParams(dimension_semantics=sem, vmem_limit_bytes=VMEM_LIMIT_BYTES)


def _dot(a, b):
    return jnp.dot(a.astype(MXU_DTYPE), b.astype(MXU_DTYPE), preferred_element_type=F32)


def _dot_nt(a, b):
    return lax.dot_general(a.astype(MXU_DTYPE), b.astype(MXU_DTYPE), (((1,), (1,)), ((), ())), preferred_element_type=F32)


def _dot_tn(a, b):
    return lax.dot_general(a.astype(MXU_DTYPE), b.astype(MXU_DTYPE), (((0,), (0,)), ((), ())), preferred_element_type=F32)


def _gelu(x):
    return x * (0.5 * (1.0 + jnp.tanh(GELU_C * (x + GELU_A * (x * x * x)))))


def _gelu_grad(x):
    t = jnp.tanh(GELU_C * (x + GELU_A * (x * x * x)))
    return 0.5 * (1.0 + t) + 0.5 * x * (1.0 - t * t) * (GELU_C * (1.0 + 3.0 * GELU_A * x * x))


def _sigmoid(x):
    return jax.nn.sigmoid(x)


def _sigmoid_by_tanh(x):
    return 0.5 * jnp.tanh(0.5 * x) + 0.5


def _rsqrt_ms(x):
    return lax.rsqrt(jnp.mean(x * x, axis=-1, keepdims=True) + EPS)


def _rowsum(x):
    return jnp.sum(x, axis=0, keepdims=True)


def _tile(n, want):
    t = min(n, want)
    assert n % t == 0, (n, want)
    return t


def _row_tile(rows, row_bytes):
    step = 2 * SUBLANES
    cap = max(step, ROW_TILE_BYTES // row_bytes)
    best = None
    for t in range(step, min(rows, cap) + 1, step):
        if rows % t == 0:
            best = t
    assert best is not None, (rows, row_bytes)
    return best


def _modnorm(x, gn, sh, sc):
    S, D = x.shape
    tm = _tile(S, 1024)

    def body(x_ref, gn_ref, sh_ref, sc_ref, h_ref):
        xv = x_ref[...]
        h = (xv * _rsqrt_ms(xv) * gn_ref[...]) * (1.0 + sc_ref[...]) + sh_ref[...]
        h_ref[...] = h.astype(ACT_DTYPE)

    row = pl.BlockSpec((1, D), lambda i: (0, 0))
    return pl.pallas_call(
        body, name="modnorm", grid=(S // tm,),
        in_specs=[pl.BlockSpec((tm, D), lambda i: (i, 0)), row, row, row],
        out_specs=pl.BlockSpec((tm, D), lambda i: (i, 0)),
        out_shape=SDS((S, D), ACT_DTYPE), compiler_params=_params("parallel"),
    )(x, gn, sh, sc)


def _mm_nt_norm_bwd(ac, wc, x, dxo, f, gn, sc, res_scale, next_gate, next_scale):
    P, S, K = ac.shape
    D = x.shape[1]
    tm = _tile(S, 512)

    def body(a_ref, w_ref, x_ref, dxo_ref, f_ref, gn_ref, sc_ref, ng_ref, dx_ref, dq_ref, acc_ref):
        @pl.when(pl.program_id(0) == 0)
        def _():
            acc_ref[...] = jnp.zeros_like(acc_ref)

        dh = _dot_nt(a_ref[0], w_ref[0])
        for p in range(1, P):
            dh += _dot_nt(a_ref[p], w_ref[p])
        xv, dxo = x_ref[...], dxo_ref[...]
        r = _rsqrt_ms(xv)
        xhat = xv * r
        gn = gn_ref[...]
        dn = dh * (1.0 + sc_ref[...])
        dxh = dn * gn
        dx = dxo + r * (dxh - xhat * jnp.mean(dxh * xhat, axis=-1, keepdims=True))
        dx_ref[...] = dx
        dq_ref[...] = ((next_scale * ng_ref[...]) * dx).astype(ACT_DTYPE)
        acc_ref[0:1, :] += _rowsum(dh)
        acc_ref[1:2, :] += _rowsum(dh * (xhat * gn))
        acc_ref[2:3, :] += _rowsum(dn * xhat)
        acc_ref[3:4, :] += _rowsum((res_scale * f_ref[...]) * dxo)

    tile = pl.BlockSpec((tm, D), lambda i: (i, 0))
    row = pl.BlockSpec((1, D), lambda i: (0, 0))
    return pl.pallas_call(
        body, name=f"mm_nt_norm_bwd_k{K}", grid=(S // tm,),
        in_specs=[pl.BlockSpec((P, tm, K), lambda i: (0, i, 0)),
                  pl.BlockSpec((P, D, K), lambda i: (0, 0, 0), pipeline_mode=pl.Buffered(1)),
                  tile, tile, tile, row, row, row],
        out_specs=[tile, tile, pl.BlockSpec((SUBLANES, D), lambda i: (0, 0))],
        out_shape=[SDS((S, D), F32), SDS((S, D), ACT_DTYPE), SDS((SUBLANES, D), F32)],
        compiler_params=_params("arbitrary"),
    )(ac, wc, x, dxo, f, gn, sc, next_gate)


def _loss_head(x, target, gn, next_gate, next_scale):
    S, D = x.shape
    tm = _tile(S, 512)

    def body(x_ref, t_ref, gn_ref, ng_ref, dx_ref, dq_ref, acc_ref):
        @pl.when(pl.program_id(0) == 0)
        def _():
            acc_ref[...] = jnp.zeros_like(acc_ref)

        xv = x_ref[...]
        r = _rsqrt_ms(xv)
        xhat = xv * r
        gn = gn_ref[...]
        err = xhat * gn - t_ref[...]
        dy = err * (1.0 / D)
        dxh = dy * gn
        dx = r * (dxh - xhat * jnp.mean(dxh * xhat, axis=-1, keepdims=True))
        dx_ref[...] = dx
        dq_ref[...] = ((next_scale * ng_ref[...]) * dx).astype(ACT_DTYPE)
        acc_ref[0:1, :] += _rowsum(dy * xhat)
        acc_ref[1:2, :] += _rowsum(err * err) * (0.5 / D)

    tile = pl.BlockSpec((tm, D), lambda i: (i, 0))
    row = pl.BlockSpec((1, D), lambda i: (0, 0))
    return pl.pallas_call(
        body, name="loss_head", grid=(S // tm,),
        in_specs=[tile, tile, row, row],
        out_specs=[tile, tile, pl.BlockSpec((SUBLANES, D), lambda i: (0, 0))],
        out_shape=[SDS((S, D), F32), SDS((S, D), ACT_DTYPE), SDS((SUBLANES, D), F32)],
        compiler_params=_params("arbitrary"),
    )(x, target, gn, next_gate)


def _column_pieces(n, width=4 * LANES):
    return [(lo, min(lo + width, n)) for lo in range(0, n, width)]


def _ffn_up(h, wgu):
    S, D = h.shape
    C = wgu.shape[2]
    tm = _tile(S, 512)

    def body(h_ref, wg_ref, wu_ref, a_ref, gu_ref):
        hv = h_ref[...]
        g = _dot(hv, wg_ref[...])
        u = _dot(hv, wu_ref[...])
        a_ref[...] = (g * _sigmoid_by_tanh(g) * u).astype(ACT_DTYPE)
        gu_ref[0] = g.astype(ACT_DTYPE)
        gu_ref[1] = u.astype(ACT_DTYPE)

    return pl.pallas_call(
        body, name="ffn_up", grid=(2, S // tm),
        in_specs=[
            pl.BlockSpec((tm, D), lambda j, i: (i, 0)),
            pl.BlockSpec((None, D, C), lambda j, i: (j, 0, 0)),
            pl.BlockSpec((None, D, C), lambda j, i: (2 + j, 0, 0)),
        ],
        out_specs=[
            pl.BlockSpec((tm, C), lambda j, i: (i, j)),
            pl.BlockSpec((2, None, tm, C), lambda j, i: (0, j, i, 0)),
        ],
        out_shape=[SDS((S, 2 * C), ACT_DTYPE), SDS((2, 2, S, C), ACT_DTYPE)],
        compiler_params=_params("parallel", "parallel"),
    )(h, wgu, wgu)


def _ffn_bwd_act(dq, wd, gu):
    S, D = dq.shape
    C = gu.shape[3]
    tm = _tile(S, 512)

    def body(dq_ref, wd_ref, gu_ref, dgu_ref):
        dq = dq_ref[...]
        for lo, hi in _column_pieces(C):
            da = _dot_nt(dq, wd_ref[lo:hi, :])
            g = gu_ref[0, :, lo:hi].astype(F32)
            u = gu_ref[1, :, lo:hi].astype(F32)
            s = _sigmoid_by_tanh(g)
            gs = g * s
            dgu_ref[0, :, lo:hi] = ((da * u) * (s + gs - gs * s)).astype(ACT_DTYPE)
            dgu_ref[1, :, lo:hi] = (da * gs).astype(ACT_DTYPE)

    gu_spec = pl.BlockSpec((2, None, tm, C), lambda j, i: (0, j, i, 0))
    return pl.pallas_call(
        body, name="ffn_bwd_act", grid=(2, S // tm),
        in_specs=[pl.BlockSpec((tm, D), lambda j, i: (i, 0)), pl.BlockSpec((C, D), lambda j, i: (j, 0)), gu_spec],
        out_specs=gu_spec,
        out_shape=SDS(gu.shape, ACT_DTYPE),
        compiler_params=_params("parallel", "parallel"),
    )(dq, wd, gu)


def _mm_res(a, w, x, gate, scale, following):
    S, K = a.shape
    D = w.shape[1]
    tm = _tile(S, 512)

    def body(a_ref, w_ref, x_ref, g_ref, gn_ref, sh_ref, sc_ref, f_ref, xo_ref, h_ref):
        f = _dot(a_ref[...], w_ref[...])
        f_ref[...] = f
        xo = x_ref[...] + (scale * g_ref[...]) * f
        xo_ref[...] = xo
        h_ref[...] = ((xo * _rsqrt_ms(xo) * gn_ref[...]) * (1.0 + sc_ref[...]) + sh_ref[...]).astype(ACT_DTYPE)

    tile = pl.BlockSpec((tm, D), lambda i: (i, 0))
    row = pl.BlockSpec((1, D), lambda i: (0, 0))
    return pl.pallas_call(
        body, name=f"mm_res_k{K}", grid=(S // tm,),
        in_specs=[pl.BlockSpec((tm, K), lambda i: (i, 0)), pl.BlockSpec((K, D), lambda i: (0, 0)), tile, row, row, row, row],
        out_specs=[tile, tile, tile],
        out_shape=[SDS((S, D), F32), SDS((S, D), F32), SDS((S, D), ACT_DTYPE)],
        compiler_params=_params("parallel"),
    )(a, w, x, gate, *following)


def _mm_chunks(h, wc):
    S, K = h.shape
    P, _, N = wc.shape
    tm = _tile(S, 512)

    def body(h_ref, w_ref, o_ref):
        hv = h_ref[...]
        for p in range(P):
            o_ref[:, p * N:(p + 1) * N] = _dot(hv, w_ref[p])

    return pl.pallas_call(
        body, name="mm_chunks", grid=(S // tm,),
        in_specs=[pl.BlockSpec((tm, K), lambda i: (i, 0)), pl.BlockSpec((P, K, N), lambda i: (0, 0, 0))],
        out_specs=pl.BlockSpec((tm, P * N), lambda i: (i, 0)),
        out_shape=SDS((S, P * N), F32),
        compiler_params=_params("parallel"),
    )(h, wc)


def _mm_nt_chunks(ac, wc):
    P, S, K = ac.shape
    N = wc.shape[1]
    tm, tn = _tile(S, 512), _tile(N, 1024)

    def body(a_ref, w_ref, o_ref):
        acc = _dot_nt(a_ref[0], w_ref[0])
        for p in range(1, P):
            acc += _dot_nt(a_ref[p], w_ref[p])
        o_ref[...] = acc

    return pl.pallas_call(
        body, name=f"mm_nt_p{P}k{K}", grid=(S // tm, N // tn),
        in_specs=[pl.BlockSpec((P, tm, K), lambda i, j: (0, i, 0)), pl.BlockSpec((P, tn, K), lambda i, j: (0, j, 0))],
        out_specs=pl.BlockSpec((tm, tn), lambda i, j: (i, j)),
        out_shape=SDS((S, N), F32),
        compiler_params=_params("parallel", "parallel"),
    )(ac, wc)


def _mm_tn_chunks(a, bc, tile_m, tile_n):
    S, M = a.shape
    P, _, N = bc.shape
    ts, tm, tn = _tile(S, 2048), _tile(M, tile_m), _tile(N, tile_n)

    def body(a_ref, b_ref, o_ref):
        @pl.when(pl.program_id(3) == 0)
        def _():
            o_ref[...] = jnp.zeros_like(o_ref)

        o_ref[...] += _dot_tn(a_ref[...], b_ref[...])

    return pl.pallas_call(
        body, name=f"mm_tn_m{M}n{N}", grid=(P, M // tm, N // tn, S // ts),
        in_specs=[pl.BlockSpec((ts, tm), lambda p, m, n, k: (k, m)), pl.BlockSpec((None, ts, tn), lambda p, m, n, k: (p, k, n))],
        out_specs=pl.BlockSpec((None, tm, tn), lambda p, m, n, k: (p, m, n)),
        out_shape=SDS((P, M, N), F32),
        compiler_params=_params("parallel", "parallel", "parallel", "arbitrary"),
    )(a, bc)


def _mm_tn_together(a, bc):
    S, M = a.shape
    P, _, N = bc.shape
    ts = _tile(S, 2048)

    def body(a_ref, b_ref, o_ref):
        @pl.when(pl.program_id(0) == 0)
        def _():
            o_ref[...] = jnp.zeros_like(o_ref)

        at = a_ref[...].T
        for p in range(P):
            o_ref[p] += jnp.dot(at, b_ref[p], preferred_element_type=F32)

    return pl.pallas_call(
        body, name=f"mm_tn_together_m{M}n{N}", grid=(S // ts,),
        in_specs=[pl.BlockSpec((ts, M), lambda k: (k, 0)), pl.BlockSpec((P, ts, N), lambda k: (0, k, 0))],
        out_specs=pl.BlockSpec((P, M, N), lambda k: (0, 0, 0)),
        out_shape=SDS((P, M, N), F32), compiler_params=_params("arbitrary"),
    )(a, bc)


def _shift_down(x, s, row, fill):
    return jnp.where(row >= s, pltpu.roll(x, s, 0), fill)


def _shift_up(x, s, row, fill):
    n = x.shape[0]
    return jnp.where(row < n - s, pltpu.roll(x, n - s, 0), fill)


def _scan(a, b, row, scratch, up):
    scr_a, scr_b, scr_c = scratch
    n = a.shape[0]
    g = n // SUBLANES
    in_group = row & (SUBLANES - 1)

    def steps(a, b, pos, size):
        s = 1
        while s < size:
            m = (pos + s < size) if up else (pos >= s)
            b = jnp.where(m, a, 0.0) * pltpu.roll(b, a.shape[0] - s if up else s, 0) + b
            a = jnp.where(m, a * pltpu.roll(a, a.shape[0] - s if up else s, 0), a)
            s *= 2
        return a, b

    a, b = steps(a, b, in_group, SUBLANES)
    scr_a[...] = a
    scr_b[...] = b
    edge = 0 if up else SUBLANES - 1
    at = scr_a[pl.ds(edge, g, stride=SUBLANES), :]
    bt = scr_b[pl.ds(edge, g, stride=SUBLANES), :]
    group = lax.broadcasted_iota(jnp.int32, at.shape, 0)
    _, state = steps(at, bt, group, g)
    carry = jnp.where((group + 1 < g) if up else (group >= 1), pltpu.roll(state, g - 1 if up else 1, 0), 0.0)
    for k in range(SUBLANES):
        scr_c[pl.ds(k, g, stride=SUBLANES), :] = carry
    return b + a * scr_c[...]


def _conv(xl, cw_ref, cb_ref, row):
    y = cb_ref[...] + _shift_down(xl, 3, row, 0.0) * cw_ref[0:1, :]
    y = y + _shift_down(xl, 2, row, 0.0) * cw_ref[1:2, :]
    y = y + _shift_down(xl, 1, row, 0.0) * cw_ref[2:3, :]
    return y + xl * cw_ref[3:4, :]


def _lru_gates(xc, wa_ref, ba_ref, wx_ref, bx_ref, lam_ref):
    ra = _sigmoid(_dot(xc, wa_ref[...]) + ba_ref[...])
    ri = _sigmoid(_dot(xc, wx_ref[...]) + bx_ref[...])
    ls = jax.nn.log_sigmoid(lam_ref[...])
    a = jnp.exp((RG_LRU_C * ra) * ls)
    mult = jnp.sqrt(1.0 - a * a)
    return ra, ri, ls, a, mult


def _lru_specs(S):
    col = lambda off: pl.BlockSpec((S, LANES), lambda j: (0, off + j))
    vec = pl.BlockSpec((1, LANES), lambda j: (0, j))
    blk = pl.BlockSpec((None, LANES, LANES), lambda j: (j, 0, 0))
    cw = pl.BlockSpec((CONV_WIDTH, LANES), lambda j: (0, j))
    return col, vec, blk, cw


def _lru_fwd(proj, cw, cb, wa, ba, wx, bx, lam):
    S = proj.shape[0]
    W = cb.shape[1]
    nb = W // LANES

    def body(xl_ref, gl_ref, cw_ref, cb_ref, wa_ref, ba_ref, wx_ref, bx_ref, lam_ref, y_ref, *scratch):
        row = lax.broadcasted_iota(jnp.int32, (S, LANES), 0)
        xc = _conv(xl_ref[...], cw_ref, cb_ref, row)
        _, ri, _, a, mult = _lru_gates(xc, wa_ref, ba_ref, wx_ref, bx_ref, lam_ref)
        h = _scan(a, mult * (ri * xc), row, scratch, up=False)
        y_ref[...] = h * _gelu(gl_ref[...])

    col, vec, blk, cws = _lru_specs(S)
    return pl.pallas_call(
        body, name="lru_fwd", grid=(nb,),
        in_specs=[col(0), col(nb), cws, vec, blk, vec, blk, vec, vec],
        out_specs=pl.BlockSpec((S, LANES), lambda j: (0, j)),
        out_shape=SDS((S, W), F32), scratch_shapes=[pltpu.VMEM((S, LANES), F32)] * 3, compiler_params=_params("parallel"),
    )(proj, proj, cw, cb, wa, ba, wx, bx, lam)


def _lru_bwd(proj, dy, cw, cb, wa, ba, wx, bx, lam, wat, wxt):
    S = proj.shape[0]
    W = cb.shape[1]
    nb = W // LANES

    def body(xl_ref, gl_ref, dy_ref, cw_ref, cb_ref, wa_ref, ba_ref, wx_ref, bx_ref, lam_ref, wat_ref, wxt_ref,
             dp_ref, dwa_ref, dwx_ref, vec_ref, *scratch):
        row = lax.broadcasted_iota(jnp.int32, (S, LANES), 0)
        xl = xl_ref[...]
        xc = _conv(xl, cw_ref, cb_ref, row)
        ra, ri, ls, a, mult = _lru_gates(xc, wa_ref, ba_ref, wx_ref, bx_ref, lam_ref)
        h = _scan(a, mult * (ri * xc), row, scratch, up=False)
        gl = gl_ref[...]
        dyv = dy_ref[...]
        dp_ref[1] = (dyv * h * _gelu_grad(gl)).astype(ACT_DTYPE)
        adj = _scan(_shift_up(a, 1, row, 0.0), dyv * _gelu(gl), row, scratch, up=True)
        da = adj * _shift_down(h, 1, row, 0.0)
        dmult = adj * (ri * xc)
        dlog_a = da * a - dmult * (a * a) / mult
        dra = dlog_a * (RG_LRU_C * ls)
        dpa = dra * ra * (1.0 - ra)
        dpi = (adj * mult * xc) * ri * (1.0 - ri)
        dxc = adj * mult * ri + _dot(dpa, wat_ref[...]) + _dot(dpi, wxt_ref[...])
        dwa_ref[...] = _dot_tn(xc, dpa)
        dwx_ref[...] = _dot_tn(xc, dpi)
        dxl = dxc * cw_ref[3:4, :]
        dxl = dxl + _shift_up(dxc, 1, row, 0.0) * cw_ref[2:3, :]
        dxl = dxl + _shift_up(dxc, 2, row, 0.0) * cw_ref[1:2, :]
        dxl = dxl + _shift_up(dxc, 3, row, 0.0) * cw_ref[0:1, :]
        dp_ref[0] = dxl.astype(ACT_DTYPE)
        vec_ref[...] = jnp.zeros_like(vec_ref)
        vec_ref[0:1, :] = _rowsum(dpa)
        vec_ref[1:2, :] = _rowsum(dpi)
        vec_ref[2:3, :] = _rowsum(dlog_a * (RG_LRU_C * ra)) * _sigmoid(-lam_ref[...])
        vec_ref[3:4, :] = _rowsum(dxc)
        vec_ref[4:5, :] = _rowsum(dxc * _shift_down(xl, 3, row, 0.0))
        vec_ref[5:6, :] = _rowsum(dxc * _shift_down(xl, 2, row, 0.0))
        vec_ref[6:7, :] = _rowsum(dxc * _shift_down(xl, 1, row, 0.0))
        vec_ref[7:8, :] = _rowsum(dxc * xl)

    col, vec, blk, cws = _lru_specs(S)
    return pl.pallas_call(
        body, name="lru_bwd", grid=(nb,),
        in_specs=[col(0), col(nb), col(0), cws, vec, blk, vec, blk, vec, vec, blk, blk],
        out_specs=[pl.BlockSpec((2, S, LANES), lambda j: (0, 0, j)), blk, blk, pl.BlockSpec((2 * SUBLANES, LANES), lambda j: (0, j))],
        out_shape=[SDS((2, S, W), ACT_DTYPE), SDS((nb, LANES, LANES), F32), SDS((nb, LANES, LANES), F32), SDS((2 * SUBLANES, W), F32)],
        scratch_shapes=[pltpu.VMEM((S, LANES), F32)] * 3, compiler_params=_params("parallel"),
    )(proj, proj, dy, cw, cb, wa, ba, wx, bx, lam, wat, wxt)


def _seg_mean(x, seg_ref, width):
    hi = x.astype(jnp.bfloat16)
    lo = (x - hi.astype(F32)).astype(jnp.bfloat16)
    ones = seg_ref[...]
    s = jnp.dot(hi, ones, preferred_element_type=F32) + jnp.dot(lo, ones, preferred_element_type=F32)
    return s * (1.0 / width)


def _gmlp_core(u_ref, v_ref, gv_ref, seg_ref, ws_ref, bfull_ref, z_scr, hd):
    tm, W = u_ref.shape
    lane = lax.broadcasted_iota(jnp.int32, (CHUNK, LANES), 1)
    ug = _gelu(u_ref[...])
    vg = _gelu(v_ref[...])
    cen = vg - _seg_mean(vg, seg_ref, hd)
    rstd = lax.rsqrt(_seg_mean(cen * cen, seg_ref, hd) + EPS)
    vhat = cen * rstd
    vh = vhat * gv_ref[...]
    vcats = {}
    for ci in range(tm // CHUNK):
        for p in range(W // LANES):
            blk = vh[ci * CHUNK:(ci + 1) * CHUNK, p * LANES:(p + 1) * LANES]
            vcat = jnp.concatenate([jnp.where(lane < hd, blk, 0.0), jnp.where(lane >= hd, blk, 0.0)], axis=0).astype(MXU_DTYPE)
            vcats[ci, p] = vcat
            z_scr[ci * CHUNK:(ci + 1) * CHUNK, p * LANES:(p + 1) * LANES] = (
                jnp.dot(ws_ref[p], vcat, preferred_element_type=F32) + bfull_ref[:, p * LANES:(p + 1) * LANES])
    return ug, vhat, rstd, vcats


def _gmlp_specs(tm, W, nb):
    rows = lambda off: pl.BlockSpec((tm, W), lambda i: (i, off))
    vec = pl.BlockSpec((1, W), lambda i: (0, 0))
    seg = pl.BlockSpec((W, W), lambda i: (0, 0))
    wsp = pl.BlockSpec((nb, CHUNK, 2 * CHUNK), lambda i: (0, 0, 0))
    bfull = pl.BlockSpec((CHUNK, W), lambda i: (0, 0))
    return rows, vec, seg, wsp, bfull


def _gmlp_fwd(proj, ylru, gv, seg, wsp, bfull, g_lru, g_gm):
    S, W = ylru.shape
    nb = W // LANES
    hd = W // HEADS
    tm = _tile(S, 512)

    def body(u_ref, v_ref, yl_ref, gv_ref, seg_ref, ws_ref, bfull_ref, gl_ref, gg_ref, yn_ref, ygm_ref, z_scr):
        ug, _, _, _ = _gmlp_core(u_ref, v_ref, gv_ref, seg_ref, ws_ref, bfull_ref, z_scr, hd)
        ygm = ug * z_scr[...]
        ygm_ref[...] = ygm
        yl = yl_ref[...]
        yn_ref[:, 0:W] = (yl * _rsqrt_ms(yl) * gl_ref[...]).astype(ACT_DTYPE)
        yn_ref[:, W:2 * W] = (ygm * _rsqrt_ms(ygm) * gg_ref[...]).astype(ACT_DTYPE)

    rows, vec, segs, wsps, bfulls = _gmlp_specs(tm, W, nb)
    return pl.pallas_call(
        body, name="gmlp_fwd", grid=(S // tm,),
        in_specs=[rows(2), rows(3), rows(0), vec, segs, wsps, bfulls, vec, vec],
        out_specs=[pl.BlockSpec((tm, 2 * W), lambda i: (i, 0)), rows(0)],
        out_shape=[SDS((S, 2 * W), ACT_DTYPE), SDS((S, W), F32)],
        scratch_shapes=[pltpu.VMEM((tm, W), F32)],
        compiler_params=_params("parallel"),
    )(proj, proj, ylru, gv, seg, wsp, bfull, g_lru, g_gm)


def _rms_bwd(y, g, dyn):
    r = _rsqrt_ms(y)
    yhat = y * r
    dyh = dyn * g
    return r * (dyh - yhat * jnp.mean(dyh * yhat, axis=-1, keepdims=True)), _rowsum(dyn * yhat)


def _gmlp_bwd(proj, ylru, ygm, dyn, gv, seg, wsp, wspt, bfull, g_lru, g_gm):
    S, W = ylru.shape
    nb = W // LANES
    hd = W // HEADS
    tm = _tile(S, 256)

    def body(u_ref, v_ref, yl_ref, ygm_ref, dl_ref, dg_ref, gv_ref, seg_ref, ws_ref, wst_ref, bfull_ref, gl_ref, gg_ref,
             dyl_ref, duv_ref, dws_ref, dbf_ref, acc_ref, z_scr, dvh_scr):
        @pl.when(pl.program_id(0) == 0)
        def _():
            dws_ref[...] = jnp.zeros_like(dws_ref)
            dbf_ref[...] = jnp.zeros_like(dbf_ref)
            acc_ref[...] = jnp.zeros_like(acc_ref)

        dyl, dgl = _rms_bwd(yl_ref[...], gl_ref[...], dl_ref[...])
        dyl_ref[...] = dyl
        dygm, dgg = _rms_bwd(ygm_ref[...], gg_ref[...], dg_ref[...])
        ug, vhat, rstd, vcats = _gmlp_core(u_ref, v_ref, gv_ref, seg_ref, ws_ref, bfull_ref, z_scr, hd)
        duv_ref[0] = (dygm * z_scr[...] * _gelu_grad(u_ref[...])).astype(ACT_DTYPE)
        dz = dygm * ug
        lane = lax.broadcasted_iota(jnp.int32, (CHUNK, LANES), 1)
        dbf = dz[0:CHUNK, :]
        for ci in range(1, tm // CHUNK):
            dbf += dz[ci * CHUNK:(ci + 1) * CHUNK, :]
        dbf_ref[...] += dbf
        for ci in range(tm // CHUNK):
            for p in range(nb):
                dzb = dz[ci * CHUNK:(ci + 1) * CHUNK, p * LANES:(p + 1) * LANES].astype(MXU_DTYPE)
                dws_ref[p] += _dot_nt(dzb, vcats[ci, p])
                dvc = jnp.dot(wst_ref[p], dzb, preferred_element_type=F32)
                dvh_scr[ci * CHUNK:(ci + 1) * CHUNK, p * LANES:(p + 1) * LANES] = jnp.where(lane < hd, dvc[0:CHUNK], dvc[CHUNK:2 * CHUNK])
        dvh = dvh_scr[...]
        dvn = dvh * gv_ref[...]
        dvg = rstd * (dvn - _seg_mean(dvn, seg_ref, hd) - vhat * _seg_mean(dvn * vhat, seg_ref, hd))
        duv_ref[1] = (dvg * _gelu_grad(v_ref[...])).astype(ACT_DTYPE)
        acc_ref[0:1, :] += dgl
        acc_ref[1:2, :] += dgg
        acc_ref[2:3, :] += _rowsum(dvh * vhat)

    rows, vec, segs, wsps, bfulls = _gmlp_specs(tm, W, nb)
    wspt_spec = pl.BlockSpec((nb, 2 * CHUNK, CHUNK), lambda i: (0, 0, 0))
    return pl.pallas_call(
        body, name="gmlp_bwd", grid=(S // tm,),
        in_specs=[rows(2), rows(3), rows(0), rows(0), rows(0), rows(1), vec, segs, wsps, wspt_spec, bfulls, vec, vec],
        out_specs=[rows(0), pl.BlockSpec((2, tm, W), lambda i: (0, i, 0)), wsps, bfulls, pl.BlockSpec((SUBLANES, W), lambda i: (0, 0))],
        out_shape=[SDS((S, W), F32), SDS((2, S, W), ACT_DTYPE), SDS((nb, CHUNK, 2 * CHUNK), F32), SDS((CHUNK, W), F32), SDS((SUBLANES, W), F32)],
        scratch_shapes=[pltpu.VMEM((tm, W), F32), pltpu.VMEM((tm, W), F32)],
        compiler_params=_params("arbitrary"),
    )(proj, proj, ylru, ygm, dyn, dyn, gv, seg, wsp, wspt, bfull, g_lru, g_gm)


def _ada_fwd(c_all, w_ada, b_shard):
    L, D, N = w_ada.shape
    R = c_all.shape[0]
    tn = N // 2

    def body(c_ref, w_ref, b_ref, o_ref):
        cv = c_ref[...]
        o_ref[...] = _dot(cv * _sigmoid(cv), w_ref[...]) + b_ref[...]

    return pl.pallas_call(
        body, name="ada_fwd", grid=(L, N // tn),
        in_specs=[pl.BlockSpec((R, D), lambda l, j: (0, 0)), pl.BlockSpec((None, D, tn), lambda l, j: (l, 0, j)),
                  pl.BlockSpec((None, 1, tn), lambda l, j: (l, 0, j))],
        out_specs=pl.BlockSpec((None, R, tn), lambda l, j: (l, 0, j)),
        out_shape=SDS((L, R, N), F32), compiler_params=_params("parallel", "parallel"),
    )(c_all, w_ada, b_shard)


def _ada_step(c_all_t, dmod, w, m, v):
    D, B = c_all_t.shape
    L, _, N = dmod.shape
    tn = 3 * LANES
    assert N % tn == 0, N

    def body(c_ref, d_ref, w_ref, m_ref, v_ref, g_ref, do_ref, mo_ref, vo_ref):
        cv = c_ref[...]
        sc = cv * _sigmoid(cv)
        g = sc[:, 0:1] * d_ref[0:1, :]
        for b in range(1, B):
            g += sc[:, b:b + 1] * d_ref[b:b + 1, :]
        g_ref[...] = g
        do_ref[...], mo_ref[...], vo_ref[...] = _adam_math(w_ref[...], g, m_ref[...], v_ref[...])

    tile = pl.BlockSpec((None, D, tn), lambda l, j: (l, 0, j))
    return pl.pallas_call(
        body, name="ada_step", grid=(L, N // tn),
        in_specs=[pl.BlockSpec((D, B), lambda l, j: (0, 0)), pl.BlockSpec((None, B, tn), lambda l, j: (l, 0, j)), tile, tile, tile],
        out_specs=[tile] * 4, out_shape=[SDS((L, D, N), F32)] * 4, compiler_params=_params("parallel", "parallel"),
    )(c_all_t, dmod, w, m, v)


def _adam_math(w, g, m, v):
    mn = ADAM_B1 * m + (1.0 - ADAM_B1) * g
    vn = ADAM_B2 * v + (1.0 - ADAM_B2) * (g * g)
    m_hat = mn / (1.0 - ADAM_B1 ** ADAM_STEP)
    v_hat = vn / (1.0 - ADAM_B2 ** ADAM_STEP)
    return -ADAM_LR * (m_hat / (jnp.sqrt(v_hat) + ADAM_EPS) + ADAM_WD * w), mn, vn


def _adamw_layer(w, g, m, v, l, prev, after):
    L, R, C = w.shape
    tr = _row_tile(R, C * 4)
    prev = (after,) + (() if prev is None else tuple(prev))

    def body(w_ref, g_ref, m_ref, v_ref, *rest):
        go_ref, d_ref, mo_ref, vo_ref = rest[len(prev):]
        gv = g_ref[...]
        go_ref[...] = gv
        d_ref[...], mo_ref[...], vo_ref[...] = _adam_math(w_ref[...], gv, m_ref[...], v_ref[...])

    lay = pl.BlockSpec((None, tr, C), lambda i: (l, i, 0))
    return pl.pallas_call(
        body, name=f"adamw_layer_r{R}c{C}", grid=(R // tr,),
        in_specs=[lay, pl.BlockSpec((tr, C), lambda i: (i, 0)), lay, lay] + [ANY] * len(prev), out_specs=[lay] * 4,
        out_shape=[SDS((L, R, C), F32)] * 4, input_output_aliases={5 + k: k for k in range(len(prev) - 1)},
        compiler_params=_params("parallel"),
    )(w, g, m, v, *prev)


def _adamw_small(ws, gs, ms, vs):
    n = len(ws)

    def body(*refs):
        outs = refs[4 * n:]
        for k in range(n):
            outs[k][...], outs[n + k][...], outs[2 * n + k][...] = _adam_math(
                refs[k][...], refs[n + k][...], refs[2 * n + k][...], refs[3 * n + k][...])

    whole = pl.BlockSpec(memory_space=pltpu.VMEM)
    return pl.pallas_call(
        body, name="adamw_small", in_specs=[whole] * (4 * n), out_specs=[whole] * (3 * n),
        out_shape=[SDS(w.shape, F32) for w in ws] * 3, compiler_params=pltpu.CompilerParams(vmem_limit_bytes=VMEM_LIMIT_BYTES),
    )(*ws, *gs, *ms, *vs)


def _sum_leading(a):
    P, R, C = a.shape
    tr = _row_tile(R, P * C * 4)

    def body(a_ref, o_ref):
        acc = a_ref[0]
        for p in range(1, P):
            acc = acc + a_ref[p]
        o_ref[...] = acc

    return pl.pallas_call(
        body, name=f"sum{P}_r{R}c{C}", grid=(R // tr,),
        in_specs=[pl.BlockSpec((P, tr, C), lambda i: (0, i, 0))],
        out_specs=pl.BlockSpec((tr, C), lambda i: (i, 0)),
        out_shape=SDS((R, C), F32), compiler_params=_params("parallel"),
    )(a)


def _add_half(g4, r1, place):
    _, _, R, C = g4.shape
    tr = _row_tile(R, C * 4)

    def body(place_ref, g_ref, r_ref, h_ref, own_ref):
        s = (g_ref[...] + r_ref[...]).astype(XFER_DTYPE)
        h_ref[...] = s

        @pl.when(pl.program_id(1) == place_ref[1])
        def _():
            own_ref[...] = s

    return pl.pallas_call(
        body, name=f"add_half_r{R}c{C}",
        grid_spec=pltpu.PrefetchScalarGridSpec(
            num_scalar_prefetch=1, grid=(R // tr, N_CHIPS),
            in_specs=[pl.BlockSpec((None, None, tr, C), lambda i, p, place_ref: (p, place_ref[0], i, 0)),
                      pl.BlockSpec((None, tr, C), lambda i, p, place_ref: (p, i, 0))],
            out_specs=[pl.BlockSpec((None, tr, C), lambda i, p, place_ref: (p, i, 0)),
                       pl.BlockSpec((None, tr, C), lambda i, p, place_ref: (place_ref[1], i, 0))],
        ),
        out_shape=[SDS((N_CHIPS, R, C), XFER_DTYPE)] * 2, compiler_params=_params("parallel", "arbitrary"),
    )(place, g4, r1)


def _sum4_into_half(r2, place):
    P, R, C = r2.shape
    tr = _row_tile(R, P * C * 4)

    def body(place_ref, a_ref, o_ref):
        acc = a_ref[0].astype(F32)
        for p in range(1, P):
            acc = acc + a_ref[p].astype(F32)
        o_ref[...] = acc

    return pl.pallas_call(
        body, name=f"sum4_r{R}c{C}",
        grid_spec=pltpu.PrefetchScalarGridSpec(
            num_scalar_prefetch=1, grid=(R // tr,),
            in_specs=[pl.BlockSpec((P, tr, C), lambda i, place_ref: (0, i, 0))],
            out_specs=pl.BlockSpec((None, tr, C), lambda i, place_ref: (place_ref[0], i, 0)),
        ),
        out_shape=SDS((2, R, C), F32), compiler_params=_params("parallel"),
    )(place, r2)


def _cast_into_slot(w, l, place, after):
    _, R, C = w.shape
    tr = _row_tile(R, C * 4)

    def body(place_ref, w_ref, after_ref, o_ref):
        o_ref[...] = w_ref[...].astype(MXU_DTYPE)

    return pl.pallas_call(
        body, name=f"cast_r{R}c{C}",
        grid_spec=pltpu.PrefetchScalarGridSpec(
            num_scalar_prefetch=1, grid=(R // tr,),
            in_specs=[pl.BlockSpec((None, tr, C), lambda i, place_ref: (l, i, 0)), ANY],
            out_specs=pl.BlockSpec((None, tr, C), lambda i, place_ref: (place_ref[1], i, 0)),
        ),
        out_shape=SDS((N_CHIPS, R, C), MXU_DTYPE), compiler_params=_params("parallel"),
    )(place, w, after)


def _place():
    x, y, c = lax.axis_index("x"), lax.axis_index("y"), lax.axis_index("c")
    chips = [(1 - x, y), (x, 1 - y), (1 - x, 1 - y)]
    return x, y, c, chips


def _remote(src, dst, send_sem, recv_sem, to):
    return pltpu.make_async_remote_copy(src_ref=src, dst_ref=dst, send_sem=send_sem, recv_sem=recv_sem, device_id=to, device_id_type=MESH)


HBM_SPEC = pl.BlockSpec(memory_space=pltpu.HBM)
SEM_SPEC = pl.BlockSpec(memory_space=pltpu.SEMAPHORE)
DATAFLOW = pltpu.SideEffectType.DATAFLOW_SIDE_EFFECTING


def _in_hbm(a):
    return pltpu.with_memory_space_constraint(a, pltpu.HBM)


def _hbm_like(a):
    return pltpu.HBM(a.shape, a.dtype)


def _split_start(name, arrays, n_sems, issue, extra=()):
    m = len(arrays)

    def body(*refs):
        issue(refs[:m], refs[m + len(extra)], refs[m + len(extra) + 1])

    out = pl.pallas_call(
        body, name=name,
        out_shape=(pltpu.SemaphoreType.DMA((n_sems,)), pltpu.SemaphoreType.DMA((n_sems,)), *[_hbm_like(a) for a in arrays]),
        in_specs=[HBM_SPEC] * m + [ANY] * len(extra), out_specs=(SEM_SPEC, SEM_SPEC, *[HBM_SPEC] * m),
        input_output_aliases={k: 2 + k for k in range(m)},
        compiler_params=pltpu.CompilerParams(has_side_effects=DATAFLOW),
    )(*[_in_hbm(a) for a in arrays], *extra)
    return out[0], out[1], list(out[2:])


def _split_wait(name, send_sems, recv_sems, arrays, after, drain):
    m = len(arrays)

    def body(*refs):
        drain(refs[:m], refs[m], refs[m + 1])

    out = pl.pallas_call(
        body, name=name, out_shape=[_hbm_like(a) for a in arrays],
        in_specs=[HBM_SPEC] * m + [SEM_SPEC, SEM_SPEC, ANY], out_specs=[HBM_SPEC] * m,
        input_output_aliases={k: k for k in range(m)},
        compiler_params=pltpu.CompilerParams(has_side_effects=DATAFLOW),
    )(*arrays, send_sems, recv_sems, after)
    return list(out)


def _wait_both(cp):
    cp.wait_send()
    cp.wait_recv()


class _Flight:
    def __init__(self, name, arrays, n_sems, issue, drain, thru, extra=()):
        self.name, self.drain, self.n = name, drain, len(arrays)
        self.send, self.recv, out = _split_start(name + "_start", [*arrays, thru], n_sems, issue, extra)
        self.arrays, self.thru = out[:-1], out[-1]

    def land(self, after):
        return _split_wait(self.name + "_wait", self.send, self.recv, self.arrays, after, self.drain)


def _gather_flight(tag, ici, d2d, direct, thru):
    kinds = ["ici"] * len(ici) + ["d2d"] * len(d2d) + ["direct"] * len(direct)
    rows = [pl.ds(part * (a.shape[2] // parts), a.shape[2] // parts) for a, part, parts in ici]
    ici = [a for a, _, _ in ici]

    def issue(refs, send_sems, recv_sems):
        x, y, c, chips = _place()
        q = 2 * x + y
        for k, kind in enumerate(kinds):
            for j, chip in enumerate(chips):
                if kind == "ici":
                    src, to = refs[k].at[q, c, rows[k]], (*chip, c)
                elif kind == "d2d":
                    src, to = refs[k].at[2 * chip[0] + chip[1], c], (x, y, 1 - c)
                else:
                    src, to = refs[k].at[q], (*chip, c)
                _remote(src, src, send_sems.at[3 * k + j], recv_sems.at[3 * k + j], to).start()

    def drain(refs, send_sems, recv_sems):
        x, y, c, chips = _place()
        for k, kind in enumerate(kinds):
            for j, chip in enumerate(chips):
                p = 2 * chip[0] + chip[1]
                got = refs[k].at[p] if kind == "direct" else refs[k].at[p, c, rows[k]] if kind == "ici" else refs[k].at[p, 1 - c]
                _wait_both(_remote(got, got, send_sems.at[3 * k + j], recv_sems.at[3 * k + j], (x, y, c)))

    return _Flight(f"gather{tag}", [*ici, *d2d, *direct], 3 * len(kinds), issue, drain, thru)


def _swap_flight(tag, g4s, thru):
    n = len(g4s)
    zones = [lax.empty((N_CHIPS,) + g.shape[2:], g.dtype) for g in g4s]

    def issue(refs, send_sems, recv_sems):
        x, y, c, _ = _place()
        for k in range(n):
            for p in range(N_CHIPS):
                _remote(refs[k].at[p, 1 - c], refs[n + k].at[p], send_sems.at[N_CHIPS * k + p], recv_sems.at[N_CHIPS * k + p], (x, y, 1 - c)).start()

    def drain(refs, send_sems, recv_sems):
        x, y, c, _ = _place()
        for k in range(n):
            for p in range(N_CHIPS):
                got = refs[n + k].at[p]
                _wait_both(_remote(got, got, send_sems.at[N_CHIPS * k + p], recv_sems.at[N_CHIPS * k + p], (x, y, c)))

    return _Flight(f"swap{tag}", [*g4s, *zones], N_CHIPS * n, issue, drain, thru)


def _scatter_flight(tag, hs, lands, thru):
    n = len(hs)

    def issue(refs, send_sems, recv_sems):
        x, y, c, chips = _place()
        q = 2 * x + y
        for k in range(n):
            for j, chip in enumerate(chips):
                _remote(refs[k].at[2 * chip[0] + chip[1]], refs[n + k].at[q], send_sems.at[3 * k + j], recv_sems.at[3 * k + j], (*chip, c)).start()

    def drain(refs, send_sems, recv_sems):
        x, y, c, chips = _place()
        for k in range(n):
            for j, chip in enumerate(chips):
                got = refs[n + k].at[2 * chip[0] + chip[1]]
                _wait_both(_remote(got, got, send_sems.at[3 * k + j], recv_sems.at[3 * k + j], (x, y, c)))

    return _Flight(f"scatter{tag}", [*hs, *lands], 3 * n, issue, drain, thru)


def _exchange_flight(tag, buf, thru):
    flips = [(fx, fy, fc) for fx in (0, 1) for fy in (0, 1) for fc in (0, 1)][1:]

    def peers():
        x, y, c, _ = _place()
        return (x, y, c), [((1 - x) if fx else x, (1 - y) if fy else y, (1 - c) if fc else c) for fx, fy, fc in flips]

    def slot(ref, dev):
        return ref.at[4 * dev[0] + 2 * dev[1] + dev[2]]

    def issue(refs, send_sems, recv_sems):
        me, others = peers()
        for j, to in enumerate(others):
            _remote(slot(refs[0], me), slot(refs[0], me), send_sems.at[j], recv_sems.at[j], to).start()

    def drain(refs, send_sems, recv_sems):
        me, others = peers()
        for j, frm in enumerate(others):
            got = slot(refs[0], frm)
            _wait_both(_remote(got, got, send_sems.at[j], recv_sems.at[j], me))

    return _Flight(f"exchange{tag}", [buf], len(flips), issue, drain, thru)


def _share_flight(tag, fins, thru):
    n = len(fins)

    def issue(refs, send_sems, recv_sems):
        x, y, c, _ = _place()
        for k in range(n):
            _remote(refs[k].at[c], refs[k].at[c], send_sems.at[k], recv_sems.at[k], (x, y, 1 - c)).start()

    def drain(refs, send_sems, recv_sems):
        x, y, c, _ = _place()
        for k in range(n):
            got = refs[k].at[1 - c]
            _wait_both(_remote(got, got, send_sems.at[k], recv_sems.at[k], (x, y, c)))

    return _Flight(f"share{tag}", fins, n, issue, drain, thru)


def _pair_blocks(w):
    h, d, _ = w.shape
    z = jnp.zeros((h // 2, d, d), w.dtype)
    return jnp.concatenate([jnp.concatenate([w[0::2], z], axis=2), jnp.concatenate([z, w[1::2]], axis=2)], axis=1)


def _unpair_blocks(b):
    n, dd, _ = b.shape
    d = dd // 2
    return jnp.stack([b[:, :d, :d], b[:, d:, d:]], axis=1).reshape(2 * n, d, d)


def _pad_rows(a, rows):
    return jnp.pad(a, ((0, rows - a.shape[0]), (0, 0)))


class _Packer:
    def __init__(self, shapes, width=1024, row_multiple=64):
        self.shapes = shapes
        self.sizes = [math.prod(s) for s in shapes]
        total = sum(self.sizes)
        self.width = width
        self.rows = -(-total // (width * row_multiple)) * row_multiple
        self.pad = self.rows * width - total

    def pack(self, arrays):
        flat = jnp.concatenate([a.reshape(-1).astype(F32) for a in arrays] + [jnp.zeros((self.pad,), F32)])
        return flat.reshape(self.rows, self.width)

    def unpack(self, packed):
        flat = packed.reshape(-1)
        out, off = [], 0
        for s, n in zip(self.shapes, self.sizes):
            out.append(flat[off:off + n].reshape(s))
            off += n
        return out


SMALL = ["b_ada", "ffn1_norm", "mix_norm", "conv_w", "conv_b", "gate_a_w", "gate_a_b", "gate_x_w", "gate_x_b", "lru_lambda",
         "v_norm", "spatial_w", "spatial_b", "lru_out_norm", "gmlp_out_norm", "ffn2_norm", "final_norm"]
BIG = ["ffn1_w_gu", "ffn1_w_down", "w_in", "w_out", "ffn2_w_gu", "ffn2_w_down"]
GROUPS = (("ffn1_w_gu", "ffn1_w_down"), ("w_in", "w_out"), ("ffn2_w_gu", "ffn2_w_down"))
FWD_GROUPS = (("ffn1_w_gu",), ("ffn1_w_down",), ("w_in", "w_out"), ("ffn2_w_gu",), ("ffn2_w_down",))
MIN_AGE = {"swap": 1, "scatter": 1, "share": 1}
WEIGHTS = ["w_ada", "b_ada", "ffn1_norm", "ffn1_w_gu", "ffn1_w_down", "mix_norm", "w_in", "conv_w", "conv_b", "gate_a_w", "gate_a_b",
           "gate_x_w", "gate_x_b", "lru_lambda", "v_norm", "spatial_w", "spatial_b", "lru_out_norm", "gmlp_out_norm", "w_out",
           "ffn2_norm", "ffn2_w_gu", "ffn2_w_down", "final_norm"]


def kernel(x, c, w_ada, b_ada, ffn1_norm, ffn1_w_gu, ffn1_w_down, mix_norm, w_in, conv_w, conv_b, gate_a_w, gate_a_b, gate_x_w, gate_x_b, lru_lambda, v_norm, spatial_w, spatial_b, lru_out_norm, gmlp_out_norm, w_out, ffn2_norm, ffn2_w_gu, ffn2_w_down, final_norm, loss_target, m_w_ada, m_b_ada, m_ffn1_norm, m_ffn1_w_gu, m_ffn1_w_down, m_mix_norm, m_w_in, m_conv_w, m_conv_b, m_gate_a_w, m_gate_a_b, m_gate_x_w, m_gate_x_b, m_lru_lambda, m_v_norm, m_spatial_w, m_spatial_b, m_lru_out_norm, m_gmlp_out_norm, m_w_out, m_ffn2_norm, m_ffn2_w_gu, m_ffn2_w_down, m_final_norm, v_w_ada, v_b_ada, v_ffn1_norm, v_ffn1_w_gu, v_ffn1_w_down, v_mix_norm, v_w_in, v_conv_w, v_conv_b, v_gate_a_w, v_gate_a_b, v_gate_x_w, v_gate_x_b, v_lru_lambda, v_v_norm, v_spatial_w, v_spatial_b, v_lru_out_norm, v_gmlp_out_norm, v_w_out, v_ffn2_norm, v_ffn2_w_gu, v_ffn2_w_down, v_final_norm):
    given = dict(locals())
    W = {n: given[n] for n in WEIGHTS}
    L = w_ada.shape[0]
    S, D = x.shape[1], x.shape[2]
    LW = conv_b.shape[1]
    hd = LW // HEADS
    xi, yi, ci = lax.axis_index("x"), lax.axis_index("y"), lax.axis_index("c")
    chip = 2 * xi + yi
    dev = 2 * chip + ci
    place = jnp.stack([ci, chip]).astype(jnp.int32)
    xs = x.reshape(S, D)
    tgt = loss_target.reshape(S, D)

    n_ada = w_ada.shape[2]
    cws = LW // N_CHIPS

    def half_view(s):
        return s.reshape(N_CHIPS, 2, s.shape[1] // 2, s.shape[2])

    stages = [(l, names) for l in range(L) for names in FWD_GROUPS]
    seq = [[half_view(_cast_into_slot(W[n], l, place, place)) for n in names] for l, names in stages[:1]]
    flights = {}

    def ici_plan(t):
        if t >= len(stages):
            return []
        stage = t % len(FWD_GROUPS)
        return {2: [(t, 0, 1), (t + 1, 0, 2)], 3: [(t, 1, 2)]}.get(stage, [(t, 0, 1)])

    def launch(t, thru, direct=()):
        ici = [(a, part, parts) for g, part, parts in ici_plan(t) for a in seq[g]]
        d2d = seq[t - 1] if 1 <= t <= len(seq) else []
        if ici or d2d or direct:
            flights[t] = _gather_flight(t, ici, d2d, list(direct), thru)
            thru = flights[t].thru
        return thru

    def land(t, after):
        if t not in flights:
            return []
        out = flights.pop(t).land(after)
        for g, _, _ in ici_plan(t):
            seq[g], out = out[:len(seq[g])], out[len(seq[g]):]
        if 1 <= t <= len(seq):
            seq[t - 1], out = out[:len(seq[t - 1])], out[len(seq[t - 1]):]
        return out

    def group_weights(t):
        return [s.reshape(N_CHIPS, -1, s.shape[3]) for s in seq[t]]

    c_rows = _pad_rows(c, SUBLANES)
    c_flight = _exchange_flight("c", lax.dynamic_update_index_in_dim(jnp.zeros((N_DEV,) + c_rows.shape, F32), c_rows, dev, 0), c_rows)
    started = launch(0, c_flight.thru)
    c_all = c_flight.land(started)[0][:, 0, :]
    b_shard = lax.dynamic_slice_in_dim(b_ada, chip * n_ada, n_ada, axis=1)
    mod_shard = _ada_fwd(_pad_rows(c_all, 2 * SUBLANES), w_ada, b_shard[:, None, :])
    seq += [[half_view(_cast_into_slot(W[n], l, place, started)) for n in names] for l, names in stages[1:]]

    def in_slot(block):
        return lax.dynamic_update_index_in_dim(jnp.zeros((N_CHIPS,) + block.shape, block.dtype), block, chip, 0)

    land(0, seq[-1][-1])
    small = [in_slot(mod_shard.reshape(L * 2 * SUBLANES, n_ada)), in_slot(conv_w.reshape(L * CONV_WIDTH, cws))]
    mod_all, conv_all = land(1, launch(1, mod_shard, small))
    mod_rows = lax.dynamic_index_in_dim(mod_all.reshape(N_CHIPS, L, 2 * SUBLANES, n_ada), dev, axis=2, keepdims=False)
    mod = mod_rows.transpose(1, 0, 2).reshape(L, N_MOD, 1, D)
    conv_full = conv_all.reshape(N_CHIPS, L, CONV_WIDTH, cws).transpose(1, 2, 0, 3).reshape(L, CONV_WIDTH, LW)

    tril = jnp.tril(jnp.ones((CHUNK, CHUNK), F32))
    seg = (jnp.arange(LW)[:, None] // hd == jnp.arange(LW)[None, :] // hd).astype(jnp.bfloat16)

    def mixer_params(l):
        ws = spatial_w[l] * tril
        wsp = jnp.concatenate([ws[0::2], ws[1::2]], axis=2)
        wa, wx = _pair_blocks(gate_a_w[l]), _pair_blocks(gate_x_w[l])
        return dict(
            cw=conv_full[l], cb=conv_b[l][None],
            wa=wa.astype(MXU_DTYPE), wx=wx.astype(MXU_DTYPE), wat=wa.transpose(0, 2, 1).astype(MXU_DTYPE), wxt=wx.transpose(0, 2, 1).astype(MXU_DTYPE),
            ba=gate_a_b[l].reshape(1, LW), bx=gate_x_b[l].reshape(1, LW), lam=lru_lambda[l][None], gv=v_norm[l][None],
            wsp=wsp.astype(MXU_DTYPE), wspt=wsp.transpose(0, 2, 1).astype(MXU_DTYPE),
            bfull=jnp.repeat(spatial_b[l].T, hd, axis=1), g_lru=lru_out_norm[l][None], g_gm=gmlp_out_norm[l][None])

    saved = []
    xcur = xs
    zero_row = jnp.zeros((1, D), F32)
    h = _modnorm(xcur, ffn1_norm[0][None], mod[0][0], mod[0][1])
    for l in range(L):
        mp, md = mixer_params(l), mod[l]
        s = dict(lw={}, mp=mp, md=md)
        lw = s["lw"]
        t = len(FWD_GROUPS) * l
        s["x0"] = xcur
        s["h1"] = launch(t + 2, h)
        lw["gu1"], = group_weights(t)
        s["a1"], s["gu1"] = _ffn_up(s["h1"], lw["gu1"])
        land(t + 2, s["a1"])
        s["a1"] = launch(t + 3, s["a1"])
        lw["d1"] = group_weights(t + 1)[0].reshape(-1, D)
        s["f1"], xcur, h = _mm_res(s["a1"], lw["d1"], xcur, md[2], 0.5, (mix_norm[l][None], md[3], md[4]))
        land(t + 3, xcur)
        s["x1"] = xcur
        s["h2"] = launch(t + 4, h)
        lw["win"], wout = group_weights(t + 2)
        lw["wout"] = wout.reshape(-1, D)
        s["proj"] = _mm_chunks(s["h2"], lw["win"])
        s["ylru"] = _lru_fwd(s["proj"], mp["cw"], mp["cb"], mp["wa"], mp["ba"], mp["wx"], mp["bx"], mp["lam"])
        s["yn"], s["ygm"] = _gmlp_fwd(s["proj"], s["ylru"], mp["gv"], seg, mp["wsp"], mp["bfull"], mp["g_lru"], mp["g_gm"])
        s["f2"], xcur, h = _mm_res(s["yn"], lw["wout"], xcur, md[5], 1.0, (ffn2_norm[l][None], md[6], md[7]))
        land(t + 4, xcur)
        s["x2"] = xcur
        s["h3"] = launch(t + 5, h)
        lw["gu2"], = group_weights(t + 3)
        s["a3"], s["gu3"] = _ffn_up(s["h3"], lw["gu2"])
        land(t + 5, s["a3"])
        s["a3"] = launch(t + 6, s["a3"])
        lw["d2"] = group_weights(t + 4)[0].reshape(-1, D)
        following = (ffn1_norm[l + 1][None], mod[l + 1][0], mod[l + 1][1]) if l + 1 < L else (final_norm[None], zero_row, zero_row)
        s["f3"], xcur, h = _mm_res(s["a3"], lw["d2"], xcur, md[8], 0.5, following)
        land(t + 6, xcur)
        saved.append(s)

    dx, dq, head_acc = _loss_head(xcur, tgt, final_norm[None], saved[-1]["md"][8], 0.5)
    loss = lax.psum(jnp.sum(head_acc[1]), ("x", "y", "c"))
    big_grads = {n: [None] * L for n in BIG}
    dmods = [None] * L

    def ffn_bwd(names, l, dx, dq, x_in, h, a, gu, f, wgu, wd, gn, sc, next_gate, next_scale):
        big_grads[names[1]][l] = _mm_tn_chunks(a, dq[None], 1408, 1024)[0].reshape(N_CHIPS, -1, D)
        dgu = _ffn_bwd_act(dq, wd, gu)
        C = dgu.shape[3]
        dgu4 = dgu.reshape(N_CHIPS, S, C)
        big_grads[names[0]][l] = _mm_tn_chunks(h, dgu4, 1024, C)
        dgu4 = reduce_group(names, l, big_grads[names[0]][l], dgu4)
        dx, dq, acc = _mm_nt_norm_bwd(dgu4, wgu, x_in, dx, f, gn, sc, 0.5, next_gate, next_scale)
        return dx, move_on(dx, dq), acc

    stepped = {n: None for n in BIG}
    reducing = []

    per_layer = [n for n in SMALL if n not in ("b_ada", "final_norm")]
    packers, exchanges = {}, {}
    clock = [0]
    to_step = []

    def step_reduced(after):
        while to_step:
            name, l, g = to_step.pop(0)
            stepped[name] = _adamw_layer(W[name], g, given["m_" + name], given["v_" + name], l, stepped[name], after)
            after = stepped[name][1]
        return after

    def move_on(after, thru, force=False):
        clock[0] += 1
        for grp in list(reducing):
            if not force and clock[0] - grp["since"] < MIN_AGE[grp["step"]]:
                continue
            grp["since"] = clock[0]
            landed = grp["flight"].land(after)
            n = len(grp["names"])
            if grp["step"] == "swap":
                pairs = [_add_half(g4, r1, place) for g4, r1 in zip(landed[:n], landed[n:])]
                grp.update(step="scatter", flight=_scatter_flight(grp["tag"], [h for h, _ in pairs], [own for _, own in pairs], thru))
            elif grp["step"] == "scatter":
                grp.update(step="share", flight=_share_flight(grp["tag"], [_sum4_into_half(r2, place) for r2 in landed[n:]], thru))
            else:
                to_step.extend((name, grp["l"], fin.reshape(2 * fin.shape[1], fin.shape[2])) for name, fin in zip(grp["names"], landed))
                reducing.remove(grp)
                continue
            thru = grp["flight"].thru
        return thru

    def reduce_group(names, l, after, thru):
        thru = move_on(after, thru)
        g4s = [big_grads[n][l].reshape(N_CHIPS, 2, big_grads[n][l].shape[1] // 2, big_grads[n][l].shape[2]) for n in names]
        tag = f"{l}{GROUPS.index(names)}"
        reducing.append(dict(names=names, l=l, tag=tag, step="swap", since=clock[0], flight=_swap_flight(tag, g4s, thru)))
        return reducing[-1]["flight"].thru
    for l in reversed(range(L)):
        s = saved[l]
        lw, mp, md = s["lw"], s["mp"], s["md"]
        dx, dq, acc3 = ffn_bwd(
            GROUPS[2], l, dx, dq, s["x2"], s["h3"], s["a3"], s["gu3"], s["f3"], lw["gu2"], lw["d2"], ffn2_norm[l][None], md[7], md[5], 1.0)
        big_grads["w_out"][l] = _mm_tn_chunks(s["yn"], dq[None], 1024, 1024)[0].reshape(N_CHIPS, -1, D)
        dyn = _mm_nt_chunks(dq[None], lw["wout"][None])
        dylru, duv, dwsp, dbfull, gacc = _gmlp_bwd(s["proj"], s["ylru"], s["ygm"], dyn, mp["gv"], seg, mp["wsp"], mp["wspt"], mp["bfull"], mp["g_lru"], mp["g_gm"])
        dxg, dwa, dwx, lvec = _lru_bwd(s["proj"], dylru, mp["cw"], mp["cb"], mp["wa"], mp["ba"], mp["wx"], mp["bx"], mp["lam"], mp["wat"], mp["wxt"])
        dproj = jnp.concatenate([dxg, duv], axis=0)
        big_grads["w_in"][l] = _mm_tn_together(s["h2"], dproj)
        dproj = reduce_group(GROUPS[1], l, big_grads["w_in"][l], dproj)
        dx, dq, acc2 = _mm_nt_norm_bwd(dproj, lw["win"], s["x1"], dx, s["f2"], mix_norm[l][None], md[4], 1.0, md[2], 0.5)
        dq = move_on(dx, dq)
        if l > 0:
            ng, ns = saved[l - 1]["md"][8], 0.5
        else:
            ng, ns = zero_row, 0.0
        dx, dq, acc1 = ffn_bwd(
            GROUPS[0], l, dx, dq, s["x0"], s["h1"], s["a1"], s["gu1"], s["f1"], lw["gu1"], lw["d1"], ffn1_norm[l][None], md[1], ng, ns)

        dmods[l] = jnp.concatenate([acc1[0:2], acc1[3:4], acc2[0:2], acc2[3:4], acc3[0:2], acc3[3:4]], axis=0)
        dws = jnp.stack([dwsp[:, :, :CHUNK], dwsp[:, :, CHUNK:]], axis=1).reshape(HEADS, CHUNK, CHUNK) * tril
        lg = {"ffn1_norm": acc1[2], "mix_norm": acc2[2], "ffn2_norm": acc3[2],
              "conv_w": lvec[4:8], "conv_b": lvec[3], "gate_a_w": _unpair_blocks(dwa), "gate_a_b": lvec[0].reshape(HEADS, hd),
              "gate_x_w": _unpair_blocks(dwx), "gate_x_b": lvec[1].reshape(HEADS, hd), "lru_lambda": lvec[2], "v_norm": gacc[2],
              "spatial_w": dws, "spatial_b": dbfull.reshape(CHUNK, HEADS, hd).sum(-1).T, "lru_out_norm": gacc[0], "gmlp_out_norm": gacc[1]}
        part = [lg[n] for n in per_layer] + [dmods[l]] + ([head_acc[0]] if l == L - 1 else [])
        packers[l] = _Packer([p.shape for p in part])
        packed = packers[l].pack(part)
        exchanges[l] = _exchange_flight(l, lax.dynamic_update_index_in_dim(jnp.zeros((N_DEV,) + packed.shape, F32), packed, dev, 0), dq)
        dq = exchanges[l].thru

    grad_x = dx.reshape(x.shape)

    done = step_reduced(dq)
    while reducing:
        dq = move_on(done, dq, force=True)
        done = step_reduced(dq)
    summed, dmod_rows = [], []
    for l in range(L):
        gathered, = exchanges[l].land(done)
        summed.append(packers[l].unpack(_sum_leading(gathered)))
        off = sum(packers[l].sizes[:len(per_layer)])
        dmod_rows.append(gathered.reshape(N_DEV, -1)[:, off:off + N_MOD * D])
    grads = {n: jnp.stack([summed[l][k] for l in range(L)]) for k, n in enumerate(per_layer)}
    grads["final_norm"] = summed[L - 1][len(per_layer) + 1]
    grads["b_ada"] = jnp.stack([summed[l][len(per_layer)].reshape(N_MOD * D) for l in range(L)])
    dmod_shard = lax.dynamic_slice_in_dim(jnp.stack(dmod_rows), chip * n_ada, n_ada, axis=2)
    stepped_ada = _ada_step(c_all.T, dmod_shard, w_ada, m_w_ada, v_w_ada)
    grads["w_ada"] = stepped_ada[0]
    grads["conv_w"] = lax.dynamic_slice_in_dim(grads["conv_w"], chip * cws, cws, axis=2)

    delta, new_m, new_v = {}, {}, {}
    for n in BIG:
        grads[n], delta[n], new_m[n], new_v[n] = stepped[n]
    delta["w_ada"], new_m["w_ada"], new_v["w_ada"] = stepped_ada[1:]
    def rows_of(a):
        return a.reshape(-1, a.shape[-1])

    stepped_small = _adamw_small(*[[rows_of(src[n].reshape(W[n].shape)) for n in SMALL]
                                   for src in (W, grads, {n: given["m_" + n] for n in SMALL}, {n: given["v_" + n] for n in SMALL})])
    for k, n in enumerate(SMALL):
        delta[n], new_m[n], new_v[n] = (stepped_small[i * len(SMALL) + k].reshape(W[n].shape) for i in range(3))
    grads = {n: grads[n].reshape(W[n].shape) for n in WEIGHTS}
    return (loss, grad_x, *[grads[n] for n in WEIGHTS], *[delta[n] for n in WEIGHTS], *[new_m[n] for n in WEIGHTS], *[new_v[n] for n in WEIGHTS])
```

```python
import math

import jax
import jax.numpy as jnp
from jax import lax
from jax.experimental import pallas as pl
from jax.experimental.pallas import tpu as pltpu

F32 = jnp.float32
MXU_DTYPE = jnp.bfloat16
ACT_DTYPE = jnp.bfloat16
XFER_DTYPE = jnp.bfloat16
EPS = 1e-6
RG_LRU_C = 8.0
N_MOD = 9
CONV_WIDTH = 4
HEADS = 8
CHUNK = 128
LANES = 128
SUBLANES = 8
N_CHIPS = 4
N_DEV = 8
ADAM_LR, ADAM_B1, ADAM_B2, ADAM_EPS, ADAM_WD, ADAM_STEP = 0.001, 0.9, 0.999, 1e-08, 0.01, 10
VMEM_LIMIT_BYTES = 60 * 1024 * 1024
ROW_TILE_BYTES = 3 << 20
GELU_C = math.sqrt(2.0 / math.pi)
GELU_A = 0.044715

ANY = pl.BlockSpec(memory_space=pl.ANY)
MESH = pl.DeviceIdType.MESH
SDS = jax.ShapeDtypeStruct


def _params(*sem):
    return pltpu.CompilerParams(dimension_semantics=sem, vmem_limit_bytes=VMEM_LIMIT_BYTES)


def _dot(a, b):
    return jnp.dot(a.astype(MXU_DTYPE), b.astype(MXU_DTYPE), preferred_element_type=F32)


def _dot_nt(a, b):
    return lax.dot_general(a.astype(MXU_DTYPE), b.astype(MXU_DTYPE), (((1,), (1,)), ((), ())), preferred_element_type=F32)


def _dot_tn(a, b):
    return lax.dot_general(a.astype(MXU_DTYPE), b.astype(MXU_DTYPE), (((0,), (0,)), ((), ())), preferred_element_type=F32)


def _gelu(x):
    return x * (0.5 * (1.0 + jnp.tanh(GELU_C * (x + GELU_A * (x * x * x)))))


def _gelu_grad(x):
    t = jnp.tanh(GELU_C * (x + GELU_A * (x * x * x)))
    return 0.5 * (1.0 + t) + 0.5 * x * (1.0 - t * t) * (GELU_C * (1.0 + 3.0 * GELU_A * x * x))


def _sigmoid(x):
    return jax.nn.sigmoid(x)


def _sigmoid_by_tanh(x):
    return 0.5 * jnp.tanh(0.5 * x) + 0.5


def _rsqrt_ms(x):
    return lax.rsqrt(jnp.mean(x * x, axis=-1, keepdims=True) + EPS)


def _rowsum(x):
    return jnp.sum(x, axis=0, keepdims=True)


def _tile(n, want):
    t = min(n, want)
    assert n % t == 0, (n, want)
    return t


def _row_tile(rows, row_bytes):
    step = 2 * SUBLANES
    cap = max(step, ROW_TILE_BYTES // row_bytes)
    best = None
    for t in range(step, min(rows, cap) + 1, step):
        if rows % t == 0:
            best = t
    assert best is not None, (rows, row_bytes)
    return best


def _modnorm(x, gn, sh, sc):
    S, D = x.shape
    tm = _tile(S, 1024)

    def body(x_ref, gn_ref, sh_ref, sc_ref, h_ref):
        xv = x_ref[...]
        h = (xv * _rsqrt_ms(xv) * gn_ref[...]) * (1.0 + sc_ref[...]) + sh_ref[...]
        h_ref[...] = h.astype(ACT_DTYPE)

    row = pl.BlockSpec((1, D), lambda i: (0, 0))
    return pl.pallas_call(
        body, name="modnorm", grid=(S // tm,),
        in_specs=[pl.BlockSpec((tm, D), lambda i: (i, 0)), row, row, row],
        out_specs=pl.BlockSpec((tm, D), lambda i: (i, 0)),
        out_shape=SDS((S, D), ACT_DTYPE), compiler_params=_params("parallel"),
    )(x, gn, sh, sc)


def _mm_nt_norm_bwd(ac, wc, x, dxo, f, gn, sc, res_scale, next_gate, next_scale):
    P, S, K = ac.shape
    D = x.shape[1]
    tm = _tile(S, 512)

    def body(a_ref, w_ref, x_ref, dxo_ref, f_ref, gn_ref, sc_ref, ng_ref, dx_ref, dq_ref, acc_ref):
        @pl.when(pl.program_id(0) == 0)
        def _():
            acc_ref[...] = jnp.zeros_like(acc_ref)

        dh = _dot_nt(a_ref[0], w_ref[0])
        for p in range(1, P):
            dh += _dot_nt(a_ref[p], w_ref[p])
        xv, dxo = x_ref[...], dxo_ref[...]
        r = _rsqrt_ms(xv)
        xhat = xv * r
        gn = gn_ref[...]
        dn = dh * (1.0 + sc_ref[...])
        dxh = dn * gn
        dx = dxo + r * (dxh - xhat * jnp.mean(dxh * xhat, axis=-1, keepdims=True))
        dx_ref[...] = dx
        dq_ref[...] = ((next_scale * ng_ref[...]) * dx).astype(ACT_DTYPE)
        acc_ref[0:1, :] += _rowsum(dh)
        acc_ref[1:2, :] += _rowsum(dh * (xhat * gn))
        acc_ref[2:3, :] += _rowsum(dn * xhat)
        acc_ref[3:4, :] += _rowsum((res_scale * f_ref[...]) * dxo)

    tile = pl.BlockSpec((tm, D), lambda i: (i, 0))
    row = pl.BlockSpec((1, D), lambda i: (0, 0))
    return pl.pallas_call(
        body, name=f"mm_nt_norm_bwd_k{K}", grid=(S // tm,),
        in_specs=[pl.BlockSpec((P, tm, K), lambda i: (0, i, 0)),
                  pl.BlockSpec((P, D, K), lambda i: (0, 0, 0), pipeline_mode=pl.Buffered(1)),
                  tile, tile, tile, row, row, row],
        out_specs=[tile, tile, pl.BlockSpec((SUBLANES, D), lambda i: (0, 0))],
        out_shape=[SDS((S, D), F32), SDS((S, D), ACT_DTYPE), SDS((SUBLANES, D), F32)],
        compiler_params=_params("arbitrary"),
    )(ac, wc, x, dxo, f, gn, sc, next_gate)


def _loss_head(x, target, gn, next_gate, next_scale):
    S, D = x.shape
    tm = _tile(S, 512)

    def body(x_ref, t_ref, gn_ref, ng_ref, dx_ref, dq_ref, acc_ref):
        @pl.when(pl.program_id(0) == 0)
        def _():
            acc_ref[...] = jnp.zeros_like(acc_ref)

        xv = x_ref[...]
        r = _rsqrt_ms(xv)
        xhat = xv * r
        gn = gn_ref[...]
        err = xhat * gn - t_ref[...]
        dy = err * (1.0 / D)
        dxh = dy * gn
        dx = r * (dxh - xhat * jnp.mean(dxh * xhat, axis=-1, keepdims=True))
        dx_ref[...] = dx
        dq_ref[...] = ((next_scale * ng_ref[...]) * dx).astype(ACT_DTYPE)
        acc_ref[0:1, :] += _rowsum(dy * xhat)
        acc_ref[1:2, :] += _rowsum(err * err) * (0.5 / D)

    tile = pl.BlockSpec((tm, D), lambda i: (i, 0))
    row = pl.BlockSpec((1, D), lambda i: (0, 0))
    return pl.pallas_call(
        body, name="loss_head", grid=(S // tm,),
        in_specs=[tile, tile, row, row],
        out_specs=[tile, tile, pl.BlockSpec((SUBLANES, D), lambda i: (0, 0))],
        out_shape=[SDS((S, D), F32), SDS((S, D), ACT_DTYPE), SDS((SUBLANES, D), F32)],
        compiler_params=_params("arbitrary"),
    )(x, target, gn, next_gate)


def _ffn_up(h, wgu):
    S, D = h.shape
    C = wgu.shape[2]
    tm = _tile(S, 512)

    def body(h_ref, wg_ref, wu_ref, a_ref, gu_ref):
        hv = h_ref[...]
        g = _dot(hv, wg_ref[...])
        u = _dot(hv, wu_ref[...])
        a_ref[...] = (g * _sigmoid_by_tanh(g) * u).astype(ACT_DTYPE)
        gu_ref[0] = g.astype(ACT_DTYPE)
        gu_ref[1] = u.astype(ACT_DTYPE)

    return pl.pallas_call(
        body, name="ffn_up", grid=(2, S // tm),
        in_specs=[
            pl.BlockSpec((tm, D), lambda j, i: (i, 0)),
            pl.BlockSpec((None, D, C), lambda j, i: (j, 0, 0)),
            pl.BlockSpec((None, D, C), lambda j, i: (2 + j, 0, 0)),
        ],
        out_specs=[
            pl.BlockSpec((tm, C), lambda j, i: (i, j)),
            pl.BlockSpec((2, None, tm, C), lambda j, i: (0, j, i, 0)),
        ],
        out_shape=[SDS((S, 2 * C), ACT_DTYPE), SDS((2, 2, S, C), ACT_DTYPE)],
        compiler_params=_params("parallel", "parallel"),
    )(h, wgu, wgu)


def _ffn_bwd_act(dq, wd, gu):
    S, D = dq.shape
    C = gu.shape[3]
    tm = _tile(S, 512)

    def body(dq_ref, wd_ref, gu_ref, dgu_ref):
        da = _dot_nt(dq_ref[...], wd_ref[...])
        g = gu_ref[0].astype(F32)
        u = gu_ref[1].astype(F32)
        s = _sigmoid_by_tanh(g)
        gs = g * s
        dgu_ref[0] = ((da * u) * (s + gs - gs * s)).astype(ACT_DTYPE)
        dgu_ref[1] = (da * gs).astype(ACT_DTYPE)

    gu_spec = pl.BlockSpec((2, None, tm, C), lambda j, i: (0, j, i, 0))
    return pl.pallas_call(
        body, name="ffn_bwd_act", grid=(2, S // tm),
        in_specs=[pl.BlockSpec((tm, D), lambda j, i: (i, 0)), pl.BlockSpec((C, D), lambda j, i: (j, 0)), gu_spec],
        out_specs=gu_spec,
        out_shape=SDS(gu.shape, ACT_DTYPE),
        compiler_params=_params("parallel", "parallel"),
    )(dq, wd, gu)


def _mm_res(a, w, x, gate, scale, following):
    S, K = a.shape
    D = w.shape[1]
    tm = _tile(S, 512)

    def body(a_ref, w_ref, x_ref, g_ref, gn_ref, sh_ref, sc_ref, f_ref, xo_ref, h_ref):
        f = _dot(a_ref[...], w_ref[...])
        f_ref[...] = f
        xo = x_ref[...] + (scale * g_ref[...]) * f
        xo_ref[...] = xo
        h_ref[...] = ((xo * _rsqrt_ms(xo) * gn_ref[...]) * (1.0 + sc_ref[...]) + sh_ref[...]).astype(ACT_DTYPE)

    tile = pl.BlockSpec((tm, D), lambda i: (i, 0))
    row = pl.BlockSpec((1, D), lambda i: (0, 0))
    return pl.pallas_call(
        body, name=f"mm_res_k{K}", grid=(S // tm,),
        in_specs=[pl.BlockSpec((tm, K), lambda i: (i, 0)), pl.BlockSpec((K, D), lambda i: (0, 0)), tile, row, row, row, row],
        out_specs=[tile, tile, tile],
        out_shape=[SDS((S, D), F32), SDS((S, D), F32), SDS((S, D), ACT_DTYPE)],
        compiler_params=_params("parallel"),
    )(a, w, x, gate, *following)


def _mm_chunks(h, wc):
    S, K = h.shape
    P, _, N = wc.shape
    tm = _tile(S, 512)

    def body(h_ref, w_ref, o_ref):
        hv = h_ref[...]
        for p in range(P):
            o_ref[:, p * N:(p + 1) * N] = _dot(hv, w_ref[p])

    return pl.pallas_call(
        body, name="mm_chunks", grid=(S // tm,),
        in_specs=[pl.BlockSpec((tm, K), lambda i: (i, 0)), pl.BlockSpec((P, K, N), lambda i: (0, 0, 0))],
        out_specs=pl.BlockSpec((tm, P * N), lambda i: (i, 0)),
        out_shape=SDS((S, P * N), F32),
        compiler_params=_params("parallel"),
    )(h, wc)


def _mm_nt_chunks(ac, wc):
    P, S, K = ac.shape
    N = wc.shape[1]
    tm, tn = _tile(S, 512), _tile(N, 1024)

    def body(a_ref, w_ref, o_ref):
        acc = _dot_nt(a_ref[0], w_ref[0])
        for p in range(1, P):
            acc += _dot_nt(a_ref[p], w_ref[p])
        o_ref[...] = acc

    return pl.pallas_call(
        body, name=f"mm_nt_p{P}k{K}", grid=(S // tm, N // tn),
        in_specs=[pl.BlockSpec((P, tm, K), lambda i, j: (0, i, 0)), pl.BlockSpec((P, tn, K), lambda i, j: (0, j, 0))],
        out_specs=pl.BlockSpec((tm, tn), lambda i, j: (i, j)),
        out_shape=SDS((S, N), F32),
        compiler_params=_params("parallel", "parallel"),
    )(ac, wc)


def _mm_tn_chunks(a, bc, tile_m, tile_n):
    S, M = a.shape
    P, _, N = bc.shape
    ts, tm, tn = _tile(S, 2048), _tile(M, tile_m), _tile(N, tile_n)

    def body(a_ref, b_ref, o_ref):
        @pl.when(pl.program_id(3) == 0)
        def _():
            o_ref[...] = jnp.zeros_like(o_ref)

        o_ref[...] += _dot_tn(a_ref[...], b_ref[...])

    return pl.pallas_call(
        body, name=f"mm_tn_m{M}n{N}", grid=(P, M // tm, N // tn, S // ts),
        in_specs=[pl.BlockSpec((ts, tm), lambda p, m, n, k: (k, m)), pl.BlockSpec((None, ts, tn), lambda p, m, n, k: (p, k, n))],
        out_specs=pl.BlockSpec((None, tm, tn), lambda p, m, n, k: (p, m, n)),
        out_shape=SDS((P, M, N), F32),
        compiler_params=_params("parallel", "parallel", "parallel", "arbitrary"),
    )(a, bc)


def _mm_tn_together(a, bc):
    S, M = a.shape
    P, _, N = bc.shape
    ts = _tile(S, 2048)

    def body(a_ref, b_ref, o_ref):
        @pl.when(pl.program_id(0) == 0)
        def _():
            o_ref[...] = jnp.zeros_like(o_ref)

        at = a_ref[...].T
        for p in range(P):
            o_ref[p] += jnp.dot(at, b_ref[p], preferred_element_type=F32)

    return pl.pallas_call(
        body, name=f"mm_tn_together_m{M}n{N}", grid=(S // ts,),
        in_specs=[pl.BlockSpec((ts, M), lambda k: (k, 0)), pl.BlockSpec((P, ts, N), lambda k: (0, k, 0))],
        out_specs=pl.BlockSpec((P, M, N), lambda k: (0, 0, 0)),
        out_shape=SDS((P, M, N), F32), compiler_params=_params("arbitrary"),
    )(a, bc)


def _shift_down(x, s, row, fill):
    return jnp.where(row >= s, pltpu.roll(x, s, 0), fill)


def _shift_up(x, s, row, fill):
    n = x.shape[0]
    return jnp.where(row < n - s, pltpu.roll(x, n - s, 0), fill)


def _scan(a, b, row, scratch, up):
    scr_a, scr_b, scr_c = scratch
    n = a.shape[0]
    g = n // SUBLANES
    in_group = row & (SUBLANES - 1)

    def steps(a, b, pos, size):
        s = 1
        while s < size:
            m = (pos + s < size) if up else (pos >= s)
            b = jnp.where(m, a, 0.0) * pltpu.roll(b, a.shape[0] - s if up else s, 0) + b
            a = jnp.where(m, a * pltpu.roll(a, a.shape[0] - s if up else s, 0), a)
            s *= 2
        return a, b

    a, b = steps(a, b, in_group, SUBLANES)
    scr_a[...] = a
    scr_b[...] = b
    edge = 0 if up else SUBLANES - 1
    at = scr_a[pl.ds(edge, g, stride=SUBLANES), :]
    bt = scr_b[pl.ds(edge, g, stride=SUBLANES), :]
    group = lax.broadcasted_iota(jnp.int32, at.shape, 0)
    _, state = steps(at, bt, group, g)
    carry = jnp.where((group + 1 < g) if up else (group >= 1), pltpu.roll(state, g - 1 if up else 1, 0), 0.0)
    for k in range(SUBLANES):
        scr_c[pl.ds(k, g, stride=SUBLANES), :] = carry
    return b + a * scr_c[...]


def _conv(xl, cw_ref, cb_ref, row):
    y = cb_ref[...] + _shift_down(xl, 3, row, 0.0) * cw_ref[0:1, :]
    y = y + _shift_down(xl, 2, row, 0.0) * cw_ref[1:2, :]
    y = y + _shift_down(xl, 1, row, 0.0) * cw_ref[2:3, :]
    return y + xl * cw_ref[3:4, :]


def _lru_gates(xc, wa_ref, ba_ref, wx_ref, bx_ref, lam_ref):
    ra = _sigmoid(_dot(xc, wa_ref[...]) + ba_ref[...])
    ri = _sigmoid(_dot(xc, wx_ref[...]) + bx_ref[...])
    ls = jax.nn.log_sigmoid(lam_ref[...])
    a = jnp.exp((RG_LRU_C * ra) * ls)
    mult = jnp.sqrt(1.0 - a * a)
    return ra, ri, ls, a, mult


def _lru_specs(S):
    col = lambda off: pl.BlockSpec((S, LANES), lambda j: (0, off + j))
    vec = pl.BlockSpec((1, LANES), lambda j: (0, j))
    blk = pl.BlockSpec((None, LANES, LANES), lambda j: (j, 0, 0))
    cw = pl.BlockSpec((CONV_WIDTH, LANES), lambda j: (0, j))
    return col, vec, blk, cw


def _lru_fwd(proj, cw, cb, wa, ba, wx, bx, lam):
    S = proj.shape[0]
    W = cb.shape[1]
    nb = W // LANES

    def body(xl_ref, gl_ref, cw_ref, cb_ref, wa_ref, ba_ref, wx_ref, bx_ref, lam_ref, y_ref, *scratch):
        row = lax.broadcasted_iota(jnp.int32, (S, LANES), 0)
        xc = _conv(xl_ref[...], cw_ref, cb_ref, row)
        _, ri, _, a, mult = _lru_gates(xc, wa_ref, ba_ref, wx_ref, bx_ref, lam_ref)
        h = _scan(a, mult * (ri * xc), row, scratch, up=False)
        y_ref[...] = h * _gelu(gl_ref[...])

    col, vec, blk, cws = _lru_specs(S)
    return pl.pallas_call(
        body, name="lru_fwd", grid=(nb,),
        in_specs=[col(0), col(nb), cws, vec, blk, vec, blk, vec, vec],
        out_specs=pl.BlockSpec((S, LANES), lambda j: (0, j)),
        out_shape=SDS((S, W), F32), scratch_shapes=[pltpu.VMEM((S, LANES), F32)] * 3, compiler_params=_params("parallel"),
    )(proj, proj, cw, cb, wa, ba, wx, bx, lam)


def _lru_bwd(proj, dy, cw, cb, wa, ba, wx, bx, lam, wat, wxt):
    S = proj.shape[0]
    W = cb.shape[1]
    nb = W // LANES

    def body(xl_ref, gl_ref, dy_ref, cw_ref, cb_ref, wa_ref, ba_ref, wx_ref, bx_ref, lam_ref, wat_ref, wxt_ref,
             dp_ref, dwa_ref, dwx_ref, vec_ref, *scratch):
        row = lax.broadcasted_iota(jnp.int32, (S, LANES), 0)
        xl = xl_ref[...]
        xc = _conv(xl, cw_ref, cb_ref, row)
        ra, ri, ls, a, mult = _lru_gates(xc, wa_ref, ba_ref, wx_ref, bx_ref, lam_ref)
        h = _scan(a, mult * (ri * xc), row, scratch, up=False)
        gl = gl_ref[...]
        dyv = dy_ref[...]
        dp_ref[1] = (dyv * h * _gelu_grad(gl)).astype(ACT_DTYPE)
        adj = _scan(_shift_up(a, 1, row, 0.0), dyv * _gelu(gl), row, scratch, up=True)
        da = adj * _shift_down(h, 1, row, 0.0)
        dmult = adj * (ri * xc)
        dlog_a = da * a - dmult * (a * a) / mult
        dra = dlog_a * (RG_LRU_C * ls)
        dpa = dra * ra * (1.0 - ra)
        dpi = (adj * mult * xc) * ri * (1.0 - ri)
        dxc = adj * mult * ri + _dot(dpa, wat_ref[...]) + _dot(dpi, wxt_ref[...])
        dwa_ref[...] = _dot_tn(xc, dpa)
        dwx_ref[...] = _dot_tn(xc, dpi)
        dxl = dxc * cw_ref[3:4, :]
        dxl = dxl + _shift_up(dxc, 1, row, 0.0) * cw_ref[2:3, :]
        dxl = dxl + _shift_up(dxc, 2, row, 0.0) * cw_ref[1:2, :]
        dxl = dxl + _shift_up(dxc, 3, row, 0.0) * cw_ref[0:1, :]
        dp_ref[0] = dxl.astype(ACT_DTYPE)
        vec_ref[...] = jnp.zeros_like(vec_ref)
        vec_ref[0:1, :] = _rowsum(dpa)
        vec_ref[1:2, :] = _rowsum(dpi)
        vec_ref[2:3, :] = _rowsum(dlog_a * (RG_LRU_C * ra)) * _sigmoid(-lam_ref[...])
        vec_ref[3:4, :] = _rowsum(dxc)
        vec_ref[4:5, :] = _rowsum(dxc * _shift_down(xl, 3, row, 0.0))
        vec_ref[5:6, :] = _rowsum(dxc * _shift_down(xl, 2, row, 0.0))
        vec_ref[6:7, :] = _rowsum(dxc * _shift_down(xl, 1, row, 0.0))
        vec_ref[7:8, :] = _rowsum(dxc * xl)

    col, vec, blk, cws = _lru_specs(S)
    return pl.pallas_call(
        body, name="lru_bwd", grid=(nb,),
        in_specs=[col(0), col(nb), col(0), cws, vec, blk, vec, blk, vec, vec, blk, blk],
        out_specs=[pl.BlockSpec((2, S, LANES), lambda j: (0, 0, j)), blk, blk, pl.BlockSpec((2 * SUBLANES, LANES), lambda j: (0, j))],
        out_shape=[SDS((2, S, W), ACT_DTYPE), SDS((nb, LANES, LANES), F32), SDS((nb, LANES, LANES), F32), SDS((2 * SUBLANES, W), F32)],
        scratch_shapes=[pltpu.VMEM((S, LANES), F32)] * 3, compiler_params=_params("parallel"),
    )(proj, proj, dy, cw, cb, wa, ba, wx, bx, lam, wat, wxt)


def _seg_mean(x, seg_ref, width):
    hi = x.astype(jnp.bfloat16)
    lo = (x - hi.astype(F32)).astype(jnp.bfloat16)
    ones = seg_ref[...]
    s = jnp.dot(hi, ones, preferred_element_type=F32) + jnp.dot(lo, ones, preferred_element_type=F32)
    return s * (1.0 / width)


def _gmlp_core(u_ref, v_ref, gv_ref, seg_ref, ws_ref, bfull_ref, z_scr, hd):
    tm, W = u_ref.shape
    lane = lax.broadcasted_iota(jnp.int32, (CHUNK, LANES), 1)
    ug = _gelu(u_ref[...])
    vg = _gelu(v_ref[...])
    cen = vg - _seg_mean(vg, seg_ref, hd)
    rstd = lax.rsqrt(_seg_mean(cen * cen, seg_ref, hd) + EPS)
    vhat = cen * rstd
    vh = vhat * gv_ref[...]
    vcats = {}
    for ci in range(tm // CHUNK):
        for p in range(W // LANES):
            blk = vh[ci * CHUNK:(ci + 1) * CHUNK, p * LANES:(p + 1) * LANES]
            vcat = jnp.concatenate([jnp.where(lane < hd, blk, 0.0), jnp.where(lane >= hd, blk, 0.0)], axis=0).astype(MXU_DTYPE)
            vcats[ci, p] = vcat
            z_scr[ci * CHUNK:(ci + 1) * CHUNK, p * LANES:(p + 1) * LANES] = (
                jnp.dot(ws_ref[p], vcat, preferred_element_type=F32) + bfull_ref[:, p * LANES:(p + 1) * LANES])
    return ug, vhat, rstd, vcats


def _gmlp_specs(tm, W, nb):
    rows = lambda off: pl.BlockSpec((tm, W), lambda i: (i, off))
    vec = pl.BlockSpec((1, W), lambda i: (0, 0))
    seg = pl.BlockSpec((W, W), lambda i: (0, 0))
    wsp = pl.BlockSpec((nb, CHUNK, 2 * CHUNK), lambda i: (0, 0, 0))
    bfull = pl.BlockSpec((CHUNK, W), lambda i: (0, 0))
    return rows, vec, seg, wsp, bfull


def _gmlp_fwd(proj, ylru, gv, seg, wsp, bfull, g_lru, g_gm):
    S, W = ylru.shape
    nb = W // LANES
    hd = W // HEADS
    tm = _tile(S, 512)

    def body(u_ref, v_ref, yl_ref, gv_ref, seg_ref, ws_ref, bfull_ref, gl_ref, gg_ref, yn_ref, ygm_ref, z_scr):
        ug, _, _, _ = _gmlp_core(u_ref, v_ref, gv_ref, seg_ref, ws_ref, bfull_ref, z_scr, hd)
        ygm = ug * z_scr[...]
        ygm_ref[...] = ygm
        yl = yl_ref[...]
        yn_ref[:, 0:W] = (yl * _rsqrt_ms(yl) * gl_ref[...]).astype(ACT_DTYPE)
        yn_ref[:, W:2 * W] = (ygm * _rsqrt_ms(ygm) * gg_ref[...]).astype(ACT_DTYPE)

    rows, vec, segs, wsps, bfulls = _gmlp_specs(tm, W, nb)
    return pl.pallas_call(
        body, name="gmlp_fwd", grid=(S // tm,),
        in_specs=[rows(2), rows(3), rows(0), vec, segs, wsps, bfulls, vec, vec],
        out_specs=[pl.BlockSpec((tm, 2 * W), lambda i: (i, 0)), rows(0)],
        out_shape=[SDS((S, 2 * W), ACT_DTYPE), SDS((S, W), F32)],
        scratch_shapes=[pltpu.VMEM((tm, W), F32)],
        compiler_params=_params("parallel"),
    )(proj, proj, ylru, gv, seg, wsp, bfull, g_lru, g_gm)


def _rms_bwd(y, g, dyn):
    r = _rsqrt_ms(y)
    yhat = y * r
    dyh = dyn * g
    return r * (dyh - yhat * jnp.mean(dyh * yhat, axis=-1, keepdims=True)), _rowsum(dyn * yhat)


def _gmlp_bwd(proj, ylru, ygm, dyn, gv, seg, wsp, wspt, bfull, g_lru, g_gm):
    S, W = ylru.shape
    nb = W // LANES
    hd = W // HEADS
    tm = _tile(S, 256)

    def body(u_ref, v_ref, yl_ref, ygm_ref, dl_ref, dg_ref, gv_ref, seg_ref, ws_ref, wst_ref, bfull_ref, gl_ref, gg_ref,
             dyl_ref, duv_ref, dws_ref, dbf_ref, acc_ref, z_scr, dvh_scr):
        @pl.when(pl.program_id(0) == 0)
        def _():
            dws_ref[...] = jnp.zeros_like(dws_ref)
            dbf_ref[...] = jnp.zeros_like(dbf_ref)
            acc_ref[...] = jnp.zeros_like(acc_ref)

        dyl, dgl = _rms_bwd(yl_ref[...], gl_ref[...], dl_ref[...])
        dyl_ref[...] = dyl
        dygm, dgg = _rms_bwd(ygm_ref[...], gg_ref[...], dg_ref[...])
        ug, vhat, rstd, vcats = _gmlp_core(u_ref, v_ref, gv_ref, seg_ref, ws_ref, bfull_ref, z_scr, hd)
        duv_ref[0] = (dygm * z_scr[...] * _gelu_grad(u_ref[...])).astype(ACT_DTYPE)
        dz = dygm * ug
        lane = lax.broadcasted_iota(jnp.int32, (CHUNK, LANES), 1)
        dbf = dz[0:CHUNK, :]
        for ci in range(1, tm // CHUNK):
            dbf += dz[ci * CHUNK:(ci + 1) * CHUNK, :]
        dbf_ref[...] += dbf
        for ci in range(tm // CHUNK):
            for p in range(nb):
                dzb = dz[ci * CHUNK:(ci + 1) * CHUNK, p * LANES:(p + 1) * LANES].astype(MXU_DTYPE)
                dws_ref[p] += _dot_nt(dzb, vcats[ci, p])
                dvc = jnp.dot(wst_ref[p], dzb, preferred_element_type=F32)
                dvh_scr[ci * CHUNK:(ci + 1) * CHUNK, p * LANES:(p + 1) * LANES] = jnp.where(lane < hd, dvc[0:CHUNK], dvc[CHUNK:2 * CHUNK])
        dvh = dvh_scr[...]
        dvn = dvh * gv_ref[...]
        dvg = rstd * (dvn - _seg_mean(dvn, seg_ref, hd) - vhat * _seg_mean(dvn * vhat, seg_ref, hd))
        duv_ref[1] = (dvg * _gelu_grad(v_ref[...])).astype(ACT_DTYPE)
        acc_ref[0:1, :] += dgl
        acc_ref[1:2, :] += dgg
        acc_ref[2:3, :] += _rowsum(dvh * vhat)

    rows, vec, segs, wsps, bfulls = _gmlp_specs(tm, W, nb)
    wspt_spec = pl.BlockSpec((nb, 2 * CHUNK, CHUNK), lambda i: (0, 0, 0))
    return pl.pallas_call(
        body, name="gmlp_bwd", grid=(S // tm,),
        in_specs=[rows(2), rows(3), rows(0), rows(0), rows(0), rows(1), vec, segs, wsps, wspt_spec, bfulls, vec, vec],
        out_specs=[rows(0), pl.BlockSpec((2, tm, W), lambda i: (0, i, 0)), wsps, bfulls, pl.BlockSpec((SUBLANES, W), lambda i: (0, 0))],
        out_shape=[SDS((S, W), F32), SDS((2, S, W), ACT_DTYPE), SDS((nb, CHUNK, 2 * CHUNK), F32), SDS((CHUNK, W), F32), SDS((SUBLANES, W), F32)],
        scratch_shapes=[pltpu.VMEM((tm, W), F32), pltpu.VMEM((tm, W), F32)],
        compiler_params=_params("arbitrary"),
    )(proj, proj, ylru, ygm, dyn, dyn, gv, seg, wsp, wspt, bfull, g_lru, g_gm)


def _ada_fwd(c_all, w_ada, b_shard):
    L, D, N = w_ada.shape
    R = c_all.shape[0]
    tn = N // 2

    def body(c_ref, w_ref, b_ref, o_ref):
        cv = c_ref[...]
        o_ref[...] = _dot(cv * _sigmoid(cv), w_ref[...]) + b_ref[...]

    return pl.pallas_call(
        body, name="ada_fwd", grid=(L, N // tn),
        in_specs=[pl.BlockSpec((R, D), lambda l, j: (0, 0)), pl.BlockSpec((None, D, tn), lambda l, j: (l, 0, j)),
                  pl.BlockSpec((None, 1, tn), lambda l, j: (l, 0, j))],
        out_specs=pl.BlockSpec((None, R, tn), lambda l, j: (l, 0, j)),
        out_shape=SDS((L, R, N), F32), compiler_params=_params("parallel", "parallel"),
    )(c_all, w_ada, b_shard)


def _ada_step(c_all_t, dmod, w, m, v):
    D, B = c_all_t.shape
    L, _, N = dmod.shape
    tn = 3 * LANES
    assert N % tn == 0, N

    def body(c_ref, d_ref, w_ref, m_ref, v_ref, g_ref, do_ref, mo_ref, vo_ref):
        cv = c_ref[...]
        sc = cv * _sigmoid(cv)
        g = sc[:, 0:1] * d_ref[0:1, :]
        for b in range(1, B):
            g += sc[:, b:b + 1] * d_ref[b:b + 1, :]
        g_ref[...] = g
        do_ref[...], mo_ref[...], vo_ref[...] = _adam_math(w_ref[...], g, m_ref[...], v_ref[...])

    tile = pl.BlockSpec((None, D, tn), lambda l, j: (l, 0, j))
    return pl.pallas_call(
        body, name="ada_step", grid=(L, N // tn),
        in_specs=[pl.BlockSpec((D, B), lambda l, j: (0, 0)), pl.BlockSpec((None, B, tn), lambda l, j: (l, 0, j)), tile, tile, tile],
        out_specs=[tile] * 4, out_shape=[SDS((L, D, N), F32)] * 4, compiler_params=_params("parallel", "parallel"),
    )(c_all_t, dmod, w, m, v)


def _adam_math(w, g, m, v):
    mn = ADAM_B1 * m + (1.0 - ADAM_B1) * g
    vn = ADAM_B2 * v + (1.0 - ADAM_B2) * (g * g)
    m_hat = mn / (1.0 - ADAM_B1 ** ADAM_STEP)
    v_hat = vn / (1.0 - ADAM_B2 ** ADAM_STEP)
    return -ADAM_LR * (m_hat / (jnp.sqrt(v_hat) + ADAM_EPS) + ADAM_WD * w), mn, vn


def _adamw_layer(w, g, m, v, l, prev, after):
    L, R, C = w.shape
    tr = _row_tile(R, C * 4)
    prev = (after,) + (() if prev is None else tuple(prev))

    def body(w_ref, g_ref, m_ref, v_ref, *rest):
        go_ref, d_ref, mo_ref, vo_ref = rest[len(prev):]
        gv = g_ref[...]
        go_ref[...] = gv
        d_ref[...], mo_ref[...], vo_ref[...] = _adam_math(w_ref[...], gv, m_ref[...], v_ref[...])

    lay = pl.BlockSpec((None, tr, C), lambda i: (l, i, 0))
    return pl.pallas_call(
        body, name=f"adamw_layer_r{R}c{C}", grid=(R // tr,),
        in_specs=[lay, pl.BlockSpec((tr, C), lambda i: (i, 0)), lay, lay] + [ANY] * len(prev), out_specs=[lay] * 4,
        out_shape=[SDS((L, R, C), F32)] * 4, input_output_aliases={5 + k: k for k in range(len(prev) - 1)},
        compiler_params=_params("parallel"),
    )(w, g, m, v, *prev)


def _adamw_small(ws, gs, ms, vs):
    n = len(ws)

    def body(*refs):
        outs = refs[4 * n:]
        for k in range(n):
            outs[k][...], outs[n + k][...], outs[2 * n + k][...] = _adam_math(
                refs[k][...], refs[n + k][...], refs[2 * n + k][...], refs[3 * n + k][...])

    whole = pl.BlockSpec(memory_space=pltpu.VMEM)
    return pl.pallas_call(
        body, name="adamw_small", in_specs=[whole] * (4 * n), out_specs=[whole] * (3 * n),
        out_shape=[SDS(w.shape, F32) for w in ws] * 3, compiler_params=pltpu.CompilerParams(vmem_limit_bytes=VMEM_LIMIT_BYTES),
    )(*ws, *gs, *ms, *vs)


def _sum_leading(a):
    P, R, C = a.shape
    tr = _row_tile(R, P * C * 4)

    def body(a_ref, o_ref):
        acc = a_ref[0]
        for p in range(1, P):
            acc = acc + a_ref[p]
        o_ref[...] = acc

    return pl.pallas_call(
        body, name=f"sum{P}_r{R}c{C}", grid=(R // tr,),
        in_specs=[pl.BlockSpec((P, tr, C), lambda i: (0, i, 0))],
        out_specs=pl.BlockSpec((tr, C), lambda i: (i, 0)),
        out_shape=SDS((R, C), F32), compiler_params=_params("parallel"),
    )(a)


def _add_half(g4, r1, place):
    _, _, R, C = g4.shape
    tr = _row_tile(R, C * 4)

    def body(place_ref, g_ref, r_ref, h_ref, own_ref):
        s = (g_ref[...] + r_ref[...]).astype(XFER_DTYPE)
        h_ref[...] = s

        @pl.when(pl.program_id(1) == place_ref[1])
        def _():
            own_ref[...] = s

    return pl.pallas_call(
        body, name=f"add_half_r{R}c{C}",
        grid_spec=pltpu.PrefetchScalarGridSpec(
            num_scalar_prefetch=1, grid=(R // tr, N_CHIPS),
            in_specs=[pl.BlockSpec((None, None, tr, C), lambda i, p, place_ref: (p, place_ref[0], i, 0)),
                      pl.BlockSpec((None, tr, C), lambda i, p, place_ref: (p, i, 0))],
            out_specs=[pl.BlockSpec((None, tr, C), lambda i, p, place_ref: (p, i, 0)),
                       pl.BlockSpec((None, tr, C), lambda i, p, place_ref: (place_ref[1], i, 0))],
        ),
        out_shape=[SDS((N_CHIPS, R, C), XFER_DTYPE)] * 2, compiler_params=_params("parallel", "arbitrary"),
    )(place, g4, r1)


def _sum4_into_half(r2, place):
    P, R, C = r2.shape
    tr = _row_tile(R, P * C * 4)

    def body(place_ref, a_ref, o_ref):
        acc = a_ref[0].astype(F32)
        for p in range(1, P):
            acc = acc + a_ref[p].astype(F32)
        o_ref[...] = acc

    return pl.pallas_call(
        body, name=f"sum4_r{R}c{C}",
        grid_spec=pltpu.PrefetchScalarGridSpec(
            num_scalar_prefetch=1, grid=(R // tr,),
            in_specs=[pl.BlockSpec((P, tr, C), lambda i, place_ref: (0, i, 0))],
            out_specs=pl.BlockSpec((None, tr, C), lambda i, place_ref: (place_ref[0], i, 0)),
        ),
        out_shape=SDS((2, R, C), F32), compiler_params=_params("parallel"),
    )(place, r2)


def _cast_into_slot(w, l, place, after):
    _, R, C = w.shape
    tr = _row_tile(R, C * 4)

    def body(place_ref, w_ref, after_ref, o_ref):
        o_ref[...] = w_ref[...].astype(MXU_DTYPE)

    return pl.pallas_call(
        body, name=f"cast_r{R}c{C}",
        grid_spec=pltpu.PrefetchScalarGridSpec(
            num_scalar_prefetch=1, grid=(R // tr,),
            in_specs=[pl.BlockSpec((None, tr, C), lambda i, place_ref: (l, i, 0)), ANY],
            out_specs=pl.BlockSpec((None, tr, C), lambda i, place_ref: (place_ref[1], i, 0)),
        ),
        out_shape=SDS((N_CHIPS, R, C), MXU_DTYPE), compiler_params=_params("parallel"),
    )(place, w, after)


def _place():
    x, y, c = lax.axis_index("x"), lax.axis_index("y"), lax.axis_index("c")
    chips = [(1 - x, y), (x, 1 - y), (1 - x, 1 - y)]
    return x, y, c, chips


def _remote(src, dst, send_sem, recv_sem, to):
    return pltpu.make_async_remote_copy(src_ref=src, dst_ref=dst, send_sem=send_sem, recv_sem=recv_sem, device_id=to, device_id_type=MESH)


HBM_SPEC = pl.BlockSpec(memory_space=pltpu.HBM)
SEM_SPEC = pl.BlockSpec(memory_space=pltpu.SEMAPHORE)
DATAFLOW = pltpu.SideEffectType.DATAFLOW_SIDE_EFFECTING


def _in_hbm(a):
    return pltpu.with_memory_space_constraint(a, pltpu.HBM)


def _hbm_like(a):
    return pltpu.HBM(a.shape, a.dtype)


def _split_start(name, arrays, n_sems, issue, extra=()):
    m = len(arrays)

    def body(*refs):
        issue(refs[:m], refs[m + len(extra)], refs[m + len(extra) + 1])

    out = pl.pallas_call(
        body, name=name,
        out_shape=(pltpu.SemaphoreType.DMA((n_sems,)), pltpu.SemaphoreType.DMA((n_sems,)), *[_hbm_like(a) for a in arrays]),
        in_specs=[HBM_SPEC] * m + [ANY] * len(extra), out_specs=(SEM_SPEC, SEM_SPEC, *[HBM_SPEC] * m),
        input_output_aliases={k: 2 + k for k in range(m)},
        compiler_params=pltpu.CompilerParams(has_side_effects=DATAFLOW),
    )(*[_in_hbm(a) for a in arrays], *extra)
    return out[0], out[1], list(out[2:])


def _split_wait(name, send_sems, recv_sems, arrays, after, drain):
    m = len(arrays)

    def body(*refs):
        drain(refs[:m], refs[m], refs[m + 1])

    out = pl.pallas_call(
        body, name=name, out_shape=[_hbm_like(a) for a in arrays],
        in_specs=[HBM_SPEC] * m + [SEM_SPEC, SEM_SPEC, ANY], out_specs=[HBM_SPEC] * m,
        input_output_aliases={k: k for k in range(m)},
        compiler_params=pltpu.CompilerParams(has_side_effects=DATAFLOW),
    )(*arrays, send_sems, recv_sems, after)
    return list(out)


def _wait_both(cp):
    cp.wait_send()
    cp.wait_recv()


class _Flight:
    def __init__(self, name, arrays, n_sems, issue, drain, thru, extra=()):
        self.name, self.drain, self.n = name, drain, len(arrays)
        self.send, self.recv, out = _split_start(name + "_start", [*arrays, thru], n_sems, issue, extra)
        self.arrays, self.thru = out[:-1], out[-1]

    def land(self, after):
        return _split_wait(self.name + "_wait", self.send, self.recv, self.arrays, after, self.drain)


def _gather_flight(tag, ici, d2d, direct, thru):
    kinds = ["ici"] * len(ici) + ["d2d"] * len(d2d) + ["direct"] * len(direct)
    rows = [pl.ds(part * (a.shape[2] // parts), a.shape[2] // parts) for a, part, parts in ici]
    ici = [a for a, _, _ in ici]

    def issue(refs, send_sems, recv_sems):
        x, y, c, chips = _place()
        q = 2 * x + y
        for k, kind in enumerate(kinds):
            for j, chip in enumerate(chips):
                if kind == "ici":
                    src, to = refs[k].at[q, c, rows[k]], (*chip, c)
                elif kind == "d2d":
                    src, to = refs[k].at[2 * chip[0] + chip[1], c], (x, y, 1 - c)
                else:
                    src, to = refs[k].at[q], (*chip, c)
                _remote(src, src, send_sems.at[3 * k + j], recv_sems.at[3 * k + j], to).start()

    def drain(refs, send_sems, recv_sems):
        x, y, c, chips = _place()
        for k, kind in enumerate(kinds):
            for j, chip in enumerate(chips):
                p = 2 * chip[0] + chip[1]
                got = refs[k].at[p] if kind == "direct" else refs[k].at[p, c, rows[k]] if kind == "ici" else refs[k].at[p, 1 - c]
                _wait_both(_remote(got, got, send_sems.at[3 * k + j], recv_sems.at[3 * k + j], (x, y, c)))

    return _Flight(f"gather{tag}", [*ici, *d2d, *direct], 3 * len(kinds), issue, drain, thru)


def _swap_flight(tag, g4s, thru):
    n = len(g4s)
    zones = [lax.empty((N_CHIPS,) + g.shape[2:], g.dtype) for g in g4s]

    def issue(refs, send_sems, recv_sems):
        x, y, c, _ = _place()
        for k in range(n):
            for p in range(N_CHIPS):
                _remote(refs[k].at[p, 1 - c], refs[n + k].at[p], send_sems.at[N_CHIPS * k + p], recv_sems.at[N_CHIPS * k + p], (x, y, 1 - c)).start()

    def drain(refs, send_sems, recv_sems):
        x, y, c, _ = _place()
        for k in range(n):
            for p in range(N_CHIPS):
                got = refs[n + k].at[p]
                _wait_both(_remote(got, got, send_sems.at[N_CHIPS * k + p], recv_sems.at[N_CHIPS * k + p], (x, y, c)))

    return _Flight(f"swap{tag}", [*g4s, *zones], N_CHIPS * n, issue, drain, thru)


def _scatter_flight(tag, hs, lands, thru):
    n = len(hs)

    def issue(refs, send_sems, recv_sems):
        x, y, c, chips = _place()
        q = 2 * x + y
        for k in range(n):
            for j, chip in enumerate(chips):
                _remote(refs[k].at[2 * chip[0] + chip[1]], refs[n + k].at[q], send_sems.at[3 * k + j], recv_sems.at[3 * k + j], (*chip, c)).start()

    def drain(refs, send_sems, recv_sems):
        x, y, c, chips = _place()
        for k in range(n):
            for j, chip in enumerate(chips):
                got = refs[n + k].at[2 * chip[0] + chip[1]]
                _wait_both(_remote(got, got, send_sems.at[3 * k + j], recv_sems.at[3 * k + j], (x, y, c)))

    return _Flight(f"scatter{tag}", [*hs, *lands], 3 * n, issue, drain, thru)


def _exchange_flight(tag, buf, thru):
    flips = [(fx, fy, fc) for fx in (0, 1) for fy in (0, 1) for fc in (0, 1)][1:]

    def peers():
        x, y, c, _ = _place()
        return (x, y, c), [((1 - x) if fx else x, (1 - y) if fy else y, (1 - c) if fc else c) for fx, fy, fc in flips]

    def slot(ref, dev):
        return ref.at[4 * dev[0] + 2 * dev[1] + dev[2]]

    def issue(refs, send_sems, recv_sems):
        me, others = peers()
        for j, to in enumerate(others):
            _remote(slot(refs[0], me), slot(refs[0], me), send_sems.at[j], recv_sems.at[j], to).start()

    def drain(refs, send_sems, recv_sems):
        me, others = peers()
        for j, frm in enumerate(others):
            got = slot(refs[0], frm)
            _wait_both(_remote(got, got, send_sems.at[j], recv_sems.at[j], me))

    return _Flight(f"exchange{tag}", [buf], len(flips), issue, drain, thru)


def _share_flight(tag, fins, thru):
    n = len(fins)

    def issue(refs, send_sems, recv_sems):
        x, y, c, _ = _place()
        for k in range(n):
            _remote(refs[k].at[c], refs[k].at[c], send_sems.at[k], recv_sems.at[k], (x, y, 1 - c)).start()

    def drain(refs, send_sems, recv_sems):
        x, y, c, _ = _place()
        for k in range(n):
            got = refs[k].at[1 - c]
            _wait_both(_remote(got, got, send_sems.at[k], recv_sems.at[k], (x, y, c)))

    return _Flight(f"share{tag}", fins, n, issue, drain, thru)


def _pair_blocks(w):
    h, d, _ = w.shape
    z = jnp.zeros((h // 2, d, d), w.dtype)
    return jnp.concatenate([jnp.concatenate([w[0::2], z], axis=2), jnp.concatenate([z, w[1::2]], axis=2)], axis=1)


def _unpair_blocks(b):
    n, dd, _ = b.shape
    d = dd // 2
    return jnp.stack([b[:, :d, :d], b[:, d:, d:]], axis=1).reshape(2 * n, d, d)


def _pad_rows(a, rows):
    return jnp.pad(a, ((0, rows - a.shape[0]), (0, 0)))


class _Packer:
    def __init__(self, shapes, width=1024, row_multiple=64):
        self.shapes = shapes
        self.sizes = [math.prod(s) for s in shapes]
        total = sum(self.sizes)
        self.width = width
        self.rows = -(-total // (width * row_multiple)) * row_multiple
        self.pad = self.rows * width - total

    def pack(self, arrays):
        flat = jnp.concatenate([a.reshape(-1).astype(F32) for a in arrays] + [jnp.zeros((self.pad,), F32)])
        return flat.reshape(self.rows, self.width)

    def unpack(self, packed):
        flat = packed.reshape(-1)
        out, off = [], 0
        for s, n in zip(self.shapes, self.sizes):
            out.append(flat[off:off + n].reshape(s))
            off += n
        return out


SMALL = ["b_ada", "ffn1_norm", "mix_norm", "conv_w", "conv_b", "gate_a_w", "gate_a_b", "gate_x_w", "gate_x_b", "lru_lambda",
         "v_norm", "spatial_w", "spatial_b", "lru_out_norm", "gmlp_out_norm", "ffn2_norm", "final_norm"]
BIG = ["ffn1_w_gu", "ffn1_w_down", "w_in", "w_out", "ffn2_w_gu", "ffn2_w_down"]
GROUPS = (("ffn1_w_gu", "ffn1_w_down"), ("w_in", "w_out"), ("ffn2_w_gu", "ffn2_w_down"))
FWD_GROUPS = (("ffn1_w_gu",), ("ffn1_w_down",), ("w_in", "w_out"), ("ffn2_w_gu",), ("ffn2_w_down",))
MIN_AGE = {"swap": 1, "scatter": 1, "share": 1}
WEIGHTS = ["w_ada", "b_ada", "ffn1_norm", "ffn1_w_gu", "ffn1_w_down", "mix_norm", "w_in", "conv_w", "conv_b", "gate_a_w", "gate_a_b",
           "gate_x_w", "gate_x_b", "lru_lambda", "v_norm", "spatial_w", "spatial_b", "lru_out_norm", "gmlp_out_norm", "w_out",
           "ffn2_norm", "ffn2_w_gu", "ffn2_w_down", "final_norm"]


def kernel(x, c, w_ada, b_ada, ffn1_norm, ffn1_w_gu, ffn1_w_down, mix_norm, w_in, conv_w, conv_b, gate_a_w, gate_a_b, gate_x_w, gate_x_b, lru_lambda, v_norm, spatial_w, spatial_b, lru_out_norm, gmlp_out_norm, w_out, ffn2_norm, ffn2_w_gu, ffn2_w_down, final_norm, loss_target, m_w_ada, m_b_ada, m_ffn1_norm, m_ffn1_w_gu, m_ffn1_w_down, m_mix_norm, m_w_in, m_conv_w, m_conv_b, m_gate_a_w, m_gate_a_b, m_gate_x_w, m_gate_x_b, m_lru_lambda, m_v_norm, m_spatial_w, m_spatial_b, m_lru_out_norm, m_gmlp_out_norm, m_w_out, m_ffn2_norm, m_ffn2_w_gu, m_ffn2_w_down, m_final_norm, v_w_ada, v_b_ada, v_ffn1_norm, v_ffn1_w_gu, v_ffn1_w_down, v_mix_norm, v_w_in, v_conv_w, v_conv_b, v_gate_a_w, v_gate_a_b, v_gate_x_w, v_gate_x_b, v_lru_lambda, v_v_norm, v_spatial_w, v_spatial_b, v_lru_out_norm, v_gmlp_out_norm, v_w_out, v_ffn2_norm, v_ffn2_w_gu, v_ffn2_w_down, v_final_norm):
    given = dict(locals())
    W = {n: given[n] for n in WEIGHTS}
    L = w_ada.shape[0]
    S, D = x.shape[1], x.shape[2]
    LW = conv_b.shape[1]
    hd = LW // HEADS
    xi, yi, ci = lax.axis_index("x"), lax.axis_index("y"), lax.axis_index("c")
    chip = 2 * xi + yi
    dev = 2 * chip + ci
    place = jnp.stack([ci, chip]).astype(jnp.int32)
    xs = x.reshape(S, D)
    tgt = loss_target.reshape(S, D)

    n_ada = w_ada.shape[2]
    cws = LW // N_CHIPS

    def half_view(s):
        return s.reshape(N_CHIPS, 2, s.shape[1] // 2, s.shape[2])

    stages = [(l, names) for l in range(L) for names in FWD_GROUPS]
    seq = [[half_view(_cast_into_slot(W[n], l, place, place)) for n in names] for l, names in stages[:1]]
    flights = {}

    def ici_plan(t):
        if t >= len(stages):
            return []
        stage = t % len(FWD_GROUPS)
        return {2: [(t, 0, 1), (t + 1, 0, 2)], 3: [(t, 1, 2)]}.get(stage, [(t, 0, 1)])

    def launch(t, thru, direct=()):
        ici = [(a, part, parts) for g, part, parts in ici_plan(t) for a in seq[g]]
        d2d = seq[t - 1] if 1 <= t <= len(seq) else []
        if ici or d2d or direct:
            flights[t] = _gather_flight(t, ici, d2d, list(direct), thru)
            thru = flights[t].thru
        return thru

    def land(t, after):
        if t not in flights:
            return []
        out = flights.pop(t).land(after)
        for g, _, _ in ici_plan(t):
            seq[g], out = out[:len(seq[g])], out[len(seq[g]):]
        if 1 <= t <= len(seq):
            seq[t - 1], out = out[:len(seq[t - 1])], out[len(seq[t - 1]):]
        return out

    def group_weights(t):
        return [s.reshape(N_CHIPS, -1, s.shape[3]) for s in seq[t]]

    c_rows = _pad_rows(c, SUBLANES)
    c_flight = _exchange_flight("c", lax.dynamic_update_index_in_dim(jnp.zeros((N_DEV,) + c_rows.shape, F32), c_rows, dev, 0), c_rows)
    started = launch(0, c_flight.thru)
    c_all = c_flight.land(started)[0][:, 0, :]
    b_shard = lax.dynamic_slice_in_dim(b_ada, chip * n_ada, n_ada, axis=1)
    mod_shard = _ada_fwd(_pad_rows(c_all, 2 * SUBLANES), w_ada, b_shard[:, None, :])
    seq += [[half_view(_cast_into_slot(W[n], l, place, started)) for n in names] for l, names in stages[1:]]

    def in_slot(block):
        return lax.dynamic_update_index_in_dim(jnp.zeros((N_CHIPS,) + block.shape, block.dtype), block, chip, 0)

    land(0, seq[-1][-1])
    small = [in_slot(mod_shard.reshape(L * 2 * SUBLANES, n_ada)), in_slot(conv_w.reshape(L * CONV_WIDTH, cws))]
    mod_all, conv_all = land(1, launch(1, mod_shard, small))
    mod_rows = lax.dynamic_index_in_dim(mod_all.reshape(N_CHIPS, L, 2 * SUBLANES, n_ada), dev, axis=2, keepdims=False)
    mod = mod_rows.transpose(1, 0, 2).reshape(L, N_MOD, 1, D)
    conv_full = conv_all.reshape(N_CHIPS, L, CONV_WIDTH, cws).transpose(1, 2, 0, 3).reshape(L, CONV_WIDTH, LW)

    tril = jnp.tril(jnp.ones((CHUNK, CHUNK), F32))
    seg = (jnp.arange(LW)[:, None] // hd == jnp.arange(LW)[None, :] // hd).astype(jnp.bfloat16)

    def mixer_params(l):
        ws = spatial_w[l] * tril
        wsp = jnp.concatenate([ws[0::2], ws[1::2]], axis=2)
        wa, wx = _pair_blocks(gate_a_w[l]), _pair_blocks(gate_x_w[l])
        return dict(
            cw=conv_full[l], cb=conv_b[l][None],
            wa=wa.astype(MXU_DTYPE), wx=wx.astype(MXU_DTYPE), wat=wa.transpose(0, 2, 1).astype(MXU_DTYPE), wxt=wx.transpose(0, 2, 1).astype(MXU_DTYPE),
            ba=gate_a_b[l].reshape(1, LW), bx=gate_x_b[l].reshape(1, LW), lam=lru_lambda[l][None], gv=v_norm[l][None],
            wsp=wsp.astype(MXU_DTYPE), wspt=wsp.transpose(0, 2, 1).astype(MXU_DTYPE),
            bfull=jnp.repeat(spatial_b[l].T, hd, axis=1), g_lru=lru_out_norm[l][None], g_gm=gmlp_out_norm[l][None])

    saved = []
    xcur = xs
    zero_row = jnp.zeros((1, D), F32)
    h = _modnorm(xcur, ffn1_norm[0][None], mod[0][0], mod[0][1])
    for l in range(L):
        mp, md = mixer_params(l), mod[l]
        s = dict(lw={}, mp=mp, md=md)
        lw = s["lw"]
        t = len(FWD_GROUPS) * l
        s["x0"] = xcur
        s["h1"] = launch(t + 2, h)
        lw["gu1"], = group_weights(t)
        s["a1"], s["gu1"] = _ffn_up(s["h1"], lw["gu1"])
        land(t + 2, s["a1"])
        s["a1"] = launch(t + 3, s["a1"])
        lw["d1"] = group_weights(t + 1)[0].reshape(-1, D)
        s["f1"], xcur, h = _mm_res(s["a1"], lw["d1"], xcur, md[2], 0.5, (mix_norm[l][None], md[3], md[4]))
        land(t + 3, xcur)
        s["x1"] = xcur
        s["h2"] = launch(t + 4, h)
        lw["win"], wout = group_weights(t + 2)
        lw["wout"] = wout.reshape(-1, D)
        s["proj"] = _mm_chunks(s["h2"], lw["win"])
        s["ylru"] = _lru_fwd(s["proj"], mp["cw"], mp["cb"], mp["wa"], mp["ba"], mp["wx"], mp["bx"], mp["lam"])
        s["yn"], s["ygm"] = _gmlp_fwd(s["proj"], s["ylru"], mp["gv"], seg, mp["wsp"], mp["bfull"], mp["g_lru"], mp["g_gm"])
        s["f2"], xcur, h = _mm_res(s["yn"], lw["wout"], xcur, md[5], 1.0, (ffn2_norm[l][None], md[6], md[7]))
        land(t + 4, xcur)
        s["x2"] = xcur
        s["h3"] = launch(t + 5, h)
        lw["gu2"], = group_weights(t + 3)
        s["a3"], s["gu3"] = _ffn_up(s["h3"], lw["gu2"])
        land(t + 5, s["a3"])
        s["a3"] = launch(t + 6, s["a3"])
        lw["d2"] = group_weights(t + 4)[0].reshape(-1, D)
        following = (ffn1_norm[l + 1][None], mod[l + 1][0], mod[l + 1][1]) if l + 1 < L else (final_norm[None], zero_row, zero_row)
        s["f3"], xcur, h = _mm_res(s["a3"], lw["d2"], xcur, md[8], 0.5, following)
        land(t + 6, xcur)
        saved.append(s)

    dx, dq, head_acc = _loss_head(xcur, tgt, final_norm[None], saved[-1]["md"][8], 0.5)
    loss = lax.psum(jnp.sum(head_acc[1]), ("x", "y", "c"))
    big_grads = {n: [None] * L for n in BIG}
    dmods = [None] * L

    def ffn_bwd(names, l, dx, dq, x_in, h, a, gu, f, wgu, wd, gn, sc, next_gate, next_scale):
        big_grads[names[1]][l] = _mm_tn_chunks(a, dq[None], 1408, 1024)[0].reshape(N_CHIPS, -1, D)
        dgu = _ffn_bwd_act(dq, wd, gu)
        C = dgu.shape[3]
        dgu4 = dgu.reshape(N_CHIPS, S, C)
        big_grads[names[0]][l] = _mm_tn_chunks(h, dgu4, 1024, C)
        dgu4 = reduce_group(names, l, big_grads[names[0]][l], dgu4)
        dx, dq, acc = _mm_nt_norm_bwd(dgu4, wgu, x_in, dx, f, gn, sc, 0.5, next_gate, next_scale)
        return dx, move_on(dx, dq), acc

    stepped = {n: None for n in BIG}
    reducing = []

    per_layer = [n for n in SMALL if n not in ("b_ada", "final_norm")]
    packers, exchanges = {}, {}
    clock = [0]
    to_step = []

    def step_reduced(after):
        while to_step:
            name, l, g = to_step.pop(0)
            stepped[name] = _adamw_layer(W[name], g, given["m_" + name], given["v_" + name], l, stepped[name], after)
            after = stepped[name][1]
        return after

    def move_on(after, thru, force=False):
        clock[0] += 1
        for grp in list(reducing):
            if not force and clock[0] - grp["since"] < MIN_AGE[grp["step"]]:
                continue
            grp["since"] = clock[0]
            landed = grp["flight"].land(after)
            n = len(grp["names"])
            if grp["step"] == "swap":
                pairs = [_add_half(g4, r1, place) for g4, r1 in zip(landed[:n], landed[n:])]
                grp.update(step="scatter", flight=_scatter_flight(grp["tag"], [h for h, _ in pairs], [own for _, own in pairs], thru))
            elif grp["step"] == "scatter":
                grp.update(step="share", flight=_share_flight(grp["tag"], [_sum4_into_half(r2, place) for r2 in landed[n:]], thru))
            else:
                to_step.extend((name, grp["l"], fin.reshape(2 * fin.shape[1], fin.shape[2])) for name, fin in zip(grp["names"], landed))
                reducing.remove(grp)
                continue
            thru = grp["flight"].thru
        return thru

    def reduce_group(names, l, after, thru):
        thru = move_on(after, thru)
        g4s = [big_grads[n][l].reshape(N_CHIPS, 2, big_grads[n][l].shape[1] // 2, big_grads[n][l].shape[2]) for n in names]
        tag = f"{l}{GROUPS.index(names)}"
        reducing.append(dict(names=names, l=l, tag=tag, step="swap", since=clock[0], flight=_swap_flight(tag, g4s, thru)))
        return reducing[-1]["flight"].thru
    for l in reversed(range(L)):
        s = saved[l]
        lw, mp, md = s["lw"], s["mp"], s["md"]
        dx, dq, acc3 = ffn_bwd(
            GROUPS[2], l, dx, dq, s["x2"], s["h3"], s["a3"], s["gu3"], s["f3"], lw["gu2"], lw["d2"], ffn2_norm[l][None], md[7], md[5], 1.0)
        big_grads["w_out"][l] = _mm_tn_chunks(s["yn"], dq[None], 1024, 1024)[0].reshape(N_CHIPS, -1, D)
        dyn = _mm_nt_chunks(dq[None], lw["wout"][None])
        dylru, duv, dwsp, dbfull, gacc = _gmlp_bwd(s["proj"], s["ylru"], s["ygm"], dyn, mp["gv"], seg, mp["wsp"], mp["wspt"], mp["bfull"], mp["g_lru"], mp["g_gm"])
        dxg, dwa, dwx, lvec = _lru_bwd(s["proj"], dylru, mp["cw"], mp["cb"], mp["wa"], mp["ba"], mp["wx"], mp["bx"], mp["lam"], mp["wat"], mp["wxt"])
        dproj = jnp.concatenate([dxg, duv], axis=0)
        big_grads["w_in"][l] = _mm_tn_together(s["h2"], dproj)
        dproj = reduce_group(GROUPS[1], l, big_grads["w_in"][l], dproj)
        dx, dq, acc2 = _mm_nt_norm_bwd(dproj, lw["win"], s["x1"], dx, s["f2"], mix_norm[l][None], md[4], 1.0, md[2], 0.5)
        dq = move_on(dx, dq)
        if l > 0:
            ng, ns = saved[l - 1]["md"][8], 0.5
        else:
            ng, ns = zero_row, 0.0
        dx, dq, acc1 = ffn_bwd(
            GROUPS[0], l, dx, dq, s["x0"], s["h1"], s["a1"], s["gu1"], s["f1"], lw["gu1"], lw["d1"], ffn1_norm[l][None], md[1], ng, ns)

        dmods[l] = jnp.concatenate([acc1[0:2], acc1[3:4], acc2[0:2], acc2[3:4], acc3[0:2], acc3[3:4]], axis=0)
        dws = jnp.stack([dwsp[:, :, :CHUNK], dwsp[:, :, CHUNK:]], axis=1).reshape(HEADS, CHUNK, CHUNK) * tril
        lg = {"ffn1_norm": acc1[2], "mix_norm": acc2[2], "ffn2_norm": acc3[2],
              "conv_w": lvec[4:8], "conv_b": lvec[3], "gate_a_w": _unpair_blocks(dwa), "gate_a_b": lvec[0].reshape(HEADS, hd),
              "gate_x_w": _unpair_blocks(dwx), "gate_x_b": lvec[1].reshape(HEADS, hd), "lru_lambda": lvec[2], "v_norm": gacc[2],
              "spatial_w": dws, "spatial_b": dbfull.reshape(CHUNK, HEADS, hd).sum(-1).T, "lru_out_norm": gacc[0], "gmlp_out_norm": gacc[1]}
        part = [lg[n] for n in per_layer] + [dmods[l]] + ([head_acc[0]] if l == L - 1 else [])
        packers[l] = _Packer([p.shape for p in part])
        packed = packers[l].pack(part)
        exchanges[l] = _exchange_flight(l, lax.dynamic_update_index_in_dim(jnp.zeros((N_DEV,) + packed.shape, F32), packed, dev, 0), dq)
        dq = exchanges[l].thru

    grad_x = dx.reshape(x.shape)

    done = step_reduced(dq)
    while reducing:
        dq = move_on(done, dq, force=True)
        done = step_reduced(dq)
    summed, dmod_rows = [], []
    for l in range(L):
        gathered, = exchanges[l].land(done)
        summed.append(packers[l].unpack(_sum_leading(gathered)))
        off = sum(packers[l].sizes[:len(per_layer)])
        dmod_rows.append(gathered.reshape(N_DEV, -1)[:, off:off + N_MOD * D])
    grads = {n: jnp.stack([summed[l][k] for l in range(L)]) for k, n in enumerate(per_layer)}
    grads["final_norm"] = summed[L - 1][len(per_layer) + 1]
    grads["b_ada"] = jnp.stack([summed[l][len(per_layer)].reshape(N_MOD * D) for l in range(L)])
    dmod_shard = lax.dynamic_slice_in_dim(jnp.stack(dmod_rows), chip * n_ada, n_ada, axis=2)
    stepped_ada = _ada_step(c_all.T, dmod_shard, w_ada, m_w_ada, v_w_ada)
    grads["w_ada"] = stepped_ada[0]
    grads["conv_w"] = lax.dynamic_slice_in_dim(grads["conv_w"], chip * cws, cws, axis=2)

    delta, new_m, new_v = {}, {}, {}
    for n in BIG:
        grads[n], delta[n], new_m[n], new_v[n] = stepped[n]
    delta["w_ada"], new_m["w_ada"], new_v["w_ada"] = stepped_ada[1:]
    def rows_of(a):
        return a.reshape(-1, a.shape[-1])

    stepped_small = _adamw_small(*[[rows_of(src[n].reshape(W[n].shape)) for n in SMALL]
                                   for src in (W, grads, {n: given["m_" + n] for n in SMALL}, {n: given["v_" + n] for n in SMALL})])
    for k, n in enumerate(SMALL):
        delta[n], new_m[n], new_v[n] = (stepped_small[i * len(SMALL) + k].reshape(W[n].shape) for i in range(3))
    grads = {n: grads[n].reshape(W[n].shape) for n in WEIGHTS}
    return (loss, grad_x, *[grads[n] for n in WEIGHTS], *[delta[n] for n in WEIGHTS], *[new_m[n] for n in WEIGHTS], *[new_v[n] for n in WEIGHTS])
```

```python
import math

import jax
import jax.numpy as jnp
from jax import lax
from jax.experimental import pallas as pl
from jax.experimental.pallas import tpu as pltpu

F32 = jnp.float32
MXU_DTYPE = jnp.bfloat16
ACT_DTYPE = jnp.bfloat16
XFER_DTYPE = jnp.bfloat16
EPS = 1e-6
RG_LRU_C = 8.0
N_MOD = 9
CONV_WIDTH = 4
HEADS = 8
CHUNK = 128
LANES = 128
SUBLANES = 8
N_CHIPS = 4
N_DEV = 8
ADAM_LR, ADAM_B1, ADAM_B2, ADAM_EPS, ADAM_WD, ADAM_STEP = 0.001, 0.9, 0.999, 1e-08, 0.01, 10
VMEM_LIMIT_BYTES = 60 * 1024 * 1024
ROW_TILE_BYTES = 4 << 20
GELU_C = math.sqrt(2.0 / math.pi)
GELU_A = 0.044715

ANY = pl.BlockSpec(memory_space=pl.ANY)
MESH = pl.DeviceIdType.MESH
SDS = jax.ShapeDtypeStruct


def _params(*sem):
    return pltpu.CompilerParams(dimension_semantics=sem, vmem_limit_bytes=VMEM_LIMIT_BYTES)


def _dot(a, b):
    return jnp.dot(a.astype(MXU_DTYPE), b.astype(MXU_DTYPE), preferred_element_type=F32)


def _dot_nt(a, b):
    return lax.dot_general(a.astype(MXU_DTYPE), b.astype(MXU_DTYPE), (((1,), (1,)), ((), ())), preferred_element_type=F32)


def _dot_tn(a, b):
    return lax.dot_general(a.astype(MXU_DTYPE), b.astype(MXU_DTYPE), (((0,), (0,)), ((), ())), preferred_element_type=F32)


def _gelu(x):
    return x * (0.5 * (1.0 + jnp.tanh(GELU_C * (x + GELU_A * (x * x * x)))))


def _gelu_grad(x):
    t = jnp.tanh(GELU_C * (x + GELU_A * (x * x * x)))
    return 0.5 * (1.0 + t) + 0.5 * x * (1.0 - t * t) * (GELU_C * (1.0 + 3.0 * GELU_A * x * x))


def _sigmoid(x):
    return jax.nn.sigmoid(x)


def _sigmoid_by_tanh(x):
    return 0.5 * jnp.tanh(0.5 * x) + 0.5


def _rsqrt_ms(x):
    return lax.rsqrt(jnp.mean(x * x, axis=-1, keepdims=True) + EPS)


def _rowsum(x):
    return jnp.sum(x, axis=0, keepdims=True)


def _tile(n, want):
    t = min(n, want)
    assert n % t == 0, (n, want)
    return t


def _row_tile(rows, row_bytes):
    step = 2 * SUBLANES
    cap = max(step, ROW_TILE_BYTES // row_bytes)
    best = None
    for t in range(step, min(rows, cap) + 1, step):
        if rows % t == 0:
            best = t
    assert best is not None, (rows, row_bytes)
    return best


def _modnorm(x, gn, sh, sc):
    S, D = x.shape
    tm = _tile(S, 1024)

    def body(x_ref, gn_ref, sh_ref, sc_ref, h_ref):
        xv = x_ref[...]
        h = (xv * _rsqrt_ms(xv) * gn_ref[...]) * (1.0 + sc_ref[...]) + sh_ref[...]
        h_ref[...] = h.astype(ACT_DTYPE)

    row = pl.BlockSpec((1, D), lambda i: (0, 0))
    return pl.pallas_call(
        body, name="modnorm", grid=(S // tm,),
        in_specs=[pl.BlockSpec((tm, D), lambda i: (i, 0)), row, row, row],
        out_specs=pl.BlockSpec((tm, D), lambda i: (i, 0)),
        out_shape=SDS((S, D), ACT_DTYPE), compiler_params=_params("parallel"),
    )(x, gn, sh, sc)


def _mm_nt_norm_bwd(ac, wc, x, dxo, f, gn, sc, res_scale, next_gate, next_scale):
    P, S, K = ac.shape
    D = x.shape[1]
    tm = _tile(S, 512)

    def body(a_ref, w_ref, x_ref, dxo_ref, f_ref, gn_ref, sc_ref, ng_ref, dx_ref, dq_ref, acc_ref):
        @pl.when(pl.program_id(0) == 0)
        def _():
            acc_ref[...] = jnp.zeros_like(acc_ref)

        dh = _dot_nt(a_ref[0], w_ref[0])
        for p in range(1, P):
            dh += _dot_nt(a_ref[p], w_ref[p])
        xv, dxo = x_ref[...], dxo_ref[...]
        r = _rsqrt_ms(xv)
        xhat = xv * r
        gn = gn_ref[...]
        dn = dh * (1.0 + sc_ref[...])
        dxh = dn * gn
        dx = dxo + r * (dxh - xhat * jnp.mean(dxh * xhat, axis=-1, keepdims=True))
        dx_ref[...] = dx
        dq_ref[...] = ((next_scale * ng_ref[...]) * dx).astype(ACT_DTYPE)
        acc_ref[0:1, :] += _rowsum(dh)
        acc_ref[1:2, :] += _rowsum(dh * (xhat * gn))
        acc_ref[2:3, :] += _rowsum(dn * xhat)
        acc_ref[3:4, :] += _rowsum((res_scale * f_ref[...]) * dxo)

    tile = pl.BlockSpec((tm, D), lambda i: (i, 0))
    row = pl.BlockSpec((1, D), lambda i: (0, 0))
    return pl.pallas_call(
        body, name=f"mm_nt_norm_bwd_k{K}", grid=(S // tm,),
        in_specs=[pl.BlockSpec((P, tm, K), lambda i: (0, i, 0)),
                  pl.BlockSpec((P, D, K), lambda i: (0, 0, 0), pipeline_mode=pl.Buffered(1)),
                  tile, tile, tile, row, row, row],
        out_specs=[tile, tile, pl.BlockSpec((SUBLANES, D), lambda i: (0, 0))],
        out_shape=[SDS((S, D), F32), SDS((S, D), ACT_DTYPE), SDS((SUBLANES, D), F32)],
        compiler_params=_params("arbitrary"),
    )(ac, wc, x, dxo, f, gn, sc, next_gate)


def _loss_head(x, target, gn, next_gate, next_scale):
    S, D = x.shape
    tm = _tile(S, 512)

    def body(x_ref, t_ref, gn_ref, ng_ref, dx_ref, dq_ref, acc_ref):
        @pl.when(pl.program_id(0) == 0)
        def _():
            acc_ref[...] = jnp.zeros_like(acc_ref)

        xv = x_ref[...]
        r = _rsqrt_ms(xv)
        xhat = xv * r
        gn = gn_ref[...]
        err = xhat * gn - t_ref[...]
        dy = err * (1.0 / D)
        dxh = dy * gn
        dx = r * (dxh - xhat * jnp.mean(dxh * xhat, axis=-1, keepdims=True))
        dx_ref[...] = dx
        dq_ref[...] = ((next_scale * ng_ref[...]) * dx).astype(ACT_DTYPE)
        acc_ref[0:1, :] += _rowsum(dy * xhat)
        acc_ref[1:2, :] += _rowsum(err * err) * (0.5 / D)

    tile = pl.BlockSpec((tm, D), lambda i: (i, 0))
    row = pl.BlockSpec((1, D), lambda i: (0, 0))
    return pl.pallas_call(
        body, name="loss_head", grid=(S // tm,),
        in_specs=[tile, tile, row, row],
        out_specs=[tile, tile, pl.BlockSpec((SUBLANES, D), lambda i: (0, 0))],
        out_shape=[SDS((S, D), F32), SDS((S, D), ACT_DTYPE), SDS((SUBLANES, D), F32)],
        compiler_params=_params("arbitrary"),
    )(x, target, gn, next_gate)


def _ffn_up(h, wgu):
    S, D = h.shape
    C = wgu.shape[2]
    tm = _tile(S, 512)

    def body(h_ref, wg_ref, wu_ref, a_ref, gu_ref):
        hv = h_ref[...]
        g = _dot(hv, wg_ref[...])
        u = _dot(hv, wu_ref[...])
        a_ref[...] = (g * _sigmoid_by_tanh(g) * u).astype(ACT_DTYPE)
        gu_ref[0] = g.astype(ACT_DTYPE)
        gu_ref[1] = u.astype(ACT_DTYPE)

    return pl.pallas_call(
        body, name="ffn_up", grid=(2, S // tm),
        in_specs=[
            pl.BlockSpec((tm, D), lambda j, i: (i, 0)),
            pl.BlockSpec((None, D, C), lambda j, i: (j, 0, 0)),
            pl.BlockSpec((None, D, C), lambda j, i: (2 + j, 0, 0)),
        ],
        out_specs=[
            pl.BlockSpec((tm, C), lambda j, i: (i, j)),
            pl.BlockSpec((2, None, tm, C), lambda j, i: (0, j, i, 0)),
        ],
        out_shape=[SDS((S, 2 * C), ACT_DTYPE), SDS((2, 2, S, C), ACT_DTYPE)],
        compiler_params=_params("parallel", "parallel"),
    )(h, wgu, wgu)


def _ffn_bwd_act(dq, wd, gu):
    S, D = dq.shape
    C = gu.shape[3]
    tm = _tile(S, 512)

    def body(dq_ref, wd_ref, gu_ref, dgu_ref):
        da = _dot_nt(dq_ref[...], wd_ref[...])
        g = gu_ref[0].astype(F32)
        u = gu_ref[1].astype(F32)
        s = _sigmoid_by_tanh(g)
        gs = g * s
        dgu_ref[0] = ((da * u) * (s + gs - gs * s)).astype(ACT_DTYPE)
        dgu_ref[1] = (da * gs).astype(ACT_DTYPE)

    gu_spec = pl.BlockSpec((2, None, tm, C), lambda j, i: (0, j, i, 0))
    return pl.pallas_call(
        body, name="ffn_bwd_act", grid=(2, S // tm),
        in_specs=[pl.BlockSpec((tm, D), lambda j, i: (i, 0)), pl.BlockSpec((C, D), lambda j, i: (j, 0)), gu_spec],
        out_specs=gu_spec,
        out_shape=SDS(gu.shape, ACT_DTYPE),
        compiler_params=_params("parallel", "parallel"),
    )(dq, wd, gu)


def _mm_res(a, w, x, gate, scale, following):
    S, K = a.shape
    D = w.shape[1]
    tm = _tile(S, 512)

    def body(a_ref, w_ref, x_ref, g_ref, gn_ref, sh_ref, sc_ref, f_ref, xo_ref, h_ref):
        f = _dot(a_ref[...], w_ref[...])
        f_ref[...] = f
        xo = x_ref[...] + (scale * g_ref[...]) * f
        xo_ref[...] = xo
        h_ref[...] = ((xo * _rsqrt_ms(xo) * gn_ref[...]) * (1.0 + sc_ref[...]) + sh_ref[...]).astype(ACT_DTYPE)

    tile = pl.BlockSpec((tm, D), lambda i: (i, 0))
    row = pl.BlockSpec((1, D), lambda i: (0, 0))
    return pl.pallas_call(
        body, name=f"mm_res_k{K}", grid=(S // tm,),
        in_specs=[pl.BlockSpec((tm, K), lambda i: (i, 0)), pl.BlockSpec((K, D), lambda i: (0, 0)), tile, row, row, row, row],
        out_specs=[tile, tile, tile],
        out_shape=[SDS((S, D), F32), SDS((S, D), F32), SDS((S, D), ACT_DTYPE)],
        compiler_params=_params("parallel"),
    )(a, w, x, gate, *following)


def _mm_chunks(h, wc):
    S, K = h.shape
    P, _, N = wc.shape
    tm = _tile(S, 512)

    def body(h_ref, w_ref, o_ref):
        hv = h_ref[...]
        for p in range(P):
            o_ref[:, p * N:(p + 1) * N] = _dot(hv, w_ref[p])

    return pl.pallas_call(
        body, name="mm_chunks", grid=(S // tm,),
        in_specs=[pl.BlockSpec((tm, K), lambda i: (i, 0)), pl.BlockSpec((P, K, N), lambda i: (0, 0, 0))],
        out_specs=pl.BlockSpec((tm, P * N), lambda i: (i, 0)),
        out_shape=SDS((S, P * N), F32),
        compiler_params=_params("parallel"),
    )(h, wc)


def _mm_nt_chunks(ac, wc):
    P, S, K = ac.shape
    N = wc.shape[1]
    tm, tn = _tile(S, 512), _tile(N, 1024)

    def body(a_ref, w_ref, o_ref):
        acc = _dot_nt(a_ref[0], w_ref[0])
        for p in range(1, P):
            acc += _dot_nt(a_ref[p], w_ref[p])
        o_ref[...] = acc

    return pl.pallas_call(
        body, name=f"mm_nt_p{P}k{K}", grid=(S // tm, N // tn),
        in_specs=[pl.BlockSpec((P, tm, K), lambda i, j: (0, i, 0)), pl.BlockSpec((P, tn, K), lambda i, j: (0, j, 0))],
        out_specs=pl.BlockSpec((tm, tn), lambda i, j: (i, j)),
        out_shape=SDS((S, N), F32),
        compiler_params=_params("parallel", "parallel"),
    )(ac, wc)


def _mm_tn_chunks(a, bc, tile_m, tile_n):
    S, M = a.shape
    P, _, N = bc.shape
    ts, tm, tn = _tile(S, 2048), _tile(M, tile_m), _tile(N, tile_n)

    def body(a_ref, b_ref, o_ref):
        @pl.when(pl.program_id(3) == 0)
        def _():
            o_ref[...] = jnp.zeros_like(o_ref)

        o_ref[...] += _dot_tn(a_ref[...], b_ref[...])

    return pl.pallas_call(
        body, name=f"mm_tn_m{M}n{N}", grid=(P, M // tm, N // tn, S // ts),
        in_specs=[pl.BlockSpec((ts, tm), lambda p, m, n, k: (k, m)), pl.BlockSpec((None, ts, tn), lambda p, m, n, k: (p, k, n))],
        out_specs=pl.BlockSpec((None, tm, tn), lambda p, m, n, k: (p, m, n)),
        out_shape=SDS((P, M, N), F32),
        compiler_params=_params("parallel", "parallel", "parallel", "arbitrary"),
    )(a, bc)


def _mm_tn_together(a, bc):
    S, M = a.shape
    P, _, N = bc.shape
    ts = _tile(S, 2048)

    def body(a_ref, b_ref, o_ref):
        @pl.when(pl.program_id(0) == 0)
        def _():
            o_ref[...] = jnp.zeros_like(o_ref)

        at = a_ref[...].T
        for p in range(P):
            o_ref[p] += jnp.dot(at, b_ref[p], preferred_element_type=F32)

    return pl.pallas_call(
        body, name=f"mm_tn_together_m{M}n{N}", grid=(S // ts,),
        in_specs=[pl.BlockSpec((ts, M), lambda k: (k, 0)), pl.BlockSpec((P, ts, N), lambda k: (0, k, 0))],
        out_specs=pl.BlockSpec((P, M, N), lambda k: (0, 0, 0)),
        out_shape=SDS((P, M, N), F32), compiler_params=_params("arbitrary"),
    )(a, bc)


def _shift_down(x, s, row, fill):
    return jnp.where(row >= s, pltpu.roll(x, s, 0), fill)


def _shift_up(x, s, row, fill):
    n = x.shape[0]
    return jnp.where(row < n - s, pltpu.roll(x, n - s, 0), fill)


def _scan(a, b, row, scratch, up):
    scr_a, scr_b, scr_c = scratch
    n = a.shape[0]
    g = n // SUBLANES
    in_group = row & (SUBLANES - 1)

    def steps(a, b, pos, size):
        s = 1
        while s < size:
            m = (pos + s < size) if up else (pos >= s)
            b = jnp.where(m, a, 0.0) * pltpu.roll(b, a.shape[0] - s if up else s, 0) + b
            a = jnp.where(m, a * pltpu.roll(a, a.shape[0] - s if up else s, 0), a)
            s *= 2
        return a, b

    a, b = steps(a, b, in_group, SUBLANES)
    scr_a[...] = a
    scr_b[...] = b
    edge = 0 if up else SUBLANES - 1
    at = scr_a[pl.ds(edge, g, stride=SUBLANES), :]
    bt = scr_b[pl.ds(edge, g, stride=SUBLANES), :]
    group = lax.broadcasted_iota(jnp.int32, at.shape, 0)
    _, state = steps(at, bt, group, g)
    carry = jnp.where((group + 1 < g) if up else (group >= 1), pltpu.roll(state, g - 1 if up else 1, 0), 0.0)
    for k in range(SUBLANES):
        scr_c[pl.ds(k, g, stride=SUBLANES), :] = carry
    return b + a * scr_c[...]


def _conv(xl, cw_ref, cb_ref, row):
    y = cb_ref[...] + _shift_down(xl, 3, row, 0.0) * cw_ref[0:1, :]
    y = y + _shift_down(xl, 2, row, 0.0) * cw_ref[1:2, :]
    y = y + _shift_down(xl, 1, row, 0.0) * cw_ref[2:3, :]
    return y + xl * cw_ref[3:4, :]


def _lru_gates(xc, wa_ref, ba_ref, wx_ref, bx_ref, lam_ref):
    ra = _sigmoid(_dot(xc, wa_ref[...]) + ba_ref[...])
    ri = _sigmoid(_dot(xc, wx_ref[...]) + bx_ref[...])
    ls = jax.nn.log_sigmoid(lam_ref[...])
    a = jnp.exp((RG_LRU_C * ra) * ls)
    mult = jnp.sqrt(1.0 - a * a)
    return ra, ri, ls, a, mult


def _lru_specs(S):
    col = lambda off: pl.BlockSpec((S, LANES), lambda j: (0, off + j))
    vec = pl.BlockSpec((1, LANES), lambda j: (0, j))
    blk = pl.BlockSpec((None, LANES, LANES), lambda j: (j, 0, 0))
    cw = pl.BlockSpec((CONV_WIDTH, LANES), lambda j: (0, j))
    return col, vec, blk, cw


def _lru_fwd(proj, cw, cb, wa, ba, wx, bx, lam):
    S = proj.shape[0]
    W = cb.shape[1]
    nb = W // LANES

    def body(xl_ref, gl_ref, cw_ref, cb_ref, wa_ref, ba_ref, wx_ref, bx_ref, lam_ref, y_ref, *scratch):
        row = lax.broadcasted_iota(jnp.int32, (S, LANES), 0)
        xc = _conv(xl_ref[...], cw_ref, cb_ref, row)
        _, ri, _, a, mult = _lru_gates(xc, wa_ref, ba_ref, wx_ref, bx_ref, lam_ref)
        h = _scan(a, mult * (ri * xc), row, scratch, up=False)
        y_ref[...] = h * _gelu(gl_ref[...])

    col, vec, blk, cws = _lru_specs(S)
    return pl.pallas_call(
        body, name="lru_fwd", grid=(nb,),
        in_specs=[col(0), col(nb), cws, vec, blk, vec, blk, vec, vec],
        out_specs=pl.BlockSpec((S, LANES), lambda j: (0, j)),
        out_shape=SDS((S, W), F32), scratch_shapes=[pltpu.VMEM((S, LANES), F32)] * 3, compiler_params=_params("parallel"),
    )(proj, proj, cw, cb, wa, ba, wx, bx, lam)


def _lru_bwd(proj, dy, cw, cb, wa, ba, wx, bx, lam, wat, wxt):
    S = proj.shape[0]
    W = cb.shape[1]
    nb = W // LANES

    def body(xl_ref, gl_ref, dy_ref, cw_ref, cb_ref, wa_ref, ba_ref, wx_ref, bx_ref, lam_ref, wat_ref, wxt_ref,
             dp_ref, dwa_ref, dwx_ref, vec_ref, *scratch):
        row = lax.broadcasted_iota(jnp.int32, (S, LANES), 0)
        xl = xl_ref[...]
        xc = _conv(xl, cw_ref, cb_ref, row)
        ra, ri, ls, a, mult = _lru_gates(xc, wa_ref, ba_ref, wx_ref, bx_ref, lam_ref)
        h = _scan(a, mult * (ri * xc), row, scratch, up=False)
        gl = gl_ref[...]
        dyv = dy_ref[...]
        dp_ref[1] = (dyv * h * _gelu_grad(gl)).astype(ACT_DTYPE)
        adj = _scan(_shift_up(a, 1, row, 0.0), dyv * _gelu(gl), row, scratch, up=True)
        da = adj * _shift_down(h, 1, row, 0.0)
        dmult = adj * (ri * xc)
        dlog_a = da * a - dmult * (a * a) / mult
        dra = dlog_a * (RG_LRU_C * ls)
        dpa = dra * ra * (1.0 - ra)
        dpi = (adj * mult * xc) * ri * (1.0 - ri)
        dxc = adj * mult * ri + _dot(dpa, wat_ref[...]) + _dot(dpi, wxt_ref[...])
        dwa_ref[...] = _dot_tn(xc, dpa)
        dwx_ref[...] = _dot_tn(xc, dpi)
        dxl = dxc * cw_ref[3:4, :]
        dxl = dxl + _shift_up(dxc, 1, row, 0.0) * cw_ref[2:3, :]
        dxl = dxl + _shift_up(dxc, 2, row, 0.0) * cw_ref[1:2, :]
        dxl = dxl + _shift_up(dxc, 3, row, 0.0) * cw_ref[0:1, :]
        dp_ref[0] = dxl.astype(ACT_DTYPE)
        vec_ref[...] = jnp.zeros_like(vec_ref)
        vec_ref[0:1, :] = _rowsum(dpa)
        vec_ref[1:2, :] = _rowsum(dpi)
        vec_ref[2:3, :] = _rowsum(dlog_a * (RG_LRU_C * ra)) * _sigmoid(-lam_ref[...])
        vec_ref[3:4, :] = _rowsum(dxc)
        vec_ref[4:5, :] = _rowsum(dxc * _shift_down(xl, 3, row, 0.0))
        vec_ref[5:6, :] = _rowsum(dxc * _shift_down(xl, 2, row, 0.0))
        vec_ref[6:7, :] = _rowsum(dxc * _shift_down(xl, 1, row, 0.0))
        vec_ref[7:8, :] = _rowsum(dxc * xl)

    col, vec, blk, cws = _lru_specs(S)
    return pl.pallas_call(
        body, name="lru_bwd", grid=(nb,),
        in_specs=[col(0), col(nb), col(0), cws, vec, blk, vec, blk, vec, vec, blk, blk],
        out_specs=[pl.BlockSpec((2, S, LANES), lambda j: (0, 0, j)), blk, blk, pl.BlockSpec((2 * SUBLANES, LANES), lambda j: (0, j))],
        out_shape=[SDS((2, S, W), ACT_DTYPE), SDS((nb, LANES, LANES), F32), SDS((nb, LANES, LANES), F32), SDS((2 * SUBLANES, W), F32)],
        scratch_shapes=[pltpu.VMEM((S, LANES), F32)] * 3, compiler_params=_params("parallel"),
    )(proj, proj, dy, cw, cb, wa, ba, wx, bx, lam, wat, wxt)


def _seg_mean(x, seg_ref, width):
    hi = x.astype(jnp.bfloat16)
    lo = (x - hi.astype(F32)).astype(jnp.bfloat16)
    ones = seg_ref[...]
    s = jnp.dot(hi, ones, preferred_element_type=F32) + jnp.dot(lo, ones, preferred_element_type=F32)
    return s * (1.0 / width)


def _gmlp_core(u_ref, v_ref, gv_ref, seg_ref, ws_ref, bfull_ref, z_scr, hd):
    tm, W = u_ref.shape
    lane = lax.broadcasted_iota(jnp.int32, (CHUNK, LANES), 1)
    ug = _gelu(u_ref[...])
    vg = _gelu(v_ref[...])
    cen = vg - _seg_mean(vg, seg_ref, hd)
    rstd = lax.rsqrt(_seg_mean(cen * cen, seg_ref, hd) + EPS)
    vhat = cen * rstd
    vh = vhat * gv_ref[...]
    vcats = {}
    for ci in range(tm // CHUNK):
        for p in range(W // LANES):
            blk = vh[ci * CHUNK:(ci + 1) * CHUNK, p * LANES:(p + 1) * LANES]
            vcat = jnp.concatenate([jnp.where(lane < hd, blk, 0.0), jnp.where(lane >= hd, blk, 0.0)], axis=0).astype(MXU_DTYPE)
            vcats[ci, p] = vcat
            z_scr[ci * CHUNK:(ci + 1) * CHUNK, p * LANES:(p + 1) * LANES] = (
                jnp.dot(ws_ref[p], vcat, preferred_element_type=F32) + bfull_ref[:, p * LANES:(p + 1) * LANES])
    return ug, vhat, rstd, vcats


def _gmlp_specs(tm, W, nb):
    rows = lambda off: pl.BlockSpec((tm, W), lambda i: (i, off))
    vec = pl.BlockSpec((1, W), lambda i: (0, 0))
    seg = pl.BlockSpec((W, W), lambda i: (0, 0))
    wsp = pl.BlockSpec((nb, CHUNK, 2 * CHUNK), lambda i: (0, 0, 0))
    bfull = pl.BlockSpec((CHUNK, W), lambda i: (0, 0))
    return rows, vec, seg, wsp, bfull


def _gmlp_fwd(proj, ylru, gv, seg, wsp, bfull, g_lru, g_gm):
    S, W = ylru.shape
    nb = W // LANES
    hd = W // HEADS
    tm = _tile(S, 512)

    def body(u_ref, v_ref, yl_ref, gv_ref, seg_ref, ws_ref, bfull_ref, gl_ref, gg_ref, yn_ref, ygm_ref, z_scr):
        ug, _, _, _ = _gmlp_core(u_ref, v_ref, gv_ref, seg_ref, ws_ref, bfull_ref, z_scr, hd)
        ygm = ug * z_scr[...]
        ygm_ref[...] = ygm
        yl = yl_ref[...]
        yn_ref[:, 0:W] = (yl * _rsqrt_ms(yl) * gl_ref[...]).astype(ACT_DTYPE)
        yn_ref[:, W:2 * W] = (ygm * _rsqrt_ms(ygm) * gg_ref[...]).astype(ACT_DTYPE)

    rows, vec, segs, wsps, bfulls = _gmlp_specs(tm, W, nb)
    return pl.pallas_call(
        body, name="gmlp_fwd", grid=(S // tm,),
        in_specs=[rows(2), rows(3), rows(0), vec, segs, wsps, bfulls, vec, vec],
        out_specs=[pl.BlockSpec((tm, 2 * W), lambda i: (i, 0)), rows(0)],
        out_shape=[SDS((S, 2 * W), ACT_DTYPE), SDS((S, W), F32)],
        scratch_shapes=[pltpu.VMEM((tm, W), F32)],
        compiler_params=_params("parallel"),
    )(proj, proj, ylru, gv, seg, wsp, bfull, g_lru, g_gm)


def _rms_bwd(y, g, dyn):
    r = _rsqrt_ms(y)
    yhat = y * r
    dyh = dyn * g
    return r * (dyh - yhat * jnp.mean(dyh * yhat, axis=-1, keepdims=True)), _rowsum(dyn * yhat)


def _gmlp_bwd(proj, ylru, ygm, dyn, gv, seg, wsp, wspt, bfull, g_lru, g_gm):
    S, W = ylru.shape
    nb = W // LANES
    hd = W // HEADS
    tm = _tile(S, 256)

    def body(u_ref, v_ref, yl_ref, ygm_ref, dl_ref, dg_ref, gv_ref, seg_ref, ws_ref, wst_ref, bfull_ref, gl_ref, gg_ref,
             dyl_ref, duv_ref, dws_ref, dbf_ref, acc_ref, z_scr, dvh_scr):
        @pl.when(pl.program_id(0) == 0)
        def _():
            dws_ref[...] = jnp.zeros_like(dws_ref)
            dbf_ref[...] = jnp.zeros_like(dbf_ref)
            acc_ref[...] = jnp.zeros_like(acc_ref)

        dyl, dgl = _rms_bwd(yl_ref[...], gl_ref[...], dl_ref[...])
        dyl_ref[...] = dyl
        dygm, dgg = _rms_bwd(ygm_ref[...], gg_ref[...], dg_ref[...])
        ug, vhat, rstd, vcats = _gmlp_core(u_ref, v_ref, gv_ref, seg_ref, ws_ref, bfull_ref, z_scr, hd)
        duv_ref[0] = (dygm * z_scr[...] * _gelu_grad(u_ref[...])).astype(ACT_DTYPE)
        dz = dygm * ug
        lane = lax.broadcasted_iota(jnp.int32, (CHUNK, LANES), 1)
        dbf = dz[0:CHUNK, :]
        for ci in range(1, tm // CHUNK):
            dbf += dz[ci * CHUNK:(ci + 1) * CHUNK, :]
        dbf_ref[...] += dbf
        for ci in range(tm // CHUNK):
            for p in range(nb):
                dzb = dz[ci * CHUNK:(ci + 1) * CHUNK, p * LANES:(p + 1) * LANES].astype(MXU_DTYPE)
                dws_ref[p] += _dot_nt(dzb, vcats[ci, p])
                dvc = jnp.dot(wst_ref[p], dzb, preferred_element_type=F32)
                dvh_scr[ci * CHUNK:(ci + 1) * CHUNK, p * LANES:(p + 1) * LANES] = jnp.where(lane < hd, dvc[0:CHUNK], dvc[CHUNK:2 * CHUNK])
        dvh = dvh_scr[...]
        dvn = dvh * gv_ref[...]
        dvg = rstd * (dvn - _seg_mean(dvn, seg_ref, hd) - vhat * _seg_mean(dvn * vhat, seg_ref, hd))
        duv_ref[1] = (dvg * _gelu_grad(v_ref[...])).astype(ACT_DTYPE)
        acc_ref[0:1, :] += dgl
        acc_ref[1:2, :] += dgg
        acc_ref[2:3, :] += _rowsum(dvh * vhat)

    rows, vec, segs, wsps, bfulls = _gmlp_specs(tm, W, nb)
    wspt_spec = pl.BlockSpec((nb, 2 * CHUNK, CHUNK), lambda i: (0, 0, 0))
    return pl.pallas_call(
        body, name="gmlp_bwd", grid=(S // tm,),
        in_specs=[rows(2), rows(3), rows(0), rows(0), rows(0), rows(1), vec, segs, wsps, wspt_spec, bfulls, vec, vec],
        out_specs=[rows(0), pl.BlockSpec((2, tm, W), lambda i: (0, i, 0)), wsps, bfulls, pl.BlockSpec((SUBLANES, W), lambda i: (0, 0))],
        out_shape=[SDS((S, W), F32), SDS((2, S, W), ACT_DTYPE), SDS((nb, CHUNK, 2 * CHUNK), F32), SDS((CHUNK, W), F32), SDS((SUBLANES, W), F32)],
        scratch_shapes=[pltpu.VMEM((tm, W), F32), pltpu.VMEM((tm, W), F32)],
        compiler_params=_params("arbitrary"),
    )(proj, proj, ylru, ygm, dyn, dyn, gv, seg, wsp, wspt, bfull, g_lru, g_gm)


def _ada_fwd(c_all, w_ada, b_shard):
    L, D, N = w_ada.shape
    R = c_all.shape[0]
    tn = N // 2

    def body(c_ref, w_ref, b_ref, o_ref):
        cv = c_ref[...]
        o_ref[...] = _dot(cv * _sigmoid(cv), w_ref[...]) + b_ref[...]

    return pl.pallas_call(
        body, name="ada_fwd", grid=(L, N // tn),
        in_specs=[pl.BlockSpec((R, D), lambda l, j: (0, 0)), pl.BlockSpec((None, D, tn), lambda l, j: (l, 0, j)),
                  pl.BlockSpec((None, 1, tn), lambda l, j: (l, 0, j))],
        out_specs=pl.BlockSpec((None, R, tn), lambda l, j: (l, 0, j)),
        out_shape=SDS((L, R, N), F32), compiler_params=_params("parallel", "parallel"),
    )(c_all, w_ada, b_shard)


def _ada_step(c_all_t, dmod, w, m, v):
    D, B = c_all_t.shape
    L, _, N = dmod.shape
    tn = 3 * LANES
    assert N % tn == 0, N

    def body(c_ref, d_ref, w_ref, m_ref, v_ref, g_ref, do_ref, mo_ref, vo_ref):
        cv = c_ref[...]
        sc = cv * _sigmoid(cv)
        g = sc[:, 0:1] * d_ref[0:1, :]
        for b in range(1, B):
            g += sc[:, b:b + 1] * d_ref[b:b + 1, :]
        g_ref[...] = g
        do_ref[...], mo_ref[...], vo_ref[...] = _adam_math(w_ref[...], g, m_ref[...], v_ref[...])

    tile = pl.BlockSpec((None, D, tn), lambda l, j: (l, 0, j))
    return pl.pallas_call(
        body, name="ada_step", grid=(L, N // tn),
        in_specs=[pl.BlockSpec((D, B), lambda l, j: (0, 0)), pl.BlockSpec((None, B, tn), lambda l, j: (l, 0, j)), tile, tile, tile],
        out_specs=[tile] * 4, out_shape=[SDS((L, D, N), F32)] * 4, compiler_params=_params("parallel", "parallel"),
    )(c_all_t, dmod, w, m, v)


def _adam_math(w, g, m, v):
    mn = ADAM_B1 * m + (1.0 - ADAM_B1) * g
    vn = ADAM_B2 * v + (1.0 - ADAM_B2) * (g * g)
    m_hat = mn / (1.0 - ADAM_B1 ** ADAM_STEP)
    v_hat = vn / (1.0 - ADAM_B2 ** ADAM_STEP)
    return -ADAM_LR * (m_hat / (jnp.sqrt(v_hat) + ADAM_EPS) + ADAM_WD * w), mn, vn


def _adamw_layer(w, g, m, v, l, prev, after):
    L, R, C = w.shape
    tr = _row_tile(R, C * 4)
    prev = (after,) + (() if prev is None else tuple(prev))

    def body(w_ref, g_ref, m_ref, v_ref, *rest):
        go_ref, d_ref, mo_ref, vo_ref = rest[len(prev):]
        gv = g_ref[...]
        go_ref[...] = gv
        d_ref[...], mo_ref[...], vo_ref[...] = _adam_math(w_ref[...], gv, m_ref[...], v_ref[...])

    lay = pl.BlockSpec((None, tr, C), lambda i: (l, i, 0))
    return pl.pallas_call(
        body, name=f"adamw_layer_r{R}c{C}", grid=(R // tr,),
        in_specs=[lay, pl.BlockSpec((tr, C), lambda i: (i, 0)), lay, lay] + [ANY] * len(prev), out_specs=[lay] * 4,
        out_shape=[SDS((L, R, C), F32)] * 4, input_output_aliases={5 + k: k for k in range(len(prev) - 1)},
        compiler_params=_params("parallel"),
    )(w, g, m, v, *prev)


def _adamw_small(ws, gs, ms, vs):
    n = len(ws)

    def body(*refs):
        outs = refs[4 * n:]
        for k in range(n):
            outs[k][...], outs[n + k][...], outs[2 * n + k][...] = _adam_math(
                refs[k][...], refs[n + k][...], refs[2 * n + k][...], refs[3 * n + k][...])

    whole = pl.BlockSpec(memory_space=pltpu.VMEM)
    return pl.pallas_call(
        body, name="adamw_small", in_specs=[whole] * (4 * n), out_specs=[whole] * (3 * n),
        out_shape=[SDS(w.shape, F32) for w in ws] * 3, compiler_params=pltpu.CompilerParams(vmem_limit_bytes=VMEM_LIMIT_BYTES),
    )(*ws, *gs, *ms, *vs)


def _sum_leading(a):
    P, R, C = a.shape
    tr = _row_tile(R, P * C * 4)

    def body(a_ref, o_ref):
        acc = a_ref[0]
        for p in range(1, P):
            acc = acc + a_ref[p]
        o_ref[...] = acc

    return pl.pallas_call(
        body, name=f"sum{P}_r{R}c{C}", grid=(R // tr,),
        in_specs=[pl.BlockSpec((P, tr, C), lambda i: (0, i, 0))],
        out_specs=pl.BlockSpec((tr, C), lambda i: (i, 0)),
        out_shape=SDS((R, C), F32), compiler_params=_params("parallel"),
    )(a)


def _add_half(g4, r1, place):
    _, _, R, C = g4.shape
    tr = _row_tile(R, C * 4)

    def body(place_ref, g_ref, r_ref, h_ref, own_ref):
        s = (g_ref[...] + r_ref[...]).astype(XFER_DTYPE)
        h_ref[...] = s

        @pl.when(pl.program_id(1) == place_ref[1])
        def _():
            own_ref[...] = s

    return pl.pallas_call(
        body, name=f"add_half_r{R}c{C}",
        grid_spec=pltpu.PrefetchScalarGridSpec(
            num_scalar_prefetch=1, grid=(R // tr, N_CHIPS),
            in_specs=[pl.BlockSpec((None, None, tr, C), lambda i, p, place_ref: (p, place_ref[0], i, 0)),
                      pl.BlockSpec((None, tr, C), lambda i, p, place_ref: (p, i, 0))],
            out_specs=[pl.BlockSpec((None, tr, C), lambda i, p, place_ref: (p, i, 0)),
                       pl.BlockSpec((None, tr, C), lambda i, p, place_ref: (place_ref[1], i, 0))],
        ),
        out_shape=[SDS((N_CHIPS, R, C), XFER_DTYPE)] * 2, compiler_params=_params("parallel", "arbitrary"),
    )(place, g4, r1)


def _sum4_into_half(r2, place):
    P, R, C = r2.shape
    tr = _row_tile(R, P * C * 4)

    def body(place_ref, a_ref, o_ref):
        acc = a_ref[0].astype(F32)
        for p in range(1, P):
            acc = acc + a_ref[p].astype(F32)
        o_ref[...] = acc

    return pl.pallas_call(
        body, name=f"sum4_r{R}c{C}",
        grid_spec=pltpu.PrefetchScalarGridSpec(
            num_scalar_prefetch=1, grid=(R // tr,),
            in_specs=[pl.BlockSpec((P, tr, C), lambda i, place_ref: (0, i, 0))],
            out_specs=pl.BlockSpec((None, tr, C), lambda i, place_ref: (place_ref[0], i, 0)),
        ),
        out_shape=SDS((2, R, C), F32), compiler_params=_params("parallel"),
    )(place, r2)


def _cast_into_slot(w, l, place, after):
    _, R, C = w.shape
    tr = _row_tile(R, C * 4)

    def body(place_ref, w_ref, after_ref, o_ref):
        o_ref[...] = w_ref[...].astype(MXU_DTYPE)

    return pl.pallas_call(
        body, name=f"cast_r{R}c{C}",
        grid_spec=pltpu.PrefetchScalarGridSpec(
            num_scalar_prefetch=1, grid=(R // tr,),
            in_specs=[pl.BlockSpec((None, tr, C), lambda i, place_ref: (l, i, 0)), ANY],
            out_specs=pl.BlockSpec((None, tr, C), lambda i, place_ref: (place_ref[1], i, 0)),
        ),
        out_shape=SDS((N_CHIPS, R, C), MXU_DTYPE), compiler_params=_params("parallel"),
    )(place, w, after)


def _place():
    x, y, c = lax.axis_index("x"), lax.axis_index("y"), lax.axis_index("c")
    chips = [(1 - x, y), (x, 1 - y), (1 - x, 1 - y)]
    return x, y, c, chips


def _remote(src, dst, send_sem, recv_sem, to):
    return pltpu.make_async_remote_copy(src_ref=src, dst_ref=dst, send_sem=send_sem, recv_sem=recv_sem, device_id=to, device_id_type=MESH)


HBM_SPEC = pl.BlockSpec(memory_space=pltpu.HBM)
SEM_SPEC = pl.BlockSpec(memory_space=pltpu.SEMAPHORE)
DATAFLOW = pltpu.SideEffectType.DATAFLOW_SIDE_EFFECTING


def _in_hbm(a):
    return pltpu.with_memory_space_constraint(a, pltpu.HBM)


def _hbm_like(a):
    return pltpu.HBM(a.shape, a.dtype)


def _split_start(name, arrays, n_sems, issue, extra=()):
    m = len(arrays)

    def body(*refs):
        issue(refs[:m], refs[m + len(extra)], refs[m + len(extra) + 1])

    out = pl.pallas_call(
        body, name=name,
        out_shape=(pltpu.SemaphoreType.DMA((n_sems,)), pltpu.SemaphoreType.DMA((n_sems,)), *[_hbm_like(a) for a in arrays]),
        in_specs=[HBM_SPEC] * m + [ANY] * len(extra), out_specs=(SEM_SPEC, SEM_SPEC, *[HBM_SPEC] * m),
        input_output_aliases={k: 2 + k for k in range(m)},
        compiler_params=pltpu.CompilerParams(has_side_effects=DATAFLOW),
    )(*[_in_hbm(a) for a in arrays], *extra)
    return out[0], out[1], list(out[2:])


def _split_wait(name, send_sems, recv_sems, arrays, after, drain):
    m = len(arrays)

    def body(*refs):
        drain(refs[:m], refs[m], refs[m + 1])

    out = pl.pallas_call(
        body, name=name, out_shape=[_hbm_like(a) for a in arrays],
        in_specs=[HBM_SPEC] * m + [SEM_SPEC, SEM_SPEC, ANY], out_specs=[HBM_SPEC] * m,
        input_output_aliases={k: k for k in range(m)},
        compiler_params=pltpu.CompilerParams(has_side_effects=DATAFLOW),
    )(*arrays, send_sems, recv_sems, after)
    return list(out)


def _wait_both(cp):
    cp.wait_send()
    cp.wait_recv()


class _Flight:
    def __init__(self, name, arrays, n_sems, issue, drain, thru, extra=()):
        self.name, self.drain, self.n = name, drain, len(arrays)
        self.send, self.recv, out = _split_start(name + "_start", [*arrays, thru], n_sems, issue, extra)
        self.arrays, self.thru = out[:-1], out[-1]

    def land(self, after):
        return _split_wait(self.name + "_wait", self.send, self.recv, self.arrays, after, self.drain)


def _gather_flight(tag, ici, d2d, direct, thru):
    kinds = ["ici"] * len(ici) + ["d2d"] * len(d2d) + ["direct"] * len(direct)
    rows = [pl.ds(part * (a.shape[2] // parts), a.shape[2] // parts) for a, part, parts in ici]
    ici = [a for a, _, _ in ici]

    def issue(refs, send_sems, recv_sems):
        x, y, c, chips = _place()
        q = 2 * x + y
        for k, kind in enumerate(kinds):
            for j, chip in enumerate(chips):
                if kind == "ici":
                    src, to = refs[k].at[q, c, rows[k]], (*chip, c)
                elif kind == "d2d":
                    src, to = refs[k].at[2 * chip[0] + chip[1], c], (x, y, 1 - c)
                else:
                    src, to = refs[k].at[q], (*chip, c)
                _remote(src, src, send_sems.at[3 * k + j], recv_sems.at[3 * k + j], to).start()

    def drain(refs, send_sems, recv_sems):
        x, y, c, chips = _place()
        for k, kind in enumerate(kinds):
            for j, chip in enumerate(chips):
                p = 2 * chip[0] + chip[1]
                got = refs[k].at[p] if kind == "direct" else refs[k].at[p, c, rows[k]] if kind == "ici" else refs[k].at[p, 1 - c]
                _wait_both(_remote(got, got, send_sems.at[3 * k + j], recv_sems.at[3 * k + j], (x, y, c)))

    return _Flight(f"gather{tag}", [*ici, *d2d, *direct], 3 * len(kinds), issue, drain, thru)


def _swap_flight(tag, g4s, thru):
    n = len(g4s)
    zones = [lax.empty((N_CHIPS,) + g.shape[2:], g.dtype) for g in g4s]

    def issue(refs, send_sems, recv_sems):
        x, y, c, _ = _place()
        for k in range(n):
            for p in range(N_CHIPS):
                _remote(refs[k].at[p, 1 - c], refs[n + k].at[p], send_sems.at[N_CHIPS * k + p], recv_sems.at[N_CHIPS * k + p], (x, y, 1 - c)).start()

    def drain(refs, send_sems, recv_sems):
        x, y, c, _ = _place()
        for k in range(n):
            for p in range(N_CHIPS):
                got = refs[n + k].at[p]
                _wait_both(_remote(got, got, send_sems.at[N_CHIPS * k + p], recv_sems.at[N_CHIPS * k + p], (x, y, c)))

    return _Flight(f"swap{tag}", [*g4s, *zones], N_CHIPS * n, issue, drain, thru)


def _scatter_flight(tag, hs, lands, thru):
    n = len(hs)

    def issue(refs, send_sems, recv_sems):
        x, y, c, chips = _place()
        q = 2 * x + y
        for k in range(n):
            for j, chip in enumerate(chips):
                _remote(refs[k].at[2 * chip[0] + chip[1]], refs[n + k].at[q], send_sems.at[3 * k + j], recv_sems.at[3 * k + j], (*chip, c)).start()

    def drain(refs, send_sems, recv_sems):
        x, y, c, chips = _place()
        for k in range(n):
            for j, chip in enumerate(chips):
                got = refs[n + k].at[2 * chip[0] + chip[1]]
                _wait_both(_remote(got, got, send_sems.at[3 * k + j], recv_sems.at[3 * k + j], (x, y, c)))

    return _Flight(f"scatter{tag}", [*hs, *lands], 3 * n, issue, drain, thru)


def _exchange_flight(tag, buf, thru):
    flips = [(fx, fy, fc) for fx in (0, 1) for fy in (0, 1) for fc in (0, 1)][1:]

    def peers():
        x, y, c, _ = _place()
        return (x, y, c), [((1 - x) if fx else x, (1 - y) if fy else y, (1 - c) if fc else c) for fx, fy, fc in flips]

    def slot(ref, dev):
        return ref.at[4 * dev[0] + 2 * dev[1] + dev[2]]

    def issue(refs, send_sems, recv_sems):
        me, others = peers()
        for j, to in enumerate(others):
            _remote(slot(refs[0], me), slot(refs[0], me), send_sems.at[j], recv_sems.at[j], to).start()

    def drain(refs, send_sems, recv_sems):
        me, others = peers()
        for j, frm in enumerate(others):
            got = slot(refs[0], frm)
            _wait_both(_remote(got, got, send_sems.at[j], recv_sems.at[j], me))

    return _Flight(f"exchange{tag}", [buf], len(flips), issue, drain, thru)


def _share_flight(tag, fins, thru):
    n = len(fins)

    def issue(refs, send_sems, recv_sems):
        x, y, c, _ = _place()
        for k in range(n):
            _remote(refs[k].at[c], refs[k].at[c], send_sems.at[k], recv_sems.at[k], (x, y, 1 - c)).start()

    def drain(refs, send_sems, recv_sems):
        x, y, c, _ = _place()
        for k in range(n):
            got = refs[k].at[1 - c]
            _wait_both(_remote(got, got, send_sems.at[k], recv_sems.at[k], (x, y, c)))

    return _Flight(f"share{tag}", fins, n, issue, drain, thru)


def _pair_blocks(w):
    h, d, _ = w.shape
    z = jnp.zeros((h // 2, d, d), w.dtype)
    return jnp.concatenate([jnp.concatenate([w[0::2], z], axis=2), jnp.concatenate([z, w[1::2]], axis=2)], axis=1)


def _unpair_blocks(b):
    n, dd, _ = b.shape
    d = dd // 2
    return jnp.stack([b[:, :d, :d], b[:, d:, d:]], axis=1).reshape(2 * n, d, d)


def _pad_rows(a, rows):
    return jnp.pad(a, ((0, rows - a.shape[0]), (0, 0)))


class _Packer:
    def __init__(self, shapes, width=1024, row_multiple=64):
        self.shapes = shapes
        self.sizes = [math.prod(s) for s in shapes]
        total = sum(self.sizes)
        self.width = width
        self.rows = -(-total // (width * row_multiple)) * row_multiple
        self.pad = self.rows * width - total

    def pack(self, arrays):
        flat = jnp.concatenate([a.reshape(-1).astype(F32) for a in arrays] + [jnp.zeros((self.pad,), F32)])
        return flat.reshape(self.rows, self.width)

    def unpack(self, packed):
        flat = packed.reshape(-1)
        out, off = [], 0
        for s, n in zip(self.shapes, self.sizes):
            out.append(flat[off:off + n].reshape(s))
            off += n
        return out


SMALL = ["b_ada", "ffn1_norm", "mix_norm", "conv_w", "conv_b", "gate_a_w", "gate_a_b", "gate_x_w", "gate_x_b", "lru_lambda",
         "v_norm", "spatial_w", "spatial_b", "lru_out_norm", "gmlp_out_norm", "ffn2_norm", "final_norm"]
BIG = ["ffn1_w_gu", "ffn1_w_down", "w_in", "w_out", "ffn2_w_gu", "ffn2_w_down"]
GROUPS = (("ffn1_w_gu", "ffn1_w_down"), ("w_in", "w_out"), ("ffn2_w_gu", "ffn2_w_down"))
FWD_GROUPS = (("ffn1_w_gu",), ("ffn1_w_down",), ("w_in", "w_out"), ("ffn2_w_gu",), ("ffn2_w_down",))
MIN_AGE = {"swap": 1, "scatter": 1, "share": 1}
WEIGHTS = ["w_ada", "b_ada", "ffn1_norm", "ffn1_w_gu", "ffn1_w_down", "mix_norm", "w_in", "conv_w", "conv_b", "gate_a_w", "gate_a_b",
           "gate_x_w", "gate_x_b", "lru_lambda", "v_norm", "spatial_w", "spatial_b", "lru_out_norm", "gmlp_out_norm", "w_out",
           "ffn2_norm", "ffn2_w_gu", "ffn2_w_down", "final_norm"]


def kernel(x, c, w_ada, b_ada, ffn1_norm, ffn1_w_gu, ffn1_w_down, mix_norm, w_in, conv_w, conv_b, gate_a_w, gate_a_b, gate_x_w, gate_x_b, lru_lambda, v_norm, spatial_w, spatial_b, lru_out_norm, gmlp_out_norm, w_out, ffn2_norm, ffn2_w_gu, ffn2_w_down, final_norm, loss_target, m_w_ada, m_b_ada, m_ffn1_norm, m_ffn1_w_gu, m_ffn1_w_down, m_mix_norm, m_w_in, m_conv_w, m_conv_b, m_gate_a_w, m_gate_a_b, m_gate_x_w, m_gate_x_b, m_lru_lambda, m_v_norm, m_spatial_w, m_spatial_b, m_lru_out_norm, m_gmlp_out_norm, m_w_out, m_ffn2_norm, m_ffn2_w_gu, m_ffn2_w_down, m_final_norm, v_w_ada, v_b_ada, v_ffn1_norm, v_ffn1_w_gu, v_ffn1_w_down, v_mix_norm, v_w_in, v_conv_w, v_conv_b, v_gate_a_w, v_gate_a_b, v_gate_x_w, v_gate_x_b, v_lru_lambda, v_v_norm, v_spatial_w, v_spatial_b, v_lru_out_norm, v_gmlp_out_norm, v_w_out, v_ffn2_norm, v_ffn2_w_gu, v_ffn2_w_down, v_final_norm):
    given = dict(locals())
    W = {n: given[n] for n in WEIGHTS}
    L = w_ada.shape[0]
    S, D = x.shape[1], x.shape[2]
    LW = conv_b.shape[1]
    hd = LW // HEADS
    xi, yi, ci = lax.axis_index("x"), lax.axis_index("y"), lax.axis_index("c")
    chip = 2 * xi + yi
    dev = 2 * chip + ci
    place = jnp.stack([ci, chip]).astype(jnp.int32)
    xs = x.reshape(S, D)
    tgt = loss_target.reshape(S, D)

    n_ada = w_ada.shape[2]
    cws = LW // N_CHIPS

    def half_view(s):
        return s.reshape(N_CHIPS, 2, s.shape[1] // 2, s.shape[2])

    stages = [(l, names) for l in range(L) for names in FWD_GROUPS]
    seq = [[half_view(_cast_into_slot(W[n], l, place, place)) for n in names] for l, names in stages[:1]]
    flights = {}

    def ici_plan(t):
        if t >= len(stages):
            return []
        stage = t % len(FWD_GROUPS)
        return {2: [(t, 0, 1), (t + 1, 0, 2)], 3: [(t, 1, 2)]}.get(stage, [(t, 0, 1)])

    def launch(t, thru, direct=()):
        ici = [(a, part, parts) for g, part, parts in ici_plan(t) for a in seq[g]]
        d2d = seq[t - 1] if 1 <= t <= len(seq) else []
        if ici or d2d or direct:
            flights[t] = _gather_flight(t, ici, d2d, list(direct), thru)
            thru = flights[t].thru
        return thru

    def land(t, after):
        if t not in flights:
            return []
        out = flights.pop(t).land(after)
        for g, _, _ in ici_plan(t):
            seq[g], out = out[:len(seq[g])], out[len(seq[g]):]
        if 1 <= t <= len(seq):
            seq[t - 1], out = out[:len(seq[t - 1])], out[len(seq[t - 1]):]
        return out

    def group_weights(t):
        return [s.reshape(N_CHIPS, -1, s.shape[3]) for s in seq[t]]

    c_rows = _pad_rows(c, SUBLANES)
    c_flight = _exchange_flight("c", lax.dynamic_update_index_in_dim(jnp.zeros((N_DEV,) + c_rows.shape, F32), c_rows, dev, 0), c_rows)
    started = launch(0, c_flight.thru)
    c_all = c_flight.land(started)[0][:, 0, :]
    b_shard = lax.dynamic_slice_in_dim(b_ada, chip * n_ada, n_ada, axis=1)
    mod_shard = _ada_fwd(_pad_rows(c_all, 2 * SUBLANES), w_ada, b_shard[:, None, :])
    seq += [[half_view(_cast_into_slot(W[n], l, place, started)) for n in names] for l, names in stages[1:]]

    def in_slot(block):
        return lax.dynamic_update_index_in_dim(jnp.zeros((N_CHIPS,) + block.shape, block.dtype), block, chip, 0)

    land(0, seq[-1][-1])
    small = [in_slot(mod_shard.reshape(L * 2 * SUBLANES, n_ada)), in_slot(conv_w.reshape(L * CONV_WIDTH, cws))]
    mod_all, conv_all = land(1, launch(1, mod_shard, small))
    mod_rows = lax.dynamic_index_in_dim(mod_all.reshape(N_CHIPS, L, 2 * SUBLANES, n_ada), dev, axis=2, keepdims=False)
    mod = mod_rows.transpose(1, 0, 2).reshape(L, N_MOD, 1, D)
    conv_full = conv_all.reshape(N_CHIPS, L, CONV_WIDTH, cws).transpose(1, 2, 0, 3).reshape(L, CONV_WIDTH, LW)

    tril = jnp.tril(jnp.ones((CHUNK, CHUNK), F32))
    seg = (jnp.arange(LW)[:, None] // hd == jnp.arange(LW)[None, :] // hd).astype(jnp.bfloat16)

    def mixer_params(l):
        ws = spatial_w[l] * tril
        wsp = jnp.concatenate([ws[0::2], ws[1::2]], axis=2)
        wa, wx = _pair_blocks(gate_a_w[l]), _pair_blocks(gate_x_w[l])
        return dict(
            cw=conv_full[l], cb=conv_b[l][None],
            wa=wa.astype(MXU_DTYPE), wx=wx.astype(MXU_DTYPE), wat=wa.transpose(0, 2, 1).astype(MXU_DTYPE), wxt=wx.transpose(0, 2, 1).astype(MXU_DTYPE),
            ba=gate_a_b[l].reshape(1, LW), bx=gate_x_b[l].reshape(1, LW), lam=lru_lambda[l][None], gv=v_norm[l][None],
            wsp=wsp.astype(MXU_DTYPE), wspt=wsp.transpose(0, 2, 1).astype(MXU_DTYPE),
            bfull=jnp.repeat(spatial_b[l].T, hd, axis=1), g_lru=lru_out_norm[l][None], g_gm=gmlp_out_norm[l][None])

    saved = []
    xcur = xs
    zero_row = jnp.zeros((1, D), F32)
    h = _modnorm(xcur, ffn1_norm[0][None], mod[0][0], mod[0][1])
    for l in range(L):
        mp, md = mixer_params(l), mod[l]
        s = dict(lw={}, mp=mp, md=md)
        lw = s["lw"]
        t = len(FWD_GROUPS) * l
        s["x0"] = xcur
        s["h1"] = launch(t + 2, h)
        lw["gu1"], = group_weights(t)
        s["a1"], s["gu1"] = _ffn_up(s["h1"], lw["gu1"])
        land(t + 2, s["a1"])
        s["a1"] = launch(t + 3, s["a1"])
        lw["d1"] = group_weights(t + 1)[0].reshape(-1, D)
        s["f1"], xcur, h = _mm_res(s["a1"], lw["d1"], xcur, md[2], 0.5, (mix_norm[l][None], md[3], md[4]))
        land(t + 3, xcur)
        s["x1"] = xcur
        s["h2"] = launch(t + 4, h)
        lw["win"], wout = group_weights(t + 2)
        lw["wout"] = wout.reshape(-1, D)
        s["proj"] = _mm_chunks(s["h2"], lw["win"])
        s["ylru"] = _lru_fwd(s["proj"], mp["cw"], mp["cb"], mp["wa"], mp["ba"], mp["wx"], mp["bx"], mp["lam"])
        s["yn"], s["ygm"] = _gmlp_fwd(s["proj"], s["ylru"], mp["gv"], seg, mp["wsp"], mp["bfull"], mp["g_lru"], mp["g_gm"])
        s["f2"], xcur, h = _mm_res(s["yn"], lw["wout"], xcur, md[5], 1.0, (ffn2_norm[l][None], md[6], md[7]))
        land(t + 4, xcur)
        s["x2"] = xcur
        s["h3"] = launch(t + 5, h)
        lw["gu2"], = group_weights(t + 3)
        s["a3"], s["gu3"] = _ffn_up(s["h3"], lw["gu2"])
        land(t + 5, s["a3"])
        s["a3"] = launch(t + 6, s["a3"])
        lw["d2"] = group_weights(t + 4)[0].reshape(-1, D)
        following = (ffn1_norm[l + 1][None], mod[l + 1][0], mod[l + 1][1]) if l + 1 < L else (final_norm[None], zero_row, zero_row)
        s["f3"], xcur, h = _mm_res(s["a3"], lw["d2"], xcur, md[8], 0.5, following)
        land(t + 6, xcur)
        saved.append(s)

    dx, dq, head_acc = _loss_head(xcur, tgt, final_norm[None], saved[-1]["md"][8], 0.5)
    loss = lax.psum(jnp.sum(head_acc[1]), ("x", "y", "c"))
    big_grads = {n: [None] * L for n in BIG}
    dmods = [None] * L

    def ffn_bwd(names, l, dx, dq, x_in, h, a, gu, f, wgu, wd, gn, sc, next_gate, next_scale):
        big_grads[names[1]][l] = _mm_tn_chunks(a, dq[None], 1408, 1024)[0].reshape(N_CHIPS, -1, D)
        dgu = _ffn_bwd_act(dq, wd, gu)
        C = dgu.shape[3]
        dgu4 = dgu.reshape(N_CHIPS, S, C)
        big_grads[names[0]][l] = _mm_tn_chunks(h, dgu4, 1024, C)
        dgu4 = reduce_group(names, l, big_grads[names[0]][l], dgu4)
        dx, dq, acc = _mm_nt_norm_bwd(dgu4, wgu, x_in, dx, f, gn, sc, 0.5, next_gate, next_scale)
        return dx, move_on(dx, dq), acc

    stepped = {n: None for n in BIG}
    reducing = []

    per_layer = [n for n in SMALL if n not in ("b_ada", "final_norm")]
    packers, exchanges = {}, {}
    clock = [0]
    to_step = []

    def step_reduced(after):
        while to_step:
            name, l, g = to_step.pop(0)
            stepped[name] = _adamw_layer(W[name], g, given["m_" + name], given["v_" + name], l, stepped[name], after)
            after = stepped[name][1]
        return after

    def move_on(after, thru, force=False):
        clock[0] += 1
        for grp in list(reducing):
            if not force and clock[0] - grp["since"] < MIN_AGE[grp["step"]]:
                continue
            grp["since"] = clock[0]
            landed = grp["flight"].land(after)
            n = len(grp["names"])
            if grp["step"] == "swap":
                pairs = [_add_half(g4, r1, place) for g4, r1 in zip(landed[:n], landed[n:])]
                grp.update(step="scatter", flight=_scatter_flight(grp["tag"], [h for h, _ in pairs], [own for _, own in pairs], thru))
            elif grp["step"] == "scatter":
                grp.update(step="share", flight=_share_flight(grp["tag"], [_sum4_into_half(r2, place) for r2 in landed[n:]], thru))
            else:
                to_step.extend((name, grp["l"], fin.reshape(2 * fin.shape[1], fin.shape[2])) for name, fin in zip(grp["names"], landed))
                reducing.remove(grp)
                continue
            thru = grp["flight"].thru
        return thru

    def reduce_group(names, l, after, thru):
        thru = move_on(after, thru)
        g4s = [big_grads[n][l].reshape(N_CHIPS, 2, big_grads[n][l].shape[1] // 2, big_grads[n][l].shape[2]) for n in names]
        tag = f"{l}{GROUPS.index(names)}"
        reducing.append(dict(names=names, l=l, tag=tag, step="swap", since=clock[0], flight=_swap_flight(tag, g4s, thru)))
        return reducing[-1]["flight"].thru
    for l in reversed(range(L)):
        s = saved[l]
        lw, mp, md = s["lw"], s["mp"], s["md"]
        dx, dq, acc3 = ffn_bwd(
            GROUPS[2], l, dx, dq, s["x2"], s["h3"], s["a3"], s["gu3"], s["f3"], lw["gu2"], lw["d2"], ffn2_norm[l][None], md[7], md[5], 1.0)
        big_grads["w_out"][l] = _mm_tn_chunks(s["yn"], dq[None], 1024, 1024)[0].reshape(N_CHIPS, -1, D)
        dyn = _mm_nt_chunks(dq[None], lw["wout"][None])
        dylru, duv, dwsp, dbfull, gacc = _gmlp_bwd(s["proj"], s["ylru"], s["ygm"], dyn, mp["gv"], seg, mp["wsp"], mp["wspt"], mp["bfull"], mp["g_lru"], mp["g_gm"])
        dxg, dwa, dwx, lvec = _lru_bwd(s["proj"], dylru, mp["cw"], mp["cb"], mp["wa"], mp["ba"], mp["wx"], mp["bx"], mp["lam"], mp["wat"], mp["wxt"])
        dproj = jnp.concatenate([dxg, duv], axis=0)
        big_grads["w_in"][l] = _mm_tn_together(s["h2"], dproj)
        dproj = reduce_group(GROUPS[1], l, big_grads["w_in"][l], dproj)
        dx, dq, acc2 = _mm_nt_norm_bwd(dproj, lw["win"], s["x1"], dx, s["f2"], mix_norm[l][None], md[4], 1.0, md[2], 0.5)
        dq = move_on(dx, dq)
        if l > 0:
            ng, ns = saved[l - 1]["md"][8], 0.5
        else:
            ng, ns = zero_row, 0.0
        dx, dq, acc1 = ffn_bwd(
            GROUPS[0], l, dx, dq, s["x0"], s["h1"], s["a1"], s["gu1"], s["f1"], lw["gu1"], lw["d1"], ffn1_norm[l][None], md[1], ng, ns)

        dmods[l] = jnp.concatenate([acc1[0:2], acc1[3:4], acc2[0:2], acc2[3:4], acc3[0:2], acc3[3:4]], axis=0)
        dws = jnp.stack([dwsp[:, :, :CHUNK], dwsp[:, :, CHUNK:]], axis=1).reshape(HEADS, CHUNK, CHUNK) * tril
        lg = {"ffn1_norm": acc1[2], "mix_norm": acc2[2], "ffn2_norm": acc3[2],
              "conv_w": lvec[4:8], "conv_b": lvec[3], "gate_a_w": _unpair_blocks(dwa), "gate_a_b": lvec[0].reshape(HEADS, hd),
              "gate_x_w": _unpair_blocks(dwx), "gate_x_b": lvec[1].reshape(HEADS, hd), "lru_lambda": lvec[2], "v_norm": gacc[2],
              "spatial_w": dws, "spatial_b": dbfull.reshape(CHUNK, HEADS, hd).sum(-1).T, "lru_out_norm": gacc[0], "gmlp_out_norm": gacc[1]}
        part = [lg[n] for n in per_layer] + [dmods[l]] + ([head_acc[0]] if l == L - 1 else [])
        packers[l] = _Packer([p.shape for p in part])
        packed = packers[l].pack(part)
        exchanges[l] = _exchange_flight(l, lax.dynamic_update_index_in_dim(jnp.zeros((N_DEV,) + packed.shape, F32), packed, dev, 0), dq)
        dq = exchanges[l].thru

    grad_x = dx.reshape(x.shape)

    done = step_reduced(dq)
    while reducing:
        dq = move_on(done, dq, force=True)
        done = step_reduced(dq)
    summed, dmod_rows = [], []
    for l in range(L):
        gathered, = exchanges[l].land(done)
        summed.append(packers[l].unpack(_sum_leading(gathered)))
        off = sum(packers[l].sizes[:len(per_layer)])
        dmod_rows.append(gathered.reshape(N_DEV, -1)[:, off:off + N_MOD * D])
    grads = {n: jnp.stack([summed[l][k] for l in range(L)]) for k, n in enumerate(per_layer)}
    grads["final_norm"] = summed[L - 1][len(per_layer) + 1]
    grads["b_ada"] = jnp.stack([summed[l][len(per_layer)].reshape(N_MOD * D) for l in range(L)])
    dmod_shard = lax.dynamic_slice_in_dim(jnp.stack(dmod_rows), chip * n_ada, n_ada, axis=2)
    stepped_ada = _ada_step(c_all.T, dmod_shard, w_ada, m_w_ada, v_w_ada)
    grads["w_ada"] = stepped_ada[0]
    grads["conv_w"] = lax.dynamic_slice_in_dim(grads["conv_w"], chip * cws, cws, axis=2)

    delta, new_m, new_v = {}, {}, {}
    for n in BIG:
        grads[n], delta[n], new_m[n], new_v[n] = stepped[n]
    delta["w_ada"], new_m["w_ada"], new_v["w_ada"] = stepped_ada[1:]
    def rows_of(a):
        return a.reshape(-1, a.shape[-1])

    stepped_small = _adamw_small(*[[rows_of(src[n].reshape(W[n].shape)) for n in SMALL]
                                   for src in (W, grads, {n: given["m_" + n] for n in SMALL}, {n: given["v_" + n] for n in SMALL})])
    for k, n in enumerate(SMALL):
        delta[n], new_m[n], new_v[n] = (stepped_small[i * len(SMALL) + k].reshape(W[n].shape) for i in range(3))
    grads = {n: grads[n].reshape(W[n].shape) for n in WEIGHTS}
    return (loss, grad_x, *[grads[n] for n in WEIGHTS], *[delta[n] for n in WEIGHTS], *[new_m[n] for n in WEIGHTS], *[new_v[n] for n in WEIGHTS])
```
